```python
import jax, jax.numpy as jnp
from jax import lax
import numpy as np

D_MODEL = 1024
BATCH = 8
SEQ = 4096
DEPTH = 4

CHUNK = 64
QBLOCK = 128
N_A_LAYERS = DEPTH // 2
N_B_LAYERS = DEPTH - N_A_LAYERS

GDN_HEAD_DIM = 128
GDN_HEADS = D_MODEL // GDN_HEAD_DIM
GDN_WIDTH = GDN_HEADS * GDN_HEAD_DIM
CONV_K = 4
GDN_PROJ = 4 * GDN_WIDTH + 2 * GDN_HEADS

MLA_HEADS = D_MODEL // 128
QK_NOPE = 128
QK_ROPE = 64
QK_HEAD = QK_NOPE + QK_ROPE
V_HEAD = 128
KV_LORA = D_MODEL // 4
Q_LORA = 3 * D_MODEL // 8
ROPE_BASE = 10000.0

D_FF = ((8 * D_MODEL // 3 + 127) // 128) * 128
N_MOD = 9
EPS = 1e-6
MAX_POS_OFFSET = 2048

kernel_name = "hybrid_gdn_mla_yoco_macaron_adaln"


def rms_norm(t, g):
    tf = t.astype(jnp.float32)
    y = tf * lax.rsqrt(jnp.mean(tf * tf, axis=-1, keepdims=True) + EPS)
    return (y * g.astype(jnp.float32)).astype(t.dtype)


def l2_norm(t):
    tf = t.astype(jnp.float32)
    return tf * lax.rsqrt(jnp.sum(tf * tf, axis=-1, keepdims=True) + EPS)


def modulate(t, g, shift, scale):
    return rms_norm(t, g) * (1 + scale[:, None, :]) + shift[:, None, :]


def swiglu(h, w_in, w_out):
    gate, up = jnp.split(h @ w_in, 2, axis=-1)
    return (jax.nn.silu(gate) * up) @ w_out


def rotary(t, positions):
    half = t.shape[-1] // 2
    inv_freq = ROPE_BASE ** (-jnp.arange(half, dtype=jnp.float32) / half)
    ang = positions.astype(jnp.float32)[..., None] * inv_freq
    cos = jnp.cos(ang)[:, :, None, :]
    sin = jnp.sin(ang)[:, :, None, :]
    tf = t.astype(jnp.float32)
    t1, t2 = tf[..., :half], tf[..., half:]
    return jnp.concatenate([t1 * cos - t2 * sin, t2 * cos + t1 * sin], axis=-1).astype(t.dtype)


def causal_depthwise_conv(t, w):
    k = w.shape[0]
    return lax.conv_general_dilated(
        t, w[:, None, :].astype(t.dtype), window_strides=(1,), padding=[(k - 1, 0)],
        dimension_numbers=("NWC", "WIO", "NWC"), feature_group_count=t.shape[-1])


def gated_delta_rule(q, k, v, g, beta):
    b_, s_, h_, dk = q.shape
    dv = v.shape[-1]
    nc = s_ // CHUNK
    q = q.astype(jnp.float32) * (dk ** -0.5)

    def to_chunks(t):
        return t.astype(jnp.float32).reshape(b_, nc, CHUNK, h_, -1).transpose(1, 0, 3, 2, 4)

    qc, kc, vc = to_chunks(q), to_chunks(k), to_chunks(v)
    gc = g.astype(jnp.float32).reshape(b_, nc, CHUNK, h_).transpose(1, 0, 3, 2)
    bc = beta.astype(jnp.float32).reshape(b_, nc, CHUNK, h_).transpose(1, 0, 3, 2)
    gcum = jnp.cumsum(gc, axis=-1)

    idx = jnp.arange(CHUNK)
    incl = idx[:, None] >= idx[None, :]
    strict = idx[:, None] > idx[None, :]
    diff = gcum[..., :, None] - gcum[..., None, :]
    decay = jnp.where(incl, jnp.exp(jnp.where(incl, diff, 0.0)), 0.0)

    kb = kc * bc[..., None]
    lower = jnp.where(strict, jnp.einsum("nbhid,nbhjd->nbhij", kb, kc) * decay, 0.0)
    eye = jnp.eye(CHUNK, dtype=jnp.float32)
    rhs = jnp.concatenate([vc * bc[..., None], kb * jnp.exp(gcum)[..., None]], axis=-1)
    sol = lax.linalg.triangular_solve(eye + lower, rhs, left_side=True, lower=True)
    u, w = sol[..., :dv], sol[..., dv:]
    attn_intra = jnp.where(incl, jnp.einsum("nbhid,nbhjd->nbhij", qc, kc) * decay, 0.0)

    def step(state, inp):
        qi, ki, ui, wi, gi, ai = inp
        v_new = ui - jnp.einsum("bhck,bhkv->bhcv", wi, state)
        o = jnp.einsum("bhck,bhkv->bhcv", qi * jnp.exp(gi)[..., None], state) \
            + jnp.einsum("bhcj,bhjv->bhcv", ai, v_new)
        g_last = gi[..., -1]
        k_dec = ki * jnp.exp(g_last[..., None] - gi)[..., None]
        state = state * jnp.exp(g_last)[..., None, None] + jnp.einsum("bhck,bhcv->bhkv", k_dec, v_new)
        return state, o

    state0 = jnp.zeros((b_, h_, dk, dv), jnp.float32)
    _, o = lax.scan(step, state0, (qc, kc, u, w, gcum, attn_intra))
    return o.transpose(1, 0, 3, 2, 4).reshape(b_, s_, h_, dv)


def gated_deltanet(h, w_in, conv_w, a_log, dt_bias, norm_g, w_out):
    b_, s_, _ = h.shape
    proj = h @ w_in
    qkv = proj[..., :3 * GDN_WIDTH]
    z = proj[..., 3 * GDN_WIDTH:4 * GDN_WIDTH]
    b_logit = proj[..., 4 * GDN_WIDTH:4 * GDN_WIDTH + GDN_HEADS]
    a_logit = proj[..., 4 * GDN_WIDTH + GDN_HEADS:]
    qkv = jax.nn.silu(causal_depthwise_conv(qkv, conv_w))
    q, k, v = [t.reshape(b_, s_, GDN_HEADS, GDN_HEAD_DIM) for t in jnp.split(qkv, 3, axis=-1)]
    q, k = l2_norm(q), l2_norm(k)
    beta = jax.nn.sigmoid(b_logit.astype(jnp.float32))
    g = -jnp.exp(a_log.astype(jnp.float32)) * jax.nn.softplus(
        a_logit.astype(jnp.float32) + dt_bias.astype(jnp.float32))
    o = gated_delta_rule(q, k, v, g, beta)
    zf = z.reshape(b_, s_, GDN_HEADS, GDN_HEAD_DIM).astype(jnp.float32)
    o = rms_norm(o, norm_g) * jax.nn.silu(zf)
    return o.reshape(b_, s_, GDN_WIDTH).astype(h.dtype) @ w_out


def mla_shared_kv(x, c, ada_w, ada_b, norm_g, w_dkv, kv_norm_g, w_ukv, k_norm_g, positions):
    b_, s_, _ = x.shape
    shift, scale = jnp.split(jax.nn.silu(c) @ ada_w + ada_b, 2, axis=-1)
    h = modulate(x, norm_g, shift, scale)
    ckv = h @ w_dkv
    latent = rms_norm(ckv[..., :KV_LORA], kv_norm_g)
    k_rope = ckv[..., KV_LORA:]
    kv = (latent @ w_ukv).reshape(b_, s_, MLA_HEADS, QK_NOPE + V_HEAD)
    k_nope, v = kv[..., :QK_NOPE], kv[..., QK_NOPE:]
    k = jnp.concatenate(
        [k_nope, jnp.broadcast_to(k_rope[:, :, None, :], (b_, s_, MLA_HEADS, QK_ROPE))], axis=-1)
    k = rms_norm(k, k_norm_g)
    k = jnp.concatenate([k[..., :QK_NOPE], rotary(k[..., QK_NOPE:], positions)], axis=-1)
    return k, v


def block_causal_attention(q, k, v):
    b_, s_, h_, dq = q.shape
    nqb = s_ // QBLOCK
    qb = q.reshape(b_, nqb, QBLOCK, h_, dq).transpose(1, 0, 2, 3, 4)
    key_chunk = jnp.arange(s_) // CHUNK
    scale = QK_HEAD ** -0.5

    def one_block(args):
        qi, blk = args
        s = jnp.einsum("bqhd,bkhd->bhqk", qi, k).astype(jnp.float32) * scale
        q_chunk = (blk * QBLOCK + jnp.arange(QBLOCK)) // CHUNK
        mask = key_chunk[None, :] <= q_chunk[:, None]
        s = jnp.where(mask, s, jnp.finfo(jnp.float32).min)
        p = jax.nn.softmax(s, axis=-1)
        return jnp.einsum("bhqk,bkhd->bqhd", p.astype(v.dtype), v)

    o = lax.map(one_block, (qb, jnp.arange(nqb)))
    return o.transpose(1, 0, 2, 3, 4).reshape(b_, s_, h_ * v.shape[-1])


def mla_attention(h, k, v, positions, w_dq, q_lora_norm_g, w_uq, q_norm_g, w_out):
    b_, s_, _ = h.shape
    q = rms_norm(h @ w_dq, q_lora_norm_g) @ w_uq
    q = rms_norm(q.reshape(b_, s_, MLA_HEADS, QK_HEAD), q_norm_g)
    q = jnp.concatenate([q[..., :QK_NOPE], rotary(q[..., QK_NOPE:], positions)], axis=-1)
    return block_causal_attention(q, k, v) @ w_out


def _fwd_setup_inputs(seed: int = 0) -> dict:
    key = jax.random.key(seed)
    ks = iter(jax.random.split(key, 40))
    f32 = jnp.float32

    def nrm(shape, fan_in, scale=1.0):
        return jax.random.normal(next(ks), shape, f32) * (scale * fan_in ** -0.5)

    def gain(shape):
        return 1.0 + 0.02 * jax.random.normal(next(ks), shape, f32)

    def bias(shape):
        return 0.02 * jax.random.normal(next(ks), shape, f32)

    x = jax.random.normal(next(ks), (BATCH, SEQ, D_MODEL), f32)
    c = jax.random.normal(next(ks), (BATCH, D_MODEL), f32)
    offs = jax.random.randint(next(ks), (BATCH, 1), 0, MAX_POS_OFFSET)
    positions = (offs + jnp.arange(SEQ)[None, :]).astype(jnp.int32)

    dt = jnp.exp(jax.random.uniform(next(ks), (N_A_LAYERS, GDN_HEADS), f32,
                                    np.log(1e-3), np.log(1e-1)))
    return {
        "x": x,
        "c": c,
        "positions": positions,
        "ada_w": nrm((DEPTH, D_MODEL, N_MOD * D_MODEL), D_MODEL, 0.5),
        "ada_b": bias((DEPTH, N_MOD * D_MODEL)),
        "norm_g": gain((DEPTH, 3, D_MODEL)),
        "ffn_w_in": nrm((DEPTH, 2, D_MODEL, 2 * D_FF), D_MODEL),
        "ffn_w_out": nrm((DEPTH, 2, D_FF, D_MODEL), D_FF),
        "gdn_w_in": nrm((N_A_LAYERS, D_MODEL, GDN_PROJ), D_MODEL),
        "gdn_conv_w": nrm((N_A_LAYERS, CONV_K, 3 * GDN_WIDTH), CONV_K),
        "gdn_a_log": jnp.log(jax.random.uniform(next(ks), (N_A_LAYERS, GDN_HEADS), f32, 1.0, 16.0)),
        "gdn_dt_bias": dt + jnp.log(-jnp.expm1(-dt)),
        "gdn_norm_g": gain((N_A_LAYERS, GDN_HEAD_DIM)),
        "gdn_w_out": nrm((N_A_LAYERS, GDN_WIDTH, D_MODEL), GDN_WIDTH),
        "kv_ada_w": nrm((D_MODEL, 2 * D_MODEL), D_MODEL, 0.5),
        "kv_ada_b": bias((2 * D_MODEL,)),
        "kv_norm_g": gain((D_MODEL,)),
        "mla_w_dkv": nrm((D_MODEL, KV_LORA + QK_ROPE), D_MODEL),
        "mla_kv_norm_g": gain((KV_LORA,)),
        "mla_w_ukv": nrm((KV_LORA, MLA_HEADS * (QK_NOPE + V_HEAD)), KV_LORA),
        "mla_k_norm_g": gain((QK_HEAD,)),
        "mla_w_dq": nrm((N_B_LAYERS, D_MODEL, Q_LORA), D_MODEL),
        "mla_q_lora_norm_g": gain((N_B_LAYERS, Q_LORA)),
        "mla_w_uq": nrm((N_B_LAYERS, Q_LORA, MLA_HEADS * QK_HEAD), Q_LORA),
        "mla_q_norm_g": gain((N_B_LAYERS, QK_HEAD)),
        "mla_w_out": nrm((N_B_LAYERS, MLA_HEADS * V_HEAD, D_MODEL), MLA_HEADS * V_HEAD),
    }


def _fwd_reference(x, c, positions, ada_w, ada_b, norm_g, ffn_w_in, ffn_w_out,
              gdn_w_in, gdn_conv_w, gdn_a_log, gdn_dt_bias, gdn_norm_g, gdn_w_out,
              kv_ada_w, kv_ada_b, kv_norm_g, mla_w_dkv, mla_kv_norm_g, mla_w_ukv, mla_k_norm_g,
              mla_w_dq, mla_q_lora_norm_g, mla_w_uq, mla_q_norm_g, mla_w_out):
    b_ = x.shape[0]
    c_act = jax.nn.silu(c)
    shared_k = shared_v = None
    for l in range(DEPTH):
        mod = (c_act @ ada_w[l] + ada_b[l]).reshape(b_, N_MOD, D_MODEL)
        h = modulate(x, norm_g[l, 0], mod[:, 0], mod[:, 1])
        x = x + 0.5 * mod[:, 2][:, None, :] * swiglu(h, ffn_w_in[l, 0], ffn_w_out[l, 0])
        h = modulate(x, norm_g[l, 1], mod[:, 3], mod[:, 4])
        if l < N_A_LAYERS:
            y = gated_deltanet(h, gdn_w_in[l], gdn_conv_w[l], gdn_a_log[l], gdn_dt_bias[l],
                               gdn_norm_g[l], gdn_w_out[l])
        else:
            j = l - N_A_LAYERS
            y = mla_attention(h, shared_k, shared_v, positions, mla_w_dq[j], mla_q_lora_norm_g[j],
                              mla_w_uq[j], mla_q_norm_g[j], mla_w_out[j])
        x = x + mod[:, 5][:, None, :] * y
        h = modulate(x, norm_g[l, 2], mod[:, 6], mod[:, 7])
        x = x + 0.5 * mod[:, 8][:, None, :] * swiglu(h, ffn_w_in[l, 1], ffn_w_out[l, 1])
        if l == N_A_LAYERS - 1:
            shared_k, shared_v = mla_shared_kv(x, c, kv_ada_w, kv_ada_b, kv_norm_g, mla_w_dkv,
                                               mla_kv_norm_g, mla_w_ukv, mla_k_norm_g, positions)
    return x


import jax as _jax
import jax.numpy as _jnp

TWIN_FORMAT = 'train_step'
FWD_PARAMS = ['x', 'c', 'positions', 'ada_w', 'ada_b', 'norm_g', 'ffn_w_in', 'ffn_w_out', 'gdn_w_in', 'gdn_conv_w', 'gdn_a_log', 'gdn_dt_bias', 'gdn_norm_g', 'gdn_w_out', 'kv_ada_w', 'kv_ada_b', 'kv_norm_g', 'mla_w_dkv', 'mla_kv_norm_g', 'mla_w_ukv', 'mla_k_norm_g', 'mla_w_dq', 'mla_q_lora_norm_g', 'mla_w_uq', 'mla_q_norm_g', 'mla_w_out']
TWIN_WEIGHTS = ['ada_w', 'ada_b', 'norm_g', 'ffn_w_in', 'ffn_w_out', 'gdn_w_in', 'gdn_conv_w', 'gdn_a_log', 'gdn_dt_bias', 'gdn_norm_g', 'gdn_w_out', 'kv_ada_w', 'kv_ada_b', 'kv_norm_g', 'mla_w_dkv', 'mla_kv_norm_g', 'mla_w_ukv', 'mla_k_norm_g', 'mla_w_dq', 'mla_q_lora_norm_g', 'mla_w_uq', 'mla_q_norm_g', 'mla_w_out']
TWIN_DIFF_INPUT = 'x'
TWIN_INPUTS = ['x', 'c', 'positions', 'ada_w', 'ada_b', 'norm_g', 'ffn_w_in', 'ffn_w_out', 'gdn_w_in', 'gdn_conv_w', 'gdn_a_log', 'gdn_dt_bias', 'gdn_norm_g', 'gdn_w_out', 'kv_ada_w', 'kv_ada_b', 'kv_norm_g', 'mla_w_dkv', 'mla_kv_norm_g', 'mla_w_ukv', 'mla_k_norm_g', 'mla_w_dq', 'mla_q_lora_norm_g', 'mla_w_uq', 'mla_q_norm_g', 'mla_w_out', 'loss_target', 'm_ada_w', 'm_ada_b', 'm_norm_g', 'm_ffn_w_in', 'm_ffn_w_out', 'm_gdn_w_in', 'm_gdn_conv_w', 'm_gdn_a_log', 'm_gdn_dt_bias', 'm_gdn_norm_g', 'm_gdn_w_out', 'm_kv_ada_w', 'm_kv_ada_b', 'm_kv_norm_g', 'm_mla_w_dkv', 'm_mla_kv_norm_g', 'm_mla_w_ukv', 'm_mla_k_norm_g', 'm_mla_w_dq', 'm_mla_q_lora_norm_g', 'm_mla_w_uq', 'm_mla_q_norm_g', 'm_mla_w_out', 'v_ada_w', 'v_ada_b', 'v_norm_g', 'v_ffn_w_in', 'v_ffn_w_out', 'v_gdn_w_in', 'v_gdn_conv_w', 'v_gdn_a_log', 'v_gdn_dt_bias', 'v_gdn_norm_g', 'v_gdn_w_out', 'v_kv_ada_w', 'v_kv_ada_b', 'v_kv_norm_g', 'v_mla_w_dkv', 'v_mla_kv_norm_g', 'v_mla_w_ukv', 'v_mla_k_norm_g', 'v_mla_w_dq', 'v_mla_q_lora_norm_g', 'v_mla_w_uq', 'v_mla_q_norm_g', 'v_mla_w_out']
TWIN_OUTPUTS = ['loss', 'grad_x', 'grad_ada_w', 'grad_ada_b', 'grad_norm_g', 'grad_ffn_w_in', 'grad_ffn_w_out', 'grad_gdn_w_in', 'grad_gdn_conv_w', 'grad_gdn_a_log', 'grad_gdn_dt_bias', 'grad_gdn_norm_g', 'grad_gdn_w_out', 'grad_kv_ada_w', 'grad_kv_ada_b', 'grad_kv_norm_g', 'grad_mla_w_dkv', 'grad_mla_kv_norm_g', 'grad_mla_w_ukv', 'grad_mla_k_norm_g', 'grad_mla_w_dq', 'grad_mla_q_lora_norm_g', 'grad_mla_w_uq', 'grad_mla_q_norm_g', 'grad_mla_w_out', 'delta_ada_w', 'delta_ada_b', 'delta_norm_g', 'delta_ffn_w_in', 'delta_ffn_w_out', 'delta_gdn_w_in', 'delta_gdn_conv_w', 'delta_gdn_a_log', 'delta_gdn_dt_bias', 'delta_gdn_norm_g', 'delta_gdn_w_out', 'delta_kv_ada_w', 'delta_kv_ada_b', 'delta_kv_norm_g', 'delta_mla_w_dkv', 'delta_mla_kv_norm_g', 'delta_mla_w_ukv', 'delta_mla_k_norm_g', 'delta_mla_w_dq', 'delta_mla_q_lora_norm_g', 'delta_mla_w_uq', 'delta_mla_q_norm_g', 'delta_mla_w_out', 'new_m_ada_w', 'new_m_ada_b', 'new_m_norm_g', 'new_m_ffn_w_in', 'new_m_ffn_w_out', 'new_m_gdn_w_in', 'new_m_gdn_conv_w', 'new_m_gdn_a_log', 'new_m_gdn_dt_bias', 'new_m_gdn_norm_g', 'new_m_gdn_w_out', 'new_m_kv_ada_w', 'new_m_kv_ada_b', 'new_m_kv_norm_g', 'new_m_mla_w_dkv', 'new_m_mla_kv_norm_g', 'new_m_mla_w_ukv', 'new_m_mla_k_norm_g', 'new_m_mla_w_dq', 'new_m_mla_q_lora_norm_g', 'new_m_mla_w_uq', 'new_m_mla_q_norm_g', 'new_m_mla_w_out', 'new_v_ada_w', 'new_v_ada_b', 'new_v_norm_g', 'new_v_ffn_w_in', 'new_v_ffn_w_out', 'new_v_gdn_w_in', 'new_v_gdn_conv_w', 'new_v_gdn_a_log', 'new_v_gdn_dt_bias', 'new_v_gdn_norm_g', 'new_v_gdn_w_out', 'new_v_kv_ada_w', 'new_v_kv_ada_b', 'new_v_kv_norm_g', 'new_v_mla_w_dkv', 'new_v_mla_kv_norm_g', 'new_v_mla_w_ukv', 'new_v_mla_k_norm_g', 'new_v_mla_w_dq', 'new_v_mla_q_lora_norm_g', 'new_v_mla_w_uq', 'new_v_mla_q_norm_g', 'new_v_mla_w_out']
TWIN_LEAF_KINDS = {'loss': 'loss', 'grad_x': 'grad_x', 'grad_ada_w': 'grad_w', 'grad_ada_b': 'grad_w', 'grad_norm_g': 'grad_w', 'grad_ffn_w_in': 'grad_w', 'grad_ffn_w_out': 'grad_w', 'grad_gdn_w_in': 'grad_w', 'grad_gdn_conv_w': 'grad_w', 'grad_gdn_a_log': 'grad_w', 'grad_gdn_dt_bias': 'grad_w', 'grad_gdn_norm_g': 'grad_w', 'grad_gdn_w_out': 'grad_w', 'grad_kv_ada_w': 'grad_w', 'grad_kv_ada_b': 'grad_w', 'grad_kv_norm_g': 'grad_w', 'grad_mla_w_dkv': 'grad_w', 'grad_mla_kv_norm_g': 'grad_w', 'grad_mla_w_ukv': 'grad_w', 'grad_mla_k_norm_g': 'grad_w', 'grad_mla_w_dq': 'grad_w', 'grad_mla_q_lora_norm_g': 'grad_w', 'grad_mla_w_uq': 'grad_w', 'grad_mla_q_norm_g': 'grad_w', 'grad_mla_w_out': 'grad_w', 'delta_ada_w': 'delta_w', 'delta_ada_b': 'delta_w', 'delta_norm_g': 'delta_w', 'delta_ffn_w_in': 'delta_w', 'delta_ffn_w_out': 'delta_w', 'delta_gdn_w_in': 'delta_w', 'delta_gdn_conv_w': 'delta_w', 'delta_gdn_a_log': 'delta_w', 'delta_gdn_dt_bias': 'delta_w', 'delta_gdn_norm_g': 'delta_w', 'delta_gdn_w_out': 'delta_w', 'delta_kv_ada_w': 'delta_w', 'delta_kv_ada_b': 'delta_w', 'delta_kv_norm_g': 'delta_w', 'delta_mla_w_dkv': 'delta_w', 'delta_mla_kv_norm_g': 'delta_w', 'delta_mla_w_ukv': 'delta_w', 'delta_mla_k_norm_g': 'delta_w', 'delta_mla_w_dq': 'delta_w', 'delta_mla_q_lora_norm_g': 'delta_w', 'delta_mla_w_uq': 'delta_w', 'delta_mla_q_norm_g': 'delta_w', 'delta_mla_w_out': 'delta_w', 'new_m_ada_w': 'new_m', 'new_m_ada_b': 'new_m', 'new_m_norm_g': 'new_m', 'new_m_ffn_w_in': 'new_m', 'new_m_ffn_w_out': 'new_m', 'new_m_gdn_w_in': 'new_m', 'new_m_gdn_conv_w': 'new_m', 'new_m_gdn_a_log': 'new_m', 'new_m_gdn_dt_bias': 'new_m', 'new_m_gdn_norm_g': 'new_m', 'new_m_gdn_w_out': 'new_m', 'new_m_kv_ada_w': 'new_m', 'new_m_kv_ada_b': 'new_m', 'new_m_kv_norm_g': 'new_m', 'new_m_mla_w_dkv': 'new_m', 'new_m_mla_kv_norm_g': 'new_m', 'new_m_mla_w_ukv': 'new_m', 'new_m_mla_k_norm_g': 'new_m', 'new_m_mla_w_dq': 'new_m', 'new_m_mla_q_lora_norm_g': 'new_m', 'new_m_mla_w_uq': 'new_m', 'new_m_mla_q_norm_g': 'new_m', 'new_m_mla_w_out': 'new_m', 'new_v_ada_w': 'new_v', 'new_v_ada_b': 'new_v', 'new_v_norm_g': 'new_v', 'new_v_ffn_w_in': 'new_v', 'new_v_ffn_w_out': 'new_v', 'new_v_gdn_w_in': 'new_v', 'new_v_gdn_conv_w': 'new_v', 'new_v_gdn_a_log': 'new_v', 'new_v_gdn_dt_bias': 'new_v', 'new_v_gdn_norm_g': 'new_v', 'new_v_gdn_w_out': 'new_v', 'new_v_kv_ada_w': 'new_v', 'new_v_kv_ada_b': 'new_v', 'new_v_kv_norm_g': 'new_v', 'new_v_mla_w_dkv': 'new_v', 'new_v_mla_kv_norm_g': 'new_v', 'new_v_mla_w_ukv': 'new_v', 'new_v_mla_k_norm_g': 'new_v', 'new_v_mla_w_dq': 'new_v', 'new_v_mla_q_lora_norm_g': 'new_v', 'new_v_mla_w_uq': 'new_v', 'new_v_mla_q_norm_g': 'new_v', 'new_v_mla_w_out': 'new_v'}


def _forward(args):
    return _fwd_reference(*[args[k] for k in FWD_PARAMS])


def _output_shape():
    def fwd():
        inp = _fwd_setup_inputs(0)
        return _fwd_reference(*[inp[k] for k in FWD_PARAMS])
    out = _jax.eval_shape(fwd)
    return out.shape, out.dtype

N_MICROBATCH = 1
ADAM_LR = 0.001
ADAM_B1 = 0.9
ADAM_B2 = 0.999
ADAM_EPS = 1e-08
ADAM_WD = 0.01
ADAM_STEP = 10
PER_EXAMPLE_BATCH_AXIS = {'x': 0, 'c': 0, 'positions': 0, 'loss_target': 0}
SHARED_INPUTS = []
_WEIGHT_DTYPES = {'ada_w': _jnp.float32, 'ada_b': _jnp.float32, 'norm_g': _jnp.float32, 'ffn_w_in': _jnp.float32, 'ffn_w_out': _jnp.float32, 'gdn_w_in': _jnp.float32, 'gdn_conv_w': _jnp.float32, 'gdn_a_log': _jnp.float32, 'gdn_dt_bias': _jnp.float32, 'gdn_norm_g': _jnp.float32, 'gdn_w_out': _jnp.float32, 'kv_ada_w': _jnp.float32, 'kv_ada_b': _jnp.float32, 'kv_norm_g': _jnp.float32, 'mla_w_dkv': _jnp.float32, 'mla_kv_norm_g': _jnp.float32, 'mla_w_ukv': _jnp.float32, 'mla_k_norm_g': _jnp.float32, 'mla_w_dq': _jnp.float32, 'mla_q_lora_norm_g': _jnp.float32, 'mla_w_uq': _jnp.float32, 'mla_q_norm_g': _jnp.float32, 'mla_w_out': _jnp.float32}
MOMENT_SCALE = {'ada_w': 3.292836e-01, 'ada_b': 8.271635e-01, 'norm_g': 8.648782e-01, 'ffn_w_in': 2.532738e-02, 'ffn_w_out': 4.212682e-02, 'gdn_w_in': 1.352502e-01, 'gdn_conv_w': 1.489692e-01, 'gdn_a_log': 4.047021e+00, 'gdn_dt_bias': 3.813532e+00, 'gdn_norm_g': 1.056024e+01, 'gdn_w_out': 1.661130e-01, 'kv_ada_w': 6.879954e-01, 'kv_ada_b': 1.350556e+00, 'kv_norm_g': 4.561894e-01, 'mla_w_dkv': 8.116595e-01, 'mla_kv_norm_g': 2.440965e+00, 'mla_w_ukv': 2.330514e-01, 'mla_k_norm_g': 1.465617e-01, 'mla_w_dq': 2.016680e-02, 'mla_q_lora_norm_g': 1.984378e-02, 'mla_w_uq': 1.012932e-02, 'mla_q_norm_g': 7.610787e-02, 'mla_w_out': 1.888367e-01}


def _to_microbatches(a, axis):
    t = _jnp.moveaxis(a, axis, 0)
    t = t.reshape((N_MICROBATCH, t.shape[0] // N_MICROBATCH) + t.shape[1:])
    return _jnp.moveaxis(t, 1, axis + 1)


def setup_inputs(seed: int = 0) -> dict:
    inp = _fwd_setup_inputs(seed)
    key = _jax.random.fold_in(_jax.random.key(seed), 7919)
    shape, _ = _output_shape()
    out = dict(inp)
    out["loss_target"] = _jax.random.normal(_jax.random.fold_in(key, 0), shape, _jnp.float32)
    for i, name in enumerate(TWIN_WEIGHTS):
        w = inp[name].astype(_jnp.float32)
        if MOMENT_SCALE is None:
            s = _jnp.sqrt(_jnp.mean(_jnp.square(w)) + 1e-30)
        else:
            s = MOMENT_SCALE[name]
        km, kv = _jax.random.split(_jax.random.fold_in(key, i + 1))
        out[name] = w
        out["m_" + name] = s * _jax.random.normal(km, w.shape, _jnp.float32)
        out["v_" + name] = (s * s) * _jax.random.uniform(kv, w.shape, _jnp.float32, 0.5, 1.5)
    if N_MICROBATCH > 1:
        for name, axis in PER_EXAMPLE_BATCH_AXIS.items():
            out[name] = _to_microbatches(out[name], axis)
    return {'x': out['x'], 'c': out['c'], 'positions': out['positions'], 'ada_w': out['ada_w'], 'ada_b': out['ada_b'], 'norm_g': out['norm_g'], 'ffn_w_in': out['ffn_w_in'], 'ffn_w_out': out['ffn_w_out'], 'gdn_w_in': out['gdn_w_in'], 'gdn_conv_w': out['gdn_conv_w'], 'gdn_a_log': out['gdn_a_log'], 'gdn_dt_bias': out['gdn_dt_bias'], 'gdn_norm_g': out['gdn_norm_g'], 'gdn_w_out': out['gdn_w_out'], 'kv_ada_w': out['kv_ada_w'], 'kv_ada_b': out['kv_ada_b'], 'kv_norm_g': out['kv_norm_g'], 'mla_w_dkv': out['mla_w_dkv'], 'mla_kv_norm_g': out['mla_kv_norm_g'], 'mla_w_ukv': out['mla_w_ukv'], 'mla_k_norm_g': out['mla_k_norm_g'], 'mla_w_dq': out['mla_w_dq'], 'mla_q_lora_norm_g': out['mla_q_lora_norm_g'], 'mla_w_uq': out['mla_w_uq'], 'mla_q_norm_g': out['mla_q_norm_g'], 'mla_w_out': out['mla_w_out'], 'loss_target': out['loss_target'], 'm_ada_w': out['m_ada_w'], 'm_ada_b': out['m_ada_b'], 'm_norm_g': out['m_norm_g'], 'm_ffn_w_in': out['m_ffn_w_in'], 'm_ffn_w_out': out['m_ffn_w_out'], 'm_gdn_w_in': out['m_gdn_w_in'], 'm_gdn_conv_w': out['m_gdn_conv_w'], 'm_gdn_a_log': out['m_gdn_a_log'], 'm_gdn_dt_bias': out['m_gdn_dt_bias'], 'm_gdn_norm_g': out['m_gdn_norm_g'], 'm_gdn_w_out': out['m_gdn_w_out'], 'm_kv_ada_w': out['m_kv_ada_w'], 'm_kv_ada_b': out['m_kv_ada_b'], 'm_kv_norm_g': out['m_kv_norm_g'], 'm_mla_w_dkv': out['m_mla_w_dkv'], 'm_mla_kv_norm_g': out['m_mla_kv_norm_g'], 'm_mla_w_ukv': out['m_mla_w_ukv'], 'm_mla_k_norm_g': out['m_mla_k_norm_g'], 'm_mla_w_dq': out['m_mla_w_dq'], 'm_mla_q_lora_norm_g': out['m_mla_q_lora_norm_g'], 'm_mla_w_uq': out['m_mla_w_uq'], 'm_mla_q_norm_g': out['m_mla_q_norm_g'], 'm_mla_w_out': out['m_mla_w_out'], 'v_ada_w': out['v_ada_w'], 'v_ada_b': out['v_ada_b'], 'v_norm_g': out['v_norm_g'], 'v_ffn_w_in': out['v_ffn_w_in'], 'v_ffn_w_out': out['v_ffn_w_out'], 'v_gdn_w_in': out['v_gdn_w_in'], 'v_gdn_conv_w': out['v_gdn_conv_w'], 'v_gdn_a_log': out['v_gdn_a_log'], 'v_gdn_dt_bias': out['v_gdn_dt_bias'], 'v_gdn_norm_g': out['v_gdn_norm_g'], 'v_gdn_w_out': out['v_gdn_w_out'], 'v_kv_ada_w': out['v_kv_ada_w'], 'v_kv_ada_b': out['v_kv_ada_b'], 'v_kv_norm_g': out['v_kv_norm_g'], 'v_mla_w_dkv': out['v_mla_w_dkv'], 'v_mla_kv_norm_g': out['v_mla_kv_norm_g'], 'v_mla_w_ukv': out['v_mla_w_ukv'], 'v_mla_k_norm_g': out['v_mla_k_norm_g'], 'v_mla_w_dq': out['v_mla_w_dq'], 'v_mla_q_lora_norm_g': out['v_mla_q_lora_norm_g'], 'v_mla_w_uq': out['v_mla_w_uq'], 'v_mla_q_norm_g': out['v_mla_q_norm_g'], 'v_mla_w_out': out['v_mla_w_out']}


def _loss(weights, diff, rest, loss_target):
    with _jax.named_scope("forward"):
        args = {**rest, TWIN_DIFF_INPUT: diff, **{k: w.astype(_WEIGHT_DTYPES[k]) for k, w in weights.items()}}
        y = _forward(args)
    with _jax.named_scope("loss_head"):
        err = _jnp.square(y.astype(_jnp.float32) - loss_target)
        return 0.5 * _jnp.sum(_jnp.mean(err, axis=-1)) if err.ndim else 0.5 * err


def _adamw(w, g, m, v):
    m = ADAM_B1 * m + (1.0 - ADAM_B1) * g
    v = ADAM_B2 * v + (1.0 - ADAM_B2) * _jnp.square(g)
    m_hat = m / (1.0 - ADAM_B1 ** ADAM_STEP)
    v_hat = v / (1.0 - ADAM_B2 ** ADAM_STEP)
    delta = -ADAM_LR * (m_hat / (_jnp.sqrt(v_hat) + ADAM_EPS) + ADAM_WD * w)
    return delta, m, v


def reference(x, c, positions, ada_w, ada_b, norm_g, ffn_w_in, ffn_w_out, gdn_w_in, gdn_conv_w, gdn_a_log, gdn_dt_bias, gdn_norm_g, gdn_w_out, kv_ada_w, kv_ada_b, kv_norm_g, mla_w_dkv, mla_kv_norm_g, mla_w_ukv, mla_k_norm_g, mla_w_dq, mla_q_lora_norm_g, mla_w_uq, mla_q_norm_g, mla_w_out, loss_target, m_ada_w, m_ada_b, m_norm_g, m_ffn_w_in, m_ffn_w_out, m_gdn_w_in, m_gdn_conv_w, m_gdn_a_log, m_gdn_dt_bias, m_gdn_norm_g, m_gdn_w_out, m_kv_ada_w, m_kv_ada_b, m_kv_norm_g, m_mla_w_dkv, m_mla_kv_norm_g, m_mla_w_ukv, m_mla_k_norm_g, m_mla_w_dq, m_mla_q_lora_norm_g, m_mla_w_uq, m_mla_q_norm_g, m_mla_w_out, v_ada_w, v_ada_b, v_norm_g, v_ffn_w_in, v_ffn_w_out, v_gdn_w_in, v_gdn_conv_w, v_gdn_a_log, v_gdn_dt_bias, v_gdn_norm_g, v_gdn_w_out, v_kv_ada_w, v_kv_ada_b, v_kv_norm_g, v_mla_w_dkv, v_mla_kv_norm_g, v_mla_w_ukv, v_mla_k_norm_g, v_mla_w_dq, v_mla_q_lora_norm_g, v_mla_w_uq, v_mla_q_norm_g, v_mla_w_out):
    given = dict(x=x, c=c, positions=positions, ada_w=ada_w, ada_b=ada_b, norm_g=norm_g, ffn_w_in=ffn_w_in, ffn_w_out=ffn_w_out, gdn_w_in=gdn_w_in, gdn_conv_w=gdn_conv_w, gdn_a_log=gdn_a_log, gdn_dt_bias=gdn_dt_bias, gdn_norm_g=gdn_norm_g, gdn_w_out=gdn_w_out, kv_ada_w=kv_ada_w, kv_ada_b=kv_ada_b, kv_norm_g=kv_norm_g, mla_w_dkv=mla_w_dkv, mla_kv_norm_g=mla_kv_norm_g, mla_w_ukv=mla_w_ukv, mla_k_norm_g=mla_k_norm_g, mla_w_dq=mla_w_dq, mla_q_lora_norm_g=mla_q_lora_norm_g, mla_w_uq=mla_w_uq, mla_q_norm_g=mla_q_norm_g, mla_w_out=mla_w_out, loss_target=loss_target, m_ada_w=m_ada_w, m_ada_b=m_ada_b, m_norm_g=m_norm_g, m_ffn_w_in=m_ffn_w_in, m_ffn_w_out=m_ffn_w_out, m_gdn_w_in=m_gdn_w_in, m_gdn_conv_w=m_gdn_conv_w, m_gdn_a_log=m_gdn_a_log, m_gdn_dt_bias=m_gdn_dt_bias, m_gdn_norm_g=m_gdn_norm_g, m_gdn_w_out=m_gdn_w_out, m_kv_ada_w=m_kv_ada_w, m_kv_ada_b=m_kv_ada_b, m_kv_norm_g=m_kv_norm_g, m_mla_w_dkv=m_mla_w_dkv, m_mla_kv_norm_g=m_mla_kv_norm_g, m_mla_w_ukv=m_mla_w_ukv, m_mla_k_norm_g=m_mla_k_norm_g, m_mla_w_dq=m_mla_w_dq, m_mla_q_lora_norm_g=m_mla_q_lora_norm_g, m_mla_w_uq=m_mla_w_uq, m_mla_q_norm_g=m_mla_q_norm_g, m_mla_w_out=m_mla_w_out, v_ada_w=v_ada_w, v_ada_b=v_ada_b, v_norm_g=v_norm_g, v_ffn_w_in=v_ffn_w_in, v_ffn_w_out=v_ffn_w_out, v_gdn_w_in=v_gdn_w_in, v_gdn_conv_w=v_gdn_conv_w, v_gdn_a_log=v_gdn_a_log, v_gdn_dt_bias=v_gdn_dt_bias, v_gdn_norm_g=v_gdn_norm_g, v_gdn_w_out=v_gdn_w_out, v_kv_ada_w=v_kv_ada_w, v_kv_ada_b=v_kv_ada_b, v_kv_norm_g=v_kv_norm_g, v_mla_w_dkv=v_mla_w_dkv, v_mla_kv_norm_g=v_mla_kv_norm_g, v_mla_w_ukv=v_mla_w_ukv, v_mla_k_norm_g=v_mla_k_norm_g, v_mla_w_dq=v_mla_w_dq, v_mla_q_lora_norm_g=v_mla_q_lora_norm_g, v_mla_w_uq=v_mla_w_uq, v_mla_q_norm_g=v_mla_q_norm_g, v_mla_w_out=v_mla_w_out)
    weights = {n: given[n] for n in TWIN_WEIGHTS}
    shared = {n: given[n] for n in SHARED_INPUTS}
    per_example = {n: given[n] for n in ['x', 'c', 'positions']}
    grad_fn = _jax.value_and_grad(_loss, argnums=(0, 1))

    def one_microbatch(ex, loss_target):
        ex = dict(ex)
        diff = ex.pop(TWIN_DIFF_INPUT)
        return grad_fn(weights, diff, {**shared, **ex}, loss_target)

    if N_MICROBATCH == 1:
        loss, (grad_w, grad_x) = one_microbatch(per_example, given["loss_target"])
    else:
        def body(carry, xs):
            loss_sum, grad_sum = carry
            l_k, (gw_k, gx_k) = one_microbatch(xs[0], xs[1])
            with _jax.named_scope("update"):
                return (loss_sum + l_k, _jax.tree.map(_jnp.add, grad_sum, gw_k)), gx_k

        init = (_jnp.zeros((), _jnp.float32), _jax.tree.map(_jnp.zeros_like, weights))
        (loss, grad_w), grad_x = _jax.lax.scan(body, init, (per_example, given["loss_target"]))
    with _jax.named_scope("update"):
        delta_w, new_m, new_v = {}, {}, {}
        for n in TWIN_WEIGHTS:
            delta_w[n], new_m[n], new_v[n] = _adamw(weights[n], grad_w[n], given["m_" + n], given["v_" + n])
    return (loss, grad_x, *[grad_w[n] for n in TWIN_WEIGHTS], *[delta_w[n] for n in TWIN_WEIGHTS],
            *[new_m[n] for n in TWIN_WEIGHTS], *[new_v[n] for n in TWIN_WEIGHTS])
```

```python
import functools

import jax
import jax.numpy as jnp
from jax import lax
from jax.experimental import pallas as pl
from jax.experimental.pallas import tpu as pltpu

F32 = jnp.float32
BF16 = jnp.bfloat16
HI = lax.Precision.HIGHEST
MESH = pl.DeviceIdType.MESH

D = 1024
NH = 8
DH = 128
FF = 2816
NMOD = 9
CHUNK = 64
ROPE = 64
QKH = 192
HP = 256
KVL = 256
QL = 384
GDN_IN = 4224
GATE_CB = 32
EPS = 1e-6
ROPE_BASE = 10000.0
LANE = 128
SUB = 8
VMEM_LIMIT = 56 * 1024 * 1024

ADAM_LR, ADAM_B1, ADAM_B2, ADAM_EPS, ADAM_WD, ADAM_STEP = 0.001, 0.9, 0.999, 1e-08, 0.01, 10


def _tile(n, prefs=(512, 384, 256, 128)):
    for p in prefs:
        if n % p == 0:
            return p
    return n


def _cparams(sem):
    return pltpu.CompilerParams(dimension_semantics=sem, vmem_limit_bytes=VMEM_LIMIT)


class Row:
    def __init__(self, arr, width=None, cb=0, splits=None, halo=None):
        self.arr = arr
        self.width = arr.shape[1] if width is None else width
        self.cb = cb
        self.splits = splits
        self.halo = halo


def _rowwise(name, fn, rows, bcs, outs, accs, tm):
    S = rows[0].arr.shape[0]
    n = S // tm
    nr, nb, no, na = len(rows), len(bcs), len(outs), len(accs)

    def body(*refs):
        rrefs, brefs = refs[:nr], refs[nr:nr + nb]
        orefs, arefs = refs[nr + nb:nr + nb + no], refs[nr + nb + no:]
        pieces = []
        for r, ref in zip(rows, rrefs):
            if r.splits is None:
                pieces.append(ref[...])
            else:
                off = 0
                for w in r.splits:
                    pieces.append(ref[:, off:off + w])
                    off += w
        out_pieces, acc_vals = fn(pieces, [b[...] for b in brefs])
        k = 0
        for (widths, dt), oref in zip(outs, orefs):
            off = 0
            for w in widths:
                oref[:, off:off + w] = out_pieces[k].astype(dt)
                k += 1
                off += w
        if na:
            @pl.when(pl.program_id(0) == 0)
            def _():
                for a in arefs:
                    a[...] = jnp.zeros(a.shape, F32)
            for a, v in zip(arefs, acc_vals):
                a[...] += v

    in_specs = []
    for r in rows:
        if r.halo is None:
            in_specs.append(pl.BlockSpec((tm, r.width), lambda i, cb=r.cb: (i, cb)))
        elif r.halo == "prev":
            in_specs.append(pl.BlockSpec((SUB, r.width), lambda i, cb=r.cb: (jnp.maximum(i * (tm // SUB) - 1, 0), cb)))
        else:
            in_specs.append(pl.BlockSpec((SUB, r.width), lambda i, cb=r.cb: (jnp.minimum((i + 1) * (tm // SUB), S // SUB - 1), cb)))
    in_specs += [pl.BlockSpec(b.shape, lambda i, nd=b.ndim: (0,) * nd) for b in bcs]
    out_specs = [pl.BlockSpec((tm, sum(w)), lambda i: (i, 0)) for w, _ in outs]
    out_specs += [pl.BlockSpec(s, lambda i: (0, 0)) for s in accs]
    out_shape = [jax.ShapeDtypeStruct((S, sum(w)), dt) for w, dt in outs]
    out_shape += [jax.ShapeDtypeStruct(s, F32) for s in accs]
    res = pl.pallas_call(body, name=name, grid=(n,), in_specs=in_specs, out_specs=out_specs, out_shape=out_shape,
                         compiler_params=_cparams(("arbitrary",)))(*[r.arr for r in rows], *bcs)
    return res


def _rw_fwd(name, f, rows, bcs, outs, tm):
    def fn(pieces, bvals):
        return list(f(*[p.astype(F32) for p in pieces], *[b.astype(F32) for b in bvals])), []
    return _rowwise(name, fn, rows, bcs, outs, [], tm)


def _npieces(rows):
    return sum(1 if r.splits is None else len(r.splits) for r in rows)


def _rw_bwd(name, f, rows, bcs, cts, drow, dbc, outs, tm, add=None):
    np_, nct = _npieces(rows), _npieces(cts)

    def fn(pieces, bvals):
        allv = [p.astype(F32) for p in pieces[:np_]] + [b.astype(F32) for b in bvals]
        ct = [p.astype(F32) for p in pieces[np_:np_ + nct]]
        didx = [i for i, m in enumerate(list(drow) + list(dbc)) if m]

        def g(*dv):
            full = list(allv)
            for i, v in zip(didx, dv):
                full[i] = v
            return tuple(f(*full))

        _, vjp = jax.vjp(g, *[allv[i] for i in didx])
        grads = vjp(tuple(ct))
        nrd = sum(bool(m) for m in drow)
        rg, bg = list(grads[:nrd]), list(grads[nrd:])
        if add is not None:
            rg[0] = rg[0] + pieces[np_ + nct].astype(F32)
        return rg, bg

    accs = [b.shape for b, m in zip(bcs, dbc) if m]
    return _rowwise(name, fn, list(rows) + list(cts) + ([add] if add is not None else []), bcs, outs, accs, tm)


def _sigmoid(x):
    return 1.0 / (1.0 + jnp.exp(-x))


def _silu(x):
    return x * _sigmoid(x)


def _softplus(x):
    return jnp.maximum(x, 0.0) + jnp.log(1.0 + jnp.exp(-jnp.abs(x)))


def f_mod(x, g, shift, scale):
    y = x * lax.rsqrt(jnp.mean(x * x, axis=-1, keepdims=True) + EPS)
    return (y * g * (1.0 + scale) + shift,)


def f_rms(x, g):
    return (x * lax.rsqrt(jnp.mean(x * x, axis=-1, keepdims=True) + EPS) * g,)


def f_act(gate, up):
    return (_silu(gate) * up,)


def make_f_res(coef):
    def f_res(y, gate):
        return (coef * gate * y,)
    return f_res


def f_gdnpre(*p):
    out = []
    for i, t in enumerate(p):
        t = _silu(t)
        if i < 2 * NH:
            t = t * lax.rsqrt(jnp.sum(t * t, axis=-1, keepdims=True) + EPS)
        out.append(t)
    return tuple(out)


def f_gates(gates, a_log, dt_bias):
    return _sigmoid(gates), -jnp.exp(a_log) * _softplus(gates + dt_bias)


def f_gdnpost(*a):
    o, z, g = a[:NH], a[NH:2 * NH], a[2 * NH]
    out = []
    for oh, zh in zip(o, z):
        y = oh * lax.rsqrt(jnp.mean(oh * oh, axis=-1, keepdims=True) + EPS) * g
        out.append(y * _silu(zh))
    return tuple(out)


def make_f_qk(shared_rope):
    def f(*a):
        if shared_rope:
            ns, rs = a[:NH], [a[NH]] * NH
            cosp, sins, gn, gr, pm = a[NH + 1:NH + 6]
        else:
            ns, rs = a[0:2 * NH:2], a[1:2 * NH:2]
            cosp, sins, gn, gr, pm = a[2 * NH:2 * NH + 5]
        out = []
        for n, r in zip(ns, rs):
            ss = jnp.sum(n * n, axis=-1, keepdims=True) + jnp.sum(r * r, axis=-1, keepdims=True)
            rstd = lax.rsqrt(ss * (1.0 / QKH) + EPS)
            yn = n * rstd * gn
            yr = r * rstd * gr
            sw = jnp.dot(yr, pm, precision=HI, preferred_element_type=F32)
            out += [yn, yr * cosp + sw * sins]
        return tuple(out)
    return f


def _matmul(name, a, b, mode="nn", out_dtype=F32, precise=False):
    if mode == "nn":
        (M, K), N = a.shape, b.shape[1]
    elif mode == "nt":
        (M, K), N = a.shape, b.shape[0]
    else:
        (K, M), N = a.shape, b.shape[1]
    tm, tn, tk = _tile(M), _tile(N), _tile(K)
    nk = K // tk
    dims = {"nn": (((1,), (0,)), ((), ())), "nt": (((1,), (1,)), ((), ())), "tn": (((0,), (0,)), ((), ()))}[mode]

    def body(a_ref, b_ref, o_ref, acc_ref):
        k = pl.program_id(2)

        @pl.when(k == 0)
        def _():
            acc_ref[...] = jnp.zeros(acc_ref.shape, F32)

        if precise:
            acc_ref[...] += lax.dot_general(a_ref[...].astype(F32), b_ref[...].astype(F32), dims, precision=HI,
                                            preferred_element_type=F32)
        else:
            acc_ref[...] += lax.dot_general(a_ref[...].astype(BF16), b_ref[...].astype(BF16), dims,
                                            preferred_element_type=F32)

        @pl.when(k == nk - 1)
        def _():
            o_ref[...] = acc_ref[...].astype(o_ref.dtype)

    if mode == "nn":
        in_specs = [pl.BlockSpec((tm, tk), lambda i, j, k: (i, k)), pl.BlockSpec((tk, tn), lambda i, j, k: (k, j))]
    elif mode == "nt":
        in_specs = [pl.BlockSpec((tm, tk), lambda i, j, k: (i, k)), pl.BlockSpec((tn, tk), lambda i, j, k: (j, k))]
    else:
        in_specs = [pl.BlockSpec((tk, tm), lambda i, j, k: (k, i)), pl.BlockSpec((tk, tn), lambda i, j, k: (k, j))]
    return pl.pallas_call(body, name=name, grid=(M // tm, N // tn, nk), in_specs=in_specs,
                          out_specs=pl.BlockSpec((tm, tn), lambda i, j, k: (i, j)),
                          out_shape=jax.ShapeDtypeStruct((M, N), out_dtype),
                          scratch_shapes=[pltpu.VMEM((tm, tn), F32)],
                          compiler_params=_cparams(("parallel", "parallel", "arbitrary")))(a, b)


def _shift_down(t, p, d):
    if d == 0:
        return t
    tr = pltpu.roll(t, d, 0)
    pr = pltpu.roll(p, d, 0)
    r8 = lax.broadcasted_iota(jnp.int32, p.shape, 0)
    first = jnp.where(r8 < d, pr, tr[:SUB])
    return jnp.concatenate([first, tr[SUB:]], axis=0)


def _shift_up(t, nx, d):
    if d == 0:
        return t
    tm = t.shape[0]
    tr = pltpu.roll(t, tm - d, 0)
    nr = pltpu.roll(nx, SUB - d, 0)
    r8 = lax.broadcasted_iota(jnp.int32, nx.shape, 0)
    last = jnp.where(r8 >= SUB - d, nr, tr[tm - SUB:])
    return jnp.concatenate([tr[:tm - SUB], last], axis=0)


def _conv_fwd(name, proj, w8, C, tm):
    def fn(pieces, bvals):
        t, p = pieces[0].astype(F32), pieces[1].astype(F32)
        w = bvals[0]
        p = jnp.where(pl.program_id(0) == 0, 0.0, p)
        out = w[3:4] * t
        for d in (1, 2, 3):
            out = out + w[3 - d:4 - d] * _shift_down(t, p, d)
        return [out], []
    return _rowwise(name, fn, [Row(proj, C), Row(proj, C, halo="prev")], [w8], [((C,), F32)], [], tm)[0]


def _conv_bwd(name, proj, dout, w8, C, tm, out_dtype):
    n = proj.shape[0] // tm

    def fn(pieces, bvals):
        t, p, g, gn = [v.astype(F32) for v in pieces]
        w = bvals[0]
        i = pl.program_id(0)
        p = jnp.where(i == 0, 0.0, p)
        gn = jnp.where(i == n - 1, 0.0, gn)
        dx = w[3:4] * g
        dws = [jnp.sum(g * t, axis=0, keepdims=True)]
        for d in (1, 2, 3):
            dx = dx + w[3 - d:4 - d] * _shift_up(g, gn, d)
            dws.append(jnp.sum(g * _shift_down(t, p, d), axis=0, keepdims=True))
        dw = jnp.concatenate([dws[3], dws[2], dws[1], dws[0], jnp.zeros((4, g.shape[1]), F32)], axis=0)
        return [dx], [dw]
    return _rowwise(name, fn, [Row(proj, C), Row(proj, C, halo="prev"), Row(dout), Row(dout, halo="next")], [w8],
                    [((C,), out_dtype)], [(SUB, C)], tm)


def _gdn_chunk(q, k, v, gcol, grow, bcol, S):
    C = CHUNK
    ii = lax.broadcasted_iota(jnp.int32, (C, C), 0)
    jj = lax.broadcasted_iota(jnp.int32, (C, C), 1)
    incl, strict = ii >= jj, ii > jj
    gc_col = jnp.sum(jnp.where(incl, 1.0, 0.0) * grow, axis=1, keepdims=True)
    gc_row = jnp.sum(jnp.where(jj >= ii, 1.0, 0.0) * gcol, axis=0, keepdims=True)
    decay = jnp.where(incl, jnp.exp(jnp.where(incl, gc_col - gc_row, 0.0)), 0.0)
    dot = functools.partial(jnp.dot, precision=HI, preferred_element_type=F32)
    dot_nt = lambda a, b: lax.dot_general(a, b, (((1,), (1,)), ((), ())), precision=HI, preferred_element_type=F32)
    dot_tn = lambda a, b: lax.dot_general(a, b, (((0,), (0,)), ((), ())), precision=HI, preferred_element_type=F32)
    qs = q * (DH ** -0.5)
    kb = k * bcol
    nl = -jnp.where(strict, dot_nt(kb, k) * decay, 0.0)
    T = jnp.where(ii == jj, 1.0, 0.0) + nl
    pw = nl
    for _ in range(5):
        pw = dot(pw, pw)
        T = T + dot(T, pw)
    egc = jnp.exp(gc_col)
    u = dot(T, v * bcol)
    w = dot(T, kb * egc)
    att = jnp.where(incl, dot_nt(qs, k) * decay, 0.0)
    v_new = u - dot(w, S)
    o = dot(qs * egc, S) + dot(att, v_new)
    g_last = gc_col[C - 1:C, :]
    k_dec = k * jnp.exp(g_last - gc_col)
    S_out = S * jnp.exp(g_last) + dot_tn(k_dec, v_new)
    return o, S_out


def _gdn_specs(NC, rev):
    ix = (lambda i: NC - 1 - i) if rev else (lambda i: i)
    wide = pl.BlockSpec((CHUNK, D), lambda i: (ix(i), 0))
    col = pl.BlockSpec((CHUNK, NH), lambda i: (ix(i), 0))
    row = pl.BlockSpec((1, NH, CHUNK), lambda i: (ix(i), 0, 0))
    st = pl.BlockSpec((1, NH, DH, DH), lambda i: (ix(i), 0, 0, 0))
    return wide, col, row, st


def _gdn_fwd(name, q, k, v, gcol, grow, bcol):
    S = q.shape[0]
    NC = S // CHUNK

    def body(q_ref, k_ref, v_ref, gc_ref, gr_ref, b_ref, o_ref, ss_ref, st):
        @pl.when(pl.program_id(0) == 0)
        def _():
            st[...] = jnp.zeros(st.shape, F32)
        for h in range(NH):
            hs = slice(h * DH, (h + 1) * DH)
            s_in = st[h]
            ss_ref[0, h] = s_in
            o, s_out = _gdn_chunk(q_ref[:, hs], k_ref[:, hs], v_ref[:, hs], gc_ref[:, h:h + 1], gr_ref[0, h:h + 1, :],
                                  b_ref[:, h:h + 1], s_in)
            o_ref[:, hs] = o
            st[h] = s_out

    wide, col, row, stsp = _gdn_specs(NC, False)
    return pl.pallas_call(body, name=name, grid=(NC,), in_specs=[wide, wide, wide, col, row, col],
                          out_specs=[wide, stsp],
                          out_shape=[jax.ShapeDtypeStruct((S, D), F32), jax.ShapeDtypeStruct((NC, NH, DH, DH), F32)],
                          scratch_shapes=[pltpu.VMEM((NH, DH, DH), F32)],
                          compiler_params=_cparams(("arbitrary",)))(q, k, v, gcol, grow, bcol)


def _gdn_bwd(name, q, k, v, gcol, grow, bcol, ssave, do):
    S = q.shape[0]
    NC = S // CHUNK

    def body(q_ref, k_ref, v_ref, gc_ref, gr_ref, b_ref, ss_ref, do_ref, dq_ref, dk_ref, dv_ref, dgc_ref, dgr_ref, db_ref, dst):
        @pl.when(pl.program_id(0) == 0)
        def _():
            dst[...] = jnp.zeros(dst.shape, F32)
        for h in range(NH):
            hs = slice(h * DH, (h + 1) * DH)
            prim = (q_ref[:, hs], k_ref[:, hs], v_ref[:, hs], gc_ref[:, h:h + 1], gr_ref[0, h:h + 1, :], b_ref[:, h:h + 1],
                    ss_ref[0, h])
            _, vjp = jax.vjp(_gdn_chunk, *prim)
            dq, dk, dv, dgc, dgr, db, ds = vjp((do_ref[:, hs], dst[h]))
            dq_ref[:, hs] = dq
            dk_ref[:, hs] = dk
            dv_ref[:, hs] = dv
            dgc_ref[:, h:h + 1] = dgc
            dgr_ref[0, h:h + 1, :] = dgr
            db_ref[:, h:h + 1] = db
            dst[h] = ds

    wide, col, row, stsp = _gdn_specs(NC, True)
    return pl.pallas_call(body, name=name, grid=(NC,), in_specs=[wide, wide, wide, col, row, col, stsp, wide],
                          out_specs=[wide, wide, wide, col, row, col],
                          out_shape=[jax.ShapeDtypeStruct((S, D), F32)] * 3 + [jax.ShapeDtypeStruct((S, NH), F32),
                                                                                 jax.ShapeDtypeStruct((NC, NH, CHUNK), F32),
                                                                                 jax.ShapeDtypeStruct((S, NH), F32)],
                          scratch_shapes=[pltpu.VMEM((NH, DH, DH), F32)],
                          compiler_params=_cparams(("arbitrary",)))(q, k, v, gcol, grow, bcol, ssave, do)


TQ = 128
SM_SCALE = QKH ** -0.5
NEG = -1e30


def _diag_mask(transposed):
    r = lax.broadcasted_iota(jnp.int32, (TQ, TQ), 0) // CHUNK
    c = lax.broadcasted_iota(jnp.int32, (TQ, TQ), 1) // CHUNK
    return (r <= c) if transposed else (c <= r)


def _dot_nt(a, b):
    return lax.dot_general(a, b, (((1,), (1,)), ((), ())), preferred_element_type=F32)


def _flash_fwd(name, qp, kp, kv):
    S = qp.shape[0]
    nq = S // TQ

    def body(q_ref, k_ref, v_ref, o_ref, lse_ref):
        qi = pl.program_id(1)
        q = q_ref[...]

        def step(j, carry, masked):
            m, l, acc = carry
            rows = pl.ds(pl.multiple_of(j * TQ, TQ), TQ)
            s = _dot_nt(q, k_ref[rows, :]) * SM_SCALE
            if masked:
                s = jnp.where(_diag_mask(False), s, NEG)
            m_new = jnp.maximum(m, jnp.max(s, axis=-1, keepdims=True))
            p = jnp.exp(s - m_new)
            alpha = jnp.exp(m - m_new)
            l = alpha * l + jnp.sum(p, axis=-1, keepdims=True)
            acc = alpha * acc + jnp.dot(p.astype(BF16), v_ref[rows, :].astype(BF16), preferred_element_type=F32)
            return m_new, l, acc

        carry = (jnp.full((TQ, 1), NEG, F32), jnp.zeros((TQ, 1), F32), jnp.zeros((TQ, DH), F32))
        carry = lax.fori_loop(0, qi, lambda j, c: step(j, c, False), carry)
        m, l, acc = step(qi, carry, True)
        o_ref[...] = acc / l
        lse_ref[0] = m + jnp.log(l)

    return pl.pallas_call(
        body, name=name, grid=(NH, nq),
        in_specs=[pl.BlockSpec((TQ, HP), lambda h, i: (i, h)), pl.BlockSpec((S, HP), lambda h, i: (0, h)),
                  pl.BlockSpec((S, DH), lambda h, i: (0, NH + h))],
        out_specs=[pl.BlockSpec((TQ, DH), lambda h, i: (i, h)), pl.BlockSpec((1, TQ, 1), lambda h, i: (h, i, 0))],
        out_shape=[jax.ShapeDtypeStruct((S, NH * DH), F32), jax.ShapeDtypeStruct((NH, S, 1), F32)],
        compiler_params=_cparams(("parallel", "arbitrary")))(qp, kp, kv)


def _flash_bwd_dq(name, qp, kp, kv, o, do, lse):
    S = qp.shape[0]
    nq = S // TQ

    def body(q_ref, k_ref, v_ref, o_ref, do_ref, lse_ref, dq_ref, dl_ref):
        qi = pl.program_id(1)
        q = q_ref[...]
        do = do_ref[...]
        delta = jnp.sum(o_ref[...] * do, axis=-1, keepdims=True)
        dl_ref[0] = delta
        dob = do.astype(BF16)
        lse = lse_ref[0]

        def step(j, dq, masked):
            rows = pl.ds(pl.multiple_of(j * TQ, TQ), TQ)
            k = k_ref[rows, :]
            s = _dot_nt(q, k) * SM_SCALE
            if masked:
                s = jnp.where(_diag_mask(False), s, NEG)
            p = jnp.exp(s - lse)
            dp = _dot_nt(dob, v_ref[rows, :].astype(BF16))
            ds = p * (dp - delta) * SM_SCALE
            return dq + jnp.dot(ds.astype(BF16), k, preferred_element_type=F32)

        dq = lax.fori_loop(0, qi, lambda j, c: step(j, c, False), jnp.zeros((TQ, HP), F32))
        dq_ref[...] = step(qi, dq, True)

    return pl.pallas_call(
        body, name=name, grid=(NH, nq),
        in_specs=[pl.BlockSpec((TQ, HP), lambda h, i: (i, h)), pl.BlockSpec((S, HP), lambda h, i: (0, h)),
                  pl.BlockSpec((S, DH), lambda h, i: (0, NH + h)), pl.BlockSpec((TQ, DH), lambda h, i: (i, h)),
                  pl.BlockSpec((TQ, DH), lambda h, i: (i, h)), pl.BlockSpec((1, TQ, 1), lambda h, i: (h, i, 0))],
        out_specs=[pl.BlockSpec((TQ, HP), lambda h, i: (i, h)), pl.BlockSpec((1, TQ, 1), lambda h, i: (h, i, 0))],
        out_shape=[jax.ShapeDtypeStruct((S, NH * HP), F32), jax.ShapeDtypeStruct((NH, S, 1), F32)],
        compiler_params=_cparams(("parallel", "arbitrary")))(qp, kp, kv, o, do, lse)


def _flash_bwd_dkv(name, qp, kp, kv, do, lse_row, delta_row):
    S = qp.shape[0]
    nq = S // TQ

    def body(q_ref, k_ref, v_ref, do_ref, lse_ref, dl_ref, dk_ref, dv_ref):
        kj = pl.program_id(1)
        k = k_ref[...]
        vb = v_ref[...].astype(BF16)

        def step(i, carry, masked):
            dk, dv = carry
            rows = pl.ds(pl.multiple_of(i * TQ, TQ), TQ)
            q = q_ref[rows, :]
            dob = do_ref[rows, :].astype(BF16)
            st = _dot_nt(k, q) * SM_SCALE
            pt = jnp.exp(st - lse_ref[0, :, rows])
            if masked:
                pt = jnp.where(_diag_mask(True), pt, 0.0)
            dv = dv + jnp.dot(pt.astype(BF16), dob, preferred_element_type=F32)
            dpt = _dot_nt(vb, dob)
            dst = pt * (dpt - dl_ref[0, :, rows]) * SM_SCALE
            dk = dk + jnp.dot(dst.astype(BF16), q, preferred_element_type=F32)
            return dk, dv

        carry = step(kj, (jnp.zeros((TQ, HP), F32), jnp.zeros((TQ, DH), F32)), True)
        dk, dv = lax.fori_loop(kj + 1, nq, lambda i, c: step(i, c, False), carry)
        dk_ref[...] = dk
        dv_ref[...] = dv

    return pl.pallas_call(
        body, name=name, grid=(NH, nq),
        in_specs=[pl.BlockSpec((S, HP), lambda h, j: (0, h)), pl.BlockSpec((TQ, HP), lambda h, j: (j, h)),
                  pl.BlockSpec((TQ, DH), lambda h, j: (j, NH + h)), pl.BlockSpec((S, DH), lambda h, j: (0, h)),
                  pl.BlockSpec((1, 1, S), lambda h, j: (h, 0, 0)), pl.BlockSpec((1, 1, S), lambda h, j: (h, 0, 0))],
        out_specs=[pl.BlockSpec((TQ, HP), lambda h, j: (j, h)), pl.BlockSpec((TQ, DH), lambda h, j: (j, h))],
        out_shape=[jax.ShapeDtypeStruct((S, NH * HP), F32), jax.ShapeDtypeStruct((S, NH * DH), F32)],
        compiler_params=_cparams(("parallel", "arbitrary")))(qp, kp, kv, do, lse_row, delta_row)


def _tm(S, width):
    t = 512 if width <= 1024 else (256 if width <= 3072 else 128)
    return min(t, S)


def _mod_fwd(tag, x, g, shift, scale):
    S = x.shape[0]
    return _rw_fwd(tag + "_mod", f_mod, [Row(x)], [g, shift, scale], [((D,), BF16)], _tm(S, D))[0]


def _mod_bwd(tag, x, g, shift, scale, dh, dx_direct):
    S = x.shape[0]
    r = _rw_bwd(tag + "_mod_b", f_mod, [Row(x)], [g, shift, scale], [Row(dh)], [True], [True] * 3, [((D,), F32)],
                _tm(S, D), add=Row(dx_direct))
    return r[0], r[1:]


def _res_fwd(tag, x, y, gate, coef):
    S = x.shape[0]

    def fn(pieces, bvals):
        return [pieces[0] + coef * bvals[0] * pieces[1]], []
    return _rowwise(tag + "_res", fn, [Row(x), Row(y)], [gate], [((D,), F32)], [], _tm(S, D))[0]


def _res_bwd(tag, y, gate, dxn, coef):
    S = y.shape[0]
    r = _rw_bwd(tag + "_res_b", make_f_res(coef), [Row(y)], [gate], [Row(dxn)], [True], [True], [((D,), BF16)], _tm(S, D))
    return r[0], r[1]


def _ffn_fwd(tag, x, mod3, g, w_in, w_out):
    shift, scale, gate = mod3
    S = x.shape[0]
    h = _mod_fwd(tag, x, g, shift, scale)
    gu = _matmul(tag + "_in", h, w_in)
    a = _rw_fwd(tag + "_act", f_act, [Row(gu, splits=[FF, FF])], [], [((FF,), BF16)], _tm(S, 2 * FF))[0]
    y = _matmul(tag + "_out", a, w_out)
    xn = _res_fwd(tag, x, y, gate, 0.5)
    return xn, (x, h, gu, a, y)


def _ffn_bwd(tag, dxn, res, mod3, g, w_in, w_out):
    shift, scale, gate = mod3
    x, h, gu, a, y = res
    S = x.shape[0]
    dy, dgate = _res_bwd(tag, y, gate, dxn, 0.5)
    da = _matmul(tag + "_out_bi", dy, w_out, "nt")
    dw_out = _matmul(tag + "_out_bw", a, dy, "tn")
    dgu = _rw_bwd(tag + "_act_b", f_act, [Row(gu, splits=[FF, FF])], [], [Row(da)], [True, True], [],
                  [((FF, FF), BF16)], _tm(S, 2 * FF))[0]
    dh = _matmul(tag + "_in_bi", dgu, w_in, "nt")
    dw_in = _matmul(tag + "_in_bw", h, dgu, "tn")
    dx, (dg, dshift, dscale) = _mod_bwd(tag, x, g, shift, scale, dh, dxn)
    return dx, dict(w_in=dw_in, w_out=dw_out, g=dg, mod=(dshift, dscale, dgate))


def _pad_lanes(a, lo, width=LANE):
    return jnp.pad(a, ((0, 0), (lo, width - lo - a.shape[1])))


def _gdn_layer_fwd(tag, x, mod3, g, p):
    shift, scale, gate = mod3
    S = x.shape[0]
    NC = S // CHUNK
    h = _mod_fwd(tag, x, g, shift, scale)
    proj = _matmul(tag + "_in", h, p["w_in"])
    qc = _conv_fwd(tag + "_conv", proj, p["conv_w8"], 3 * D, _tm(S, 3 * D))
    q, k, v = _rw_fwd(tag + "_pre", f_gdnpre, [Row(qc, splits=[DH] * (3 * NH))], [],
                      [((DH,) * NH, F32)] * 3, _tm(S, 3 * D))
    betaf, gf = _rw_fwd(tag + "_gates", f_gates, [Row(proj, LANE, cb=GATE_CB)], [p["a_log128"], p["dt_bias128"]],
                        [((LANE,), F32)] * 2, _tm(S, LANE))
    bcol, gcol = betaf[:, :NH], gf[:, NH:2 * NH]
    grow = gcol.reshape(NC, CHUNK, NH).transpose(0, 2, 1)
    o, ssave = _gdn_fwd(tag + "_core", q, k, v, gcol, grow, bcol)
    on = _rw_fwd(tag + "_post", f_gdnpost, [Row(o, splits=[DH] * NH), Row(proj, D, cb=3, splits=[DH] * NH)],
                 [p["norm_g"]], [((DH,) * NH, BF16)], _tm(S, 2 * D))[0]
    y = _matmul(tag + "_out", on, p["w_out"])
    xn = _res_fwd(tag, x, y, gate, 1.0)
    return xn, (x, h, proj, qc, q, k, v, gcol, grow, bcol, ssave, o, on, y)


def _gdn_layer_bwd(tag, dxn, res, mod3, g, p):
    shift, scale, gate = mod3
    x, h, proj, qc, q, k, v, gcol, grow, bcol, ssave, o, on, y = res
    S = x.shape[0]
    dy, dgate = _res_bwd(tag, y, gate, dxn, 1.0)
    don = _matmul(tag + "_out_bi", dy, p["w_out"], "nt")
    dw_out = _matmul(tag + "_out_bw", on, dy, "tn")
    do, dz, dnorm = _rw_bwd(tag + "_post_b", f_gdnpost, [Row(o, splits=[DH] * NH), Row(proj, D, cb=3, splits=[DH] * NH)],
                            [p["norm_g"]], [Row(don, splits=[DH] * NH)], [True] * (2 * NH), [True],
                            [((DH,) * NH, F32), ((DH,) * NH, BF16)], _tm(S, 2 * D))
    dq, dk, dv, dgc, dgr, db = _gdn_bwd(tag + "_core_b", q, k, v, gcol, grow, bcol, ssave, do)
    dgcol = dgc + dgr.transpose(0, 2, 1).reshape(S, NH)
    dgates, da_log, ddt = _rw_bwd(tag + "_gates_b", f_gates, [Row(proj, LANE, cb=GATE_CB)], [p["a_log128"], p["dt_bias128"]],
                                  [Row(_pad_lanes(db, 0)), Row(_pad_lanes(dgcol, NH))], [True], [True, True],
                                  [((LANE,), BF16)], _tm(S, LANE))
    dqc = _rw_bwd(tag + "_pre_b", f_gdnpre, [Row(qc, splits=[DH] * (3 * NH))], [],
                  [Row(dq, splits=[DH] * NH), Row(dk, splits=[DH] * NH), Row(dv, splits=[DH] * NH)],
                  [True] * (3 * NH), [], [((DH,) * (3 * NH), F32)], _tm(S, 3 * D))[0]
    dqkv, dconv = _conv_bwd(tag + "_conv_b", proj, dqc, p["conv_w8"], 3 * D, _tm(S, 3 * D), BF16)
    dproj = jnp.concatenate([dqkv, dz, dgates], axis=1)
    dh = _matmul(tag + "_in_bi", dproj, p["w_in"], "nt")
    dw_in = _matmul(tag + "_in_bw", h, dproj, "tn")
    dx, (dg, dshift, dscale) = _mod_bwd(tag, x, g, shift, scale, dh, dxn)
    return dx, dict(w_in=dw_in, conv_w8=dconv, a_log128=da_log, dt_bias128=ddt, norm_g=dnorm,
                    w_out=dw_out, g=dg, mod=(dshift, dscale, dgate))


def _qk_rows(src, shared_rope, ckv=None):
    if shared_rope:
        return [Row(src, D, cb=0, splits=[DH] * NH), Row(ckv, LANE, cb=2)]
    return [Row(src, splits=[DH] * (2 * NH))]


def _kv_fwd(x, kvmod, p, tabs):
    shift, scale = kvmod
    S = x.shape[0]
    h = _mod_fwd("kv", x, p["kv_norm_g"], shift, scale)
    ckv = _matmul("kv_dkv", h, p["w_dkv"])
    lat = _rw_fwd("kv_lat", f_rms, [Row(ckv, KVL)], [p["kv_lat_g"]], [((KVL,), BF16)], _tm(S, KVL))[0]
    kvf = _matmul("kv_ukv", lat, p["w_ukv"])
    kp = _rw_fwd("kv_k", make_f_qk(True), _qk_rows(kvf, True, ckv) + [Row(tabs[0]), Row(tabs[1])],
                 [p["k_gn"], p["k_gr"], p["pm"]], [((DH,) * (2 * NH), BF16)], _tm(S, 2 * D))[0]
    return kp, kvf, (x, h, ckv, lat)


def _kv_bwd(dkp, dv, dx_direct, res, kvmod, kvf, p, tabs):
    shift, scale = kvmod
    x, h, ckv, lat = res
    S = x.shape[0]
    dkn, dkr, dgn, dgr = _rw_bwd("kv_k_b", make_f_qk(True), _qk_rows(kvf, True, ckv) + [Row(tabs[0]), Row(tabs[1])],
                                 [p["k_gn"], p["k_gr"], p["pm"]], [Row(dkp, splits=[DH] * (2 * NH))],
                                 [True] * (NH + 1) + [False, False], [True, True, False],
                                 [((DH,) * NH, BF16), ((LANE,), BF16)], _tm(S, 2 * D))
    dkvf = jnp.concatenate([dkn, dv.astype(BF16)], axis=1)
    dlat = _matmul("kv_ukv_bi", dkvf, p["w_ukv"], "nt")
    dw_ukv = _matmul("kv_ukv_bw", lat, dkvf, "tn")
    dcl, dlg = _rw_bwd("kv_lat_b", f_rms, [Row(ckv, KVL)], [p["kv_lat_g"]], [Row(dlat)], [True], [True],
                       [((KVL,), BF16)], _tm(S, KVL))
    dckv = jnp.concatenate([dcl, dkr], axis=1)
    dh = _matmul("kv_dkv_bi", dckv, p["w_dkv"], "nt")
    dw_dkv = _matmul("kv_dkv_bw", h, dckv, "tn")
    dx, (dg, dshift, dscale) = _mod_bwd("kv", x, p["kv_norm_g"], shift, scale, dh, dx_direct)
    return dx, dict(w_dkv=dw_dkv, w_ukv=dw_ukv, kv_lat_g=dlg, k_gn=dgn, k_gr=dgr, kv_norm_g=dg, mod=(dshift, dscale))


def _mla_layer_fwd(tag, x, mod3, g, p, kp, kvf, tabs):
    shift, scale, gate = mod3
    S = x.shape[0]
    h = _mod_fwd(tag, x, g, shift, scale)
    ql = _matmul(tag + "_dq", h, p["w_dq"])
    qln = _rw_fwd(tag + "_qln", f_rms, [Row(ql)], [p["ql_g"]], [((QL,), BF16)], _tm(S, QL))[0]
    qu = _matmul(tag + "_uq", qln, p["w_uq"])
    qp = _rw_fwd(tag + "_q", make_f_qk(False), _qk_rows(qu, False) + [Row(tabs[0]), Row(tabs[1])],
                 [p["q_gn"], p["q_gr"], p["pm"]], [((DH,) * (2 * NH), BF16)], _tm(S, 2 * D))[0]
    o, lse = _flash_fwd(tag + "_att", qp, kp, kvf)
    y = _matmul(tag + "_out", o, p["w_out"])
    xn = _res_fwd(tag, x, y, gate, 1.0)
    return xn, (x, h, ql, qln, qu, qp, o, lse, y)


def _mla_layer_bwd(tag, dxn, res, mod3, g, p, kp, kvf, tabs):
    shift, scale, gate = mod3
    x, h, ql, qln, qu, qp, o, lse, y = res
    S = x.shape[0]
    dy, dgate = _res_bwd(tag, y, gate, dxn, 1.0)
    do = _matmul(tag + "_out_bi", dy, p["w_out"], "nt")
    dw_out = _matmul(tag + "_out_bw", o, dy, "tn")
    dqp, delta = _flash_bwd_dq(tag + "_att_bq", qp, kp, kvf, o, do, lse)
    dkp, dv = _flash_bwd_dkv(tag + "_att_bkv", qp, kp, kvf, do, lse.reshape(NH, 1, S), delta.reshape(NH, 1, S))
    dqu, dgn, dgr = _rw_bwd(tag + "_q_b", make_f_qk(False), _qk_rows(qu, False) + [Row(tabs[0]), Row(tabs[1])],
                            [p["q_gn"], p["q_gr"], p["pm"]], [Row(dqp, splits=[DH] * (2 * NH))],
                            [True] * (2 * NH) + [False, False], [True, True, False],
                            [((DH,) * (2 * NH), BF16)], _tm(S, 2 * D))
    dqln = _matmul(tag + "_uq_bi", dqu, p["w_uq"], "nt")
    dw_uq = _matmul(tag + "_uq_bw", qln, dqu, "tn")
    dql, dqlg = _rw_bwd(tag + "_qln_b", f_rms, [Row(ql)], [p["ql_g"]], [Row(dqln)], [True], [True], [((QL,), BF16)],
                        _tm(S, QL))
    dh = _matmul(tag + "_dq_bi", dql, p["w_dq"], "nt")
    dw_dq = _matmul(tag + "_dq_bw", h, dql, "tn")
    dx, (dg, dshift, dscale) = _mod_bwd(tag, x, g, shift, scale, dh, dxn)
    return dx, dkp, dv, dict(w_dq=dw_dq, w_uq=dw_uq, w_out=dw_out, ql_g=dqlg, q_gn=dgn, q_gr=dgr, g=dg,
                             mod=(dshift, dscale, dgate))


def _loss_head(y, tgt):
    S = y.shape[0]

    def fn(pieces, bvals):
        e = pieces[0] - pieces[1]
        part = jnp.sum(e * e) * (0.5 / D)
        return [e * (1.0 / D)], [jnp.full((1, LANE), part, F32)]
    dy, part = _rowwise("loss", fn, [Row(y), Row(tgt)], [], [((D,), F32)], [(1, LANE)], _tm(S, D))
    return part[0, 0], dy


def _rope_tables(positions):
    S = positions.shape[0]
    half = ROPE // 2
    lane = lax.broadcasted_iota(jnp.int32, (1, LANE), 1)
    inv_freq = ROPE_BASE ** (-(lane % half).astype(F32) / half)
    live = (lane < ROPE).astype(F32)
    sign = jnp.where(lane < half, -1.0, 1.0) * live

    def fn(pieces, bvals):
        ang = pieces[0] * bvals[0]
        return [jnp.cos(ang) * bvals[1], jnp.sin(ang) * bvals[2]], []
    pos = jnp.broadcast_to(positions.astype(F32)[:, None], (S, LANE))
    cosp, sins = _rowwise("rope_tab", fn, [Row(pos)], [inv_freq, live, sign], [((LANE,), F32)] * 2, [], _tm(S, LANE))
    r = lax.broadcasted_iota(jnp.int32, (LANE, LANE), 0)
    c = lax.broadcasted_iota(jnp.int32, (LANE, LANE), 1)
    pm = (((c < half) & (r == c + half)) | ((c >= half) & (c < ROPE) & (r == c - half))).astype(F32)
    return (cosp, sins), pm


def _adamw(name, w, g, m, v):
    shape = w.shape
    C = shape[-1]
    R = w.size // C
    tr = R
    for t in (1024, 512, 256, 128, 64, 32, 16, 8):
        if R % t == 0 and t * C * 4 <= (1 << 21):
            tr = t
            break
    c1 = 1.0 - ADAM_B1 ** ADAM_STEP
    c2 = 1.0 - ADAM_B2 ** ADAM_STEP

    def body(w_ref, g_ref, m_ref, v_ref, d_ref, mo_ref, vo_ref):
        gg = g_ref[...]
        mn = ADAM_B1 * m_ref[...] + (1.0 - ADAM_B1) * gg
        vn = ADAM_B2 * v_ref[...] + (1.0 - ADAM_B2) * (gg * gg)
        d_ref[...] = -ADAM_LR * ((mn / c1) / (jnp.sqrt(vn / c2) + ADAM_EPS) + ADAM_WD * w_ref[...])
        mo_ref[...] = mn
        vo_ref[...] = vn

    spec = pl.BlockSpec((tr, C), lambda i: (i, 0))
    outs = pl.pallas_call(body, name=name, grid=(R // tr,), in_specs=[spec] * 4, out_specs=[spec] * 3,
                          out_shape=[jax.ShapeDtypeStruct((R, C), F32)] * 3,
                          compiler_params=_cparams(("parallel",)))(*[t.reshape(R, C) for t in (w, g, m, v)])
    return [o.reshape(shape) for o in outs]


HBM_SPEC = pl.BlockSpec(memory_space=pltpu.HBM)
OTHER_CHIPS = (4, 2, 6)
SIBLING = 1


def _me():
    return lax.axis_index("x"), lax.axis_index("y"), lax.axis_index("c")


def _peer(me, k):
    mx, my, mc = me
    return ((1 - mx) if k & 4 else mx, (1 - my) if k & 2 else my, (1 - mc) if k & 1 else mc)


def _rcopy(src, dst, ssem, rsem, to):
    return pltpu.make_async_remote_copy(src_ref=src, dst_ref=dst, send_sem=ssem, recv_sem=rsem, device_id=to,
                                        device_id_type=MESH)


def _all_gather8(name, x):
    def body(x_ref, o_ref, ssem, rsem, lsem):
        me = _me()
        mine = 4 * me[0] + 2 * me[1] + me[2]
        loc = pltpu.make_async_copy(x_ref, o_ref.at[mine], lsem)
        loc.start()
        sends = []
        for k in range(1, 8):
            cp = _rcopy(x_ref, o_ref.at[mine], ssem.at[k - 1], rsem.at[k - 1], _peer(me, k))
            cp.start()
            sends.append(cp)
        for k in range(1, 8):
            px, py, pc = _peer(me, k)
            _rcopy(x_ref, o_ref.at[4 * px + 2 * py + pc], ssem.at[k - 1], rsem.at[k - 1], (px, py, pc)).wait_recv()
        for cp in sends:
            cp.wait_send()
        loc.wait()

    return pl.pallas_call(body, name=name, out_shape=jax.ShapeDtypeStruct((8,) + x.shape, x.dtype),
                          in_specs=[HBM_SPEC], out_specs=HBM_SPEC,
                          scratch_shapes=[pltpu.SemaphoreType.DMA((7,)), pltpu.SemaphoreType.DMA((7,)),
                                          pltpu.SemaphoreType.DMA(())])(x)


def _gather_weights(name, wp):
    def body(w_ref, o_ref, ssem, rsem, lsem):
        me = _me()
        mc = me[2]
        loc = pltpu.make_async_copy(w_ref, o_ref.at[2 * me[0] + me[1]], lsem)
        loc.start()
        first = []
        for j, k in enumerate(OTHER_CHIPS):
            cp = _rcopy(w_ref.at[mc], o_ref.at[2 * me[0] + me[1], mc], ssem.at[j], rsem.at[j], _peer(me, k))
            cp.start()
            first.append(cp)
        passed = []
        for j, k in enumerate(OTHER_CHIPS):
            px, py, _ = _peer(me, k)
            land = o_ref.at[2 * px + py, mc]
            _rcopy(w_ref.at[mc], land, ssem.at[j], rsem.at[j], _peer(me, k)).wait_recv()
            fw = _rcopy(land, land, ssem.at[3 + j], rsem.at[3 + j], _peer(me, SIBLING))
            fw.start()
            passed.append(fw)
        for j, k in enumerate(OTHER_CHIPS):
            px, py, _ = _peer(me, k)
            land = o_ref.at[2 * px + py, 1 - mc]
            _rcopy(land, land, ssem.at[3 + j], rsem.at[3 + j], _peer(me, SIBLING)).wait_recv()
        for cp in first + passed:
            cp.wait_send()
        loc.wait()

    return pl.pallas_call(body, name=name, out_shape=jax.ShapeDtypeStruct((4,) + wp.shape, wp.dtype),
                          in_specs=[HBM_SPEC], out_specs=HBM_SPEC,
                          scratch_shapes=[pltpu.SemaphoreType.DMA((6,)), pltpu.SemaphoreType.DMA((6,)),
                                          pltpu.SemaphoreType.DMA(())])(wp)


def _exchange_half(name, g):
    def body(g_ref, p_ref, ssem, rsem):
        me = _me()
        cps = []
        for s in range(4):
            cp = _rcopy(g_ref.at[s, 1 - me[2]], p_ref.at[s], ssem.at[s], rsem.at[s], _peer(me, SIBLING))
            cp.start()
            cps.append(cp)
        for cp in cps:
            cp.wait()

    return pl.pallas_call(body, name=name, out_shape=jax.ShapeDtypeStruct((4,) + g.shape[2:], g.dtype),
                          in_specs=[HBM_SPEC], out_specs=HBM_SPEC,
                          scratch_shapes=[pltpu.SemaphoreType.DMA((4,)), pltpu.SemaphoreType.DMA((4,))])(g)


def _scatter_chips(name, q):
    def body(q_ref, t_ref, ssem, rsem):
        me = _me()
        cps = []
        for j, k in enumerate(OTHER_CHIPS):
            px, py, _ = _peer(me, k)
            cp = _rcopy(q_ref.at[2 * px + py], t_ref.at[j], ssem.at[j], rsem.at[j], _peer(me, k))
            cp.start()
            cps.append(cp)
        for cp in cps:
            cp.wait()

    return pl.pallas_call(body, name=name, out_shape=jax.ShapeDtypeStruct((3,) + q.shape[1:], q.dtype),
                          in_specs=[HBM_SPEC], out_specs=HBM_SPEC,
                          scratch_shapes=[pltpu.SemaphoreType.DMA((3,)), pltpu.SemaphoreType.DMA((3,))])(q)


def _exchange_full(name, r):
    def body(r_ref, o_ref, ssem, rsem, lsem):
        me = _me()
        mc = me[2]
        loc = pltpu.make_async_copy(r_ref, o_ref.at[mc], lsem)
        loc.start()
        cp = _rcopy(r_ref, o_ref.at[mc], ssem, rsem, _peer(me, SIBLING))
        cp.start()
        _rcopy(r_ref, o_ref.at[1 - mc], ssem, rsem, _peer(me, SIBLING)).wait_recv()
        cp.wait_send()
        loc.wait()

    return pl.pallas_call(body, name=name, out_shape=jax.ShapeDtypeStruct((2,) + r.shape, r.dtype),
                          in_specs=[HBM_SPEC], out_specs=HBM_SPEC,
                          scratch_shapes=[pltpu.SemaphoreType.DMA(()), pltpu.SemaphoreType.DMA(()),
                                          pltpu.SemaphoreType.DMA(())])(r)


PACK_L = 1024
PACK_RT = 512


def _add_half(name, g, p, c):
    rh = g.shape[2]

    def body(c_ref, g_ref, p_ref, o_ref):
        o_ref[0] = g_ref[0, 0] + p_ref[0]

    gs = pltpu.PrefetchScalarGridSpec(
        num_scalar_prefetch=1, grid=(4, rh // PACK_RT),
        in_specs=[pl.BlockSpec((1, 1, PACK_RT, PACK_L), lambda s, i, c_ref: (s, c_ref[0], i, 0)),
                  pl.BlockSpec((1, PACK_RT, PACK_L), lambda s, i, c_ref: (s, i, 0))],
        out_specs=pl.BlockSpec((1, PACK_RT, PACK_L), lambda s, i, c_ref: (s, i, 0)))
    return pl.pallas_call(body, name=name, grid_spec=gs, out_shape=jax.ShapeDtypeStruct((4, rh, PACK_L), F32),
                          compiler_params=_cparams(("parallel", "parallel")))(c.reshape(1).astype(jnp.int32), g, p)


def _add_chips(name, q, t, s):
    rh = q.shape[1]

    def body(s_ref, q_ref, t_ref, o_ref):
        o_ref[...] = ((q_ref[0] + t_ref[0]) + t_ref[1]) + t_ref[2]

    gs = pltpu.PrefetchScalarGridSpec(
        num_scalar_prefetch=1, grid=(rh // PACK_RT,),
        in_specs=[pl.BlockSpec((1, PACK_RT, PACK_L), lambda i, s_ref: (s_ref[0], i, 0)),
                  pl.BlockSpec((3, PACK_RT, PACK_L), lambda i, s_ref: (0, i, 0))],
        out_specs=pl.BlockSpec((PACK_RT, PACK_L), lambda i, s_ref: (i, 0)))
    return pl.pallas_call(body, name=name, grid_spec=gs, out_shape=jax.ShapeDtypeStruct((rh, PACK_L), F32),
                          compiler_params=_cparams(("parallel",)))(s.reshape(1).astype(jnp.int32), q, t)


def _sum8(name, a):
    def body(a_ref, o_ref):
        acc = a_ref[0]
        for d in range(1, 8):
            acc = acc + a_ref[d]
        o_ref[...] = acc
    return pl.pallas_call(body, name=name, out_shape=jax.ShapeDtypeStruct(a.shape[1:], F32))(a)


def _silu_rows(name, a):
    def body(a_ref, o_ref):
        o_ref[...] = _silu(a_ref[...])
    return pl.pallas_call(body, name=name, out_shape=jax.ShapeDtypeStruct(a.shape, F32))(a)


BIG = (("ffn_w_in", 3), ("ffn_w_out", 2), ("gdn_w_in", 2), ("gdn_w_out", 1), ("mla_w_dkv", 0), ("mla_w_ukv", 1),
       ("mla_w_dq", 1), ("mla_w_uq", 2), ("mla_w_out", 1))


def _packed_rows(n):
    per_half = -(-n // (2 * PACK_L))
    return -(-per_half // PACK_RT) * PACK_RT


def _pack_flat(flat):
    n = flat.shape[-1]
    rh = _packed_rows(n)
    pad = [(0, 0)] * (flat.ndim - 1) + [(0, 2 * rh * PACK_L - n)]
    return jnp.pad(flat, pad).reshape(flat.shape[:-1] + (2, rh, PACK_L))


def _shards_first(full, axis):
    sh = full.shape
    t = full.reshape(sh[:axis] + (4, sh[axis] // 4) + sh[axis + 1:])
    return jnp.moveaxis(t, axis, 0)


def _shards_merge(stacked, axis):
    t = jnp.moveaxis(stacked, 0, axis)
    sh = t.shape
    return t.reshape(sh[:axis] + (4 * sh[axis + 1],) + sh[axis + 2:])


def _pack_small(parts):
    flat = jnp.concatenate([p.reshape(-1).astype(F32) for p in parts])
    n = flat.shape[0]
    rows = -(-n // (SUB * LANE)) * SUB
    return jnp.pad(flat, (0, rows * LANE - n)).reshape(rows, LANE)


def _unpack_small(buf, shapes):
    lead = buf.shape[:-2]
    flat = buf.reshape(lead + (-1,))
    out, off = [], 0
    for sh in shapes:
        n = 1
        for d in sh:
            n *= d
        out.append(flat[..., off:off + n].reshape(lead + tuple(sh)))
        off += n
    return out


WEIGHTS = ('ada_w', 'ada_b', 'norm_g', 'ffn_w_in', 'ffn_w_out', 'gdn_w_in', 'gdn_conv_w', 'gdn_a_log', 'gdn_dt_bias',
           'gdn_norm_g', 'gdn_w_out', 'kv_ada_w', 'kv_ada_b', 'kv_norm_g', 'mla_w_dkv', 'mla_kv_norm_g', 'mla_w_ukv',
           'mla_k_norm_g', 'mla_w_dq', 'mla_q_lora_norm_g', 'mla_w_uq', 'mla_q_norm_g', 'mla_w_out')
ARGS = ('x', 'c', 'positions') + WEIGHTS + ('loss_target',) + tuple('m_' + n for n in WEIGHTS) + tuple('v_' + n for n in WEIGHTS)


def _split_norm(v):
    return v[None, :DH], _pad_lanes(v[None, DH:], 0)


def _join_norm(gn, gr):
    return jnp.concatenate([gn[0], gr[0, :ROPE]])


def _step(x, tgt, pos, mods, kvmod, W, P):
    tabs, pm = _rope_tables(pos)
    m3 = lambda l, i: tuple(mods[l][3 * i + j][None] for j in range(3))
    ng = lambda l, i: P["norm_g"][l, i][None]
    gdn_p, mla_p = [], []
    for l in range(2):
        gdn_p.append(dict(w_in=jnp.pad(W["gdn_w_in"][l], ((0, 0), (0, GDN_IN - W["gdn_w_in"].shape[2]))),
                          conv_w8=jnp.pad(P["gdn_conv_w"][l], ((0, 4), (0, 0))),
                          a_log128=_pad_lanes(P["gdn_a_log"][l][None], NH), dt_bias128=_pad_lanes(P["gdn_dt_bias"][l][None], NH),
                          norm_g=P["gdn_norm_g"][l][None], w_out=W["gdn_w_out"][l]))
        q_gn, q_gr = _split_norm(P["mla_q_norm_g"][l])
        mla_p.append(dict(w_dq=W["mla_w_dq"][l], ql_g=P["mla_q_lora_norm_g"][l][None],
                          w_uq=jnp.pad(W["mla_w_uq"][l].reshape(QL, NH, QKH), ((0, 0), (0, 0), (0, HP - QKH))).reshape(QL, NH * HP),
                          q_gn=q_gn, q_gr=q_gr, pm=pm, w_out=W["mla_w_out"][l]))
    k_gn, k_gr = _split_norm(P["mla_k_norm_g"])
    kv_p = dict(kv_norm_g=P["kv_norm_g"][None], w_dkv=jnp.pad(W["mla_w_dkv"], ((0, 0), (0, QL - KVL - ROPE))),
                kv_lat_g=P["mla_kv_norm_g"][None],
                w_ukv=W["mla_w_ukv"].reshape(KVL, NH, 2, DH).transpose(0, 2, 1, 3).reshape(KVL, 2 * NH * DH),
                k_gn=k_gn, k_gr=k_gr, pm=pm)
    kvm = (kvmod[0][None], kvmod[1][None])

    res = {}
    for l in range(4):
        x, res[l, 0] = _ffn_fwd(f"l{l}a", x, m3(l, 0), ng(l, 0), W["ffn_w_in"][l, 0], W["ffn_w_out"][l, 0])
        if l < 2:
            x, res[l, 1] = _gdn_layer_fwd(f"l{l}g", x, m3(l, 1), ng(l, 1), gdn_p[l])
        else:
            x, res[l, 1] = _mla_layer_fwd(f"l{l}m", x, m3(l, 1), ng(l, 1), mla_p[l - 2], kp, kvf, tabs)
        x, res[l, 2] = _ffn_fwd(f"l{l}b", x, m3(l, 2), ng(l, 2), W["ffn_w_in"][l, 1], W["ffn_w_out"][l, 1])
        if l == 1:
            kp, kvf, kres = _kv_fwd(x, kvm, kv_p, tabs)
    loss, dx = _loss_head(x, tgt)

    gw = {n: [None] * W[n].shape[0] for n in ("gdn_w_in", "gdn_w_out", "mla_w_dq", "mla_w_uq", "mla_w_out")}
    gw["ffn_w_in"] = [[None, None] for _ in range(4)]
    gw["ffn_w_out"] = [[None, None] for _ in range(4)]
    gp = {n: [None] * 2 for n in ("gdn_conv_w", "gdn_a_log", "gdn_dt_bias", "gdn_norm_g", "mla_q_lora_norm_g", "mla_q_norm_g")}
    gnorm = [[None] * 3 for _ in range(4)]
    dmod = [[None] * NMOD for _ in range(4)]
    dkp = dv = None
    for l in (3, 2, 1, 0):
        if l == 1:
            dx, gk = _kv_bwd(dkp, dv, dx, kres, kvm, kvf, kv_p, tabs)
        for i in (2, 1, 0):
            if i != 1:
                dx, gd = _ffn_bwd(f"l{l}{'ab'[i // 2]}", dx, res[l, i], m3(l, i), ng(l, i), W["ffn_w_in"][l, i // 2],
                                  W["ffn_w_out"][l, i // 2])
                gw["ffn_w_in"][l][i // 2], gw["ffn_w_out"][l][i // 2] = gd["w_in"], gd["w_out"]
            elif l < 2:
                dx, gd = _gdn_layer_bwd(f"l{l}g", dx, res[l, 1], m3(l, 1), ng(l, 1), gdn_p[l])
                gw["gdn_w_in"][l] = gd["w_in"][:, :W["gdn_w_in"].shape[2]]
                gw["gdn_w_out"][l] = gd["w_out"]
                gp["gdn_conv_w"][l] = gd["conv_w8"][:4]
                gp["gdn_a_log"][l] = gd["a_log128"][0, NH:2 * NH]
                gp["gdn_dt_bias"][l] = gd["dt_bias128"][0, NH:2 * NH]
                gp["gdn_norm_g"][l] = gd["norm_g"][0]
            else:
                dx, dkp_l, dv_l, gd = _mla_layer_bwd(f"l{l}m", dx, res[l, 1], m3(l, 1), ng(l, 1), mla_p[l - 2], kp, kvf, tabs)
                dkp = dkp_l if dkp is None else dkp + dkp_l
                dv = dv_l if dv is None else dv + dv_l
                gw["mla_w_dq"][l - 2], gw["mla_w_out"][l - 2] = gd["w_dq"], gd["w_out"]
                gw["mla_w_uq"][l - 2] = gd["w_uq"].reshape(QL, NH, HP)[:, :, :QKH].reshape(QL, NH * QKH)
                gp["mla_q_lora_norm_g"][l - 2] = gd["ql_g"][0]
                gp["mla_q_norm_g"][l - 2] = _join_norm(gd["q_gn"], gd["q_gr"])
            gnorm[l][i] = gd["g"][0]
            for j in range(3):
                dmod[l][3 * i + j] = gd["mod"][j][0]
    gwf = {n: jnp.stack([jnp.stack(r) if isinstance(r, list) else r for r in v]) for n, v in gw.items()}
    gwf["mla_w_dkv"] = gk["w_dkv"][:, :KVL + ROPE]
    gwf["mla_w_ukv"] = gk["w_ukv"].reshape(KVL, 2, NH, DH).transpose(0, 2, 1, 3).reshape(KVL, 2 * NH * DH)
    gpf = {n: jnp.stack(v) for n, v in gp.items()}
    gpf["norm_g"] = jnp.stack([jnp.stack(r) for r in gnorm])
    gpf["kv_norm_g"] = gk["kv_norm_g"][0]
    gpf["mla_kv_norm_g"] = gk["kv_lat_g"][0]
    gpf["mla_k_norm_g"] = _join_norm(gk["k_gn"], gk["k_gr"])
    dmods = jnp.stack([jnp.stack(r) for r in dmod])
    dkvmod = jnp.stack([gk["mod"][0][0], gk["mod"][1][0]])
    return loss, dx, gwf, gpf, dmods, dkvmod


SMALL = ("norm_g", "gdn_conv_w", "gdn_a_log", "gdn_dt_bias", "gdn_norm_g", "kv_norm_g", "mla_kv_norm_g", "mla_k_norm_g",
         "mla_q_lora_norm_g", "mla_q_norm_g")


def kernel(x, c, positions, ada_w, ada_b, norm_g, ffn_w_in, ffn_w_out, gdn_w_in, gdn_conv_w, gdn_a_log, gdn_dt_bias,
           gdn_norm_g, gdn_w_out, kv_ada_w, kv_ada_b, kv_norm_g, mla_w_dkv, mla_kv_norm_g, mla_w_ukv, mla_k_norm_g,
           mla_w_dq, mla_q_lora_norm_g, mla_w_uq, mla_q_norm_g, mla_w_out, loss_target, m_ada_w, m_ada_b, m_norm_g,
           m_ffn_w_in, m_ffn_w_out, m_gdn_w_in, m_gdn_conv_w, m_gdn_a_log, m_gdn_dt_bias, m_gdn_norm_g, m_gdn_w_out,
           m_kv_ada_w, m_kv_ada_b, m_kv_norm_g, m_mla_w_dkv, m_mla_kv_norm_g, m_mla_w_ukv, m_mla_k_norm_g, m_mla_w_dq,
           m_mla_q_lora_norm_g, m_mla_w_uq, m_mla_q_norm_g, m_mla_w_out, v_ada_w, v_ada_b, v_norm_g, v_ffn_w_in,
           v_ffn_w_out, v_gdn_w_in, v_gdn_conv_w, v_gdn_a_log, v_gdn_dt_bias, v_gdn_norm_g, v_gdn_w_out, v_kv_ada_w,
           v_kv_ada_b, v_kv_norm_g, v_mla_w_dkv, v_mla_kv_norm_g, v_mla_w_ukv, v_mla_k_norm_g, v_mla_w_dq,
           v_mla_q_lora_norm_g, v_mla_w_uq, v_mla_q_norm_g, v_mla_w_out):
    a = dict(locals())
    mx, my, mc = _me()
    dev = 4 * mx + 2 * my + mc
    chip = 2 * mx + my
    x, tgt, pos = a["x"][0], a["loss_target"][0], a["positions"][0]
    take = lambda arr, i, axis=0: lax.dynamic_index_in_dim(arr, i, axis, keepdims=False)

    pre = _all_gather8("ag_pre", _pack_small([a["c"], a["gdn_conv_w"], a["norm_g"]]))
    c_all, conv_sh, norm_sh = _unpack_small(pre, [(D,), a["gdn_conv_w"].shape, a["norm_g"].shape])
    P = {n: a[n] for n in SMALL}
    P["gdn_conv_w"] = jnp.concatenate([conv_sh[2 * s] for s in range(4)], axis=2)
    P["norm_g"] = jnp.concatenate([norm_sh[2 * s] for s in range(4)], axis=2)
    c_act = _silu_rows("c_act", c_all)
    nada = a["ada_w"].shape[2]
    nkv = a["kv_ada_w"].shape[1]
    modp = [_matmul(f"mod{l}", c_act, a["ada_w"][l], precise=True) for l in range(4)]
    kvp = _matmul("modkv", c_act, a["kv_ada_w"], precise=True)
    mp = _all_gather8("ag_mod", _pack_small(modp + [kvp]))
    modp_all, kvp_all = _unpack_small(mp, [(4, 8, nada), (8, nkv)])
    mods = jnp.concatenate([take(modp_all[2 * s], dev, 1) for s in range(4)], axis=1) + a["ada_b"]
    mods = mods.reshape(4, NMOD, D)
    kvmod = (jnp.concatenate([take(kvp_all[2 * s], dev, 0) for s in range(4)]) + a["kv_ada_b"]).reshape(2, D)

    flat = jnp.concatenate([a[n].reshape(-1).astype(BF16) for n, _ in BIG])
    wall = _gather_weights("ag_w", _pack_flat(flat)).reshape(4, -1)
    W, off = {}, 0
    for n, ax in BIG:
        sz = a[n].size
        W[n] = _shards_merge(wall[:, off:off + sz].reshape((4,) + a[n].shape), ax)
        off += sz

    loss, dx, gw, gp, dmods, dkvmod = _step(x, tgt, pos, mods, kvmod, W, P)
    loss = lax.psum(loss, ("x", "y", "c"))

    gflat = jnp.concatenate([_shards_first(gw[n], ax).reshape(4, -1) for n, ax in BIG], axis=1)
    g4 = _pack_flat(gflat)
    q = _add_half("rs_pair", g4, _exchange_half("rs_x1", g4), mc)
    r = _add_chips("rs_chips", q, _scatter_chips("rs_x2", q), chip)
    gsh = _exchange_full("rs_x3", r).reshape(-1)
    grads, off = {}, 0
    for n, _ in BIG:
        grads[n] = gsh[off:off + a[n].size].reshape(a[n].shape)
        off += a[n].size

    small = _all_gather8("ag_small", _pack_small([dmods, dkvmod] + [gp[n] for n in SMALL]))
    shapes = [(4, NMOD * D), (2 * D,)] + [gp[n].shape for n in SMALL]
    dmod_all, dkv_all = _unpack_small(small, shapes)[:2]
    tot = _unpack_small(_sum8("sum_small", small), shapes)
    grads["ada_b"], grads["kv_ada_b"] = tot[0], tot[1]
    for n, t in zip(SMALL, tot[2:]):
        grads[n] = t
    grads["norm_g"] = lax.dynamic_slice_in_dim(grads["norm_g"], chip * a["norm_g"].shape[2], a["norm_g"].shape[2], 2)
    grads["gdn_conv_w"] = lax.dynamic_slice_in_dim(grads["gdn_conv_w"], chip * a["gdn_conv_w"].shape[2],
                                                   a["gdn_conv_w"].shape[2], 2)
    ca = jnp.pad(c_act, ((0, LANE - 8), (0, 0)))
    dm = jnp.pad(lax.dynamic_slice_in_dim(dmod_all.reshape(8, 4, NMOD * D), chip * nada, nada, 2), ((0, LANE - 8), (0, 0), (0, 0)))
    grads["ada_w"] = jnp.stack([_matmul(f"gada{l}", ca, dm[:, l], "tn", precise=True) for l in range(4)])
    dk = jnp.pad(lax.dynamic_slice_in_dim(dkv_all, chip * nkv, nkv, 1), ((0, LANE - 8), (0, 0)))
    grads["kv_ada_w"] = _matmul("gadakv", ca, dk, "tn", precise=True)

    upd = [_adamw("adamw_" + n, a[n], grads[n], a["m_" + n], a["v_" + n]) for n in WEIGHTS]
    return (loss, dx[None], *[grads[n] for n in WEIGHTS], *[u[0] for u in upd], *[u[1] for u in upd], *[u[2] for u in upd])
```

```python
import functools

import jax
import jax.numpy as jnp
from jax import lax
from jax.experimental import pallas as pl
from jax.experimental.pallas import tpu as pltpu

F32 = jnp.float32
BF16 = jnp.bfloat16
HI = lax.Precision.HIGHEST
MESH = pl.DeviceIdType.MESH

D = 1024
NH = 8
DH = 128
FF = 2816
NMOD = 9
CHUNK = 64
ROPE = 64
QKH = 192
HP = 256
KVL = 256
QL = 384
GDN_IN = 4224
GATE_CB = 32
EPS = 1e-6
ROPE_BASE = 10000.0
LANE = 128
SUB = 8
VMEM_LIMIT = 56 * 1024 * 1024

ADAM_LR, ADAM_B1, ADAM_B2, ADAM_EPS, ADAM_WD, ADAM_STEP = 0.001, 0.9, 0.999, 1e-08, 0.01, 10


def _tile(n, prefs=(512, 384, 256, 128)):
    for p in prefs:
        if n % p == 0:
            return p
    return n


def _cparams(sem):
    return pltpu.CompilerParams(dimension_semantics=sem, vmem_limit_bytes=VMEM_LIMIT)


class Row:
    def __init__(self, arr, width=None, cb=0, splits=None, halo=None):
        self.arr = arr
        self.width = arr.shape[1] if width is None else width
        self.cb = cb
        self.splits = splits
        self.halo = halo


def _rowwise(name, fn, rows, bcs, outs, accs, tm):
    S = rows[0].arr.shape[0]
    n = S // tm
    nr, nb, no, na = len(rows), len(bcs), len(outs), len(accs)

    def body(*refs):
        rrefs, brefs = refs[:nr], refs[nr:nr + nb]
        orefs, arefs = refs[nr + nb:nr + nb + no], refs[nr + nb + no:]
        pieces = []
        for r, ref in zip(rows, rrefs):
            if r.splits is None:
                pieces.append(ref[...])
            else:
                off = 0
                for w in r.splits:
                    pieces.append(ref[:, off:off + w])
                    off += w
        out_pieces, acc_vals = fn(pieces, [b[...] for b in brefs])
        k = 0
        for (widths, dt), oref in zip(outs, orefs):
            off = 0
            for w in widths:
                oref[:, off:off + w] = out_pieces[k].astype(dt)
                k += 1
                off += w
        if na:
            @pl.when(pl.program_id(0) == 0)
            def _():
                for a in arefs:
                    a[...] = jnp.zeros(a.shape, F32)
            for a, v in zip(arefs, acc_vals):
                a[...] += v

    in_specs = []
    for r in rows:
        if r.halo is None:
            in_specs.append(pl.BlockSpec((tm, r.width), lambda i, cb=r.cb: (i, cb)))
        elif r.halo == "prev":
            in_specs.append(pl.BlockSpec((SUB, r.width), lambda i, cb=r.cb: (jnp.maximum(i * (tm // SUB) - 1, 0), cb)))
        else:
            in_specs.append(pl.BlockSpec((SUB, r.width), lambda i, cb=r.cb: (jnp.minimum((i + 1) * (tm // SUB), S // SUB - 1), cb)))
    in_specs += [pl.BlockSpec(b.shape, lambda i, nd=b.ndim: (0,) * nd) for b in bcs]
    out_specs = [pl.BlockSpec((tm, sum(w)), lambda i: (i, 0)) for w, _ in outs]
    out_specs += [pl.BlockSpec(s, lambda i: (0, 0)) for s in accs]
    out_shape = [jax.ShapeDtypeStruct((S, sum(w)), dt) for w, dt in outs]
    out_shape += [jax.ShapeDtypeStruct(s, F32) for s in accs]
    res = pl.pallas_call(body, name=name, grid=(n,), in_specs=in_specs, out_specs=out_specs, out_shape=out_shape,
                         compiler_params=_cparams(("arbitrary",)))(*[r.arr for r in rows], *bcs)
    return res


def _rw_fwd(name, f, rows, bcs, outs, tm):
    def fn(pieces, bvals):
        return list(f(*[p.astype(F32) for p in pieces], *[b.astype(F32) for b in bvals])), []
    return _rowwise(name, fn, rows, bcs, outs, [], tm)


def _npieces(rows):
    return sum(1 if r.splits is None else len(r.splits) for r in rows)


def _rw_bwd(name, f, rows, bcs, cts, drow, dbc, outs, tm, add=None):
    np_, nct = _npieces(rows), _npieces(cts)

    def fn(pieces, bvals):
        allv = [p.astype(F32) for p in pieces[:np_]] + [b.astype(F32) for b in bvals]
        ct = [p.astype(F32) for p in pieces[np_:np_ + nct]]
        didx = [i for i, m in enumerate(list(drow) + list(dbc)) if m]

        def g(*dv):
            full = list(allv)
            for i, v in zip(didx, dv):
                full[i] = v
            return tuple(f(*full))

        _, vjp = jax.vjp(g, *[allv[i] for i in didx])
        grads = vjp(tuple(ct))
        nrd = sum(bool(m) for m in drow)
        rg, bg = list(grads[:nrd]), list(grads[nrd:])
        if add is not None:
            rg[0] = rg[0] + pieces[np_ + nct].astype(F32)
        return rg, bg

    accs = [b.shape for b, m in zip(bcs, dbc) if m]
    return _rowwise(name, fn, list(rows) + list(cts) + ([add] if add is not None else []), bcs, outs, accs, tm)


def _sigmoid(x):
    return 1.0 / (1.0 + jnp.exp(-x))


def _silu(x):
    return x * _sigmoid(x)


def _softplus(x):
    return jnp.maximum(x, 0.0) + jnp.log(1.0 + jnp.exp(-jnp.abs(x)))


def f_mod(x, g, shift, scale):
    y = x * lax.rsqrt(jnp.mean(x * x, axis=-1, keepdims=True) + EPS)
    return (y * g * (1.0 + scale) + shift,)


def f_rms(x, g):
    return (x * lax.rsqrt(jnp.mean(x * x, axis=-1, keepdims=True) + EPS) * g,)


def f_act(gate, up):
    return (_silu(gate) * up,)


def make_f_res(coef):
    def f_res(y, gate):
        return (coef * gate * y,)
    return f_res


def f_gdnpre(*p):
    out = []
    for i, t in enumerate(p):
        t = _silu(t)
        if i < 2 * NH:
            t = t * lax.rsqrt(jnp.sum(t * t, axis=-1, keepdims=True) + EPS)
        out.append(t)
    return tuple(out)


def f_gates(gates, a_log, dt_bias):
    return _sigmoid(gates), -jnp.exp(a_log) * _softplus(gates + dt_bias)


def f_gdnpost(*a):
    o, z, g = a[:NH], a[NH:2 * NH], a[2 * NH]
    out = []
    for oh, zh in zip(o, z):
        y = oh * lax.rsqrt(jnp.mean(oh * oh, axis=-1, keepdims=True) + EPS) * g
        out.append(y * _silu(zh))
    return tuple(out)


def make_f_qk(shared_rope):
    def f(*a):
        if shared_rope:
            ns, rs = a[:NH], [a[NH]] * NH
            cosp, sins, gn, gr, pm = a[NH + 1:NH + 6]
        else:
            ns, rs = a[0:2 * NH:2], a[1:2 * NH:2]
            cosp, sins, gn, gr, pm = a[2 * NH:2 * NH + 5]
        out = []
        for n, r in zip(ns, rs):
            ss = jnp.sum(n * n, axis=-1, keepdims=True) + jnp.sum(r * r, axis=-1, keepdims=True)
            rstd = lax.rsqrt(ss * (1.0 / QKH) + EPS)
            yn = n * rstd * gn
            yr = r * rstd * gr
            sw = jnp.dot(yr, pm, precision=HI, preferred_element_type=F32)
            out += [yn, yr * cosp + sw * sins]
        return tuple(out)
    return f


def _matmul(name, a, b, mode="nn", out_dtype=F32, precise=False):
    if mode == "nn":
        (M, K), N = a.shape, b.shape[1]
    elif mode == "nt":
        (M, K), N = a.shape, b.shape[0]
    else:
        (K, M), N = a.shape, b.shape[1]
    tm = _tile(M, (1024, 512, 256, 128))
    tn = _tile(N, (1024, 512, 384, 256, 128))
    tk = _tile(K, (1408, 1024, 512, 384, 256, 128))
    nk = K // tk
    dims = {"nn": (((1,), (0,)), ((), ())), "nt": (((1,), (1,)), ((), ())), "tn": (((0,), (0,)), ((), ()))}[mode]

    def body(a_ref, b_ref, o_ref, acc_ref):
        k = pl.program_id(2)

        @pl.when(k == 0)
        def _():
            acc_ref[...] = jnp.zeros(acc_ref.shape, F32)

        if precise:
            acc_ref[...] += lax.dot_general(a_ref[...].astype(F32), b_ref[...].astype(F32), dims, precision=HI,
                                            preferred_element_type=F32)
        else:
            acc_ref[...] += lax.dot_general(a_ref[...].astype(BF16), b_ref[...].astype(BF16), dims,
                                            preferred_element_type=F32)

        @pl.when(k == nk - 1)
        def _():
            o_ref[...] = acc_ref[...].astype(o_ref.dtype)

    if mode == "nn":
        in_specs = [pl.BlockSpec((tm, tk), lambda i, j, k: (i, k)), pl.BlockSpec((tk, tn), lambda i, j, k: (k, j))]
    elif mode == "nt":
        in_specs = [pl.BlockSpec((tm, tk), lambda i, j, k: (i, k)), pl.BlockSpec((tn, tk), lambda i, j, k: (j, k))]
    else:
        in_specs = [pl.BlockSpec((tk, tm), lambda i, j, k: (k, i)), pl.BlockSpec((tk, tn), lambda i, j, k: (k, j))]
    return pl.pallas_call(body, name=name, grid=(M // tm, N // tn, nk), in_specs=in_specs,
                          out_specs=pl.BlockSpec((tm, tn), lambda i, j, k: (i, j)),
                          out_shape=jax.ShapeDtypeStruct((M, N), out_dtype),
                          scratch_shapes=[pltpu.VMEM((tm, tn), F32)],
                          compiler_params=_cparams(("parallel", "parallel", "arbitrary")))(a, b)


def _shift_down(t, p, d):
    if d == 0:
        return t
    tr = pltpu.roll(t, d, 0)
    pr = pltpu.roll(p, d, 0)
    r8 = lax.broadcasted_iota(jnp.int32, p.shape, 0)
    first = jnp.where(r8 < d, pr, tr[:SUB])
    return jnp.concatenate([first, tr[SUB:]], axis=0)


def _shift_up(t, nx, d):
    if d == 0:
        return t
    tm = t.shape[0]
    tr = pltpu.roll(t, tm - d, 0)
    nr = pltpu.roll(nx, SUB - d, 0)
    r8 = lax.broadcasted_iota(jnp.int32, nx.shape, 0)
    last = jnp.where(r8 >= SUB - d, nr, tr[tm - SUB:])
    return jnp.concatenate([tr[:tm - SUB], last], axis=0)


def _conv_fwd(name, proj, w8, C, tm):
    def fn(pieces, bvals):
        t, p = pieces[0].astype(F32), pieces[1].astype(F32)
        w = bvals[0]
        p = jnp.where(pl.program_id(0) == 0, 0.0, p)
        out = w[3:4] * t
        for d in (1, 2, 3):
            out = out + w[3 - d:4 - d] * _shift_down(t, p, d)
        return [out], []
    return _rowwise(name, fn, [Row(proj, C), Row(proj, C, halo="prev")], [w8], [((C,), F32)], [], tm)[0]


def _conv_bwd(name, proj, dout, w8, C, tm, out_dtype):
    n = proj.shape[0] // tm

    def fn(pieces, bvals):
        t, p, g, gn = [v.astype(F32) for v in pieces]
        w = bvals[0]
        i = pl.program_id(0)
        p = jnp.where(i == 0, 0.0, p)
        gn = jnp.where(i == n - 1, 0.0, gn)
        dx = w[3:4] * g
        dws = [jnp.sum(g * t, axis=0, keepdims=True)]
        for d in (1, 2, 3):
            dx = dx + w[3 - d:4 - d] * _shift_up(g, gn, d)
            dws.append(jnp.sum(g * _shift_down(t, p, d), axis=0, keepdims=True))
        dw = jnp.concatenate([dws[3], dws[2], dws[1], dws[0], jnp.zeros((4, g.shape[1]), F32)], axis=0)
        return [dx], [dw]
    return _rowwise(name, fn, [Row(proj, C), Row(proj, C, halo="prev"), Row(dout), Row(dout, halo="next")], [w8],
                    [((C,), out_dtype)], [(SUB, C)], tm)


def _bdot(a, b, ca, cb, exact):
    dims = (((ca,), (cb,)), ((0,), (0,)))
    if exact:
        return lax.dot_general(a, b, dims, precision=HI, preferred_element_type=F32)
    return lax.dot_general(a.astype(BF16), b.astype(BF16), dims, preferred_element_type=F32)


def _gdn_chunk(q, k, v, gcol, grow, bcol, S):
    C = CHUNK
    ii = lax.broadcasted_iota(jnp.int32, (1, C, C), 1)
    jj = lax.broadcasted_iota(jnp.int32, (1, C, C), 2)
    incl, strict = ii >= jj, ii > jj
    gc_col = jnp.sum(jnp.where(incl, 1.0, 0.0) * grow, axis=2, keepdims=True)
    gc_row = jnp.sum(jnp.where(jj >= ii, 1.0, 0.0) * gcol, axis=1, keepdims=True)
    decay = jnp.where(incl, jnp.exp(jnp.where(incl, gc_col - gc_row, 0.0)), 0.0)
    qs = q * (DH ** -0.5)
    kb = k * bcol
    nl = -jnp.where(strict, _bdot(kb, k, 2, 2, False) * decay, 0.0)
    T = jnp.where(ii == jj, 1.0, 0.0) + nl
    pw = nl
    for _ in range(5):
        pw = _bdot(pw, pw, 2, 1, True)
        T = T + _bdot(T, pw, 2, 1, True)
    egc = jnp.exp(gc_col)
    u = _bdot(T, v * bcol, 2, 1, True)
    w = _bdot(T, kb * egc, 2, 1, True)
    att = jnp.where(incl, _bdot(qs, k, 2, 2, False) * decay, 0.0)
    v_new = u - _bdot(w, S, 2, 1, False)
    o = _bdot(qs * egc, S, 2, 1, False) + _bdot(att, v_new, 2, 1, False)
    g_last = jnp.sum(grow, axis=2, keepdims=True)
    k_dec = k * jnp.exp(g_last - gc_col)
    S_out = S * jnp.exp(g_last) + _bdot(k_dec, v_new, 1, 1, False)
    return o, S_out


def _heads(ref, w):
    return jnp.stack([ref[:, h * w:(h + 1) * w] for h in range(NH)])


def _gdn_specs(NC, rev):
    ix = (lambda i: NC - 1 - i) if rev else (lambda i: i)
    wide = pl.BlockSpec((CHUNK, D), lambda i: (ix(i), 0))
    col = pl.BlockSpec((CHUNK, NH), lambda i: (ix(i), 0))
    row = pl.BlockSpec((1, NH, CHUNK), lambda i: (ix(i), 0, 0))
    st = pl.BlockSpec((1, NH, DH, DH), lambda i: (ix(i), 0, 0, 0))
    return wide, col, row, st


def _gdn_fwd(name, q, k, v, gcol, grow, bcol):
    S = q.shape[0]
    NC = S // CHUNK

    def body(q_ref, k_ref, v_ref, gc_ref, gr_ref, b_ref, o_ref, ss_ref, st):
        @pl.when(pl.program_id(0) == 0)
        def _():
            st[...] = jnp.zeros(st.shape, F32)
        s_in = st[...]
        ss_ref[0] = s_in
        grow = jnp.stack([gr_ref[0, h:h + 1, :] for h in range(NH)])
        o, s_out = _gdn_chunk(_heads(q_ref, DH), _heads(k_ref, DH), _heads(v_ref, DH), _heads(gc_ref, 1), grow,
                              _heads(b_ref, 1), s_in)
        for h in range(NH):
            o_ref[:, h * DH:(h + 1) * DH] = o[h]
        st[...] = s_out

    wide, col, row, stsp = _gdn_specs(NC, False)
    return pl.pallas_call(body, name=name, grid=(NC,), in_specs=[wide, wide, wide, col, row, col],
                          out_specs=[wide, stsp],
                          out_shape=[jax.ShapeDtypeStruct((S, D), F32), jax.ShapeDtypeStruct((NC, NH, DH, DH), F32)],
                          scratch_shapes=[pltpu.VMEM((NH, DH, DH), F32)],
                          compiler_params=_cparams(("arbitrary",)))(q, k, v, gcol, grow, bcol)


def _gdn_bwd(name, q, k, v, gcol, grow, bcol, ssave, do):
    S = q.shape[0]
    NC = S // CHUNK

    def body(q_ref, k_ref, v_ref, gc_ref, gr_ref, b_ref, ss_ref, do_ref, dq_ref, dk_ref, dv_ref, dgc_ref, dgr_ref, db_ref, dst):
        @pl.when(pl.program_id(0) == 0)
        def _():
            dst[...] = jnp.zeros(dst.shape, F32)
        grow = jnp.stack([gr_ref[0, h:h + 1, :] for h in range(NH)])
        prim = (_heads(q_ref, DH), _heads(k_ref, DH), _heads(v_ref, DH), _heads(gc_ref, 1), grow, _heads(b_ref, 1), ss_ref[0])
        _, vjp = jax.vjp(_gdn_chunk, *prim)
        dq, dk, dv, dgc, dgr, db, ds = vjp((_heads(do_ref, DH), dst[...]))
        for h in range(NH):
            hs = slice(h * DH, (h + 1) * DH)
            dq_ref[:, hs] = dq[h]
            dk_ref[:, hs] = dk[h]
            dv_ref[:, hs] = dv[h]
            dgc_ref[:, h:h + 1] = dgc[h]
            dgr_ref[0, h:h + 1, :] = dgr[h]
            db_ref[:, h:h + 1] = db[h]
        dst[...] = ds

    wide, col, row, stsp = _gdn_specs(NC, True)
    return pl.pallas_call(body, name=name, grid=(NC,), in_specs=[wide, wide, wide, col, row, col, stsp, wide],
                          out_specs=[wide, wide, wide, col, row, col],
                          out_shape=[jax.ShapeDtypeStruct((S, D), F32)] * 3 + [jax.ShapeDtypeStruct((S, NH), F32),
                                                                                 jax.ShapeDtypeStruct((NC, NH, CHUNK), F32),
                                                                                 jax.ShapeDtypeStruct((S, NH), F32)],
                          scratch_shapes=[pltpu.VMEM((NH, DH, DH), F32)],
                          compiler_params=_cparams(("arbitrary",)))(q, k, v, gcol, grow, bcol, ssave, do)


TQ = 512
SM_SCALE = QKH ** -0.5
NEG = -1e30


def _diag_mask(transposed):
    r = lax.broadcasted_iota(jnp.int32, (TQ, TQ), 0) // CHUNK
    c = lax.broadcasted_iota(jnp.int32, (TQ, TQ), 1) // CHUNK
    return (r <= c) if transposed else (c <= r)


def _dot_nt(a, b):
    return lax.dot_general(a, b, (((1,), (1,)), ((), ())), preferred_element_type=F32)


def _flash_fwd(name, qp, kp, kv):
    S = qp.shape[0]
    nq = S // TQ

    def body(q_ref, k_ref, v_ref, o_ref, lse_ref):
        qi = pl.program_id(1)
        q = q_ref[...]

        def step(j, carry, masked):
            m, l, acc = carry
            rows = pl.ds(pl.multiple_of(j * TQ, TQ), TQ)
            s = _dot_nt(q, k_ref[rows, :]) * SM_SCALE
            if masked:
                s = jnp.where(_diag_mask(False), s, NEG)
            m_new = jnp.maximum(m, jnp.max(s, axis=-1, keepdims=True))
            p = jnp.exp(s - m_new)
            alpha = jnp.exp(m - m_new)
            l = alpha * l + jnp.sum(p, axis=-1, keepdims=True)
            acc = alpha * acc + jnp.dot(p.astype(BF16), v_ref[rows, :].astype(BF16), preferred_element_type=F32)
            return m_new, l, acc

        carry = (jnp.full((TQ, 1), NEG, F32), jnp.zeros((TQ, 1), F32), jnp.zeros((TQ, DH), F32))
        carry = lax.fori_loop(0, qi, lambda j, c: step(j, c, False), carry)
        m, l, acc = step(qi, carry, True)
        o_ref[...] = acc / l
        lse_ref[0] = m + jnp.log(l)

    return pl.pallas_call(
        body, name=name, grid=(NH, nq),
        in_specs=[pl.BlockSpec((TQ, HP), lambda h, i: (i, h)), pl.BlockSpec((S, HP), lambda h, i: (0, h)),
                  pl.BlockSpec((S, DH), lambda h, i: (0, NH + h))],
        out_specs=[pl.BlockSpec((TQ, DH), lambda h, i: (i, h)), pl.BlockSpec((1, TQ, 1), lambda h, i: (h, i, 0))],
        out_shape=[jax.ShapeDtypeStruct((S, NH * DH), F32), jax.ShapeDtypeStruct((NH, S, 1), F32)],
        compiler_params=_cparams(("parallel", "arbitrary")))(qp, kp, kv)


def _flash_bwd_dq(name, qp, kp, kv, o, do, lse):
    S = qp.shape[0]
    nq = S // TQ

    def body(q_ref, k_ref, v_ref, o_ref, do_ref, lse_ref, dq_ref, dl_ref):
        qi = pl.program_id(1)
        q = q_ref[...]
        do = do_ref[...]
        delta = jnp.sum(o_ref[...] * do, axis=-1, keepdims=True)
        dl_ref[0] = delta
        dob = do.astype(BF16)
        lse = lse_ref[0]

        def step(j, dq, masked):
            rows = pl.ds(pl.multiple_of(j * TQ, TQ), TQ)
            k = k_ref[rows, :]
            s = _dot_nt(q, k) * SM_SCALE
            if masked:
                s = jnp.where(_diag_mask(False), s, NEG)
            p = jnp.exp(s - lse)
            dp = _dot_nt(dob, v_ref[rows, :].astype(BF16))
            ds = p * (dp - delta) * SM_SCALE
            return dq + jnp.dot(ds.astype(BF16), k, preferred_element_type=F32)

        dq = lax.fori_loop(0, qi, lambda j, c: step(j, c, False), jnp.zeros((TQ, HP), F32))
        dq_ref[...] = step(qi, dq, True)

    return pl.pallas_call(
        body, name=name, grid=(NH, nq),
        in_specs=[pl.BlockSpec((TQ, HP), lambda h, i: (i, h)), pl.BlockSpec((S, HP), lambda h, i: (0, h)),
                  pl.BlockSpec((S, DH), lambda h, i: (0, NH + h)), pl.BlockSpec((TQ, DH), lambda h, i: (i, h)),
                  pl.BlockSpec((TQ, DH), lambda h, i: (i, h)), pl.BlockSpec((1, TQ, 1), lambda h, i: (h, i, 0))],
        out_specs=[pl.BlockSpec((TQ, HP), lambda h, i: (i, h)), pl.BlockSpec((1, TQ, 1), lambda h, i: (h, i, 0))],
        out_shape=[jax.ShapeDtypeStruct((S, NH * HP), F32), jax.ShapeDtypeStruct((NH, S, 1), F32)],
        compiler_params=_cparams(("parallel", "arbitrary")))(qp, kp, kv, o, do, lse)


def _flash_bwd_dkv(name, qp, kp, kv, do, lse_row, delta_row):
    S = qp.shape[0]
    nq = S // TQ

    def body(q_ref, k_ref, v_ref, do_ref, lse_ref, dl_ref, dk_ref, dv_ref):
        kj = pl.program_id(1)
        k = k_ref[...]
        vb = v_ref[...].astype(BF16)

        def step(i, carry, masked):
            dk, dv = carry
            rows = pl.ds(pl.multiple_of(i * TQ, TQ), TQ)
            q = q_ref[rows, :]
            dob = do_ref[rows, :].astype(BF16)
            st = _dot_nt(k, q) * SM_SCALE
            pt = jnp.exp(st - lse_ref[0, :, rows])
            if masked:
                pt = jnp.where(_diag_mask(True), pt, 0.0)
            dv = dv + jnp.dot(pt.astype(BF16), dob, preferred_element_type=F32)
            dpt = _dot_nt(vb, dob)
            dst = pt * (dpt - dl_ref[0, :, rows]) * SM_SCALE
            dk = dk + jnp.dot(dst.astype(BF16), q, preferred_element_type=F32)
            return dk, dv

        carry = step(kj, (jnp.zeros((TQ, HP), F32), jnp.zeros((TQ, DH), F32)), True)
        dk, dv = lax.fori_loop(kj + 1, nq, lambda i, c: step(i, c, False), carry)
        dk_ref[...] = dk
        dv_ref[...] = dv

    return pl.pallas_call(
        body, name=name, grid=(NH, nq),
        in_specs=[pl.BlockSpec((S, HP), lambda h, j: (0, h)), pl.BlockSpec((TQ, HP), lambda h, j: (j, h)),
                  pl.BlockSpec((TQ, DH), lambda h, j: (j, NH + h)), pl.BlockSpec((S, DH), lambda h, j: (0, h)),
                  pl.BlockSpec((1, 1, S), lambda h, j: (h, 0, 0)), pl.BlockSpec((1, 1, S), lambda h, j: (h, 0, 0))],
        out_specs=[pl.BlockSpec((TQ, HP), lambda h, j: (j, h)), pl.BlockSpec((TQ, DH), lambda h, j: (j, h))],
        out_shape=[jax.ShapeDtypeStruct((S, NH * HP), F32), jax.ShapeDtypeStruct((S, NH * DH), F32)],
        compiler_params=_cparams(("parallel", "arbitrary")))(qp, kp, kv, do, lse_row, delta_row)


def _tm(S, width):
    t = 512 if width <= 1024 else (256 if width <= 3072 else 128)
    return min(t, S)


def _mod_fwd(tag, x, g, shift, scale):
    S = x.shape[0]
    return _rw_fwd(tag + "_mod", f_mod, [Row(x)], [g, shift, scale], [((D,), BF16)], _tm(S, D))[0]


def _mod_bwd(tag, x, g, shift, scale, dh, dx_direct):
    S = x.shape[0]
    r = _rw_bwd(tag + "_mod_b", f_mod, [Row(x)], [g, shift, scale], [Row(dh)], [True], [True] * 3, [((D,), F32)],
                _tm(S, D), add=Row(dx_direct))
    return r[0], r[1:]


def _res_fwd(tag, x, y, gate, coef):
    S = x.shape[0]

    def fn(pieces, bvals):
        return [pieces[0] + coef * bvals[0] * pieces[1]], []
    return _rowwise(tag + "_res", fn, [Row(x), Row(y)], [gate], [((D,), F32)], [], _tm(S, D))[0]


def _res_bwd(tag, y, gate, dxn, coef):
    S = y.shape[0]
    r = _rw_bwd(tag + "_res_b", make_f_res(coef), [Row(y)], [gate], [Row(dxn)], [True], [True], [((D,), BF16)], _tm(S, D))
    return r[0], r[1]


def _ffn_fwd(tag, x, mod3, g, w_in, w_out):
    shift, scale, gate = mod3
    S = x.shape[0]
    h = _mod_fwd(tag, x, g, shift, scale)
    gu = _matmul(tag + "_in", h, w_in)
    a = _rw_fwd(tag + "_act", f_act, [Row(gu, splits=[FF, FF])], [], [((FF,), BF16)], _tm(S, 2 * FF))[0]
    y = _matmul(tag + "_out", a, w_out)
    xn = _res_fwd(tag, x, y, gate, 0.5)
    return xn, (x, h, gu, a, y)


def _ffn_bwd(tag, dxn, res, mod3, g, w_in, w_out):
    shift, scale, gate = mod3
    x, h, gu, a, y = res
    S = x.shape[0]
    dy, dgate = _res_bwd(tag, y, gate, dxn, 0.5)
    da = _matmul(tag + "_out_bi", dy, w_out, "nt")
    dw_out = _matmul(tag + "_out_bw", a, dy, "tn")
    dgu = _rw_bwd(tag + "_act_b", f_act, [Row(gu, splits=[FF, FF])], [], [Row(da)], [True, True], [],
                  [((FF, FF), BF16)], _tm(S, 2 * FF))[0]
    dh = _matmul(tag + "_in_bi", dgu, w_in, "nt")
    dw_in = _matmul(tag + "_in_bw", h, dgu, "tn")
    dx, (dg, dshift, dscale) = _mod_bwd(tag, x, g, shift, scale, dh, dxn)
    return dx, dict(w_in=dw_in, w_out=dw_out, g=dg, mod=(dshift, dscale, dgate))


def _pad_lanes(a, lo, width=LANE):
    return jnp.pad(a, ((0, 0), (lo, width - lo - a.shape[1])))


def _gdn_layer_fwd(tag, x, mod3, g, p):
    shift, scale, gate = mod3
    S = x.shape[0]
    NC = S // CHUNK
    h = _mod_fwd(tag, x, g, shift, scale)
    proj = _matmul(tag + "_in", h, p["w_in"])
    qc = _conv_fwd(tag + "_conv", proj, p["conv_w8"], 3 * D, _tm(S, 3 * D))
    q, k, v = _rw_fwd(tag + "_pre", f_gdnpre, [Row(qc, splits=[DH] * (3 * NH))], [],
                      [((DH,) * NH, F32)] * 3, _tm(S, 3 * D))
    betaf, gf = _rw_fwd(tag + "_gates", f_gates, [Row(proj, LANE, cb=GATE_CB)], [p["a_log128"], p["dt_bias128"]],
                        [((LANE,), F32)] * 2, _tm(S, LANE))
    bcol, gcol = betaf[:, :NH], gf[:, NH:2 * NH]
    grow = gcol.reshape(NC, CHUNK, NH).transpose(0, 2, 1)
    o, ssave = _gdn_fwd(tag + "_core", q, k, v, gcol, grow, bcol)
    on = _rw_fwd(tag + "_post", f_gdnpost, [Row(o, splits=[DH] * NH), Row(proj, D, cb=3, splits=[DH] * NH)],
                 [p["norm_g"]], [((DH,) * NH, BF16)], _tm(S, 2 * D))[0]
    y = _matmul(tag + "_out", on, p["w_out"])
    xn = _res_fwd(tag, x, y, gate, 1.0)
    return xn, (x, h, proj, qc, q, k, v, gcol, grow, bcol, ssave, o, on, y)


def _gdn_layer_bwd(tag, dxn, res, mod3, g, p):
    shift, scale, gate = mod3
    x, h, proj, qc, q, k, v, gcol, grow, bcol, ssave, o, on, y = res
    S = x.shape[0]
    dy, dgate = _res_bwd(tag, y, gate, dxn, 1.0)
    don = _matmul(tag + "_out_bi", dy, p["w_out"], "nt")
    dw_out = _matmul(tag + "_out_bw", on, dy, "tn")
    do, dz, dnorm = _rw_bwd(tag + "_post_b", f_gdnpost, [Row(o, splits=[DH] * NH), Row(proj, D, cb=3, splits=[DH] * NH)],
                            [p["norm_g"]], [Row(don, splits=[DH] * NH)], [True] * (2 * NH), [True],
                            [((DH,) * NH, F32), ((DH,) * NH, BF16)], _tm(S, 2 * D))
    dq, dk, dv, dgc, dgr, db = _gdn_bwd(tag + "_core_b", q, k, v, gcol, grow, bcol, ssave, do)
    dgcol = dgc + dgr.transpose(0, 2, 1).reshape(S, NH)
    dgates, da_log, ddt = _rw_bwd(tag + "_gates_b", f_gates, [Row(proj, LANE, cb=GATE_CB)], [p["a_log128"], p["dt_bias128"]],
                                  [Row(_pad_lanes(db, 0)), Row(_pad_lanes(dgcol, NH))], [True], [True, True],
                                  [((LANE,), BF16)], _tm(S, LANE))
    dqc = _rw_bwd(tag + "_pre_b", f_gdnpre, [Row(qc, splits=[DH] * (3 * NH))], [],
                  [Row(dq, splits=[DH] * NH), Row(dk, splits=[DH] * NH), Row(dv, splits=[DH] * NH)],
                  [True] * (3 * NH), [], [((DH,) * (3 * NH), F32)], _tm(S, 3 * D))[0]
    dqkv, dconv = _conv_bwd(tag + "_conv_b", proj, dqc, p["conv_w8"], 3 * D, _tm(S, 3 * D), BF16)
    dproj = jnp.concatenate([dqkv, dz, dgates], axis=1)
    dh = _matmul(tag + "_in_bi", dproj, p["w_in"], "nt")
    dw_in = _matmul(tag + "_in_bw", h, dproj, "tn")
    dx, (dg, dshift, dscale) = _mod_bwd(tag, x, g, shift, scale, dh, dxn)
    return dx, dict(w_in=dw_in, conv_w8=dconv, a_log128=da_log, dt_bias128=ddt, norm_g=dnorm,
                    w_out=dw_out, g=dg, mod=(dshift, dscale, dgate))


def _qk_rows(src, shared_rope, ckv=None):
    if shared_rope:
        return [Row(src, D, cb=0, splits=[DH] * NH), Row(ckv, LANE, cb=2)]
    return [Row(src, splits=[DH] * (2 * NH))]


def _kv_fwd(x, kvmod, p, tabs):
    shift, scale = kvmod
    S = x.shape[0]
    h = _mod_fwd("kv", x, p["kv_norm_g"], shift, scale)
    ckv = _matmul("kv_dkv", h, p["w_dkv"])
    lat = _rw_fwd("kv_lat", f_rms, [Row(ckv, KVL)], [p["kv_lat_g"]], [((KVL,), BF16)], _tm(S, KVL))[0]
    kvf = _matmul("kv_ukv", lat, p["w_ukv"])
    kp = _rw_fwd("kv_k", make_f_qk(True), _qk_rows(kvf, True, ckv) + [Row(tabs[0]), Row(tabs[1])],
                 [p["k_gn"], p["k_gr"], p["pm"]], [((DH,) * (2 * NH), BF16)], _tm(S, 2 * D))[0]
    return kp, kvf, (x, h, ckv, lat)


def _kv_bwd(dkp, dv, dx_direct, res, kvmod, kvf, p, tabs):
    shift, scale = kvmod
    x, h, ckv, lat = res
    S = x.shape[0]
    dkn, dkr, dgn, dgr = _rw_bwd("kv_k_b", make_f_qk(True), _qk_rows(kvf, True, ckv) + [Row(tabs[0]), Row(tabs[1])],
                                 [p["k_gn"], p["k_gr"], p["pm"]], [Row(dkp, splits=[DH] * (2 * NH))],
                                 [True] * (NH + 1) + [False, False], [True, True, False],
                                 [((DH,) * NH, BF16), ((LANE,), BF16)], _tm(S, 2 * D))
    dkvf = jnp.concatenate([dkn, dv.astype(BF16)], axis=1)
    dlat = _matmul("kv_ukv_bi", dkvf, p["w_ukv"], "nt")
    dw_ukv = _matmul("kv_ukv_bw", lat, dkvf, "tn")
    dcl, dlg = _rw_bwd("kv_lat_b", f_rms, [Row(ckv, KVL)], [p["kv_lat_g"]], [Row(dlat)], [True], [True],
                       [((KVL,), BF16)], _tm(S, KVL))
    dckv = jnp.concatenate([dcl, dkr], axis=1)
    dh = _matmul("kv_dkv_bi", dckv, p["w_dkv"], "nt")
    dw_dkv = _matmul("kv_dkv_bw", h, dckv, "tn")
    dx, (dg, dshift, dscale) = _mod_bwd("kv", x, p["kv_norm_g"], shift, scale, dh, dx_direct)
    return dx, dict(w_dkv=dw_dkv, w_ukv=dw_ukv, kv_lat_g=dlg, k_gn=dgn, k_gr=dgr, kv_norm_g=dg, mod=(dshift, dscale))


def _mla_layer_fwd(tag, x, mod3, g, p, kp, kvf, tabs):
    shift, scale, gate = mod3
    S = x.shape[0]
    h = _mod_fwd(tag, x, g, shift, scale)
    ql = _matmul(tag + "_dq", h, p["w_dq"])
    qln = _rw_fwd(tag + "_qln", f_rms, [Row(ql)], [p["ql_g"]], [((QL,), BF16)], _tm(S, QL))[0]
    qu = _matmul(tag + "_uq", qln, p["w_uq"])
    qp = _rw_fwd(tag + "_q", make_f_qk(False), _qk_rows(qu, False) + [Row(tabs[0]), Row(tabs[1])],
                 [p["q_gn"], p["q_gr"], p["pm"]], [((DH,) * (2 * NH), BF16)], _tm(S, 2 * D))[0]
    o, lse = _flash_fwd(tag + "_att", qp, kp, kvf)
    y = _matmul(tag + "_out", o, p["w_out"])
    xn = _res_fwd(tag, x, y, gate, 1.0)
    return xn, (x, h, ql, qln, qu, qp, o, lse, y)


def _mla_layer_bwd(tag, dxn, res, mod3, g, p, kp, kvf, tabs):
    shift, scale, gate = mod3
    x, h, ql, qln, qu, qp, o, lse, y = res
    S = x.shape[0]
    dy, dgate = _res_bwd(tag, y, gate, dxn, 1.0)
    do = _matmul(tag + "_out_bi", dy, p["w_out"], "nt")
    dw_out = _matmul(tag + "_out_bw", o, dy, "tn")
    dqp, delta = _flash_bwd_dq(tag + "_att_bq", qp, kp, kvf, o, do, lse)
    dkp, dv = _flash_bwd_dkv(tag + "_att_bkv", qp, kp, kvf, do, lse.reshape(NH, 1, S), delta.reshape(NH, 1, S))
    dqu, dgn, dgr = _rw_bwd(tag + "_q_b", make_f_qk(False), _qk_rows(qu, False) + [Row(tabs[0]), Row(tabs[1])],
                            [p["q_gn"], p["q_gr"], p["pm"]], [Row(dqp, splits=[DH] * (2 * NH))],
                            [True] * (2 * NH) + [False, False], [True, True, False],
                            [((DH,) * (2 * NH), BF16)], _tm(S, 2 * D))
    dqln = _matmul(tag + "_uq_bi", dqu, p["w_uq"], "nt")
    dw_uq = _matmul(tag + "_uq_bw", qln, dqu, "tn")
    dql, dqlg = _rw_bwd(tag + "_qln_b", f_rms, [Row(ql)], [p["ql_g"]], [Row(dqln)], [True], [True], [((QL,), BF16)],
                        _tm(S, QL))
    dh = _matmul(tag + "_dq_bi", dql, p["w_dq"], "nt")
    dw_dq = _matmul(tag + "_dq_bw", h, dql, "tn")
    dx, (dg, dshift, dscale) = _mod_bwd(tag, x, g, shift, scale, dh, dxn)
    return dx, dkp, dv, dict(w_dq=dw_dq, w_uq=dw_uq, w_out=dw_out, ql_g=dqlg, q_gn=dgn, q_gr=dgr, g=dg,
                             mod=(dshift, dscale, dgate))


def _loss_head(y, tgt):
    S = y.shape[0]

    def fn(pieces, bvals):
        e = pieces[0] - pieces[1]
        part = jnp.sum(e * e) * (0.5 / D)
        return [e * (1.0 / D)], [jnp.full((1, LANE), part, F32)]
    dy, part = _rowwise("loss", fn, [Row(y), Row(tgt)], [], [((D,), F32)], [(1, LANE)], _tm(S, D))
    return part[0, 0], dy


def _rope_tables(positions):
    S = positions.shape[0]
    half = ROPE // 2
    lane = lax.broadcasted_iota(jnp.int32, (1, LANE), 1)
    inv_freq = ROPE_BASE ** (-(lane % half).astype(F32) / half)
    live = (lane < ROPE).astype(F32)
    sign = jnp.where(lane < half, -1.0, 1.0) * live

    def fn(pieces, bvals):
        ang = pieces[0] * bvals[0]
        return [jnp.cos(ang) * bvals[1], jnp.sin(ang) * bvals[2]], []
    pos = jnp.broadcast_to(positions.astype(F32)[:, None], (S, LANE))
    cosp, sins = _rowwise("rope_tab", fn, [Row(pos)], [inv_freq, live, sign], [((LANE,), F32)] * 2, [], _tm(S, LANE))
    r = lax.broadcasted_iota(jnp.int32, (LANE, LANE), 0)
    c = lax.broadcasted_iota(jnp.int32, (LANE, LANE), 1)
    pm = (((c < half) & (r == c + half)) | ((c >= half) & (c < ROPE) & (r == c - half))).astype(F32)
    return (cosp, sins), pm


def _adamw(name, w, g, m, v):
    shape = w.shape
    C = shape[-1]
    R = w.size // C
    tr = R
    for t in (1024, 512, 256, 128, 64, 32, 16, 8):
        if R % t == 0 and t * C * 4 <= (1 << 21):
            tr = t
            break
    c1 = 1.0 - ADAM_B1 ** ADAM_STEP
    c2 = 1.0 - ADAM_B2 ** ADAM_STEP

    def body(w_ref, g_ref, m_ref, v_ref, d_ref, mo_ref, vo_ref):
        gg = g_ref[...]
        mn = ADAM_B1 * m_ref[...] + (1.0 - ADAM_B1) * gg
        vn = ADAM_B2 * v_ref[...] + (1.0 - ADAM_B2) * (gg * gg)
        d_ref[...] = -ADAM_LR * ((mn / c1) / (jnp.sqrt(vn / c2) + ADAM_EPS) + ADAM_WD * w_ref[...])
        mo_ref[...] = mn
        vo_ref[...] = vn

    spec = pl.BlockSpec((tr, C), lambda i: (i, 0))
    outs = pl.pallas_call(body, name=name, grid=(R // tr,), in_specs=[spec] * 4, out_specs=[spec] * 3,
                          out_shape=[jax.ShapeDtypeStruct((R, C), F32)] * 3,
                          compiler_params=_cparams(("parallel",)))(*[t.reshape(R, C) for t in (w, g, m, v)])
    return [o.reshape(shape) for o in outs]


HBM_SPEC = pl.BlockSpec(memory_space=pltpu.HBM)
OTHER_CHIPS = (4, 2, 6)
SIBLING = 1


def _me():
    return lax.axis_index("x"), lax.axis_index("y"), lax.axis_index("c")


def _peer(me, k):
    mx, my, mc = me
    return ((1 - mx) if k & 4 else mx, (1 - my) if k & 2 else my, (1 - mc) if k & 1 else mc)


def _rcopy(src, dst, ssem, rsem, to):
    return pltpu.make_async_remote_copy(src_ref=src, dst_ref=dst, send_sem=ssem, recv_sem=rsem, device_id=to,
                                        device_id_type=MESH)


def _all_gather8(name, x):
    def body(x_ref, o_ref, ssem, rsem, lsem):
        me = _me()
        mine = 4 * me[0] + 2 * me[1] + me[2]
        loc = pltpu.make_async_copy(x_ref, o_ref.at[mine], lsem)
        loc.start()
        sends = []
        for k in range(1, 8):
            cp = _rcopy(x_ref, o_ref.at[mine], ssem.at[k - 1], rsem.at[k - 1], _peer(me, k))
            cp.start()
            sends.append(cp)
        for k in range(1, 8):
            px, py, pc = _peer(me, k)
            _rcopy(x_ref, o_ref.at[4 * px + 2 * py + pc], ssem.at[k - 1], rsem.at[k - 1], (px, py, pc)).wait_recv()
        for cp in sends:
            cp.wait_send()
        loc.wait()

    return pl.pallas_call(body, name=name, out_shape=jax.ShapeDtypeStruct((8,) + x.shape, x.dtype),
                          in_specs=[HBM_SPEC], out_specs=HBM_SPEC,
                          scratch_shapes=[pltpu.SemaphoreType.DMA((7,)), pltpu.SemaphoreType.DMA((7,)),
                                          pltpu.SemaphoreType.DMA(())])(x)


def _gather_weights(name, wp):
    def body(w_ref, o_ref, ssem, rsem, lsem):
        me = _me()
        mc = me[2]
        loc = pltpu.make_async_copy(w_ref, o_ref.at[2 * me[0] + me[1]], lsem)
        loc.start()
        first = []
        for j, k in enumerate(OTHER_CHIPS):
            cp = _rcopy(w_ref.at[mc], o_ref.at[2 * me[0] + me[1], mc], ssem.at[j], rsem.at[j], _peer(me, k))
            cp.start()
            first.append(cp)
        passed = []
        for j, k in enumerate(OTHER_CHIPS):
            px, py, _ = _peer(me, k)
            land = o_ref.at[2 * px + py, mc]
            _rcopy(w_ref.at[mc], land, ssem.at[j], rsem.at[j], _peer(me, k)).wait_recv()
            fw = _rcopy(land, land, ssem.at[3 + j], rsem.at[3 + j], _peer(me, SIBLING))
            fw.start()
            passed.append(fw)
        for j, k in enumerate(OTHER_CHIPS):
            px, py, _ = _peer(me, k)
            land = o_ref.at[2 * px + py, 1 - mc]
            _rcopy(land, land, ssem.at[3 + j], rsem.at[3 + j], _peer(me, SIBLING)).wait_recv()
        for cp in first + passed:
            cp.wait_send()
        loc.wait()

    return pl.pallas_call(body, name=name, out_shape=jax.ShapeDtypeStruct((4,) + wp.shape, wp.dtype),
                          in_specs=[HBM_SPEC], out_specs=HBM_SPEC,
                          scratch_shapes=[pltpu.SemaphoreType.DMA((6,)), pltpu.SemaphoreType.DMA((6,)),
                                          pltpu.SemaphoreType.DMA(())])(wp)


def _exchange_half(name, g):
    def body(g_ref, p_ref, ssem, rsem):
        me = _me()
        cps = []
        for s in range(4):
            cp = _rcopy(g_ref.at[s, 1 - me[2]], p_ref.at[s], ssem.at[s], rsem.at[s], _peer(me, SIBLING))
            cp.start()
            cps.append(cp)
        for cp in cps:
            cp.wait()

    return pl.pallas_call(body, name=name, out_shape=jax.ShapeDtypeStruct((4,) + g.shape[2:], g.dtype),
                          in_specs=[HBM_SPEC], out_specs=HBM_SPEC,
                          scratch_shapes=[pltpu.SemaphoreType.DMA((4,)), pltpu.SemaphoreType.DMA((4,))])(g)


def _scatter_chips(name, q):
    def body(q_ref, t_ref, ssem, rsem):
        me = _me()
        cps = []
        for j, k in enumerate(OTHER_CHIPS):
            px, py, _ = _peer(me, k)
            cp = _rcopy(q_ref.at[2 * px + py], t_ref.at[j], ssem.at[j], rsem.at[j], _peer(me, k))
            cp.start()
            cps.append(cp)
        for cp in cps:
            cp.wait()

    return pl.pallas_call(body, name=name, out_shape=jax.ShapeDtypeStruct((3,) + q.shape[1:], q.dtype),
                          in_specs=[HBM_SPEC], out_specs=HBM_SPEC,
                          scratch_shapes=[pltpu.SemaphoreType.DMA((3,)), pltpu.SemaphoreType.DMA((3,))])(q)


def _exchange_full(name, r):
    def body(r_ref, o_ref, ssem, rsem, lsem):
        me = _me()
        mc = me[2]
        loc = pltpu.make_async_copy(r_ref, o_ref.at[mc], lsem)
        loc.start()
        cp = _rcopy(r_ref, o_ref.at[mc], ssem, rsem, _peer(me, SIBLING))
        cp.start()
        _rcopy(r_ref, o_ref.at[1 - mc], ssem, rsem, _peer(me, SIBLING)).wait_recv()
        cp.wait_send()
        loc.wait()

    return pl.pallas_call(body, name=name, out_shape=jax.ShapeDtypeStruct((2,) + r.shape, r.dtype),
                          in_specs=[HBM_SPEC], out_specs=HBM_SPEC,
                          scratch_shapes=[pltpu.SemaphoreType.DMA(()), pltpu.SemaphoreType.DMA(()),
                                          pltpu.SemaphoreType.DMA(())])(r)


PACK_L = 1024
PACK_RT = 512


def _add_half(name, g, p, c):
    rh = g.shape[2]

    def body(c_ref, g_ref, p_ref, o_ref):
        o_ref[0] = (g_ref[0, 0] + p_ref[0]).astype(o_ref.dtype)

    gs = pltpu.PrefetchScalarGridSpec(
        num_scalar_prefetch=1, grid=(4, rh // PACK_RT),
        in_specs=[pl.BlockSpec((1, 1, PACK_RT, PACK_L), lambda s, i, c_ref: (s, c_ref[0], i, 0)),
                  pl.BlockSpec((1, PACK_RT, PACK_L), lambda s, i, c_ref: (s, i, 0))],
        out_specs=pl.BlockSpec((1, PACK_RT, PACK_L), lambda s, i, c_ref: (s, i, 0)))
    return pl.pallas_call(body, name=name, grid_spec=gs, out_shape=jax.ShapeDtypeStruct((4, rh, PACK_L), BF16),
                          compiler_params=_cparams(("parallel", "parallel")))(c.reshape(1).astype(jnp.int32), g, p)


def _add_chips(name, q, t, s):
    rh = q.shape[1]

    def body(s_ref, q_ref, t_ref, o_ref):
        o_ref[...] = ((q_ref[0].astype(F32) + t_ref[0].astype(F32)) + t_ref[1].astype(F32)) + t_ref[2].astype(F32)

    gs = pltpu.PrefetchScalarGridSpec(
        num_scalar_prefetch=1, grid=(rh // PACK_RT,),
        in_specs=[pl.BlockSpec((1, PACK_RT, PACK_L), lambda i, s_ref: (s_ref[0], i, 0)),
                  pl.BlockSpec((3, PACK_RT, PACK_L), lambda i, s_ref: (0, i, 0))],
        out_specs=pl.BlockSpec((PACK_RT, PACK_L), lambda i, s_ref: (i, 0)))
    return pl.pallas_call(body, name=name, grid_spec=gs, out_shape=jax.ShapeDtypeStruct((rh, PACK_L), F32),
                          compiler_params=_cparams(("parallel",)))(s.reshape(1).astype(jnp.int32), q, t)


def _sum8(name, a):
    def body(a_ref, o_ref):
        acc = a_ref[0]
        for d in range(1, 8):
            acc = acc + a_ref[d]
        o_ref[...] = acc
    return pl.pallas_call(body, name=name, out_shape=jax.ShapeDtypeStruct(a.shape[1:], F32))(a)


def _silu_rows(name, a):
    def body(a_ref, o_ref):
        o_ref[...] = _silu(a_ref[...])
    return pl.pallas_call(body, name=name, out_shape=jax.ShapeDtypeStruct(a.shape, F32))(a)


BIG = (("ffn_w_in", 3), ("ffn_w_out", 2), ("gdn_w_in", 2), ("gdn_w_out", 1), ("mla_w_dkv", 0), ("mla_w_ukv", 1),
       ("mla_w_dq", 1), ("mla_w_uq", 2), ("mla_w_out", 1))


def _packed_rows(n):
    per_half = -(-n // (2 * PACK_L))
    return -(-per_half // PACK_RT) * PACK_RT


def _pack_flat(flat):
    n = flat.shape[-1]
    rh = _packed_rows(n)
    pad = [(0, 0)] * (flat.ndim - 1) + [(0, 2 * rh * PACK_L - n)]
    return jnp.pad(flat, pad).reshape(flat.shape[:-1] + (2, rh, PACK_L))


def _shards_first(full, axis):
    sh = full.shape
    t = full.reshape(sh[:axis] + (4, sh[axis] // 4) + sh[axis + 1:])
    return jnp.moveaxis(t, axis, 0)


def _shards_merge(stacked, axis):
    t = jnp.moveaxis(stacked, 0, axis)
    sh = t.shape
    return t.reshape(sh[:axis] + (4 * sh[axis + 1],) + sh[axis + 2:])


def _pack_small(parts):
    flat = jnp.concatenate([p.reshape(-1).astype(F32) for p in parts])
    n = flat.shape[0]
    rows = -(-n // (SUB * LANE)) * SUB
    return jnp.pad(flat, (0, rows * LANE - n)).reshape(rows, LANE)


def _unpack_small(buf, shapes):
    lead = buf.shape[:-2]
    flat = buf.reshape(lead + (-1,))
    out, off = [], 0
    for sh in shapes:
        n = 1
        for d in sh:
            n *= d
        out.append(flat[..., off:off + n].reshape(lead + tuple(sh)))
        off += n
    return out


WEIGHTS = ('ada_w', 'ada_b', 'norm_g', 'ffn_w_in', 'ffn_w_out', 'gdn_w_in', 'gdn_conv_w', 'gdn_a_log', 'gdn_dt_bias',
           'gdn_norm_g', 'gdn_w_out', 'kv_ada_w', 'kv_ada_b', 'kv_norm_g', 'mla_w_dkv', 'mla_kv_norm_g', 'mla_w_ukv',
           'mla_k_norm_g', 'mla_w_dq', 'mla_q_lora_norm_g', 'mla_w_uq', 'mla_q_norm_g', 'mla_w_out')
ARGS = ('x', 'c', 'positions') + WEIGHTS + ('loss_target',) + tuple('m_' + n for n in WEIGHTS) + tuple('v_' + n for n in WEIGHTS)


def _split_norm(v):
    return v[None, :DH], _pad_lanes(v[None, DH:], 0)


def _join_norm(gn, gr):
    return jnp.concatenate([gn[0], gr[0, :ROPE]])


def _step(x, tgt, pos, mods, kvmod, W, P):
    tabs, pm = _rope_tables(pos)
    m3 = lambda l, i: tuple(mods[l][3 * i + j][None] for j in range(3))
    ng = lambda l, i: P["norm_g"][l, i][None]
    gdn_p, mla_p = [], []
    for l in range(2):
        gdn_p.append(dict(w_in=jnp.pad(W["gdn_w_in"][l], ((0, 0), (0, GDN_IN - W["gdn_w_in"].shape[2]))),
                          conv_w8=jnp.pad(P["gdn_conv_w"][l], ((0, 4), (0, 0))),
                          a_log128=_pad_lanes(P["gdn_a_log"][l][None], NH), dt_bias128=_pad_lanes(P["gdn_dt_bias"][l][None], NH),
                          norm_g=P["gdn_norm_g"][l][None], w_out=W["gdn_w_out"][l]))
        q_gn, q_gr = _split_norm(P["mla_q_norm_g"][l])
        mla_p.append(dict(w_dq=W["mla_w_dq"][l], ql_g=P["mla_q_lora_norm_g"][l][None],
                          w_uq=jnp.pad(W["mla_w_uq"][l].reshape(QL, NH, QKH), ((0, 0), (0, 0), (0, HP - QKH))).reshape(QL, NH * HP),
                          q_gn=q_gn, q_gr=q_gr, pm=pm, w_out=W["mla_w_out"][l]))
    k_gn, k_gr = _split_norm(P["mla_k_norm_g"])
    kv_p = dict(kv_norm_g=P["kv_norm_g"][None], w_dkv=jnp.pad(W["mla_w_dkv"], ((0, 0), (0, QL - KVL - ROPE))),
                kv_lat_g=P["mla_kv_norm_g"][None],
                w_ukv=W["mla_w_ukv"].reshape(KVL, NH, 2, DH).transpose(0, 2, 1, 3).reshape(KVL, 2 * NH * DH),
                k_gn=k_gn, k_gr=k_gr, pm=pm)
    kvm = (kvmod[0][None], kvmod[1][None])

    res = {}
    for l in range(4):
        x, res[l, 0] = _ffn_fwd(f"l{l}a", x, m3(l, 0), ng(l, 0), W["ffn_w_in"][l, 0], W["ffn_w_out"][l, 0])
        if l < 2:
            x, res[l, 1] = _gdn_layer_fwd(f"l{l}g", x, m3(l, 1), ng(l, 1), gdn_p[l])
        else:
            x, res[l, 1] = _mla_layer_fwd(f"l{l}m", x, m3(l, 1), ng(l, 1), mla_p[l - 2], kp, kvf, tabs)
        x, res[l, 2] = _ffn_fwd(f"l{l}b", x, m3(l, 2), ng(l, 2), W["ffn_w_in"][l, 1], W["ffn_w_out"][l, 1])
        if l == 1:
            kp, kvf, kres = _kv_fwd(x, kvm, kv_p, tabs)
    loss, dx = _loss_head(x, tgt)

    gw = {n: [None] * W[n].shape[0] for n in ("gdn_w_in", "gdn_w_out", "mla_w_dq", "mla_w_uq", "mla_w_out")}
    gw["ffn_w_in"] = [[None, None] for _ in range(4)]
    gw["ffn_w_out"] = [[None, None] for _ in range(4)]
    gp = {n: [None] * 2 for n in ("gdn_conv_w", "gdn_a_log", "gdn_dt_bias", "gdn_norm_g", "mla_q_lora_norm_g", "mla_q_norm_g")}
    gnorm = [[None] * 3 for _ in range(4)]
    dmod = [[None] * NMOD for _ in range(4)]
    dkp = dv = None
    for l in (3, 2, 1, 0):
        if l == 1:
            dx, gk = _kv_bwd(dkp, dv, dx, kres, kvm, kvf, kv_p, tabs)
        for i in (2, 1, 0):
            if i != 1:
                dx, gd = _ffn_bwd(f"l{l}{'ab'[i // 2]}", dx, res[l, i], m3(l, i), ng(l, i), W["ffn_w_in"][l, i // 2],
                                  W["ffn_w_out"][l, i // 2])
                gw["ffn_w_in"][l][i // 2], gw["ffn_w_out"][l][i // 2] = gd["w_in"], gd["w_out"]
            elif l < 2:
                dx, gd = _gdn_layer_bwd(f"l{l}g", dx, res[l, 1], m3(l, 1), ng(l, 1), gdn_p[l])
                gw["gdn_w_in"][l] = gd["w_in"][:, :W["gdn_w_in"].shape[2]]
                gw["gdn_w_out"][l] = gd["w_out"]
                gp["gdn_conv_w"][l] = gd["conv_w8"][:4]
                gp["gdn_a_log"][l] = gd["a_log128"][0, NH:2 * NH]
                gp["gdn_dt_bias"][l] = gd["dt_bias128"][0, NH:2 * NH]
                gp["gdn_norm_g"][l] = gd["norm_g"][0]
            else:
                dx, dkp_l, dv_l, gd = _mla_layer_bwd(f"l{l}m", dx, res[l, 1], m3(l, 1), ng(l, 1), mla_p[l - 2], kp, kvf, tabs)
                dkp = dkp_l if dkp is None else dkp + dkp_l
                dv = dv_l if dv is None else dv + dv_l
                gw["mla_w_dq"][l - 2], gw["mla_w_out"][l - 2] = gd["w_dq"], gd["w_out"]
                gw["mla_w_uq"][l - 2] = gd["w_uq"].reshape(QL, NH, HP)[:, :, :QKH].reshape(QL, NH * QKH)
                gp["mla_q_lora_norm_g"][l - 2] = gd["ql_g"][0]
                gp["mla_q_norm_g"][l - 2] = _join_norm(gd["q_gn"], gd["q_gr"])
            gnorm[l][i] = gd["g"][0]
            for j in range(3):
                dmod[l][3 * i + j] = gd["mod"][j][0]
    gwf = {n: jnp.stack([jnp.stack(r) if isinstance(r, list) else r for r in v]) for n, v in gw.items()}
    gwf["mla_w_dkv"] = gk["w_dkv"][:, :KVL + ROPE]
    gwf["mla_w_ukv"] = gk["w_ukv"].reshape(KVL, 2, NH, DH).transpose(0, 2, 1, 3).reshape(KVL, 2 * NH * DH)
    gpf = {n: jnp.stack(v) for n, v in gp.items()}
    gpf["norm_g"] = jnp.stack([jnp.stack(r) for r in gnorm])
    gpf["kv_norm_g"] = gk["kv_norm_g"][0]
    gpf["mla_kv_norm_g"] = gk["kv_lat_g"][0]
    gpf["mla_k_norm_g"] = _join_norm(gk["k_gn"], gk["k_gr"])
    dmods = jnp.stack([jnp.stack(r) for r in dmod])
    dkvmod = jnp.stack([gk["mod"][0][0], gk["mod"][1][0]])
    return loss, dx, gwf, gpf, dmods, dkvmod


SMALL = ("norm_g", "gdn_conv_w", "gdn_a_log", "gdn_dt_bias", "gdn_norm_g", "kv_norm_g", "mla_kv_norm_g", "mla_k_norm_g",
         "mla_q_lora_norm_g", "mla_q_norm_g")


def kernel(x, c, positions, ada_w, ada_b, norm_g, ffn_w_in, ffn_w_out, gdn_w_in, gdn_conv_w, gdn_a_log, gdn_dt_bias,
           gdn_norm_g, gdn_w_out, kv_ada_w, kv_ada_b, kv_norm_g, mla_w_dkv, mla_kv_norm_g, mla_w_ukv, mla_k_norm_g,
           mla_w_dq, mla_q_lora_norm_g, mla_w_uq, mla_q_norm_g, mla_w_out, loss_target, m_ada_w, m_ada_b, m_norm_g,
           m_ffn_w_in, m_ffn_w_out, m_gdn_w_in, m_gdn_conv_w, m_gdn_a_log, m_gdn_dt_bias, m_gdn_norm_g, m_gdn_w_out,
           m_kv_ada_w, m_kv_ada_b, m_kv_norm_g, m_mla_w_dkv, m_mla_kv_norm_g, m_mla_w_ukv, m_mla_k_norm_g, m_mla_w_dq,
           m_mla_q_lora_norm_g, m_mla_w_uq, m_mla_q_norm_g, m_mla_w_out, v_ada_w, v_ada_b, v_norm_g, v_ffn_w_in,
           v_ffn_w_out, v_gdn_w_in, v_gdn_conv_w, v_gdn_a_log, v_gdn_dt_bias, v_gdn_norm_g, v_gdn_w_out, v_kv_ada_w,
           v_kv_ada_b, v_kv_norm_g, v_mla_w_dkv, v_mla_kv_norm_g, v_mla_w_ukv, v_mla_k_norm_g, v_mla_w_dq,
           v_mla_q_lora_norm_g, v_mla_w_uq, v_mla_q_norm_g, v_mla_w_out):
    a = dict(locals())
    mx, my, mc = _me()
    dev = 4 * mx + 2 * my + mc
    chip = 2 * mx + my
    x, tgt, pos = a["x"][0], a["loss_target"][0], a["positions"][0]
    take = lambda arr, i, axis=0: lax.dynamic_index_in_dim(arr, i, axis, keepdims=False)

    pre = _all_gather8("ag_pre", _pack_small([a["c"], a["gdn_conv_w"], a["norm_g"]]))
    c_all, conv_sh, norm_sh = _unpack_small(pre, [(D,), a["gdn_conv_w"].shape, a["norm_g"].shape])
    P = {n: a[n] for n in SMALL}
    P["gdn_conv_w"] = jnp.concatenate([conv_sh[2 * s] for s in range(4)], axis=2)
    P["norm_g"] = jnp.concatenate([norm_sh[2 * s] for s in range(4)], axis=2)
    c_act = _silu_rows("c_act", c_all)
    nada = a["ada_w"].shape[2]
    nkv = a["kv_ada_w"].shape[1]
    modp = [_matmul(f"mod{l}", c_act, a["ada_w"][l], precise=True) for l in range(4)]
    kvp = _matmul("modkv", c_act, a["kv_ada_w"], precise=True)
    mp = _all_gather8("ag_mod", _pack_small(modp + [kvp]))
    modp_all, kvp_all = _unpack_small(mp, [(4, 8, nada), (8, nkv)])
    mods = jnp.concatenate([take(modp_all[2 * s], dev, 1) for s in range(4)], axis=1) + a["ada_b"]
    mods = mods.reshape(4, NMOD, D)
    kvmod = (jnp.concatenate([take(kvp_all[2 * s], dev, 0) for s in range(4)]) + a["kv_ada_b"]).reshape(2, D)

    flat = jnp.concatenate([a[n].reshape(-1).astype(BF16) for n, _ in BIG])
    wall = _gather_weights("ag_w", _pack_flat(flat)).reshape(4, -1)
    W, off = {}, 0
    for n, ax in BIG:
        sz = a[n].size
        W[n] = _shards_merge(wall[:, off:off + sz].reshape((4,) + a[n].shape), ax)
        off += sz

    loss, dx, gw, gp, dmods, dkvmod = _step(x, tgt, pos, mods, kvmod, W, P)
    loss = lax.psum(loss, ("x", "y", "c"))

    gflat = jnp.concatenate([_shards_first(gw[n], ax).reshape(4, -1) for n, ax in BIG], axis=1)
    g4 = _pack_flat(gflat)
    q = _add_half("rs_pair", g4, _exchange_half("rs_x1", g4), mc)
    r = _add_chips("rs_chips", q, _scatter_chips("rs_x2", q), chip)
    gsh = _exchange_full("rs_x3", r).reshape(-1)
    grads, off = {}, 0
    for n, _ in BIG:
        grads[n] = gsh[off:off + a[n].size].reshape(a[n].shape)
        off += a[n].size

    small = _all_gather8("ag_small", _pack_small([dmods, dkvmod] + [gp[n] for n in SMALL]))
    shapes = [(4, NMOD * D), (2 * D,)] + [gp[n].shape for n in SMALL]
    dmod_all, dkv_all = _unpack_small(small, shapes)[:2]
    tot = _unpack_small(_sum8("sum_small", small), shapes)
    grads["ada_b"], grads["kv_ada_b"] = tot[0], tot[1]
    for n, t in zip(SMALL, tot[2:]):
        grads[n] = t
    grads["norm_g"] = lax.dynamic_slice_in_dim(grads["norm_g"], chip * a["norm_g"].shape[2], a["norm_g"].shape[2], 2)
    grads["gdn_conv_w"] = lax.dynamic_slice_in_dim(grads["gdn_conv_w"], chip * a["gdn_conv_w"].shape[2],
                                                   a["gdn_conv_w"].shape[2], 2)
    ca = jnp.pad(c_act, ((0, LANE - 8), (0, 0)))
    dm = jnp.pad(lax.dynamic_slice_in_dim(dmod_all.reshape(8, 4, NMOD * D), chip * nada, nada, 2), ((0, LANE - 8), (0, 0), (0, 0)))
    grads["ada_w"] = jnp.stack([_matmul(f"gada{l}", ca, dm[:, l], "tn", precise=True) for l in range(4)])
    dk = jnp.pad(lax.dynamic_slice_in_dim(dkv_all, chip * nkv, nkv, 1), ((0, LANE - 8), (0, 0)))
    grads["kv_ada_w"] = _matmul("gadakv", ca, dk, "tn", precise=True)

    upd = [_adamw("adamw_" + n, a[n], grads[n], a["m_" + n], a["v_" + n]) for n in WEIGHTS]
    return (loss, dx[None], *[grads[n] for n in WEIGHTS], *[u[0] for u in upd], *[u[1] for u in upd], *[u[2] for u in upd])
```

```python
import functools

import jax
import jax.numpy as jnp
from jax import lax
from jax.experimental import pallas as pl
from jax.experimental.pallas import tpu as pltpu

F32 = jnp.float32
BF16 = jnp.bfloat16
HI = lax.Precision.HIGHEST
MESH = pl.DeviceIdType.MESH

D = 1024
NH = 8
DH = 128
FF = 2816
NMOD = 9
CHUNK = 64
ROPE = 64
QKH = 192
HP = 256
KVL = 256
QL = 384
GDN_IN = 4224
GATE_CB = 32
EPS = 1e-6
ROPE_BASE = 10000.0
LANE = 128
SUB = 8
VMEM_LIMIT = 56 * 1024 * 1024

ADAM_LR, ADAM_B1, ADAM_B2, ADAM_EPS, ADAM_WD, ADAM_STEP = 0.001, 0.9, 0.999, 1e-08, 0.01, 10


def _tile(n, prefs=(512, 384, 256, 128)):
    for p in prefs:
        if n % p == 0:
            return p
    return n


def _cparams(sem):
    return pltpu.CompilerParams(dimension_semantics=sem, vmem_limit_bytes=VMEM_LIMIT)


class Row:
    def __init__(self, arr, width=None, cb=0, splits=None, halo=None):
        self.arr = arr
        self.width = arr.shape[1] if width is None else width
        self.cb = cb
        self.splits = splits
        self.halo = halo


def _rowwise(name, fn, rows, bcs, outs, accs, tm):
    S = rows[0].arr.shape[0]
    n = S // tm
    nr, nb, no, na = len(rows), len(bcs), len(outs), len(accs)

    def body(*refs):
        rrefs, brefs = refs[:nr], refs[nr:nr + nb]
        orefs, arefs = refs[nr + nb:nr + nb + no], refs[nr + nb + no:]
        pieces = []
        for r, ref in zip(rows, rrefs):
            if r.splits is None:
                pieces.append(ref[...])
            else:
                off = 0
                for w in r.splits:
                    pieces.append(ref[:, off:off + w])
                    off += w
        out_pieces, acc_vals = fn(pieces, [b[...] for b in brefs])
        k = 0
        for (widths, dt), oref in zip(outs, orefs):
            off = 0
            for w in widths:
                oref[:, off:off + w] = out_pieces[k].astype(dt)
                k += 1
                off += w
        if na:
            @pl.when(pl.program_id(0) == 0)
            def _():
                for a in arefs:
                    a[...] = jnp.zeros(a.shape, F32)
            for a, v in zip(arefs, acc_vals):
                a[...] += v

    in_specs = []
    for r in rows:
        if r.halo is None:
            in_specs.append(pl.BlockSpec((tm, r.width), lambda i, cb=r.cb: (i, cb)))
        elif r.halo == "prev":
            in_specs.append(pl.BlockSpec((SUB, r.width), lambda i, cb=r.cb: (jnp.maximum(i * (tm // SUB) - 1, 0), cb)))
        else:
            in_specs.append(pl.BlockSpec((SUB, r.width), lambda i, cb=r.cb: (jnp.minimum((i + 1) * (tm // SUB), S // SUB - 1), cb)))
    in_specs += [pl.BlockSpec(b.shape, lambda i, nd=b.ndim: (0,) * nd) for b in bcs]
    out_specs = [pl.BlockSpec((tm, sum(w)), lambda i: (i, 0)) for w, _ in outs]
    out_specs += [pl.BlockSpec(s, lambda i: (0, 0)) for s in accs]
    out_shape = [jax.ShapeDtypeStruct((S, sum(w)), dt) for w, dt in outs]
    out_shape += [jax.ShapeDtypeStruct(s, F32) for s in accs]
    res = pl.pallas_call(body, name=name, grid=(n,), in_specs=in_specs, out_specs=out_specs, out_shape=out_shape,
                         compiler_params=_cparams(("arbitrary",)))(*[r.arr for r in rows], *bcs)
    return res


def _rw_fwd(name, f, rows, bcs, outs, tm):
    def fn(pieces, bvals):
        return list(f(*[p.astype(F32) for p in pieces], *[b.astype(F32) for b in bvals])), []
    return _rowwise(name, fn, rows, bcs, outs, [], tm)


def _npieces(rows):
    return sum(1 if r.splits is None else len(r.splits) for r in rows)


def _rw_bwd(name, f, rows, bcs, cts, drow, dbc, outs, tm, add=None):
    np_, nct = _npieces(rows), _npieces(cts)

    def fn(pieces, bvals):
        allv = [p.astype(F32) for p in pieces[:np_]] + [b.astype(F32) for b in bvals]
        ct = [p.astype(F32) for p in pieces[np_:np_ + nct]]
        didx = [i for i, m in enumerate(list(drow) + list(dbc)) if m]

        def g(*dv):
            full = list(allv)
            for i, v in zip(didx, dv):
                full[i] = v
            return tuple(f(*full))

        _, vjp = jax.vjp(g, *[allv[i] for i in didx])
        grads = vjp(tuple(ct))
        nrd = sum(bool(m) for m in drow)
        rg, bg = list(grads[:nrd]), list(grads[nrd:])
        if add is not None:
            rg[0] = rg[0] + pieces[np_ + nct].astype(F32)
        return rg, bg

    accs = [b.shape for b, m in zip(bcs, dbc) if m]
    return _rowwise(name, fn, list(rows) + list(cts) + ([add] if add is not None else []), bcs, outs, accs, tm)


def _sigmoid(x):
    return 1.0 / (1.0 + jnp.exp(-x))


def _silu(x):
    return x * _sigmoid(x)


def _softplus(x):
    return jnp.maximum(x, 0.0) + jnp.log(1.0 + jnp.exp(-jnp.abs(x)))


def f_mod(x, g, shift, scale):
    y = x * lax.rsqrt(jnp.mean(x * x, axis=-1, keepdims=True) + EPS)
    return (y * g * (1.0 + scale) + shift,)


def f_rms(x, g):
    return (x * lax.rsqrt(jnp.mean(x * x, axis=-1, keepdims=True) + EPS) * g,)


def f_act(gate, up):
    return (_silu(gate) * up,)


def make_f_res(coef):
    def f_res(y, gate):
        return (coef * gate * y,)
    return f_res


def f_gdnpre(*p):
    out = []
    for i, t in enumerate(p):
        t = _silu(t)
        if i < 2 * NH:
            t = t * lax.rsqrt(jnp.sum(t * t, axis=-1, keepdims=True) + EPS)
        out.append(t)
    return tuple(out)


def f_gates(gates, a_log, dt_bias):
    return _sigmoid(gates), -jnp.exp(a_log) * _softplus(gates + dt_bias)


def f_gdnpost(*a):
    o, z, g = a[:NH], a[NH:2 * NH], a[2 * NH]
    out = []
    for oh, zh in zip(o, z):
        y = oh * lax.rsqrt(jnp.mean(oh * oh, axis=-1, keepdims=True) + EPS) * g
        out.append(y * _silu(zh))
    return tuple(out)


def make_f_qk(shared_rope):
    def f(*a):
        if shared_rope:
            ns, rs = a[:NH], [a[NH]] * NH
            cosp, sins, gn, gr, pm = a[NH + 1:NH + 6]
        else:
            ns, rs = a[0:2 * NH:2], a[1:2 * NH:2]
            cosp, sins, gn, gr, pm = a[2 * NH:2 * NH + 5]
        out = []
        for n, r in zip(ns, rs):
            ss = jnp.sum(n * n, axis=-1, keepdims=True) + jnp.sum(r * r, axis=-1, keepdims=True)
            rstd = lax.rsqrt(ss * (1.0 / QKH) + EPS)
            yn = n * rstd * gn
            yr = r * rstd * gr
            sw = jnp.dot(yr, pm, precision=HI, preferred_element_type=F32)
            out += [yn, yr * cosp + sw * sins]
        return tuple(out)
    return f


def _matmul(name, a, b, mode="nn", out_dtype=F32, precise=False, lay=None, li=0, into=None, nmat=1):
    if lay == "b_cols":
        per = b.shape[3]
        rb, cb = b.shape[2], 4 * per
    elif lay == "b_rows":
        per = b.shape[2]
        rb, cb = 4 * per, b.shape[3]
    else:
        rb, cb = b.shape
    if mode == "nn":
        (M, K), N = a.shape, cb
    elif mode == "nt":
        (M, K), N = a.shape, rb
    else:
        (K, M), N = a.shape, cb
    tm = _tile(M, (1024, 512, 256, 128))
    tn = _tile(N, (1024, 512, 384, 256, 128))
    tk = _tile(K, (1408, 1024, 512, 384, 256, 128))
    if lay == "b_cols":
        tn, tk = (per, tk) if mode == "nn" else (tn, per)
    elif lay == "b_rows":
        tm, tn, tk = (tm, 512, K) if mode == "nn" else (min(tm, 512), N, tk)
    elif lay == "o_cols":
        per = N // 4
        tn = per
    elif lay == "o_rows":
        per = M // 4
        tm, tn = M, 512
    nk = K // tk
    dims = {"nn": (((1,), (0,)), ((), ())), "nt": (((1,), (1,)), ((), ())), "tn": (((0,), (0,)), ((), ()))}[mode]

    def body(a_ref, b_ref, *rest):
        o_ref, acc_ref = rest[-2:]
        k = pl.program_id(2)

        @pl.when(k == 0)
        def _():
            acc_ref[...] = jnp.zeros(acc_ref.shape, F32)

        bv = b_ref[...]
        if lay == "b_rows":
            bv = bv.reshape(4 * per, bv.shape[2])
        if precise:
            acc_ref[...] += lax.dot_general(a_ref[...].astype(F32), bv.astype(F32), dims, precision=HI,
                                            preferred_element_type=F32)
        else:
            acc_ref[...] += lax.dot_general(a_ref[...].astype(BF16), bv.astype(BF16), dims, preferred_element_type=F32)

        @pl.when(k == nk - 1)
        def _():
            if lay == "o_rows":
                for s in range(4):
                    o_ref[s] = acc_ref[s * per:(s + 1) * per, :].astype(o_ref.dtype)
            else:
                o_ref[...] = acc_ref[...].astype(o_ref.dtype)

    a_spec = pl.BlockSpec((tk, tm), lambda i, j, k: (k, i)) if mode == "tn" else pl.BlockSpec((tm, tk), lambda i, j, k: (i, k))
    if lay == "b_cols":
        b_spec = (pl.BlockSpec((None, None, tk, per), lambda i, j, k: (j, li, k, 0)) if mode == "nn" else
                  pl.BlockSpec((None, None, tn, per), lambda i, j, k: (k, li, j, 0)))
    elif lay == "b_rows":
        b_spec = (pl.BlockSpec((4, None, per, tn), lambda i, j, k: (0, li, 0, j)) if mode == "nn" else
                  pl.BlockSpec((4, None, per, tk), lambda i, j, k: (0, li, 0, k)))
    elif mode == "nt":
        b_spec = pl.BlockSpec((tn, tk), lambda i, j, k: (j, k))
    else:
        b_spec = pl.BlockSpec((tk, tn), lambda i, j, k: (k, j))
    if lay == "o_cols":
        o_spec = pl.BlockSpec((None, None, tm, per), lambda i, j, k: (j, li, i, 0))
        o_shape = jax.ShapeDtypeStruct((4, nmat, M, per), out_dtype)
    elif lay == "o_rows":
        o_spec = pl.BlockSpec((4, None, per, tn), lambda i, j, k: (0, li, 0, j))
        o_shape = jax.ShapeDtypeStruct((4, nmat, per, N), out_dtype)
    else:
        o_spec = pl.BlockSpec((tm, tn), lambda i, j, k: (i, j))
        o_shape = jax.ShapeDtypeStruct((M, N), out_dtype)
    in_specs, args, alias = [a_spec, b_spec], [a, b], {}
    if into is not None:
        in_specs.append(pl.BlockSpec(memory_space=pl.ANY))
        args.append(into)
        alias = {2: 0}
    return pl.pallas_call(body, name=name, grid=(M // tm, N // tn, nk), in_specs=in_specs, out_specs=o_spec,
                          out_shape=o_shape, scratch_shapes=[pltpu.VMEM((tm, tn), F32)], input_output_aliases=alias,
                          compiler_params=_cparams(("parallel", "parallel", "arbitrary")))(*args)


def _shift_down(t, p, d):
    if d == 0:
        return t
    tr = pltpu.roll(t, d, 0)
    pr = pltpu.roll(p, d, 0)
    r8 = lax.broadcasted_iota(jnp.int32, p.shape, 0)
    first = jnp.where(r8 < d, pr, tr[:SUB])
    return jnp.concatenate([first, tr[SUB:]], axis=0)


def _shift_up(t, nx, d):
    if d == 0:
        return t
    tm = t.shape[0]
    tr = pltpu.roll(t, tm - d, 0)
    nr = pltpu.roll(nx, SUB - d, 0)
    r8 = lax.broadcasted_iota(jnp.int32, nx.shape, 0)
    last = jnp.where(r8 >= SUB - d, nr, tr[tm - SUB:])
    return jnp.concatenate([tr[:tm - SUB], last], axis=0)


def _conv_fwd(name, proj, w8, C, tm):
    def fn(pieces, bvals):
        t, p = pieces[0].astype(F32), pieces[1].astype(F32)
        w = bvals[0]
        p = jnp.where(pl.program_id(0) == 0, 0.0, p)
        out = w[3:4] * t
        for d in (1, 2, 3):
            out = out + w[3 - d:4 - d] * _shift_down(t, p, d)
        return [out], []
    return _rowwise(name, fn, [Row(proj, C), Row(proj, C, halo="prev")], [w8], [((C,), F32)], [], tm)[0]


def _conv_bwd(name, proj, dout, w8, C, tm, out_dtype):
    n = proj.shape[0] // tm

    def fn(pieces, bvals):
        t, p, g, gn = [v.astype(F32) for v in pieces]
        w = bvals[0]
        i = pl.program_id(0)
        p = jnp.where(i == 0, 0.0, p)
        gn = jnp.where(i == n - 1, 0.0, gn)
        dx = w[3:4] * g
        dws = [jnp.sum(g * t, axis=0, keepdims=True)]
        for d in (1, 2, 3):
            dx = dx + w[3 - d:4 - d] * _shift_up(g, gn, d)
            dws.append(jnp.sum(g * _shift_down(t, p, d), axis=0, keepdims=True))
        dw = jnp.concatenate([dws[3], dws[2], dws[1], dws[0], jnp.zeros((4, g.shape[1]), F32)], axis=0)
        return [dx], [dw]
    return _rowwise(name, fn, [Row(proj, C), Row(proj, C, halo="prev"), Row(dout), Row(dout, halo="next")], [w8],
                    [((C,), out_dtype)], [(SUB, C)], tm)


def _bdot(a, b, ca, cb, exact):
    dims = (((ca,), (cb,)), ((0,), (0,)))
    if exact:
        return lax.dot_general(a, b, dims, precision=HI, preferred_element_type=F32)
    return lax.dot_general(a.astype(BF16), b.astype(BF16), dims, preferred_element_type=F32)


def _gdn_chunk(q, k, v, gcol, grow, bcol, S):
    C = CHUNK
    ii = lax.broadcasted_iota(jnp.int32, (1, C, C), 1)
    jj = lax.broadcasted_iota(jnp.int32, (1, C, C), 2)
    incl, strict = ii >= jj, ii > jj
    gc_col = jnp.sum(jnp.where(incl, 1.0, 0.0) * grow, axis=2, keepdims=True)
    gc_row = jnp.sum(jnp.where(jj >= ii, 1.0, 0.0) * gcol, axis=1, keepdims=True)
    decay = jnp.where(incl, jnp.exp(jnp.where(incl, gc_col - gc_row, 0.0)), 0.0)
    qs = q * (DH ** -0.5)
    kb = k * bcol
    nl = -jnp.where(strict, _bdot(kb, k, 2, 2, False) * decay, 0.0)
    T = jnp.where(ii == jj, 1.0, 0.0) + nl
    pw = nl
    for _ in range(5):
        pw = _bdot(pw, pw, 2, 1, True)
        T = T + _bdot(T, pw, 2, 1, True)
    egc = jnp.exp(gc_col)
    u = _bdot(T, v * bcol, 2, 1, True)
    w = _bdot(T, kb * egc, 2, 1, True)
    att = jnp.where(incl, _bdot(qs, k, 2, 2, False) * decay, 0.0)
    v_new = u - _bdot(w, S, 2, 1, False)
    o = _bdot(qs * egc, S, 2, 1, False) + _bdot(att, v_new, 2, 1, False)
    g_last = jnp.sum(grow, axis=2, keepdims=True)
    k_dec = k * jnp.exp(g_last - gc_col)
    S_out = S * jnp.exp(g_last) + _bdot(k_dec, v_new, 1, 1, False)
    return o, S_out


def _heads(ref, w):
    return jnp.stack([ref[:, h * w:(h + 1) * w] for h in range(NH)])


def _gdn_specs(NC, rev):
    ix = (lambda i: NC - 1 - i) if rev else (lambda i: i)
    wide = pl.BlockSpec((CHUNK, D), lambda i: (ix(i), 0))
    col = pl.BlockSpec((CHUNK, NH), lambda i: (ix(i), 0))
    row = pl.BlockSpec((1, NH, CHUNK), lambda i: (ix(i), 0, 0))
    st = pl.BlockSpec((1, NH, DH, DH), lambda i: (ix(i), 0, 0, 0))
    return wide, col, row, st


def _gdn_fwd(name, q, k, v, gcol, grow, bcol):
    S = q.shape[0]
    NC = S // CHUNK

    def body(q_ref, k_ref, v_ref, gc_ref, gr_ref, b_ref, o_ref, ss_ref, st):
        @pl.when(pl.program_id(0) == 0)
        def _():
            st[...] = jnp.zeros(st.shape, F32)
        s_in = st[...]
        ss_ref[0] = s_in
        grow = jnp.stack([gr_ref[0, h:h + 1, :] for h in range(NH)])
        o, s_out = _gdn_chunk(_heads(q_ref, DH), _heads(k_ref, DH), _heads(v_ref, DH), _heads(gc_ref, 1), grow,
                              _heads(b_ref, 1), s_in)
        for h in range(NH):
            o_ref[:, h * DH:(h + 1) * DH] = o[h]
        st[...] = s_out

    wide, col, row, stsp = _gdn_specs(NC, False)
    return pl.pallas_call(body, name=name, grid=(NC,), in_specs=[wide, wide, wide, col, row, col],
                          out_specs=[wide, stsp],
                          out_shape=[jax.ShapeDtypeStruct((S, D), F32), jax.ShapeDtypeStruct((NC, NH, DH, DH), F32)],
                          scratch_shapes=[pltpu.VMEM((NH, DH, DH), F32)],
                          compiler_params=_cparams(("arbitrary",)))(q, k, v, gcol, grow, bcol)


def _gdn_bwd(name, q, k, v, gcol, grow, bcol, ssave, do):
    S = q.shape[0]
    NC = S // CHUNK

    def body(q_ref, k_ref, v_ref, gc_ref, gr_ref, b_ref, ss_ref, do_ref, dq_ref, dk_ref, dv_ref, dgc_ref, dgr_ref, db_ref, dst):
        @pl.when(pl.program_id(0) == 0)
        def _():
            dst[...] = jnp.zeros(dst.shape, F32)
        grow = jnp.stack([gr_ref[0, h:h + 1, :] for h in range(NH)])
        prim = (_heads(q_ref, DH), _heads(k_ref, DH), _heads(v_ref, DH), _heads(gc_ref, 1), grow, _heads(b_ref, 1), ss_ref[0])
        _, vjp = jax.vjp(_gdn_chunk, *prim)
        dq, dk, dv, dgc, dgr, db, ds = vjp((_heads(do_ref, DH), dst[...]))
        for h in range(NH):
            hs = slice(h * DH, (h + 1) * DH)
            dq_ref[:, hs] = dq[h]
            dk_ref[:, hs] = dk[h]
            dv_ref[:, hs] = dv[h]
            dgc_ref[:, h:h + 1] = dgc[h]
            dgr_ref[0, h:h + 1, :] = dgr[h]
            db_ref[:, h:h + 1] = db[h]
        dst[...] = ds

    wide, col, row, stsp = _gdn_specs(NC, True)
    return pl.pallas_call(body, name=name, grid=(NC,), in_specs=[wide, wide, wide, col, row, col, stsp, wide],
                          out_specs=[wide, wide, wide, col, row, col],
                          out_shape=[jax.ShapeDtypeStruct((S, D), F32)] * 3 + [jax.ShapeDtypeStruct((S, NH), F32),
                                                                                 jax.ShapeDtypeStruct((NC, NH, CHUNK), F32),
                                                                                 jax.ShapeDtypeStruct((S, NH), F32)],
                          scratch_shapes=[pltpu.VMEM((NH, DH, DH), F32)],
                          compiler_params=_cparams(("arbitrary",)))(q, k, v, gcol, grow, bcol, ssave, do)


TQ = 512
SM_SCALE = QKH ** -0.5
NEG = -1e30


def _diag_mask(transposed):
    r = lax.broadcasted_iota(jnp.int32, (TQ, TQ), 0) // CHUNK
    c = lax.broadcasted_iota(jnp.int32, (TQ, TQ), 1) // CHUNK
    return (r <= c) if transposed else (c <= r)


def _dot_nt(a, b):
    return lax.dot_general(a, b, (((1,), (1,)), ((), ())), preferred_element_type=F32)


def _flash_fwd(name, qp, kp, kv):
    S = qp.shape[0]
    nq = S // TQ

    def body(q_ref, k_ref, v_ref, o_ref, lse_ref):
        qi = pl.program_id(1)
        q = q_ref[...]

        def step(j, carry, masked):
            m, l, acc = carry
            rows = pl.ds(pl.multiple_of(j * TQ, TQ), TQ)
            s = _dot_nt(q, k_ref[rows, :]) * SM_SCALE
            if masked:
                s = jnp.where(_diag_mask(False), s, NEG)
            m_new = jnp.maximum(m, jnp.max(s, axis=-1, keepdims=True))
            p = jnp.exp(s - m_new)
            alpha = jnp.exp(m - m_new)
            l = alpha * l + jnp.sum(p, axis=-1, keepdims=True)
            acc = alpha * acc + jnp.dot(p.astype(BF16), v_ref[rows, :].astype(BF16), preferred_element_type=F32)
            return m_new, l, acc

        carry = (jnp.full((TQ, 1), NEG, F32), jnp.zeros((TQ, 1), F32), jnp.zeros((TQ, DH), F32))
        carry = lax.fori_loop(0, qi, lambda j, c: step(j, c, False), carry)
        m, l, acc = step(qi, carry, True)
        o_ref[...] = acc / l
        lse_ref[0] = m + jnp.log(l)

    return pl.pallas_call(
        body, name=name, grid=(NH, nq),
        in_specs=[pl.BlockSpec((TQ, HP), lambda h, i: (i, h)), pl.BlockSpec((S, HP), lambda h, i: (0, h)),
                  pl.BlockSpec((S, DH), lambda h, i: (0, NH + h))],
        out_specs=[pl.BlockSpec((TQ, DH), lambda h, i: (i, h)), pl.BlockSpec((1, TQ, 1), lambda h, i: (h, i, 0))],
        out_shape=[jax.ShapeDtypeStruct((S, NH * DH), F32), jax.ShapeDtypeStruct((NH, S, 1), F32)],
        compiler_params=_cparams(("parallel", "arbitrary")))(qp, kp, kv)


def _flash_bwd_dq(name, qp, kp, kv, o, do, lse):
    S = qp.shape[0]
    nq = S // TQ

    def body(q_ref, k_ref, v_ref, o_ref, do_ref, lse_ref, dq_ref, dl_ref):
        qi = pl.program_id(1)
        q = q_ref[...]
        do = do_ref[...]
        delta = jnp.sum(o_ref[...] * do, axis=-1, keepdims=True)
        dl_ref[0] = delta
        dob = do.astype(BF16)
        lse = lse_ref[0]

        def step(j, dq, masked):
            rows = pl.ds(pl.multiple_of(j * TQ, TQ), TQ)
            k = k_ref[rows, :]
            s = _dot_nt(q, k) * SM_SCALE
            if masked:
                s = jnp.where(_diag_mask(False), s, NEG)
            p = jnp.exp(s - lse)
            dp = _dot_nt(dob, v_ref[rows, :].astype(BF16))
            ds = p * (dp - delta) * SM_SCALE
            return dq + jnp.dot(ds.astype(BF16), k, preferred_element_type=F32)

        dq = lax.fori_loop(0, qi, lambda j, c: step(j, c, False), jnp.zeros((TQ, HP), F32))
        dq_ref[...] = step(qi, dq, True)

    return pl.pallas_call(
        body, name=name, grid=(NH, nq),
        in_specs=[pl.BlockSpec((TQ, HP), lambda h, i: (i, h)), pl.BlockSpec((S, HP), lambda h, i: (0, h)),
                  pl.BlockSpec((S, DH), lambda h, i: (0, NH + h)), pl.BlockSpec((TQ, DH), lambda h, i: (i, h)),
                  pl.BlockSpec((TQ, DH), lambda h, i: (i, h)), pl.BlockSpec((1, TQ, 1), lambda h, i: (h, i, 0))],
        out_specs=[pl.BlockSpec((TQ, HP), lambda h, i: (i, h)), pl.BlockSpec((1, TQ, 1), lambda h, i: (h, i, 0))],
        out_shape=[jax.ShapeDtypeStruct((S, NH * HP), F32), jax.ShapeDtypeStruct((NH, S, 1), F32)],
        compiler_params=_cparams(("parallel", "arbitrary")))(qp, kp, kv, o, do, lse)


def _flash_bwd_dkv(name, qp, kp, kv, do, lse_row, delta_row):
    S = qp.shape[0]
    nq = S // TQ

    def body(q_ref, k_ref, v_ref, do_ref, lse_ref, dl_ref, dk_ref, dv_ref):
        kj = pl.program_id(1)
        k = k_ref[...]
        vb = v_ref[...].astype(BF16)

        def step(i, carry, masked):
            dk, dv = carry
            rows = pl.ds(pl.multiple_of(i * TQ, TQ), TQ)
            q = q_ref[rows, :]
            dob = do_ref[rows, :].astype(BF16)
            st = _dot_nt(k, q) * SM_SCALE
            pt = jnp.exp(st - lse_ref[0, :, rows])
            if masked:
                pt = jnp.where(_diag_mask(True), pt, 0.0)
            dv = dv + jnp.dot(pt.astype(BF16), dob, preferred_element_type=F32)
            dpt = _dot_nt(vb, dob)
            dst = pt * (dpt - dl_ref[0, :, rows]) * SM_SCALE
            dk = dk + jnp.dot(dst.astype(BF16), q, preferred_element_type=F32)
            return dk, dv

        carry = step(kj, (jnp.zeros((TQ, HP), F32), jnp.zeros((TQ, DH), F32)), True)
        dk, dv = lax.fori_loop(kj + 1, nq, lambda i, c: step(i, c, False), carry)
        dk_ref[...] = dk
        dv_ref[...] = dv

    return pl.pallas_call(
        body, name=name, grid=(NH, nq),
        in_specs=[pl.BlockSpec((S, HP), lambda h, j: (0, h)), pl.BlockSpec((TQ, HP), lambda h, j: (j, h)),
                  pl.BlockSpec((TQ, DH), lambda h, j: (j, NH + h)), pl.BlockSpec((S, DH), lambda h, j: (0, h)),
                  pl.BlockSpec((1, 1, S), lambda h, j: (h, 0, 0)), pl.BlockSpec((1, 1, S), lambda h, j: (h, 0, 0))],
        out_specs=[pl.BlockSpec((TQ, HP), lambda h, j: (j, h)), pl.BlockSpec((TQ, DH), lambda h, j: (j, h))],
        out_shape=[jax.ShapeDtypeStruct((S, NH * HP), F32), jax.ShapeDtypeStruct((S, NH * DH), F32)],
        compiler_params=_cparams(("parallel", "arbitrary")))(qp, kp, kv, do, lse_row, delta_row)


def _tm(S, width):
    t = 512 if width <= 1024 else (256 if width <= 3072 else 128)
    return min(t, S)


def _mod_fwd(tag, x, g, shift, scale):
    S = x.shape[0]
    return _rw_fwd(tag + "_mod", f_mod, [Row(x)], [g, shift, scale], [((D,), BF16)], _tm(S, D))[0]


def _mod_bwd(tag, x, g, shift, scale, dh, dx_direct):
    S = x.shape[0]
    r = _rw_bwd(tag + "_mod_b", f_mod, [Row(x)], [g, shift, scale], [Row(dh)], [True], [True] * 3, [((D,), F32)],
                _tm(S, D), add=Row(dx_direct))
    return r[0], r[1:]


def _res_fwd(tag, x, y, gate, coef):
    S = x.shape[0]

    def fn(pieces, bvals):
        return [pieces[0] + coef * bvals[0] * pieces[1]], []
    return _rowwise(tag + "_res", fn, [Row(x), Row(y)], [gate], [((D,), F32)], [], _tm(S, D))[0]


def _res_bwd(tag, y, gate, dxn, coef):
    S = y.shape[0]
    r = _rw_bwd(tag + "_res_b", make_f_res(coef), [Row(y)], [gate], [Row(dxn)], [True], [True], [((D,), BF16)], _tm(S, D))
    return r[0], r[1]


def _ffn_fwd(tag, x, mod3, g, w_in4, w_out4, li):
    shift, scale, gate = mod3
    S = x.shape[0]
    h = _mod_fwd(tag, x, g, shift, scale)
    gu = _matmul(tag + "_in", h, w_in4, lay="b_cols", li=li)
    a = _rw_fwd(tag + "_act", f_act, [Row(gu, splits=[FF, FF])], [], [((FF,), BF16)], _tm(S, 2 * FF))[0]
    y = _matmul(tag + "_out", a, w_out4, lay="b_rows", li=li)
    xn = _res_fwd(tag, x, y, gate, 0.5)
    return xn, (x, h, gu, a, y)


def _ffn_bwd(tag, dxn, res, mod3, g, w_in4, w_out4, li, g_in4, g_out4):
    shift, scale, gate = mod3
    x, h, gu, a, y = res
    S = x.shape[0]
    nmat = w_in4.shape[1]
    dy, dgate = _res_bwd(tag, y, gate, dxn, 0.5)
    da = _matmul(tag + "_out_bi", dy, w_out4, "nt", lay="b_rows", li=li)
    g_out4 = _matmul(tag + "_out_bw", a, dy, "tn", lay="o_rows", li=li, into=g_out4, nmat=nmat)
    dgu = _rw_bwd(tag + "_act_b", f_act, [Row(gu, splits=[FF, FF])], [], [Row(da)], [True, True], [],
                  [((FF, FF), BF16)], _tm(S, 2 * FF))[0]
    dh = _matmul(tag + "_in_bi", dgu, w_in4, "nt", lay="b_cols", li=li)
    g_in4 = _matmul(tag + "_in_bw", h, dgu, "tn", lay="o_cols", li=li, into=g_in4, nmat=nmat)
    dx, (dg, dshift, dscale) = _mod_bwd(tag, x, g, shift, scale, dh, dxn)
    return dx, g_in4, g_out4, dict(g=dg, mod=(dshift, dscale, dgate))


def _pad_lanes(a, lo, width=LANE):
    return jnp.pad(a, ((0, 0), (lo, width - lo - a.shape[1])))


def _gdn_layer_fwd(tag, x, mod3, g, p):
    shift, scale, gate = mod3
    S = x.shape[0]
    NC = S // CHUNK
    h = _mod_fwd(tag, x, g, shift, scale)
    proj = _matmul(tag + "_in", h, p["w_in"])
    qc = _conv_fwd(tag + "_conv", proj, p["conv_w8"], 3 * D, _tm(S, 3 * D))
    q, k, v = _rw_fwd(tag + "_pre", f_gdnpre, [Row(qc, splits=[DH] * (3 * NH))], [],
                      [((DH,) * NH, F32)] * 3, _tm(S, 3 * D))
    betaf, gf = _rw_fwd(tag + "_gates", f_gates, [Row(proj, LANE, cb=GATE_CB)], [p["a_log128"], p["dt_bias128"]],
                        [((LANE,), F32)] * 2, _tm(S, LANE))
    bcol, gcol = betaf[:, :NH], gf[:, NH:2 * NH]
    grow = gcol.reshape(NC, CHUNK, NH).transpose(0, 2, 1)
    o, ssave = _gdn_fwd(tag + "_core", q, k, v, gcol, grow, bcol)
    on = _rw_fwd(tag + "_post", f_gdnpost, [Row(o, splits=[DH] * NH), Row(proj, D, cb=3, splits=[DH] * NH)],
                 [p["norm_g"]], [((DH,) * NH, BF16)], _tm(S, 2 * D))[0]
    y = _matmul(tag + "_out", on, p["w_out"])
    xn = _res_fwd(tag, x, y, gate, 1.0)
    return xn, (x, h, proj, qc, q, k, v, gcol, grow, bcol, ssave, o, on, y)


def _gdn_layer_bwd(tag, dxn, res, mod3, g, p):
    shift, scale, gate = mod3
    x, h, proj, qc, q, k, v, gcol, grow, bcol, ssave, o, on, y = res
    S = x.shape[0]
    dy, dgate = _res_bwd(tag, y, gate, dxn, 1.0)
    don = _matmul(tag + "_out_bi", dy, p["w_out"], "nt")
    dw_out = _matmul(tag + "_out_bw", on, dy, "tn")
    do, dz, dnorm = _rw_bwd(tag + "_post_b", f_gdnpost, [Row(o, splits=[DH] * NH), Row(proj, D, cb=3, splits=[DH] * NH)],
                            [p["norm_g"]], [Row(don, splits=[DH] * NH)], [True] * (2 * NH), [True],
                            [((DH,) * NH, F32), ((DH,) * NH, BF16)], _tm(S, 2 * D))
    dq, dk, dv, dgc, dgr, db = _gdn_bwd(tag + "_core_b", q, k, v, gcol, grow, bcol, ssave, do)
    dgcol = dgc + dgr.transpose(0, 2, 1).reshape(S, NH)
    dgates, da_log, ddt = _rw_bwd(tag + "_gates_b", f_gates, [Row(proj, LANE, cb=GATE_CB)], [p["a_log128"], p["dt_bias128"]],
                                  [Row(_pad_lanes(db, 0)), Row(_pad_lanes(dgcol, NH))], [True], [True, True],
                                  [((LANE,), BF16)], _tm(S, LANE))
    dqc = _rw_bwd(tag + "_pre_b", f_gdnpre, [Row(qc, splits=[DH] * (3 * NH))], [],
                  [Row(dq, splits=[DH] * NH), Row(dk, splits=[DH] * NH), Row(dv, splits=[DH] * NH)],
                  [True] * (3 * NH), [], [((DH,) * (3 * NH), F32)], _tm(S, 3 * D))[0]
    dqkv, dconv = _conv_bwd(tag + "_conv_b", proj, dqc, p["conv_w8"], 3 * D, _tm(S, 3 * D), BF16)
    dproj = jnp.concatenate([dqkv, dz, dgates], axis=1)
    dh = _matmul(tag + "_in_bi", dproj, p["w_in"], "nt")
    dw_in = _matmul(tag + "_in_bw", h, dproj, "tn")
    dx, (dg, dshift, dscale) = _mod_bwd(tag, x, g, shift, scale, dh, dxn)
    return dx, dict(w_in=dw_in, conv_w8=dconv, a_log128=da_log, dt_bias128=ddt, norm_g=dnorm,
                    w_out=dw_out, g=dg, mod=(dshift, dscale, dgate))


def _qk_rows(src, shared_rope, ckv=None):
    if shared_rope:
        return [Row(src, D, cb=0, splits=[DH] * NH), Row(ckv, LANE, cb=2)]
    return [Row(src, splits=[DH] * (2 * NH))]


def _kv_fwd(x, kvmod, p, tabs):
    shift, scale = kvmod
    S = x.shape[0]
    h = _mod_fwd("kv", x, p["kv_norm_g"], shift, scale)
    ckv = _matmul("kv_dkv", h, p["w_dkv"])
    lat = _rw_fwd("kv_lat", f_rms, [Row(ckv, KVL)], [p["kv_lat_g"]], [((KVL,), BF16)], _tm(S, KVL))[0]
    kvf = _matmul("kv_ukv", lat, p["w_ukv"])
    kp = _rw_fwd("kv_k", make_f_qk(True), _qk_rows(kvf, True, ckv) + [Row(tabs[0]), Row(tabs[1])],
                 [p["k_gn"], p["k_gr"], p["pm"]], [((DH,) * (2 * NH), BF16)], _tm(S, 2 * D))[0]
    return kp, kvf, (x, h, ckv, lat)


def _kv_bwd(dkp, dv, dx_direct, res, kvmod, kvf, p, tabs):
    shift, scale = kvmod
    x, h, ckv, lat = res
    S = x.shape[0]
    dkn, dkr, dgn, dgr = _rw_bwd("kv_k_b", make_f_qk(True), _qk_rows(kvf, True, ckv) + [Row(tabs[0]), Row(tabs[1])],
                                 [p["k_gn"], p["k_gr"], p["pm"]], [Row(dkp, splits=[DH] * (2 * NH))],
                                 [True] * (NH + 1) + [False, False], [True, True, False],
                                 [((DH,) * NH, BF16), ((LANE,), BF16)], _tm(S, 2 * D))
    dkvf = jnp.concatenate([dkn, dv.astype(BF16)], axis=1)
    dlat = _matmul("kv_ukv_bi", dkvf, p["w_ukv"], "nt")
    dw_ukv = _matmul("kv_ukv_bw", lat, dkvf, "tn")
    dcl, dlg = _rw_bwd("kv_lat_b", f_rms, [Row(ckv, KVL)], [p["kv_lat_g"]], [Row(dlat)], [True], [True],
                       [((KVL,), BF16)], _tm(S, KVL))
    dckv = jnp.concatenate([dcl, dkr], axis=1)
    dh = _matmul("kv_dkv_bi", dckv, p["w_dkv"], "nt")
    dw_dkv = _matmul("kv_dkv_bw", h, dckv, "tn")
    dx, (dg, dshift, dscale) = _mod_bwd("kv", x, p["kv_norm_g"], shift, scale, dh, dx_direct)
    return dx, dict(w_dkv=dw_dkv, w_ukv=dw_ukv, kv_lat_g=dlg, k_gn=dgn, k_gr=dgr, kv_norm_g=dg, mod=(dshift, dscale))


def _mla_layer_fwd(tag, x, mod3, g, p, kp, kvf, tabs):
    shift, scale, gate = mod3
    S = x.shape[0]
    h = _mod_fwd(tag, x, g, shift, scale)
    ql = _matmul(tag + "_dq", h, p["w_dq"])
    qln = _rw_fwd(tag + "_qln", f_rms, [Row(ql)], [p["ql_g"]], [((QL,), BF16)], _tm(S, QL))[0]
    qu = _matmul(tag + "_uq", qln, p["w_uq"])
    qp = _rw_fwd(tag + "_q", make_f_qk(False), _qk_rows(qu, False) + [Row(tabs[0]), Row(tabs[1])],
                 [p["q_gn"], p["q_gr"], p["pm"]], [((DH,) * (2 * NH), BF16)], _tm(S, 2 * D))[0]
    o, lse = _flash_fwd(tag + "_att", qp, kp, kvf)
    y = _matmul(tag + "_out", o, p["w_out"])
    xn = _res_fwd(tag, x, y, gate, 1.0)
    return xn, (x, h, ql, qln, qu, qp, o, lse, y)


def _mla_layer_bwd(tag, dxn, res, mod3, g, p, kp, kvf, tabs):
    shift, scale, gate = mod3
    x, h, ql, qln, qu, qp, o, lse, y = res
    S = x.shape[0]
    dy, dgate = _res_bwd(tag, y, gate, dxn, 1.0)
    do = _matmul(tag + "_out_bi", dy, p["w_out"], "nt")
    dw_out = _matmul(tag + "_out_bw", o, dy, "tn")
    dqp, delta = _flash_bwd_dq(tag + "_att_bq", qp, kp, kvf, o, do, lse)
    dkp, dv = _flash_bwd_dkv(tag + "_att_bkv", qp, kp, kvf, do, lse.reshape(NH, 1, S), delta.reshape(NH, 1, S))
    dqu, dgn, dgr = _rw_bwd(tag + "_q_b", make_f_qk(False), _qk_rows(qu, False) + [Row(tabs[0]), Row(tabs[1])],
                            [p["q_gn"], p["q_gr"], p["pm"]], [Row(dqp, splits=[DH] * (2 * NH))],
                            [True] * (2 * NH) + [False, False], [True, True, False],
                            [((DH,) * (2 * NH), BF16)], _tm(S, 2 * D))
    dqln = _matmul(tag + "_uq_bi", dqu, p["w_uq"], "nt")
    dw_uq = _matmul(tag + "_uq_bw", qln, dqu, "tn")
    dql, dqlg = _rw_bwd(tag + "_qln_b", f_rms, [Row(ql)], [p["ql_g"]], [Row(dqln)], [True], [True], [((QL,), BF16)],
                        _tm(S, QL))
    dh = _matmul(tag + "_dq_bi", dql, p["w_dq"], "nt")
    dw_dq = _matmul(tag + "_dq_bw", h, dql, "tn")
    dx, (dg, dshift, dscale) = _mod_bwd(tag, x, g, shift, scale, dh, dxn)
    return dx, dkp, dv, dict(w_dq=dw_dq, w_uq=dw_uq, w_out=dw_out, ql_g=dqlg, q_gn=dgn, q_gr=dgr, g=dg,
                             mod=(dshift, dscale, dgate))


def _loss_head(y, tgt):
    S = y.shape[0]

    def fn(pieces, bvals):
        e = pieces[0] - pieces[1]
        part = jnp.sum(e * e) * (0.5 / D)
        return [e * (1.0 / D)], [jnp.full((1, LANE), part, F32)]
    dy, part = _rowwise("loss", fn, [Row(y), Row(tgt)], [], [((D,), F32)], [(1, LANE)], _tm(S, D))
    return part[0, 0], dy


def _rope_tables(positions):
    S = positions.shape[0]
    half = ROPE // 2
    lane = lax.broadcasted_iota(jnp.int32, (1, LANE), 1)
    inv_freq = ROPE_BASE ** (-(lane % half).astype(F32) / half)
    live = (lane < ROPE).astype(F32)
    sign = jnp.where(lane < half, -1.0, 1.0) * live

    def fn(pieces, bvals):
        ang = pieces[0] * bvals[0]
        return [jnp.cos(ang) * bvals[1], jnp.sin(ang) * bvals[2]], []
    pos = jnp.broadcast_to(positions.astype(F32)[:, None], (S, LANE))
    cosp, sins = _rowwise("rope_tab", fn, [Row(pos)], [inv_freq, live, sign], [((LANE,), F32)] * 2, [], _tm(S, LANE))
    r = lax.broadcasted_iota(jnp.int32, (LANE, LANE), 0)
    c = lax.broadcasted_iota(jnp.int32, (LANE, LANE), 1)
    pm = (((c < half) & (r == c + half)) | ((c >= half) & (c < ROPE) & (r == c - half))).astype(F32)
    return (cosp, sins), pm


def _adamw(name, w, g, m, v):
    shape = w.shape
    C = shape[-1]
    R = w.size // C
    tr = R
    for t in (1024, 512, 256, 128, 64, 32, 16, 8):
        if R % t == 0 and t * C * 4 <= (1 << 21):
            tr = t
            break
    c1 = 1.0 - ADAM_B1 ** ADAM_STEP
    c2 = 1.0 - ADAM_B2 ** ADAM_STEP

    def body(w_ref, g_ref, m_ref, v_ref, d_ref, mo_ref, vo_ref):
        gg = g_ref[...]
        mn = ADAM_B1 * m_ref[...] + (1.0 - ADAM_B1) * gg
        vn = ADAM_B2 * v_ref[...] + (1.0 - ADAM_B2) * (gg * gg)
        d_ref[...] = -ADAM_LR * ((mn / c1) / (jnp.sqrt(vn / c2) + ADAM_EPS) + ADAM_WD * w_ref[...])
        mo_ref[...] = mn
        vo_ref[...] = vn

    spec = pl.BlockSpec((tr, C), lambda i: (i, 0))
    outs = pl.pallas_call(body, name=name, grid=(R // tr,), in_specs=[spec] * 4, out_specs=[spec] * 3,
                          out_shape=[jax.ShapeDtypeStruct((R, C), F32)] * 3,
                          compiler_params=_cparams(("parallel",)))(*[t.reshape(R, C) for t in (w, g, m, v)])
    return [o.reshape(shape) for o in outs]


HBM_SPEC = pl.BlockSpec(memory_space=pltpu.HBM)
OTHER_CHIPS = (4, 2, 6)
SIBLING = 1


def _me():
    return lax.axis_index("x"), lax.axis_index("y"), lax.axis_index("c")


def _peer(me, k):
    mx, my, mc = me
    return ((1 - mx) if k & 4 else mx, (1 - my) if k & 2 else my, (1 - mc) if k & 1 else mc)


def _rcopy(src, dst, ssem, rsem, to):
    return pltpu.make_async_remote_copy(src_ref=src, dst_ref=dst, send_sem=ssem, recv_sem=rsem, device_id=to,
                                        device_id_type=MESH)


def _all_gather8(name, x):
    def body(x_ref, o_ref, ssem, rsem, lsem):
        me = _me()
        mine = 4 * me[0] + 2 * me[1] + me[2]
        loc = pltpu.make_async_copy(x_ref, o_ref.at[mine], lsem)
        loc.start()
        sends = []
        for k in range(1, 8):
            cp = _rcopy(x_ref, o_ref.at[mine], ssem.at[k - 1], rsem.at[k - 1], _peer(me, k))
            cp.start()
            sends.append(cp)
        for k in range(1, 8):
            px, py, pc = _peer(me, k)
            _rcopy(x_ref, o_ref.at[4 * px + 2 * py + pc], ssem.at[k - 1], rsem.at[k - 1], (px, py, pc)).wait_recv()
        for cp in sends:
            cp.wait_send()
        loc.wait()

    return pl.pallas_call(body, name=name, out_shape=jax.ShapeDtypeStruct((8,) + x.shape, x.dtype),
                          in_specs=[HBM_SPEC], out_specs=HBM_SPEC,
                          scratch_shapes=[pltpu.SemaphoreType.DMA((7,)), pltpu.SemaphoreType.DMA((7,)),
                                          pltpu.SemaphoreType.DMA(())])(x)


PACK_L = 1024
PACK_RT = 256


def _place_shard(name, wp, chip):
    rh = wp.shape[1]

    def body(s_ref, w_ref, o_ref):
        o_ref[...] = w_ref[...]

    gs = pltpu.PrefetchScalarGridSpec(
        num_scalar_prefetch=1, grid=(2, rh // PACK_RT),
        in_specs=[pl.BlockSpec((None, PACK_RT, PACK_L), lambda h, i, s_ref: (h, i, 0))],
        out_specs=pl.BlockSpec((None, None, PACK_RT, PACK_L), lambda h, i, s_ref: (s_ref[0], h, i, 0)))
    return pl.pallas_call(body, name=name, grid_spec=gs, out_shape=jax.ShapeDtypeStruct((4,) + wp.shape, wp.dtype),
                          compiler_params=_cparams(("parallel", "parallel")))(chip.reshape(1).astype(jnp.int32), wp)


def _gather_weights(name, w4):
    def body(w_ref, o_ref, ssem, rsem):
        me = _me()
        mc = me[2]
        mine = o_ref.at[2 * me[0] + me[1], mc]
        first = []
        for j, k in enumerate(OTHER_CHIPS):
            cp = _rcopy(mine, mine, ssem.at[j], rsem.at[j], _peer(me, k))
            cp.start()
            first.append(cp)
        passed = []
        for j, k in enumerate(OTHER_CHIPS):
            px, py, _ = _peer(me, k)
            land = o_ref.at[2 * px + py, mc]
            _rcopy(land, land, ssem.at[j], rsem.at[j], _peer(me, k)).wait_recv()
            fw = _rcopy(land, land, ssem.at[3 + j], rsem.at[3 + j], _peer(me, SIBLING))
            fw.start()
            passed.append(fw)
        for j, k in enumerate(OTHER_CHIPS):
            px, py, _ = _peer(me, k)
            land = o_ref.at[2 * px + py, 1 - mc]
            _rcopy(land, land, ssem.at[3 + j], rsem.at[3 + j], _peer(me, SIBLING)).wait_recv()
        for cp in first + passed:
            cp.wait_send()

    return pl.pallas_call(body, name=name, out_shape=jax.ShapeDtypeStruct(w4.shape, w4.dtype),
                          in_specs=[HBM_SPEC], out_specs=HBM_SPEC, input_output_aliases={0: 0},
                          scratch_shapes=[pltpu.SemaphoreType.DMA((6,)), pltpu.SemaphoreType.DMA((6,))])(w4)


def _exchange_half(name, g):
    def body(g_ref, p_ref, ssem, rsem):
        me = _me()
        cps = []
        for s in range(4):
            cp = _rcopy(g_ref.at[s, 1 - me[2]], p_ref.at[s], ssem.at[s], rsem.at[s], _peer(me, SIBLING))
            cp.start()
            cps.append(cp)
        for cp in cps:
            cp.wait()

    return pl.pallas_call(body, name=name, out_shape=jax.ShapeDtypeStruct((4,) + g.shape[2:], g.dtype),
                          in_specs=[HBM_SPEC], out_specs=HBM_SPEC,
                          scratch_shapes=[pltpu.SemaphoreType.DMA((4,)), pltpu.SemaphoreType.DMA((4,))])(g)


def _scatter_chips(name, q):
    def body(q_ref, t_ref, ssem, rsem):
        me = _me()
        cps = []
        for j, k in enumerate(OTHER_CHIPS):
            px, py, _ = _peer(me, k)
            cp = _rcopy(q_ref.at[2 * px + py], t_ref.at[j], ssem.at[j], rsem.at[j], _peer(me, k))
            cp.start()
            cps.append(cp)
        for cp in cps:
            cp.wait()

    return pl.pallas_call(body, name=name, out_shape=jax.ShapeDtypeStruct((3,) + q.shape[1:], q.dtype),
                          in_specs=[HBM_SPEC], out_specs=HBM_SPEC,
                          scratch_shapes=[pltpu.SemaphoreType.DMA((3,)), pltpu.SemaphoreType.DMA((3,))])(q)


def _exchange_full(name, r2):
    def body(r_ref, o_ref, ssem, rsem):
        me = _me()
        mc = me[2]
        cp = _rcopy(o_ref.at[mc], o_ref.at[mc], ssem, rsem, _peer(me, SIBLING))
        cp.start()
        _rcopy(o_ref.at[1 - mc], o_ref.at[1 - mc], ssem, rsem, _peer(me, SIBLING)).wait_recv()
        cp.wait_send()

    return pl.pallas_call(body, name=name, out_shape=jax.ShapeDtypeStruct(r2.shape, r2.dtype),
                          in_specs=[HBM_SPEC], out_specs=HBM_SPEC, input_output_aliases={0: 0},
                          scratch_shapes=[pltpu.SemaphoreType.DMA(()), pltpu.SemaphoreType.DMA(())])(r2)


def _add_half(name, g, p, c):
    rh = g.shape[2]

    def body(c_ref, g_ref, p_ref, o_ref):
        o_ref[0] = (g_ref[0, 0] + p_ref[0]).astype(o_ref.dtype)

    gs = pltpu.PrefetchScalarGridSpec(
        num_scalar_prefetch=1, grid=(4, rh // PACK_RT),
        in_specs=[pl.BlockSpec((1, 1, PACK_RT, PACK_L), lambda s, i, c_ref: (s, c_ref[0], i, 0)),
                  pl.BlockSpec((1, PACK_RT, PACK_L), lambda s, i, c_ref: (s, i, 0))],
        out_specs=pl.BlockSpec((1, PACK_RT, PACK_L), lambda s, i, c_ref: (s, i, 0)))
    return pl.pallas_call(body, name=name, grid_spec=gs, out_shape=jax.ShapeDtypeStruct((4, rh, PACK_L), BF16),
                          compiler_params=_cparams(("parallel", "parallel")))(c.reshape(1).astype(jnp.int32), g, p)


def _add_chips(name, q, t, chip, c):
    rh = q.shape[1]

    def body(s_ref, c_ref, q_ref, t_ref, o_ref):
        o_ref[...] = ((q_ref[0].astype(F32) + t_ref[0].astype(F32)) + t_ref[1].astype(F32)) + t_ref[2].astype(F32)

    gs = pltpu.PrefetchScalarGridSpec(
        num_scalar_prefetch=2, grid=(rh // PACK_RT,),
        in_specs=[pl.BlockSpec((1, PACK_RT, PACK_L), lambda i, s_ref, c_ref: (s_ref[0], i, 0)),
                  pl.BlockSpec((3, PACK_RT, PACK_L), lambda i, s_ref, c_ref: (0, i, 0))],
        out_specs=pl.BlockSpec((None, PACK_RT, PACK_L), lambda i, s_ref, c_ref: (c_ref[0], i, 0)))
    return pl.pallas_call(body, name=name, grid_spec=gs, out_shape=jax.ShapeDtypeStruct((2, rh, PACK_L), F32),
                          compiler_params=_cparams(("parallel",)))(chip.reshape(1).astype(jnp.int32),
                                                                    c.reshape(1).astype(jnp.int32), q, t)


def _sum8(name, a):
    def body(a_ref, o_ref):
        acc = a_ref[0]
        for d in range(1, 8):
            acc = acc + a_ref[d]
        o_ref[...] = acc
    return pl.pallas_call(body, name=name, out_shape=jax.ShapeDtypeStruct(a.shape[1:], F32))(a)


def _silu_rows(name, a):
    def body(a_ref, o_ref):
        o_ref[...] = _silu(a_ref[...])
    return pl.pallas_call(body, name=name, out_shape=jax.ShapeDtypeStruct(a.shape, F32))(a)


REST = (("gdn_w_in", 2), ("gdn_w_out", 1), ("mla_w_dkv", 0), ("mla_w_ukv", 1), ("mla_w_dq", 1), ("mla_w_uq", 2),
        ("mla_w_out", 1))


def _packed_rows(n):
    per_half = -(-n // (2 * PACK_L))
    return -(-per_half // PACK_RT) * PACK_RT


def _pack_flat(flat):
    n = flat.shape[-1]
    rh = _packed_rows(n)
    pad = [(0, 0)] * (flat.ndim - 1) + [(0, 2 * rh * PACK_L - n)]
    return jnp.pad(flat, pad).reshape(flat.shape[:-1] + (2, rh, PACK_L))


def _shards_first(full, axis):
    sh = full.shape
    t = full.reshape(sh[:axis] + (4, sh[axis] // 4) + sh[axis + 1:])
    return jnp.moveaxis(t, axis, 0)


def _shards_merge(stacked, axis):
    t = jnp.moveaxis(stacked, 0, axis)
    sh = t.shape
    return t.reshape(sh[:axis] + (4 * sh[axis + 1],) + sh[axis + 2:])


def _pack_small(parts):
    flat = jnp.concatenate([p.reshape(-1).astype(F32) for p in parts])
    n = flat.shape[0]
    rows = -(-n // (SUB * LANE)) * SUB
    return jnp.pad(flat, (0, rows * LANE - n)).reshape(rows, LANE)


def _unpack_small(buf, shapes):
    lead = buf.shape[:-2]
    flat = buf.reshape(lead + (-1,))
    out, off = [], 0
    for sh in shapes:
        n = 1
        for d in sh:
            n *= d
        out.append(flat[..., off:off + n].reshape(lead + tuple(sh)))
        off += n
    return out


WEIGHTS = ('ada_w', 'ada_b', 'norm_g', 'ffn_w_in', 'ffn_w_out', 'gdn_w_in', 'gdn_conv_w', 'gdn_a_log', 'gdn_dt_bias',
           'gdn_norm_g', 'gdn_w_out', 'kv_ada_w', 'kv_ada_b', 'kv_norm_g', 'mla_w_dkv', 'mla_kv_norm_g', 'mla_w_ukv',
           'mla_k_norm_g', 'mla_w_dq', 'mla_q_lora_norm_g', 'mla_w_uq', 'mla_q_norm_g', 'mla_w_out')
ARGS = ('x', 'c', 'positions') + WEIGHTS + ('loss_target',) + tuple('m_' + n for n in WEIGHTS) + tuple('v_' + n for n in WEIGHTS)


def _split_norm(v):
    return v[None, :DH], _pad_lanes(v[None, DH:], 0)


def _join_norm(gn, gr):
    return jnp.concatenate([gn[0], gr[0, :ROPE]])


def _step(x, tgt, pos, mods, kvmod, W, P):
    tabs, pm = _rope_tables(pos)
    m3 = lambda l, i: tuple(mods[l][3 * i + j][None] for j in range(3))
    ng = lambda l, i: P["norm_g"][l, i][None]
    gdn_p, mla_p = [], []
    for l in range(2):
        gdn_p.append(dict(w_in=jnp.pad(W["gdn_w_in"][l], ((0, 0), (0, GDN_IN - W["gdn_w_in"].shape[2]))),
                          conv_w8=jnp.pad(P["gdn_conv_w"][l], ((0, 4), (0, 0))),
                          a_log128=_pad_lanes(P["gdn_a_log"][l][None], NH), dt_bias128=_pad_lanes(P["gdn_dt_bias"][l][None], NH),
                          norm_g=P["gdn_norm_g"][l][None], w_out=W["gdn_w_out"][l]))
        q_gn, q_gr = _split_norm(P["mla_q_norm_g"][l])
        mla_p.append(dict(w_dq=W["mla_w_dq"][l], ql_g=P["mla_q_lora_norm_g"][l][None],
                          w_uq=jnp.pad(W["mla_w_uq"][l].reshape(QL, NH, QKH), ((0, 0), (0, 0), (0, HP - QKH))).reshape(QL, NH * HP),
                          q_gn=q_gn, q_gr=q_gr, pm=pm, w_out=W["mla_w_out"][l]))
    k_gn, k_gr = _split_norm(P["mla_k_norm_g"])
    kv_p = dict(kv_norm_g=P["kv_norm_g"][None], w_dkv=jnp.pad(W["mla_w_dkv"], ((0, 0), (0, QL - KVL - ROPE))),
                kv_lat_g=P["mla_kv_norm_g"][None],
                w_ukv=W["mla_w_ukv"].reshape(KVL, NH, 2, DH).transpose(0, 2, 1, 3).reshape(KVL, 2 * NH * DH),
                k_gn=k_gn, k_gr=k_gr, pm=pm)
    kvm = (kvmod[0][None], kvmod[1][None])

    res = {}
    for l in range(4):
        x, res[l, 0] = _ffn_fwd(f"l{l}a", x, m3(l, 0), ng(l, 0), W["ffn_w_in"], W["ffn_w_out"], 2 * l)
        if l < 2:
            x, res[l, 1] = _gdn_layer_fwd(f"l{l}g", x, m3(l, 1), ng(l, 1), gdn_p[l])
        else:
            x, res[l, 1] = _mla_layer_fwd(f"l{l}m", x, m3(l, 1), ng(l, 1), mla_p[l - 2], kp, kvf, tabs)
        x, res[l, 2] = _ffn_fwd(f"l{l}b", x, m3(l, 2), ng(l, 2), W["ffn_w_in"], W["ffn_w_out"], 2 * l + 1)
        if l == 1:
            kp, kvf, kres = _kv_fwd(x, kvm, kv_p, tabs)
    loss, dx = _loss_head(x, tgt)

    gw = {n: [None] * W[n].shape[0] for n in ("gdn_w_in", "gdn_w_out", "mla_w_dq", "mla_w_uq", "mla_w_out")}
    g_in4 = g_out4 = None
    gp = {n: [None] * 2 for n in ("gdn_conv_w", "gdn_a_log", "gdn_dt_bias", "gdn_norm_g", "mla_q_lora_norm_g", "mla_q_norm_g")}
    gnorm = [[None] * 3 for _ in range(4)]
    dmod = [[None] * NMOD for _ in range(4)]
    dkp = dv = None
    for l in (3, 2, 1, 0):
        if l == 1:
            dx, gk = _kv_bwd(dkp, dv, dx, kres, kvm, kvf, kv_p, tabs)
        for i in (2, 1, 0):
            if i != 1:
                dx, g_in4, g_out4, gd = _ffn_bwd(f"l{l}{'ab'[i // 2]}", dx, res[l, i], m3(l, i), ng(l, i), W["ffn_w_in"],
                                                 W["ffn_w_out"], 2 * l + i // 2, g_in4, g_out4)
            elif l < 2:
                dx, gd = _gdn_layer_bwd(f"l{l}g", dx, res[l, 1], m3(l, 1), ng(l, 1), gdn_p[l])
                gw["gdn_w_in"][l] = gd["w_in"][:, :W["gdn_w_in"].shape[2]]
                gw["gdn_w_out"][l] = gd["w_out"]
                gp["gdn_conv_w"][l] = gd["conv_w8"][:4]
                gp["gdn_a_log"][l] = gd["a_log128"][0, NH:2 * NH]
                gp["gdn_dt_bias"][l] = gd["dt_bias128"][0, NH:2 * NH]
                gp["gdn_norm_g"][l] = gd["norm_g"][0]
            else:
                dx, dkp_l, dv_l, gd = _mla_layer_bwd(f"l{l}m", dx, res[l, 1], m3(l, 1), ng(l, 1), mla_p[l - 2], kp, kvf, tabs)
                dkp = dkp_l if dkp is None else dkp + dkp_l
                dv = dv_l if dv is None else dv + dv_l
                gw["mla_w_dq"][l - 2], gw["mla_w_out"][l - 2] = gd["w_dq"], gd["w_out"]
                gw["mla_w_uq"][l - 2] = gd["w_uq"].reshape(QL, NH, HP)[:, :, :QKH].reshape(QL, NH * QKH)
                gp["mla_q_lora_norm_g"][l - 2] = gd["ql_g"][0]
                gp["mla_q_norm_g"][l - 2] = _join_norm(gd["q_gn"], gd["q_gr"])
            gnorm[l][i] = gd["g"][0]
            for j in range(3):
                dmod[l][3 * i + j] = gd["mod"][j][0]
    gwf = {n: jnp.stack(v) for n, v in gw.items()}
    gwf["ffn_w_in"], gwf["ffn_w_out"] = g_in4, g_out4
    gwf["mla_w_dkv"] = gk["w_dkv"][:, :KVL + ROPE]
    gwf["mla_w_ukv"] = gk["w_ukv"].reshape(KVL, 2, NH, DH).transpose(0, 2, 1, 3).reshape(KVL, 2 * NH * DH)
    gpf = {n: jnp.stack(v) for n, v in gp.items()}
    gpf["norm_g"] = jnp.stack([jnp.stack(r) for r in gnorm])
    gpf["kv_norm_g"] = gk["kv_norm_g"][0]
    gpf["mla_kv_norm_g"] = gk["kv_lat_g"][0]
    gpf["mla_k_norm_g"] = _join_norm(gk["k_gn"], gk["k_gr"])
    dmods = jnp.stack([jnp.stack(r) for r in dmod])
    dkvmod = jnp.stack([gk["mod"][0][0], gk["mod"][1][0]])
    return loss, dx, gwf, gpf, dmods, dkvmod


SMALL = ("norm_g", "gdn_conv_w", "gdn_a_log", "gdn_dt_bias", "gdn_norm_g", "kv_norm_g", "mla_kv_norm_g", "mla_k_norm_g",
         "mla_q_lora_norm_g", "mla_q_norm_g")


def kernel(x, c, positions, ada_w, ada_b, norm_g, ffn_w_in, ffn_w_out, gdn_w_in, gdn_conv_w, gdn_a_log, gdn_dt_bias,
           gdn_norm_g, gdn_w_out, kv_ada_w, kv_ada_b, kv_norm_g, mla_w_dkv, mla_kv_norm_g, mla_w_ukv, mla_k_norm_g,
           mla_w_dq, mla_q_lora_norm_g, mla_w_uq, mla_q_norm_g, mla_w_out, loss_target, m_ada_w, m_ada_b, m_norm_g,
           m_ffn_w_in, m_ffn_w_out, m_gdn_w_in, m_gdn_conv_w, m_gdn_a_log, m_gdn_dt_bias, m_gdn_norm_g, m_gdn_w_out,
           m_kv_ada_w, m_kv_ada_b, m_kv_norm_g, m_mla_w_dkv, m_mla_kv_norm_g, m_mla_w_ukv, m_mla_k_norm_g, m_mla_w_dq,
           m_mla_q_lora_norm_g, m_mla_w_uq, m_mla_q_norm_g, m_mla_w_out, v_ada_w, v_ada_b, v_norm_g, v_ffn_w_in,
           v_ffn_w_out, v_gdn_w_in, v_gdn_conv_w, v_gdn_a_log, v_gdn_dt_bias, v_gdn_norm_g, v_gdn_w_out, v_kv_ada_w,
           v_kv_ada_b, v_kv_norm_g, v_mla_w_dkv, v_mla_kv_norm_g, v_mla_w_ukv, v_mla_k_norm_g, v_mla_w_dq,
           v_mla_q_lora_norm_g, v_mla_w_uq, v_mla_q_norm_g, v_mla_w_out):
    a = dict(locals())
    mx, my, mc = _me()
    dev = 4 * mx + 2 * my + mc
    chip = 2 * mx + my
    x, tgt, pos = a["x"][0], a["loss_target"][0], a["positions"][0]
    take = lambda arr, i, axis=0: lax.dynamic_index_in_dim(arr, i, axis, keepdims=False)

    pre = _all_gather8("ag_pre", _pack_small([a["c"], a["gdn_conv_w"], a["norm_g"]]))
    c_all, conv_sh, norm_sh = _unpack_small(pre, [(D,), a["gdn_conv_w"].shape, a["norm_g"].shape])
    P = {n: a[n] for n in SMALL}
    P["gdn_conv_w"] = jnp.concatenate([conv_sh[2 * s] for s in range(4)], axis=2)
    P["norm_g"] = jnp.concatenate([norm_sh[2 * s] for s in range(4)], axis=2)
    c_act = _silu_rows("c_act", c_all)
    nada = a["ada_w"].shape[2]
    nkv = a["kv_ada_w"].shape[1]
    modp = [_matmul(f"mod{l}", c_act, a["ada_w"][l], precise=True) for l in range(4)]
    kvp = _matmul("modkv", c_act, a["kv_ada_w"], precise=True)
    mp = _all_gather8("ag_mod", _pack_small(modp + [kvp]))
    modp_all, kvp_all = _unpack_small(mp, [(4, 8, nada), (8, nkv)])
    mods = jnp.concatenate([take(modp_all[2 * s], dev, 1) for s in range(4)], axis=1) + a["ada_b"]
    mods = mods.reshape(4, NMOD, D)
    kvmod = (jnp.concatenate([take(kvp_all[2 * s], dev, 0) for s in range(4)]) + a["kv_ada_b"]).reshape(2, D)

    def gather(tag, flat):
        n = flat.shape[0]
        w4 = _gather_weights("ag_" + tag, _place_shard("own_" + tag, _pack_flat(flat), chip)).reshape(4, -1)
        return w4 if w4.shape[1] == n else w4[:, :n]

    def reduce(tag, g4):
        g4 = g4.reshape(4, 2, -1, PACK_L)
        q = _add_half("rsp_" + tag, g4, _exchange_half("rs1_" + tag, g4), mc)
        r2 = _add_chips("rsc_" + tag, q, _scatter_chips("rs2_" + tag, q), chip, mc)
        return _exchange_full("rs3_" + tag, r2).reshape(-1)

    W = {n: gather(t, a[n].reshape(-1).astype(BF16)).reshape((4, 8) + a[n].shape[2:])
         for n, t in (("ffn_w_in", "wi"), ("ffn_w_out", "wo"))}
    wall = gather("wr", jnp.concatenate([a[n].reshape(-1).astype(BF16) for n, _ in REST]))
    off = 0
    for n, ax in REST:
        sz = a[n].size
        W[n] = _shards_merge(wall[:, off:off + sz].reshape((4,) + a[n].shape), ax)
        off += sz

    loss, dx, gw, gp, dmods, dkvmod = _step(x, tgt, pos, mods, kvmod, W, P)
    loss = lax.psum(loss, ("x", "y", "c"))

    grads = {n: reduce(t, gw[n]).reshape(a[n].shape) for n, t in (("ffn_w_in", "wi"), ("ffn_w_out", "wo"))}
    gsh = reduce("wr", _pack_flat(jnp.concatenate([_shards_first(gw[n], ax).reshape(4, -1) for n, ax in REST], axis=1)))
    off = 0
    for n, _ in REST:
        grads[n] = gsh[off:off + a[n].size].reshape(a[n].shape)
        off += a[n].size

    small = _all_gather8("ag_small", _pack_small([dmods, dkvmod] + [gp[n] for n in SMALL]))
    shapes = [(4, NMOD * D), (2 * D,)] + [gp[n].shape for n in SMALL]
    dmod_all, dkv_all = _unpack_small(small, shapes)[:2]
    tot = _unpack_small(_sum8("sum_small", small), shapes)
    grads["ada_b"], grads["kv_ada_b"] = tot[0], tot[1]
    for n, t in zip(SMALL, tot[2:]):
        grads[n] = t
    grads["norm_g"] = lax.dynamic_slice_in_dim(grads["norm_g"], chip * a["norm_g"].shape[2], a["norm_g"].shape[2], 2)
    grads["gdn_conv_w"] = lax.dynamic_slice_in_dim(grads["gdn_conv_w"], chip * a["gdn_conv_w"].shape[2],
                                                   a["gdn_conv_w"].shape[2], 2)
    ca = jnp.pad(c_act, ((0, LANE - 8), (0, 0)))
    dm = jnp.pad(lax.dynamic_slice_in_dim(dmod_all.reshape(8, 4, NMOD * D), chip * nada, nada, 2), ((0, LANE - 8), (0, 0), (0, 0)))
    grads["ada_w"] = jnp.stack([_matmul(f"gada{l}", ca, dm[:, l], "tn", precise=True) for l in range(4)])
    dk = jnp.pad(lax.dynamic_slice_in_dim(dkv_all, chip * nkv, nkv, 1), ((0, LANE - 8), (0, 0)))
    grads["kv_ada_w"] = _matmul("gadakv", ca, dk, "tn", precise=True)

    upd = [_adamw("adamw_" + n, a[n], grads[n], a["m_" + n], a["v_" + n]) for n in WEIGHTS]
    return (loss, dx[None], *[grads[n] for n in WEIGHTS], *[u[0] for u in upd], *[u[1] for u in upd], *[u[2] for u in upd])
```

```python
import functools

import jax
import jax.numpy as jnp
from jax import lax
from jax.experimental import pallas as pl
from jax.experimental.pallas import tpu as pltpu

F32 = jnp.float32
BF16 = jnp.bfloat16
HI = lax.Precision.HIGHEST
MESH = pl.DeviceIdType.MESH

D = 1024
NH = 8
DH = 128
FF = 2816
NMOD = 9
CHUNK = 64
ROPE = 64
QKH = 192
HP = 256
KVL = 256
QL = 384
GDN_IN = 4224
GATE_CB = 32
EPS = 1e-6
ROPE_BASE = 10000.0
LANE = 128
SUB = 8
VMEM_LIMIT = 56 * 1024 * 1024

ADAM_LR, ADAM_B1, ADAM_B2, ADAM_EPS, ADAM_WD, ADAM_STEP = 0.001, 0.9, 0.999, 1e-08, 0.01, 10


def _tile(n, prefs=(512, 384, 256, 128)):
    for p in prefs:
        if n % p == 0:
            return p
    return n


def _cparams(sem):
    return pltpu.CompilerParams(dimension_semantics=sem, vmem_limit_bytes=VMEM_LIMIT)


class Row:
    def __init__(self, arr, width=None, cb=0, splits=None, halo=None):
        self.arr = arr
        self.width = arr.shape[1] if width is None else width
        self.cb = cb
        self.splits = splits
        self.halo = halo


def _rowwise(name, fn, rows, bcs, outs, accs, tm):
    S = rows[0].arr.shape[0]
    n = S // tm
    nr, nb, no, na = len(rows), len(bcs), len(outs), len(accs)

    def body(*refs):
        rrefs, brefs = refs[:nr], refs[nr:nr + nb]
        orefs, arefs = refs[nr + nb:nr + nb + no], refs[nr + nb + no:]
        pieces = []
        for r, ref in zip(rows, rrefs):
            if r.splits is None:
                pieces.append(ref[...])
            else:
                off = 0
                for w in r.splits:
                    pieces.append(ref[:, off:off + w])
                    off += w
        out_pieces, acc_vals = fn(pieces, [b[...] for b in brefs])
        k = 0
        for (widths, dt), oref in zip(outs, orefs):
            off = 0
            for w in widths:
                oref[:, off:off + w] = out_pieces[k].astype(dt)
                k += 1
                off += w
        if na:
            @pl.when(pl.program_id(0) == 0)
            def _():
                for a in arefs:
                    a[...] = jnp.zeros(a.shape, F32)
            for a, v in zip(arefs, acc_vals):
                a[...] += v

    in_specs = []
    for r in rows:
        if r.halo is None:
            in_specs.append(pl.BlockSpec((tm, r.width), lambda i, cb=r.cb: (i, cb)))
        elif r.halo == "prev":
            in_specs.append(pl.BlockSpec((SUB, r.width), lambda i, cb=r.cb: (jnp.maximum(i * (tm // SUB) - 1, 0), cb)))
        else:
            in_specs.append(pl.BlockSpec((SUB, r.width), lambda i, cb=r.cb: (jnp.minimum((i + 1) * (tm // SUB), S // SUB - 1), cb)))
    in_specs += [pl.BlockSpec(b.shape, lambda i, nd=b.ndim: (0,) * nd) for b in bcs]
    out_specs = [pl.BlockSpec((tm, sum(w)), lambda i: (i, 0)) for w, _ in outs]
    out_specs += [pl.BlockSpec(s, lambda i: (0, 0)) for s in accs]
    out_shape = [jax.ShapeDtypeStruct((S, sum(w)), dt) for w, dt in outs]
    out_shape += [jax.ShapeDtypeStruct(s, F32) for s in accs]
    res = pl.pallas_call(body, name=name, grid=(n,), in_specs=in_specs, out_specs=out_specs, out_shape=out_shape,
                         compiler_params=_cparams(("arbitrary",)))(*[r.arr for r in rows], *bcs)
    return res


def _rw_fwd(name, f, rows, bcs, outs, tm):
    def fn(pieces, bvals):
        return list(f(*[p.astype(F32) for p in pieces], *[b.astype(F32) for b in bvals])), []
    return _rowwise(name, fn, rows, bcs, outs, [], tm)


def _npieces(rows):
    return sum(1 if r.splits is None else len(r.splits) for r in rows)


def _rw_bwd(name, f, rows, bcs, cts, drow, dbc, outs, tm, add=None):
    np_, nct = _npieces(rows), _npieces(cts)

    def fn(pieces, bvals):
        allv = [p.astype(F32) for p in pieces[:np_]] + [b.astype(F32) for b in bvals]
        ct = [p.astype(F32) for p in pieces[np_:np_ + nct]]
        didx = [i for i, m in enumerate(list(drow) + list(dbc)) if m]

        def g(*dv):
            full = list(allv)
            for i, v in zip(didx, dv):
                full[i] = v
            return tuple(f(*full))

        _, vjp = jax.vjp(g, *[allv[i] for i in didx])
        grads = vjp(tuple(ct))
        nrd = sum(bool(m) for m in drow)
        rg, bg = list(grads[:nrd]), list(grads[nrd:])
        if add is not None:
            rg[0] = rg[0] + pieces[np_ + nct].astype(F32)
        return rg, bg

    accs = [b.shape for b, m in zip(bcs, dbc) if m]
    return _rowwise(name, fn, list(rows) + list(cts) + ([add] if add is not None else []), bcs, outs, accs, tm)


def _sigmoid(x):
    return 1.0 / (1.0 + jnp.exp(-x))


def _silu(x):
    return x * _sigmoid(x)


def _softplus(x):
    return jnp.maximum(x, 0.0) + jnp.log(1.0 + jnp.exp(-jnp.abs(x)))


def f_mod(x, g, shift, scale):
    y = x * lax.rsqrt(jnp.mean(x * x, axis=-1, keepdims=True) + EPS)
    return (y * g * (1.0 + scale) + shift,)


def f_rms(x, g):
    return (x * lax.rsqrt(jnp.mean(x * x, axis=-1, keepdims=True) + EPS) * g,)


def f_act(gate, up):
    return (_silu(gate) * up,)


def make_f_res(coef):
    def f_res(y, gate):
        return (coef * gate * y,)
    return f_res


def f_gdnpre(*p):
    out = []
    for i, t in enumerate(p):
        t = _silu(t)
        if i < 2 * NH:
            t = t * lax.rsqrt(jnp.sum(t * t, axis=-1, keepdims=True) + EPS)
        out.append(t)
    return tuple(out)


def f_gates(gates, a_log, dt_bias):
    return _sigmoid(gates), -jnp.exp(a_log) * _softplus(gates + dt_bias)


def f_gdnpost(*a):
    o, z, g = a[:NH], a[NH:2 * NH], a[2 * NH]
    out = []
    for oh, zh in zip(o, z):
        y = oh * lax.rsqrt(jnp.mean(oh * oh, axis=-1, keepdims=True) + EPS) * g
        out.append(y * _silu(zh))
    return tuple(out)


def make_f_qk(shared_rope):
    def f(*a):
        if shared_rope:
            ns, rs = a[:NH], [a[NH]] * NH
            cosp, sins, gn, gr, pm = a[NH + 1:NH + 6]
        else:
            ns, rs = a[0:2 * NH:2], a[1:2 * NH:2]
            cosp, sins, gn, gr, pm = a[2 * NH:2 * NH + 5]
        out = []
        for n, r in zip(ns, rs):
            ss = jnp.sum(n * n, axis=-1, keepdims=True) + jnp.sum(r * r, axis=-1, keepdims=True)
            rstd = lax.rsqrt(ss * (1.0 / QKH) + EPS)
            yn = n * rstd * gn
            yr = r * rstd * gr
            sw = jnp.dot(yr, pm, precision=HI, preferred_element_type=F32)
            out += [yn, yr * cosp + sw * sins]
        return tuple(out)
    return f


def _matmul(name, a, b, mode="nn", out_dtype=F32, precise=False, lay=None, li=0, into=None, nmat=1):
    if lay == "b_cols":
        per = b.shape[3]
        rb, cb = b.shape[2], 4 * per
    elif lay == "b_rows":
        per = b.shape[2]
        rb, cb = 4 * per, b.shape[3]
    else:
        rb, cb = b.shape
    if mode == "nn":
        (M, K), N = a.shape, cb
    elif mode == "nt":
        (M, K), N = a.shape, rb
    else:
        (K, M), N = a.shape, cb
    tm = _tile(M, (1024, 512, 256, 128))
    tn = _tile(N, (1024, 512, 384, 256, 128))
    tk = _tile(K, (1408, 1024, 512, 384, 256, 128))
    if lay == "b_cols":
        tn, tk = (per, tk) if mode == "nn" else (tn, per)
    elif lay == "b_rows":
        tm, tn, tk = (tm, 512, K) if mode == "nn" else (min(tm, 512), N, tk)
    elif lay == "o_cols":
        per = N // 4
        tn = per
    elif lay == "o_rows":
        per = M // 4
        tm, tn = M, 512
    nk = K // tk
    dims = {"nn": (((1,), (0,)), ((), ())), "nt": (((1,), (1,)), ((), ())), "tn": (((0,), (0,)), ((), ()))}[mode]

    def body(a_ref, b_ref, *rest):
        o_ref, acc_ref = rest[-2:]
        k = pl.program_id(2)

        @pl.when(k == 0)
        def _():
            acc_ref[...] = jnp.zeros(acc_ref.shape, F32)

        bv = b_ref[...]
        if lay == "b_rows":
            bv = bv.reshape(4 * per, bv.shape[2])
        if precise:
            acc_ref[...] += lax.dot_general(a_ref[...].astype(F32), bv.astype(F32), dims, precision=HI,
                                            preferred_element_type=F32)
        else:
            acc_ref[...] += lax.dot_general(a_ref[...].astype(BF16), bv.astype(BF16), dims, preferred_element_type=F32)

        @pl.when(k == nk - 1)
        def _():
            if lay == "o_rows":
                for s in range(4):
                    o_ref[s] = acc_ref[s * per:(s + 1) * per, :].astype(o_ref.dtype)
            else:
                o_ref[...] = acc_ref[...].astype(o_ref.dtype)

    a_spec = pl.BlockSpec((tk, tm), lambda i, j, k: (k, i)) if mode == "tn" else pl.BlockSpec((tm, tk), lambda i, j, k: (i, k))
    if lay == "b_cols":
        b_spec = (pl.BlockSpec((None, None, tk, per), lambda i, j, k: (j, li, k, 0)) if mode == "nn" else
                  pl.BlockSpec((None, None, tn, per), lambda i, j, k: (k, li, j, 0)))
    elif lay == "b_rows":
        b_spec = (pl.BlockSpec((4, None, per, tn), lambda i, j, k: (0, li, 0, j)) if mode == "nn" else
                  pl.BlockSpec((4, None, per, tk), lambda i, j, k: (0, li, 0, k)))
    elif mode == "nt":
        b_spec = pl.BlockSpec((tn, tk), lambda i, j, k: (j, k))
    else:
        b_spec = pl.BlockSpec((tk, tn), lambda i, j, k: (k, j))
    if lay == "o_cols":
        o_spec = pl.BlockSpec((None, None, tm, per), lambda i, j, k: (j, li, i, 0))
        o_shape = jax.ShapeDtypeStruct((4, nmat, M, per), out_dtype)
    elif lay == "o_rows":
        o_spec = pl.BlockSpec((4, None, per, tn), lambda i, j, k: (0, li, 0, j))
        o_shape = jax.ShapeDtypeStruct((4, nmat, per, N), out_dtype)
    else:
        o_spec = pl.BlockSpec((tm, tn), lambda i, j, k: (i, j))
        o_shape = jax.ShapeDtypeStruct((M, N), out_dtype)
    in_specs, args, alias = [a_spec, b_spec], [a, b], {}
    if into is not None:
        in_specs.append(pl.BlockSpec(memory_space=pl.ANY))
        args.append(into)
        alias = {2: 0}
    return pl.pallas_call(body, name=name, grid=(M // tm, N // tn, nk), in_specs=in_specs, out_specs=o_spec,
                          out_shape=o_shape, scratch_shapes=[pltpu.VMEM((tm, tn), F32)], input_output_aliases=alias,
                          compiler_params=_cparams(("parallel", "parallel", "arbitrary")))(*args)


def _shift_down(t, p, d):
    if d == 0:
        return t
    tr = pltpu.roll(t, d, 0)
    pr = pltpu.roll(p, d, 0)
    r8 = lax.broadcasted_iota(jnp.int32, p.shape, 0)
    first = jnp.where(r8 < d, pr, tr[:SUB])
    return jnp.concatenate([first, tr[SUB:]], axis=0)


def _shift_up(t, nx, d):
    if d == 0:
        return t
    tm = t.shape[0]
    tr = pltpu.roll(t, tm - d, 0)
    nr = pltpu.roll(nx, SUB - d, 0)
    r8 = lax.broadcasted_iota(jnp.int32, nx.shape, 0)
    last = jnp.where(r8 >= SUB - d, nr, tr[tm - SUB:])
    return jnp.concatenate([tr[:tm - SUB], last], axis=0)


def _conv_fwd(name, proj, w8, C, tm):
    def fn(pieces, bvals):
        t, p = pieces[0].astype(F32), pieces[1].astype(F32)
        w = bvals[0]
        p = jnp.where(pl.program_id(0) == 0, 0.0, p)
        out = w[3:4] * t
        for d in (1, 2, 3):
            out = out + w[3 - d:4 - d] * _shift_down(t, p, d)
        return [out], []
    return _rowwise(name, fn, [Row(proj, C), Row(proj, C, halo="prev")], [w8], [((C,), F32)], [], tm)[0]


def _conv_bwd(name, proj, dout, w8, C, tm, out_dtype):
    n = proj.shape[0] // tm

    def fn(pieces, bvals):
        t, p, g, gn = [v.astype(F32) for v in pieces]
        w = bvals[0]
        i = pl.program_id(0)
        p = jnp.where(i == 0, 0.0, p)
        gn = jnp.where(i == n - 1, 0.0, gn)
        dx = w[3:4] * g
        dws = [jnp.sum(g * t, axis=0, keepdims=True)]
        for d in (1, 2, 3):
            dx = dx + w[3 - d:4 - d] * _shift_up(g, gn, d)
            dws.append(jnp.sum(g * _shift_down(t, p, d), axis=0, keepdims=True))
        dw = jnp.concatenate([dws[3], dws[2], dws[1], dws[0], jnp.zeros((4, g.shape[1]), F32)], axis=0)
        return [dx], [dw]
    return _rowwise(name, fn, [Row(proj, C), Row(proj, C, halo="prev"), Row(dout), Row(dout, halo="next")], [w8],
                    [((C,), out_dtype)], [(SUB, C)], tm)


def _bdot(a, b, ca, cb, exact):
    dims = (((ca,), (cb,)), ((0,), (0,)))
    if exact:
        return lax.dot_general(a, b, dims, precision=HI, preferred_element_type=F32)
    return lax.dot_general(a.astype(BF16), b.astype(BF16), dims, preferred_element_type=F32)


def _gdn_chunk(q, k, v, gcol, grow, bcol, S):
    C = CHUNK
    ii = lax.broadcasted_iota(jnp.int32, (1, C, C), 1)
    jj = lax.broadcasted_iota(jnp.int32, (1, C, C), 2)
    incl, strict = ii >= jj, ii > jj
    gc_col = jnp.sum(jnp.where(incl, 1.0, 0.0) * grow, axis=2, keepdims=True)
    gc_row = jnp.sum(jnp.where(jj >= ii, 1.0, 0.0) * gcol, axis=1, keepdims=True)
    decay = jnp.where(incl, jnp.exp(jnp.where(incl, gc_col - gc_row, 0.0)), 0.0)
    qs = q * (DH ** -0.5)
    kb = k * bcol
    nl = -jnp.where(strict, _bdot(kb, k, 2, 2, False) * decay, 0.0)
    T = jnp.where(ii == jj, 1.0, 0.0) + nl
    pw = nl
    for _ in range(5):
        pw = _bdot(pw, pw, 2, 1, True)
        T = T + _bdot(T, pw, 2, 1, True)
    egc = jnp.exp(gc_col)
    u = _bdot(T, v * bcol, 2, 1, True)
    w = _bdot(T, kb * egc, 2, 1, True)
    att = jnp.where(incl, _bdot(qs, k, 2, 2, False) * decay, 0.0)
    v_new = u - _bdot(w, S, 2, 1, False)
    o = _bdot(qs * egc, S, 2, 1, False) + _bdot(att, v_new, 2, 1, False)
    g_last = jnp.sum(grow, axis=2, keepdims=True)
    k_dec = k * jnp.exp(g_last - gc_col)
    S_out = S * jnp.exp(g_last) + _bdot(k_dec, v_new, 1, 1, False)
    return o, S_out


def _heads(ref, w):
    return jnp.stack([ref[:, h * w:(h + 1) * w] for h in range(NH)])


def _gdn_specs(NC, rev):
    ix = (lambda i: NC - 1 - i) if rev else (lambda i: i)
    wide = pl.BlockSpec((CHUNK, D), lambda i: (ix(i), 0))
    col = pl.BlockSpec((CHUNK, NH), lambda i: (ix(i), 0))
    row = pl.BlockSpec((1, NH, CHUNK), lambda i: (ix(i), 0, 0))
    st = pl.BlockSpec((1, NH, DH, DH), lambda i: (ix(i), 0, 0, 0))
    return wide, col, row, st


def _gdn_fwd(name, q, k, v, gcol, grow, bcol):
    S = q.shape[0]
    NC = S // CHUNK

    def body(q_ref, k_ref, v_ref, gc_ref, gr_ref, b_ref, o_ref, ss_ref, st):
        @pl.when(pl.program_id(0) == 0)
        def _():
            st[...] = jnp.zeros(st.shape, F32)
        s_in = st[...]
        ss_ref[0] = s_in
        grow = jnp.stack([gr_ref[0, h:h + 1, :] for h in range(NH)])
        o, s_out = _gdn_chunk(_heads(q_ref, DH), _heads(k_ref, DH), _heads(v_ref, DH), _heads(gc_ref, 1), grow,
                              _heads(b_ref, 1), s_in)
        for h in range(NH):
            o_ref[:, h * DH:(h + 1) * DH] = o[h]
        st[...] = s_out

    wide, col, row, stsp = _gdn_specs(NC, False)
    return pl.pallas_call(body, name=name, grid=(NC,), in_specs=[wide, wide, wide, col, row, col],
                          out_specs=[wide, stsp],
                          out_shape=[jax.ShapeDtypeStruct((S, D), F32), jax.ShapeDtypeStruct((NC, NH, DH, DH), F32)],
                          scratch_shapes=[pltpu.VMEM((NH, DH, DH), F32)],
                          compiler_params=_cparams(("arbitrary",)))(q, k, v, gcol, grow, bcol)


def _gdn_bwd(name, q, k, v, gcol, grow, bcol, ssave, do):
    S = q.shape[0]
    NC = S // CHUNK

    def body(q_ref, k_ref, v_ref, gc_ref, gr_ref, b_ref, ss_ref, do_ref, dq_ref, dk_ref, dv_ref, dgc_ref, dgr_ref, db_ref, dst):
        @pl.when(pl.program_id(0) == 0)
        def _():
            dst[...] = jnp.zeros(dst.shape, F32)
        grow = jnp.stack([gr_ref[0, h:h + 1, :] for h in range(NH)])
        prim = (_heads(q_ref, DH), _heads(k_ref, DH), _heads(v_ref, DH), _heads(gc_ref, 1), grow, _heads(b_ref, 1), ss_ref[0])
        _, vjp = jax.vjp(_gdn_chunk, *prim)
        dq, dk, dv, dgc, dgr, db, ds = vjp((_heads(do_ref, DH), dst[...]))
        for h in range(NH):
            hs = slice(h * DH, (h + 1) * DH)
            dq_ref[:, hs] = dq[h]
            dk_ref[:, hs] = dk[h]
            dv_ref[:, hs] = dv[h]
            dgc_ref[:, h:h + 1] = dgc[h]
            dgr_ref[0, h:h + 1, :] = dgr[h]
            db_ref[:, h:h + 1] = db[h]
        dst[...] = ds

    wide, col, row, stsp = _gdn_specs(NC, True)
    return pl.pallas_call(body, name=name, grid=(NC,), in_specs=[wide, wide, wide, col, row, col, stsp, wide],
                          out_specs=[wide, wide, wide, col, row, col],
                          out_shape=[jax.ShapeDtypeStruct((S, D), F32)] * 3 + [jax.ShapeDtypeStruct((S, NH), F32),
                                                                                 jax.ShapeDtypeStruct((NC, NH, CHUNK), F32),
                                                                                 jax.ShapeDtypeStruct((S, NH), F32)],
                          scratch_shapes=[pltpu.VMEM((NH, DH, DH), F32)],
                          compiler_params=_cparams(("arbitrary",)))(q, k, v, gcol, grow, bcol, ssave, do)


TQ = 512
SM_SCALE = QKH ** -0.5
NEG = -1e30


def _diag_mask(transposed):
    r = lax.broadcasted_iota(jnp.int32, (TQ, TQ), 0) // CHUNK
    c = lax.broadcasted_iota(jnp.int32, (TQ, TQ), 1) // CHUNK
    return (r <= c) if transposed else (c <= r)


def _dot_nt(a, b):
    return lax.dot_general(a, b, (((1,), (1,)), ((), ())), preferred_element_type=F32)


def _flash_fwd(name, qp, kp, kv):
    S = qp.shape[0]
    nq = S // TQ

    def body(q_ref, k_ref, v_ref, o_ref, lse_ref):
        qi = pl.program_id(1)
        q = q_ref[...]

        def step(j, carry, masked):
            m, l, acc = carry
            rows = pl.ds(pl.multiple_of(j * TQ, TQ), TQ)
            s = _dot_nt(q, k_ref[rows, :]) * SM_SCALE
            if masked:
                s = jnp.where(_diag_mask(False), s, NEG)
            m_new = jnp.maximum(m, jnp.max(s, axis=-1, keepdims=True))
            p = jnp.exp(s - m_new)
            alpha = jnp.exp(m - m_new)
            l = alpha * l + jnp.sum(p, axis=-1, keepdims=True)
            acc = alpha * acc + jnp.dot(p.astype(BF16), v_ref[rows, :].astype(BF16), preferred_element_type=F32)
            return m_new, l, acc

        carry = (jnp.full((TQ, 1), NEG, F32), jnp.zeros((TQ, 1), F32), jnp.zeros((TQ, DH), F32))
        carry = lax.fori_loop(0, qi, lambda j, c: step(j, c, False), carry)
        m, l, acc = step(qi, carry, True)
        o_ref[...] = acc / l
        lse_ref[0] = m + jnp.log(l)

    return pl.pallas_call(
        body, name=name, grid=(NH, nq),
        in_specs=[pl.BlockSpec((TQ, HP), lambda h, i: (i, h)), pl.BlockSpec((S, HP), lambda h, i: (0, h)),
                  pl.BlockSpec((S, DH), lambda h, i: (0, NH + h))],
        out_specs=[pl.BlockSpec((TQ, DH), lambda h, i: (i, h)), pl.BlockSpec((1, TQ, 1), lambda h, i: (h, i, 0))],
        out_shape=[jax.ShapeDtypeStruct((S, NH * DH), F32), jax.ShapeDtypeStruct((NH, S, 1), F32)],
        compiler_params=_cparams(("parallel", "arbitrary")))(qp, kp, kv)


def _flash_bwd_dq(name, qp, kp, kv, o, do, lse):
    S = qp.shape[0]
    nq = S // TQ

    def body(q_ref, k_ref, v_ref, o_ref, do_ref, lse_ref, dq_ref, dl_ref):
        qi = pl.program_id(1)
        q = q_ref[...]
        do = do_ref[...]
        delta = jnp.sum(o_ref[...] * do, axis=-1, keepdims=True)
        dl_ref[0] = delta
        dob = do.astype(BF16)
        lse = lse_ref[0]

        def step(j, dq, masked):
            rows = pl.ds(pl.multiple_of(j * TQ, TQ), TQ)
            k = k_ref[rows, :]
            s = _dot_nt(q, k) * SM_SCALE
            if masked:
                s = jnp.where(_diag_mask(False), s, NEG)
            p = jnp.exp(s - lse)
            dp = _dot_nt(dob, v_ref[rows, :].astype(BF16))
            ds = p * (dp - delta) * SM_SCALE
            return dq + jnp.dot(ds.astype(BF16), k, preferred_element_type=F32)

        dq = lax.fori_loop(0, qi, lambda j, c: step(j, c, False), jnp.zeros((TQ, HP), F32))
        dq_ref[...] = step(qi, dq, True)

    return pl.pallas_call(
        body, name=name, grid=(NH, nq),
        in_specs=[pl.BlockSpec((TQ, HP), lambda h, i: (i, h)), pl.BlockSpec((S, HP), lambda h, i: (0, h)),
                  pl.BlockSpec((S, DH), lambda h, i: (0, NH + h)), pl.BlockSpec((TQ, DH), lambda h, i: (i, h)),
                  pl.BlockSpec((TQ, DH), lambda h, i: (i, h)), pl.BlockSpec((1, TQ, 1), lambda h, i: (h, i, 0))],
        out_specs=[pl.BlockSpec((TQ, HP), lambda h, i: (i, h)), pl.BlockSpec((1, TQ, 1), lambda h, i: (h, i, 0))],
        out_shape=[jax.ShapeDtypeStruct((S, NH * HP), F32), jax.ShapeDtypeStruct((NH, S, 1), F32)],
        compiler_params=_cparams(("parallel", "arbitrary")))(qp, kp, kv, o, do, lse)


def _flash_bwd_dkv(name, qp, kp, kv, do, lse_row, delta_row):
    S = qp.shape[0]
    nq = S // TQ

    def body(q_ref, k_ref, v_ref, do_ref, lse_ref, dl_ref, dk_ref, dv_ref):
        kj = pl.program_id(1)
        k = k_ref[...]
        vb = v_ref[...].astype(BF16)

        def step(i, carry, masked):
            dk, dv = carry
            rows = pl.ds(pl.multiple_of(i * TQ, TQ), TQ)
            q = q_ref[rows, :]
            dob = do_ref[rows, :].astype(BF16)
            st = _dot_nt(k, q) * SM_SCALE
            pt = jnp.exp(st - lse_ref[0, :, rows])
            if masked:
                pt = jnp.where(_diag_mask(True), pt, 0.0)
            dv = dv + jnp.dot(pt.astype(BF16), dob, preferred_element_type=F32)
            dpt = _dot_nt(vb, dob)
            dst = pt * (dpt - dl_ref[0, :, rows]) * SM_SCALE
            dk = dk + jnp.dot(dst.astype(BF16), q, preferred_element_type=F32)
            return dk, dv

        carry = step(kj, (jnp.zeros((TQ, HP), F32), jnp.zeros((TQ, DH), F32)), True)
        dk, dv = lax.fori_loop(kj + 1, nq, lambda i, c: step(i, c, False), carry)
        dk_ref[...] = dk
        dv_ref[...] = dv

    return pl.pallas_call(
        body, name=name, grid=(NH, nq),
        in_specs=[pl.BlockSpec((S, HP), lambda h, j: (0, h)), pl.BlockSpec((TQ, HP), lambda h, j: (j, h)),
                  pl.BlockSpec((TQ, DH), lambda h, j: (j, NH + h)), pl.BlockSpec((S, DH), lambda h, j: (0, h)),
                  pl.BlockSpec((1, 1, S), lambda h, j: (h, 0, 0)), pl.BlockSpec((1, 1, S), lambda h, j: (h, 0, 0))],
        out_specs=[pl.BlockSpec((TQ, HP), lambda h, j: (j, h)), pl.BlockSpec((TQ, DH), lambda h, j: (j, h))],
        out_shape=[jax.ShapeDtypeStruct((S, NH * HP), F32), jax.ShapeDtypeStruct((S, NH * DH), F32)],
        compiler_params=_cparams(("parallel", "arbitrary")))(qp, kp, kv, do, lse_row, delta_row)


def _tm(S, width):
    t = 512 if width <= 1024 else (256 if width <= 3072 else 128)
    return min(t, S)


def _mod_fwd(tag, x, g, shift, scale):
    S = x.shape[0]
    return _rw_fwd(tag + "_mod", f_mod, [Row(x)], [g, shift, scale], [((D,), BF16)], _tm(S, D))[0]


def _mod_bwd(tag, x, g, shift, scale, dh, dx_direct):
    S = x.shape[0]
    r = _rw_bwd(tag + "_mod_b", f_mod, [Row(x)], [g, shift, scale], [Row(dh)], [True], [True] * 3, [((D,), F32)],
                _tm(S, D), add=Row(dx_direct))
    return r[0], r[1:]


def _res_fwd(tag, x, y, gate, coef):
    S = x.shape[0]

    def fn(pieces, bvals):
        return [pieces[0] + coef * bvals[0] * pieces[1]], []
    return _rowwise(tag + "_res", fn, [Row(x), Row(y)], [gate], [((D,), F32)], [], _tm(S, D))[0]


def _res_bwd(tag, y, gate, dxn, coef):
    S = y.shape[0]
    r = _rw_bwd(tag + "_res_b", make_f_res(coef), [Row(y)], [gate], [Row(dxn)], [True], [True], [((D,), BF16)], _tm(S, D))
    return r[0], r[1]


def _ffn_fwd(tag, x, mod3, g, w_in4, w_out4, li):
    shift, scale, gate = mod3
    S = x.shape[0]
    h = _mod_fwd(tag, x, g, shift, scale)
    gu = _matmul(tag + "_in", h, w_in4, lay="b_cols", li=li)
    a = _rw_fwd(tag + "_act", f_act, [Row(gu, splits=[FF, FF])], [], [((FF,), BF16)], _tm(S, 2 * FF))[0]
    y = _matmul(tag + "_out", a, w_out4, lay="b_rows", li=li)
    xn = _res_fwd(tag, x, y, gate, 0.5)
    return xn, (x, h, gu, a, y)


def _ffn_bwd(tag, dxn, res, mod3, g, w_in4, w_out4, li, g_in4, g_out4):
    shift, scale, gate = mod3
    x, h, gu, a, y = res
    S = x.shape[0]
    nmat = w_in4.shape[1]
    dy, dgate = _res_bwd(tag, y, gate, dxn, 0.5)
    da = _matmul(tag + "_out_bi", dy, w_out4, "nt", lay="b_rows", li=li)
    g_out4 = _matmul(tag + "_out_bw", a, dy, "tn", lay="o_rows", li=li, into=g_out4, nmat=nmat)
    dgu = _rw_bwd(tag + "_act_b", f_act, [Row(gu, splits=[FF, FF])], [], [Row(da)], [True, True], [],
                  [((FF, FF), BF16)], _tm(S, 2 * FF))[0]
    dh = _matmul(tag + "_in_bi", dgu, w_in4, "nt", lay="b_cols", li=li)
    g_in4 = _matmul(tag + "_in_bw", h, dgu, "tn", lay="o_cols", li=li, into=g_in4, nmat=nmat)
    dx, (dg, dshift, dscale) = _mod_bwd(tag, x, g, shift, scale, dh, dxn)
    return dx, g_in4, g_out4, dict(g=dg, mod=(dshift, dscale, dgate))


def _pad_lanes(a, lo, width=LANE):
    return jnp.pad(a, ((0, 0), (lo, width - lo - a.shape[1])))


def _gdn_layer_fwd(tag, x, mod3, g, p):
    shift, scale, gate = mod3
    S = x.shape[0]
    NC = S // CHUNK
    h = _mod_fwd(tag, x, g, shift, scale)
    proj = _matmul(tag + "_in", h, p["w_in"])
    qc = _conv_fwd(tag + "_conv", proj, p["conv_w8"], 3 * D, _tm(S, 3 * D))
    q, k, v = _rw_fwd(tag + "_pre", f_gdnpre, [Row(qc, splits=[DH] * (3 * NH))], [],
                      [((DH,) * NH, F32)] * 3, _tm(S, 3 * D))
    betaf, gf = _rw_fwd(tag + "_gates", f_gates, [Row(proj, LANE, cb=GATE_CB)], [p["a_log128"], p["dt_bias128"]],
                        [((LANE,), F32)] * 2, _tm(S, LANE))
    bcol, gcol = betaf[:, :NH], gf[:, NH:2 * NH]
    grow = gcol.reshape(NC, CHUNK, NH).transpose(0, 2, 1)
    o, ssave = _gdn_fwd(tag + "_core", q, k, v, gcol, grow, bcol)
    on = _rw_fwd(tag + "_post", f_gdnpost, [Row(o, splits=[DH] * NH), Row(proj, D, cb=3, splits=[DH] * NH)],
                 [p["norm_g"]], [((DH,) * NH, BF16)], _tm(S, 2 * D))[0]
    y = _matmul(tag + "_out", on, p["w_out"])
    xn = _res_fwd(tag, x, y, gate, 1.0)
    return xn, (x, h, proj, qc, q, k, v, gcol, grow, bcol, ssave, o, on, y)


def _gdn_layer_bwd(tag, dxn, res, mod3, g, p):
    shift, scale, gate = mod3
    x, h, proj, qc, q, k, v, gcol, grow, bcol, ssave, o, on, y = res
    S = x.shape[0]
    dy, dgate = _res_bwd(tag, y, gate, dxn, 1.0)
    don = _matmul(tag + "_out_bi", dy, p["w_out"], "nt")
    dw_out = _matmul(tag + "_out_bw", on, dy, "tn")
    do, dz, dnorm = _rw_bwd(tag + "_post_b", f_gdnpost, [Row(o, splits=[DH] * NH), Row(proj, D, cb=3, splits=[DH] * NH)],
                            [p["norm_g"]], [Row(don, splits=[DH] * NH)], [True] * (2 * NH), [True],
                            [((DH,) * NH, F32), ((DH,) * NH, BF16)], _tm(S, 2 * D))
    dq, dk, dv, dgc, dgr, db = _gdn_bwd(tag + "_core_b", q, k, v, gcol, grow, bcol, ssave, do)
    dgcol = dgc + dgr.transpose(0, 2, 1).reshape(S, NH)
    dgates, da_log, ddt = _rw_bwd(tag + "_gates_b", f_gates, [Row(proj, LANE, cb=GATE_CB)], [p["a_log128"], p["dt_bias128"]],
                                  [Row(_pad_lanes(db, 0)), Row(_pad_lanes(dgcol, NH))], [True], [True, True],
                                  [((LANE,), BF16)], _tm(S, LANE))
    dqc = _rw_bwd(tag + "_pre_b", f_gdnpre, [Row(qc, splits=[DH] * (3 * NH))], [],
                  [Row(dq, splits=[DH] * NH), Row(dk, splits=[DH] * NH), Row(dv, splits=[DH] * NH)],
                  [True] * (3 * NH), [], [((DH,) * (3 * NH), F32)], _tm(S, 3 * D))[0]
    dqkv, dconv = _conv_bwd(tag + "_conv_b", proj, dqc, p["conv_w8"], 3 * D, _tm(S, 3 * D), BF16)
    dproj = jnp.concatenate([dqkv, dz, dgates], axis=1)
    dh = _matmul(tag + "_in_bi", dproj, p["w_in"], "nt")
    dw_in = _matmul(tag + "_in_bw", h, dproj, "tn")
    dx, (dg, dshift, dscale) = _mod_bwd(tag, x, g, shift, scale, dh, dxn)
    return dx, dict(w_in=dw_in, conv_w8=dconv, a_log128=da_log, dt_bias128=ddt, norm_g=dnorm,
                    w_out=dw_out, g=dg, mod=(dshift, dscale, dgate))


def _qk_rows(src, shared_rope, ckv=None):
    if shared_rope:
        return [Row(src, D, cb=0, splits=[DH] * NH), Row(ckv, LANE, cb=2)]
    return [Row(src, splits=[DH] * (2 * NH))]


def _kv_fwd(x, kvmod, p, tabs):
    shift, scale = kvmod
    S = x.shape[0]
    h = _mod_fwd("kv", x, p["kv_norm_g"], shift, scale)
    ckv = _matmul("kv_dkv", h, p["w_dkv"])
    lat = _rw_fwd("kv_lat", f_rms, [Row(ckv, KVL)], [p["kv_lat_g"]], [((KVL,), BF16)], _tm(S, KVL))[0]
    kvf = _matmul("kv_ukv", lat, p["w_ukv"])
    kp = _rw_fwd("kv_k", make_f_qk(True), _qk_rows(kvf, True, ckv) + [Row(tabs[0]), Row(tabs[1])],
                 [p["k_gn"], p["k_gr"], p["pm"]], [((DH,) * (2 * NH), BF16)], _tm(S, 2 * D))[0]
    return kp, kvf, (x, h, ckv, lat)


def _kv_bwd(dkp, dv, dx_direct, res, kvmod, kvf, p, tabs):
    shift, scale = kvmod
    x, h, ckv, lat = res
    S = x.shape[0]
    dkn, dkr, dgn, dgr = _rw_bwd("kv_k_b", make_f_qk(True), _qk_rows(kvf, True, ckv) + [Row(tabs[0]), Row(tabs[1])],
                                 [p["k_gn"], p["k_gr"], p["pm"]], [Row(dkp, splits=[DH] * (2 * NH))],
                                 [True] * (NH + 1) + [False, False], [True, True, False],
                                 [((DH,) * NH, BF16), ((LANE,), BF16)], _tm(S, 2 * D))
    dkvf = jnp.concatenate([dkn, dv.astype(BF16)], axis=1)
    dlat = _matmul("kv_ukv_bi", dkvf, p["w_ukv"], "nt")
    dw_ukv = _matmul("kv_ukv_bw", lat, dkvf, "tn")
    dcl, dlg = _rw_bwd("kv_lat_b", f_rms, [Row(ckv, KVL)], [p["kv_lat_g"]], [Row(dlat)], [True], [True],
                       [((KVL,), BF16)], _tm(S, KVL))
    dckv = jnp.concatenate([dcl, dkr], axis=1)
    dh = _matmul("kv_dkv_bi", dckv, p["w_dkv"], "nt")
    dw_dkv = _matmul("kv_dkv_bw", h, dckv, "tn")
    dx, (dg, dshift, dscale) = _mod_bwd("kv", x, p["kv_norm_g"], shift, scale, dh, dx_direct)
    return dx, dict(w_dkv=dw_dkv, w_ukv=dw_ukv, kv_lat_g=dlg, k_gn=dgn, k_gr=dgr, kv_norm_g=dg, mod=(dshift, dscale))


def _mla_layer_fwd(tag, x, mod3, g, p, kp, kvf, tabs):
    shift, scale, gate = mod3
    S = x.shape[0]
    h = _mod_fwd(tag, x, g, shift, scale)
    ql = _matmul(tag + "_dq", h, p["w_dq"])
    qln = _rw_fwd(tag + "_qln", f_rms, [Row(ql)], [p["ql_g"]], [((QL,), BF16)], _tm(S, QL))[0]
    qu = _matmul(tag + "_uq", qln, p["w_uq"])
    qp = _rw_fwd(tag + "_q", make_f_qk(False), _qk_rows(qu, False) + [Row(tabs[0]), Row(tabs[1])],
                 [p["q_gn"], p["q_gr"], p["pm"]], [((DH,) * (2 * NH), BF16)], _tm(S, 2 * D))[0]
    o, lse = _flash_fwd(tag + "_att", qp, kp, kvf)
    y = _matmul(tag + "_out", o, p["w_out"])
    xn = _res_fwd(tag, x, y, gate, 1.0)
    return xn, (x, h, ql, qln, qu, qp, o, lse, y)


def _mla_layer_bwd(tag, dxn, res, mod3, g, p, kp, kvf, tabs):
    shift, scale, gate = mod3
    x, h, ql, qln, qu, qp, o, lse, y = res
    S = x.shape[0]
    dy, dgate = _res_bwd(tag, y, gate, dxn, 1.0)
    do = _matmul(tag + "_out_bi", dy, p["w_out"], "nt")
    dw_out = _matmul(tag + "_out_bw", o, dy, "tn")
    dqp, delta = _flash_bwd_dq(tag + "_att_bq", qp, kp, kvf, o, do, lse)
    dkp, dv = _flash_bwd_dkv(tag + "_att_bkv", qp, kp, kvf, do, lse.reshape(NH, 1, S), delta.reshape(NH, 1, S))
    dqu, dgn, dgr = _rw_bwd(tag + "_q_b", make_f_qk(False), _qk_rows(qu, False) + [Row(tabs[0]), Row(tabs[1])],
                            [p["q_gn"], p["q_gr"], p["pm"]], [Row(dqp, splits=[DH] * (2 * NH))],
                            [True] * (2 * NH) + [False, False], [True, True, False],
                            [((DH,) * (2 * NH), BF16)], _tm(S, 2 * D))
    dqln = _matmul(tag + "_uq_bi", dqu, p["w_uq"], "nt")
    dw_uq = _matmul(tag + "_uq_bw", qln, dqu, "tn")
    dql, dqlg = _rw_bwd(tag + "_qln_b", f_rms, [Row(ql)], [p["ql_g"]], [Row(dqln)], [True], [True], [((QL,), BF16)],
                        _tm(S, QL))
    dh = _matmul(tag + "_dq_bi", dql, p["w_dq"], "nt")
    dw_dq = _matmul(tag + "_dq_bw", h, dql, "tn")
    dx, (dg, dshift, dscale) = _mod_bwd(tag, x, g, shift, scale, dh, dxn)
    return dx, dkp, dv, dict(w_dq=dw_dq, w_uq=dw_uq, w_out=dw_out, ql_g=dqlg, q_gn=dgn, q_gr=dgr, g=dg,
                             mod=(dshift, dscale, dgate))


def _loss_head(y, tgt):
    S = y.shape[0]

    def fn(pieces, bvals):
        e = pieces[0] - pieces[1]
        part = jnp.sum(e * e) * (0.5 / D)
        return [e * (1.0 / D)], [jnp.full((1, LANE), part, F32)]
    dy, part = _rowwise("loss", fn, [Row(y), Row(tgt)], [], [((D,), F32)], [(1, LANE)], _tm(S, D))
    return part[0, 0], dy


def _rope_tables(positions):
    S = positions.shape[0]
    half = ROPE // 2
    lane = lax.broadcasted_iota(jnp.int32, (1, LANE), 1)
    inv_freq = ROPE_BASE ** (-(lane % half).astype(F32) / half)
    live = (lane < ROPE).astype(F32)
    sign = jnp.where(lane < half, -1.0, 1.0) * live

    def fn(pieces, bvals):
        ang = pieces[0] * bvals[0]
        return [jnp.cos(ang) * bvals[1], jnp.sin(ang) * bvals[2]], []
    pos = jnp.broadcast_to(positions.astype(F32)[:, None], (S, LANE))
    cosp, sins = _rowwise("rope_tab", fn, [Row(pos)], [inv_freq, live, sign], [((LANE,), F32)] * 2, [], _tm(S, LANE))
    r = lax.broadcasted_iota(jnp.int32, (LANE, LANE), 0)
    c = lax.broadcasted_iota(jnp.int32, (LANE, LANE), 1)
    pm = (((c < half) & (r == c + half)) | ((c >= half) & (c < ROPE) & (r == c - half))).astype(F32)
    return (cosp, sins), pm


def _adamw(name, w, g, m, v):
    shape = w.shape
    C = shape[-1]
    R = w.size // C
    tr = R
    for t in (1024, 512, 256, 128, 64, 32, 16, 8):
        if R % t == 0 and t * C * 4 <= (1 << 21):
            tr = t
            break
    c1 = 1.0 - ADAM_B1 ** ADAM_STEP
    c2 = 1.0 - ADAM_B2 ** ADAM_STEP

    def body(w_ref, g_ref, m_ref, v_ref, d_ref, mo_ref, vo_ref):
        gg = g_ref[...]
        mn = ADAM_B1 * m_ref[...] + (1.0 - ADAM_B1) * gg
        vn = ADAM_B2 * v_ref[...] + (1.0 - ADAM_B2) * (gg * gg)
        d_ref[...] = -ADAM_LR * ((mn / c1) / (jnp.sqrt(vn / c2) + ADAM_EPS) + ADAM_WD * w_ref[...])
        mo_ref[...] = mn
        vo_ref[...] = vn

    spec = pl.BlockSpec((tr, C), lambda i: (i, 0))
    outs = pl.pallas_call(body, name=name, grid=(R // tr,), in_specs=[spec] * 4, out_specs=[spec] * 3,
                          out_shape=[jax.ShapeDtypeStruct((R, C), F32)] * 3,
                          compiler_params=_cparams(("parallel",)))(*[t.reshape(R, C) for t in (w, g, m, v)])
    return [o.reshape(shape) for o in outs]


HBM_SPEC = pl.BlockSpec(memory_space=pltpu.HBM)
OTHER_CHIPS = (4, 2, 6)
SIBLING = 1


def _me():
    return lax.axis_index("x"), lax.axis_index("y"), lax.axis_index("c")


def _peer(me, k):
    mx, my, mc = me
    return ((1 - mx) if k & 4 else mx, (1 - my) if k & 2 else my, (1 - mc) if k & 1 else mc)


def _rcopy(src, dst, ssem, rsem, to):
    return pltpu.make_async_remote_copy(src_ref=src, dst_ref=dst, send_sem=ssem, recv_sem=rsem, device_id=to,
                                        device_id_type=MESH)


def _all_gather8(name, x):
    def body(x_ref, o_ref, ssem, rsem, lsem):
        me = _me()
        mine = 4 * me[0] + 2 * me[1] + me[2]
        loc = pltpu.make_async_copy(x_ref, o_ref.at[mine], lsem)
        loc.start()
        sends = []
        for k in range(1, 8):
            cp = _rcopy(x_ref, o_ref.at[mine], ssem.at[k - 1], rsem.at[k - 1], _peer(me, k))
            cp.start()
            sends.append(cp)
        for k in range(1, 8):
            px, py, pc = _peer(me, k)
            _rcopy(x_ref, o_ref.at[4 * px + 2 * py + pc], ssem.at[k - 1], rsem.at[k - 1], (px, py, pc)).wait_recv()
        for cp in sends:
            cp.wait_send()
        loc.wait()

    return pl.pallas_call(body, name=name, out_shape=jax.ShapeDtypeStruct((8,) + x.shape, x.dtype),
                          in_specs=[HBM_SPEC], out_specs=HBM_SPEC,
                          scratch_shapes=[pltpu.SemaphoreType.DMA((7,)), pltpu.SemaphoreType.DMA((7,)),
                                          pltpu.SemaphoreType.DMA(())])(x)


PACK_L = 1024
PACK_RT = 256


def _place_shard(name, wp, chip):
    rh, ln = wp.shape[1:]

    def body(s_ref, w_ref, o_ref):
        o_ref[...] = w_ref[...]

    gs = pltpu.PrefetchScalarGridSpec(
        num_scalar_prefetch=1, grid=(2, rh // PACK_RT),
        in_specs=[pl.BlockSpec((None, PACK_RT, ln), lambda h, i, s_ref: (h, i, 0))],
        out_specs=pl.BlockSpec((None, None, PACK_RT, ln), lambda h, i, s_ref: (s_ref[0], h, i, 0)))
    return pl.pallas_call(body, name=name, grid_spec=gs, out_shape=jax.ShapeDtypeStruct((4,) + wp.shape, wp.dtype),
                          compiler_params=_cparams(("parallel", "parallel")))(chip.reshape(1).astype(jnp.int32), wp)


def _gather_weights(name, w4):
    def body(w_ref, o_ref, ssem, rsem):
        me = _me()
        mc = me[2]
        mine = o_ref.at[2 * me[0] + me[1], mc]
        first = []
        for j, k in enumerate(OTHER_CHIPS):
            cp = _rcopy(mine, mine, ssem.at[j], rsem.at[j], _peer(me, k))
            cp.start()
            first.append(cp)
        passed = []
        for j, k in enumerate(OTHER_CHIPS):
            px, py, _ = _peer(me, k)
            land = o_ref.at[2 * px + py, mc]
            _rcopy(land, land, ssem.at[j], rsem.at[j], _peer(me, k)).wait_recv()
            fw = _rcopy(land, land, ssem.at[3 + j], rsem.at[3 + j], _peer(me, SIBLING))
            fw.start()
            passed.append(fw)
        for j, k in enumerate(OTHER_CHIPS):
            px, py, _ = _peer(me, k)
            land = o_ref.at[2 * px + py, 1 - mc]
            _rcopy(land, land, ssem.at[3 + j], rsem.at[3 + j], _peer(me, SIBLING)).wait_recv()
        for cp in first + passed:
            cp.wait_send()

    return pl.pallas_call(body, name=name, out_shape=jax.ShapeDtypeStruct(w4.shape, w4.dtype),
                          in_specs=[HBM_SPEC], out_specs=HBM_SPEC, input_output_aliases={0: 0},
                          scratch_shapes=[pltpu.SemaphoreType.DMA((6,)), pltpu.SemaphoreType.DMA((6,))])(w4)


def _exchange_half(name, g):
    def body(g_ref, p_ref, ssem, rsem):
        me = _me()
        cps = []
        for s in range(4):
            cp = _rcopy(g_ref.at[s, 1 - me[2]], p_ref.at[s], ssem.at[s], rsem.at[s], _peer(me, SIBLING))
            cp.start()
            cps.append(cp)
        for cp in cps:
            cp.wait()

    return pl.pallas_call(body, name=name, out_shape=jax.ShapeDtypeStruct((4,) + g.shape[2:], g.dtype),
                          in_specs=[HBM_SPEC], out_specs=HBM_SPEC,
                          scratch_shapes=[pltpu.SemaphoreType.DMA((4,)), pltpu.SemaphoreType.DMA((4,))])(g)


def _scatter_chips(name, q):
    def body(q_ref, t_ref, ssem, rsem):
        me = _me()
        cps = []
        for j, k in enumerate(OTHER_CHIPS):
            px, py, _ = _peer(me, k)
            cp = _rcopy(q_ref.at[2 * px + py], t_ref.at[j], ssem.at[j], rsem.at[j], _peer(me, k))
            cp.start()
            cps.append(cp)
        for cp in cps:
            cp.wait()

    return pl.pallas_call(body, name=name, out_shape=jax.ShapeDtypeStruct((3,) + q.shape[1:], q.dtype),
                          in_specs=[HBM_SPEC], out_specs=HBM_SPEC,
                          scratch_shapes=[pltpu.SemaphoreType.DMA((3,)), pltpu.SemaphoreType.DMA((3,))])(q)


def _exchange_full(name, r2):
    def body(r_ref, o_ref, ssem, rsem):
        me = _me()
        mc = me[2]
        cp = _rcopy(o_ref.at[mc], o_ref.at[mc], ssem, rsem, _peer(me, SIBLING))
        cp.start()
        _rcopy(o_ref.at[1 - mc], o_ref.at[1 - mc], ssem, rsem, _peer(me, SIBLING)).wait_recv()
        cp.wait_send()

    return pl.pallas_call(body, name=name, out_shape=jax.ShapeDtypeStruct(r2.shape, r2.dtype),
                          in_specs=[HBM_SPEC], out_specs=HBM_SPEC, input_output_aliases={0: 0},
                          scratch_shapes=[pltpu.SemaphoreType.DMA(()), pltpu.SemaphoreType.DMA(())])(r2)


def _add_half(name, g, p, c):
    rh, ln = g.shape[2:]

    def body(c_ref, g_ref, p_ref, o_ref):
        o_ref[0] = (g_ref[0, 0] + p_ref[0]).astype(o_ref.dtype)

    gs = pltpu.PrefetchScalarGridSpec(
        num_scalar_prefetch=1, grid=(4, rh // PACK_RT),
        in_specs=[pl.BlockSpec((1, 1, PACK_RT, ln), lambda s, i, c_ref: (s, c_ref[0], i, 0)),
                  pl.BlockSpec((1, PACK_RT, ln), lambda s, i, c_ref: (s, i, 0))],
        out_specs=pl.BlockSpec((1, PACK_RT, ln), lambda s, i, c_ref: (s, i, 0)))
    return pl.pallas_call(body, name=name, grid_spec=gs, out_shape=jax.ShapeDtypeStruct((4, rh, ln), BF16),
                          compiler_params=_cparams(("parallel", "parallel")))(c.reshape(1).astype(jnp.int32), g, p)


def _add_chips(name, q, t, chip, c):
    rh, ln = q.shape[1:]

    def body(s_ref, c_ref, q_ref, t_ref, o_ref):
        o_ref[...] = ((q_ref[0].astype(F32) + t_ref[0].astype(F32)) + t_ref[1].astype(F32)) + t_ref[2].astype(F32)

    gs = pltpu.PrefetchScalarGridSpec(
        num_scalar_prefetch=2, grid=(rh // PACK_RT,),
        in_specs=[pl.BlockSpec((1, PACK_RT, ln), lambda i, s_ref, c_ref: (s_ref[0], i, 0)),
                  pl.BlockSpec((3, PACK_RT, ln), lambda i, s_ref, c_ref: (0, i, 0))],
        out_specs=pl.BlockSpec((None, PACK_RT, ln), lambda i, s_ref, c_ref: (c_ref[0], i, 0)))
    return pl.pallas_call(body, name=name, grid_spec=gs, out_shape=jax.ShapeDtypeStruct((2, rh, ln), F32),
                          compiler_params=_cparams(("parallel",)))(chip.reshape(1).astype(jnp.int32),
                                                                    c.reshape(1).astype(jnp.int32), q, t)


def _sum8(name, a):
    def body(a_ref, o_ref):
        acc = a_ref[0]
        for d in range(1, 8):
            acc = acc + a_ref[d]
        o_ref[...] = acc
    return pl.pallas_call(body, name=name, out_shape=jax.ShapeDtypeStruct(a.shape[1:], F32))(a)


def _silu_rows(name, a):
    def body(a_ref, o_ref):
        o_ref[...] = _silu(a_ref[...])
    return pl.pallas_call(body, name=name, out_shape=jax.ShapeDtypeStruct(a.shape, F32))(a)


REST = (("gdn_w_out", 1), ("mla_w_dkv", 0), ("mla_w_ukv", 1), ("mla_w_dq", 1), ("mla_w_uq", 2), ("mla_w_out", 1))


def _packed_rows(n):
    per_half = -(-n // (2 * PACK_L))
    return -(-per_half // PACK_RT) * PACK_RT


def _pack_flat(flat):
    n = flat.shape[-1]
    rh = _packed_rows(n)
    pad = [(0, 0)] * (flat.ndim - 1) + [(0, 2 * rh * PACK_L - n)]
    return jnp.pad(flat, pad).reshape(flat.shape[:-1] + (2, rh, PACK_L))


def _shards_first(full, axis):
    sh = full.shape
    t = full.reshape(sh[:axis] + (4, sh[axis] // 4) + sh[axis + 1:])
    return jnp.moveaxis(t, axis, 0)


def _shards_merge(stacked, axis):
    t = jnp.moveaxis(stacked, 0, axis)
    sh = t.shape
    return t.reshape(sh[:axis] + (4 * sh[axis + 1],) + sh[axis + 2:])


def _pack_small(parts):
    flat = jnp.concatenate([p.reshape(-1).astype(F32) for p in parts])
    n = flat.shape[0]
    rows = -(-n // (SUB * LANE)) * SUB
    return jnp.pad(flat, (0, rows * LANE - n)).reshape(rows, LANE)


def _unpack_small(buf, shapes):
    lead = buf.shape[:-2]
    flat = buf.reshape(lead + (-1,))
    out, off = [], 0
    for sh in shapes:
        n = 1
        for d in sh:
            n *= d
        out.append(flat[..., off:off + n].reshape(lead + tuple(sh)))
        off += n
    return out


WEIGHTS = ('ada_w', 'ada_b', 'norm_g', 'ffn_w_in', 'ffn_w_out', 'gdn_w_in', 'gdn_conv_w', 'gdn_a_log', 'gdn_dt_bias',
           'gdn_norm_g', 'gdn_w_out', 'kv_ada_w', 'kv_ada_b', 'kv_norm_g', 'mla_w_dkv', 'mla_kv_norm_g', 'mla_w_ukv',
           'mla_k_norm_g', 'mla_w_dq', 'mla_q_lora_norm_g', 'mla_w_uq', 'mla_q_norm_g', 'mla_w_out')
ARGS = ('x', 'c', 'positions') + WEIGHTS + ('loss_target',) + tuple('m_' + n for n in WEIGHTS) + tuple('v_' + n for n in WEIGHTS)


def _split_norm(v):
    return v[None, :DH], _pad_lanes(v[None, DH:], 0)


def _join_norm(gn, gr):
    return jnp.concatenate([gn[0], gr[0, :ROPE]])


def _step(x, tgt, pos, mods, kvmod, W, P):
    tabs, pm = _rope_tables(pos)
    m3 = lambda l, i: tuple(mods[l][3 * i + j][None] for j in range(3))
    ng = lambda l, i: P["norm_g"][l, i][None]
    gdn_p, mla_p = [], []
    for l in range(2):
        gdn_p.append(dict(w_in=jnp.pad(W["gdn_w_in"][l], ((0, 0), (0, GDN_IN - W["gdn_w_in"].shape[2]))),
                          conv_w8=jnp.pad(P["gdn_conv_w"][l], ((0, 4), (0, 0))),
                          a_log128=_pad_lanes(P["gdn_a_log"][l][None], NH), dt_bias128=_pad_lanes(P["gdn_dt_bias"][l][None], NH),
                          norm_g=P["gdn_norm_g"][l][None], w_out=W["gdn_w_out"][l]))
        q_gn, q_gr = _split_norm(P["mla_q_norm_g"][l])
        mla_p.append(dict(w_dq=W["mla_w_dq"][l], ql_g=P["mla_q_lora_norm_g"][l][None],
                          w_uq=jnp.pad(W["mla_w_uq"][l].reshape(QL, NH, QKH), ((0, 0), (0, 0), (0, HP - QKH))).reshape(QL, NH * HP),
                          q_gn=q_gn, q_gr=q_gr, pm=pm, w_out=W["mla_w_out"][l]))
    k_gn, k_gr = _split_norm(P["mla_k_norm_g"])
    kv_p = dict(kv_norm_g=P["kv_norm_g"][None], w_dkv=jnp.pad(W["mla_w_dkv"], ((0, 0), (0, QL - KVL - ROPE))),
                kv_lat_g=P["mla_kv_norm_g"][None],
                w_ukv=W["mla_w_ukv"].reshape(KVL, NH, 2, DH).transpose(0, 2, 1, 3).reshape(KVL, 2 * NH * DH),
                k_gn=k_gn, k_gr=k_gr, pm=pm)
    kvm = (kvmod[0][None], kvmod[1][None])

    res = {}
    for l in range(4):
        x, res[l, 0] = _ffn_fwd(f"l{l}a", x, m3(l, 0), ng(l, 0), W["ffn_w_in"], W["ffn_w_out"], 2 * l)
        if l < 2:
            x, res[l, 1] = _gdn_layer_fwd(f"l{l}g", x, m3(l, 1), ng(l, 1), gdn_p[l])
        else:
            x, res[l, 1] = _mla_layer_fwd(f"l{l}m", x, m3(l, 1), ng(l, 1), mla_p[l - 2], kp, kvf, tabs)
        x, res[l, 2] = _ffn_fwd(f"l{l}b", x, m3(l, 2), ng(l, 2), W["ffn_w_in"], W["ffn_w_out"], 2 * l + 1)
        if l == 1:
            kp, kvf, kres = _kv_fwd(x, kvm, kv_p, tabs)
    loss, dx = _loss_head(x, tgt)

    gw = {n: [None] * W[n].shape[0] for n in ("gdn_w_in", "gdn_w_out", "mla_w_dq", "mla_w_uq", "mla_w_out")}
    g_in4 = g_out4 = None
    gp = {n: [None] * 2 for n in ("gdn_conv_w", "gdn_a_log", "gdn_dt_bias", "gdn_norm_g", "mla_q_lora_norm_g", "mla_q_norm_g")}
    gnorm = [[None] * 3 for _ in range(4)]
    dmod = [[None] * NMOD for _ in range(4)]
    dkp = dv = None
    for l in (3, 2, 1, 0):
        if l == 1:
            dx, gk = _kv_bwd(dkp, dv, dx, kres, kvm, kvf, kv_p, tabs)
        for i in (2, 1, 0):
            if i != 1:
                dx, g_in4, g_out4, gd = _ffn_bwd(f"l{l}{'ab'[i // 2]}", dx, res[l, i], m3(l, i), ng(l, i), W["ffn_w_in"],
                                                 W["ffn_w_out"], 2 * l + i // 2, g_in4, g_out4)
            elif l < 2:
                dx, gd = _gdn_layer_bwd(f"l{l}g", dx, res[l, 1], m3(l, 1), ng(l, 1), gdn_p[l])
                gw["gdn_w_in"][l] = gd["w_in"][:, :W["gdn_w_in"].shape[2]]
                gw["gdn_w_out"][l] = gd["w_out"]
                gp["gdn_conv_w"][l] = gd["conv_w8"][:4]
                gp["gdn_a_log"][l] = gd["a_log128"][0, NH:2 * NH]
                gp["gdn_dt_bias"][l] = gd["dt_bias128"][0, NH:2 * NH]
                gp["gdn_norm_g"][l] = gd["norm_g"][0]
            else:
                dx, dkp_l, dv_l, gd = _mla_layer_bwd(f"l{l}m", dx, res[l, 1], m3(l, 1), ng(l, 1), mla_p[l - 2], kp, kvf, tabs)
                dkp = dkp_l if dkp is None else dkp + dkp_l
                dv = dv_l if dv is None else dv + dv_l
                gw["mla_w_dq"][l - 2], gw["mla_w_out"][l - 2] = gd["w_dq"], gd["w_out"]
                gw["mla_w_uq"][l - 2] = gd["w_uq"].reshape(QL, NH, HP)[:, :, :QKH].reshape(QL, NH * QKH)
                gp["mla_q_lora_norm_g"][l - 2] = gd["ql_g"][0]
                gp["mla_q_norm_g"][l - 2] = _join_norm(gd["q_gn"], gd["q_gr"])
            gnorm[l][i] = gd["g"][0]
            for j in range(3):
                dmod[l][3 * i + j] = gd["mod"][j][0]
    gwf = {n: jnp.stack(v) for n, v in gw.items()}
    gwf["ffn_w_in"], gwf["ffn_w_out"] = g_in4, g_out4
    gwf["mla_w_dkv"] = gk["w_dkv"][:, :KVL + ROPE]
    gwf["mla_w_ukv"] = gk["w_ukv"].reshape(KVL, 2, NH, DH).transpose(0, 2, 1, 3).reshape(KVL, 2 * NH * DH)
    gpf = {n: jnp.stack(v) for n, v in gp.items()}
    gpf["norm_g"] = jnp.stack([jnp.stack(r) for r in gnorm])
    gpf["kv_norm_g"] = gk["kv_norm_g"][0]
    gpf["mla_kv_norm_g"] = gk["kv_lat_g"][0]
    gpf["mla_k_norm_g"] = _join_norm(gk["k_gn"], gk["k_gr"])
    dmods = jnp.stack([jnp.stack(r) for r in dmod])
    dkvmod = jnp.stack([gk["mod"][0][0], gk["mod"][1][0]])
    return loss, dx, gwf, gpf, dmods, dkvmod


SMALL = ("norm_g", "gdn_conv_w", "gdn_a_log", "gdn_dt_bias", "gdn_norm_g", "kv_norm_g", "mla_kv_norm_g", "mla_k_norm_g",
         "mla_q_lora_norm_g", "mla_q_norm_g")


def kernel(x, c, positions, ada_w, ada_b, norm_g, ffn_w_in, ffn_w_out, gdn_w_in, gdn_conv_w, gdn_a_log, gdn_dt_bias,
           gdn_norm_g, gdn_w_out, kv_ada_w, kv_ada_b, kv_norm_g, mla_w_dkv, mla_kv_norm_g, mla_w_ukv, mla_k_norm_g,
           mla_w_dq, mla_q_lora_norm_g, mla_w_uq, mla_q_norm_g, mla_w_out, loss_target, m_ada_w, m_ada_b, m_norm_g,
           m_ffn_w_in, m_ffn_w_out, m_gdn_w_in, m_gdn_conv_w, m_gdn_a_log, m_gdn_dt_bias, m_gdn_norm_g, m_gdn_w_out,
           m_kv_ada_w, m_kv_ada_b, m_kv_norm_g, m_mla_w_dkv, m_mla_kv_norm_g, m_mla_w_ukv, m_mla_k_norm_g, m_mla_w_dq,
           m_mla_q_lora_norm_g, m_mla_w_uq, m_mla_q_norm_g, m_mla_w_out, v_ada_w, v_ada_b, v_norm_g, v_ffn_w_in,
           v_ffn_w_out, v_gdn_w_in, v_gdn_conv_w, v_gdn_a_log, v_gdn_dt_bias, v_gdn_norm_g, v_gdn_w_out, v_kv_ada_w,
           v_kv_ada_b, v_kv_norm_g, v_mla_w_dkv, v_mla_kv_norm_g, v_mla_w_ukv, v_mla_k_norm_g, v_mla_w_dq,
           v_mla_q_lora_norm_g, v_mla_w_uq, v_mla_q_norm_g, v_mla_w_out):
    a = dict(locals())
    mx, my, mc = _me()
    dev = 4 * mx + 2 * my + mc
    chip = 2 * mx + my
    x, tgt, pos = a["x"][0], a["loss_target"][0], a["positions"][0]
    take = lambda arr, i, axis=0: lax.dynamic_index_in_dim(arr, i, axis, keepdims=False)

    pre = _all_gather8("ag_pre", _pack_small([a["c"], a["gdn_conv_w"], a["norm_g"]]))
    c_all, conv_sh, norm_sh = _unpack_small(pre, [(D,), a["gdn_conv_w"].shape, a["norm_g"].shape])
    P = {n: a[n] for n in SMALL}
    P["gdn_conv_w"] = jnp.concatenate([conv_sh[2 * s] for s in range(4)], axis=2)
    P["norm_g"] = jnp.concatenate([norm_sh[2 * s] for s in range(4)], axis=2)
    c_act = _silu_rows("c_act", c_all)
    nada = a["ada_w"].shape[2]
    nkv = a["kv_ada_w"].shape[1]
    modp = [_matmul(f"mod{l}", c_act, a["ada_w"][l], precise=True) for l in range(4)]
    kvp = _matmul("modkv", c_act, a["kv_ada_w"], precise=True)
    mp = _all_gather8("ag_mod", _pack_small(modp + [kvp]))
    modp_all, kvp_all = _unpack_small(mp, [(4, 8, nada), (8, nkv)])
    mods = jnp.concatenate([take(modp_all[2 * s], dev, 1) for s in range(4)], axis=1) + a["ada_b"]
    mods = mods.reshape(4, NMOD, D)
    kvmod = (jnp.concatenate([take(kvp_all[2 * s], dev, 0) for s in range(4)]) + a["kv_ada_b"]).reshape(2, D)

    def gather(tag, w2):
        return _gather_weights("ag_" + tag, _place_shard("own_" + tag, w2, chip))

    def reduce(tag, g4):
        q = _add_half("rsp_" + tag, g4, _exchange_half("rs1_" + tag, g4), mc)
        r2 = _add_chips("rsc_" + tag, q, _scatter_chips("rs2_" + tag, q), chip, mc)
        return _exchange_full("rs3_" + tag, r2)

    halves = lambda t: t.reshape((2, -1) + t.shape[-1:])
    W = {n: gather(t, halves(a[n].astype(BF16))).reshape((4, 8) + a[n].shape[2:])
         for n, t in (("ffn_w_in", "wi"), ("ffn_w_out", "wo"))}
    wg = gather("wg", a["gdn_w_in"].astype(BF16))
    W["gdn_w_in"] = jnp.concatenate([wg[s] for s in range(4)], axis=2)
    wall = gather("wr", _pack_flat(jnp.concatenate([a[n].reshape(-1).astype(BF16) for n, _ in REST]))).reshape(4, -1)
    off = 0
    for n, ax in REST:
        sz = a[n].size
        W[n] = _shards_merge(wall[:, off:off + sz].reshape((4,) + a[n].shape), ax)
        off += sz

    loss, dx, gw, gp, dmods, dkvmod = _step(x, tgt, pos, mods, kvmod, W, P)
    loss = lax.psum(loss, ("x", "y", "c"))

    grads = {n: reduce(t, gw[n].reshape((4, 2, -1) + a[n].shape[-1:])).reshape(a[n].shape)
             for n, t in (("ffn_w_in", "wi"), ("ffn_w_out", "wo"))}
    ng = a["gdn_w_in"].shape[2]
    grads["gdn_w_in"] = reduce("wg", jnp.stack([gw["gdn_w_in"][:, :, s * ng:(s + 1) * ng] for s in range(4)]))
    gsh = reduce("wr", _pack_flat(jnp.concatenate([_shards_first(gw[n], ax).reshape(4, -1) for n, ax in REST], axis=1)))
    gsh = gsh.reshape(-1)
    off = 0
    for n, _ in REST:
        grads[n] = gsh[off:off + a[n].size].reshape(a[n].shape)
        off += a[n].size

    small = _all_gather8("ag_small", _pack_small([dmods, dkvmod] + [gp[n] for n in SMALL]))
    shapes = [(4, NMOD * D), (2 * D,)] + [gp[n].shape for n in SMALL]
    dmod_all, dkv_all = _unpack_small(small, shapes)[:2]
    tot = _unpack_small(_sum8("sum_small", small), shapes)
    grads["ada_b"], grads["kv_ada_b"] = tot[0], tot[1]
    for n, t in zip(SMALL, tot[2:]):
        grads[n] = t
    grads["norm_g"] = lax.dynamic_slice_in_dim(grads["norm_g"], chip * a["norm_g"].shape[2], a["norm_g"].shape[2], 2)
    grads["gdn_conv_w"] = lax.dynamic_slice_in_dim(grads["gdn_conv_w"], chip * a["gdn_conv_w"].shape[2],
                                                   a["gdn_conv_w"].shape[2], 2)
    ca = jnp.pad(c_act, ((0, LANE - 8), (0, 0)))
    dm = jnp.pad(lax.dynamic_slice_in_dim(dmod_all.reshape(8, 4, NMOD * D), chip * nada, nada, 2), ((0, LANE - 8), (0, 0), (0, 0)))
    grads["ada_w"] = jnp.stack([_matmul(f"gada{l}", ca, dm[:, l], "tn", precise=True) for l in range(4)])
    dk = jnp.pad(lax.dynamic_slice_in_dim(dkv_all, chip * nkv, nkv, 1), ((0, LANE - 8), (0, 0)))
    grads["kv_ada_w"] = _matmul("gadakv", ca, dk, "tn", precise=True)

    upd = [_adamw("adamw_" + n, a[n], grads[n], a["m_" + n], a["v_" + n]) for n in WEIGHTS]
    return (loss, dx[None], *[grads[n] for n in WEIGHTS], *[u[0] for u in upd], *[u[1] for u in upd], *[u[2] for u in upd])
```

```python
import functools

import jax
import jax.numpy as jnp
from jax import lax
from jax.experimental import pallas as pl
from jax.experimental.pallas import tpu as pltpu

F32 = jnp.float32
BF16 = jnp.bfloat16
HI = lax.Precision.HIGHEST
MESH = pl.DeviceIdType.MESH

D = 1024
NH = 8
DH = 128
FF = 2816
NMOD = 9
CHUNK = 64
ROPE = 64
QKH = 192
HP = 256
KVL = 256
QL = 384
GDN_IN = 4224
GATE_CB = 32
EPS = 1e-6
ROPE_BASE = 10000.0
LANE = 128
SUB = 8
VMEM_LIMIT = 56 * 1024 * 1024

ADAM_LR, ADAM_B1, ADAM_B2, ADAM_EPS, ADAM_WD, ADAM_STEP = 0.001, 0.9, 0.999, 1e-08, 0.01, 10


def _tile(n, prefs=(512, 384, 256, 128)):
    for p in prefs:
        if n % p == 0:
            return p
    return n


def _cparams(sem):
    return pltpu.CompilerParams(dimension_semantics=sem, vmem_limit_bytes=VMEM_LIMIT)


class Row:
    def __init__(self, arr, width=None, cb=0, splits=None, halo=None):
        self.arr = arr
        self.width = arr.shape[1] if width is None else width
        self.cb = cb
        self.splits = splits
        self.halo = halo


def _rowwise(name, fn, rows, bcs, outs, accs, tm):
    S = rows[0].arr.shape[0]
    n = S // tm
    nr, nb, no, na = len(rows), len(bcs), len(outs), len(accs)

    def body(*refs):
        rrefs, brefs = refs[:nr], refs[nr:nr + nb]
        orefs, arefs = refs[nr + nb:nr + nb + no], refs[nr + nb + no:]
        pieces = []
        for r, ref in zip(rows, rrefs):
            if r.splits is None:
                pieces.append(ref[...])
            else:
                off = 0
                for w in r.splits:
                    pieces.append(ref[:, off:off + w])
                    off += w
        out_pieces, acc_vals = fn(pieces, [b[...] for b in brefs])
        k = 0
        for (widths, dt), oref in zip(outs, orefs):
            off = 0
            for w in widths:
                oref[:, off:off + w] = out_pieces[k].astype(dt)
                k += 1
                off += w
        if na:
            @pl.when(pl.program_id(0) == 0)
            def _():
                for a in arefs:
                    a[...] = jnp.zeros(a.shape, F32)
            for a, v in zip(arefs, acc_vals):
                a[...] += v

    in_specs = []
    for r in rows:
        if r.halo is None:
            in_specs.append(pl.BlockSpec((tm, r.width), lambda i, cb=r.cb: (i, cb)))
        elif r.halo == "prev":
            in_specs.append(pl.BlockSpec((SUB, r.width), lambda i, cb=r.cb: (jnp.maximum(i * (tm // SUB) - 1, 0), cb)))
        else:
            in_specs.append(pl.BlockSpec((SUB, r.width), lambda i, cb=r.cb: (jnp.minimum((i + 1) * (tm // SUB), S // SUB - 1), cb)))
    in_specs += [pl.BlockSpec(b.shape, lambda i, nd=b.ndim: (0,) * nd) for b in bcs]
    out_specs = [pl.BlockSpec((tm, sum(w)), lambda i: (i, 0)) for w, _ in outs]
    out_specs += [pl.BlockSpec(s, lambda i: (0, 0)) for s in accs]
    out_shape = [jax.ShapeDtypeStruct((S, sum(w)), dt) for w, dt in outs]
    out_shape += [jax.ShapeDtypeStruct(s, F32) for s in accs]
    res = pl.pallas_call(body, name=name, grid=(n,), in_specs=in_specs, out_specs=out_specs, out_shape=out_shape,
                         compiler_params=_cparams(("arbitrary",)))(*[r.arr for r in rows], *bcs)
    return res


def _rw_fwd(name, f, rows, bcs, outs, tm):
    def fn(pieces, bvals):
        return list(f(*[p.astype(F32) for p in pieces], *[b.astype(F32) for b in bvals])), []
    return _rowwise(name, fn, rows, bcs, outs, [], tm)


def _npieces(rows):
    return sum(1 if r.splits is None else len(r.splits) for r in rows)


def _rw_bwd(name, f, rows, bcs, cts, drow, dbc, outs, tm, add=None):
    np_, nct = _npieces(rows), _npieces(cts)

    def fn(pieces, bvals):
        allv = [p.astype(F32) for p in pieces[:np_]] + [b.astype(F32) for b in bvals]
        ct = [p.astype(F32) for p in pieces[np_:np_ + nct]]
        didx = [i for i, m in enumerate(list(drow) + list(dbc)) if m]

        def g(*dv):
            full = list(allv)
            for i, v in zip(didx, dv):
                full[i] = v
            return tuple(f(*full))

        _, vjp = jax.vjp(g, *[allv[i] for i in didx])
        grads = vjp(tuple(ct))
        nrd = sum(bool(m) for m in drow)
        rg, bg = list(grads[:nrd]), list(grads[nrd:])
        if add is not None:
            rg[0] = rg[0] + pieces[np_ + nct].astype(F32)
        return rg, bg

    accs = [b.shape for b, m in zip(bcs, dbc) if m]
    return _rowwise(name, fn, list(rows) + list(cts) + ([add] if add is not None else []), bcs, outs, accs, tm)


def _sigmoid(x):
    return 1.0 / (1.0 + jnp.exp(-x))


def _silu(x):
    return x * _sigmoid(x)


def _softplus(x):
    return jnp.maximum(x, 0.0) + jnp.log(1.0 + jnp.exp(-jnp.abs(x)))


def f_mod(x, g, shift, scale):
    y = x * lax.rsqrt(jnp.mean(x * x, axis=-1, keepdims=True) + EPS)
    return (y * g * (1.0 + scale) + shift,)


def f_rms(x, g):
    return (x * lax.rsqrt(jnp.mean(x * x, axis=-1, keepdims=True) + EPS) * g,)


def f_act(gate, up):
    return (_silu(gate) * up,)


def make_f_res(coef):
    def f_res(y, gate):
        return (coef * gate * y,)
    return f_res


def f_gdnpre(*p):
    out = []
    for i, t in enumerate(p):
        t = _silu(t)
        if i < 2 * NH:
            t = t * lax.rsqrt(jnp.sum(t * t, axis=-1, keepdims=True) + EPS)
        out.append(t)
    return tuple(out)


def f_gates(gates, a_log, dt_bias):
    return _sigmoid(gates), -jnp.exp(a_log) * _softplus(gates + dt_bias)


def f_gdnpost(*a):
    o, z, g = a[:NH], a[NH:2 * NH], a[2 * NH]
    out = []
    for oh, zh in zip(o, z):
        y = oh * lax.rsqrt(jnp.mean(oh * oh, axis=-1, keepdims=True) + EPS) * g
        out.append(y * _silu(zh))
    return tuple(out)


def make_f_qk(shared_rope):
    def f(*a):
        if shared_rope:
            ns, rs = a[:NH], [a[NH]] * NH
            cosp, sins, gn, gr, pm = a[NH + 1:NH + 6]
        else:
            ns, rs = a[0:2 * NH:2], a[1:2 * NH:2]
            cosp, sins, gn, gr, pm = a[2 * NH:2 * NH + 5]
        out = []
        for n, r in zip(ns, rs):
            ss = jnp.sum(n * n, axis=-1, keepdims=True) + jnp.sum(r * r, axis=-1, keepdims=True)
            rstd = lax.rsqrt(ss * (1.0 / QKH) + EPS)
            yn = n * rstd * gn
            yr = r * rstd * gr
            sw = jnp.dot(yr, pm, precision=HI, preferred_element_type=F32)
            out += [yn, yr * cosp + sw * sins]
        return tuple(out)
    return f


def _matmul(name, a, b, mode="nn", out_dtype=F32, precise=False, lay=None, li=0, into=None, nmat=1):
    if lay == "b_cols":
        per = b.shape[3]
        rb, cb = b.shape[2], 4 * per
    elif lay == "b_rows":
        per = b.shape[2]
        rb, cb = 4 * per, b.shape[3]
    else:
        rb, cb = b.shape
    if mode == "nn":
        (M, K), N = a.shape, cb
    elif mode == "nt":
        (M, K), N = a.shape, rb
    else:
        (K, M), N = a.shape, cb
    tm = _tile(M, (1024, 512, 256, 128))
    tn = _tile(N, (1024, 512, 384, 256, 128))
    tk = _tile(K, (1408, 1024, 512, 384, 256, 128))
    if lay == "b_cols":
        tn, tk = (per, tk) if mode == "nn" else (tn, per)
    elif lay == "b_rows":
        tm, tn, tk = (tm, 512, K) if mode == "nn" else (min(tm, 512), N, tk)
    elif lay == "o_cols":
        per = N // 4
        tn = per
    elif lay == "o_rows":
        per = M // 4
        tm, tn = M, 512
    nk = K // tk
    dims = {"nn": (((1,), (0,)), ((), ())), "nt": (((1,), (1,)), ((), ())), "tn": (((0,), (0,)), ((), ()))}[mode]

    def body(a_ref, b_ref, *rest):
        o_ref, acc_ref = rest[-2:]
        k = pl.program_id(2)

        @pl.when(k == 0)
        def _():
            acc_ref[...] = jnp.zeros(acc_ref.shape, F32)

        bv = b_ref[...]
        if lay == "b_rows":
            bv = bv.reshape(4 * per, bv.shape[2])
        if precise:
            acc_ref[...] += lax.dot_general(a_ref[...].astype(F32), bv.astype(F32), dims, precision=HI,
                                            preferred_element_type=F32)
        else:
            acc_ref[...] += lax.dot_general(a_ref[...].astype(BF16), bv.astype(BF16), dims, preferred_element_type=F32)

        @pl.when(k == nk - 1)
        def _():
            if lay == "o_rows":
                for s in range(4):
                    o_ref[s] = acc_ref[s * per:(s + 1) * per, :].astype(o_ref.dtype)
            else:
                o_ref[...] = acc_ref[...].astype(o_ref.dtype)

    a_spec = pl.BlockSpec((tk, tm), lambda i, j, k: (k, i)) if mode == "tn" else pl.BlockSpec((tm, tk), lambda i, j, k: (i, k))
    if lay == "b_cols":
        b_spec = (pl.BlockSpec((None, None, tk, per), lambda i, j, k: (j, li, k, 0)) if mode == "nn" else
                  pl.BlockSpec((None, None, tn, per), lambda i, j, k: (k, li, j, 0)))
    elif lay == "b_rows":
        b_spec = (pl.BlockSpec((4, None, per, tn), lambda i, j, k: (0, li, 0, j)) if mode == "nn" else
                  pl.BlockSpec((4, None, per, tk), lambda i, j, k: (0, li, 0, k)))
    elif mode == "nt":
        b_spec = pl.BlockSpec((tn, tk), lambda i, j, k: (j, k))
    else:
        b_spec = pl.BlockSpec((tk, tn), lambda i, j, k: (k, j))
    if lay == "o_cols":
        o_spec = pl.BlockSpec((None, None, tm, per), lambda i, j, k: (j, li, i, 0))
        o_shape = jax.ShapeDtypeStruct((4, nmat, M, per), out_dtype)
    elif lay == "o_rows":
        o_spec = pl.BlockSpec((4, None, per, tn), lambda i, j, k: (0, li, 0, j))
        o_shape = jax.ShapeDtypeStruct((4, nmat, per, N), out_dtype)
    else:
        o_spec = pl.BlockSpec((tm, tn), lambda i, j, k: (i, j))
        o_shape = jax.ShapeDtypeStruct((M, N), out_dtype)
    in_specs, args, alias = [a_spec, b_spec], [a, b], {}
    if into is not None:
        in_specs.append(pl.BlockSpec(memory_space=pl.ANY))
        args.append(into)
        alias = {2: 0}
    return pl.pallas_call(body, name=name, grid=(M // tm, N // tn, nk), in_specs=in_specs, out_specs=o_spec,
                          out_shape=o_shape, scratch_shapes=[pltpu.VMEM((tm, tn), F32)], input_output_aliases=alias,
                          compiler_params=_cparams(("parallel", "parallel", "arbitrary")))(*args)


def _shift_down(t, p, d):
    if d == 0:
        return t
    tr = pltpu.roll(t, d, 0)
    pr = pltpu.roll(p, d, 0)
    r8 = lax.broadcasted_iota(jnp.int32, p.shape, 0)
    first = jnp.where(r8 < d, pr, tr[:SUB])
    return jnp.concatenate([first, tr[SUB:]], axis=0)


def _shift_up(t, nx, d):
    if d == 0:
        return t
    tm = t.shape[0]
    tr = pltpu.roll(t, tm - d, 0)
    nr = pltpu.roll(nx, SUB - d, 0)
    r8 = lax.broadcasted_iota(jnp.int32, nx.shape, 0)
    last = jnp.where(r8 >= SUB - d, nr, tr[tm - SUB:])
    return jnp.concatenate([tr[:tm - SUB], last], axis=0)


def _conv_fwd(name, proj, w8, C, tm):
    def fn(pieces, bvals):
        t, p = pieces[0].astype(F32), pieces[1].astype(F32)
        w = bvals[0]
        p = jnp.where(pl.program_id(0) == 0, 0.0, p)
        out = w[3:4] * t
        for d in (1, 2, 3):
            out = out + w[3 - d:4 - d] * _shift_down(t, p, d)
        return [out], []
    return _rowwise(name, fn, [Row(proj, C), Row(proj, C, halo="prev")], [w8], [((C,), F32)], [], tm)[0]


def _conv_bwd(name, proj, dout, w8, C, tm, out_dtype):
    n = proj.shape[0] // tm

    def fn(pieces, bvals):
        t, p, g, gn = [v.astype(F32) for v in pieces]
        w = bvals[0]
        i = pl.program_id(0)
        p = jnp.where(i == 0, 0.0, p)
        gn = jnp.where(i == n - 1, 0.0, gn)
        dx = w[3:4] * g
        dws = [jnp.sum(g * t, axis=0, keepdims=True)]
        for d in (1, 2, 3):
            dx = dx + w[3 - d:4 - d] * _shift_up(g, gn, d)
            dws.append(jnp.sum(g * _shift_down(t, p, d), axis=0, keepdims=True))
        dw = jnp.concatenate([dws[3], dws[2], dws[1], dws[0], jnp.zeros((4, g.shape[1]), F32)], axis=0)
        return [dx], [dw]
    return _rowwise(name, fn, [Row(proj, C), Row(proj, C, halo="prev"), Row(dout), Row(dout, halo="next")], [w8],
                    [((C,), out_dtype)], [(SUB, C)], tm)


def _bdot(a, b, ca, cb):
    return lax.dot_general(a.astype(BF16), b.astype(BF16), (((ca,), (cb,)), ((0,), (0,))), preferred_element_type=F32)


def _bdot3(a, b, ca, cb):
    dims = (((ca,), (cb,)), ((0,), (0,)))
    ah, bh = a.astype(BF16), b.astype(BF16)
    al, bl = (a - ah.astype(F32)).astype(BF16), (b - bh.astype(F32)).astype(BF16)
    d = lambda x, y: lax.dot_general(x, y, dims, preferred_element_type=F32)
    return d(ah, bh) + (d(ah, bl) + d(al, bh))


@jax.custom_vjp
def _bmm3(a, b):
    return _bdot3(a, b, 2, 1)


_bmm3.defvjp(lambda a, b: (_bdot3(a, b, 2, 1), (a, b)),
             lambda res, g: (_bdot3(g, res[1], 2, 2), _bdot3(res[0], g, 1, 1)))


def _neumann(nl):
    C = nl.shape[1]
    eye = (lax.broadcasted_iota(jnp.int32, (1, C, C), 1) == lax.broadcasted_iota(jnp.int32, (1, C, C), 2)).astype(F32)
    T = eye + nl
    pw = nl
    for _ in range(C.bit_length() - 2):
        pw = _bdot3(pw, pw, 2, 1)
        T = T + _bdot3(T, pw, 2, 1)
    return T


_unit_lower_inv = jax.custom_vjp(_neumann)


def _unit_lower_inv_fwd(nl):
    T = _neumann(nl)
    return T, T


def _unit_lower_inv_bwd(T, g):
    return (_bdot3(_bdot3(T, g, 1, 1), T, 2, 2),)


_unit_lower_inv.defvjp(_unit_lower_inv_fwd, _unit_lower_inv_bwd)


def _gdn_chunk(q, k, v, gcol, grow, bcol, S):
    C = CHUNK
    ii = lax.broadcasted_iota(jnp.int32, (1, C, C), 1)
    jj = lax.broadcasted_iota(jnp.int32, (1, C, C), 2)
    incl, strict = ii >= jj, ii > jj
    gc_col = jnp.sum(jnp.where(incl, 1.0, 0.0) * grow, axis=2, keepdims=True)
    gc_row = jnp.sum(jnp.where(jj >= ii, 1.0, 0.0) * gcol, axis=1, keepdims=True)
    decay = jnp.where(incl, jnp.exp(jnp.where(incl, gc_col - gc_row, 0.0)), 0.0)
    qs = q * (DH ** -0.5)
    kb = k * bcol
    T = _unit_lower_inv(-jnp.where(strict, _bdot(kb, k, 2, 2) * decay, 0.0))
    egc = jnp.exp(gc_col)
    u = _bmm3(T, v * bcol)
    w = _bmm3(T, kb * egc)
    att = jnp.where(incl, _bdot(qs, k, 2, 2) * decay, 0.0)
    v_new = u - _bdot(w, S, 2, 1)
    o = _bdot(qs * egc, S, 2, 1) + _bdot(att, v_new, 2, 1)
    g_last = jnp.sum(grow, axis=2, keepdims=True)
    k_dec = k * jnp.exp(g_last - gc_col)
    S_out = S * jnp.exp(g_last) + _bdot(k_dec, v_new, 1, 1)
    return o, S_out


def _heads(ref, w):
    return jnp.stack([ref[:, h * w:(h + 1) * w] for h in range(NH)])


def _gdn_specs(NC, rev):
    ix = (lambda i: NC - 1 - i) if rev else (lambda i: i)
    wide = pl.BlockSpec((CHUNK, D), lambda i: (ix(i), 0))
    col = pl.BlockSpec((CHUNK, NH), lambda i: (ix(i), 0))
    row = pl.BlockSpec((1, NH, CHUNK), lambda i: (ix(i), 0, 0))
    st = pl.BlockSpec((1, NH, DH, DH), lambda i: (ix(i), 0, 0, 0))
    return wide, col, row, st


def _gdn_fwd(name, q, k, v, gcol, grow, bcol):
    S = q.shape[0]
    NC = S // CHUNK

    def body(q_ref, k_ref, v_ref, gc_ref, gr_ref, b_ref, o_ref, ss_ref, st):
        @pl.when(pl.program_id(0) == 0)
        def _():
            st[...] = jnp.zeros(st.shape, F32)
        s_in = st[...]
        ss_ref[0] = s_in
        grow = jnp.stack([gr_ref[0, h:h + 1, :] for h in range(NH)])
        o, s_out = _gdn_chunk(_heads(q_ref, DH), _heads(k_ref, DH), _heads(v_ref, DH), _heads(gc_ref, 1), grow,
                              _heads(b_ref, 1), s_in)
        for h in range(NH):
            o_ref[:, h * DH:(h + 1) * DH] = o[h]
        st[...] = s_out

    wide, col, row, stsp = _gdn_specs(NC, False)
    return pl.pallas_call(body, name=name, grid=(NC,), in_specs=[wide, wide, wide, col, row, col],
                          out_specs=[wide, stsp],
                          out_shape=[jax.ShapeDtypeStruct((S, D), F32), jax.ShapeDtypeStruct((NC, NH, DH, DH), F32)],
                          scratch_shapes=[pltpu.VMEM((NH, DH, DH), F32)],
                          compiler_params=_cparams(("arbitrary",)))(q, k, v, gcol, grow, bcol)


def _gdn_bwd(name, q, k, v, gcol, grow, bcol, ssave, do):
    S = q.shape[0]
    NC = S // CHUNK

    def body(q_ref, k_ref, v_ref, gc_ref, gr_ref, b_ref, ss_ref, do_ref, dq_ref, dk_ref, dv_ref, dgc_ref, dgr_ref, db_ref, dst):
        @pl.when(pl.program_id(0) == 0)
        def _():
            dst[...] = jnp.zeros(dst.shape, F32)
        grow = jnp.stack([gr_ref[0, h:h + 1, :] for h in range(NH)])
        prim = (_heads(q_ref, DH), _heads(k_ref, DH), _heads(v_ref, DH), _heads(gc_ref, 1), grow, _heads(b_ref, 1), ss_ref[0])
        _, vjp = jax.vjp(_gdn_chunk, *prim)
        dq, dk, dv, dgc, dgr, db, ds = vjp((_heads(do_ref, DH), dst[...]))
        for h in range(NH):
            hs = slice(h * DH, (h + 1) * DH)
            dq_ref[:, hs] = dq[h]
            dk_ref[:, hs] = dk[h]
            dv_ref[:, hs] = dv[h]
            dgc_ref[:, h:h + 1] = dgc[h]
            dgr_ref[0, h:h + 1, :] = dgr[h]
            db_ref[:, h:h + 1] = db[h]
        dst[...] = ds

    wide, col, row, stsp = _gdn_specs(NC, True)
    return pl.pallas_call(body, name=name, grid=(NC,), in_specs=[wide, wide, wide, col, row, col, stsp, wide],
                          out_specs=[wide, wide, wide, col, row, col],
                          out_shape=[jax.ShapeDtypeStruct((S, D), F32)] * 3 + [jax.ShapeDtypeStruct((S, NH), F32),
                                                                                 jax.ShapeDtypeStruct((NC, NH, CHUNK), F32),
                                                                                 jax.ShapeDtypeStruct((S, NH), F32)],
                          scratch_shapes=[pltpu.VMEM((NH, DH, DH), F32)],
                          compiler_params=_cparams(("arbitrary",)))(q, k, v, gcol, grow, bcol, ssave, do)


TQ = 512
SM_SCALE = QKH ** -0.5
NEG = -1e30


def _diag_mask(transposed):
    r = lax.broadcasted_iota(jnp.int32, (TQ, TQ), 0) // CHUNK
    c = lax.broadcasted_iota(jnp.int32, (TQ, TQ), 1) // CHUNK
    return (r <= c) if transposed else (c <= r)


def _dot_nt(a, b):
    return lax.dot_general(a, b, (((1,), (1,)), ((), ())), preferred_element_type=F32)


def _flash_fwd(name, qp, kp, kv):
    S = qp.shape[0]
    nq = S // TQ

    def body(q_ref, k_ref, v_ref, o_ref, lse_ref):
        qi = pl.program_id(1)
        q = q_ref[...]

        def step(j, carry, masked):
            m, l, acc = carry
            rows = pl.ds(pl.multiple_of(j * TQ, TQ), TQ)
            s = _dot_nt(q, k_ref[rows, :]) * SM_SCALE
            if masked:
                s = jnp.where(_diag_mask(False), s, NEG)
            m_new = jnp.maximum(m, jnp.max(s, axis=-1, keepdims=True))
            p = jnp.exp(s - m_new)
            alpha = jnp.exp(m - m_new)
            l = alpha * l + jnp.sum(p, axis=-1, keepdims=True)
            acc = alpha * acc + jnp.dot(p.astype(BF16), v_ref[rows, :].astype(BF16), preferred_element_type=F32)
            return m_new, l, acc

        carry = (jnp.full((TQ, 1), NEG, F32), jnp.zeros((TQ, 1), F32), jnp.zeros((TQ, DH), F32))
        carry = lax.fori_loop(0, qi, lambda j, c: step(j, c, False), carry)
        m, l, acc = step(qi, carry, True)
        o_ref[...] = acc / l
        lse_ref[0] = m + jnp.log(l)

    return pl.pallas_call(
        body, name=name, grid=(NH, nq),
        in_specs=[pl.BlockSpec((TQ, HP), lambda h, i: (i, h)), pl.BlockSpec((S, HP), lambda h, i: (0, h)),
                  pl.BlockSpec((S, DH), lambda h, i: (0, NH + h))],
        out_specs=[pl.BlockSpec((TQ, DH), lambda h, i: (i, h)), pl.BlockSpec((1, TQ, 1), lambda h, i: (h, i, 0))],
        out_shape=[jax.ShapeDtypeStruct((S, NH * DH), F32), jax.ShapeDtypeStruct((NH, S, 1), F32)],
        compiler_params=_cparams(("parallel", "arbitrary")))(qp, kp, kv)


def _flash_bwd_dq(name, qp, kp, kv, o, do, lse):
    S = qp.shape[0]
    nq = S // TQ

    def body(q_ref, k_ref, v_ref, o_ref, do_ref, lse_ref, dq_ref, dl_ref):
        qi = pl.program_id(1)
        q = q_ref[...]
        do = do_ref[...]
        delta = jnp.sum(o_ref[...] * do, axis=-1, keepdims=True)
        dl_ref[0] = delta
        dob = do.astype(BF16)
        lse = lse_ref[0]

        def step(j, dq, masked):
            rows = pl.ds(pl.multiple_of(j * TQ, TQ), TQ)
            k = k_ref[rows, :]
            s = _dot_nt(q, k) * SM_SCALE
            if masked:
                s = jnp.where(_diag_mask(False), s, NEG)
            p = jnp.exp(s - lse)
            dp = _dot_nt(dob, v_ref[rows, :].astype(BF16))
            ds = p * (dp - delta) * SM_SCALE
            return dq + jnp.dot(ds.astype(BF16), k, preferred_element_type=F32)

        dq = lax.fori_loop(0, qi, lambda j, c: step(j, c, False), jnp.zeros((TQ, HP), F32))
        dq_ref[...] = step(qi, dq, True)

    return pl.pallas_call(
        body, name=name, grid=(NH, nq),
        in_specs=[pl.BlockSpec((TQ, HP), lambda h, i: (i, h)), pl.BlockSpec((S, HP), lambda h, i: (0, h)),
                  pl.BlockSpec((S, DH), lambda h, i: (0, NH + h)), pl.BlockSpec((TQ, DH), lambda h, i: (i, h)),
                  pl.BlockSpec((TQ, DH), lambda h, i: (i, h)), pl.BlockSpec((1, TQ, 1), lambda h, i: (h, i, 0))],
        out_specs=[pl.BlockSpec((TQ, HP), lambda h, i: (i, h)), pl.BlockSpec((1, TQ, 1), lambda h, i: (h, i, 0))],
        out_shape=[jax.ShapeDtypeStruct((S, NH * HP), F32), jax.ShapeDtypeStruct((NH, S, 1), F32)],
        compiler_params=_cparams(("parallel", "arbitrary")))(qp, kp, kv, o, do, lse)


def _flash_bwd_dkv(name, qp, kp, kv, do, lse_row, delta_row):
    S = qp.shape[0]
    nq = S // TQ

    def body(q_ref, k_ref, v_ref, do_ref, lse_ref, dl_ref, dk_ref, dv_ref):
        kj = pl.program_id(1)
        k = k_ref[...]
        vb = v_ref[...].astype(BF16)

        def step(i, carry, masked):
            dk, dv = carry
            rows = pl.ds(pl.multiple_of(i * TQ, TQ), TQ)
            q = q_ref[rows, :]
            dob = do_ref[rows, :].astype(BF16)
            st = _dot_nt(k, q) * SM_SCALE
            pt = jnp.exp(st - lse_ref[0, :, rows])
            if masked:
                pt = jnp.where(_diag_mask(True), pt, 0.0)
            dv = dv + jnp.dot(pt.astype(BF16), dob, preferred_element_type=F32)
            dpt = _dot_nt(vb, dob)
            dst = pt * (dpt - dl_ref[0, :, rows]) * SM_SCALE
            dk = dk + jnp.dot(dst.astype(BF16), q, preferred_element_type=F32)
            return dk, dv

        carry = step(kj, (jnp.zeros((TQ, HP), F32), jnp.zeros((TQ, DH), F32)), True)
        dk, dv = lax.fori_loop(kj + 1, nq, lambda i, c: step(i, c, False), carry)
        dk_ref[...] = dk
        dv_ref[...] = dv

    return pl.pallas_call(
        body, name=name, grid=(NH, nq),
        in_specs=[pl.BlockSpec((S, HP), lambda h, j: (0, h)), pl.BlockSpec((TQ, HP), lambda h, j: (j, h)),
                  pl.BlockSpec((TQ, DH), lambda h, j: (j, NH + h)), pl.BlockSpec((S, DH), lambda h, j: (0, h)),
                  pl.BlockSpec((1, 1, S), lambda h, j: (h, 0, 0)), pl.BlockSpec((1, 1, S), lambda h, j: (h, 0, 0))],
        out_specs=[pl.BlockSpec((TQ, HP), lambda h, j: (j, h)), pl.BlockSpec((TQ, DH), lambda h, j: (j, h))],
        out_shape=[jax.ShapeDtypeStruct((S, NH * HP), F32), jax.ShapeDtypeStruct((S, NH * DH), F32)],
        compiler_params=_cparams(("parallel", "arbitrary")))(qp, kp, kv, do, lse_row, delta_row)


def _tm(S, width):
    t = 512 if width <= 1024 else (256 if width <= 3072 else 128)
    return min(t, S)


def _mod_fwd(tag, x, g, shift, scale):
    S = x.shape[0]
    return _rw_fwd(tag + "_mod", f_mod, [Row(x)], [g, shift, scale], [((D,), BF16)], _tm(S, D))[0]


def _mod_bwd(tag, x, g, shift, scale, dh, dx_direct):
    S = x.shape[0]
    r = _rw_bwd(tag + "_mod_b", f_mod, [Row(x)], [g, shift, scale], [Row(dh)], [True], [True] * 3, [((D,), F32)],
                _tm(S, D), add=Row(dx_direct))
    return r[0], r[1:]


def _res_fwd(tag, x, y, gate, coef):
    S = x.shape[0]

    def fn(pieces, bvals):
        return [pieces[0] + coef * bvals[0] * pieces[1]], []
    return _rowwise(tag + "_res", fn, [Row(x), Row(y)], [gate], [((D,), F32)], [], _tm(S, D))[0]


def _res_bwd(tag, y, gate, dxn, coef):
    S = y.shape[0]
    r = _rw_bwd(tag + "_res_b", make_f_res(coef), [Row(y)], [gate], [Row(dxn)], [True], [True], [((D,), BF16)], _tm(S, D))
    return r[0], r[1]


def _ffn_fwd(tag, x, mod3, g, w_in4, w_out4, li):
    shift, scale, gate = mod3
    S = x.shape[0]
    h = _mod_fwd(tag, x, g, shift, scale)
    gu = _matmul(tag + "_in", h, w_in4, lay="b_cols", li=li)
    a = _rw_fwd(tag + "_act", f_act, [Row(gu, splits=[FF, FF])], [], [((FF,), BF16)], _tm(S, 2 * FF))[0]
    y = _matmul(tag + "_out", a, w_out4, lay="b_rows", li=li)
    xn = _res_fwd(tag, x, y, gate, 0.5)
    return xn, (x, h, gu, a, y)


def _ffn_bwd(tag, dxn, res, mod3, g, w_in4, w_out4, li, g_in4, g_out4):
    shift, scale, gate = mod3
    x, h, gu, a, y = res
    S = x.shape[0]
    nmat = w_in4.shape[1]
    dy, dgate = _res_bwd(tag, y, gate, dxn, 0.5)
    da = _matmul(tag + "_out_bi", dy, w_out4, "nt", lay="b_rows", li=li)
    g_out4 = _matmul(tag + "_out_bw", a, dy, "tn", lay="o_rows", li=li, into=g_out4, nmat=nmat)
    dgu = _rw_bwd(tag + "_act_b", f_act, [Row(gu, splits=[FF, FF])], [], [Row(da)], [True, True], [],
                  [((FF, FF), BF16)], _tm(S, 2 * FF))[0]
    dh = _matmul(tag + "_in_bi", dgu, w_in4, "nt", lay="b_cols", li=li)
    g_in4 = _matmul(tag + "_in_bw", h, dgu, "tn", lay="o_cols", li=li, into=g_in4, nmat=nmat)
    dx, (dg, dshift, dscale) = _mod_bwd(tag, x, g, shift, scale, dh, dxn)
    return dx, g_in4, g_out4, dict(g=dg, mod=(dshift, dscale, dgate))


def _pad_lanes(a, lo, width=LANE):
    return jnp.pad(a, ((0, 0), (lo, width - lo - a.shape[1])))


def _gdn_layer_fwd(tag, x, mod3, g, p):
    shift, scale, gate = mod3
    S = x.shape[0]
    NC = S // CHUNK
    h = _mod_fwd(tag, x, g, shift, scale)
    proj = _matmul(tag + "_in", h, p["w_in"])
    qc = _conv_fwd(tag + "_conv", proj, p["conv_w8"], 3 * D, _tm(S, 3 * D))
    q, k, v = _rw_fwd(tag + "_pre", f_gdnpre, [Row(qc, splits=[DH] * (3 * NH))], [],
                      [((DH,) * NH, F32)] * 3, _tm(S, 3 * D))
    betaf, gf = _rw_fwd(tag + "_gates", f_gates, [Row(proj, LANE, cb=GATE_CB)], [p["a_log128"], p["dt_bias128"]],
                        [((LANE,), F32)] * 2, _tm(S, LANE))
    bcol, gcol = betaf[:, :NH], gf[:, NH:2 * NH]
    grow = gcol.reshape(NC, CHUNK, NH).transpose(0, 2, 1)
    o, ssave = _gdn_fwd(tag + "_core", q, k, v, gcol, grow, bcol)
    on = _rw_fwd(tag + "_post", f_gdnpost, [Row(o, splits=[DH] * NH), Row(proj, D, cb=3, splits=[DH] * NH)],
                 [p["norm_g"]], [((DH,) * NH, BF16)], _tm(S, 2 * D))[0]
    y = _matmul(tag + "_out", on, p["w_out"])
    xn = _res_fwd(tag, x, y, gate, 1.0)
    return xn, (x, h, proj, qc, q, k, v, gcol, grow, bcol, ssave, o, on, y)


def _gdn_layer_bwd(tag, dxn, res, mod3, g, p):
    shift, scale, gate = mod3
    x, h, proj, qc, q, k, v, gcol, grow, bcol, ssave, o, on, y = res
    S = x.shape[0]
    dy, dgate = _res_bwd(tag, y, gate, dxn, 1.0)
    don = _matmul(tag + "_out_bi", dy, p["w_out"], "nt")
    dw_out = _matmul(tag + "_out_bw", on, dy, "tn")
    do, dz, dnorm = _rw_bwd(tag + "_post_b", f_gdnpost, [Row(o, splits=[DH] * NH), Row(proj, D, cb=3, splits=[DH] * NH)],
                            [p["norm_g"]], [Row(don, splits=[DH] * NH)], [True] * (2 * NH), [True],
                            [((DH,) * NH, F32), ((DH,) * NH, BF16)], _tm(S, 2 * D))
    dq, dk, dv, dgc, dgr, db = _gdn_bwd(tag + "_core_b", q, k, v, gcol, grow, bcol, ssave, do)
    dgcol = dgc + dgr.transpose(0, 2, 1).reshape(S, NH)
    dgates, da_log, ddt = _rw_bwd(tag + "_gates_b", f_gates, [Row(proj, LANE, cb=GATE_CB)], [p["a_log128"], p["dt_bias128"]],
                                  [Row(_pad_lanes(db, 0)), Row(_pad_lanes(dgcol, NH))], [True], [True, True],
                                  [((LANE,), BF16)], _tm(S, LANE))
    dqc = _rw_bwd(tag + "_pre_b", f_gdnpre, [Row(qc, splits=[DH] * (3 * NH))], [],
                  [Row(dq, splits=[DH] * NH), Row(dk, splits=[DH] * NH), Row(dv, splits=[DH] * NH)],
                  [True] * (3 * NH), [], [((DH,) * (3 * NH), F32)], _tm(S, 3 * D))[0]
    dqkv, dconv = _conv_bwd(tag + "_conv_b", proj, dqc, p["conv_w8"], 3 * D, _tm(S, 3 * D), BF16)
    dproj = jnp.concatenate([dqkv, dz, dgates], axis=1)
    dh = _matmul(tag + "_in_bi", dproj, p["w_in"], "nt")
    dw_in = _matmul(tag + "_in_bw", h, dproj, "tn")
    dx, (dg, dshift, dscale) = _mod_bwd(tag, x, g, shift, scale, dh, dxn)
    return dx, dict(w_in=dw_in, conv_w8=dconv, a_log128=da_log, dt_bias128=ddt, norm_g=dnorm,
                    w_out=dw_out, g=dg, mod=(dshift, dscale, dgate))


def _qk_rows(src, shared_rope, ckv=None):
    if shared_rope:
        return [Row(src, D, cb=0, splits=[DH] * NH), Row(ckv, LANE, cb=2)]
    return [Row(src, splits=[DH] * (2 * NH))]


def _kv_fwd(x, kvmod, p, tabs):
    shift, scale = kvmod
    S = x.shape[0]
    h = _mod_fwd("kv", x, p["kv_norm_g"], shift, scale)
    ckv = _matmul("kv_dkv", h, p["w_dkv"])
    lat = _rw_fwd("kv_lat", f_rms, [Row(ckv, KVL)], [p["kv_lat_g"]], [((KVL,), BF16)], _tm(S, KVL))[0]
    kvf = _matmul("kv_ukv", lat, p["w_ukv"])
    kp = _rw_fwd("kv_k", make_f_qk(True), _qk_rows(kvf, True, ckv) + [Row(tabs[0]), Row(tabs[1])],
                 [p["k_gn"], p["k_gr"], p["pm"]], [((DH,) * (2 * NH), BF16)], _tm(S, 2 * D))[0]
    return kp, kvf, (x, h, ckv, lat)


def _kv_bwd(dkp, dv, dx_direct, res, kvmod, kvf, p, tabs):
    shift, scale = kvmod
    x, h, ckv, lat = res
    S = x.shape[0]
    dkn, dkr, dgn, dgr = _rw_bwd("kv_k_b", make_f_qk(True), _qk_rows(kvf, True, ckv) + [Row(tabs[0]), Row(tabs[1])],
                                 [p["k_gn"], p["k_gr"], p["pm"]], [Row(dkp, splits=[DH] * (2 * NH))],
                                 [True] * (NH + 1) + [False, False], [True, True, False],
                                 [((DH,) * NH, BF16), ((LANE,), BF16)], _tm(S, 2 * D))
    dkvf = jnp.concatenate([dkn, dv.astype(BF16)], axis=1)
    dlat = _matmul("kv_ukv_bi", dkvf, p["w_ukv"], "nt")
    dw_ukv = _matmul("kv_ukv_bw", lat, dkvf, "tn")
    dcl, dlg = _rw_bwd("kv_lat_b", f_rms, [Row(ckv, KVL)], [p["kv_lat_g"]], [Row(dlat)], [True], [True],
                       [((KVL,), BF16)], _tm(S, KVL))
    dckv = jnp.concatenate([dcl, dkr], axis=1)
    dh = _matmul("kv_dkv_bi", dckv, p["w_dkv"], "nt")
    dw_dkv = _matmul("kv_dkv_bw", h, dckv, "tn")
    dx, (dg, dshift, dscale) = _mod_bwd("kv", x, p["kv_norm_g"], shift, scale, dh, dx_direct)
    return dx, dict(w_dkv=dw_dkv, w_ukv=dw_ukv, kv_lat_g=dlg, k_gn=dgn, k_gr=dgr, kv_norm_g=dg, mod=(dshift, dscale))


def _mla_layer_fwd(tag, x, mod3, g, p, kp, kvf, tabs):
    shift, scale, gate = mod3
    S = x.shape[0]
    h = _mod_fwd(tag, x, g, shift, scale)
    ql = _matmul(tag + "_dq", h, p["w_dq"])
    qln = _rw_fwd(tag + "_qln", f_rms, [Row(ql)], [p["ql_g"]], [((QL,), BF16)], _tm(S, QL))[0]
    qu = _matmul(tag + "_uq", qln, p["w_uq"])
    qp = _rw_fwd(tag + "_q", make_f_qk(False), _qk_rows(qu, False) + [Row(tabs[0]), Row(tabs[1])],
                 [p["q_gn"], p["q_gr"], p["pm"]], [((DH,) * (2 * NH), BF16)], _tm(S, 2 * D))[0]
    o, lse = _flash_fwd(tag + "_att", qp, kp, kvf)
    y = _matmul(tag + "_out", o, p["w_out"])
    xn = _res_fwd(tag, x, y, gate, 1.0)
    return xn, (x, h, ql, qln, qu, qp, o, lse, y)


def _mla_layer_bwd(tag, dxn, res, mod3, g, p, kp, kvf, tabs):
    shift, scale, gate = mod3
    x, h, ql, qln, qu, qp, o, lse, y = res
    S = x.shape[0]
    dy, dgate = _res_bwd(tag, y, gate, dxn, 1.0)
    do = _matmul(tag + "_out_bi", dy, p["w_out"], "nt")
    dw_out = _matmul(tag + "_out_bw", o, dy, "tn")
    dqp, delta = _flash_bwd_dq(tag + "_att_bq", qp, kp, kvf, o, do, lse)
    dkp, dv = _flash_bwd_dkv(tag + "_att_bkv", qp, kp, kvf, do, lse.reshape(NH, 1, S), delta.reshape(NH, 1, S))
    dqu, dgn, dgr = _rw_bwd(tag + "_q_b", make_f_qk(False), _qk_rows(qu, False) + [Row(tabs[0]), Row(tabs[1])],
                            [p["q_gn"], p["q_gr"], p["pm"]], [Row(dqp, splits=[DH] * (2 * NH))],
                            [True] * (2 * NH) + [False, False], [True, True, False],
                            [((DH,) * (2 * NH), BF16)], _tm(S, 2 * D))
    dqln = _matmul(tag + "_uq_bi", dqu, p["w_uq"], "nt")
    dw_uq = _matmul(tag + "_uq_bw", qln, dqu, "tn")
    dql, dqlg = _rw_bwd(tag + "_qln_b", f_rms, [Row(ql)], [p["ql_g"]], [Row(dqln)], [True], [True], [((QL,), BF16)],
                        _tm(S, QL))
    dh = _matmul(tag + "_dq_bi", dql, p["w_dq"], "nt")
    dw_dq = _matmul(tag + "_dq_bw", h, dql, "tn")
    dx, (dg, dshift, dscale) = _mod_bwd(tag, x, g, shift, scale, dh, dxn)
    return dx, dkp, dv, dict(w_dq=dw_dq, w_uq=dw_uq, w_out=dw_out, ql_g=dqlg, q_gn=dgn, q_gr=dgr, g=dg,
                             mod=(dshift, dscale, dgate))


def _loss_head(y, tgt):
    S = y.shape[0]

    def fn(pieces, bvals):
        e = pieces[0] - pieces[1]
        part = jnp.sum(e * e) * (0.5 / D)
        return [e * (1.0 / D)], [jnp.full((1, LANE), part, F32)]
    dy, part = _rowwise("loss", fn, [Row(y), Row(tgt)], [], [((D,), F32)], [(1, LANE)], _tm(S, D))
    return part[0, 0], dy


def _rope_tables(positions):
    S = positions.shape[0]
    half = ROPE // 2
    lane = lax.broadcasted_iota(jnp.int32, (1, LANE), 1)
    inv_freq = ROPE_BASE ** (-(lane % half).astype(F32) / half)
    live = (lane < ROPE).astype(F32)
    sign = jnp.where(lane < half, -1.0, 1.0) * live

    def fn(pieces, bvals):
        ang = pieces[0] * bvals[0]
        return [jnp.cos(ang) * bvals[1], jnp.sin(ang) * bvals[2]], []
    pos = jnp.broadcast_to(positions.astype(F32)[:, None], (S, LANE))
    cosp, sins = _rowwise("rope_tab", fn, [Row(pos)], [inv_freq, live, sign], [((LANE,), F32)] * 2, [], _tm(S, LANE))
    r = lax.broadcasted_iota(jnp.int32, (LANE, LANE), 0)
    c = lax.broadcasted_iota(jnp.int32, (LANE, LANE), 1)
    pm = (((c < half) & (r == c + half)) | ((c >= half) & (c < ROPE) & (r == c - half))).astype(F32)
    return (cosp, sins), pm


def _adamw(name, w, g, m, v):
    shape = w.shape
    C = shape[-1]
    R = w.size // C
    tr = R
    for t in (1024, 512, 256, 128, 64, 32, 16, 8):
        if R % t == 0 and t * C * 4 <= (1 << 21):
            tr = t
            break
    c1 = 1.0 - ADAM_B1 ** ADAM_STEP
    c2 = 1.0 - ADAM_B2 ** ADAM_STEP

    def body(w_ref, g_ref, m_ref, v_ref, d_ref, mo_ref, vo_ref):
        gg = g_ref[...]
        mn = ADAM_B1 * m_ref[...] + (1.0 - ADAM_B1) * gg
        vn = ADAM_B2 * v_ref[...] + (1.0 - ADAM_B2) * (gg * gg)
        d_ref[...] = -ADAM_LR * ((mn / c1) / (jnp.sqrt(vn / c2) + ADAM_EPS) + ADAM_WD * w_ref[...])
        mo_ref[...] = mn
        vo_ref[...] = vn

    spec = pl.BlockSpec((tr, C), lambda i: (i, 0))
    outs = pl.pallas_call(body, name=name, grid=(R // tr,), in_specs=[spec] * 4, out_specs=[spec] * 3,
                          out_shape=[jax.ShapeDtypeStruct((R, C), F32)] * 3,
                          compiler_params=_cparams(("parallel",)))(*[t.reshape(R, C) for t in (w, g, m, v)])
    return [o.reshape(shape) for o in outs]


HBM_SPEC = pl.BlockSpec(memory_space=pltpu.HBM)
OTHER_CHIPS = (4, 2, 6)
SIBLING = 1


def _me():
    return lax.axis_index("x"), lax.axis_index("y"), lax.axis_index("c")


def _peer(me, k):
    mx, my, mc = me
    return ((1 - mx) if k & 4 else mx, (1 - my) if k & 2 else my, (1 - mc) if k & 1 else mc)


def _rcopy(src, dst, ssem, rsem, to):
    return pltpu.make_async_remote_copy(src_ref=src, dst_ref=dst, send_sem=ssem, recv_sem=rsem, device_id=to,
                                        device_id_type=MESH)


def _all_gather8(name, x):
    def body(x_ref, o_ref, ssem, rsem, lsem):
        me = _me()
        mine = 4 * me[0] + 2 * me[1] + me[2]
        loc = pltpu.make_async_copy(x_ref, o_ref.at[mine], lsem)
        loc.start()
        sends = []
        for k in range(1, 8):
            cp = _rcopy(x_ref, o_ref.at[mine], ssem.at[k - 1], rsem.at[k - 1], _peer(me, k))
            cp.start()
            sends.append(cp)
        for k in range(1, 8):
            px, py, pc = _peer(me, k)
            _rcopy(x_ref, o_ref.at[4 * px + 2 * py + pc], ssem.at[k - 1], rsem.at[k - 1], (px, py, pc)).wait_recv()
        for cp in sends:
            cp.wait_send()
        loc.wait()

    return pl.pallas_call(body, name=name, out_shape=jax.ShapeDtypeStruct((8,) + x.shape, x.dtype),
                          in_specs=[HBM_SPEC], out_specs=HBM_SPEC,
                          scratch_shapes=[pltpu.SemaphoreType.DMA((7,)), pltpu.SemaphoreType.DMA((7,)),
                                          pltpu.SemaphoreType.DMA(())])(x)


PACK_L = 1024
PACK_RT = 256


def _place_shard(name, wp, chip):
    rh, ln = wp.shape[1:]

    def body(s_ref, w_ref, o_ref):
        o_ref[...] = w_ref[...]

    gs = pltpu.PrefetchScalarGridSpec(
        num_scalar_prefetch=1, grid=(2, rh // PACK_RT),
        in_specs=[pl.BlockSpec((None, PACK_RT, ln), lambda h, i, s_ref: (h, i, 0))],
        out_specs=pl.BlockSpec((None, None, PACK_RT, ln), lambda h, i, s_ref: (s_ref[0], h, i, 0)))
    return pl.pallas_call(body, name=name, grid_spec=gs, out_shape=jax.ShapeDtypeStruct((4,) + wp.shape, wp.dtype),
                          compiler_params=_cparams(("parallel", "parallel")))(chip.reshape(1).astype(jnp.int32), wp)


def _gather_weights(name, w4):
    def body(w_ref, o_ref, ssem, rsem):
        me = _me()
        mc = me[2]
        mine = o_ref.at[2 * me[0] + me[1], mc]
        first = []
        for j, k in enumerate(OTHER_CHIPS):
            cp = _rcopy(mine, mine, ssem.at[j], rsem.at[j], _peer(me, k))
            cp.start()
            first.append(cp)
        passed = []
        for j, k in enumerate(OTHER_CHIPS):
            px, py, _ = _peer(me, k)
            land = o_ref.at[2 * px + py, mc]
            _rcopy(land, land, ssem.at[j], rsem.at[j], _peer(me, k)).wait_recv()
            fw = _rcopy(land, land, ssem.at[3 + j], rsem.at[3 + j], _peer(me, SIBLING))
            fw.start()
            passed.append(fw)
        for j, k in enumerate(OTHER_CHIPS):
            px, py, _ = _peer(me, k)
            land = o_ref.at[2 * px + py, 1 - mc]
            _rcopy(land, land, ssem.at[3 + j], rsem.at[3 + j], _peer(me, SIBLING)).wait_recv()
        for cp in first + passed:
            cp.wait_send()

    return pl.pallas_call(body, name=name, out_shape=jax.ShapeDtypeStruct(w4.shape, w4.dtype),
                          in_specs=[HBM_SPEC], out_specs=HBM_SPEC, input_output_aliases={0: 0},
                          scratch_shapes=[pltpu.SemaphoreType.DMA((6,)), pltpu.SemaphoreType.DMA((6,))])(w4)


def _exchange_half(name, g):
    def body(g_ref, p_ref, ssem, rsem):
        me = _me()
        cps = []
        for s in range(4):
            cp = _rcopy(g_ref.at[s, 1 - me[2]], p_ref.at[s], ssem.at[s], rsem.at[s], _peer(me, SIBLING))
            cp.start()
            cps.append(cp)
        for cp in cps:
            cp.wait()

    return pl.pallas_call(body, name=name, out_shape=jax.ShapeDtypeStruct((4,) + g.shape[2:], g.dtype),
                          in_specs=[HBM_SPEC], out_specs=HBM_SPEC,
                          scratch_shapes=[pltpu.SemaphoreType.DMA((4,)), pltpu.SemaphoreType.DMA((4,))])(g)


def _scatter_chips(name, q):
    def body(q_ref, t_ref, ssem, rsem):
        me = _me()
        cps = []
        for j, k in enumerate(OTHER_CHIPS):
            px, py, _ = _peer(me, k)
            cp = _rcopy(q_ref.at[2 * px + py], t_ref.at[j], ssem.at[j], rsem.at[j], _peer(me, k))
            cp.start()
            cps.append(cp)
        for cp in cps:
            cp.wait()

    return pl.pallas_call(body, name=name, out_shape=jax.ShapeDtypeStruct((3,) + q.shape[1:], q.dtype),
                          in_specs=[HBM_SPEC], out_specs=HBM_SPEC,
                          scratch_shapes=[pltpu.SemaphoreType.DMA((3,)), pltpu.SemaphoreType.DMA((3,))])(q)


def _exchange_full(name, r2):
    def body(r_ref, o_ref, ssem, rsem):
        me = _me()
        mc = me[2]
        cp = _rcopy(o_ref.at[mc], o_ref.at[mc], ssem, rsem, _peer(me, SIBLING))
        cp.start()
        _rcopy(o_ref.at[1 - mc], o_ref.at[1 - mc], ssem, rsem, _peer(me, SIBLING)).wait_recv()
        cp.wait_send()

    return pl.pallas_call(body, name=name, out_shape=jax.ShapeDtypeStruct(r2.shape, r2.dtype),
                          in_specs=[HBM_SPEC], out_specs=HBM_SPEC, input_output_aliases={0: 0},
                          scratch_shapes=[pltpu.SemaphoreType.DMA(()), pltpu.SemaphoreType.DMA(())])(r2)


def _add_half(name, g, p, c):
    rh, ln = g.shape[2:]

    def body(c_ref, g_ref, p_ref, o_ref):
        o_ref[0] = (g_ref[0, 0] + p_ref[0]).astype(o_ref.dtype)

    gs = pltpu.PrefetchScalarGridSpec(
        num_scalar_prefetch=1, grid=(4, rh // PACK_RT),
        in_specs=[pl.BlockSpec((1, 1, PACK_RT, ln), lambda s, i, c_ref: (s, c_ref[0], i, 0)),
                  pl.BlockSpec((1, PACK_RT, ln), lambda s, i, c_ref: (s, i, 0))],
        out_specs=pl.BlockSpec((1, PACK_RT, ln), lambda s, i, c_ref: (s, i, 0)))
    return pl.pallas_call(body, name=name, grid_spec=gs, out_shape=jax.ShapeDtypeStruct((4, rh, ln), BF16),
                          compiler_params=_cparams(("parallel", "parallel")))(c.reshape(1).astype(jnp.int32), g, p)


def _add_chips(name, q, t, chip, c):
    rh, ln = q.shape[1:]

    def body(s_ref, c_ref, q_ref, t_ref, o_ref):
        o_ref[...] = ((q_ref[0].astype(F32) + t_ref[0].astype(F32)) + t_ref[1].astype(F32)) + t_ref[2].astype(F32)

    gs = pltpu.PrefetchScalarGridSpec(
        num_scalar_prefetch=2, grid=(rh // PACK_RT,),
        in_specs=[pl.BlockSpec((1, PACK_RT, ln), lambda i, s_ref, c_ref: (s_ref[0], i, 0)),
                  pl.BlockSpec((3, PACK_RT, ln), lambda i, s_ref, c_ref: (0, i, 0))],
        out_specs=pl.BlockSpec((None, PACK_RT, ln), lambda i, s_ref, c_ref: (c_ref[0], i, 0)))
    return pl.pallas_call(body, name=name, grid_spec=gs, out_shape=jax.ShapeDtypeStruct((2, rh, ln), F32),
                          compiler_params=_cparams(("parallel",)))(chip.reshape(1).astype(jnp.int32),
                                                                    c.reshape(1).astype(jnp.int32), q, t)


def _sum8(name, a):
    def body(a_ref, o_ref):
        acc = a_ref[0]
        for d in range(1, 8):
            acc = acc + a_ref[d]
        o_ref[...] = acc
    return pl.pallas_call(body, name=name, out_shape=jax.ShapeDtypeStruct(a.shape[1:], F32))(a)


def _silu_rows(name, a):
    def body(a_ref, o_ref):
        o_ref[...] = _silu(a_ref[...])
    return pl.pallas_call(body, name=name, out_shape=jax.ShapeDtypeStruct(a.shape, F32))(a)


REST = (("gdn_w_out", 1), ("mla_w_dkv", 0), ("mla_w_ukv", 1), ("mla_w_dq", 1), ("mla_w_uq", 2), ("mla_w_out", 1))


def _packed_rows(n):
    per_half = -(-n // (2 * PACK_L))
    return -(-per_half // PACK_RT) * PACK_RT


def _pack_flat(flat):
    n = flat.shape[-1]
    rh = _packed_rows(n)
    pad = [(0, 0)] * (flat.ndim - 1) + [(0, 2 * rh * PACK_L - n)]
    return jnp.pad(flat, pad).reshape(flat.shape[:-1] + (2, rh, PACK_L))


def _shards_first(full, axis):
    sh = full.shape
    t = full.reshape(sh[:axis] + (4, sh[axis] // 4) + sh[axis + 1:])
    return jnp.moveaxis(t, axis, 0)


def _shards_merge(stacked, axis):
    t = jnp.moveaxis(stacked, 0, axis)
    sh = t.shape
    return t.reshape(sh[:axis] + (4 * sh[axis + 1],) + sh[axis + 2:])


def _pack_small(parts):
    flat = jnp.concatenate([p.reshape(-1).astype(F32) for p in parts])
    n = flat.shape[0]
    rows = -(-n // (SUB * LANE)) * SUB
    return jnp.pad(flat, (0, rows * LANE - n)).reshape(rows, LANE)


def _unpack_small(buf, shapes):
    lead = buf.shape[:-2]
    flat = buf.reshape(lead + (-1,))
    out, off = [], 0
    for sh in shapes:
        n = 1
        for d in sh:
            n *= d
        out.append(flat[..., off:off + n].reshape(lead + tuple(sh)))
        off += n
    return out


WEIGHTS = ('ada_w', 'ada_b', 'norm_g', 'ffn_w_in', 'ffn_w_out', 'gdn_w_in', 'gdn_conv_w', 'gdn_a_log', 'gdn_dt_bias',
           'gdn_norm_g', 'gdn_w_out', 'kv_ada_w', 'kv_ada_b', 'kv_norm_g', 'mla_w_dkv', 'mla_kv_norm_g', 'mla_w_ukv',
           'mla_k_norm_g', 'mla_w_dq', 'mla_q_lora_norm_g', 'mla_w_uq', 'mla_q_norm_g', 'mla_w_out')
ARGS = ('x', 'c', 'positions') + WEIGHTS + ('loss_target',) + tuple('m_' + n for n in WEIGHTS) + tuple('v_' + n for n in WEIGHTS)


def _split_norm(v):
    return v[None, :DH], _pad_lanes(v[None, DH:], 0)


def _join_norm(gn, gr):
    return jnp.concatenate([gn[0], gr[0, :ROPE]])


def _step(x, tgt, pos, mods, kvmod, W, P):
    tabs, pm = _rope_tables(pos)
    m3 = lambda l, i: tuple(mods[l][3 * i + j][None] for j in range(3))
    ng = lambda l, i: P["norm_g"][l, i][None]
    gdn_p, mla_p = [], []
    for l in range(2):
        gdn_p.append(dict(w_in=jnp.pad(W["gdn_w_in"][l], ((0, 0), (0, GDN_IN - W["gdn_w_in"].shape[2]))),
                          conv_w8=jnp.pad(P["gdn_conv_w"][l], ((0, 4), (0, 0))),
                          a_log128=_pad_lanes(P["gdn_a_log"][l][None], NH), dt_bias128=_pad_lanes(P["gdn_dt_bias"][l][None], NH),
                          norm_g=P["gdn_norm_g"][l][None], w_out=W["gdn_w_out"][l]))
        q_gn, q_gr = _split_norm(P["mla_q_norm_g"][l])
        mla_p.append(dict(w_dq=W["mla_w_dq"][l], ql_g=P["mla_q_lora_norm_g"][l][None],
                          w_uq=jnp.pad(W["mla_w_uq"][l].reshape(QL, NH, QKH), ((0, 0), (0, 0), (0, HP - QKH))).reshape(QL, NH * HP),
                          q_gn=q_gn, q_gr=q_gr, pm=pm, w_out=W["mla_w_out"][l]))
    k_gn, k_gr = _split_norm(P["mla_k_norm_g"])
    kv_p = dict(kv_norm_g=P["kv_norm_g"][None], w_dkv=jnp.pad(W["mla_w_dkv"], ((0, 0), (0, QL - KVL - ROPE))),
                kv_lat_g=P["mla_kv_norm_g"][None],
                w_ukv=W["mla_w_ukv"].reshape(KVL, NH, 2, DH).transpose(0, 2, 1, 3).reshape(KVL, 2 * NH * DH),
                k_gn=k_gn, k_gr=k_gr, pm=pm)
    kvm = (kvmod[0][None], kvmod[1][None])

    res = {}
    for l in range(4):
        x, res[l, 0] = _ffn_fwd(f"l{l}a", x, m3(l, 0), ng(l, 0), W["ffn_w_in"], W["ffn_w_out"], 2 * l)
        if l < 2:
            x, res[l, 1] = _gdn_layer_fwd(f"l{l}g", x, m3(l, 1), ng(l, 1), gdn_p[l])
        else:
            x, res[l, 1] = _mla_layer_fwd(f"l{l}m", x, m3(l, 1), ng(l, 1), mla_p[l - 2], kp, kvf, tabs)
        x, res[l, 2] = _ffn_fwd(f"l{l}b", x, m3(l, 2), ng(l, 2), W["ffn_w_in"], W["ffn_w_out"], 2 * l + 1)
        if l == 1:
            kp, kvf, kres = _kv_fwd(x, kvm, kv_p, tabs)
    loss, dx = _loss_head(x, tgt)

    gw = {n: [None] * W[n].shape[0] for n in ("gdn_w_in", "gdn_w_out", "mla_w_dq", "mla_w_uq", "mla_w_out")}
    g_in4 = g_out4 = None
    gp = {n: [None] * 2 for n in ("gdn_conv_w", "gdn_a_log", "gdn_dt_bias", "gdn_norm_g", "mla_q_lora_norm_g", "mla_q_norm_g")}
    gnorm = [[None] * 3 for _ in range(4)]
    dmod = [[None] * NMOD for _ in range(4)]
    dkp = dv = None
    for l in (3, 2, 1, 0):
        if l == 1:
            dx, gk = _kv_bwd(dkp, dv, dx, kres, kvm, kvf, kv_p, tabs)
        for i in (2, 1, 0):
            if i != 1:
                dx, g_in4, g_out4, gd = _ffn_bwd(f"l{l}{'ab'[i // 2]}", dx, res[l, i], m3(l, i), ng(l, i), W["ffn_w_in"],
                                                 W["ffn_w_out"], 2 * l + i // 2, g_in4, g_out4)
            elif l < 2:
                dx, gd = _gdn_layer_bwd(f"l{l}g", dx, res[l, 1], m3(l, 1), ng(l, 1), gdn_p[l])
                gw["gdn_w_in"][l] = gd["w_in"][:, :W["gdn_w_in"].shape[2]]
                gw["gdn_w_out"][l] = gd["w_out"]
                gp["gdn_conv_w"][l] = gd["conv_w8"][:4]
                gp["gdn_a_log"][l] = gd["a_log128"][0, NH:2 * NH]
                gp["gdn_dt_bias"][l] = gd["dt_bias128"][0, NH:2 * NH]
                gp["gdn_norm_g"][l] = gd["norm_g"][0]
            else:
                dx, dkp_l, dv_l, gd = _mla_layer_bwd(f"l{l}m", dx, res[l, 1], m3(l, 1), ng(l, 1), mla_p[l - 2], kp, kvf, tabs)
                dkp = dkp_l if dkp is None else dkp + dkp_l
                dv = dv_l if dv is None else dv + dv_l
                gw["mla_w_dq"][l - 2], gw["mla_w_out"][l - 2] = gd["w_dq"], gd["w_out"]
                gw["mla_w_uq"][l - 2] = gd["w_uq"].reshape(QL, NH, HP)[:, :, :QKH].reshape(QL, NH * QKH)
                gp["mla_q_lora_norm_g"][l - 2] = gd["ql_g"][0]
                gp["mla_q_norm_g"][l - 2] = _join_norm(gd["q_gn"], gd["q_gr"])
            gnorm[l][i] = gd["g"][0]
            for j in range(3):
                dmod[l][3 * i + j] = gd["mod"][j][0]
    gwf = {n: jnp.stack(v) for n, v in gw.items()}
    gwf["ffn_w_in"], gwf["ffn_w_out"] = g_in4, g_out4
    gwf["mla_w_dkv"] = gk["w_dkv"][:, :KVL + ROPE]
    gwf["mla_w_ukv"] = gk["w_ukv"].reshape(KVL, 2, NH, DH).transpose(0, 2, 1, 3).reshape(KVL, 2 * NH * DH)
    gpf = {n: jnp.stack(v) for n, v in gp.items()}
    gpf["norm_g"] = jnp.stack([jnp.stack(r) for r in gnorm])
    gpf["kv_norm_g"] = gk["kv_norm_g"][0]
    gpf["mla_kv_norm_g"] = gk["kv_lat_g"][0]
    gpf["mla_k_norm_g"] = _join_norm(gk["k_gn"], gk["k_gr"])
    dmods = jnp.stack([jnp.stack(r) for r in dmod])
    dkvmod = jnp.stack([gk["mod"][0][0], gk["mod"][1][0]])
    return loss, dx, gwf, gpf, dmods, dkvmod


SMALL = ("norm_g", "gdn_conv_w", "gdn_a_log", "gdn_dt_bias", "gdn_norm_g", "kv_norm_g", "mla_kv_norm_g", "mla_k_norm_g",
         "mla_q_lora_norm_g", "mla_q_norm_g")


def kernel(x, c, positions, ada_w, ada_b, norm_g, ffn_w_in, ffn_w_out, gdn_w_in, gdn_conv_w, gdn_a_log, gdn_dt_bias,
           gdn_norm_g, gdn_w_out, kv_ada_w, kv_ada_b, kv_norm_g, mla_w_dkv, mla_kv_norm_g, mla_w_ukv, mla_k_norm_g,
           mla_w_dq, mla_q_lora_norm_g, mla_w_uq, mla_q_norm_g, mla_w_out, loss_target, m_ada_w, m_ada_b, m_norm_g,
           m_ffn_w_in, m_ffn_w_out, m_gdn_w_in, m_gdn_conv_w, m_gdn_a_log, m_gdn_dt_bias, m_gdn_norm_g, m_gdn_w_out,
           m_kv_ada_w, m_kv_ada_b, m_kv_norm_g, m_mla_w_dkv, m_mla_kv_norm_g, m_mla_w_ukv, m_mla_k_norm_g, m_mla_w_dq,
           m_mla_q_lora_norm_g, m_mla_w_uq, m_mla_q_norm_g, m_mla_w_out, v_ada_w, v_ada_b, v_norm_g, v_ffn_w_in,
           v_ffn_w_out, v_gdn_w_in, v_gdn_conv_w, v_gdn_a_log, v_gdn_dt_bias, v_gdn_norm_g, v_gdn_w_out, v_kv_ada_w,
           v_kv_ada_b, v_kv_norm_g, v_mla_w_dkv, v_mla_kv_norm_g, v_mla_w_ukv, v_mla_k_norm_g, v_mla_w_dq,
           v_mla_q_lora_norm_g, v_mla_w_uq, v_mla_q_norm_g, v_mla_w_out):
    a = dict(locals())
    mx, my, mc = _me()
    dev = 4 * mx + 2 * my + mc
    chip = 2 * mx + my
    x, tgt, pos = a["x"][0], a["loss_target"][0], a["positions"][0]
    take = lambda arr, i, axis=0: lax.dynamic_index_in_dim(arr, i, axis, keepdims=False)

    pre = _all_gather8("ag_pre", _pack_small([a["c"], a["gdn_conv_w"], a["norm_g"]]))
    c_all, conv_sh, norm_sh = _unpack_small(pre, [(D,), a["gdn_conv_w"].shape, a["norm_g"].shape])
    P = {n: a[n] for n in SMALL}
    P["gdn_conv_w"] = jnp.concatenate([conv_sh[2 * s] for s in range(4)], axis=2)
    P["norm_g"] = jnp.concatenate([norm_sh[2 * s] for s in range(4)], axis=2)
    c_act = _silu_rows("c_act", c_all)
    nada = a["ada_w"].shape[2]
    nkv = a["kv_ada_w"].shape[1]
    modp = [_matmul(f"mod{l}", c_act, a["ada_w"][l], precise=True) for l in range(4)]
    kvp = _matmul("modkv", c_act, a["kv_ada_w"], precise=True)
    mp = _all_gather8("ag_mod", _pack_small(modp + [kvp]))
    modp_all, kvp_all = _unpack_small(mp, [(4, 8, nada), (8, nkv)])
    mods = jnp.concatenate([take(modp_all[2 * s], dev, 1) for s in range(4)], axis=1) + a["ada_b"]
    mods = mods.reshape(4, NMOD, D)
    kvmod = (jnp.concatenate([take(kvp_all[2 * s], dev, 0) for s in range(4)]) + a["kv_ada_b"]).reshape(2, D)

    def gather(tag, w2):
        return _gather_weights("ag_" + tag, _place_shard("own_" + tag, w2, chip))

    def reduce(tag, g4):
        q = _add_half("rsp_" + tag, g4, _exchange_half("rs1_" + tag, g4), mc)
        r2 = _add_chips("rsc_" + tag, q, _scatter_chips("rs2_" + tag, q), chip, mc)
        return _exchange_full("rs3_" + tag, r2)

    halves = lambda t: t.reshape((2, -1) + t.shape[-1:])
    W = {n: gather(t, halves(a[n].astype(BF16))).reshape((4, 8) + a[n].shape[2:])
         for n, t in (("ffn_w_in", "wi"), ("ffn_w_out", "wo"))}
    wg = gather("wg", a["gdn_w_in"].astype(BF16))
    W["gdn_w_in"] = jnp.concatenate([wg[s] for s in range(4)], axis=2)
    wall = gather("wr", _pack_flat(jnp.concatenate([a[n].reshape(-1).astype(BF16) for n, _ in REST]))).reshape(4, -1)
    off = 0
    for n, ax in REST:
        sz = a[n].size
        W[n] = _shards_merge(wall[:, off:off + sz].reshape((4,) + a[n].shape), ax)
        off += sz

    loss, dx, gw, gp, dmods, dkvmod = _step(x, tgt, pos, mods, kvmod, W, P)
    loss = lax.psum(loss, ("x", "y", "c"))

    grads = {n: reduce(t, gw[n].reshape((4, 2, -1) + a[n].shape[-1:])).reshape(a[n].shape)
             for n, t in (("ffn_w_in", "wi"), ("ffn_w_out", "wo"))}
    ng = a["gdn_w_in"].shape[2]
    grads["gdn_w_in"] = reduce("wg", jnp.stack([gw["gdn_w_in"][:, :, s * ng:(s + 1) * ng] for s in range(4)]))
    gsh = reduce("wr", _pack_flat(jnp.concatenate([_shards_first(gw[n], ax).reshape(4, -1) for n, ax in REST], axis=1)))
    gsh = gsh.reshape(-1)
    off = 0
    for n, _ in REST:
        grads[n] = gsh[off:off + a[n].size].reshape(a[n].shape)
        off += a[n].size

    small = _all_gather8("ag_small", _pack_small([dmods, dkvmod] + [gp[n] for n in SMALL]))
    shapes = [(4, NMOD * D), (2 * D,)] + [gp[n].shape for n in SMALL]
    dmod_all, dkv_all = _unpack_small(small, shapes)[:2]
    tot = _unpack_small(_sum8("sum_small", small), shapes)
    grads["ada_b"], grads["kv_ada_b"] = tot[0], tot[1]
    for n, t in zip(SMALL, tot[2:]):
        grads[n] = t
    grads["norm_g"] = lax.dynamic_slice_in_dim(grads["norm_g"], chip * a["norm_g"].shape[2], a["norm_g"].shape[2], 2)
    grads["gdn_conv_w"] = lax.dynamic_slice_in_dim(grads["gdn_conv_w"], chip * a["gdn_conv_w"].shape[2],
                                                   a["gdn_conv_w"].shape[2], 2)
    ca = jnp.pad(c_act, ((0, LANE - 8), (0, 0)))
    dm = jnp.pad(lax.dynamic_slice_in_dim(dmod_all.reshape(8, 4, NMOD * D), chip * nada, nada, 2), ((0, LANE - 8), (0, 0), (0, 0)))
    grads["ada_w"] = jnp.stack([_matmul(f"gada{l}", ca, dm[:, l], "tn", precise=True) for l in range(4)])
    dk = jnp.pad(lax.dynamic_slice_in_dim(dkv_all, chip * nkv, nkv, 1), ((0, LANE - 8), (0, 0)))
    grads["kv_ada_w"] = _matmul("gadakv", ca, dk, "tn", precise=True)

    upd = [_adamw("adamw_" + n, a[n], grads[n], a["m_" + n], a["v_" + n]) for n in WEIGHTS]
    return (loss, dx[None], *[grads[n] for n in WEIGHTS], *[u[0] for u in upd], *[u[1] for u in upd], *[u[2] for u in upd])
```

```python
import functools

import jax
import jax.numpy as jnp
from jax import lax
from jax.experimental import pallas as pl
from jax.experimental.pallas import tpu as pltpu

F32 = jnp.float32
BF16 = jnp.bfloat16
HI = lax.Precision.HIGHEST
MESH = pl.DeviceIdType.MESH

D = 1024
NH = 8
DH = 128
FF = 2816
NMOD = 9
CHUNK = 64
ROPE = 64
QKH = 192
HP = 256
KVL = 256
QL = 384
GDN_IN = 4224
GATE_CB = 32
EPS = 1e-6
ROPE_BASE = 10000.0
LANE = 128
SUB = 8
VMEM_LIMIT = 56 * 1024 * 1024

ADAM_LR, ADAM_B1, ADAM_B2, ADAM_EPS, ADAM_WD, ADAM_STEP = 0.001, 0.9, 0.999, 1e-08, 0.01, 10


def _tile(n, prefs=(512, 384, 256, 128)):
    for p in prefs:
        if n % p == 0:
            return p
    return n


def _cparams(sem):
    return pltpu.CompilerParams(dimension_semantics=sem, vmem_limit_bytes=VMEM_LIMIT)


class Row:
    def __init__(self, arr, width=None, cb=0, splits=None, halo=None):
        self.arr = arr
        self.width = arr.shape[1] if width is None else width
        self.cb = cb
        self.splits = splits
        self.halo = halo


def _rowwise(name, fn, rows, bcs, outs, accs, tm):
    S = rows[0].arr.shape[0]
    n = S // tm
    nr, nb, no, na = len(rows), len(bcs), len(outs), len(accs)

    def body(*refs):
        rrefs, brefs = refs[:nr], refs[nr:nr + nb]
        orefs, arefs = refs[nr + nb:nr + nb + no], refs[nr + nb + no:]
        pieces = []
        for r, ref in zip(rows, rrefs):
            if r.splits is None:
                pieces.append(ref[...])
            else:
                off = 0
                for w in r.splits:
                    pieces.append(ref[:, off:off + w])
                    off += w
        out_pieces, acc_vals = fn(pieces, [b[...] for b in brefs])
        k = 0
        for (widths, dt), oref in zip(outs, orefs):
            off = 0
            for w in widths:
                oref[:, off:off + w] = out_pieces[k].astype(dt)
                k += 1
                off += w
        if na:
            @pl.when(pl.program_id(0) == 0)
            def _():
                for a in arefs:
                    a[...] = jnp.zeros(a.shape, F32)
            for a, v in zip(arefs, acc_vals):
                a[...] += v

    in_specs = []
    for r in rows:
        if r.halo is None:
            in_specs.append(pl.BlockSpec((tm, r.width), lambda i, cb=r.cb: (i, cb)))
        elif r.halo == "prev":
            in_specs.append(pl.BlockSpec((SUB, r.width), lambda i, cb=r.cb: (jnp.maximum(i * (tm // SUB) - 1, 0), cb)))
        else:
            in_specs.append(pl.BlockSpec((SUB, r.width), lambda i, cb=r.cb: (jnp.minimum((i + 1) * (tm // SUB), S // SUB - 1), cb)))
    in_specs += [pl.BlockSpec(b.shape, lambda i, nd=b.ndim: (0,) * nd) for b in bcs]
    out_specs = [pl.BlockSpec((tm, sum(w)), lambda i: (i, 0)) for w, _ in outs]
    out_specs += [pl.BlockSpec(s, lambda i: (0, 0)) for s in accs]
    out_shape = [jax.ShapeDtypeStruct((S, sum(w)), dt) for w, dt in outs]
    out_shape += [jax.ShapeDtypeStruct(s, F32) for s in accs]
    res = pl.pallas_call(body, name=name, grid=(n,), in_specs=in_specs, out_specs=out_specs, out_shape=out_shape,
                         compiler_params=_cparams(("arbitrary",)))(*[r.arr for r in rows], *bcs)
    return res


def _rw_fwd(name, f, rows, bcs, outs, tm):
    def fn(pieces, bvals):
        return list(f(*[p.astype(F32) for p in pieces], *[b.astype(F32) for b in bvals])), []
    return _rowwise(name, fn, rows, bcs, outs, [], tm)


def _npieces(rows):
    return sum(1 if r.splits is None else len(r.splits) for r in rows)


def _rw_bwd(name, f, rows, bcs, cts, drow, dbc, outs, tm, add=None):
    np_, nct = _npieces(rows), _npieces(cts)

    def fn(pieces, bvals):
        allv = [p.astype(F32) for p in pieces[:np_]] + [b.astype(F32) for b in bvals]
        ct = [p.astype(F32) for p in pieces[np_:np_ + nct]]
        didx = [i for i, m in enumerate(list(drow) + list(dbc)) if m]

        def g(*dv):
            full = list(allv)
            for i, v in zip(didx, dv):
                full[i] = v
            return tuple(f(*full))

        _, vjp = jax.vjp(g, *[allv[i] for i in didx])
        grads = vjp(tuple(ct))
        nrd = sum(bool(m) for m in drow)
        rg, bg = list(grads[:nrd]), list(grads[nrd:])
        if add is not None:
            rg[0] = rg[0] + pieces[np_ + nct].astype(F32)
        return rg, bg

    accs = [b.shape for b, m in zip(bcs, dbc) if m]
    return _rowwise(name, fn, list(rows) + list(cts) + ([add] if add is not None else []), bcs, outs, accs, tm)


def _sigmoid(x):
    return 1.0 / (1.0 + jnp.exp(-x))


def _silu(x):
    return x * _sigmoid(x)


def _softplus(x):
    return jnp.maximum(x, 0.0) + jnp.log(1.0 + jnp.exp(-jnp.abs(x)))


def f_mod(x, g, shift, scale):
    y = x * lax.rsqrt(jnp.mean(x * x, axis=-1, keepdims=True) + EPS)
    return (y * g * (1.0 + scale) + shift,)


def f_rms(x, g):
    return (x * lax.rsqrt(jnp.mean(x * x, axis=-1, keepdims=True) + EPS) * g,)


def f_act(gate, up):
    return (_silu(gate) * up,)


def make_f_res(coef):
    def f_res(y, gate):
        return (coef * gate * y,)
    return f_res


def f_gdnpre(*p):
    out = []
    for i, t in enumerate(p):
        t = _silu(t)
        if i < 2 * NH:
            t = t * lax.rsqrt(jnp.sum(t * t, axis=-1, keepdims=True) + EPS)
        out.append(t)
    return tuple(out)


def f_gates(gates, a_log, dt_bias):
    return _sigmoid(gates), -jnp.exp(a_log) * _softplus(gates + dt_bias)


def f_gdnpost(*a):
    o, z, g = a[:NH], a[NH:2 * NH], a[2 * NH]
    out = []
    for oh, zh in zip(o, z):
        y = oh * lax.rsqrt(jnp.mean(oh * oh, axis=-1, keepdims=True) + EPS) * g
        out.append(y * _silu(zh))
    return tuple(out)


def make_f_qk(shared_rope):
    def f(*a):
        if shared_rope:
            ns, rs = a[:NH], [a[NH]] * NH
            cosp, sins, gn, gr, pm = a[NH + 1:NH + 6]
        else:
            ns, rs = a[0:2 * NH:2], a[1:2 * NH:2]
            cosp, sins, gn, gr, pm = a[2 * NH:2 * NH + 5]
        out = []
        for n, r in zip(ns, rs):
            ss = jnp.sum(n * n, axis=-1, keepdims=True) + jnp.sum(r * r, axis=-1, keepdims=True)
            rstd = lax.rsqrt(ss * (1.0 / QKH) + EPS)
            yn = n * rstd * gn
            yr = r * rstd * gr
            sw = jnp.dot(yr, pm, precision=HI, preferred_element_type=F32)
            out += [yn, yr * cosp + sw * sins]
        return tuple(out)
    return f


def _matmul(name, a, b, mode="nn", out_dtype=F32, precise=False, lay=None, li=0, into=None, nmat=1):
    if lay == "b_cols":
        per = b.shape[3]
        rb, cb = b.shape[2], 4 * per
    elif lay == "b_rows":
        per = b.shape[2]
        rb, cb = 4 * per, b.shape[3]
    else:
        rb, cb = b.shape
    if mode == "nn":
        (M, K), N = a.shape, cb
    elif mode == "nt":
        (M, K), N = a.shape, rb
    else:
        (K, M), N = a.shape, cb
    tm = _tile(M, (1024, 512, 256, 128))
    tn = _tile(N, (1024, 512, 384, 256, 128))
    tk = _tile(K, (1408, 1024, 512, 384, 256, 128))
    if lay == "b_cols":
        tn, tk = (per, tk) if mode == "nn" else (tn, per)
    elif lay == "b_rows":
        tm, tn, tk = (tm, 512, K) if mode == "nn" else (min(tm, 512), N, tk)
    elif lay == "o_cols":
        per = N // 4
        tn = per
    elif lay == "o_rows":
        per = M // 4
        tm, tn = M, 512
    nk = K // tk
    dims = {"nn": (((1,), (0,)), ((), ())), "nt": (((1,), (1,)), ((), ())), "tn": (((0,), (0,)), ((), ()))}[mode]

    def body(a_ref, b_ref, *rest):
        o_ref, acc_ref = rest[-2:]
        k = pl.program_id(2)

        @pl.when(k == 0)
        def _():
            acc_ref[...] = jnp.zeros(acc_ref.shape, F32)

        bv = b_ref[...]
        if lay == "b_rows":
            bv = bv.reshape(4 * per, bv.shape[2])
        if precise:
            acc_ref[...] += lax.dot_general(a_ref[...].astype(F32), bv.astype(F32), dims, precision=HI,
                                            preferred_element_type=F32)
        else:
            acc_ref[...] += lax.dot_general(a_ref[...].astype(BF16), bv.astype(BF16), dims, preferred_element_type=F32)

        @pl.when(k == nk - 1)
        def _():
            if lay == "o_rows":
                for s in range(4):
                    o_ref[s] = acc_ref[s * per:(s + 1) * per, :].astype(o_ref.dtype)
            else:
                o_ref[...] = acc_ref[...].astype(o_ref.dtype)

    a_spec = pl.BlockSpec((tk, tm), lambda i, j, k: (k, i)) if mode == "tn" else pl.BlockSpec((tm, tk), lambda i, j, k: (i, k))
    if lay == "b_cols":
        b_spec = (pl.BlockSpec((None, None, tk, per), lambda i, j, k: (j, li, k, 0)) if mode == "nn" else
                  pl.BlockSpec((None, None, tn, per), lambda i, j, k: (k, li, j, 0)))
    elif lay == "b_rows":
        b_spec = (pl.BlockSpec((4, None, per, tn), lambda i, j, k: (0, li, 0, j)) if mode == "nn" else
                  pl.BlockSpec((4, None, per, tk), lambda i, j, k: (0, li, 0, k)))
    elif mode == "nt":
        b_spec = pl.BlockSpec((tn, tk), lambda i, j, k: (j, k))
    else:
        b_spec = pl.BlockSpec((tk, tn), lambda i, j, k: (k, j))
    if lay == "o_cols":
        o_spec = pl.BlockSpec((None, None, tm, per), lambda i, j, k: (j, li, i, 0))
        o_shape = jax.ShapeDtypeStruct((4, nmat, M, per), out_dtype)
    elif lay == "o_rows":
        o_spec = pl.BlockSpec((4, None, per, tn), lambda i, j, k: (0, li, 0, j))
        o_shape = jax.ShapeDtypeStruct((4, nmat, per, N), out_dtype)
    else:
        o_spec = pl.BlockSpec((tm, tn), lambda i, j, k: (i, j))
        o_shape = jax.ShapeDtypeStruct((M, N), out_dtype)
    in_specs, args, alias = [a_spec, b_spec], [a, b], {}
    if into is not None:
        in_specs.append(pl.BlockSpec(memory_space=pl.ANY))
        args.append(into)
        alias = {2: 0}
    return pl.pallas_call(body, name=name, grid=(M // tm, N // tn, nk), in_specs=in_specs, out_specs=o_spec,
                          out_shape=o_shape, scratch_shapes=[pltpu.VMEM((tm, tn), F32)], input_output_aliases=alias,
                          compiler_params=_cparams(("parallel", "parallel", "arbitrary")))(*args)


def _shift_down(t, p, d):
    if d == 0:
        return t
    tr = pltpu.roll(t, d, 0)
    pr = pltpu.roll(p, d, 0)
    r8 = lax.broadcasted_iota(jnp.int32, p.shape, 0)
    first = jnp.where(r8 < d, pr, tr[:SUB])
    return jnp.concatenate([first, tr[SUB:]], axis=0)


def _shift_up(t, nx, d):
    if d == 0:
        return t
    tm = t.shape[0]
    tr = pltpu.roll(t, tm - d, 0)
    nr = pltpu.roll(nx, SUB - d, 0)
    r8 = lax.broadcasted_iota(jnp.int32, nx.shape, 0)
    last = jnp.where(r8 >= SUB - d, nr, tr[tm - SUB:])
    return jnp.concatenate([tr[:tm - SUB], last], axis=0)


def _conv_fwd(name, proj, w8, C, tm):
    def fn(pieces, bvals):
        t, p = pieces[0].astype(F32), pieces[1].astype(F32)
        w = bvals[0]
        p = jnp.where(pl.program_id(0) == 0, 0.0, p)
        out = w[3:4] * t
        for d in (1, 2, 3):
            out = out + w[3 - d:4 - d] * _shift_down(t, p, d)
        return [out], []
    return _rowwise(name, fn, [Row(proj, C), Row(proj, C, halo="prev")], [w8], [((C,), F32)], [], tm)[0]


def _conv_bwd(name, proj, dout, w8, C, tm, out_dtype):
    n = proj.shape[0] // tm

    def fn(pieces, bvals):
        t, p, g, gn = [v.astype(F32) for v in pieces]
        w = bvals[0]
        i = pl.program_id(0)
        p = jnp.where(i == 0, 0.0, p)
        gn = jnp.where(i == n - 1, 0.0, gn)
        dx = w[3:4] * g
        dws = [jnp.sum(g * t, axis=0, keepdims=True)]
        for d in (1, 2, 3):
            dx = dx + w[3 - d:4 - d] * _shift_up(g, gn, d)
            dws.append(jnp.sum(g * _shift_down(t, p, d), axis=0, keepdims=True))
        dw = jnp.concatenate([dws[3], dws[2], dws[1], dws[0], jnp.zeros((4, g.shape[1]), F32)], axis=0)
        return [dx], [dw]
    return _rowwise(name, fn, [Row(proj, C), Row(proj, C, halo="prev"), Row(dout), Row(dout, halo="next")], [w8],
                    [((C,), out_dtype)], [(SUB, C)], tm)


def _bdot(a, b, ca, cb):
    return lax.dot_general(a.astype(BF16), b.astype(BF16), (((ca,), (cb,)), ((0,), (0,))), preferred_element_type=F32)


def _bdot3(a, b, ca, cb):
    dims = (((ca,), (cb,)), ((0,), (0,)))
    ah, bh = a.astype(BF16), b.astype(BF16)
    al, bl = (a - ah.astype(F32)).astype(BF16), (b - bh.astype(F32)).astype(BF16)
    d = lambda x, y: lax.dot_general(x, y, dims, preferred_element_type=F32)
    return d(ah, bh) + (d(ah, bl) + d(al, bh))


@jax.custom_vjp
def _bmm3(a, b):
    return _bdot3(a, b, 2, 1)


_bmm3.defvjp(lambda a, b: (_bdot3(a, b, 2, 1), (a, b)),
             lambda res, g: (_bdot3(g, res[1], 2, 2), _bdot3(res[0], g, 1, 1)))


def _neumann(nl):
    C = nl.shape[1]
    eye = (lax.broadcasted_iota(jnp.int32, (1, C, C), 1) == lax.broadcasted_iota(jnp.int32, (1, C, C), 2)).astype(F32)
    T = eye + nl
    pw = nl
    for _ in range(C.bit_length() - 2):
        pw = _bdot3(pw, pw, 2, 1)
        T = T + _bdot3(T, pw, 2, 1)
    return T


_unit_lower_inv = jax.custom_vjp(_neumann)


def _unit_lower_inv_fwd(nl):
    T = _neumann(nl)
    return T, T


def _unit_lower_inv_bwd(T, g):
    return (_bdot3(_bdot3(T, g, 1, 1), T, 2, 2),)


_unit_lower_inv.defvjp(_unit_lower_inv_fwd, _unit_lower_inv_bwd)


def _gdn_chunk(q, k, v, gcol, grow, bcol, S):
    C = CHUNK
    ii = lax.broadcasted_iota(jnp.int32, (1, C, C), 1)
    jj = lax.broadcasted_iota(jnp.int32, (1, C, C), 2)
    incl, strict = ii >= jj, ii > jj
    gc_col = jnp.sum(jnp.where(incl, 1.0, 0.0) * grow, axis=2, keepdims=True)
    gc_row = jnp.sum(jnp.where(jj >= ii, 1.0, 0.0) * gcol, axis=1, keepdims=True)
    decay = jnp.where(incl, jnp.exp(jnp.where(incl, gc_col - gc_row, 0.0)), 0.0)
    qs = q * (DH ** -0.5)
    kb = k * bcol
    T = _unit_lower_inv(-jnp.where(strict, _bdot(kb, k, 2, 2) * decay, 0.0))
    egc = jnp.exp(gc_col)
    u = _bmm3(T, v * bcol)
    w = _bmm3(T, kb * egc)
    att = jnp.where(incl, _bdot(qs, k, 2, 2) * decay, 0.0)
    v_new = u - _bdot(w, S, 2, 1)
    o = _bdot(qs * egc, S, 2, 1) + _bdot(att, v_new, 2, 1)
    g_last = jnp.sum(grow, axis=2, keepdims=True)
    k_dec = k * jnp.exp(g_last - gc_col)
    S_out = S * jnp.exp(g_last) + _bdot(k_dec, v_new, 1, 1)
    return o, S_out


def _heads(ref, w):
    return jnp.stack([ref[:, h * w:(h + 1) * w] for h in range(NH)])


def _gdn_specs(NC, rev):
    ix = (lambda i: NC - 1 - i) if rev else (lambda i: i)
    wide = pl.BlockSpec((CHUNK, D), lambda i: (ix(i), 0))
    col = pl.BlockSpec((CHUNK, NH), lambda i: (ix(i), 0))
    row = pl.BlockSpec((1, NH, CHUNK), lambda i: (ix(i), 0, 0))
    st = pl.BlockSpec((1, NH, DH, DH), lambda i: (ix(i), 0, 0, 0))
    return wide, col, row, st


def _gdn_fwd(name, q, k, v, gcol, grow, bcol):
    S = q.shape[0]
    NC = S // CHUNK

    def body(q_ref, k_ref, v_ref, gc_ref, gr_ref, b_ref, o_ref, ss_ref, st):
        @pl.when(pl.program_id(0) == 0)
        def _():
            st[...] = jnp.zeros(st.shape, F32)
        s_in = st[...]
        ss_ref[0] = s_in
        grow = jnp.stack([gr_ref[0, h:h + 1, :] for h in range(NH)])
        o, s_out = _gdn_chunk(_heads(q_ref, DH), _heads(k_ref, DH), _heads(v_ref, DH), _heads(gc_ref, 1), grow,
                              _heads(b_ref, 1), s_in)
        for h in range(NH):
            o_ref[:, h * DH:(h + 1) * DH] = o[h]
        st[...] = s_out

    wide, col, row, stsp = _gdn_specs(NC, False)
    return pl.pallas_call(body, name=name, grid=(NC,), in_specs=[wide, wide, wide, col, row, col],
                          out_specs=[wide, stsp],
                          out_shape=[jax.ShapeDtypeStruct((S, D), F32), jax.ShapeDtypeStruct((NC, NH, DH, DH), F32)],
                          scratch_shapes=[pltpu.VMEM((NH, DH, DH), F32)],
                          compiler_params=_cparams(("arbitrary",)))(q, k, v, gcol, grow, bcol)


def _gdn_bwd(name, q, k, v, gcol, grow, bcol, ssave, do):
    S = q.shape[0]
    NC = S // CHUNK

    def body(q_ref, k_ref, v_ref, gc_ref, gr_ref, b_ref, ss_ref, do_ref, dq_ref, dk_ref, dv_ref, dgc_ref, dgr_ref, db_ref, dst):
        @pl.when(pl.program_id(0) == 0)
        def _():
            dst[...] = jnp.zeros(dst.shape, F32)
        grow = jnp.stack([gr_ref[0, h:h + 1, :] for h in range(NH)])
        prim = (_heads(q_ref, DH), _heads(k_ref, DH), _heads(v_ref, DH), _heads(gc_ref, 1), grow, _heads(b_ref, 1), ss_ref[0])
        _, vjp = jax.vjp(_gdn_chunk, *prim)
        dq, dk, dv, dgc, dgr, db, ds = vjp((_heads(do_ref, DH), dst[...]))
        for h in range(NH):
            hs = slice(h * DH, (h + 1) * DH)
            dq_ref[:, hs] = dq[h]
            dk_ref[:, hs] = dk[h]
            dv_ref[:, hs] = dv[h]
            dgc_ref[:, h:h + 1] = dgc[h]
            dgr_ref[0, h:h + 1, :] = dgr[h]
            db_ref[:, h:h + 1] = db[h]
        dst[...] = ds

    wide, col, row, stsp = _gdn_specs(NC, True)
    return pl.pallas_call(body, name=name, grid=(NC,), in_specs=[wide, wide, wide, col, row, col, stsp, wide],
                          out_specs=[wide, wide, wide, col, row, col],
                          out_shape=[jax.ShapeDtypeStruct((S, D), F32)] * 3 + [jax.ShapeDtypeStruct((S, NH), F32),
                                                                                 jax.ShapeDtypeStruct((NC, NH, CHUNK), F32),
                                                                                 jax.ShapeDtypeStruct((S, NH), F32)],
                          scratch_shapes=[pltpu.VMEM((NH, DH, DH), F32)],
                          compiler_params=_cparams(("arbitrary",)))(q, k, v, gcol, grow, bcol, ssave, do)


TQ = 512
SM_SCALE = QKH ** -0.5
NEG = -1e30


def _diag_mask(transposed):
    r = lax.broadcasted_iota(jnp.int32, (TQ, TQ), 0) // CHUNK
    c = lax.broadcasted_iota(jnp.int32, (TQ, TQ), 1) // CHUNK
    return (r <= c) if transposed else (c <= r)


def _dot_nt(a, b):
    return lax.dot_general(a, b, (((1,), (1,)), ((), ())), preferred_element_type=F32)


def _flash_fwd(name, qp, kp, kv):
    S = qp.shape[0]
    nq = S // TQ

    def body(q_ref, k_ref, v_ref, o_ref, lse_ref):
        qi = pl.program_id(1)
        q = q_ref[...]

        def step(j, carry, masked):
            m, l, acc = carry
            rows = pl.ds(pl.multiple_of(j * TQ, TQ), TQ)
            s = _dot_nt(q, k_ref[rows, :]) * SM_SCALE
            if masked:
                s = jnp.where(_diag_mask(False), s, NEG)
            m_new = jnp.maximum(m, jnp.max(s, axis=-1, keepdims=True))
            p = jnp.exp(s - m_new)
            alpha = jnp.exp(m - m_new)
            l = alpha * l + jnp.sum(p, axis=-1, keepdims=True)
            acc = alpha * acc + jnp.dot(p.astype(BF16), v_ref[rows, :].astype(BF16), preferred_element_type=F32)
            return m_new, l, acc

        carry = (jnp.full((TQ, 1), NEG, F32), jnp.zeros((TQ, 1), F32), jnp.zeros((TQ, DH), F32))
        carry = lax.fori_loop(0, qi, lambda j, c: step(j, c, False), carry)
        m, l, acc = step(qi, carry, True)
        o_ref[...] = acc / l
        lse_ref[0] = m + jnp.log(l)

    return pl.pallas_call(
        body, name=name, grid=(NH, nq),
        in_specs=[pl.BlockSpec((TQ, HP), lambda h, i: (i, h)), pl.BlockSpec((S, HP), lambda h, i: (0, h)),
                  pl.BlockSpec((S, DH), lambda h, i: (0, NH + h))],
        out_specs=[pl.BlockSpec((TQ, DH), lambda h, i: (i, h)), pl.BlockSpec((1, TQ, 1), lambda h, i: (h, i, 0))],
        out_shape=[jax.ShapeDtypeStruct((S, NH * DH), F32), jax.ShapeDtypeStruct((NH, S, 1), F32)],
        compiler_params=_cparams(("parallel", "arbitrary")))(qp, kp, kv)


def _flash_bwd_dq(name, qp, kp, kv, o, do, lse):
    S = qp.shape[0]
    nq = S // TQ

    def body(q_ref, k_ref, v_ref, o_ref, do_ref, lse_ref, dq_ref, dl_ref):
        qi = pl.program_id(1)
        q = q_ref[...]
        do = do_ref[...]
        delta = jnp.sum(o_ref[...] * do, axis=-1, keepdims=True)
        dl_ref[0] = delta
        dob = do.astype(BF16)
        lse = lse_ref[0]

        def step(j, dq, masked):
            rows = pl.ds(pl.multiple_of(j * TQ, TQ), TQ)
            k = k_ref[rows, :]
            s = _dot_nt(q, k) * SM_SCALE
            if masked:
                s = jnp.where(_diag_mask(False), s, NEG)
            p = jnp.exp(s - lse)
            dp = _dot_nt(dob, v_ref[rows, :].astype(BF16))
            ds = p * (dp - delta) * SM_SCALE
            return dq + jnp.dot(ds.astype(BF16), k, preferred_element_type=F32)

        dq = lax.fori_loop(0, qi, lambda j, c: step(j, c, False), jnp.zeros((TQ, HP), F32))
        dq_ref[...] = step(qi, dq, True)

    return pl.pallas_call(
        body, name=name, grid=(NH, nq),
        in_specs=[pl.BlockSpec((TQ, HP), lambda h, i: (i, h)), pl.BlockSpec((S, HP), lambda h, i: (0, h)),
                  pl.BlockSpec((S, DH), lambda h, i: (0, NH + h)), pl.BlockSpec((TQ, DH), lambda h, i: (i, h)),
                  pl.BlockSpec((TQ, DH), lambda h, i: (i, h)), pl.BlockSpec((1, TQ, 1), lambda h, i: (h, i, 0))],
        out_specs=[pl.BlockSpec((TQ, HP), lambda h, i: (i, h)), pl.BlockSpec((1, TQ, 1), lambda h, i: (h, i, 0))],
        out_shape=[jax.ShapeDtypeStruct((S, NH * HP), F32), jax.ShapeDtypeStruct((NH, S, 1), F32)],
        compiler_params=_cparams(("parallel", "arbitrary")))(qp, kp, kv, o, do, lse)


def _flash_bwd_dkv(name, qp, kp, kv, do, lse_row, delta_row):
    S = qp.shape[0]
    nq = S // TQ

    def body(q_ref, k_ref, v_ref, do_ref, lse_ref, dl_ref, dk_ref, dv_ref):
        kj = pl.program_id(1)
        k = k_ref[...]
        vb = v_ref[...].astype(BF16)

        def step(i, carry, masked):
            dk, dv = carry
            rows = pl.ds(pl.multiple_of(i * TQ, TQ), TQ)
            q = q_ref[rows, :]
            dob = do_ref[rows, :].astype(BF16)
            st = _dot_nt(k, q) * SM_SCALE
            pt = jnp.exp(st - lse_ref[0, :, rows])
            if masked:
                pt = jnp.where(_diag_mask(True), pt, 0.0)
            dv = dv + jnp.dot(pt.astype(BF16), dob, preferred_element_type=F32)
            dpt = _dot_nt(vb, dob)
            dst = pt * (dpt - dl_ref[0, :, rows]) * SM_SCALE
            dk = dk + jnp.dot(dst.astype(BF16), q, preferred_element_type=F32)
            return dk, dv

        carry = step(kj, (jnp.zeros((TQ, HP), F32), jnp.zeros((TQ, DH), F32)), True)
        dk, dv = lax.fori_loop(kj + 1, nq, lambda i, c: step(i, c, False), carry)
        dk_ref[...] = dk
        dv_ref[...] = dv

    return pl.pallas_call(
        body, name=name, grid=(NH, nq),
        in_specs=[pl.BlockSpec((S, HP), lambda h, j: (0, h)), pl.BlockSpec((TQ, HP), lambda h, j: (j, h)),
                  pl.BlockSpec((TQ, DH), lambda h, j: (j, NH + h)), pl.BlockSpec((S, DH), lambda h, j: (0, h)),
                  pl.BlockSpec((1, 1, S), lambda h, j: (h, 0, 0)), pl.BlockSpec((1, 1, S), lambda h, j: (h, 0, 0))],
        out_specs=[pl.BlockSpec((TQ, HP), lambda h, j: (j, h)), pl.BlockSpec((TQ, DH), lambda h, j: (j, h))],
        out_shape=[jax.ShapeDtypeStruct((S, NH * HP), F32), jax.ShapeDtypeStruct((S, NH * DH), F32)],
        compiler_params=_cparams(("parallel", "arbitrary")))(qp, kp, kv, do, lse_row, delta_row)


def _tm(S, width):
    t = 512 if width <= 1024 else (256 if width <= 3072 else 128)
    return min(t, S)


def _mod_fwd(tag, x, g, shift, scale):
    S = x.shape[0]
    return _rw_fwd(tag + "_mod", f_mod, [Row(x)], [g, shift, scale], [((D,), BF16)], _tm(S, D))[0]


def _mod_bwd(tag, x, g, shift, scale, dh, dx_direct):
    S = x.shape[0]
    r = _rw_bwd(tag + "_mod_b", f_mod, [Row(x)], [g, shift, scale], [Row(dh)], [True], [True] * 3, [((D,), F32)],
                _tm(S, D), add=Row(dx_direct))
    return r[0], r[1:]


def _res_fwd(tag, x, y, gate, coef):
    S = x.shape[0]

    def fn(pieces, bvals):
        return [pieces[0] + coef * bvals[0] * pieces[1]], []
    return _rowwise(tag + "_res", fn, [Row(x), Row(y)], [gate], [((D,), F32)], [], _tm(S, D))[0]


def _res_bwd(tag, y, gate, dxn, coef):
    S = y.shape[0]
    r = _rw_bwd(tag + "_res_b", make_f_res(coef), [Row(y)], [gate], [Row(dxn)], [True], [True], [((D,), BF16)], _tm(S, D))
    return r[0], r[1]


def _ffn_fwd(tag, x, mod3, g, w_in4, w_out4, li):
    shift, scale, gate = mod3
    S = x.shape[0]
    h = _mod_fwd(tag, x, g, shift, scale)
    gu = _matmul(tag + "_in", h, w_in4, lay="b_cols", li=li, out_dtype=BF16)
    a = _rw_fwd(tag + "_act", f_act, [Row(gu, splits=[FF, FF])], [], [((FF,), BF16)], _tm(S, 2 * FF))[0]
    y = _matmul(tag + "_out", a, w_out4, lay="b_rows", li=li)
    xn = _res_fwd(tag, x, y, gate, 0.5)
    return xn, (x, h, gu, a, y)


def _ffn_bwd(tag, dxn, res, mod3, g, w_in4, w_out4, li, g_in4, g_out4):
    shift, scale, gate = mod3
    x, h, gu, a, y = res
    S = x.shape[0]
    nmat = w_in4.shape[1]
    dy, dgate = _res_bwd(tag, y, gate, dxn, 0.5)
    da = _matmul(tag + "_out_bi", dy, w_out4, "nt", lay="b_rows", li=li, out_dtype=BF16)
    g_out4 = _matmul(tag + "_out_bw", a, dy, "tn", lay="o_rows", li=li, into=g_out4, nmat=nmat)
    dgu = _rw_bwd(tag + "_act_b", f_act, [Row(gu, splits=[FF, FF])], [], [Row(da)], [True, True], [],
                  [((FF, FF), BF16)], _tm(S, 2 * FF))[0]
    dh = _matmul(tag + "_in_bi", dgu, w_in4, "nt", lay="b_cols", li=li)
    g_in4 = _matmul(tag + "_in_bw", h, dgu, "tn", lay="o_cols", li=li, into=g_in4, nmat=nmat)
    dx, (dg, dshift, dscale) = _mod_bwd(tag, x, g, shift, scale, dh, dxn)
    return dx, g_in4, g_out4, dict(g=dg, mod=(dshift, dscale, dgate))


def _pad_lanes(a, lo, width=LANE):
    return jnp.pad(a, ((0, 0), (lo, width - lo - a.shape[1])))


def _gdn_layer_fwd(tag, x, mod3, g, p):
    shift, scale, gate = mod3
    S = x.shape[0]
    NC = S // CHUNK
    h = _mod_fwd(tag, x, g, shift, scale)
    proj = _matmul(tag + "_in", h, p["w_in"])
    qc = _conv_fwd(tag + "_conv", proj, p["conv_w8"], 3 * D, _tm(S, 3 * D))
    q, k, v = _rw_fwd(tag + "_pre", f_gdnpre, [Row(qc, splits=[DH] * (3 * NH))], [],
                      [((DH,) * NH, F32)] * 3, _tm(S, 3 * D))
    betaf, gf = _rw_fwd(tag + "_gates", f_gates, [Row(proj, LANE, cb=GATE_CB)], [p["a_log128"], p["dt_bias128"]],
                        [((LANE,), F32)] * 2, _tm(S, LANE))
    bcol, gcol = betaf[:, :NH], gf[:, NH:2 * NH]
    grow = gcol.reshape(NC, CHUNK, NH).transpose(0, 2, 1)
    o, ssave = _gdn_fwd(tag + "_core", q, k, v, gcol, grow, bcol)
    on = _rw_fwd(tag + "_post", f_gdnpost, [Row(o, splits=[DH] * NH), Row(proj, D, cb=3, splits=[DH] * NH)],
                 [p["norm_g"]], [((DH,) * NH, BF16)], _tm(S, 2 * D))[0]
    y = _matmul(tag + "_out", on, p["w_out"])
    xn = _res_fwd(tag, x, y, gate, 1.0)
    return xn, (x, h, proj, qc, q, k, v, gcol, grow, bcol, ssave, o, on, y)


def _gdn_layer_bwd(tag, dxn, res, mod3, g, p):
    shift, scale, gate = mod3
    x, h, proj, qc, q, k, v, gcol, grow, bcol, ssave, o, on, y = res
    S = x.shape[0]
    dy, dgate = _res_bwd(tag, y, gate, dxn, 1.0)
    don = _matmul(tag + "_out_bi", dy, p["w_out"], "nt")
    dw_out = _matmul(tag + "_out_bw", on, dy, "tn")
    do, dz, dnorm = _rw_bwd(tag + "_post_b", f_gdnpost, [Row(o, splits=[DH] * NH), Row(proj, D, cb=3, splits=[DH] * NH)],
                            [p["norm_g"]], [Row(don, splits=[DH] * NH)], [True] * (2 * NH), [True],
                            [((DH,) * NH, F32), ((DH,) * NH, BF16)], _tm(S, 2 * D))
    dq, dk, dv, dgc, dgr, db = _gdn_bwd(tag + "_core_b", q, k, v, gcol, grow, bcol, ssave, do)
    dgcol = dgc + dgr.transpose(0, 2, 1).reshape(S, NH)
    dgates, da_log, ddt = _rw_bwd(tag + "_gates_b", f_gates, [Row(proj, LANE, cb=GATE_CB)], [p["a_log128"], p["dt_bias128"]],
                                  [Row(_pad_lanes(db, 0)), Row(_pad_lanes(dgcol, NH))], [True], [True, True],
                                  [((LANE,), BF16)], _tm(S, LANE))
    dqc = _rw_bwd(tag + "_pre_b", f_gdnpre, [Row(qc, splits=[DH] * (3 * NH))], [],
                  [Row(dq, splits=[DH] * NH), Row(dk, splits=[DH] * NH), Row(dv, splits=[DH] * NH)],
                  [True] * (3 * NH), [], [((DH,) * (3 * NH), F32)], _tm(S, 3 * D))[0]
    dqkv, dconv = _conv_bwd(tag + "_conv_b", proj, dqc, p["conv_w8"], 3 * D, _tm(S, 3 * D), BF16)
    dproj = jnp.concatenate([dqkv, dz, dgates], axis=1)
    dh = _matmul(tag + "_in_bi", dproj, p["w_in"], "nt")
    dw_in = _matmul(tag + "_in_bw", h, dproj, "tn")
    dx, (dg, dshift, dscale) = _mod_bwd(tag, x, g, shift, scale, dh, dxn)
    return dx, dict(w_in=dw_in, conv_w8=dconv, a_log128=da_log, dt_bias128=ddt, norm_g=dnorm,
                    w_out=dw_out, g=dg, mod=(dshift, dscale, dgate))


def _qk_rows(src, shared_rope, ckv=None):
    if shared_rope:
        return [Row(src, D, cb=0, splits=[DH] * NH), Row(ckv, LANE, cb=2)]
    return [Row(src, splits=[DH] * (2 * NH))]


def _kv_fwd(x, kvmod, p, tabs):
    shift, scale = kvmod
    S = x.shape[0]
    h = _mod_fwd("kv", x, p["kv_norm_g"], shift, scale)
    ckv = _matmul("kv_dkv", h, p["w_dkv"])
    lat = _rw_fwd("kv_lat", f_rms, [Row(ckv, KVL)], [p["kv_lat_g"]], [((KVL,), BF16)], _tm(S, KVL))[0]
    kvf = _matmul("kv_ukv", lat, p["w_ukv"])
    kp = _rw_fwd("kv_k", make_f_qk(True), _qk_rows(kvf, True, ckv) + [Row(tabs[0]), Row(tabs[1])],
                 [p["k_gn"], p["k_gr"], p["pm"]], [((DH,) * (2 * NH), BF16)], _tm(S, 2 * D))[0]
    return kp, kvf, (x, h, ckv, lat)


def _kv_bwd(dkp, dv, dx_direct, res, kvmod, kvf, p, tabs):
    shift, scale = kvmod
    x, h, ckv, lat = res
    S = x.shape[0]
    dkn, dkr, dgn, dgr = _rw_bwd("kv_k_b", make_f_qk(True), _qk_rows(kvf, True, ckv) + [Row(tabs[0]), Row(tabs[1])],
                                 [p["k_gn"], p["k_gr"], p["pm"]], [Row(dkp, splits=[DH] * (2 * NH))],
                                 [True] * (NH + 1) + [False, False], [True, True, False],
                                 [((DH,) * NH, BF16), ((LANE,), BF16)], _tm(S, 2 * D))
    dkvf = jnp.concatenate([dkn, dv.astype(BF16)], axis=1)
    dlat = _matmul("kv_ukv_bi", dkvf, p["w_ukv"], "nt")
    dw_ukv = _matmul("kv_ukv_bw", lat, dkvf, "tn")
    dcl, dlg = _rw_bwd("kv_lat_b", f_rms, [Row(ckv, KVL)], [p["kv_lat_g"]], [Row(dlat)], [True], [True],
                       [((KVL,), BF16)], _tm(S, KVL))
    dckv = jnp.concatenate([dcl, dkr], axis=1)
    dh = _matmul("kv_dkv_bi", dckv, p["w_dkv"], "nt")
    dw_dkv = _matmul("kv_dkv_bw", h, dckv, "tn")
    dx, (dg, dshift, dscale) = _mod_bwd("kv", x, p["kv_norm_g"], shift, scale, dh, dx_direct)
    return dx, dict(w_dkv=dw_dkv, w_ukv=dw_ukv, kv_lat_g=dlg, k_gn=dgn, k_gr=dgr, kv_norm_g=dg, mod=(dshift, dscale))


def _mla_layer_fwd(tag, x, mod3, g, p, kp, kvf, tabs):
    shift, scale, gate = mod3
    S = x.shape[0]
    h = _mod_fwd(tag, x, g, shift, scale)
    ql = _matmul(tag + "_dq", h, p["w_dq"])
    qln = _rw_fwd(tag + "_qln", f_rms, [Row(ql)], [p["ql_g"]], [((QL,), BF16)], _tm(S, QL))[0]
    qu = _matmul(tag + "_uq", qln, p["w_uq"])
    qp = _rw_fwd(tag + "_q", make_f_qk(False), _qk_rows(qu, False) + [Row(tabs[0]), Row(tabs[1])],
                 [p["q_gn"], p["q_gr"], p["pm"]], [((DH,) * (2 * NH), BF16)], _tm(S, 2 * D))[0]
    o, lse = _flash_fwd(tag + "_att", qp, kp, kvf)
    y = _matmul(tag + "_out", o, p["w_out"])
    xn = _res_fwd(tag, x, y, gate, 1.0)
    return xn, (x, h, ql, qln, qu, qp, o, lse, y)


def _mla_layer_bwd(tag, dxn, res, mod3, g, p, kp, kvf, tabs):
    shift, scale, gate = mod3
    x, h, ql, qln, qu, qp, o, lse, y = res
    S = x.shape[0]
    dy, dgate = _res_bwd(tag, y, gate, dxn, 1.0)
    do = _matmul(tag + "_out_bi", dy, p["w_out"], "nt")
    dw_out = _matmul(tag + "_out_bw", o, dy, "tn")
    dqp, delta = _flash_bwd_dq(tag + "_att_bq", qp, kp, kvf, o, do, lse)
    dkp, dv = _flash_bwd_dkv(tag + "_att_bkv", qp, kp, kvf, do, lse.reshape(NH, 1, S), delta.reshape(NH, 1, S))
    dqu, dgn, dgr = _rw_bwd(tag + "_q_b", make_f_qk(False), _qk_rows(qu, False) + [Row(tabs[0]), Row(tabs[1])],
                            [p["q_gn"], p["q_gr"], p["pm"]], [Row(dqp, splits=[DH] * (2 * NH))],
                            [True] * (2 * NH) + [False, False], [True, True, False],
                            [((DH,) * (2 * NH), BF16)], _tm(S, 2 * D))
    dqln = _matmul(tag + "_uq_bi", dqu, p["w_uq"], "nt")
    dw_uq = _matmul(tag + "_uq_bw", qln, dqu, "tn")
    dql, dqlg = _rw_bwd(tag + "_qln_b", f_rms, [Row(ql)], [p["ql_g"]], [Row(dqln)], [True], [True], [((QL,), BF16)],
                        _tm(S, QL))
    dh = _matmul(tag + "_dq_bi", dql, p["w_dq"], "nt")
    dw_dq = _matmul(tag + "_dq_bw", h, dql, "tn")
    dx, (dg, dshift, dscale) = _mod_bwd(tag, x, g, shift, scale, dh, dxn)
    return dx, dkp, dv, dict(w_dq=dw_dq, w_uq=dw_uq, w_out=dw_out, ql_g=dqlg, q_gn=dgn, q_gr=dgr, g=dg,
                             mod=(dshift, dscale, dgate))


def _loss_head(y, tgt):
    S = y.shape[0]

    def fn(pieces, bvals):
        e = pieces[0] - pieces[1]
        part = jnp.sum(e * e) * (0.5 / D)
        return [e * (1.0 / D)], [jnp.full((1, LANE), part, F32)]
    dy, part = _rowwise("loss", fn, [Row(y), Row(tgt)], [], [((D,), F32)], [(1, LANE)], _tm(S, D))
    return part[0, 0], dy


def _rope_tables(positions):
    S = positions.shape[0]
    half = ROPE // 2
    lane = lax.broadcasted_iota(jnp.int32, (1, LANE), 1)
    inv_freq = ROPE_BASE ** (-(lane % half).astype(F32) / half)
    live = (lane < ROPE).astype(F32)
    sign = jnp.where(lane < half, -1.0, 1.0) * live

    def fn(pieces, bvals):
        ang = pieces[0] * bvals[0]
        return [jnp.cos(ang) * bvals[1], jnp.sin(ang) * bvals[2]], []
    pos = jnp.broadcast_to(positions.astype(F32)[:, None], (S, LANE))
    cosp, sins = _rowwise("rope_tab", fn, [Row(pos)], [inv_freq, live, sign], [((LANE,), F32)] * 2, [], _tm(S, LANE))
    r = lax.broadcasted_iota(jnp.int32, (LANE, LANE), 0)
    c = lax.broadcasted_iota(jnp.int32, (LANE, LANE), 1)
    pm = (((c < half) & (r == c + half)) | ((c >= half) & (c < ROPE) & (r == c - half))).astype(F32)
    return (cosp, sins), pm


def _adamw(name, w, g, m, v):
    shape = w.shape
    C = shape[-1]
    R = w.size // C
    tr = R
    for t in (1024, 512, 256, 128, 64, 32, 16, 8):
        if R % t == 0 and t * C * 4 <= (1 << 21):
            tr = t
            break
    c1 = 1.0 - ADAM_B1 ** ADAM_STEP
    c2 = 1.0 - ADAM_B2 ** ADAM_STEP

    def body(w_ref, g_ref, m_ref, v_ref, d_ref, mo_ref, vo_ref):
        gg = g_ref[...]
        mn = ADAM_B1 * m_ref[...] + (1.0 - ADAM_B1) * gg
        vn = ADAM_B2 * v_ref[...] + (1.0 - ADAM_B2) * (gg * gg)
        d_ref[...] = -ADAM_LR * ((mn / c1) / (jnp.sqrt(vn / c2) + ADAM_EPS) + ADAM_WD * w_ref[...])
        mo_ref[...] = mn
        vo_ref[...] = vn

    spec = pl.BlockSpec((tr, C), lambda i: (i, 0))
    outs = pl.pallas_call(body, name=name, grid=(R // tr,), in_specs=[spec] * 4, out_specs=[spec] * 3,
                          out_shape=[jax.ShapeDtypeStruct((R, C), F32)] * 3,
                          compiler_params=_cparams(("parallel",)))(*[t.reshape(R, C) for t in (w, g, m, v)])
    return [o.reshape(shape) for o in outs]


HBM_SPEC = pl.BlockSpec(memory_space=pltpu.HBM)
OTHER_CHIPS = (4, 2, 6)
SIBLING = 1


def _me():
    return lax.axis_index("x"), lax.axis_index("y"), lax.axis_index("c")


def _peer(me, k):
    mx, my, mc = me
    return ((1 - mx) if k & 4 else mx, (1 - my) if k & 2 else my, (1 - mc) if k & 1 else mc)


def _rcopy(src, dst, ssem, rsem, to):
    return pltpu.make_async_remote_copy(src_ref=src, dst_ref=dst, send_sem=ssem, recv_sem=rsem, device_id=to,
                                        device_id_type=MESH)


def _all_gather8(name, x):
    def body(x_ref, o_ref, ssem, rsem, lsem):
        me = _me()
        mine = 4 * me[0] + 2 * me[1] + me[2]
        loc = pltpu.make_async_copy(x_ref, o_ref.at[mine], lsem)
        loc.start()
        sends = []
        for k in range(1, 8):
            cp = _rcopy(x_ref, o_ref.at[mine], ssem.at[k - 1], rsem.at[k - 1], _peer(me, k))
            cp.start()
            sends.append(cp)
        for k in range(1, 8):
            px, py, pc = _peer(me, k)
            _rcopy(x_ref, o_ref.at[4 * px + 2 * py + pc], ssem.at[k - 1], rsem.at[k - 1], (px, py, pc)).wait_recv()
        for cp in sends:
            cp.wait_send()
        loc.wait()

    return pl.pallas_call(body, name=name, out_shape=jax.ShapeDtypeStruct((8,) + x.shape, x.dtype),
                          in_specs=[HBM_SPEC], out_specs=HBM_SPEC,
                          scratch_shapes=[pltpu.SemaphoreType.DMA((7,)), pltpu.SemaphoreType.DMA((7,)),
                                          pltpu.SemaphoreType.DMA(())])(x)


PACK_L = 1024
PACK_RT = 256


def _place_shard(name, wp, chip):
    rh, ln = wp.shape[1:]

    def body(s_ref, w_ref, o_ref):
        o_ref[...] = w_ref[...]

    gs = pltpu.PrefetchScalarGridSpec(
        num_scalar_prefetch=1, grid=(2, rh // PACK_RT),
        in_specs=[pl.BlockSpec((None, PACK_RT, ln), lambda h, i, s_ref: (h, i, 0))],
        out_specs=pl.BlockSpec((None, None, PACK_RT, ln), lambda h, i, s_ref: (s_ref[0], h, i, 0)))
    return pl.pallas_call(body, name=name, grid_spec=gs, out_shape=jax.ShapeDtypeStruct((4,) + wp.shape, wp.dtype),
                          compiler_params=_cparams(("parallel", "parallel")))(chip.reshape(1).astype(jnp.int32), wp)


def _gather_weights(name, w4):
    r2 = w4.shape[2] // 2

    def body(w_ref, o_ref, ssem, rsem):
        me = _me()
        mc = me[2]
        px, py, pd, sib = _peer(me, 4), _peer(me, 2), _peer(me, 6), _peer(me, SIBLING)
        chip = lambda p: 2 * p[0] + p[1]
        mine, from_x, from_y, from_d = (o_ref.at[chip(p), mc] for p in (me, px, py, pd))
        q0, q1 = pl.ds(0, r2), pl.ds(r2, r2)
        sends = [_rcopy(mine, mine, ssem.at[0], rsem.at[0], px), _rcopy(mine, mine, ssem.at[1], rsem.at[1], py)]
        for cp in sends:
            cp.start()
        _rcopy(from_x, from_x, ssem.at[0], rsem.at[0], px).wait_recv()
        sends += [_rcopy(from_x.at[q1], from_x.at[q1], ssem.at[2], rsem.at[2], py),
                  _rcopy(from_x, from_x, ssem.at[4], rsem.at[4], sib)]
        sends[-2].start()
        sends[-1].start()
        _rcopy(from_y, from_y, ssem.at[1], rsem.at[1], py).wait_recv()
        sends += [_rcopy(from_y.at[q0], from_y.at[q0], ssem.at[3], rsem.at[3], px),
                  _rcopy(from_y, from_y, ssem.at[5], rsem.at[5], sib)]
        sends[-2].start()
        sends[-1].start()
        _rcopy(from_d.at[q0], from_d.at[q0], ssem.at[3], rsem.at[3], px).wait_recv()
        _rcopy(from_d.at[q1], from_d.at[q1], ssem.at[2], rsem.at[2], py).wait_recv()
        sends.append(_rcopy(from_d, from_d, ssem.at[6], rsem.at[6], sib))
        sends[-1].start()
        for j, p in enumerate((px, py, pd)):
            land = o_ref.at[chip(p), 1 - mc]
            _rcopy(land, land, ssem.at[4 + j], rsem.at[4 + j], sib).wait_recv()
        for cp in sends:
            cp.wait_send()

    return pl.pallas_call(body, name=name, out_shape=jax.ShapeDtypeStruct(w4.shape, w4.dtype),
                          in_specs=[HBM_SPEC], out_specs=HBM_SPEC, input_output_aliases={0: 0},
                          scratch_shapes=[pltpu.SemaphoreType.DMA((7,)), pltpu.SemaphoreType.DMA((7,))])(w4)


def _exchange_half(name, g):
    def body(g_ref, p_ref, ssem, rsem):
        me = _me()
        cps = []
        for s in range(4):
            cp = _rcopy(g_ref.at[s, 1 - me[2]], p_ref.at[s], ssem.at[s], rsem.at[s], _peer(me, SIBLING))
            cp.start()
            cps.append(cp)
        for cp in cps:
            cp.wait()

    return pl.pallas_call(body, name=name, out_shape=jax.ShapeDtypeStruct((4,) + g.shape[2:], g.dtype),
                          in_specs=[HBM_SPEC], out_specs=HBM_SPEC,
                          scratch_shapes=[pltpu.SemaphoreType.DMA((4,)), pltpu.SemaphoreType.DMA((4,))])(g)


def _scatter_chips(name, q):
    r2 = q.shape[1] // 2

    def body(q_ref, t_ref, relay, ssem, rsem):
        me = _me()
        px, py, pd = _peer(me, 4), _peer(me, 2), _peer(me, 6)
        chip = lambda p: 2 * p[0] + p[1]
        q0, q1 = pl.ds(0, r2), pl.ds(r2, r2)
        sends = [_rcopy(q_ref.at[chip(px)], t_ref.at[0], ssem.at[0], rsem.at[0], px),
                 _rcopy(q_ref.at[chip(py)], t_ref.at[1], ssem.at[1], rsem.at[1], py),
                 _rcopy(q_ref.at[chip(pd), q0], relay.at[0], ssem.at[2], rsem.at[2], py),
                 _rcopy(q_ref.at[chip(pd), q1], relay.at[1], ssem.at[3], rsem.at[3], px)]
        for cp in sends:
            cp.start()
        _rcopy(relay.at[0], relay.at[0], ssem.at[2], rsem.at[2], py).wait_recv()
        sends.append(_rcopy(relay.at[0], t_ref.at[2, q0], ssem.at[4], rsem.at[4], px))
        sends[-1].start()
        _rcopy(relay.at[1], relay.at[1], ssem.at[3], rsem.at[3], px).wait_recv()
        sends.append(_rcopy(relay.at[1], t_ref.at[2, q1], ssem.at[5], rsem.at[5], py))
        sends[-1].start()
        _rcopy(t_ref.at[0], t_ref.at[0], ssem.at[0], rsem.at[0], px).wait_recv()
        _rcopy(t_ref.at[1], t_ref.at[1], ssem.at[1], rsem.at[1], py).wait_recv()
        _rcopy(t_ref.at[2, q0], t_ref.at[2, q0], ssem.at[4], rsem.at[4], px).wait_recv()
        _rcopy(t_ref.at[2, q1], t_ref.at[2, q1], ssem.at[5], rsem.at[5], py).wait_recv()
        for cp in sends:
            cp.wait_send()

    return pl.pallas_call(body, name=name,
                          out_shape=[jax.ShapeDtypeStruct((3,) + q.shape[1:], q.dtype),
                                     jax.ShapeDtypeStruct((2, r2) + q.shape[2:], q.dtype)],
                          in_specs=[HBM_SPEC], out_specs=[HBM_SPEC, HBM_SPEC],
                          scratch_shapes=[pltpu.SemaphoreType.DMA((6,)), pltpu.SemaphoreType.DMA((6,))])(q)[0]


def _exchange_full(name, r2):
    def body(r_ref, o_ref, ssem, rsem):
        me = _me()
        mc = me[2]
        cp = _rcopy(o_ref.at[mc], o_ref.at[mc], ssem, rsem, _peer(me, SIBLING))
        cp.start()
        _rcopy(o_ref.at[1 - mc], o_ref.at[1 - mc], ssem, rsem, _peer(me, SIBLING)).wait_recv()
        cp.wait_send()

    return pl.pallas_call(body, name=name, out_shape=jax.ShapeDtypeStruct(r2.shape, r2.dtype),
                          in_specs=[HBM_SPEC], out_specs=HBM_SPEC, input_output_aliases={0: 0},
                          scratch_shapes=[pltpu.SemaphoreType.DMA(()), pltpu.SemaphoreType.DMA(())])(r2)


def _add_half(name, g, p, c):
    rh, ln = g.shape[2:]

    def body(c_ref, g_ref, p_ref, o_ref):
        o_ref[0] = (g_ref[0, 0] + p_ref[0]).astype(o_ref.dtype)

    gs = pltpu.PrefetchScalarGridSpec(
        num_scalar_prefetch=1, grid=(4, rh // PACK_RT),
        in_specs=[pl.BlockSpec((1, 1, PACK_RT, ln), lambda s, i, c_ref: (s, c_ref[0], i, 0)),
                  pl.BlockSpec((1, PACK_RT, ln), lambda s, i, c_ref: (s, i, 0))],
        out_specs=pl.BlockSpec((1, PACK_RT, ln), lambda s, i, c_ref: (s, i, 0)))
    return pl.pallas_call(body, name=name, grid_spec=gs, out_shape=jax.ShapeDtypeStruct((4, rh, ln), BF16),
                          compiler_params=_cparams(("parallel", "parallel")))(c.reshape(1).astype(jnp.int32), g, p)


def _add_chips(name, q, t, chip, c):
    rh, ln = q.shape[1:]

    def body(s_ref, c_ref, q_ref, t_ref, o_ref):
        o_ref[...] = ((q_ref[0].astype(F32) + t_ref[0].astype(F32)) + t_ref[1].astype(F32)) + t_ref[2].astype(F32)

    gs = pltpu.PrefetchScalarGridSpec(
        num_scalar_prefetch=2, grid=(rh // PACK_RT,),
        in_specs=[pl.BlockSpec((1, PACK_RT, ln), lambda i, s_ref, c_ref: (s_ref[0], i, 0)),
                  pl.BlockSpec((3, PACK_RT, ln), lambda i, s_ref, c_ref: (0, i, 0))],
        out_specs=pl.BlockSpec((None, PACK_RT, ln), lambda i, s_ref, c_ref: (c_ref[0], i, 0)))
    return pl.pallas_call(body, name=name, grid_spec=gs, out_shape=jax.ShapeDtypeStruct((2, rh, ln), F32),
                          compiler_params=_cparams(("parallel",)))(chip.reshape(1).astype(jnp.int32),
                                                                    c.reshape(1).astype(jnp.int32), q, t)


def _sum8(name, a):
    def body(a_ref, o_ref):
        acc = a_ref[0]
        for d in range(1, 8):
            acc = acc + a_ref[d]
        o_ref[...] = acc
    return pl.pallas_call(body, name=name, out_shape=jax.ShapeDtypeStruct(a.shape[1:], F32))(a)


def _silu_rows(name, a):
    def body(a_ref, o_ref):
        o_ref[...] = _silu(a_ref[...])
    return pl.pallas_call(body, name=name, out_shape=jax.ShapeDtypeStruct(a.shape, F32))(a)


REST = (("gdn_w_out", 1), ("mla_w_dkv", 0), ("mla_w_ukv", 1), ("mla_w_dq", 1), ("mla_w_uq", 2), ("mla_w_out", 1))


def _packed_rows(n):
    per_half = -(-n // (2 * PACK_L))
    return -(-per_half // PACK_RT) * PACK_RT


def _pack_flat(flat):
    n = flat.shape[-1]
    rh = _packed_rows(n)
    pad = [(0, 0)] * (flat.ndim - 1) + [(0, 2 * rh * PACK_L - n)]
    return jnp.pad(flat, pad).reshape(flat.shape[:-1] + (2, rh, PACK_L))


def _shards_first(full, axis):
    sh = full.shape
    t = full.reshape(sh[:axis] + (4, sh[axis] // 4) + sh[axis + 1:])
    return jnp.moveaxis(t, axis, 0)


def _shards_merge(stacked, axis):
    t = jnp.moveaxis(stacked, 0, axis)
    sh = t.shape
    return t.reshape(sh[:axis] + (4 * sh[axis + 1],) + sh[axis + 2:])


def _pack_small(parts):
    flat = jnp.concatenate([p.reshape(-1).astype(F32) for p in parts])
    n = flat.shape[0]
    rows = -(-n // (SUB * LANE)) * SUB
    return jnp.pad(flat, (0, rows * LANE - n)).reshape(rows, LANE)


def _unpack_small(buf, shapes):
    lead = buf.shape[:-2]
    flat = buf.reshape(lead + (-1,))
    out, off = [], 0
    for sh in shapes:
        n = 1
        for d in sh:
            n *= d
        out.append(flat[..., off:off + n].reshape(lead + tuple(sh)))
        off += n
    return out


WEIGHTS = ('ada_w', 'ada_b', 'norm_g', 'ffn_w_in', 'ffn_w_out', 'gdn_w_in', 'gdn_conv_w', 'gdn_a_log', 'gdn_dt_bias',
           'gdn_norm_g', 'gdn_w_out', 'kv_ada_w', 'kv_ada_b', 'kv_norm_g', 'mla_w_dkv', 'mla_kv_norm_g', 'mla_w_ukv',
           'mla_k_norm_g', 'mla_w_dq', 'mla_q_lora_norm_g', 'mla_w_uq', 'mla_q_norm_g', 'mla_w_out')
ARGS = ('x', 'c', 'positions') + WEIGHTS + ('loss_target',) + tuple('m_' + n for n in WEIGHTS) + tuple('v_' + n for n in WEIGHTS)


def _split_norm(v):
    return v[None, :DH], _pad_lanes(v[None, DH:], 0)


def _join_norm(gn, gr):
    return jnp.concatenate([gn[0], gr[0, :ROPE]])


def _step(x, tgt, pos, mods, kvmod, W, P):
    tabs, pm = _rope_tables(pos)
    m3 = lambda l, i: tuple(mods[l][3 * i + j][None] for j in range(3))
    ng = lambda l, i: P["norm_g"][l, i][None]
    gdn_p, mla_p = [], []
    for l in range(2):
        gdn_p.append(dict(w_in=jnp.pad(W["gdn_w_in"][l], ((0, 0), (0, GDN_IN - W["gdn_w_in"].shape[2]))),
                          conv_w8=jnp.pad(P["gdn_conv_w"][l], ((0, 4), (0, 0))),
                          a_log128=_pad_lanes(P["gdn_a_log"][l][None], NH), dt_bias128=_pad_lanes(P["gdn_dt_bias"][l][None], NH),
                          norm_g=P["gdn_norm_g"][l][None], w_out=W["gdn_w_out"][l]))
        q_gn, q_gr = _split_norm(P["mla_q_norm_g"][l])
        mla_p.append(dict(w_dq=W["mla_w_dq"][l], ql_g=P["mla_q_lora_norm_g"][l][None],
                          w_uq=jnp.pad(W["mla_w_uq"][l].reshape(QL, NH, QKH), ((0, 0), (0, 0), (0, HP - QKH))).reshape(QL, NH * HP),
                          q_gn=q_gn, q_gr=q_gr, pm=pm, w_out=W["mla_w_out"][l]))
    k_gn, k_gr = _split_norm(P["mla_k_norm_g"])
    kv_p = dict(kv_norm_g=P["kv_norm_g"][None], w_dkv=jnp.pad(W["mla_w_dkv"], ((0, 0), (0, QL - KVL - ROPE))),
                kv_lat_g=P["mla_kv_norm_g"][None],
                w_ukv=W["mla_w_ukv"].reshape(KVL, NH, 2, DH).transpose(0, 2, 1, 3).reshape(KVL, 2 * NH * DH),
                k_gn=k_gn, k_gr=k_gr, pm=pm)
    kvm = (kvmod[0][None], kvmod[1][None])

    res = {}
    for l in range(4):
        x, res[l, 0] = _ffn_fwd(f"l{l}a", x, m3(l, 0), ng(l, 0), W["ffn_w_in"], W["ffn_w_out"], 2 * l)
        if l < 2:
            x, res[l, 1] = _gdn_layer_fwd(f"l{l}g", x, m3(l, 1), ng(l, 1), gdn_p[l])
        else:
            x, res[l, 1] = _mla_layer_fwd(f"l{l}m", x, m3(l, 1), ng(l, 1), mla_p[l - 2], kp, kvf, tabs)
        x, res[l, 2] = _ffn_fwd(f"l{l}b", x, m3(l, 2), ng(l, 2), W["ffn_w_in"], W["ffn_w_out"], 2 * l + 1)
        if l == 1:
            kp, kvf, kres = _kv_fwd(x, kvm, kv_p, tabs)
    loss, dx = _loss_head(x, tgt)

    gw = {n: [None] * W[n].shape[0] for n in ("gdn_w_in", "gdn_w_out", "mla_w_dq", "mla_w_uq", "mla_w_out")}
    g_in4 = g_out4 = None
    gp = {n: [None] * 2 for n in ("gdn_conv_w", "gdn_a_log", "gdn_dt_bias", "gdn_norm_g", "mla_q_lora_norm_g", "mla_q_norm_g")}
    gnorm = [[None] * 3 for _ in range(4)]
    dmod = [[None] * NMOD for _ in range(4)]
    dkp = dv = None
    for l in (3, 2, 1, 0):
        if l == 1:
            dx, gk = _kv_bwd(dkp, dv, dx, kres, kvm, kvf, kv_p, tabs)
        for i in (2, 1, 0):
            if i != 1:
                dx, g_in4, g_out4, gd = _ffn_bwd(f"l{l}{'ab'[i // 2]}", dx, res[l, i], m3(l, i), ng(l, i), W["ffn_w_in"],
                                                 W["ffn_w_out"], 2 * l + i // 2, g_in4, g_out4)
            elif l < 2:
                dx, gd = _gdn_layer_bwd(f"l{l}g", dx, res[l, 1], m3(l, 1), ng(l, 1), gdn_p[l])
                gw["gdn_w_in"][l] = gd["w_in"][:, :W["gdn_w_in"].shape[2]]
                gw["gdn_w_out"][l] = gd["w_out"]
                gp["gdn_conv_w"][l] = gd["conv_w8"][:4]
                gp["gdn_a_log"][l] = gd["a_log128"][0, NH:2 * NH]
                gp["gdn_dt_bias"][l] = gd["dt_bias128"][0, NH:2 * NH]
                gp["gdn_norm_g"][l] = gd["norm_g"][0]
            else:
                dx, dkp_l, dv_l, gd = _mla_layer_bwd(f"l{l}m", dx, res[l, 1], m3(l, 1), ng(l, 1), mla_p[l - 2], kp, kvf, tabs)
                dkp = dkp_l if dkp is None else dkp + dkp_l
                dv = dv_l if dv is None else dv + dv_l
                gw["mla_w_dq"][l - 2], gw["mla_w_out"][l - 2] = gd["w_dq"], gd["w_out"]
                gw["mla_w_uq"][l - 2] = gd["w_uq"].reshape(QL, NH, HP)[:, :, :QKH].reshape(QL, NH * QKH)
                gp["mla_q_lora_norm_g"][l - 2] = gd["ql_g"][0]
                gp["mla_q_norm_g"][l - 2] = _join_norm(gd["q_gn"], gd["q_gr"])
            gnorm[l][i] = gd["g"][0]
            for j in range(3):
                dmod[l][3 * i + j] = gd["mod"][j][0]
    gwf = {n: jnp.stack(v) for n, v in gw.items()}
    gwf["ffn_w_in"], gwf["ffn_w_out"] = g_in4, g_out4
    gwf["mla_w_dkv"] = gk["w_dkv"][:, :KVL + ROPE]
    gwf["mla_w_ukv"] = gk["w_ukv"].reshape(KVL, 2, NH, DH).transpose(0, 2, 1, 3).reshape(KVL, 2 * NH * DH)
    gpf = {n: jnp.stack(v) for n, v in gp.items()}
    gpf["norm_g"] = jnp.stack([jnp.stack(r) for r in gnorm])
    gpf["kv_norm_g"] = gk["kv_norm_g"][0]
    gpf["mla_kv_norm_g"] = gk["kv_lat_g"][0]
    gpf["mla_k_norm_g"] = _join_norm(gk["k_gn"], gk["k_gr"])
    dmods = jnp.stack([jnp.stack(r) for r in dmod])
    dkvmod = jnp.stack([gk["mod"][0][0], gk["mod"][1][0]])
    return loss, dx, gwf, gpf, dmods, dkvmod


SMALL = ("norm_g", "gdn_conv_w", "gdn_a_log", "gdn_dt_bias", "gdn_norm_g", "kv_norm_g", "mla_kv_norm_g", "mla_k_norm_g",
         "mla_q_lora_norm_g", "mla_q_norm_g")


def kernel(x, c, positions, ada_w, ada_b, norm_g, ffn_w_in, ffn_w_out, gdn_w_in, gdn_conv_w, gdn_a_log, gdn_dt_bias,
           gdn_norm_g, gdn_w_out, kv_ada_w, kv_ada_b, kv_norm_g, mla_w_dkv, mla_kv_norm_g, mla_w_ukv, mla_k_norm_g,
           mla_w_dq, mla_q_lora_norm_g, mla_w_uq, mla_q_norm_g, mla_w_out, loss_target, m_ada_w, m_ada_b, m_norm_g,
           m_ffn_w_in, m_ffn_w_out, m_gdn_w_in, m_gdn_conv_w, m_gdn_a_log, m_gdn_dt_bias, m_gdn_norm_g, m_gdn_w_out,
           m_kv_ada_w, m_kv_ada_b, m_kv_norm_g, m_mla_w_dkv, m_mla_kv_norm_g, m_mla_w_ukv, m_mla_k_norm_g, m_mla_w_dq,
           m_mla_q_lora_norm_g, m_mla_w_uq, m_mla_q_norm_g, m_mla_w_out, v_ada_w, v_ada_b, v_norm_g, v_ffn_w_in,
           v_ffn_w_out, v_gdn_w_in, v_gdn_conv_w, v_gdn_a_log, v_gdn_dt_bias, v_gdn_norm_g, v_gdn_w_out, v_kv_ada_w,
           v_kv_ada_b, v_kv_norm_g, v_mla_w_dkv, v_mla_kv_norm_g, v_mla_w_ukv, v_mla_k_norm_g, v_mla_w_dq,
           v_mla_q_lora_norm_g, v_mla_w_uq, v_mla_q_norm_g, v_mla_w_out):
    a = dict(locals())
    mx, my, mc = _me()
    dev = 4 * mx + 2 * my + mc
    chip = 2 * mx + my
    x, tgt, pos = a["x"][0], a["loss_target"][0], a["positions"][0]
    take = lambda arr, i, axis=0: lax.dynamic_index_in_dim(arr, i, axis, keepdims=False)

    pre = _all_gather8("ag_pre", _pack_small([a["c"], a["gdn_conv_w"], a["norm_g"]]))
    c_all, conv_sh, norm_sh = _unpack_small(pre, [(D,), a["gdn_conv_w"].shape, a["norm_g"].shape])
    P = {n: a[n] for n in SMALL}
    P["gdn_conv_w"] = jnp.concatenate([conv_sh[2 * s] for s in range(4)], axis=2)
    P["norm_g"] = jnp.concatenate([norm_sh[2 * s] for s in range(4)], axis=2)
    c_act = _silu_rows("c_act", c_all)
    nada = a["ada_w"].shape[2]
    nkv = a["kv_ada_w"].shape[1]
    modp = [_matmul(f"mod{l}", c_act, a["ada_w"][l], precise=True) for l in range(4)]
    kvp = _matmul("modkv", c_act, a["kv_ada_w"], precise=True)
    mp = _all_gather8("ag_mod", _pack_small(modp + [kvp]))
    modp_all, kvp_all = _unpack_small(mp, [(4, 8, nada), (8, nkv)])
    mods = jnp.concatenate([take(modp_all[2 * s], dev, 1) for s in range(4)], axis=1) + a["ada_b"]
    mods = mods.reshape(4, NMOD, D)
    kvmod = (jnp.concatenate([take(kvp_all[2 * s], dev, 0) for s in range(4)]) + a["kv_ada_b"]).reshape(2, D)

    def gather(tag, w2):
        return _gather_weights("ag_" + tag, _place_shard("own_" + tag, w2, chip))

    def reduce(tag, g4):
        q = _add_half("rsp_" + tag, g4, _exchange_half("rs1_" + tag, g4), mc)
        r2 = _add_chips("rsc_" + tag, q, _scatter_chips("rs2_" + tag, q), chip, mc)
        return _exchange_full("rs3_" + tag, r2)

    halves = lambda t: t.reshape((2, -1) + t.shape[-1:])
    W = {n: gather(t, halves(a[n].astype(BF16))).reshape((4, 8) + a[n].shape[2:])
         for n, t in (("ffn_w_in", "wi"), ("ffn_w_out", "wo"))}
    wg = gather("wg", a["gdn_w_in"].astype(BF16))
    W["gdn_w_in"] = jnp.concatenate([wg[s] for s in range(4)], axis=2)
    wall = gather("wr", _pack_flat(jnp.concatenate([a[n].reshape(-1).astype(BF16) for n, _ in REST]))).reshape(4, -1)
    off = 0
    for n, ax in REST:
        sz = a[n].size
        W[n] = _shards_merge(wall[:, off:off + sz].reshape((4,) + a[n].shape), ax)
        off += sz

    loss, dx, gw, gp, dmods, dkvmod = _step(x, tgt, pos, mods, kvmod, W, P)
    loss = lax.psum(loss, ("x", "y", "c"))

    grads = {n: reduce(t, gw[n].reshape((4, 2, -1) + a[n].shape[-1:])).reshape(a[n].shape)
             for n, t in (("ffn_w_in", "wi"), ("ffn_w_out", "wo"))}
    ng = a["gdn_w_in"].shape[2]
    grads["gdn_w_in"] = reduce("wg", jnp.stack([gw["gdn_w_in"][:, :, s * ng:(s + 1) * ng] for s in range(4)]))
    gsh = reduce("wr", _pack_flat(jnp.concatenate([_shards_first(gw[n], ax).reshape(4, -1) for n, ax in REST], axis=1)))
    gsh = gsh.reshape(-1)
    off = 0
    for n, _ in REST:
        grads[n] = gsh[off:off + a[n].size].reshape(a[n].shape)
        off += a[n].size

    small = _all_gather8("ag_small", _pack_small([dmods, dkvmod] + [gp[n] for n in SMALL]))
    shapes = [(4, NMOD * D), (2 * D,)] + [gp[n].shape for n in SMALL]
    dmod_all, dkv_all = _unpack_small(small, shapes)[:2]
    tot = _unpack_small(_sum8("sum_small", small), shapes)
    grads["ada_b"], grads["kv_ada_b"] = tot[0], tot[1]
    for n, t in zip(SMALL, tot[2:]):
        grads[n] = t
    grads["norm_g"] = lax.dynamic_slice_in_dim(grads["norm_g"], chip * a["norm_g"].shape[2], a["norm_g"].shape[2], 2)
    grads["gdn_conv_w"] = lax.dynamic_slice_in_dim(grads["gdn_conv_w"], chip * a["gdn_conv_w"].shape[2],
                                                   a["gdn_conv_w"].shape[2], 2)
    ca = jnp.pad(c_act, ((0, LANE - 8), (0, 0)))
    dm = jnp.pad(lax.dynamic_slice_in_dim(dmod_all.reshape(8, 4, NMOD * D), chip * nada, nada, 2), ((0, LANE - 8), (0, 0), (0, 0)))
    grads["ada_w"] = jnp.stack([_matmul(f"gada{l}", ca, dm[:, l], "tn", precise=True) for l in range(4)])
    dk = jnp.pad(lax.dynamic_slice_in_dim(dkv_all, chip * nkv, nkv, 1), ((0, LANE - 8), (0, 0)))
    grads["kv_ada_w"] = _matmul("gadakv", ca, dk, "tn", precise=True)

    upd = [_adamw("adamw_" + n, a[n], grads[n], a["m_" + n], a["v_" + n]) for n in WEIGHTS]
    return (loss, dx[None], *[grads[n] for n in WEIGHTS], *[u[0] for u in upd], *[u[1] for u in upd], *[u[2] for u in upd])
```

```python
import functools

import jax
import jax.numpy as jnp
from jax import lax
from jax.experimental import pallas as pl
from jax.experimental.pallas import tpu as pltpu

F32 = jnp.float32
BF16 = jnp.bfloat16
HI = lax.Precision.HIGHEST
MESH = pl.DeviceIdType.MESH

D = 1024
NH = 8
DH = 128
FF = 2816
NMOD = 9
CHUNK = 64
ROPE = 64
QKH = 192
HP = 256
KVL = 256
QL = 384
GDN_IN = 4224
GATE_CB = 32
EPS = 1e-6
ROPE_BASE = 10000.0
LANE = 128
SUB = 8
VMEM_LIMIT = 56 * 1024 * 1024

ADAM_LR, ADAM_B1, ADAM_B2, ADAM_EPS, ADAM_WD, ADAM_STEP = 0.001, 0.9, 0.999, 1e-08, 0.01, 10


def _tile(n, prefs=(512, 384, 256, 128)):
    for p in prefs:
        if n % p == 0:
            return p
    return n


def _cparams(sem):
    return pltpu.CompilerParams(dimension_semantics=sem, vmem_limit_bytes=VMEM_LIMIT)


class Row:
    def __init__(self, arr, width=None, cb=0, splits=None, halo=None):
        self.arr = arr
        self.width = arr.shape[1] if width is None else width
        self.cb = cb
        self.splits = splits
        self.halo = halo


def _rowwise(name, fn, rows, bcs, outs, accs, tm):
    S = rows[0].arr.shape[0]
    n = S // tm
    nr, nb, no, na = len(rows), len(bcs), len(outs), len(accs)

    def body(*refs):
        rrefs, brefs = refs[:nr], refs[nr:nr + nb]
        orefs, arefs = refs[nr + nb:nr + nb + no], refs[nr + nb + no:]
        pieces = []
        for r, ref in zip(rows, rrefs):
            if r.splits is None:
                pieces.append(ref[...])
            else:
                off = 0
                for w in r.splits:
                    pieces.append(ref[:, off:off + w])
                    off += w
        out_pieces, acc_vals = fn(pieces, [b[...] for b in brefs])
        k = 0
        for (widths, dt), oref in zip(outs, orefs):
            off = 0
            for w in widths:
                oref[:, off:off + w] = out_pieces[k].astype(dt)
                k += 1
                off += w
        if na:
            @pl.when(pl.program_id(0) == 0)
            def _():
                for a in arefs:
                    a[...] = jnp.zeros(a.shape, F32)
            for a, v in zip(arefs, acc_vals):
                a[...] += v

    in_specs = []
    for r in rows:
        if r.halo is None:
            in_specs.append(pl.BlockSpec((tm, r.width), lambda i, cb=r.cb: (i, cb)))
        elif r.halo == "prev":
            in_specs.append(pl.BlockSpec((SUB, r.width), lambda i, cb=r.cb: (jnp.maximum(i * (tm // SUB) - 1, 0), cb)))
        else:
            in_specs.append(pl.BlockSpec((SUB, r.width), lambda i, cb=r.cb: (jnp.minimum((i + 1) * (tm // SUB), S // SUB - 1), cb)))
    in_specs += [pl.BlockSpec(b.shape, lambda i, nd=b.ndim: (0,) * nd) for b in bcs]
    out_specs = [pl.BlockSpec((tm, sum(w)), lambda i: (i, 0)) for w, _ in outs]
    out_specs += [pl.BlockSpec(s, lambda i: (0, 0)) for s in accs]
    out_shape = [jax.ShapeDtypeStruct((S, sum(w)), dt) for w, dt in outs]
    out_shape += [jax.ShapeDtypeStruct(s, F32) for s in accs]
    res = pl.pallas_call(body, name=name, grid=(n,), in_specs=in_specs, out_specs=out_specs, out_shape=out_shape,
                         compiler_params=_cparams(("arbitrary",)))(*[r.arr for r in rows], *bcs)
    return res


def _rw_fwd(name, f, rows, bcs, outs, tm):
    def fn(pieces, bvals):
        return list(f(*[p.astype(F32) for p in pieces], *[b.astype(F32) for b in bvals])), []
    return _rowwise(name, fn, rows, bcs, outs, [], tm)


def _npieces(rows):
    return sum(1 if r.splits is None else len(r.splits) for r in rows)


def _rw_bwd(name, f, rows, bcs, cts, drow, dbc, outs, tm, add=None):
    np_, nct = _npieces(rows), _npieces(cts)

    def fn(pieces, bvals):
        allv = [p.astype(F32) for p in pieces[:np_]] + [b.astype(F32) for b in bvals]
        ct = [p.astype(F32) for p in pieces[np_:np_ + nct]]
        didx = [i for i, m in enumerate(list(drow) + list(dbc)) if m]

        def g(*dv):
            full = list(allv)
            for i, v in zip(didx, dv):
                full[i] = v
            return tuple(f(*full))

        _, vjp = jax.vjp(g, *[allv[i] for i in didx])
        grads = vjp(tuple(ct))
        nrd = sum(bool(m) for m in drow)
        rg, bg = list(grads[:nrd]), list(grads[nrd:])
        if add is not None:
            rg[0] = rg[0] + pieces[np_ + nct].astype(F32)
        return rg, bg

    accs = [b.shape for b, m in zip(bcs, dbc) if m]
    return _rowwise(name, fn, list(rows) + list(cts) + ([add] if add is not None else []), bcs, outs, accs, tm)


def _sigmoid(x):
    return 1.0 / (1.0 + jnp.exp(-x))


def _silu(x):
    return x * _sigmoid(x)


def _softplus(x):
    return jnp.maximum(x, 0.0) + jnp.log(1.0 + jnp.exp(-jnp.abs(x)))


def f_mod(x, g, shift, scale):
    y = x * lax.rsqrt(jnp.mean(x * x, axis=-1, keepdims=True) + EPS)
    return (y * g * (1.0 + scale) + shift,)


def f_rms(x, g):
    return (x * lax.rsqrt(jnp.mean(x * x, axis=-1, keepdims=True) + EPS) * g,)


def f_act(gate, up):
    return (_silu(gate) * up,)


def make_f_res(coef):
    def f_res(y, gate):
        return (coef * gate * y,)
    return f_res


def f_gdnpre(*p):
    out = []
    for i, t in enumerate(p):
        t = _silu(t)
        if i < 2 * NH:
            t = t * lax.rsqrt(jnp.sum(t * t, axis=-1, keepdims=True) + EPS)
        out.append(t)
    return tuple(out)


def f_gates(gates, a_log, dt_bias):
    return _sigmoid(gates), -jnp.exp(a_log) * _softplus(gates + dt_bias)


def f_gdnpost(*a):
    o, z, g = a[:NH], a[NH:2 * NH], a[2 * NH]
    out = []
    for oh, zh in zip(o, z):
        y = oh * lax.rsqrt(jnp.mean(oh * oh, axis=-1, keepdims=True) + EPS) * g
        out.append(y * _silu(zh))
    return tuple(out)


def make_f_qk(shared_rope):
    def f(*a):
        if shared_rope:
            ns, rs = a[:NH], [a[NH]] * NH
            cosp, sins, gn, gr, pm = a[NH + 1:NH + 6]
        else:
            ns, rs = a[0:2 * NH:2], a[1:2 * NH:2]
            cosp, sins, gn, gr, pm = a[2 * NH:2 * NH + 5]
        out = []
        for n, r in zip(ns, rs):
            ss = jnp.sum(n * n, axis=-1, keepdims=True) + jnp.sum(r * r, axis=-1, keepdims=True)
            rstd = lax.rsqrt(ss * (1.0 / QKH) + EPS)
            yn = n * rstd * gn
            yr = r * rstd * gr
            sw = jnp.dot(yr, pm, precision=HI, preferred_element_type=F32)
            out += [yn, yr * cosp + sw * sins]
        return tuple(out)
    return f


def _matmul(name, a, b, mode="nn", out_dtype=F32, precise=False, lay=None, li=0, into=None, nmat=1):
    if lay == "b_cols":
        per = b.shape[3]
        rb, cb = b.shape[2], 4 * per
    elif lay == "b_rows":
        per = b.shape[2]
        rb, cb = 4 * per, b.shape[3]
    elif lay == "b_stack":
        rb, cb = b.shape[1:]
    else:
        rb, cb = b.shape
    if mode == "nn":
        (M, K), N = a.shape, cb
    elif mode == "nt":
        (M, K), N = a.shape, rb
    else:
        (K, M), N = a.shape, cb
    tm = _tile(M, (1024, 512, 256, 128))
    tn = _tile(N, (1024, 512, 384, 256, 128))
    tk = _tile(K, (1408, 1024, 512, 384, 256, 128))
    if lay == "b_cols":
        tn, tk = (per, tk) if mode == "nn" else (tn, per)
    elif lay == "b_rows":
        tm, tn, tk = (tm, 512, K) if mode == "nn" else (min(tm, 512), N, tk)
    elif lay == "o_cols":
        per = N // 4
        tn = per
    elif lay == "o_rows":
        per = M // 4
        tm, tn = M, 512
    nk = K // tk
    dims = {"nn": (((1,), (0,)), ((), ())), "nt": (((1,), (1,)), ((), ())), "tn": (((0,), (0,)), ((), ()))}[mode]

    def body(a_ref, b_ref, *rest):
        o_ref, acc_ref = rest[-2:]
        k = pl.program_id(2)

        @pl.when(k == 0)
        def _():
            acc_ref[...] = jnp.zeros(acc_ref.shape, F32)

        bv = b_ref[...]
        if lay == "b_rows":
            bv = bv.reshape(4 * per, bv.shape[2])
        if precise:
            acc_ref[...] += lax.dot_general(a_ref[...].astype(F32), bv.astype(F32), dims, precision=HI,
                                            preferred_element_type=F32)
        else:
            acc_ref[...] += lax.dot_general(a_ref[...].astype(BF16), bv.astype(BF16), dims, preferred_element_type=F32)

        @pl.when(k == nk - 1)
        def _():
            if lay == "o_rows":
                for s in range(4):
                    o_ref[s] = acc_ref[s * per:(s + 1) * per, :].astype(o_ref.dtype)
            else:
                o_ref[...] = acc_ref[...].astype(o_ref.dtype)

    a_spec = pl.BlockSpec((tk, tm), lambda i, j, k: (k, i)) if mode == "tn" else pl.BlockSpec((tm, tk), lambda i, j, k: (i, k))
    if lay == "b_cols":
        b_spec = (pl.BlockSpec((None, None, tk, per), lambda i, j, k: (j, li, k, 0)) if mode == "nn" else
                  pl.BlockSpec((None, None, tn, per), lambda i, j, k: (k, li, j, 0)))
    elif lay == "b_rows":
        b_spec = (pl.BlockSpec((4, None, per, tn), lambda i, j, k: (0, li, 0, j)) if mode == "nn" else
                  pl.BlockSpec((4, None, per, tk), lambda i, j, k: (0, li, 0, k)))
    elif lay == "b_stack":
        b_spec = pl.BlockSpec((None, tk, tn), lambda i, j, k: (li, k, j))
    elif mode == "nt":
        b_spec = pl.BlockSpec((tn, tk), lambda i, j, k: (j, k))
    else:
        b_spec = pl.BlockSpec((tk, tn), lambda i, j, k: (k, j))
    if lay == "o_stack":
        o_spec = pl.BlockSpec((None, tm, tn), lambda i, j, k: (li, i, j))
        o_shape = jax.ShapeDtypeStruct((nmat, M, N), out_dtype)
    elif lay == "o_cols":
        o_spec = pl.BlockSpec((None, None, tm, per), lambda i, j, k: (j, li, i, 0))
        o_shape = jax.ShapeDtypeStruct((4, nmat, M, per), out_dtype)
    elif lay == "o_rows":
        o_spec = pl.BlockSpec((4, None, per, tn), lambda i, j, k: (0, li, 0, j))
        o_shape = jax.ShapeDtypeStruct((4, nmat, per, N), out_dtype)
    else:
        o_spec = pl.BlockSpec((tm, tn), lambda i, j, k: (i, j))
        o_shape = jax.ShapeDtypeStruct((M, N), out_dtype)
    in_specs, args, alias = [a_spec, b_spec], [a, b], {}
    if into is not None:
        in_specs.append(pl.BlockSpec(memory_space=pl.ANY))
        args.append(into)
        alias = {2: 0}
    return pl.pallas_call(body, name=name, grid=(M // tm, N // tn, nk), in_specs=in_specs, out_specs=o_spec,
                          out_shape=o_shape, scratch_shapes=[pltpu.VMEM((tm, tn), F32)], input_output_aliases=alias,
                          compiler_params=_cparams(("parallel", "parallel", "arbitrary")))(*args)


def _shift_down(t, p, d):
    if d == 0:
        return t
    tr = pltpu.roll(t, d, 0)
    pr = pltpu.roll(p, d, 0)
    r8 = lax.broadcasted_iota(jnp.int32, p.shape, 0)
    first = jnp.where(r8 < d, pr, tr[:SUB])
    return jnp.concatenate([first, tr[SUB:]], axis=0)


def _shift_up(t, nx, d):
    if d == 0:
        return t
    tm = t.shape[0]
    tr = pltpu.roll(t, tm - d, 0)
    nr = pltpu.roll(nx, SUB - d, 0)
    r8 = lax.broadcasted_iota(jnp.int32, nx.shape, 0)
    last = jnp.where(r8 >= SUB - d, nr, tr[tm - SUB:])
    return jnp.concatenate([tr[:tm - SUB], last], axis=0)


def _conv_fwd(name, proj, w8, C, tm):
    def fn(pieces, bvals):
        t, p = pieces[0].astype(F32), pieces[1].astype(F32)
        w = bvals[0]
        p = jnp.where(pl.program_id(0) == 0, 0.0, p)
        out = w[3:4] * t
        for d in (1, 2, 3):
            out = out + w[3 - d:4 - d] * _shift_down(t, p, d)
        return [out], []
    return _rowwise(name, fn, [Row(proj, C), Row(proj, C, halo="prev")], [w8], [((C,), F32)], [], tm)[0]


def _conv_bwd(name, proj, dout, w8, C, tm, out_dtype):
    n = proj.shape[0] // tm

    def fn(pieces, bvals):
        t, p, g, gn = [v.astype(F32) for v in pieces]
        w = bvals[0]
        i = pl.program_id(0)
        p = jnp.where(i == 0, 0.0, p)
        gn = jnp.where(i == n - 1, 0.0, gn)
        dx = w[3:4] * g
        dws = [jnp.sum(g * t, axis=0, keepdims=True)]
        for d in (1, 2, 3):
            dx = dx + w[3 - d:4 - d] * _shift_up(g, gn, d)
            dws.append(jnp.sum(g * _shift_down(t, p, d), axis=0, keepdims=True))
        dw = jnp.concatenate([dws[3], dws[2], dws[1], dws[0], jnp.zeros((4, g.shape[1]), F32)], axis=0)
        return [dx], [dw]
    return _rowwise(name, fn, [Row(proj, C), Row(proj, C, halo="prev"), Row(dout), Row(dout, halo="next")], [w8],
                    [((C,), out_dtype)], [(SUB, C)], tm)


def _bdot(a, b, ca, cb):
    return lax.dot_general(a.astype(BF16), b.astype(BF16), (((ca,), (cb,)), ((0,), (0,))), preferred_element_type=F32)


def _bdot3(a, b, ca, cb):
    dims = (((ca,), (cb,)), ((0,), (0,)))
    ah, bh = a.astype(BF16), b.astype(BF16)
    al, bl = (a - ah.astype(F32)).astype(BF16), (b - bh.astype(F32)).astype(BF16)
    d = lambda x, y: lax.dot_general(x, y, dims, preferred_element_type=F32)
    return d(ah, bh) + (d(ah, bl) + d(al, bh))


@jax.custom_vjp
def _bmm3(a, b):
    return _bdot3(a, b, 2, 1)


_bmm3.defvjp(lambda a, b: (_bdot3(a, b, 2, 1), (a, b)),
             lambda res, g: (_bdot3(g, res[1], 2, 2), _bdot3(res[0], g, 1, 1)))


def _neumann(nl):
    C = nl.shape[1]
    eye = (lax.broadcasted_iota(jnp.int32, (1, C, C), 1) == lax.broadcasted_iota(jnp.int32, (1, C, C), 2)).astype(F32)
    T = eye + nl
    pw = nl
    for _ in range(C.bit_length() - 2):
        pw = _bdot3(pw, pw, 2, 1)
        T = T + _bdot3(T, pw, 2, 1)
    return T


_unit_lower_inv = jax.custom_vjp(_neumann)


def _unit_lower_inv_fwd(nl):
    T = _neumann(nl)
    return T, T


def _unit_lower_inv_bwd(T, g):
    return (_bdot3(_bdot3(T, g, 1, 1), T, 2, 2),)


_unit_lower_inv.defvjp(_unit_lower_inv_fwd, _unit_lower_inv_bwd)


@jax.custom_vjp
def _known_inv(nl, T):
    return T


_known_inv.defvjp(lambda nl, T: (T, T), lambda T, g: (_unit_lower_inv_bwd(T, g)[0], jnp.zeros_like(T)))


def _gdn_chunk(q, k, v, gcol, grow, bcol, S, T_saved=None):
    C = CHUNK
    ii = lax.broadcasted_iota(jnp.int32, (1, C, C), 1)
    jj = lax.broadcasted_iota(jnp.int32, (1, C, C), 2)
    incl, strict = ii >= jj, ii > jj
    gc_col = jnp.sum(jnp.where(incl, 1.0, 0.0) * grow, axis=2, keepdims=True)
    gc_row = jnp.sum(jnp.where(jj >= ii, 1.0, 0.0) * gcol, axis=1, keepdims=True)
    decay = jnp.where(incl, jnp.exp(jnp.where(incl, gc_col - gc_row, 0.0)), 0.0)
    qs = q * (DH ** -0.5)
    kb = k * bcol
    nl = -jnp.where(strict, _bdot(kb, k, 2, 2) * decay, 0.0)
    T = _unit_lower_inv(nl) if T_saved is None else _known_inv(nl, T_saved)
    egc = jnp.exp(gc_col)
    u = _bmm3(T, v * bcol)
    w = _bmm3(T, kb * egc)
    att = jnp.where(incl, _bdot(qs, k, 2, 2) * decay, 0.0)
    v_new = u - _bdot(w, S, 2, 1)
    o = _bdot(qs * egc, S, 2, 1) + _bdot(att, v_new, 2, 1)
    g_last = jnp.sum(grow, axis=2, keepdims=True)
    k_dec = k * jnp.exp(g_last - gc_col)
    S_out = S * jnp.exp(g_last) + _bdot(k_dec, v_new, 1, 1)
    return o, S_out, T


def _heads(ref, w):
    return jnp.stack([ref[:, h * w:(h + 1) * w] for h in range(NH)])


def _gdn_specs(NC, rev):
    ix = (lambda i: NC - 1 - i) if rev else (lambda i: i)
    wide = pl.BlockSpec((CHUNK, D), lambda i: (ix(i), 0))
    col = pl.BlockSpec((CHUNK, NH), lambda i: (ix(i), 0))
    row = pl.BlockSpec((1, NH, CHUNK), lambda i: (ix(i), 0, 0))
    st = pl.BlockSpec((1, NH, DH, DH), lambda i: (ix(i), 0, 0, 0))
    tinv = pl.BlockSpec((1, NH, CHUNK, CHUNK), lambda i: (ix(i), 0, 0, 0))
    return wide, col, row, st, tinv


def _gdn_fwd(name, q, k, v, gcol, grow, bcol):
    S = q.shape[0]
    NC = S // CHUNK

    def body(q_ref, k_ref, v_ref, gc_ref, gr_ref, b_ref, o_ref, ss_ref, t_ref, st):
        @pl.when(pl.program_id(0) == 0)
        def _():
            st[...] = jnp.zeros(st.shape, F32)
        s_in = st[...]
        ss_ref[0] = s_in
        grow = jnp.stack([gr_ref[0, h:h + 1, :] for h in range(NH)])
        o, s_out, tinv = _gdn_chunk(_heads(q_ref, DH), _heads(k_ref, DH), _heads(v_ref, DH), _heads(gc_ref, 1), grow,
                                    _heads(b_ref, 1), s_in)
        for h in range(NH):
            o_ref[:, h * DH:(h + 1) * DH] = o[h]
        t_ref[0] = tinv
        st[...] = s_out

    wide, col, row, stsp, tsp = _gdn_specs(NC, False)
    return pl.pallas_call(body, name=name, grid=(NC,), in_specs=[wide, wide, wide, col, row, col],
                          out_specs=[wide, stsp, tsp],
                          out_shape=[jax.ShapeDtypeStruct((S, D), F32), jax.ShapeDtypeStruct((NC, NH, DH, DH), F32),
                                     jax.ShapeDtypeStruct((NC, NH, CHUNK, CHUNK), F32)],
                          scratch_shapes=[pltpu.VMEM((NH, DH, DH), F32)],
                          compiler_params=_cparams(("arbitrary",)))(q, k, v, gcol, grow, bcol)


def _gdn_bwd(name, q, k, v, gcol, grow, bcol, ssave, tsave, do):
    S = q.shape[0]
    NC = S // CHUNK

    def body(q_ref, k_ref, v_ref, gc_ref, gr_ref, b_ref, ss_ref, t_ref, do_ref, dq_ref, dk_ref, dv_ref, dgc_ref, dgr_ref,
             db_ref, dst):
        @pl.when(pl.program_id(0) == 0)
        def _():
            dst[...] = jnp.zeros(dst.shape, F32)
        grow = jnp.stack([gr_ref[0, h:h + 1, :] for h in range(NH)])
        prim = (_heads(q_ref, DH), _heads(k_ref, DH), _heads(v_ref, DH), _heads(gc_ref, 1), grow, _heads(b_ref, 1), ss_ref[0])
        tinv = t_ref[0]
        _, vjp = jax.vjp(lambda *p: _gdn_chunk(*p, T_saved=tinv)[:2], *prim)
        dq, dk, dv, dgc, dgr, db, ds = vjp((_heads(do_ref, DH), dst[...]))
        for h in range(NH):
            hs = slice(h * DH, (h + 1) * DH)
            dq_ref[:, hs] = dq[h]
            dk_ref[:, hs] = dk[h]
            dv_ref[:, hs] = dv[h]
            dgc_ref[:, h:h + 1] = dgc[h]
            dgr_ref[0, h:h + 1, :] = dgr[h]
            db_ref[:, h:h + 1] = db[h]
        dst[...] = ds

    wide, col, row, stsp, tsp = _gdn_specs(NC, True)
    return pl.pallas_call(body, name=name, grid=(NC,), in_specs=[wide, wide, wide, col, row, col, stsp, tsp, wide],
                          out_specs=[wide, wide, wide, col, row, col],
                          out_shape=[jax.ShapeDtypeStruct((S, D), F32)] * 3 + [jax.ShapeDtypeStruct((S, NH), F32),
                                                                                 jax.ShapeDtypeStruct((NC, NH, CHUNK), F32),
                                                                                 jax.ShapeDtypeStruct((S, NH), F32)],
                          scratch_shapes=[pltpu.VMEM((NH, DH, DH), F32)],
                          compiler_params=_cparams(("arbitrary",)))(q, k, v, gcol, grow, bcol, ssave, tsave, do)


TQ = 512
SM_SCALE = QKH ** -0.5
NEG = -1e30


def _diag_mask(transposed):
    r = lax.broadcasted_iota(jnp.int32, (TQ, TQ), 0) // CHUNK
    c = lax.broadcasted_iota(jnp.int32, (TQ, TQ), 1) // CHUNK
    return (r <= c) if transposed else (c <= r)


def _dot_nt(a, b):
    return lax.dot_general(a, b, (((1,), (1,)), ((), ())), preferred_element_type=F32)


def _flash_fwd(name, qp, kp, kv):
    S = qp.shape[0]
    nq = S // (2 * TQ)

    def body(q_ref, k_ref, v_ref, o_ref, lse_ref):
        qi = pl.program_id(1)
        qs = (q_ref[:TQ, :], q_ref[TQ:, :])

        def step(q, j, carry, masked):
            m, l, acc = carry
            rows = pl.ds(pl.multiple_of(j * TQ, TQ), TQ)
            s = _dot_nt(q, k_ref[rows, :]) * SM_SCALE
            if masked:
                s = jnp.where(_diag_mask(False), s, NEG)
            m_new = jnp.maximum(m, jnp.max(s, axis=-1, keepdims=True))
            p = jnp.exp(s - m_new)
            alpha = jnp.exp(m - m_new)
            l = alpha * l + jnp.sum(p, axis=-1, keepdims=True)
            acc = alpha * acc + jnp.dot(p.astype(BF16), v_ref[rows, :].astype(BF16), preferred_element_type=F32)
            return m_new, l, acc

        init = (jnp.full((TQ, 1), NEG, F32), jnp.zeros((TQ, 1), F32), jnp.zeros((TQ, DH), F32))
        ca, cb = lax.fori_loop(0, 2 * qi, lambda j, c: (step(qs[0], j, c[0], False), step(qs[1], j, c[1], False)),
                               (init, init))
        ca = step(qs[0], 2 * qi, ca, True)
        cb = step(qs[1], 2 * qi + 1, step(qs[1], 2 * qi, cb, False), True)
        for u, (m, l, acc) in enumerate((ca, cb)):
            o_ref[u * TQ:(u + 1) * TQ, :] = acc / l
            lse_ref[0, u * TQ:(u + 1) * TQ, :] = m + jnp.log(l)

    return pl.pallas_call(
        body, name=name, grid=(NH, nq),
        in_specs=[pl.BlockSpec((2 * TQ, HP), lambda h, i: (i, h)), pl.BlockSpec((S, HP), lambda h, i: (0, h)),
                  pl.BlockSpec((S, DH), lambda h, i: (0, NH + h))],
        out_specs=[pl.BlockSpec((2 * TQ, DH), lambda h, i: (i, h)), pl.BlockSpec((1, 2 * TQ, 1), lambda h, i: (h, i, 0))],
        out_shape=[jax.ShapeDtypeStruct((S, NH * DH), F32), jax.ShapeDtypeStruct((NH, S, 1), F32)],
        compiler_params=_cparams(("parallel", "arbitrary")))(qp, kp, kv)


def _flash_bwd_dq(name, qp, kp, kv, o, do, lse):
    S = qp.shape[0]
    nq = S // TQ

    def body(q_ref, k_ref, v_ref, o_ref, do_ref, lse_ref, dq_ref, dl_ref):
        qi = pl.program_id(1)
        q = q_ref[...]
        do = do_ref[...]
        delta = jnp.sum(o_ref[...] * do, axis=-1, keepdims=True)
        dl_ref[0] = delta
        dob = do.astype(BF16)
        lse = lse_ref[0]

        def step(j, dq, masked):
            rows = pl.ds(pl.multiple_of(j * TQ, TQ), TQ)
            k = k_ref[rows, :]
            s = _dot_nt(q, k) * SM_SCALE
            if masked:
                s = jnp.where(_diag_mask(False), s, NEG)
            p = jnp.exp(s - lse)
            dp = _dot_nt(dob, v_ref[rows, :].astype(BF16))
            ds = p * (dp - delta) * SM_SCALE
            return dq + jnp.dot(ds.astype(BF16), k, preferred_element_type=F32)

        dq = lax.fori_loop(0, qi, lambda j, c: step(j, c, False), jnp.zeros((TQ, HP), F32))
        dq_ref[...] = step(qi, dq, True)

    return pl.pallas_call(
        body, name=name, grid=(NH, nq),
        in_specs=[pl.BlockSpec((TQ, HP), lambda h, i: (i, h)), pl.BlockSpec((S, HP), lambda h, i: (0, h)),
                  pl.BlockSpec((S, DH), lambda h, i: (0, NH + h)), pl.BlockSpec((TQ, DH), lambda h, i: (i, h)),
                  pl.BlockSpec((TQ, DH), lambda h, i: (i, h)), pl.BlockSpec((1, TQ, 1), lambda h, i: (h, i, 0))],
        out_specs=[pl.BlockSpec((TQ, HP), lambda h, i: (i, h)), pl.BlockSpec((1, TQ, 1), lambda h, i: (h, i, 0))],
        out_shape=[jax.ShapeDtypeStruct((S, NH * HP), F32), jax.ShapeDtypeStruct((NH, S, 1), F32)],
        compiler_params=_cparams(("parallel", "arbitrary")))(qp, kp, kv, o, do, lse)


def _flash_bwd_dkv(name, qp, kp, kv, do, lse_row, delta_row):
    S = qp.shape[0]
    nq = S // TQ

    def body(q_ref, k_ref, v_ref, do_ref, lse_ref, dl_ref, dk_ref, dv_ref):
        kj = pl.program_id(1)
        k = k_ref[...]
        vb = v_ref[...].astype(BF16)

        def step(i, carry, masked):
            dk, dv = carry
            rows = pl.ds(pl.multiple_of(i * TQ, TQ), TQ)
            q = q_ref[rows, :]
            dob = do_ref[rows, :].astype(BF16)
            st = _dot_nt(k, q) * SM_SCALE
            pt = jnp.exp(st - lse_ref[0, :, rows])
            if masked:
                pt = jnp.where(_diag_mask(True), pt, 0.0)
            dv = dv + jnp.dot(pt.astype(BF16), dob, preferred_element_type=F32)
            dpt = _dot_nt(vb, dob)
            dst = pt * (dpt - dl_ref[0, :, rows]) * SM_SCALE
            dk = dk + jnp.dot(dst.astype(BF16), q, preferred_element_type=F32)
            return dk, dv

        carry = step(kj, (jnp.zeros((TQ, HP), F32), jnp.zeros((TQ, DH), F32)), True)
        dk, dv = lax.fori_loop(kj + 1, nq, lambda i, c: step(i, c, False), carry)
        dk_ref[...] = dk
        dv_ref[...] = dv

    return pl.pallas_call(
        body, name=name, grid=(NH, nq),
        in_specs=[pl.BlockSpec((S, HP), lambda h, j: (0, h)), pl.BlockSpec((TQ, HP), lambda h, j: (j, h)),
                  pl.BlockSpec((TQ, DH), lambda h, j: (j, NH + h)), pl.BlockSpec((S, DH), lambda h, j: (0, h)),
                  pl.BlockSpec((1, 1, S), lambda h, j: (h, 0, 0)), pl.BlockSpec((1, 1, S), lambda h, j: (h, 0, 0))],
        out_specs=[pl.BlockSpec((TQ, HP), lambda h, j: (j, h)), pl.BlockSpec((TQ, DH), lambda h, j: (j, h))],
        out_shape=[jax.ShapeDtypeStruct((S, NH * HP), F32), jax.ShapeDtypeStruct((S, NH * DH), F32)],
        compiler_params=_cparams(("parallel", "arbitrary")))(qp, kp, kv, do, lse_row, delta_row)


def _tm(S, width):
    t = 512 if width <= 1024 else (256 if width <= 3072 else 128)
    return min(t, S)


def _mod_fwd(tag, x, g, shift, scale):
    S = x.shape[0]
    return _rw_fwd(tag + "_mod", f_mod, [Row(x)], [g, shift, scale], [((D,), BF16)], _tm(S, D))[0]


def _mod_bwd(tag, x, g, shift, scale, dh, dx_direct):
    S = x.shape[0]
    r = _rw_bwd(tag + "_mod_b", f_mod, [Row(x)], [g, shift, scale], [Row(dh)], [True], [True] * 3, [((D,), F32)],
                _tm(S, D), add=Row(dx_direct))
    return r[0], r[1:]


def _res_fwd(tag, x, y, gate, coef):
    S = x.shape[0]

    def fn(pieces, bvals):
        return [pieces[0] + coef * bvals[0] * pieces[1]], []
    return _rowwise(tag + "_res", fn, [Row(x), Row(y)], [gate], [((D,), F32)], [], _tm(S, D))[0]


def _res_bwd(tag, y, gate, dxn, coef):
    S = y.shape[0]
    r = _rw_bwd(tag + "_res_b", make_f_res(coef), [Row(y)], [gate], [Row(dxn)], [True], [True], [((D,), BF16)], _tm(S, D))
    return r[0], r[1]


def _ffn_fwd(tag, x, mod3, g, w_in4, w_out4, li):
    shift, scale, gate = mod3
    S = x.shape[0]
    h = _mod_fwd(tag, x, g, shift, scale)
    gu = _matmul(tag + "_in", h, w_in4, lay="b_cols", li=li, out_dtype=BF16)
    a = _rw_fwd(tag + "_act", f_act, [Row(gu, splits=[FF, FF])], [], [((FF,), BF16)], _tm(S, FF))[0]
    y = _matmul(tag + "_out", a, w_out4, lay="b_rows", li=li)
    xn = _res_fwd(tag, x, y, gate, 0.5)
    return xn, (x, h, gu, a, y)


def _ffn_bwd(tag, dxn, res, mod3, g, w_in4, w_out4, li, g_in4, g_out4):
    shift, scale, gate = mod3
    x, h, gu, a, y = res
    S = x.shape[0]
    nmat = w_in4.shape[1]
    dy, dgate = _res_bwd(tag, y, gate, dxn, 0.5)
    da = _matmul(tag + "_out_bi", dy, w_out4, "nt", lay="b_rows", li=li, out_dtype=BF16)
    g_out4 = _matmul(tag + "_out_bw", a, dy, "tn", lay="o_rows", li=li, into=g_out4, nmat=nmat)
    dgu = _rw_bwd(tag + "_act_b", f_act, [Row(gu, splits=[FF, FF])], [], [Row(da)], [True, True], [],
                  [((FF, FF), BF16)], _tm(S, FF))[0]
    dh = _matmul(tag + "_in_bi", dgu, w_in4, "nt", lay="b_cols", li=li)
    g_in4 = _matmul(tag + "_in_bw", h, dgu, "tn", lay="o_cols", li=li, into=g_in4, nmat=nmat)
    dx, (dg, dshift, dscale) = _mod_bwd(tag, x, g, shift, scale, dh, dxn)
    return dx, g_in4, g_out4, dict(g=dg, mod=(dshift, dscale, dgate))


def _pad_lanes(a, lo, width=LANE):
    return jnp.pad(a, ((0, 0), (lo, width - lo - a.shape[1])))


def _gdn_layer_fwd(tag, x, mod3, g, p):
    shift, scale, gate = mod3
    S = x.shape[0]
    NC = S // CHUNK
    h = _mod_fwd(tag, x, g, shift, scale)
    proj = _matmul(tag + "_in", h, p["w_in"])
    qc = _conv_fwd(tag + "_conv", proj, p["conv_w8"], 3 * D, _tm(S, 3 * D))
    q, k, v = _rw_fwd(tag + "_pre", f_gdnpre, [Row(qc, splits=[DH] * (3 * NH))], [],
                      [((DH,) * NH, F32)] * 3, _tm(S, 3 * D))
    betaf, gf = _rw_fwd(tag + "_gates", f_gates, [Row(proj, LANE, cb=GATE_CB)], [p["a_log128"], p["dt_bias128"]],
                        [((LANE,), F32)] * 2, _tm(S, LANE))
    bcol, gcol = betaf[:, :NH], gf[:, NH:2 * NH]
    grow = gcol.reshape(NC, CHUNK, NH).transpose(0, 2, 1)
    o, ssave, tsave = _gdn_fwd(tag + "_core", q, k, v, gcol, grow, bcol)
    on = _rw_fwd(tag + "_post", f_gdnpost, [Row(o, splits=[DH] * NH), Row(proj, D, cb=3, splits=[DH] * NH)],
                 [p["norm_g"]], [((DH,) * NH, BF16)], _tm(S, 2 * D))[0]
    y = _matmul(tag + "_out", on, p["w_out"])
    xn = _res_fwd(tag, x, y, gate, 1.0)
    return xn, (x, h, proj, qc, q, k, v, gcol, grow, bcol, ssave, tsave, o, on, y)


def _gdn_layer_bwd(tag, dxn, res, mod3, g, p):
    shift, scale, gate = mod3
    x, h, proj, qc, q, k, v, gcol, grow, bcol, ssave, tsave, o, on, y = res
    S = x.shape[0]
    dy, dgate = _res_bwd(tag, y, gate, dxn, 1.0)
    don = _matmul(tag + "_out_bi", dy, p["w_out"], "nt")
    dw_out = _matmul(tag + "_out_bw", on, dy, "tn")
    do, dz, dnorm = _rw_bwd(tag + "_post_b", f_gdnpost, [Row(o, splits=[DH] * NH), Row(proj, D, cb=3, splits=[DH] * NH)],
                            [p["norm_g"]], [Row(don, splits=[DH] * NH)], [True] * (2 * NH), [True],
                            [((DH,) * NH, F32), ((DH,) * NH, BF16)], _tm(S, 2 * D))
    dq, dk, dv, dgc, dgr, db = _gdn_bwd(tag + "_core_b", q, k, v, gcol, grow, bcol, ssave, tsave, do)
    dgcol = dgc + dgr.transpose(0, 2, 1).reshape(S, NH)
    dgates, da_log, ddt = _rw_bwd(tag + "_gates_b", f_gates, [Row(proj, LANE, cb=GATE_CB)], [p["a_log128"], p["dt_bias128"]],
                                  [Row(_pad_lanes(db, 0)), Row(_pad_lanes(dgcol, NH))], [True], [True, True],
                                  [((LANE,), BF16)], _tm(S, LANE))
    dqc = _rw_bwd(tag + "_pre_b", f_gdnpre, [Row(qc, splits=[DH] * (3 * NH))], [],
                  [Row(dq, splits=[DH] * NH), Row(dk, splits=[DH] * NH), Row(dv, splits=[DH] * NH)],
                  [True] * (3 * NH), [], [((DH,) * (3 * NH), F32)], _tm(S, 3 * D))[0]
    dqkv, dconv = _conv_bwd(tag + "_conv_b", proj, dqc, p["conv_w8"], 3 * D, _tm(S, 3 * D), BF16)
    dproj = jnp.concatenate([dqkv, dz, dgates], axis=1)
    dh = _matmul(tag + "_in_bi", dproj, p["w_in"], "nt")
    dw_in = _matmul(tag + "_in_bw", h, dproj, "tn")
    dx, (dg, dshift, dscale) = _mod_bwd(tag, x, g, shift, scale, dh, dxn)
    return dx, dict(w_in=dw_in, conv_w8=dconv, a_log128=da_log, dt_bias128=ddt, norm_g=dnorm,
                    w_out=dw_out, g=dg, mod=(dshift, dscale, dgate))


def _qk_rows(src, shared_rope, ckv=None):
    if shared_rope:
        return [Row(src, D, cb=0, splits=[DH] * NH), Row(ckv, LANE, cb=2)]
    return [Row(src, splits=[DH] * (2 * NH))]


def _kv_fwd(x, kvmod, p, tabs):
    shift, scale = kvmod
    S = x.shape[0]
    h = _mod_fwd("kv", x, p["kv_norm_g"], shift, scale)
    ckv = _matmul("kv_dkv", h, p["w_dkv"])
    lat = _rw_fwd("kv_lat", f_rms, [Row(ckv, KVL)], [p["kv_lat_g"]], [((KVL,), BF16)], _tm(S, KVL))[0]
    kvf = _matmul("kv_ukv", lat, p["w_ukv"])
    kp = _rw_fwd("kv_k", make_f_qk(True), _qk_rows(kvf, True, ckv) + [Row(tabs[0]), Row(tabs[1])],
                 [p["k_gn"], p["k_gr"], p["pm"]], [((DH,) * (2 * NH), BF16)], _tm(S, 2 * D))[0]
    return kp, kvf, (x, h, ckv, lat)


def _kv_bwd(dkp, dv, dx_direct, res, kvmod, kvf, p, tabs):
    shift, scale = kvmod
    x, h, ckv, lat = res
    S = x.shape[0]
    dkn, dkr, dgn, dgr = _rw_bwd("kv_k_b", make_f_qk(True), _qk_rows(kvf, True, ckv) + [Row(tabs[0]), Row(tabs[1])],
                                 [p["k_gn"], p["k_gr"], p["pm"]], [Row(dkp, splits=[DH] * (2 * NH))],
                                 [True] * (NH + 1) + [False, False], [True, True, False],
                                 [((DH,) * NH, BF16), ((LANE,), BF16)], _tm(S, 2 * D))
    dkvf = jnp.concatenate([dkn, dv.astype(BF16)], axis=1)
    dlat = _matmul("kv_ukv_bi", dkvf, p["w_ukv"], "nt")
    dw_ukv = _matmul("kv_ukv_bw", lat, dkvf, "tn")
    dcl, dlg = _rw_bwd("kv_lat_b", f_rms, [Row(ckv, KVL)], [p["kv_lat_g"]], [Row(dlat)], [True], [True],
                       [((KVL,), BF16)], _tm(S, KVL))
    dckv = jnp.concatenate([dcl, dkr], axis=1)
    dh = _matmul("kv_dkv_bi", dckv, p["w_dkv"], "nt")
    dw_dkv = _matmul("kv_dkv_bw", h, dckv, "tn")
    dx, (dg, dshift, dscale) = _mod_bwd("kv", x, p["kv_norm_g"], shift, scale, dh, dx_direct)
    return dx, dict(w_dkv=dw_dkv, w_ukv=dw_ukv, kv_lat_g=dlg, k_gn=dgn, k_gr=dgr, kv_norm_g=dg, mod=(dshift, dscale))


def _mla_layer_fwd(tag, x, mod3, g, p, kp, kvf, tabs):
    shift, scale, gate = mod3
    S = x.shape[0]
    h = _mod_fwd(tag, x, g, shift, scale)
    ql = _matmul(tag + "_dq", h, p["w_dq"])
    qln = _rw_fwd(tag + "_qln", f_rms, [Row(ql)], [p["ql_g"]], [((QL,), BF16)], _tm(S, QL))[0]
    qu = _matmul(tag + "_uq", qln, p["w_uq"])
    qp = _rw_fwd(tag + "_q", make_f_qk(False), _qk_rows(qu, False) + [Row(tabs[0]), Row(tabs[1])],
                 [p["q_gn"], p["q_gr"], p["pm"]], [((DH,) * (2 * NH), BF16)], _tm(S, 2 * D))[0]
    o, lse = _flash_fwd(tag + "_att", qp, kp, kvf)
    y = _matmul(tag + "_out", o, p["w_out"])
    xn = _res_fwd(tag, x, y, gate, 1.0)
    return xn, (x, h, ql, qln, qu, qp, o, lse, y)


def _mla_layer_bwd(tag, dxn, res, mod3, g, p, kp, kvf, tabs):
    shift, scale, gate = mod3
    x, h, ql, qln, qu, qp, o, lse, y = res
    S = x.shape[0]
    dy, dgate = _res_bwd(tag, y, gate, dxn, 1.0)
    do = _matmul(tag + "_out_bi", dy, p["w_out"], "nt")
    dw_out = _matmul(tag + "_out_bw", o, dy, "tn")
    dqp, delta = _flash_bwd_dq(tag + "_att_bq", qp, kp, kvf, o, do, lse)
    dkp, dv = _flash_bwd_dkv(tag + "_att_bkv", qp, kp, kvf, do, lse.reshape(NH, 1, S), delta.reshape(NH, 1, S))
    dqu, dgn, dgr = _rw_bwd(tag + "_q_b", make_f_qk(False), _qk_rows(qu, False) + [Row(tabs[0]), Row(tabs[1])],
                            [p["q_gn"], p["q_gr"], p["pm"]], [Row(dqp, splits=[DH] * (2 * NH))],
                            [True] * (2 * NH) + [False, False], [True, True, False],
                            [((DH,) * (2 * NH), BF16)], _tm(S, 2 * D))
    dqln = _matmul(tag + "_uq_bi", dqu, p["w_uq"], "nt")
    dw_uq = _matmul(tag + "_uq_bw", qln, dqu, "tn")
    dql, dqlg = _rw_bwd(tag + "_qln_b", f_rms, [Row(ql)], [p["ql_g"]], [Row(dqln)], [True], [True], [((QL,), BF16)],
                        _tm(S, QL))
    dh = _matmul(tag + "_dq_bi", dql, p["w_dq"], "nt")
    dw_dq = _matmul(tag + "_dq_bw", h, dql, "tn")
    dx, (dg, dshift, dscale) = _mod_bwd(tag, x, g, shift, scale, dh, dxn)
    return dx, dkp, dv, dict(w_dq=dw_dq, w_uq=dw_uq, w_out=dw_out, ql_g=dqlg, q_gn=dgn, q_gr=dgr, g=dg,
                             mod=(dshift, dscale, dgate))


def _loss_head(y, tgt):
    S = y.shape[0]

    def fn(pieces, bvals):
        e = pieces[0] - pieces[1]
        part = jnp.sum(e * e) * (0.5 / D)
        return [e * (1.0 / D)], [jnp.full((1, LANE), part, F32)]
    dy, part = _rowwise("loss", fn, [Row(y), Row(tgt)], [], [((D,), F32)], [(1, LANE)], _tm(S, D))
    return part[0, 0], dy


def _rope_tables(positions):
    S = positions.shape[0]
    half = ROPE // 2
    lane = lax.broadcasted_iota(jnp.int32, (1, LANE), 1)
    inv_freq = ROPE_BASE ** (-(lane % half).astype(F32) / half)
    live = (lane < ROPE).astype(F32)
    sign = jnp.where(lane < half, -1.0, 1.0) * live

    def fn(pieces, bvals):
        ang = pieces[0] * bvals[0]
        return [jnp.cos(ang) * bvals[1], jnp.sin(ang) * bvals[2]], []
    pos = jnp.broadcast_to(positions.astype(F32)[:, None], (S, LANE))
    cosp, sins = _rowwise("rope_tab", fn, [Row(pos)], [inv_freq, live, sign], [((LANE,), F32)] * 2, [], _tm(S, LANE))
    r = lax.broadcasted_iota(jnp.int32, (LANE, LANE), 0)
    c = lax.broadcasted_iota(jnp.int32, (LANE, LANE), 1)
    pm = (((c < half) & (r == c + half)) | ((c >= half) & (c < ROPE) & (r == c - half))).astype(F32)
    return (cosp, sins), pm


def _adamw(name, w, g, m, v):
    shape = w.shape
    C = shape[-1]
    R = w.size // C
    tr = R
    for t in (1024, 512, 256, 128, 64, 32, 16, 8):
        if R % t == 0 and t * C * 4 <= (1 << 21):
            tr = t
            break
    c1 = 1.0 - ADAM_B1 ** ADAM_STEP
    c2 = 1.0 - ADAM_B2 ** ADAM_STEP

    def body(w_ref, g_ref, m_ref, v_ref, d_ref, mo_ref, vo_ref):
        gg = g_ref[...]
        mn = ADAM_B1 * m_ref[...] + (1.0 - ADAM_B1) * gg
        vn = ADAM_B2 * v_ref[...] + (1.0 - ADAM_B2) * (gg * gg)
        d_ref[...] = -ADAM_LR * ((mn / c1) / (jnp.sqrt(vn / c2) + ADAM_EPS) + ADAM_WD * w_ref[...])
        mo_ref[...] = mn
        vo_ref[...] = vn

    spec = pl.BlockSpec((tr, C), lambda i: (i, 0))
    outs = pl.pallas_call(body, name=name, grid=(R // tr,), in_specs=[spec] * 4, out_specs=[spec] * 3,
                          out_shape=[jax.ShapeDtypeStruct((R, C), F32)] * 3,
                          compiler_params=_cparams(("parallel",)))(*[t.reshape(R, C) for t in (w, g, m, v)])
    return [o.reshape(shape) for o in outs]


HBM_SPEC = pl.BlockSpec(memory_space=pltpu.HBM)
OTHER_CHIPS = (4, 2, 6)
SIBLING = 1


def _me():
    return lax.axis_index("x"), lax.axis_index("y"), lax.axis_index("c")


def _peer(me, k):
    mx, my, mc = me
    return ((1 - mx) if k & 4 else mx, (1 - my) if k & 2 else my, (1 - mc) if k & 1 else mc)


def _rcopy(src, dst, ssem, rsem, to):
    return pltpu.make_async_remote_copy(src_ref=src, dst_ref=dst, send_sem=ssem, recv_sem=rsem, device_id=to,
                                        device_id_type=MESH)


def _all_gather8(name, x):
    def body(x_ref, o_ref, ssem, rsem, lsem):
        me = _me()
        mine = 4 * me[0] + 2 * me[1] + me[2]
        loc = pltpu.make_async_copy(x_ref, o_ref.at[mine], lsem)
        loc.start()
        sends = []
        for k in range(1, 8):
            cp = _rcopy(x_ref, o_ref.at[mine], ssem.at[k - 1], rsem.at[k - 1], _peer(me, k))
            cp.start()
            sends.append(cp)
        for k in range(1, 8):
            px, py, pc = _peer(me, k)
            _rcopy(x_ref, o_ref.at[4 * px + 2 * py + pc], ssem.at[k - 1], rsem.at[k - 1], (px, py, pc)).wait_recv()
        for cp in sends:
            cp.wait_send()
        loc.wait()

    return pl.pallas_call(body, name=name, out_shape=jax.ShapeDtypeStruct((8,) + x.shape, x.dtype),
                          in_specs=[HBM_SPEC], out_specs=HBM_SPEC,
                          scratch_shapes=[pltpu.SemaphoreType.DMA((7,)), pltpu.SemaphoreType.DMA((7,)),
                                          pltpu.SemaphoreType.DMA(())])(x)


PACK_L = 1024
PACK_RT = 256


def _place_shard(name, wp, chip):
    rh, ln = wp.shape[1:]

    def body(s_ref, w_ref, o_ref):
        o_ref[...] = w_ref[...]

    gs = pltpu.PrefetchScalarGridSpec(
        num_scalar_prefetch=1, grid=(2, rh // PACK_RT),
        in_specs=[pl.BlockSpec((None, PACK_RT, ln), lambda h, i, s_ref: (h, i, 0))],
        out_specs=pl.BlockSpec((None, None, PACK_RT, ln), lambda h, i, s_ref: (s_ref[0], h, i, 0)))
    return pl.pallas_call(body, name=name, grid_spec=gs, out_shape=jax.ShapeDtypeStruct((4,) + wp.shape, wp.dtype),
                          compiler_params=_cparams(("parallel", "parallel")))(chip.reshape(1).astype(jnp.int32), wp)


def _gather_weights(name, w4):
    r2 = w4.shape[2] // 2

    def body(w_ref, o_ref, ssem, rsem):
        me = _me()
        mc = me[2]
        px, py, pd, sib = _peer(me, 4), _peer(me, 2), _peer(me, 6), _peer(me, SIBLING)
        chip = lambda p: 2 * p[0] + p[1]
        mine, from_x, from_y, from_d = (o_ref.at[chip(p), mc] for p in (me, px, py, pd))
        q0, q1 = pl.ds(0, r2), pl.ds(r2, r2)
        sends = [_rcopy(mine, mine, ssem.at[0], rsem.at[0], px), _rcopy(mine, mine, ssem.at[1], rsem.at[1], py)]
        for cp in sends:
            cp.start()
        _rcopy(from_x, from_x, ssem.at[0], rsem.at[0], px).wait_recv()
        sends += [_rcopy(from_x.at[q1], from_x.at[q1], ssem.at[2], rsem.at[2], py),
                  _rcopy(from_x, from_x, ssem.at[4], rsem.at[4], sib)]
        sends[-2].start()
        sends[-1].start()
        _rcopy(from_y, from_y, ssem.at[1], rsem.at[1], py).wait_recv()
        sends += [_rcopy(from_y.at[q0], from_y.at[q0], ssem.at[3], rsem.at[3], px),
                  _rcopy(from_y, from_y, ssem.at[5], rsem.at[5], sib)]
        sends[-2].start()
        sends[-1].start()
        _rcopy(from_d.at[q0], from_d.at[q0], ssem.at[3], rsem.at[3], px).wait_recv()
        _rcopy(from_d.at[q1], from_d.at[q1], ssem.at[2], rsem.at[2], py).wait_recv()
        sends.append(_rcopy(from_d, from_d, ssem.at[6], rsem.at[6], sib))
        sends[-1].start()
        for j, p in enumerate((px, py, pd)):
            land = o_ref.at[chip(p), 1 - mc]
            _rcopy(land, land, ssem.at[4 + j], rsem.at[4 + j], sib).wait_recv()
        for cp in sends:
            cp.wait_send()

    return pl.pallas_call(body, name=name, out_shape=jax.ShapeDtypeStruct(w4.shape, w4.dtype),
                          in_specs=[HBM_SPEC], out_specs=HBM_SPEC, input_output_aliases={0: 0},
                          scratch_shapes=[pltpu.SemaphoreType.DMA((7,)), pltpu.SemaphoreType.DMA((7,))])(w4)


def _exchange_half(name, g):
    def body(g_ref, p_ref, ssem, rsem):
        me = _me()
        cps = []
        for s in range(4):
            cp = _rcopy(g_ref.at[s, 1 - me[2]], p_ref.at[s], ssem.at[s], rsem.at[s], _peer(me, SIBLING))
            cp.start()
            cps.append(cp)
        for cp in cps:
            cp.wait()

    return pl.pallas_call(body, name=name, out_shape=jax.ShapeDtypeStruct((4,) + g.shape[2:], g.dtype),
                          in_specs=[HBM_SPEC], out_specs=HBM_SPEC,
                          scratch_shapes=[pltpu.SemaphoreType.DMA((4,)), pltpu.SemaphoreType.DMA((4,))])(g)


def _scatter_chips(name, q):
    r2 = q.shape[1] // 2

    def body(q_ref, t_ref, relay, ssem, rsem):
        me = _me()
        px, py, pd = _peer(me, 4), _peer(me, 2), _peer(me, 6)
        chip = lambda p: 2 * p[0] + p[1]
        q0, q1 = pl.ds(0, r2), pl.ds(r2, r2)
        sends = [_rcopy(q_ref.at[chip(px)], t_ref.at[0], ssem.at[0], rsem.at[0], px),
                 _rcopy(q_ref.at[chip(py)], t_ref.at[1], ssem.at[1], rsem.at[1], py),
                 _rcopy(q_ref.at[chip(pd), q0], relay.at[0], ssem.at[2], rsem.at[2], py),
                 _rcopy(q_ref.at[chip(pd), q1], relay.at[1], ssem.at[3], rsem.at[3], px)]
        for cp in sends:
            cp.start()
        _rcopy(relay.at[0], relay.at[0], ssem.at[2], rsem.at[2], py).wait_recv()
        sends.append(_rcopy(relay.at[0], t_ref.at[2, q0], ssem.at[4], rsem.at[4], px))
        sends[-1].start()
        _rcopy(relay.at[1], relay.at[1], ssem.at[3], rsem.at[3], px).wait_recv()
        sends.append(_rcopy(relay.at[1], t_ref.at[2, q1], ssem.at[5], rsem.at[5], py))
        sends[-1].start()
        _rcopy(t_ref.at[0], t_ref.at[0], ssem.at[0], rsem.at[0], px).wait_recv()
        _rcopy(t_ref.at[1], t_ref.at[1], ssem.at[1], rsem.at[1], py).wait_recv()
        _rcopy(t_ref.at[2, q0], t_ref.at[2, q0], ssem.at[4], rsem.at[4], px).wait_recv()
        _rcopy(t_ref.at[2, q1], t_ref.at[2, q1], ssem.at[5], rsem.at[5], py).wait_recv()
        for cp in sends:
            cp.wait_send()

    return pl.pallas_call(body, name=name,
                          out_shape=[jax.ShapeDtypeStruct((3,) + q.shape[1:], q.dtype),
                                     jax.ShapeDtypeStruct((2, r2) + q.shape[2:], q.dtype)],
                          in_specs=[HBM_SPEC], out_specs=[HBM_SPEC, HBM_SPEC],
                          scratch_shapes=[pltpu.SemaphoreType.DMA((6,)), pltpu.SemaphoreType.DMA((6,))])(q)[0]


def _exchange_full(name, r2):
    def body(r_ref, o_ref, ssem, rsem):
        me = _me()
        mc = me[2]
        cp = _rcopy(o_ref.at[mc], o_ref.at[mc], ssem, rsem, _peer(me, SIBLING))
        cp.start()
        _rcopy(o_ref.at[1 - mc], o_ref.at[1 - mc], ssem, rsem, _peer(me, SIBLING)).wait_recv()
        cp.wait_send()

    return pl.pallas_call(body, name=name, out_shape=jax.ShapeDtypeStruct(r2.shape, r2.dtype),
                          in_specs=[HBM_SPEC], out_specs=HBM_SPEC, input_output_aliases={0: 0},
                          scratch_shapes=[pltpu.SemaphoreType.DMA(()), pltpu.SemaphoreType.DMA(())])(r2)


def _add_half(name, g, p, c):
    rh, ln = g.shape[2:]

    def body(c_ref, g_ref, p_ref, o_ref):
        o_ref[0] = (g_ref[0, 0] + p_ref[0]).astype(o_ref.dtype)

    gs = pltpu.PrefetchScalarGridSpec(
        num_scalar_prefetch=1, grid=(4, rh // PACK_RT),
        in_specs=[pl.BlockSpec((1, 1, PACK_RT, ln), lambda s, i, c_ref: (s, c_ref[0], i, 0)),
                  pl.BlockSpec((1, PACK_RT, ln), lambda s, i, c_ref: (s, i, 0))],
        out_specs=pl.BlockSpec((1, PACK_RT, ln), lambda s, i, c_ref: (s, i, 0)))
    return pl.pallas_call(body, name=name, grid_spec=gs, out_shape=jax.ShapeDtypeStruct((4, rh, ln), BF16),
                          compiler_params=_cparams(("parallel", "parallel")))(c.reshape(1).astype(jnp.int32), g, p)


def _add_chips(name, q, t, chip, c):
    rh, ln = q.shape[1:]

    def body(s_ref, c_ref, q_ref, t_ref, o_ref):
        o_ref[...] = ((q_ref[0].astype(F32) + t_ref[0].astype(F32)) + t_ref[1].astype(F32)) + t_ref[2].astype(F32)

    gs = pltpu.PrefetchScalarGridSpec(
        num_scalar_prefetch=2, grid=(rh // PACK_RT,),
        in_specs=[pl.BlockSpec((1, PACK_RT, ln), lambda i, s_ref, c_ref: (s_ref[0], i, 0)),
                  pl.BlockSpec((3, PACK_RT, ln), lambda i, s_ref, c_ref: (0, i, 0))],
        out_specs=pl.BlockSpec((None, PACK_RT, ln), lambda i, s_ref, c_ref: (c_ref[0], i, 0)))
    return pl.pallas_call(body, name=name, grid_spec=gs, out_shape=jax.ShapeDtypeStruct((2, rh, ln), F32),
                          compiler_params=_cparams(("parallel",)))(chip.reshape(1).astype(jnp.int32),
                                                                    c.reshape(1).astype(jnp.int32), q, t)


def _sum8(name, a):
    def body(a_ref, o_ref):
        acc = a_ref[0]
        for d in range(1, 8):
            acc = acc + a_ref[d]
        o_ref[...] = acc
    return pl.pallas_call(body, name=name, out_shape=jax.ShapeDtypeStruct(a.shape[1:], F32))(a)


def _silu_rows(name, a):
    def body(a_ref, o_ref):
        o_ref[...] = _silu(a_ref[...])
    return pl.pallas_call(body, name=name, out_shape=jax.ShapeDtypeStruct(a.shape, F32))(a)


REST = (("gdn_w_out", 1), ("mla_w_dkv", 0), ("mla_w_ukv", 1), ("mla_w_dq", 1), ("mla_w_uq", 2), ("mla_w_out", 1))


def _packed_rows(n):
    per_half = -(-n // (2 * PACK_L))
    return -(-per_half // PACK_RT) * PACK_RT


def _pack_flat(flat):
    n = flat.shape[-1]
    rh = _packed_rows(n)
    pad = [(0, 0)] * (flat.ndim - 1) + [(0, 2 * rh * PACK_L - n)]
    return jnp.pad(flat, pad).reshape(flat.shape[:-1] + (2, rh, PACK_L))


def _shards_first(full, axis):
    sh = full.shape
    t = full.reshape(sh[:axis] + (4, sh[axis] // 4) + sh[axis + 1:])
    return jnp.moveaxis(t, axis, 0)


def _shards_merge(stacked, axis):
    t = jnp.moveaxis(stacked, 0, axis)
    sh = t.shape
    return t.reshape(sh[:axis] + (4 * sh[axis + 1],) + sh[axis + 2:])


def _pack_small(parts):
    flat = jnp.concatenate([p.reshape(-1).astype(F32) for p in parts])
    n = flat.shape[0]
    rows = -(-n // (SUB * LANE)) * SUB
    return jnp.pad(flat, (0, rows * LANE - n)).reshape(rows, LANE)


def _unpack_small(buf, shapes):
    lead = buf.shape[:-2]
    flat = buf.reshape(lead + (-1,))
    out, off = [], 0
    for sh in shapes:
        n = 1
        for d in sh:
            n *= d
        out.append(flat[..., off:off + n].reshape(lead + tuple(sh)))
        off += n
    return out


WEIGHTS = ('ada_w', 'ada_b', 'norm_g', 'ffn_w_in', 'ffn_w_out', 'gdn_w_in', 'gdn_conv_w', 'gdn_a_log', 'gdn_dt_bias',
           'gdn_norm_g', 'gdn_w_out', 'kv_ada_w', 'kv_ada_b', 'kv_norm_g', 'mla_w_dkv', 'mla_kv_norm_g', 'mla_w_ukv',
           'mla_k_norm_g', 'mla_w_dq', 'mla_q_lora_norm_g', 'mla_w_uq', 'mla_q_norm_g', 'mla_w_out')
ARGS = ('x', 'c', 'positions') + WEIGHTS + ('loss_target',) + tuple('m_' + n for n in WEIGHTS) + tuple('v_' + n for n in WEIGHTS)


def _split_norm(v):
    return v[None, :DH], _pad_lanes(v[None, DH:], 0)


def _join_norm(gn, gr):
    return jnp.concatenate([gn[0], gr[0, :ROPE]])


def _step(x, tgt, pos, mods, kvmod, W, P):
    tabs, pm = _rope_tables(pos)
    m3 = lambda l, i: tuple(mods[l][3 * i + j][None] for j in range(3))
    ng = lambda l, i: P["norm_g"][l, i][None]
    gdn_p, mla_p = [], []
    for l in range(2):
        gdn_p.append(dict(w_in=jnp.pad(W["gdn_w_in"][l], ((0, 0), (0, GDN_IN - W["gdn_w_in"].shape[2]))),
                          conv_w8=jnp.pad(P["gdn_conv_w"][l], ((0, 4), (0, 0))),
                          a_log128=_pad_lanes(P["gdn_a_log"][l][None], NH), dt_bias128=_pad_lanes(P["gdn_dt_bias"][l][None], NH),
                          norm_g=P["gdn_norm_g"][l][None], w_out=W["gdn_w_out"][l]))
        q_gn, q_gr = _split_norm(P["mla_q_norm_g"][l])
        mla_p.append(dict(w_dq=W["mla_w_dq"][l], ql_g=P["mla_q_lora_norm_g"][l][None],
                          w_uq=jnp.pad(W["mla_w_uq"][l].reshape(QL, NH, QKH), ((0, 0), (0, 0), (0, HP - QKH))).reshape(QL, NH * HP),
                          q_gn=q_gn, q_gr=q_gr, pm=pm, w_out=W["mla_w_out"][l]))
    k_gn, k_gr = _split_norm(P["mla_k_norm_g"])
    kv_p = dict(kv_norm_g=P["kv_norm_g"][None], w_dkv=jnp.pad(W["mla_w_dkv"], ((0, 0), (0, QL - KVL - ROPE))),
                kv_lat_g=P["mla_kv_norm_g"][None],
                w_ukv=W["mla_w_ukv"].reshape(KVL, NH, 2, DH).transpose(0, 2, 1, 3).reshape(KVL, 2 * NH * DH),
                k_gn=k_gn, k_gr=k_gr, pm=pm)
    kvm = (kvmod[0][None], kvmod[1][None])

    res = {}
    for l in range(4):
        x, res[l, 0] = _ffn_fwd(f"l{l}a", x, m3(l, 0), ng(l, 0), W["ffn_w_in"], W["ffn_w_out"], 2 * l)
        if l < 2:
            x, res[l, 1] = _gdn_layer_fwd(f"l{l}g", x, m3(l, 1), ng(l, 1), gdn_p[l])
        else:
            x, res[l, 1] = _mla_layer_fwd(f"l{l}m", x, m3(l, 1), ng(l, 1), mla_p[l - 2], kp, kvf, tabs)
        x, res[l, 2] = _ffn_fwd(f"l{l}b", x, m3(l, 2), ng(l, 2), W["ffn_w_in"], W["ffn_w_out"], 2 * l + 1)
        if l == 1:
            kp, kvf, kres = _kv_fwd(x, kvm, kv_p, tabs)
    loss, dx = _loss_head(x, tgt)

    gw = {n: [None] * W[n].shape[0] for n in ("gdn_w_in", "gdn_w_out", "mla_w_dq", "mla_w_uq", "mla_w_out")}
    g_in4 = g_out4 = None
    gp = {n: [None] * 2 for n in ("gdn_conv_w", "gdn_a_log", "gdn_dt_bias", "gdn_norm_g", "mla_q_lora_norm_g", "mla_q_norm_g")}
    gnorm = [[None] * 3 for _ in range(4)]
    dmod = [[None] * NMOD for _ in range(4)]
    dkp = dv = None
    for l in (3, 2, 1, 0):
        if l == 1:
            dx, gk = _kv_bwd(dkp, dv, dx, kres, kvm, kvf, kv_p, tabs)
        for i in (2, 1, 0):
            if i != 1:
                dx, g_in4, g_out4, gd = _ffn_bwd(f"l{l}{'ab'[i // 2]}", dx, res[l, i], m3(l, i), ng(l, i), W["ffn_w_in"],
                                                 W["ffn_w_out"], 2 * l + i // 2, g_in4, g_out4)
            elif l < 2:
                dx, gd = _gdn_layer_bwd(f"l{l}g", dx, res[l, 1], m3(l, 1), ng(l, 1), gdn_p[l])
                gw["gdn_w_in"][l] = gd["w_in"][:, :W["gdn_w_in"].shape[2]]
                gw["gdn_w_out"][l] = gd["w_out"]
                gp["gdn_conv_w"][l] = gd["conv_w8"][:4]
                gp["gdn_a_log"][l] = gd["a_log128"][0, NH:2 * NH]
                gp["gdn_dt_bias"][l] = gd["dt_bias128"][0, NH:2 * NH]
                gp["gdn_norm_g"][l] = gd["norm_g"][0]
            else:
                dx, dkp_l, dv_l, gd = _mla_layer_bwd(f"l{l}m", dx, res[l, 1], m3(l, 1), ng(l, 1), mla_p[l - 2], kp, kvf, tabs)
                dkp = dkp_l if dkp is None else dkp + dkp_l
                dv = dv_l if dv is None else dv + dv_l
                gw["mla_w_dq"][l - 2], gw["mla_w_out"][l - 2] = gd["w_dq"], gd["w_out"]
                gw["mla_w_uq"][l - 2] = gd["w_uq"].reshape(QL, NH, HP)[:, :, :QKH].reshape(QL, NH * QKH)
                gp["mla_q_lora_norm_g"][l - 2] = gd["ql_g"][0]
                gp["mla_q_norm_g"][l - 2] = _join_norm(gd["q_gn"], gd["q_gr"])
            gnorm[l][i] = gd["g"][0]
            for j in range(3):
                dmod[l][3 * i + j] = gd["mod"][j][0]
    gwf = {n: jnp.stack(v) for n, v in gw.items()}
    gwf["ffn_w_in"], gwf["ffn_w_out"] = g_in4, g_out4
    gwf["mla_w_dkv"] = gk["w_dkv"][:, :KVL + ROPE]
    gwf["mla_w_ukv"] = gk["w_ukv"].reshape(KVL, 2, NH, DH).transpose(0, 2, 1, 3).reshape(KVL, 2 * NH * DH)
    gpf = {n: jnp.stack(v) for n, v in gp.items()}
    gpf["norm_g"] = jnp.stack([jnp.stack(r) for r in gnorm])
    gpf["kv_norm_g"] = gk["kv_norm_g"][0]
    gpf["mla_kv_norm_g"] = gk["kv_lat_g"][0]
    gpf["mla_k_norm_g"] = _join_norm(gk["k_gn"], gk["k_gr"])
    dmods = jnp.stack([jnp.stack(r) for r in dmod])
    dkvmod = jnp.stack([gk["mod"][0][0], gk["mod"][1][0]])
    return loss, dx, gwf, gpf, dmods, dkvmod


SMALL = ("norm_g", "gdn_conv_w", "gdn_a_log", "gdn_dt_bias", "gdn_norm_g", "kv_norm_g", "mla_kv_norm_g", "mla_k_norm_g",
         "mla_q_lora_norm_g", "mla_q_norm_g")


def kernel(x, c, positions, ada_w, ada_b, norm_g, ffn_w_in, ffn_w_out, gdn_w_in, gdn_conv_w, gdn_a_log, gdn_dt_bias,
           gdn_norm_g, gdn_w_out, kv_ada_w, kv_ada_b, kv_norm_g, mla_w_dkv, mla_kv_norm_g, mla_w_ukv, mla_k_norm_g,
           mla_w_dq, mla_q_lora_norm_g, mla_w_uq, mla_q_norm_g, mla_w_out, loss_target, m_ada_w, m_ada_b, m_norm_g,
           m_ffn_w_in, m_ffn_w_out, m_gdn_w_in, m_gdn_conv_w, m_gdn_a_log, m_gdn_dt_bias, m_gdn_norm_g, m_gdn_w_out,
           m_kv_ada_w, m_kv_ada_b, m_kv_norm_g, m_mla_w_dkv, m_mla_kv_norm_g, m_mla_w_ukv, m_mla_k_norm_g, m_mla_w_dq,
           m_mla_q_lora_norm_g, m_mla_w_uq, m_mla_q_norm_g, m_mla_w_out, v_ada_w, v_ada_b, v_norm_g, v_ffn_w_in,
           v_ffn_w_out, v_gdn_w_in, v_gdn_conv_w, v_gdn_a_log, v_gdn_dt_bias, v_gdn_norm_g, v_gdn_w_out, v_kv_ada_w,
           v_kv_ada_b, v_kv_norm_g, v_mla_w_dkv, v_mla_kv_norm_g, v_mla_w_ukv, v_mla_k_norm_g, v_mla_w_dq,
           v_mla_q_lora_norm_g, v_mla_w_uq, v_mla_q_norm_g, v_mla_w_out):
    a = dict(locals())
    mx, my, mc = _me()
    dev = 4 * mx + 2 * my + mc
    chip = 2 * mx + my
    x, tgt, pos = a["x"][0], a["loss_target"][0], a["positions"][0]
    take = lambda arr, i, axis=0: lax.dynamic_index_in_dim(arr, i, axis, keepdims=False)

    pre = _all_gather8("ag_pre", _pack_small([a["c"], a["gdn_conv_w"], a["norm_g"]]))
    c_all, conv_sh, norm_sh = _unpack_small(pre, [(D,), a["gdn_conv_w"].shape, a["norm_g"].shape])
    P = {n: a[n] for n in SMALL}
    P["gdn_conv_w"] = jnp.concatenate([conv_sh[2 * s] for s in range(4)], axis=2)
    P["norm_g"] = jnp.concatenate([norm_sh[2 * s] for s in range(4)], axis=2)
    c_act = _silu_rows("c_act", c_all)
    nada = a["ada_w"].shape[2]
    nkv = a["kv_ada_w"].shape[1]
    modp = [_matmul(f"mod{l}", c_act, a["ada_w"], precise=True, lay="b_stack", li=l) for l in range(4)]
    kvp = _matmul("modkv", c_act, a["kv_ada_w"], precise=True)
    mp = _all_gather8("ag_mod", _pack_small(modp + [kvp]))
    modp_all, kvp_all = _unpack_small(mp, [(4, 8, nada), (8, nkv)])
    mods = jnp.concatenate([take(modp_all[2 * s], dev, 1) for s in range(4)], axis=1) + a["ada_b"]
    mods = mods.reshape(4, NMOD, D)
    kvmod = (jnp.concatenate([take(kvp_all[2 * s], dev, 0) for s in range(4)]) + a["kv_ada_b"]).reshape(2, D)

    def gather(tag, w2):
        return _gather_weights("ag_" + tag, _place_shard("own_" + tag, w2, chip))

    def reduce(tag, g4):
        q = _add_half("rsp_" + tag, g4, _exchange_half("rs1_" + tag, g4), mc)
        r2 = _add_chips("rsc_" + tag, q, _scatter_chips("rs2_" + tag, q), chip, mc)
        return _exchange_full("rs3_" + tag, r2)

    halves = lambda t: t.reshape((2, -1) + t.shape[-1:])
    W = {n: gather(t, halves(a[n].astype(BF16))).reshape((4, 8) + a[n].shape[2:])
         for n, t in (("ffn_w_in", "wi"), ("ffn_w_out", "wo"))}
    wg = gather("wg", a["gdn_w_in"].astype(BF16))
    W["gdn_w_in"] = jnp.concatenate([wg[s] for s in range(4)], axis=2)
    wall = gather("wr", _pack_flat(jnp.concatenate([a[n].reshape(-1).astype(BF16) for n, _ in REST]))).reshape(4, -1)
    off = 0
    for n, ax in REST:
        sz = a[n].size
        W[n] = _shards_merge(wall[:, off:off + sz].reshape((4,) + a[n].shape), ax)
        off += sz

    loss, dx, gw, gp, dmods, dkvmod = _step(x, tgt, pos, mods, kvmod, W, P)
    loss = lax.psum(loss, ("x", "y", "c"))

    grads = {n: reduce(t, gw[n].reshape((4, 2, -1) + a[n].shape[-1:])).reshape(a[n].shape)
             for n, t in (("ffn_w_in", "wi"), ("ffn_w_out", "wo"))}
    ng = a["gdn_w_in"].shape[2]
    grads["gdn_w_in"] = reduce("wg", jnp.stack([gw["gdn_w_in"][:, :, s * ng:(s + 1) * ng] for s in range(4)]))
    gsh = reduce("wr", _pack_flat(jnp.concatenate([_shards_first(gw[n], ax).reshape(4, -1) for n, ax in REST], axis=1)))
    gsh = gsh.reshape(-1)
    off = 0
    for n, _ in REST:
        grads[n] = gsh[off:off + a[n].size].reshape(a[n].shape)
        off += a[n].size

    small = _all_gather8("ag_small", _pack_small([dmods, dkvmod] + [gp[n] for n in SMALL]))
    shapes = [(4, NMOD * D), (2 * D,)] + [gp[n].shape for n in SMALL]
    dmod_all, dkv_all = _unpack_small(small, shapes)[:2]
    tot = _unpack_small(_sum8("sum_small", small), shapes)
    grads["ada_b"], grads["kv_ada_b"] = tot[0], tot[1]
    for n, t in zip(SMALL, tot[2:]):
        grads[n] = t
    grads["norm_g"] = lax.dynamic_slice_in_dim(grads["norm_g"], chip * a["norm_g"].shape[2], a["norm_g"].shape[2], 2)
    grads["gdn_conv_w"] = lax.dynamic_slice_in_dim(grads["gdn_conv_w"], chip * a["gdn_conv_w"].shape[2],
                                                   a["gdn_conv_w"].shape[2], 2)
    ca = jnp.pad(c_act, ((0, LANE - 8), (0, 0)))
    dm = jnp.pad(lax.dynamic_slice_in_dim(dmod_all.reshape(8, 4, NMOD * D), chip * nada, nada, 2), ((0, LANE - 8), (0, 0), (0, 0)))
    gada = None
    for l in range(4):
        gada = _matmul(f"gada{l}", ca, dm[:, l], "tn", precise=True, lay="o_stack", li=l, into=gada, nmat=4)
    grads["ada_w"] = gada
    dk = jnp.pad(lax.dynamic_slice_in_dim(dkv_all, chip * nkv, nkv, 1), ((0, LANE - 8), (0, 0)))
    grads["kv_ada_w"] = _matmul("gadakv", ca, dk, "tn", precise=True)

    upd = [_adamw("adamw_" + n, a[n], grads[n], a["m_" + n], a["v_" + n]) for n in WEIGHTS]
    return (loss, dx[None], *[grads[n] for n in WEIGHTS], *[u[0] for u in upd], *[u[1] for u in upd], *[u[2] for u in upd])
```

```python
import functools

import jax
import jax.numpy as jnp
from jax import lax
from jax.experimental import pallas as pl
from jax.experimental.pallas import tpu as pltpu

F32 = jnp.float32
BF16 = jnp.bfloat16
HI = lax.Precision.HIGHEST
MESH = pl.DeviceIdType.MESH

D = 1024
NH = 8
DH = 128
FF = 2816
NMOD = 9
CHUNK = 64
ROPE = 64
QKH = 192
HP = 256
KVL = 256
QL = 384
GDN_IN = 4224
GATE_CB = 32
EPS = 1e-6
ROPE_BASE = 10000.0
LANE = 128
SUB = 8
VMEM_LIMIT = 56 * 1024 * 1024

ADAM_LR, ADAM_B1, ADAM_B2, ADAM_EPS, ADAM_WD, ADAM_STEP = 0.001, 0.9, 0.999, 1e-08, 0.01, 10


def _tile(n, prefs=(512, 384, 256, 128)):
    for p in prefs:
        if n % p == 0:
            return p
    return n


def _cparams(sem):
    return pltpu.CompilerParams(dimension_semantics=sem, vmem_limit_bytes=VMEM_LIMIT)


class Row:
    def __init__(self, arr, width=None, cb=0, splits=None, halo=None):
        self.arr = arr
        self.width = arr.shape[1] if width is None else width
        self.cb = cb
        self.splits = splits
        self.halo = halo


def _rowwise(name, fn, rows, bcs, outs, accs, tm):
    S = rows[0].arr.shape[0]
    n = S // tm
    nr, nb, no, na = len(rows), len(bcs), len(outs), len(accs)

    def body(*refs):
        rrefs, brefs = refs[:nr], refs[nr:nr + nb]
        orefs, arefs = refs[nr + nb:nr + nb + no], refs[nr + nb + no:]
        pieces = []
        for r, ref in zip(rows, rrefs):
            if r.splits is None:
                pieces.append(ref[...])
            else:
                off = 0
                for w in r.splits:
                    pieces.append(ref[:, off:off + w])
                    off += w
        out_pieces, acc_vals = fn(pieces, [b[...] for b in brefs])
        k = 0
        for (widths, dt), oref in zip(outs, orefs):
            off = 0
            for w in widths:
                oref[:, off:off + w] = out_pieces[k].astype(dt)
                k += 1
                off += w
        if na:
            @pl.when(pl.program_id(0) == 0)
            def _():
                for a in arefs:
                    a[...] = jnp.zeros(a.shape, F32)
            for a, v in zip(arefs, acc_vals):
                a[...] += v

    in_specs = []
    for r in rows:
        if r.halo is None:
            in_specs.append(pl.BlockSpec((tm, r.width), lambda i, cb=r.cb: (i, cb)))
        elif r.halo == "prev":
            in_specs.append(pl.BlockSpec((SUB, r.width), lambda i, cb=r.cb: (jnp.maximum(i * (tm // SUB) - 1, 0), cb)))
        else:
            in_specs.append(pl.BlockSpec((SUB, r.width), lambda i, cb=r.cb: (jnp.minimum((i + 1) * (tm // SUB), S // SUB - 1), cb)))
    in_specs += [pl.BlockSpec(b.shape, lambda i, nd=b.ndim: (0,) * nd) for b in bcs]
    out_specs = [pl.BlockSpec((tm, sum(w)), lambda i: (i, 0)) for w, _ in outs]
    out_specs += [pl.BlockSpec(s, lambda i: (0, 0)) for s in accs]
    out_shape = [jax.ShapeDtypeStruct((S, sum(w)), dt) for w, dt in outs]
    out_shape += [jax.ShapeDtypeStruct(s, F32) for s in accs]
    res = pl.pallas_call(body, name=name, grid=(n,), in_specs=in_specs, out_specs=out_specs, out_shape=out_shape,
                         compiler_params=_cparams(("arbitrary",)))(*[r.arr for r in rows], *bcs)
    return res


def _rw_fwd(name, f, rows, bcs, outs, tm):
    def fn(pieces, bvals):
        return list(f(*[p.astype(F32) for p in pieces], *[b.astype(F32) for b in bvals])), []
    return _rowwise(name, fn, rows, bcs, outs, [], tm)


def _npieces(rows):
    return sum(1 if r.splits is None else len(r.splits) for r in rows)


def _rw_bwd(name, f, rows, bcs, cts, drow, dbc, outs, tm, add=None):
    np_, nct = _npieces(rows), _npieces(cts)

    def fn(pieces, bvals):
        allv = [p.astype(F32) for p in pieces[:np_]] + [b.astype(F32) for b in bvals]
        ct = [p.astype(F32) for p in pieces[np_:np_ + nct]]
        didx = [i for i, m in enumerate(list(drow) + list(dbc)) if m]

        def g(*dv):
            full = list(allv)
            for i, v in zip(didx, dv):
                full[i] = v
            return tuple(f(*full))

        _, vjp = jax.vjp(g, *[allv[i] for i in didx])
        grads = vjp(tuple(ct))
        nrd = sum(bool(m) for m in drow)
        rg, bg = list(grads[:nrd]), list(grads[nrd:])
        if add is not None:
            rg[0] = rg[0] + pieces[np_ + nct].astype(F32)
        return rg, bg

    accs = [b.shape for b, m in zip(bcs, dbc) if m]
    return _rowwise(name, fn, list(rows) + list(cts) + ([add] if add is not None else []), bcs, outs, accs, tm)


def _sigmoid(x):
    return 1.0 / (1.0 + jnp.exp(-x))


def _silu(x):
    return x * _sigmoid(x)


def _softplus(x):
    return jnp.maximum(x, 0.0) + jnp.log(1.0 + jnp.exp(-jnp.abs(x)))


def f_mod(x, g, shift, scale):
    y = x * lax.rsqrt(jnp.mean(x * x, axis=-1, keepdims=True) + EPS)
    return (y * g * (1.0 + scale) + shift,)


def f_rms(x, g):
    return (x * lax.rsqrt(jnp.mean(x * x, axis=-1, keepdims=True) + EPS) * g,)


def f_act(gate, up):
    return (_silu(gate) * up,)


def make_f_res(coef):
    def f_res(y, gate):
        return (coef * gate * y,)
    return f_res


def f_gdnpre(*p):
    out = []
    for i, t in enumerate(p):
        t = _silu(t)
        if i < 2 * NH:
            t = t * lax.rsqrt(jnp.sum(t * t, axis=-1, keepdims=True) + EPS)
        out.append(t)
    return tuple(out)


def f_gates(gates, a_log, dt_bias):
    return _sigmoid(gates), -jnp.exp(a_log) * _softplus(gates + dt_bias)


def f_gdnpost(*a):
    o, z, g = a[:NH], a[NH:2 * NH], a[2 * NH]
    out = []
    for oh, zh in zip(o, z):
        y = oh * lax.rsqrt(jnp.mean(oh * oh, axis=-1, keepdims=True) + EPS) * g
        out.append(y * _silu(zh))
    return tuple(out)


def make_f_qk(shared_rope):
    def f(*a):
        if shared_rope:
            ns, rs = a[:NH], [a[NH]] * NH
            cosp, sins, gn, gr, pm = a[NH + 1:NH + 6]
        else:
            ns, rs = a[0:2 * NH:2], a[1:2 * NH:2]
            cosp, sins, gn, gr, pm = a[2 * NH:2 * NH + 5]
        out = []
        for n, r in zip(ns, rs):
            ss = jnp.sum(n * n, axis=-1, keepdims=True) + jnp.sum(r * r, axis=-1, keepdims=True)
            rstd = lax.rsqrt(ss * (1.0 / QKH) + EPS)
            yn = n * rstd * gn
            yr = r * rstd * gr
            sw = jnp.dot(yr, pm, precision=HI, preferred_element_type=F32)
            out += [yn, yr * cosp + sw * sins]
        return tuple(out)
    return f


def _matmul(name, a, b, mode="nn", out_dtype=F32, precise=False, lay=None, li=0, into=None, nmat=1):
    if lay == "b_cols":
        per = b.shape[3]
        rb, cb = b.shape[2], 4 * per
    elif lay == "b_rows":
        per = b.shape[2]
        rb, cb = 4 * per, b.shape[3]
    elif lay == "b_stack":
        rb, cb = b.shape[1:]
    else:
        rb, cb = b.shape
    if mode == "nn":
        (M, K), N = a.shape, cb
    elif mode == "nt":
        (M, K), N = a.shape, rb
    else:
        (K, M), N = a.shape, cb
    tm = _tile(M, (1024, 512, 256, 128))
    tn = _tile(N, (1408, 1024, 512, 384, 256, 128))
    tk = _tile(K, (1408, 1024, 512, 384, 256, 128))
    if lay == "b_cols":
        tn, tk = (per, tk) if mode == "nn" else (tn, per)
    elif lay == "b_rows":
        tm, tn, tk = (tm, 512, K) if mode == "nn" else (min(tm, 512), N, tk)
    elif lay == "o_cols":
        per = N // 4
        tn = per
    elif lay == "o_rows":
        per = M // 4
        tm, tn = M, 512
    nk = K // tk
    dims = {"nn": (((1,), (0,)), ((), ())), "nt": (((1,), (1,)), ((), ())), "tn": (((0,), (0,)), ((), ()))}[mode]

    def body(a_ref, b_ref, *rest):
        o_ref, acc_ref = rest[-2:]
        k = pl.program_id(2)

        @pl.when(k == 0)
        def _():
            acc_ref[...] = jnp.zeros(acc_ref.shape, F32)

        bv = b_ref[...]
        if lay == "b_rows":
            bv = bv.reshape(4 * per, bv.shape[2])
        if precise:
            acc_ref[...] += lax.dot_general(a_ref[...].astype(F32), bv.astype(F32), dims, precision=HI,
                                            preferred_element_type=F32)
        else:
            acc_ref[...] += lax.dot_general(a_ref[...].astype(BF16), bv.astype(BF16), dims, preferred_element_type=F32)

        @pl.when(k == nk - 1)
        def _():
            if lay == "o_rows":
                for s in range(4):
                    o_ref[s] = acc_ref[s * per:(s + 1) * per, :].astype(o_ref.dtype)
            else:
                o_ref[...] = acc_ref[...].astype(o_ref.dtype)

    a_spec = pl.BlockSpec((tk, tm), lambda i, j, k: (k, i)) if mode == "tn" else pl.BlockSpec((tm, tk), lambda i, j, k: (i, k))
    if lay == "b_cols":
        b_spec = (pl.BlockSpec((None, None, tk, per), lambda i, j, k: (j, li, k, 0)) if mode == "nn" else
                  pl.BlockSpec((None, None, tn, per), lambda i, j, k: (k, li, j, 0)))
    elif lay == "b_rows":
        b_spec = (pl.BlockSpec((4, None, per, tn), lambda i, j, k: (0, li, 0, j)) if mode == "nn" else
                  pl.BlockSpec((4, None, per, tk), lambda i, j, k: (0, li, 0, k)))
    elif lay == "b_stack":
        b_spec = pl.BlockSpec((None, tk, tn), lambda i, j, k: (li, k, j))
    elif mode == "nt":
        b_spec = pl.BlockSpec((tn, tk), lambda i, j, k: (j, k))
    else:
        b_spec = pl.BlockSpec((tk, tn), lambda i, j, k: (k, j))
    if lay == "o_stack":
        o_spec = pl.BlockSpec((None, tm, tn), lambda i, j, k: (li, i, j))
        o_shape = jax.ShapeDtypeStruct((nmat, M, N), out_dtype)
    elif lay == "o_cols":
        o_spec = pl.BlockSpec((None, None, tm, per), lambda i, j, k: (j, li, i, 0))
        o_shape = jax.ShapeDtypeStruct((4, nmat, M, per), out_dtype)
    elif lay == "o_rows":
        o_spec = pl.BlockSpec((4, None, per, tn), lambda i, j, k: (0, li, 0, j))
        o_shape = jax.ShapeDtypeStruct((4, nmat, per, N), out_dtype)
    else:
        o_spec = pl.BlockSpec((tm, tn), lambda i, j, k: (i, j))
        o_shape = jax.ShapeDtypeStruct((M, N), out_dtype)
    in_specs, args, alias = [a_spec, b_spec], [a, b], {}
    if into is not None:
        in_specs.append(pl.BlockSpec(memory_space=pl.ANY))
        args.append(into)
        alias = {2: 0}
    return pl.pallas_call(body, name=name, grid=(M // tm, N // tn, nk), in_specs=in_specs, out_specs=o_spec,
                          out_shape=o_shape, scratch_shapes=[pltpu.VMEM((tm, tn), F32)], input_output_aliases=alias,
                          compiler_params=_cparams(("parallel", "parallel", "arbitrary")))(*args)


def _shift_down(t, p, d):
    if d == 0:
        return t
    tr = pltpu.roll(t, d, 0)
    pr = pltpu.roll(p, d, 0)
    r8 = lax.broadcasted_iota(jnp.int32, p.shape, 0)
    first = jnp.where(r8 < d, pr, tr[:SUB])
    return jnp.concatenate([first, tr[SUB:]], axis=0)


def _shift_up(t, nx, d):
    if d == 0:
        return t
    tm = t.shape[0]
    tr = pltpu.roll(t, tm - d, 0)
    nr = pltpu.roll(nx, SUB - d, 0)
    r8 = lax.broadcasted_iota(jnp.int32, nx.shape, 0)
    last = jnp.where(r8 >= SUB - d, nr, tr[tm - SUB:])
    return jnp.concatenate([tr[:tm - SUB], last], axis=0)


def _conv_fwd(name, proj, w8, C, tm):
    def fn(pieces, bvals):
        t, p = pieces[0].astype(F32), pieces[1].astype(F32)
        w = bvals[0]
        p = jnp.where(pl.program_id(0) == 0, 0.0, p)
        out = w[3:4] * t
        for d in (1, 2, 3):
            out = out + w[3 - d:4 - d] * _shift_down(t, p, d)
        return [out], []
    return _rowwise(name, fn, [Row(proj, C), Row(proj, C, halo="prev")], [w8], [((C,), F32)], [], tm)[0]


def _conv_bwd(name, proj, dout, w8, C, tm, out_dtype):
    n = proj.shape[0] // tm

    def fn(pieces, bvals):
        t, p, g, gn = [v.astype(F32) for v in pieces]
        w = bvals[0]
        i = pl.program_id(0)
        p = jnp.where(i == 0, 0.0, p)
        gn = jnp.where(i == n - 1, 0.0, gn)
        dx = w[3:4] * g
        dws = [jnp.sum(g * t, axis=0, keepdims=True)]
        for d in (1, 2, 3):
            dx = dx + w[3 - d:4 - d] * _shift_up(g, gn, d)
            dws.append(jnp.sum(g * _shift_down(t, p, d), axis=0, keepdims=True))
        dw = jnp.concatenate([dws[3], dws[2], dws[1], dws[0], jnp.zeros((4, g.shape[1]), F32)], axis=0)
        return [dx], [dw]
    return _rowwise(name, fn, [Row(proj, C), Row(proj, C, halo="prev"), Row(dout), Row(dout, halo="next")], [w8],
                    [((C,), out_dtype)], [(SUB, C)], tm)


def _bdot(a, b, ca, cb):
    return lax.dot_general(a.astype(BF16), b.astype(BF16), (((ca,), (cb,)), ((0,), (0,))), preferred_element_type=F32)


def _bdot3(a, b, ca, cb):
    dims = (((ca,), (cb,)), ((0,), (0,)))
    ah, bh = a.astype(BF16), b.astype(BF16)
    al, bl = (a - ah.astype(F32)).astype(BF16), (b - bh.astype(F32)).astype(BF16)
    d = lambda x, y: lax.dot_general(x, y, dims, preferred_element_type=F32)
    return d(ah, bh) + (d(ah, bl) + d(al, bh))


@jax.custom_vjp
def _bmm3(a, b):
    return _bdot3(a, b, 2, 1)


_bmm3.defvjp(lambda a, b: (_bdot3(a, b, 2, 1), (a, b)),
             lambda res, g: (_bdot3(g, res[1], 2, 2), _bdot3(res[0], g, 1, 1)))


def _neumann(nl):
    C = nl.shape[1]
    eye = (lax.broadcasted_iota(jnp.int32, (1, C, C), 1) == lax.broadcasted_iota(jnp.int32, (1, C, C), 2)).astype(F32)
    T = eye + nl
    pw = nl
    for _ in range(C.bit_length() - 2):
        pw = _bdot3(pw, pw, 2, 1)
        T = T + _bdot3(T, pw, 2, 1)
    return T


_unit_lower_inv = jax.custom_vjp(_neumann)


def _unit_lower_inv_fwd(nl):
    T = _neumann(nl)
    return T, T


def _unit_lower_inv_bwd(T, g):
    return (_bdot3(_bdot3(T, g, 1, 1), T, 2, 2),)


_unit_lower_inv.defvjp(_unit_lower_inv_fwd, _unit_lower_inv_bwd)


@jax.custom_vjp
def _known_inv(nl, T):
    return T


_known_inv.defvjp(lambda nl, T: (T, T), lambda T, g: (_unit_lower_inv_bwd(T, g)[0], jnp.zeros_like(T)))


def _gdn_chunk(q, k, v, gcol, grow, bcol, S, T_saved=None):
    C = CHUNK
    ii = lax.broadcasted_iota(jnp.int32, (1, C, C), 1)
    jj = lax.broadcasted_iota(jnp.int32, (1, C, C), 2)
    incl, strict = ii >= jj, ii > jj
    gc_col = jnp.sum(jnp.where(incl, 1.0, 0.0) * grow, axis=2, keepdims=True)
    gc_row = jnp.sum(jnp.where(jj >= ii, 1.0, 0.0) * gcol, axis=1, keepdims=True)
    decay = jnp.where(incl, jnp.exp(jnp.where(incl, gc_col - gc_row, 0.0)), 0.0)
    qs = q * (DH ** -0.5)
    kb = k * bcol
    nl = -jnp.where(strict, _bdot(kb, k, 2, 2) * decay, 0.0)
    T = _unit_lower_inv(nl) if T_saved is None else _known_inv(nl, T_saved)
    egc = jnp.exp(gc_col)
    u = _bmm3(T, v * bcol)
    w = _bmm3(T, kb * egc)
    att = jnp.where(incl, _bdot(qs, k, 2, 2) * decay, 0.0)
    v_new = u - _bdot(w, S, 2, 1)
    o = _bdot(qs * egc, S, 2, 1) + _bdot(att, v_new, 2, 1)
    g_last = jnp.sum(grow, axis=2, keepdims=True)
    k_dec = k * jnp.exp(g_last - gc_col)
    S_out = S * jnp.exp(g_last) + _bdot(k_dec, v_new, 1, 1)
    return o, S_out, T


def _heads(ref, w):
    return jnp.stack([ref[:, h * w:(h + 1) * w] for h in range(NH)])


def _gdn_specs(NC, rev):
    ix = (lambda i: NC - 1 - i) if rev else (lambda i: i)
    wide = pl.BlockSpec((CHUNK, D), lambda i: (ix(i), 0))
    col = pl.BlockSpec((CHUNK, NH), lambda i: (ix(i), 0))
    row = pl.BlockSpec((1, NH, CHUNK), lambda i: (ix(i), 0, 0))
    st = pl.BlockSpec((1, NH, DH, DH), lambda i: (ix(i), 0, 0, 0))
    tinv = pl.BlockSpec((1, NH, CHUNK, CHUNK), lambda i: (ix(i), 0, 0, 0))
    return wide, col, row, st, tinv


def _gdn_fwd(name, q, k, v, gcol, grow, bcol):
    S = q.shape[0]
    NC = S // CHUNK

    def body(q_ref, k_ref, v_ref, gc_ref, gr_ref, b_ref, o_ref, ss_ref, t_ref, st):
        @pl.when(pl.program_id(0) == 0)
        def _():
            st[...] = jnp.zeros(st.shape, F32)
        s_in = st[...]
        ss_ref[0] = s_in
        grow = jnp.stack([gr_ref[0, h:h + 1, :] for h in range(NH)])
        o, s_out, tinv = _gdn_chunk(_heads(q_ref, DH), _heads(k_ref, DH), _heads(v_ref, DH), _heads(gc_ref, 1), grow,
                                    _heads(b_ref, 1), s_in)
        for h in range(NH):
            o_ref[:, h * DH:(h + 1) * DH] = o[h]
        t_ref[0] = tinv
        st[...] = s_out

    wide, col, row, stsp, tsp = _gdn_specs(NC, False)
    return pl.pallas_call(body, name=name, grid=(NC,), in_specs=[wide, wide, wide, col, row, col],
                          out_specs=[wide, stsp, tsp],
                          out_shape=[jax.ShapeDtypeStruct((S, D), F32), jax.ShapeDtypeStruct((NC, NH, DH, DH), F32),
                                     jax.ShapeDtypeStruct((NC, NH, CHUNK, CHUNK), F32)],
                          scratch_shapes=[pltpu.VMEM((NH, DH, DH), F32)],
                          compiler_params=_cparams(("arbitrary",)))(q, k, v, gcol, grow, bcol)


def _gdn_bwd(name, q, k, v, gcol, grow, bcol, ssave, tsave, do):
    S = q.shape[0]
    NC = S // CHUNK

    def body(q_ref, k_ref, v_ref, gc_ref, gr_ref, b_ref, ss_ref, t_ref, do_ref, dq_ref, dk_ref, dv_ref, dgc_ref, dgr_ref,
             db_ref, dst):
        @pl.when(pl.program_id(0) == 0)
        def _():
            dst[...] = jnp.zeros(dst.shape, F32)
        grow = jnp.stack([gr_ref[0, h:h + 1, :] for h in range(NH)])
        prim = (_heads(q_ref, DH), _heads(k_ref, DH), _heads(v_ref, DH), _heads(gc_ref, 1), grow, _heads(b_ref, 1), ss_ref[0])
        tinv = t_ref[0]
        _, vjp = jax.vjp(lambda *p: _gdn_chunk(*p, T_saved=tinv)[:2], *prim)
        dq, dk, dv, dgc, dgr, db, ds = vjp((_heads(do_ref, DH), dst[...]))
        for h in range(NH):
            hs = slice(h * DH, (h + 1) * DH)
            dq_ref[:, hs] = dq[h]
            dk_ref[:, hs] = dk[h]
            dv_ref[:, hs] = dv[h]
            dgc_ref[:, h:h + 1] = dgc[h]
            dgr_ref[0, h:h + 1, :] = dgr[h]
            db_ref[:, h:h + 1] = db[h]
        dst[...] = ds

    wide, col, row, stsp, tsp = _gdn_specs(NC, True)
    return pl.pallas_call(body, name=name, grid=(NC,), in_specs=[wide, wide, wide, col, row, col, stsp, tsp, wide],
                          out_specs=[wide, wide, wide, col, row, col],
                          out_shape=[jax.ShapeDtypeStruct((S, D), F32)] * 3 + [jax.ShapeDtypeStruct((S, NH), F32),
                                                                                 jax.ShapeDtypeStruct((NC, NH, CHUNK), F32),
                                                                                 jax.ShapeDtypeStruct((S, NH), F32)],
                          scratch_shapes=[pltpu.VMEM((NH, DH, DH), F32)],
                          compiler_params=_cparams(("arbitrary",)))(q, k, v, gcol, grow, bcol, ssave, tsave, do)


TQ = 512
SM_SCALE = QKH ** -0.5
NEG = -1e30


def _diag_mask(transposed):
    r = lax.broadcasted_iota(jnp.int32, (TQ, TQ), 0) // CHUNK
    c = lax.broadcasted_iota(jnp.int32, (TQ, TQ), 1) // CHUNK
    return (r <= c) if transposed else (c <= r)


def _dot_nt(a, b):
    return lax.dot_general(a, b, (((1,), (1,)), ((), ())), preferred_element_type=F32)


def _flash_fwd(name, qp, kp, kv):
    S = qp.shape[0]
    nq = S // (2 * TQ)

    def body(q_ref, k_ref, v_ref, o_ref, lse_ref):
        qi = pl.program_id(1)
        qs = (q_ref[:TQ, :], q_ref[TQ:, :])

        def step(q, j, carry, masked):
            m, l, acc = carry
            rows = pl.ds(pl.multiple_of(j * TQ, TQ), TQ)
            s = _dot_nt(q, k_ref[rows, :]) * SM_SCALE
            if masked:
                s = jnp.where(_diag_mask(False), s, NEG)
            m_new = jnp.maximum(m, jnp.max(s, axis=-1, keepdims=True))
            p = jnp.exp(s - m_new)
            alpha = jnp.exp(m - m_new)
            l = alpha * l + jnp.sum(p, axis=-1, keepdims=True)
            acc = alpha * acc + jnp.dot(p.astype(BF16), v_ref[rows, :].astype(BF16), preferred_element_type=F32)
            return m_new, l, acc

        init = (jnp.full((TQ, 1), NEG, F32), jnp.zeros((TQ, 1), F32), jnp.zeros((TQ, DH), F32))
        ca, cb = lax.fori_loop(0, 2 * qi, lambda j, c: (step(qs[0], j, c[0], False), step(qs[1], j, c[1], False)),
                               (init, init))
        ca = step(qs[0], 2 * qi, ca, True)
        cb = step(qs[1], 2 * qi + 1, step(qs[1], 2 * qi, cb, False), True)
        for u, (m, l, acc) in enumerate((ca, cb)):
            o_ref[u * TQ:(u + 1) * TQ, :] = acc / l
            lse_ref[0, u * TQ:(u + 1) * TQ, :] = m + jnp.log(l)

    return pl.pallas_call(
        body, name=name, grid=(NH, nq),
        in_specs=[pl.BlockSpec((2 * TQ, HP), lambda h, i: (i, h)), pl.BlockSpec((S, HP), lambda h, i: (0, h)),
                  pl.BlockSpec((S, DH), lambda h, i: (0, NH + h))],
        out_specs=[pl.BlockSpec((2 * TQ, DH), lambda h, i: (i, h)), pl.BlockSpec((1, 2 * TQ, 1), lambda h, i: (h, i, 0))],
        out_shape=[jax.ShapeDtypeStruct((S, NH * DH), F32), jax.ShapeDtypeStruct((NH, S, 1), F32)],
        compiler_params=_cparams(("parallel", "arbitrary")))(qp, kp, kv)


def _flash_bwd_dq(name, qp, kp, kv, o, do, lse):
    S = qp.shape[0]
    nq = S // (2 * TQ)

    def body(q_ref, k_ref, v_ref, o_ref, do_ref, lse_ref, dq_ref, dl_ref):
        qi = pl.program_id(1)
        subs = []
        for u in range(2):
            sl = slice(u * TQ, (u + 1) * TQ)
            do = do_ref[sl, :]
            delta = jnp.sum(o_ref[sl, :] * do, axis=-1, keepdims=True)
            dl_ref[0, sl, :] = delta
            subs.append((q_ref[sl, :], do.astype(BF16), lse_ref[0, sl, :], delta))

        def step(sub, j, dq, masked):
            q, dob, lse, delta = sub
            rows = pl.ds(pl.multiple_of(j * TQ, TQ), TQ)
            k = k_ref[rows, :]
            s = _dot_nt(q, k) * SM_SCALE
            if masked:
                s = jnp.where(_diag_mask(False), s, NEG)
            p = jnp.exp(s - lse)
            dp = _dot_nt(dob, v_ref[rows, :].astype(BF16))
            ds = p * (dp - delta) * SM_SCALE
            return dq + jnp.dot(ds.astype(BF16), k, preferred_element_type=F32)

        zero = jnp.zeros((TQ, HP), F32)
        dqa, dqb = lax.fori_loop(0, 2 * qi, lambda j, c: (step(subs[0], j, c[0], False), step(subs[1], j, c[1], False)),
                                 (zero, zero))
        dq_ref[:TQ, :] = step(subs[0], 2 * qi, dqa, True)
        dq_ref[TQ:, :] = step(subs[1], 2 * qi + 1, step(subs[1], 2 * qi, dqb, False), True)

    return pl.pallas_call(
        body, name=name, grid=(NH, nq),
        in_specs=[pl.BlockSpec((2 * TQ, HP), lambda h, i: (i, h)), pl.BlockSpec((S, HP), lambda h, i: (0, h)),
                  pl.BlockSpec((S, DH), lambda h, i: (0, NH + h)), pl.BlockSpec((2 * TQ, DH), lambda h, i: (i, h)),
                  pl.BlockSpec((2 * TQ, DH), lambda h, i: (i, h)), pl.BlockSpec((1, 2 * TQ, 1), lambda h, i: (h, i, 0))],
        out_specs=[pl.BlockSpec((2 * TQ, HP), lambda h, i: (i, h)), pl.BlockSpec((1, 2 * TQ, 1), lambda h, i: (h, i, 0))],
        out_shape=[jax.ShapeDtypeStruct((S, NH * HP), F32), jax.ShapeDtypeStruct((NH, S, 1), F32)],
        compiler_params=_cparams(("parallel", "arbitrary")))(qp, kp, kv, o, do, lse)


def _flash_bwd_dkv(name, qp, kp, kv, do, lse_row, delta_row):
    S = qp.shape[0]
    nq = S // TQ

    def body(q_ref, k_ref, v_ref, do_ref, lse_ref, dl_ref, dk_ref, dv_ref):
        kj = pl.program_id(1)
        subs = [(k_ref[u * TQ:(u + 1) * TQ, :], v_ref[u * TQ:(u + 1) * TQ, :].astype(BF16)) for u in range(2)]

        def step(sub, i, carry, masked):
            k, vb = sub
            dk, dv = carry
            rows = pl.ds(pl.multiple_of(i * TQ, TQ), TQ)
            q = q_ref[rows, :]
            dob = do_ref[rows, :].astype(BF16)
            st = _dot_nt(k, q) * SM_SCALE
            pt = jnp.exp(st - lse_ref[0, :, rows])
            if masked:
                pt = jnp.where(_diag_mask(True), pt, 0.0)
            dv = dv + jnp.dot(pt.astype(BF16), dob, preferred_element_type=F32)
            dpt = _dot_nt(vb, dob)
            dst = pt * (dpt - dl_ref[0, :, rows]) * SM_SCALE
            dk = dk + jnp.dot(dst.astype(BF16), q, preferred_element_type=F32)
            return dk, dv

        zero = (jnp.zeros((TQ, HP), F32), jnp.zeros((TQ, DH), F32))
        ca = step(subs[0], 2 * kj + 1, step(subs[0], 2 * kj, zero, True), False)
        cb = step(subs[1], 2 * kj + 1, zero, True)
        ca, cb = lax.fori_loop(2 * kj + 2, nq, lambda i, c: (step(subs[0], i, c[0], False), step(subs[1], i, c[1], False)),
                               (ca, cb))
        for u, (dk, dv) in enumerate((ca, cb)):
            dk_ref[u * TQ:(u + 1) * TQ, :] = dk
            dv_ref[u * TQ:(u + 1) * TQ, :] = dv

    return pl.pallas_call(
        body, name=name, grid=(NH, nq // 2),
        in_specs=[pl.BlockSpec((S, HP), lambda h, j: (0, h)), pl.BlockSpec((2 * TQ, HP), lambda h, j: (j, h)),
                  pl.BlockSpec((2 * TQ, DH), lambda h, j: (j, NH + h)), pl.BlockSpec((S, DH), lambda h, j: (0, h)),
                  pl.BlockSpec((1, 1, S), lambda h, j: (h, 0, 0)), pl.BlockSpec((1, 1, S), lambda h, j: (h, 0, 0))],
        out_specs=[pl.BlockSpec((2 * TQ, HP), lambda h, j: (j, h)), pl.BlockSpec((2 * TQ, DH), lambda h, j: (j, h))],
        out_shape=[jax.ShapeDtypeStruct((S, NH * HP), F32), jax.ShapeDtypeStruct((S, NH * DH), F32)],
        compiler_params=_cparams(("parallel", "arbitrary")))(qp, kp, kv, do, lse_row, delta_row)


def _tm(S, width):
    t = 512 if width <= 1024 else (256 if width <= 3072 else 128)
    return min(t, S)


def _mod_fwd(tag, x, g, shift, scale):
    S = x.shape[0]
    return _rw_fwd(tag + "_mod", f_mod, [Row(x)], [g, shift, scale], [((D,), BF16)], _tm(S, D))[0]


def _mod_bwd(tag, x, g, shift, scale, dh, dx_direct):
    S = x.shape[0]
    r = _rw_bwd(tag + "_mod_b", f_mod, [Row(x)], [g, shift, scale], [Row(dh)], [True], [True] * 3, [((D,), F32)],
                _tm(S, D), add=Row(dx_direct))
    return r[0], r[1:]


def _res_fwd(tag, x, y, gate, coef):
    S = x.shape[0]

    def fn(pieces, bvals):
        return [pieces[0] + coef * bvals[0] * pieces[1]], []
    return _rowwise(tag + "_res", fn, [Row(x), Row(y)], [gate], [((D,), F32)], [], _tm(S, D))[0]


def _res_bwd(tag, y, gate, dxn, coef):
    S = y.shape[0]
    r = _rw_bwd(tag + "_res_b", make_f_res(coef), [Row(y)], [gate], [Row(dxn)], [True], [True], [((D,), BF16)], _tm(S, D))
    return r[0], r[1]


def _ffn_fwd(tag, x, mod3, g, w_in4, w_out4, li):
    shift, scale, gate = mod3
    S = x.shape[0]
    h = _mod_fwd(tag, x, g, shift, scale)
    gu = _matmul(tag + "_in", h, w_in4, lay="b_cols", li=li, out_dtype=BF16)
    a = _rw_fwd(tag + "_act", f_act, [Row(gu, splits=[FF, FF])], [], [((FF,), BF16)], _tm(S, FF))[0]
    y = _matmul(tag + "_out", a, w_out4, lay="b_rows", li=li)
    xn = _res_fwd(tag, x, y, gate, 0.5)
    return xn, (x, h, gu, a, y)


def _ffn_bwd(tag, dxn, res, mod3, g, w_in4, w_out4, li, g_in4, g_out4):
    shift, scale, gate = mod3
    x, h, gu, a, y = res
    S = x.shape[0]
    nmat = w_in4.shape[1]
    dy, dgate = _res_bwd(tag, y, gate, dxn, 0.5)
    da = _matmul(tag + "_out_bi", dy, w_out4, "nt", lay="b_rows", li=li, out_dtype=BF16)
    g_out4 = _matmul(tag + "_out_bw", a, dy, "tn", lay="o_rows", li=li, into=g_out4, nmat=nmat)
    dgu = _rw_bwd(tag + "_act_b", f_act, [Row(gu, splits=[FF, FF])], [], [Row(da)], [True, True], [],
                  [((FF, FF), BF16)], _tm(S, FF))[0]
    dh = _matmul(tag + "_in_bi", dgu, w_in4, "nt", lay="b_cols", li=li)
    g_in4 = _matmul(tag + "_in_bw", h, dgu, "tn", lay="o_cols", li=li, into=g_in4, nmat=nmat)
    dx, (dg, dshift, dscale) = _mod_bwd(tag, x, g, shift, scale, dh, dxn)
    return dx, g_in4, g_out4, dict(g=dg, mod=(dshift, dscale, dgate))


def _pad_lanes(a, lo, width=LANE):
    return jnp.pad(a, ((0, 0), (lo, width - lo - a.shape[1])))


def _gdn_layer_fwd(tag, x, mod3, g, p):
    shift, scale, gate = mod3
    S = x.shape[0]
    NC = S // CHUNK
    h = _mod_fwd(tag, x, g, shift, scale)
    proj = _matmul(tag + "_in", h, p["w_in"])
    qc = _conv_fwd(tag + "_conv", proj, p["conv_w8"], 3 * D, _tm(S, 3 * D))
    q, k, v = _rw_fwd(tag + "_pre", f_gdnpre, [Row(qc, splits=[DH] * (3 * NH))], [],
                      [((DH,) * NH, F32)] * 3, _tm(S, 3 * D))
    betaf, gf = _rw_fwd(tag + "_gates", f_gates, [Row(proj, LANE, cb=GATE_CB)], [p["a_log128"], p["dt_bias128"]],
                        [((LANE,), F32)] * 2, _tm(S, LANE))
    bcol, gcol = betaf[:, :NH], gf[:, NH:2 * NH]
    grow = gcol.reshape(NC, CHUNK, NH).transpose(0, 2, 1)
    o, ssave, tsave = _gdn_fwd(tag + "_core", q, k, v, gcol, grow, bcol)
    on = _rw_fwd(tag + "_post", f_gdnpost, [Row(o, splits=[DH] * NH), Row(proj, D, cb=3, splits=[DH] * NH)],
                 [p["norm_g"]], [((DH,) * NH, BF16)], _tm(S, 2 * D))[0]
    y = _matmul(tag + "_out", on, p["w_out"])
    xn = _res_fwd(tag, x, y, gate, 1.0)
    return xn, (x, h, proj, qc, q, k, v, gcol, grow, bcol, ssave, tsave, o, on, y)


def _gdn_layer_bwd(tag, dxn, res, mod3, g, p):
    shift, scale, gate = mod3
    x, h, proj, qc, q, k, v, gcol, grow, bcol, ssave, tsave, o, on, y = res
    S = x.shape[0]
    dy, dgate = _res_bwd(tag, y, gate, dxn, 1.0)
    don = _matmul(tag + "_out_bi", dy, p["w_out"], "nt")
    dw_out = _matmul(tag + "_out_bw", on, dy, "tn")
    do, dz, dnorm = _rw_bwd(tag + "_post_b", f_gdnpost, [Row(o, splits=[DH] * NH), Row(proj, D, cb=3, splits=[DH] * NH)],
                            [p["norm_g"]], [Row(don, splits=[DH] * NH)], [True] * (2 * NH), [True],
                            [((DH,) * NH, F32), ((DH,) * NH, BF16)], _tm(S, 2 * D))
    dq, dk, dv, dgc, dgr, db = _gdn_bwd(tag + "_core_b", q, k, v, gcol, grow, bcol, ssave, tsave, do)
    dgcol = dgc + dgr.transpose(0, 2, 1).reshape(S, NH)
    dgates, da_log, ddt = _rw_bwd(tag + "_gates_b", f_gates, [Row(proj, LANE, cb=GATE_CB)], [p["a_log128"], p["dt_bias128"]],
                                  [Row(_pad_lanes(db, 0)), Row(_pad_lanes(dgcol, NH))], [True], [True, True],
                                  [((LANE,), BF16)], _tm(S, LANE))
    dqc = _rw_bwd(tag + "_pre_b", f_gdnpre, [Row(qc, splits=[DH] * (3 * NH))], [],
                  [Row(dq, splits=[DH] * NH), Row(dk, splits=[DH] * NH), Row(dv, splits=[DH] * NH)],
                  [True] * (3 * NH), [], [((DH,) * (3 * NH), F32)], _tm(S, 3 * D))[0]
    dqkv, dconv = _conv_bwd(tag + "_conv_b", proj, dqc, p["conv_w8"], 3 * D, _tm(S, 3 * D), BF16)
    dproj = jnp.concatenate([dqkv, dz, dgates], axis=1)
    dh = _matmul(tag + "_in_bi", dproj, p["w_in"], "nt")
    dw_in = _matmul(tag + "_in_bw", h, dproj, "tn")
    dx, (dg, dshift, dscale) = _mod_bwd(tag, x, g, shift, scale, dh, dxn)
    return dx, dict(w_in=dw_in, conv_w8=dconv, a_log128=da_log, dt_bias128=ddt, norm_g=dnorm,
                    w_out=dw_out, g=dg, mod=(dshift, dscale, dgate))


def _qk_rows(src, shared_rope, ckv=None):
    if shared_rope:
        return [Row(src, D, cb=0, splits=[DH] * NH), Row(ckv, LANE, cb=2)]
    return [Row(src, splits=[DH] * (2 * NH))]


def _kv_fwd(x, kvmod, p, tabs):
    shift, scale = kvmod
    S = x.shape[0]
    h = _mod_fwd("kv", x, p["kv_norm_g"], shift, scale)
    ckv = _matmul("kv_dkv", h, p["w_dkv"])
    lat = _rw_fwd("kv_lat", f_rms, [Row(ckv, KVL)], [p["kv_lat_g"]], [((KVL,), BF16)], _tm(S, KVL))[0]
    kvf = _matmul("kv_ukv", lat, p["w_ukv"])
    kp = _rw_fwd("kv_k", make_f_qk(True), _qk_rows(kvf, True, ckv) + [Row(tabs[0]), Row(tabs[1])],
                 [p["k_gn"], p["k_gr"], p["pm"]], [((DH,) * (2 * NH), BF16)], _tm(S, 2 * D))[0]
    return kp, kvf, (x, h, ckv, lat)


def _kv_bwd(dkp, dv, dx_direct, res, kvmod, kvf, p, tabs):
    shift, scale = kvmod
    x, h, ckv, lat = res
    S = x.shape[0]
    dkn, dkr, dgn, dgr = _rw_bwd("kv_k_b", make_f_qk(True), _qk_rows(kvf, True, ckv) + [Row(tabs[0]), Row(tabs[1])],
                                 [p["k_gn"], p["k_gr"], p["pm"]], [Row(dkp, splits=[DH] * (2 * NH))],
                                 [True] * (NH + 1) + [False, False], [True, True, False],
                                 [((DH,) * NH, BF16), ((LANE,), BF16)], _tm(S, 2 * D))
    dkvf = jnp.concatenate([dkn, dv.astype(BF16)], axis=1)
    dlat = _matmul("kv_ukv_bi", dkvf, p["w_ukv"], "nt")
    dw_ukv = _matmul("kv_ukv_bw", lat, dkvf, "tn")
    dcl, dlg = _rw_bwd("kv_lat_b", f_rms, [Row(ckv, KVL)], [p["kv_lat_g"]], [Row(dlat)], [True], [True],
                       [((KVL,), BF16)], _tm(S, KVL))
    dckv = jnp.concatenate([dcl, dkr], axis=1)
    dh = _matmul("kv_dkv_bi", dckv, p["w_dkv"], "nt")
    dw_dkv = _matmul("kv_dkv_bw", h, dckv, "tn")
    dx, (dg, dshift, dscale) = _mod_bwd("kv", x, p["kv_norm_g"], shift, scale, dh, dx_direct)
    return dx, dict(w_dkv=dw_dkv, w_ukv=dw_ukv, kv_lat_g=dlg, k_gn=dgn, k_gr=dgr, kv_norm_g=dg, mod=(dshift, dscale))


def _mla_layer_fwd(tag, x, mod3, g, p, kp, kvf, tabs):
    shift, scale, gate = mod3
    S = x.shape[0]
    h = _mod_fwd(tag, x, g, shift, scale)
    ql = _matmul(tag + "_dq", h, p["w_dq"])
    qln = _rw_fwd(tag + "_qln", f_rms, [Row(ql)], [p["ql_g"]], [((QL,), BF16)], _tm(S, QL))[0]
    qu = _matmul(tag + "_uq", qln, p["w_uq"])
    qp = _rw_fwd(tag + "_q", make_f_qk(False), _qk_rows(qu, False) + [Row(tabs[0]), Row(tabs[1])],
                 [p["q_gn"], p["q_gr"], p["pm"]], [((DH,) * (2 * NH), BF16)], _tm(S, 2 * D))[0]
    o, lse = _flash_fwd(tag + "_att", qp, kp, kvf)
    y = _matmul(tag + "_out", o, p["w_out"])
    xn = _res_fwd(tag, x, y, gate, 1.0)
    return xn, (x, h, ql, qln, qu, qp, o, lse, y)


def _mla_layer_bwd(tag, dxn, res, mod3, g, p, kp, kvf, tabs):
    shift, scale, gate = mod3
    x, h, ql, qln, qu, qp, o, lse, y = res
    S = x.shape[0]
    dy, dgate = _res_bwd(tag, y, gate, dxn, 1.0)
    do = _matmul(tag + "_out_bi", dy, p["w_out"], "nt")
    dw_out = _matmul(tag + "_out_bw", o, dy, "tn")
    dqp, delta = _flash_bwd_dq(tag + "_att_bq", qp, kp, kvf, o, do, lse)
    dkp, dv = _flash_bwd_dkv(tag + "_att_bkv", qp, kp, kvf, do, lse.reshape(NH, 1, S), delta.reshape(NH, 1, S))
    dqu, dgn, dgr = _rw_bwd(tag + "_q_b", make_f_qk(False), _qk_rows(qu, False) + [Row(tabs[0]), Row(tabs[1])],
                            [p["q_gn"], p["q_gr"], p["pm"]], [Row(dqp, splits=[DH] * (2 * NH))],
                            [True] * (2 * NH) + [False, False], [True, True, False],
                            [((DH,) * (2 * NH), BF16)], _tm(S, 2 * D))
    dqln = _matmul(tag + "_uq_bi", dqu, p["w_uq"], "nt")
    dw_uq = _matmul(tag + "_uq_bw", qln, dqu, "tn")
    dql, dqlg = _rw_bwd(tag + "_qln_b", f_rms, [Row(ql)], [p["ql_g"]], [Row(dqln)], [True], [True], [((QL,), BF16)],
                        _tm(S, QL))
    dh = _matmul(tag + "_dq_bi", dql, p["w_dq"], "nt")
    dw_dq = _matmul(tag + "_dq_bw", h, dql, "tn")
    dx, (dg, dshift, dscale) = _mod_bwd(tag, x, g, shift, scale, dh, dxn)
    return dx, dkp, dv, dict(w_dq=dw_dq, w_uq=dw_uq, w_out=dw_out, ql_g=dqlg, q_gn=dgn, q_gr=dgr, g=dg,
                             mod=(dshift, dscale, dgate))


def _loss_head(y, tgt):
    S = y.shape[0]

    def fn(pieces, bvals):
        e = pieces[0] - pieces[1]
        part = jnp.sum(e * e) * (0.5 / D)
        return [e * (1.0 / D)], [jnp.full((1, LANE), part, F32)]
    dy, part = _rowwise("loss", fn, [Row(y), Row(tgt)], [], [((D,), F32)], [(1, LANE)], _tm(S, D))
    return part[0, 0], dy


def _rope_tables(positions):
    S = positions.shape[0]
    half = ROPE // 2
    lane = lax.broadcasted_iota(jnp.int32, (1, LANE), 1)
    inv_freq = ROPE_BASE ** (-(lane % half).astype(F32) / half)
    live = (lane < ROPE).astype(F32)
    sign = jnp.where(lane < half, -1.0, 1.0) * live

    def fn(pieces, bvals):
        ang = pieces[0] * bvals[0]
        return [jnp.cos(ang) * bvals[1], jnp.sin(ang) * bvals[2]], []
    pos = jnp.broadcast_to(positions.astype(F32)[:, None], (S, LANE))
    cosp, sins = _rowwise("rope_tab", fn, [Row(pos)], [inv_freq, live, sign], [((LANE,), F32)] * 2, [], _tm(S, LANE))
    r = lax.broadcasted_iota(jnp.int32, (LANE, LANE), 0)
    c = lax.broadcasted_iota(jnp.int32, (LANE, LANE), 1)
    pm = (((c < half) & (r == c + half)) | ((c >= half) & (c < ROPE) & (r == c - half))).astype(F32)
    return (cosp, sins), pm


def _adamw(name, w, g, m, v):
    shape = w.shape
    C = shape[-1]
    R = w.size // C
    tr = R
    for t in (1024, 512, 256, 128, 64, 32, 16, 8):
        if R % t == 0 and t * C * 4 <= (1 << 21):
            tr = t
            break
    c1 = 1.0 - ADAM_B1 ** ADAM_STEP
    c2 = 1.0 - ADAM_B2 ** ADAM_STEP

    def body(w_ref, g_ref, m_ref, v_ref, d_ref, mo_ref, vo_ref):
        gg = g_ref[...]
        mn = ADAM_B1 * m_ref[...] + (1.0 - ADAM_B1) * gg
        vn = ADAM_B2 * v_ref[...] + (1.0 - ADAM_B2) * (gg * gg)
        d_ref[...] = -ADAM_LR * ((mn / c1) / (jnp.sqrt(vn / c2) + ADAM_EPS) + ADAM_WD * w_ref[...])
        mo_ref[...] = mn
        vo_ref[...] = vn

    spec = pl.BlockSpec((tr, C), lambda i: (i, 0))
    outs = pl.pallas_call(body, name=name, grid=(R // tr,), in_specs=[spec] * 4, out_specs=[spec] * 3,
                          out_shape=[jax.ShapeDtypeStruct((R, C), F32)] * 3,
                          compiler_params=_cparams(("parallel",)))(*[t.reshape(R, C) for t in (w, g, m, v)])
    return [o.reshape(shape) for o in outs]


HBM_SPEC = pl.BlockSpec(memory_space=pltpu.HBM)
OTHER_CHIPS = (4, 2, 6)
SIBLING = 1


def _me():
    return lax.axis_index("x"), lax.axis_index("y"), lax.axis_index("c")


def _peer(me, k):
    mx, my, mc = me
    return ((1 - mx) if k & 4 else mx, (1 - my) if k & 2 else my, (1 - mc) if k & 1 else mc)


def _rcopy(src, dst, ssem, rsem, to):
    return pltpu.make_async_remote_copy(src_ref=src, dst_ref=dst, send_sem=ssem, recv_sem=rsem, device_id=to,
                                        device_id_type=MESH)


def _all_gather8(name, x):
    def body(x_ref, o_ref, ssem, rsem, lsem):
        me = _me()
        mine = 4 * me[0] + 2 * me[1] + me[2]
        loc = pltpu.make_async_copy(x_ref, o_ref.at[mine], lsem)
        loc.start()
        sends = []
        for k in range(1, 8):
            cp = _rcopy(x_ref, o_ref.at[mine], ssem.at[k - 1], rsem.at[k - 1], _peer(me, k))
            cp.start()
            sends.append(cp)
        for k in range(1, 8):
            px, py, pc = _peer(me, k)
            _rcopy(x_ref, o_ref.at[4 * px + 2 * py + pc], ssem.at[k - 1], rsem.at[k - 1], (px, py, pc)).wait_recv()
        for cp in sends:
            cp.wait_send()
        loc.wait()

    return pl.pallas_call(body, name=name, out_shape=jax.ShapeDtypeStruct((8,) + x.shape, x.dtype),
                          in_specs=[HBM_SPEC], out_specs=HBM_SPEC,
                          scratch_shapes=[pltpu.SemaphoreType.DMA((7,)), pltpu.SemaphoreType.DMA((7,)),
                                          pltpu.SemaphoreType.DMA(())])(x)


PACK_L = 1024
PACK_RT = 256


def _place_shard(name, wp, chip):
    rh, ln = wp.shape[1:]

    def body(s_ref, w_ref, o_ref):
        o_ref[...] = w_ref[...]

    gs = pltpu.PrefetchScalarGridSpec(
        num_scalar_prefetch=1, grid=(2, rh // PACK_RT),
        in_specs=[pl.BlockSpec((None, PACK_RT, ln), lambda h, i, s_ref: (h, i, 0))],
        out_specs=pl.BlockSpec((None, None, PACK_RT, ln), lambda h, i, s_ref: (s_ref[0], h, i, 0)))
    return pl.pallas_call(body, name=name, grid_spec=gs, out_shape=jax.ShapeDtypeStruct((4,) + wp.shape, wp.dtype),
                          compiler_params=_cparams(("parallel", "parallel")))(chip.reshape(1).astype(jnp.int32), wp)


def _gather_weights(name, w4):
    r2 = w4.shape[2] // 2

    def body(w_ref, o_ref, ssem, rsem):
        me = _me()
        mc = me[2]
        px, py, pd, sib = _peer(me, 4), _peer(me, 2), _peer(me, 6), _peer(me, SIBLING)
        chip = lambda p: 2 * p[0] + p[1]
        mine, from_x, from_y, from_d = (o_ref.at[chip(p), mc] for p in (me, px, py, pd))
        q0, q1 = pl.ds(0, r2), pl.ds(r2, r2)
        sends = [_rcopy(mine, mine, ssem.at[0], rsem.at[0], px), _rcopy(mine, mine, ssem.at[1], rsem.at[1], py)]
        for cp in sends:
            cp.start()
        _rcopy(from_x, from_x, ssem.at[0], rsem.at[0], px).wait_recv()
        sends += [_rcopy(from_x.at[q1], from_x.at[q1], ssem.at[2], rsem.at[2], py),
                  _rcopy(from_x, from_x, ssem.at[4], rsem.at[4], sib)]
        sends[-2].start()
        sends[-1].start()
        _rcopy(from_y, from_y, ssem.at[1], rsem.at[1], py).wait_recv()
        sends += [_rcopy(from_y.at[q0], from_y.at[q0], ssem.at[3], rsem.at[3], px),
                  _rcopy(from_y, from_y, ssem.at[5], rsem.at[5], sib)]
        sends[-2].start()
        sends[-1].start()
        _rcopy(from_d.at[q0], from_d.at[q0], ssem.at[3], rsem.at[3], px).wait_recv()
        _rcopy(from_d.at[q1], from_d.at[q1], ssem.at[2], rsem.at[2], py).wait_recv()
        sends.append(_rcopy(from_d, from_d, ssem.at[6], rsem.at[6], sib))
        sends[-1].start()
        for j, p in enumerate((px, py, pd)):
            land = o_ref.at[chip(p), 1 - mc]
            _rcopy(land, land, ssem.at[4 + j], rsem.at[4 + j], sib).wait_recv()
        for cp in sends:
            cp.wait_send()

    return pl.pallas_call(body, name=name, out_shape=jax.ShapeDtypeStruct(w4.shape, w4.dtype),
                          in_specs=[HBM_SPEC], out_specs=HBM_SPEC, input_output_aliases={0: 0},
                          scratch_shapes=[pltpu.SemaphoreType.DMA((7,)), pltpu.SemaphoreType.DMA((7,))])(w4)


def _exchange_half(name, g):
    def body(g_ref, p_ref, ssem, rsem):
        me = _me()
        cps = []
        for s in range(4):
            cp = _rcopy(g_ref.at[s, 1 - me[2]], p_ref.at[s], ssem.at[s], rsem.at[s], _peer(me, SIBLING))
            cp.start()
            cps.append(cp)
        for cp in cps:
            cp.wait()

    return pl.pallas_call(body, name=name, out_shape=jax.ShapeDtypeStruct((4,) + g.shape[2:], g.dtype),
                          in_specs=[HBM_SPEC], out_specs=HBM_SPEC,
                          scratch_shapes=[pltpu.SemaphoreType.DMA((4,)), pltpu.SemaphoreType.DMA((4,))])(g)


def _scatter_chips(name, q):
    r2 = q.shape[1] // 2

    def body(q_ref, t_ref, relay, ssem, rsem):
        me = _me()
        px, py, pd = _peer(me, 4), _peer(me, 2), _peer(me, 6)
        chip = lambda p: 2 * p[0] + p[1]
        q0, q1 = pl.ds(0, r2), pl.ds(r2, r2)
        sends = [_rcopy(q_ref.at[chip(px)], t_ref.at[0], ssem.at[0], rsem.at[0], px),
                 _rcopy(q_ref.at[chip(py)], t_ref.at[1], ssem.at[1], rsem.at[1], py),
                 _rcopy(q_ref.at[chip(pd), q0], relay.at[0], ssem.at[2], rsem.at[2], py),
                 _rcopy(q_ref.at[chip(pd), q1], relay.at[1], ssem.at[3], rsem.at[3], px)]
        for cp in sends:
            cp.start()
        _rcopy(relay.at[0], relay.at[0], ssem.at[2], rsem.at[2], py).wait_recv()
        sends.append(_rcopy(relay.at[0], t_ref.at[2, q0], ssem.at[4], rsem.at[4], px))
        sends[-1].start()
        _rcopy(relay.at[1], relay.at[1], ssem.at[3], rsem.at[3], px).wait_recv()
        sends.append(_rcopy(relay.at[1], t_ref.at[2, q1], ssem.at[5], rsem.at[5], py))
        sends[-1].start()
        _rcopy(t_ref.at[0], t_ref.at[0], ssem.at[0], rsem.at[0], px).wait_recv()
        _rcopy(t_ref.at[1], t_ref.at[1], ssem.at[1], rsem.at[1], py).wait_recv()
        _rcopy(t_ref.at[2, q0], t_ref.at[2, q0], ssem.at[4], rsem.at[4], px).wait_recv()
        _rcopy(t_ref.at[2, q1], t_ref.at[2, q1], ssem.at[5], rsem.at[5], py).wait_recv()
        for cp in sends:
            cp.wait_send()

    return pl.pallas_call(body, name=name,
                          out_shape=[jax.ShapeDtypeStruct((3,) + q.shape[1:], q.dtype),
                                     jax.ShapeDtypeStruct((2, r2) + q.shape[2:], q.dtype)],
                          in_specs=[HBM_SPEC], out_specs=[HBM_SPEC, HBM_SPEC],
                          scratch_shapes=[pltpu.SemaphoreType.DMA((6,)), pltpu.SemaphoreType.DMA((6,))])(q)[0]


def _exchange_full(name, r2):
    def body(r_ref, o_ref, ssem, rsem):
        me = _me()
        mc = me[2]
        cp = _rcopy(o_ref.at[mc], o_ref.at[mc], ssem, rsem, _peer(me, SIBLING))
        cp.start()
        _rcopy(o_ref.at[1 - mc], o_ref.at[1 - mc], ssem, rsem, _peer(me, SIBLING)).wait_recv()
        cp.wait_send()

    return pl.pallas_call(body, name=name, out_shape=jax.ShapeDtypeStruct(r2.shape, r2.dtype),
                          in_specs=[HBM_SPEC], out_specs=HBM_SPEC, input_output_aliases={0: 0},
                          scratch_shapes=[pltpu.SemaphoreType.DMA(()), pltpu.SemaphoreType.DMA(())])(r2)


def _add_half(name, g, p, c):
    rh, ln = g.shape[2:]

    def body(c_ref, g_ref, p_ref, o_ref):
        o_ref[0] = (g_ref[0, 0] + p_ref[0]).astype(o_ref.dtype)

    gs = pltpu.PrefetchScalarGridSpec(
        num_scalar_prefetch=1, grid=(4, rh // PACK_RT),
        in_specs=[pl.BlockSpec((1, 1, PACK_RT, ln), lambda s, i, c_ref: (s, c_ref[0], i, 0)),
                  pl.BlockSpec((1, PACK_RT, ln), lambda s, i, c_ref: (s, i, 0))],
        out_specs=pl.BlockSpec((1, PACK_RT, ln), lambda s, i, c_ref: (s, i, 0)))
    return pl.pallas_call(body, name=name, grid_spec=gs, out_shape=jax.ShapeDtypeStruct((4, rh, ln), BF16),
                          compiler_params=_cparams(("parallel", "parallel")))(c.reshape(1).astype(jnp.int32), g, p)


def _add_chips(name, q, t, chip, c):
    rh, ln = q.shape[1:]

    def body(s_ref, c_ref, q_ref, t_ref, o_ref):
        o_ref[...] = ((q_ref[0].astype(F32) + t_ref[0].astype(F32)) + t_ref[1].astype(F32)) + t_ref[2].astype(F32)

    gs = pltpu.PrefetchScalarGridSpec(
        num_scalar_prefetch=2, grid=(rh // PACK_RT,),
        in_specs=[pl.BlockSpec((1, PACK_RT, ln), lambda i, s_ref, c_ref: (s_ref[0], i, 0)),
                  pl.BlockSpec((3, PACK_RT, ln), lambda i, s_ref, c_ref: (0, i, 0))],
        out_specs=pl.BlockSpec((None, PACK_RT, ln), lambda i, s_ref, c_ref: (c_ref[0], i, 0)))
    return pl.pallas_call(body, name=name, grid_spec=gs, out_shape=jax.ShapeDtypeStruct((2, rh, ln), F32),
                          compiler_params=_cparams(("parallel",)))(chip.reshape(1).astype(jnp.int32),
                                                                    c.reshape(1).astype(jnp.int32), q, t)


def _sum8(name, a):
    def body(a_ref, o_ref):
        acc = a_ref[0]
        for d in range(1, 8):
            acc = acc + a_ref[d]
        o_ref[...] = acc
    return pl.pallas_call(body, name=name, out_shape=jax.ShapeDtypeStruct(a.shape[1:], F32))(a)


def _silu_rows(name, a):
    def body(a_ref, o_ref):
        o_ref[...] = _silu(a_ref[...])
    return pl.pallas_call(body, name=name, out_shape=jax.ShapeDtypeStruct(a.shape, F32))(a)


REST = (("gdn_w_out", 1), ("mla_w_dkv", 0), ("mla_w_ukv", 1), ("mla_w_dq", 1), ("mla_w_uq", 2), ("mla_w_out", 1))


def _packed_rows(n):
    per_half = -(-n // (2 * PACK_L))
    return -(-per_half // PACK_RT) * PACK_RT


def _pack_flat(flat):
    n = flat.shape[-1]
    rh = _packed_rows(n)
    pad = [(0, 0)] * (flat.ndim - 1) + [(0, 2 * rh * PACK_L - n)]
    return jnp.pad(flat, pad).reshape(flat.shape[:-1] + (2, rh, PACK_L))


def _shards_first(full, axis):
    sh = full.shape
    t = full.reshape(sh[:axis] + (4, sh[axis] // 4) + sh[axis + 1:])
    return jnp.moveaxis(t, axis, 0)


def _shards_merge(stacked, axis):
    t = jnp.moveaxis(stacked, 0, axis)
    sh = t.shape
    return t.reshape(sh[:axis] + (4 * sh[axis + 1],) + sh[axis + 2:])


def _pack_small(parts):
    flat = jnp.concatenate([p.reshape(-1).astype(F32) for p in parts])
    n = flat.shape[0]
    rows = -(-n // (SUB * LANE)) * SUB
    return jnp.pad(flat, (0, rows * LANE - n)).reshape(rows, LANE)


def _unpack_small(buf, shapes):
    lead = buf.shape[:-2]
    flat = buf.reshape(lead + (-1,))
    out, off = [], 0
    for sh in shapes:
        n = 1
        for d in sh:
            n *= d
        out.append(flat[..., off:off + n].reshape(lead + tuple(sh)))
        off += n
    return out


WEIGHTS = ('ada_w', 'ada_b', 'norm_g', 'ffn_w_in', 'ffn_w_out', 'gdn_w_in', 'gdn_conv_w', 'gdn_a_log', 'gdn_dt_bias',
           'gdn_norm_g', 'gdn_w_out', 'kv_ada_w', 'kv_ada_b', 'kv_norm_g', 'mla_w_dkv', 'mla_kv_norm_g', 'mla_w_ukv',
           'mla_k_norm_g', 'mla_w_dq', 'mla_q_lora_norm_g', 'mla_w_uq', 'mla_q_norm_g', 'mla_w_out')
ARGS = ('x', 'c', 'positions') + WEIGHTS + ('loss_target',) + tuple('m_' + n for n in WEIGHTS) + tuple('v_' + n for n in WEIGHTS)


def _split_norm(v):
    return v[None, :DH], _pad_lanes(v[None, DH:], 0)


def _join_norm(gn, gr):
    return jnp.concatenate([gn[0], gr[0, :ROPE]])


def _step(x, tgt, pos, mods, kvmod, W, P):
    tabs, pm = _rope_tables(pos)
    m3 = lambda l, i: tuple(mods[l][3 * i + j][None] for j in range(3))
    ng = lambda l, i: P["norm_g"][l, i][None]
    gdn_p, mla_p = [], []
    for l in range(2):
        gdn_p.append(dict(w_in=jnp.pad(W["gdn_w_in"][l], ((0, 0), (0, GDN_IN - W["gdn_w_in"].shape[2]))),
                          conv_w8=jnp.pad(P["gdn_conv_w"][l], ((0, 4), (0, 0))),
                          a_log128=_pad_lanes(P["gdn_a_log"][l][None], NH), dt_bias128=_pad_lanes(P["gdn_dt_bias"][l][None], NH),
                          norm_g=P["gdn_norm_g"][l][None], w_out=W["gdn_w_out"][l]))
        q_gn, q_gr = _split_norm(P["mla_q_norm_g"][l])
        mla_p.append(dict(w_dq=W["mla_w_dq"][l], ql_g=P["mla_q_lora_norm_g"][l][None],
                          w_uq=jnp.pad(W["mla_w_uq"][l].reshape(QL, NH, QKH), ((0, 0), (0, 0), (0, HP - QKH))).reshape(QL, NH * HP),
                          q_gn=q_gn, q_gr=q_gr, pm=pm, w_out=W["mla_w_out"][l]))
    k_gn, k_gr = _split_norm(P["mla_k_norm_g"])
    kv_p = dict(kv_norm_g=P["kv_norm_g"][None], w_dkv=jnp.pad(W["mla_w_dkv"], ((0, 0), (0, QL - KVL - ROPE))),
                kv_lat_g=P["mla_kv_norm_g"][None],
                w_ukv=W["mla_w_ukv"].reshape(KVL, NH, 2, DH).transpose(0, 2, 1, 3).reshape(KVL, 2 * NH * DH),
                k_gn=k_gn, k_gr=k_gr, pm=pm)
    kvm = (kvmod[0][None], kvmod[1][None])

    res = {}
    for l in range(4):
        x, res[l, 0] = _ffn_fwd(f"l{l}a", x, m3(l, 0), ng(l, 0), W["ffn_w_in"], W["ffn_w_out"], 2 * l)
        if l < 2:
            x, res[l, 1] = _gdn_layer_fwd(f"l{l}g", x, m3(l, 1), ng(l, 1), gdn_p[l])
        else:
            x, res[l, 1] = _mla_layer_fwd(f"l{l}m", x, m3(l, 1), ng(l, 1), mla_p[l - 2], kp, kvf, tabs)
        x, res[l, 2] = _ffn_fwd(f"l{l}b", x, m3(l, 2), ng(l, 2), W["ffn_w_in"], W["ffn_w_out"], 2 * l + 1)
        if l == 1:
            kp, kvf, kres = _kv_fwd(x, kvm, kv_p, tabs)
    loss, dx = _loss_head(x, tgt)

    gw = {n: [None] * W[n].shape[0] for n in ("gdn_w_in", "gdn_w_out", "mla_w_dq", "mla_w_uq", "mla_w_out")}
    g_in4 = g_out4 = None
    gp = {n: [None] * 2 for n in ("gdn_conv_w", "gdn_a_log", "gdn_dt_bias", "gdn_norm_g", "mla_q_lora_norm_g", "mla_q_norm_g")}
    gnorm = [[None] * 3 for _ in range(4)]
    dmod = [[None] * NMOD for _ in range(4)]
    dkp = dv = None
    for l in (3, 2, 1, 0):
        if l == 1:
            dx, gk = _kv_bwd(dkp, dv, dx, kres, kvm, kvf, kv_p, tabs)
        for i in (2, 1, 0):
            if i != 1:
                dx, g_in4, g_out4, gd = _ffn_bwd(f"l{l}{'ab'[i // 2]}", dx, res[l, i], m3(l, i), ng(l, i), W["ffn_w_in"],
                                                 W["ffn_w_out"], 2 * l + i // 2, g_in4, g_out4)
            elif l < 2:
                dx, gd = _gdn_layer_bwd(f"l{l}g", dx, res[l, 1], m3(l, 1), ng(l, 1), gdn_p[l])
                gw["gdn_w_in"][l] = gd["w_in"][:, :W["gdn_w_in"].shape[2]]
                gw["gdn_w_out"][l] = gd["w_out"]
                gp["gdn_conv_w"][l] = gd["conv_w8"][:4]
                gp["gdn_a_log"][l] = gd["a_log128"][0, NH:2 * NH]
                gp["gdn_dt_bias"][l] = gd["dt_bias128"][0, NH:2 * NH]
                gp["gdn_norm_g"][l] = gd["norm_g"][0]
            else:
                dx, dkp_l, dv_l, gd = _mla_layer_bwd(f"l{l}m", dx, res[l, 1], m3(l, 1), ng(l, 1), mla_p[l - 2], kp, kvf, tabs)
                dkp = dkp_l if dkp is None else dkp + dkp_l
                dv = dv_l if dv is None else dv + dv_l
                gw["mla_w_dq"][l - 2], gw["mla_w_out"][l - 2] = gd["w_dq"], gd["w_out"]
                gw["mla_w_uq"][l - 2] = gd["w_uq"].reshape(QL, NH, HP)[:, :, :QKH].reshape(QL, NH * QKH)
                gp["mla_q_lora_norm_g"][l - 2] = gd["ql_g"][0]
                gp["mla_q_norm_g"][l - 2] = _join_norm(gd["q_gn"], gd["q_gr"])
            gnorm[l][i] = gd["g"][0]
            for j in range(3):
                dmod[l][3 * i + j] = gd["mod"][j][0]
    gwf = {n: jnp.stack(v) for n, v in gw.items()}
    gwf["ffn_w_in"], gwf["ffn_w_out"] = g_in4, g_out4
    gwf["mla_w_dkv"] = gk["w_dkv"][:, :KVL + ROPE]
    gwf["mla_w_ukv"] = gk["w_ukv"].reshape(KVL, 2, NH, DH).transpose(0, 2, 1, 3).reshape(KVL, 2 * NH * DH)
    gpf = {n: jnp.stack(v) for n, v in gp.items()}
    gpf["norm_g"] = jnp.stack([jnp.stack(r) for r in gnorm])
    gpf["kv_norm_g"] = gk["kv_norm_g"][0]
    gpf["mla_kv_norm_g"] = gk["kv_lat_g"][0]
    gpf["mla_k_norm_g"] = _join_norm(gk["k_gn"], gk["k_gr"])
    dmods = jnp.stack([jnp.stack(r) for r in dmod])
    dkvmod = jnp.stack([gk["mod"][0][0], gk["mod"][1][0]])
    return loss, dx, gwf, gpf, dmods, dkvmod


SMALL = ("norm_g", "gdn_conv_w", "gdn_a_log", "gdn_dt_bias", "gdn_norm_g", "kv_norm_g", "mla_kv_norm_g", "mla_k_norm_g",
         "mla_q_lora_norm_g", "mla_q_norm_g")


def kernel(x, c, positions, ada_w, ada_b, norm_g, ffn_w_in, ffn_w_out, gdn_w_in, gdn_conv_w, gdn_a_log, gdn_dt_bias,
           gdn_norm_g, gdn_w_out, kv_ada_w, kv_ada_b, kv_norm_g, mla_w_dkv, mla_kv_norm_g, mla_w_ukv, mla_k_norm_g,
           mla_w_dq, mla_q_lora_norm_g, mla_w_uq, mla_q_norm_g, mla_w_out, loss_target, m_ada_w, m_ada_b, m_norm_g,
           m_ffn_w_in, m_ffn_w_out, m_gdn_w_in, m_gdn_conv_w, m_gdn_a_log, m_gdn_dt_bias, m_gdn_norm_g, m_gdn_w_out,
           m_kv_ada_w, m_kv_ada_b, m_kv_norm_g, m_mla_w_dkv, m_mla_kv_norm_g, m_mla_w_ukv, m_mla_k_norm_g, m_mla_w_dq,
           m_mla_q_lora_norm_g, m_mla_w_uq, m_mla_q_norm_g, m_mla_w_out, v_ada_w, v_ada_b, v_norm_g, v_ffn_w_in,
           v_ffn_w_out, v_gdn_w_in, v_gdn_conv_w, v_gdn_a_log, v_gdn_dt_bias, v_gdn_norm_g, v_gdn_w_out, v_kv_ada_w,
           v_kv_ada_b, v_kv_norm_g, v_mla_w_dkv, v_mla_kv_norm_g, v_mla_w_ukv, v_mla_k_norm_g, v_mla_w_dq,
           v_mla_q_lora_norm_g, v_mla_w_uq, v_mla_q_norm_g, v_mla_w_out):
    a = dict(locals())
    mx, my, mc = _me()
    dev = 4 * mx + 2 * my + mc
    chip = 2 * mx + my
    x, tgt, pos = a["x"][0], a["loss_target"][0], a["positions"][0]
    take = lambda arr, i, axis=0: lax.dynamic_index_in_dim(arr, i, axis, keepdims=False)

    pre = _all_gather8("ag_pre", _pack_small([a["c"], a["gdn_conv_w"], a["norm_g"]]))
    c_all, conv_sh, norm_sh = _unpack_small(pre, [(D,), a["gdn_conv_w"].shape, a["norm_g"].shape])
    P = {n: a[n] for n in SMALL}
    P["gdn_conv_w"] = jnp.concatenate([conv_sh[2 * s] for s in range(4)], axis=2)
    P["norm_g"] = jnp.concatenate([norm_sh[2 * s] for s in range(4)], axis=2)
    c_act = _silu_rows("c_act", c_all)
    nada = a["ada_w"].shape[2]
    nkv = a["kv_ada_w"].shape[1]
    modp = [_matmul(f"mod{l}", c_act, a["ada_w"], precise=True, lay="b_stack", li=l) for l in range(4)]
    kvp = _matmul("modkv", c_act, a["kv_ada_w"], precise=True)
    mp = _all_gather8("ag_mod", _pack_small(modp + [kvp]))
    modp_all, kvp_all = _unpack_small(mp, [(4, 8, nada), (8, nkv)])
    mods = jnp.concatenate([take(modp_all[2 * s], dev, 1) for s in range(4)], axis=1) + a["ada_b"]
    mods = mods.reshape(4, NMOD, D)
    kvmod = (jnp.concatenate([take(kvp_all[2 * s], dev, 0) for s in range(4)]) + a["kv_ada_b"]).reshape(2, D)

    def gather(tag, w2):
        return _gather_weights("ag_" + tag, _place_shard("own_" + tag, w2, chip))

    def reduce(tag, g4):
        q = _add_half("rsp_" + tag, g4, _exchange_half("rs1_" + tag, g4), mc)
        r2 = _add_chips("rsc_" + tag, q, _scatter_chips("rs2_" + tag, q), chip, mc)
        return _exchange_full("rs3_" + tag, r2)

    halves = lambda t: t.reshape((2, -1) + t.shape[-1:])
    W = {n: gather(t, halves(a[n].astype(BF16))).reshape((4, 8) + a[n].shape[2:])
         for n, t in (("ffn_w_in", "wi"), ("ffn_w_out", "wo"))}
    wg = gather("wg", a["gdn_w_in"].astype(BF16))
    W["gdn_w_in"] = jnp.concatenate([wg[s] for s in range(4)], axis=2)
    wall = gather("wr", _pack_flat(jnp.concatenate([a[n].reshape(-1).astype(BF16) for n, _ in REST]))).reshape(4, -1)
    off = 0
    for n, ax in REST:
        sz = a[n].size
        W[n] = _shards_merge(wall[:, off:off + sz].reshape((4,) + a[n].shape), ax)
        off += sz

    loss, dx, gw, gp, dmods, dkvmod = _step(x, tgt, pos, mods, kvmod, W, P)
    loss = lax.psum(loss, ("x", "y", "c"))

    grads = {n: reduce(t, gw[n].reshape((4, 2, -1) + a[n].shape[-1:])).reshape(a[n].shape)
             for n, t in (("ffn_w_in", "wi"), ("ffn_w_out", "wo"))}
    ng = a["gdn_w_in"].shape[2]
    grads["gdn_w_in"] = reduce("wg", jnp.stack([gw["gdn_w_in"][:, :, s * ng:(s + 1) * ng] for s in range(4)]))
    gsh = reduce("wr", _pack_flat(jnp.concatenate([_shards_first(gw[n], ax).reshape(4, -1) for n, ax in REST], axis=1)))
    gsh = gsh.reshape(-1)
    off = 0
    for n, _ in REST:
        grads[n] = gsh[off:off + a[n].size].reshape(a[n].shape)
        off += a[n].size

    small = _all_gather8("ag_small", _pack_small([dmods, dkvmod] + [gp[n] for n in SMALL]))
    shapes = [(4, NMOD * D), (2 * D,)] + [gp[n].shape for n in SMALL]
    dmod_all, dkv_all = _unpack_small(small, shapes)[:2]
    tot = _unpack_small(_sum8("sum_small", small), shapes)
    grads["ada_b"], grads["kv_ada_b"] = tot[0], tot[1]
    for n, t in zip(SMALL, tot[2:]):
        grads[n] = t
    grads["norm_g"] = lax.dynamic_slice_in_dim(grads["norm_g"], chip * a["norm_g"].shape[2], a["norm_g"].shape[2], 2)
    grads["gdn_conv_w"] = lax.dynamic_slice_in_dim(grads["gdn_conv_w"], chip * a["gdn_conv_w"].shape[2],
                                                   a["gdn_conv_w"].shape[2], 2)
    ca = jnp.pad(c_act, ((0, LANE - 8), (0, 0)))
    dm = jnp.pad(lax.dynamic_slice_in_dim(dmod_all.reshape(8, 4, NMOD * D), chip * nada, nada, 2), ((0, LANE - 8), (0, 0), (0, 0)))
    gada = None
    for l in range(4):
        gada = _matmul(f"gada{l}", ca, dm[:, l], "tn", precise=True, lay="o_stack", li=l, into=gada, nmat=4)
    grads["ada_w"] = gada
    dk = jnp.pad(lax.dynamic_slice_in_dim(dkv_all, chip * nkv, nkv, 1), ((0, LANE - 8), (0, 0)))
    grads["kv_ada_w"] = _matmul("gadakv", ca, dk, "tn", precise=True)

    upd = [_adamw("adamw_" + n, a[n], grads[n], a["m_" + n], a["v_" + n]) for n in WEIGHTS]
    return (loss, dx[None], *[grads[n] for n in WEIGHTS], *[u[0] for u in upd], *[u[1] for u in upd], *[u[2] for u in upd])
```

```python
import functools

import jax
import jax.numpy as jnp
from jax import lax
from jax.experimental import pallas as pl
from jax.experimental.pallas import tpu as pltpu

F32 = jnp.float32
BF16 = jnp.bfloat16
HI = lax.Precision.HIGHEST
MESH = pl.DeviceIdType.MESH

D = 1024
NH = 8
DH = 128
FF = 2816
NMOD = 9
CHUNK = 64
ROPE = 64
QKH = 192
HP = 256
KVL = 256
QL = 384
GDN_IN = 4224
GATE_CB = 32
EPS = 1e-6
ROPE_BASE = 10000.0
LANE = 128
SUB = 8
VMEM_LIMIT = 56 * 1024 * 1024

ADAM_LR, ADAM_B1, ADAM_B2, ADAM_EPS, ADAM_WD, ADAM_STEP = 0.001, 0.9, 0.999, 1e-08, 0.01, 10


def _tile(n, prefs=(512, 384, 256, 128)):
    for p in prefs:
        if n % p == 0:
            return p
    return n


def _cparams(sem):
    return pltpu.CompilerParams(dimension_semantics=sem, vmem_limit_bytes=VMEM_LIMIT)


class Row:
    def __init__(self, arr, width=None, cb=0, splits=None, halo=None):
        self.arr = arr
        self.width = arr.shape[1] if width is None else width
        self.cb = cb
        self.splits = splits
        self.halo = halo


def _rowwise(name, fn, rows, bcs, outs, accs, tm):
    S = rows[0].arr.shape[0]
    n = S // tm
    nr, nb, no, na = len(rows), len(bcs), len(outs), len(accs)

    def body(*refs):
        rrefs, brefs = refs[:nr], refs[nr:nr + nb]
        orefs, arefs = refs[nr + nb:nr + nb + no], refs[nr + nb + no:]
        pieces = []
        for r, ref in zip(rows, rrefs):
            if r.splits is None:
                pieces.append(ref[...])
            else:
                off = 0
                for w in r.splits:
                    pieces.append(ref[:, off:off + w])
                    off += w
        out_pieces, acc_vals = fn(pieces, [b[...] for b in brefs])
        k = 0
        for (widths, dt), oref in zip(outs, orefs):
            off = 0
            for w in widths:
                oref[:, off:off + w] = out_pieces[k].astype(dt)
                k += 1
                off += w
        if na:
            @pl.when(pl.program_id(0) == 0)
            def _():
                for a in arefs:
                    a[...] = jnp.zeros(a.shape, F32)
            for a, v in zip(arefs, acc_vals):
                a[...] += v

    in_specs = []
    for r in rows:
        if r.halo is None:
            in_specs.append(pl.BlockSpec((tm, r.width), lambda i, cb=r.cb: (i, cb)))
        elif r.halo == "prev":
            in_specs.append(pl.BlockSpec((SUB, r.width), lambda i, cb=r.cb: (jnp.maximum(i * (tm // SUB) - 1, 0), cb)))
        else:
            in_specs.append(pl.BlockSpec((SUB, r.width), lambda i, cb=r.cb: (jnp.minimum((i + 1) * (tm // SUB), S // SUB - 1), cb)))
    in_specs += [pl.BlockSpec(b.shape, lambda i, nd=b.ndim: (0,) * nd) for b in bcs]
    out_specs = [pl.BlockSpec((tm, sum(w)), lambda i: (i, 0)) for w, _ in outs]
    out_specs += [pl.BlockSpec(s, lambda i: (0, 0)) for s in accs]
    out_shape = [jax.ShapeDtypeStruct((S, sum(w)), dt) for w, dt in outs]
    out_shape += [jax.ShapeDtypeStruct(s, F32) for s in accs]
    res = pl.pallas_call(body, name=name, grid=(n,), in_specs=in_specs, out_specs=out_specs, out_shape=out_shape,
                         compiler_params=_cparams(("arbitrary",)))(*[r.arr for r in rows], *bcs)
    return res


def _rw_fwd(name, f, rows, bcs, outs, tm):
    def fn(pieces, bvals):
        return list(f(*[p.astype(F32) for p in pieces], *[b.astype(F32) for b in bvals])), []
    return _rowwise(name, fn, rows, bcs, outs, [], tm)


def _npieces(rows):
    return sum(1 if r.splits is None else len(r.splits) for r in rows)


def _rw_bwd(name, f, rows, bcs, cts, drow, dbc, outs, tm, add=None):
    np_, nct = _npieces(rows), _npieces(cts)

    def fn(pieces, bvals):
        allv = [p.astype(F32) for p in pieces[:np_]] + [b.astype(F32) for b in bvals]
        ct = [p.astype(F32) for p in pieces[np_:np_ + nct]]
        didx = [i for i, m in enumerate(list(drow) + list(dbc)) if m]

        def g(*dv):
            full = list(allv)
            for i, v in zip(didx, dv):
                full[i] = v
            return tuple(f(*full))

        _, vjp = jax.vjp(g, *[allv[i] for i in didx])
        grads = vjp(tuple(ct))
        nrd = sum(bool(m) for m in drow)
        rg, bg = list(grads[:nrd]), list(grads[nrd:])
        if add is not None:
            rg[0] = rg[0] + pieces[np_ + nct].astype(F32)
        return rg, bg

    accs = [b.shape for b, m in zip(bcs, dbc) if m]
    return _rowwise(name, fn, list(rows) + list(cts) + ([add] if add is not None else []), bcs, outs, accs, tm)


def _sigmoid(x):
    return 1.0 / (1.0 + jnp.exp(-x))


def _silu(x):
    return x * _sigmoid(x)


def _softplus(x):
    return jnp.maximum(x, 0.0) + jnp.log(1.0 + jnp.exp(-jnp.abs(x)))


def f_mod(x, g, shift, scale):
    y = x * lax.rsqrt(jnp.mean(x * x, axis=-1, keepdims=True) + EPS)
    return (y * g * (1.0 + scale) + shift,)


def f_rms(x, g):
    return (x * lax.rsqrt(jnp.mean(x * x, axis=-1, keepdims=True) + EPS) * g,)


def f_act(gate, up):
    return (_silu(gate) * up,)


def make_f_res(coef):
    def f_res(y, gate):
        return (coef * gate * y,)
    return f_res


def f_gdnpre(*p):
    out = []
    for i, t in enumerate(p):
        t = _silu(t)
        if i < 2 * NH:
            t = t * lax.rsqrt(jnp.sum(t * t, axis=-1, keepdims=True) + EPS)
        out.append(t)
    return tuple(out)


def f_gates(gates, a_log, dt_bias):
    return _sigmoid(gates), -jnp.exp(a_log) * _softplus(gates + dt_bias)


def f_gdnpost(*a):
    o, z, g = a[:NH], a[NH:2 * NH], a[2 * NH]
    out = []
    for oh, zh in zip(o, z):
        y = oh * lax.rsqrt(jnp.mean(oh * oh, axis=-1, keepdims=True) + EPS) * g
        out.append(y * _silu(zh))
    return tuple(out)


def make_f_qk(shared_rope):
    def f(*a):
        if shared_rope:
            ns, rs = a[:NH], [a[NH]] * NH
            cosp, sins, gn, gr, pm = a[NH + 1:NH + 6]
        else:
            ns, rs = a[0:2 * NH:2], a[1:2 * NH:2]
            cosp, sins, gn, gr, pm = a[2 * NH:2 * NH + 5]
        out = []
        for n, r in zip(ns, rs):
            ss = jnp.sum(n * n, axis=-1, keepdims=True) + jnp.sum(r * r, axis=-1, keepdims=True)
            rstd = lax.rsqrt(ss * (1.0 / QKH) + EPS)
            yn = n * rstd * gn
            yr = r * rstd * gr
            sw = jnp.dot(yr, pm, precision=HI, preferred_element_type=F32)
            out += [yn, yr * cosp + sw * sins]
        return tuple(out)
    return f


def _matmul(name, a, b, mode="nn", out_dtype=F32, precise=False, lay=None, li=0, into=None, nmat=1):
    if lay == "b_cols":
        per = b.shape[3]
        rb, cb = b.shape[2], 4 * per
    elif lay == "b_rows":
        per = b.shape[2]
        rb, cb = 4 * per, b.shape[3]
    elif lay == "b_stack":
        rb, cb = b.shape[1:]
    else:
        rb, cb = b.shape
    if mode == "nn":
        (M, K), N = a.shape, cb
    elif mode == "nt":
        (M, K), N = a.shape, rb
    else:
        (K, M), N = a.shape, cb
    tm = _tile(M, (1024, 512, 256, 128))
    tn = _tile(N, (1408, 1024, 512, 384, 256, 128))
    tk = _tile(K, (1408, 1024, 512, 384, 256, 128))
    if lay == "b_cols":
        tn, tk = (per, tk) if mode == "nn" else (tn, per)
    elif lay == "b_rows":
        tm, tn, tk = (tm, 512, K) if mode == "nn" else (min(tm, 512), N, tk)
    elif lay == "o_cols":
        per = N // 4
        tn = per
    elif lay == "o_rows":
        per = M // 4
        tm, tn = M, 512
    nk = K // tk
    dims = {"nn": (((1,), (0,)), ((), ())), "nt": (((1,), (1,)), ((), ())), "tn": (((0,), (0,)), ((), ()))}[mode]

    def body(a_ref, b_ref, *rest):
        o_ref, acc_ref = rest[-2:]
        k = pl.program_id(2)

        @pl.when(k == 0)
        def _():
            acc_ref[...] = jnp.zeros(acc_ref.shape, F32)

        bv = b_ref[...]
        if lay == "b_rows":
            bv = bv.reshape(4 * per, bv.shape[2])
        if precise:
            acc_ref[...] += lax.dot_general(a_ref[...].astype(F32), bv.astype(F32), dims, precision=HI,
                                            preferred_element_type=F32)
        else:
            acc_ref[...] += lax.dot_general(a_ref[...].astype(BF16), bv.astype(BF16), dims, preferred_element_type=F32)

        @pl.when(k == nk - 1)
        def _():
            if lay == "o_rows":
                for s in range(4):
                    o_ref[s] = acc_ref[s * per:(s + 1) * per, :].astype(o_ref.dtype)
            else:
                o_ref[...] = acc_ref[...].astype(o_ref.dtype)

    a_spec = pl.BlockSpec((tk, tm), lambda i, j, k: (k, i)) if mode == "tn" else pl.BlockSpec((tm, tk), lambda i, j, k: (i, k))
    if lay == "b_cols":
        b_spec = (pl.BlockSpec((None, None, tk, per), lambda i, j, k: (j, li, k, 0)) if mode == "nn" else
                  pl.BlockSpec((None, None, tn, per), lambda i, j, k: (k, li, j, 0)))
    elif lay == "b_rows":
        b_spec = (pl.BlockSpec((4, None, per, tn), lambda i, j, k: (0, li, 0, j)) if mode == "nn" else
                  pl.BlockSpec((4, None, per, tk), lambda i, j, k: (0, li, 0, k)))
    elif lay == "b_stack":
        b_spec = pl.BlockSpec((None, tk, tn), lambda i, j, k: (li, k, j))
    elif mode == "nt":
        b_spec = pl.BlockSpec((tn, tk), lambda i, j, k: (j, k))
    else:
        b_spec = pl.BlockSpec((tk, tn), lambda i, j, k: (k, j))
    if lay == "o_stack":
        o_spec = pl.BlockSpec((None, tm, tn), lambda i, j, k: (li, i, j))
        o_shape = jax.ShapeDtypeStruct((nmat, M, N), out_dtype)
    elif lay == "o_cols":
        o_spec = pl.BlockSpec((None, None, tm, per), lambda i, j, k: (j, li, i, 0))
        o_shape = jax.ShapeDtypeStruct((4, nmat, M, per), out_dtype)
    elif lay == "o_rows":
        o_spec = pl.BlockSpec((4, None, per, tn), lambda i, j, k: (0, li, 0, j))
        o_shape = jax.ShapeDtypeStruct((4, nmat, per, N), out_dtype)
    else:
        o_spec = pl.BlockSpec((tm, tn), lambda i, j, k: (i, j))
        o_shape = jax.ShapeDtypeStruct((M, N), out_dtype)
    in_specs, args, alias = [a_spec, b_spec], [a, b], {}
    if into is not None:
        in_specs.append(pl.BlockSpec(memory_space=pl.ANY))
        args.append(into)
        alias = {2: 0}
    return pl.pallas_call(body, name=name, grid=(M // tm, N // tn, nk), in_specs=in_specs, out_specs=o_spec,
                          out_shape=o_shape, scratch_shapes=[pltpu.VMEM((tm, tn), F32)], input_output_aliases=alias,
                          compiler_params=_cparams(("parallel", "parallel", "arbitrary")))(*args)


def _shift_down(t, p, d):
    if d == 0:
        return t
    tr = pltpu.roll(t, d, 0)
    pr = pltpu.roll(p, d, 0)
    r8 = lax.broadcasted_iota(jnp.int32, p.shape, 0)
    first = jnp.where(r8 < d, pr, tr[:SUB])
    return jnp.concatenate([first, tr[SUB:]], axis=0)


def _shift_up(t, nx, d):
    if d == 0:
        return t
    tm = t.shape[0]
    tr = pltpu.roll(t, tm - d, 0)
    nr = pltpu.roll(nx, SUB - d, 0)
    r8 = lax.broadcasted_iota(jnp.int32, nx.shape, 0)
    last = jnp.where(r8 >= SUB - d, nr, tr[tm - SUB:])
    return jnp.concatenate([tr[:tm - SUB], last], axis=0)


def _conv_fwd(name, proj, w8, C, tm):
    def fn(pieces, bvals):
        t, p = pieces[0].astype(F32), pieces[1].astype(F32)
        w = bvals[0]
        p = jnp.where(pl.program_id(0) == 0, 0.0, p)
        out = w[3:4] * t
        for d in (1, 2, 3):
            out = out + w[3 - d:4 - d] * _shift_down(t, p, d)
        return [out], []
    return _rowwise(name, fn, [Row(proj, C), Row(proj, C, halo="prev")], [w8], [((C,), F32)], [], tm)[0]


def _conv_bwd(name, proj, dout, w8, C, tm, out_dtype):
    n = proj.shape[0] // tm

    def fn(pieces, bvals):
        t, p, g, gn = [v.astype(F32) for v in pieces]
        w = bvals[0]
        i = pl.program_id(0)
        p = jnp.where(i == 0, 0.0, p)
        gn = jnp.where(i == n - 1, 0.0, gn)
        dx = w[3:4] * g
        dws = [jnp.sum(g * t, axis=0, keepdims=True)]
        for d in (1, 2, 3):
            dx = dx + w[3 - d:4 - d] * _shift_up(g, gn, d)
            dws.append(jnp.sum(g * _shift_down(t, p, d), axis=0, keepdims=True))
        dw = jnp.concatenate([dws[3], dws[2], dws[1], dws[0], jnp.zeros((4, g.shape[1]), F32)], axis=0)
        return [dx], [dw]
    return _rowwise(name, fn, [Row(proj, C), Row(proj, C, halo="prev"), Row(dout), Row(dout, halo="next")], [w8],
                    [((C,), out_dtype)], [(SUB, C)], tm)


def _bdot(a, b, ca, cb):
    return lax.dot_general(a.astype(BF16), b.astype(BF16), (((ca,), (cb,)), ((0,), (0,))), preferred_element_type=F32)


def _bdot3(a, b, ca, cb):
    dims = (((ca,), (cb,)), ((0,), (0,)))
    ah, bh = a.astype(BF16), b.astype(BF16)
    al, bl = (a - ah.astype(F32)).astype(BF16), (b - bh.astype(F32)).astype(BF16)
    d = lambda x, y: lax.dot_general(x, y, dims, preferred_element_type=F32)
    return d(ah, bh) + (d(ah, bl) + d(al, bh))


@jax.custom_vjp
def _bmm3(a, b):
    return _bdot3(a, b, 2, 1)


_bmm3.defvjp(lambda a, b: (_bdot3(a, b, 2, 1), (a, b)),
             lambda res, g: (_bdot3(g, res[1], 2, 2), _bdot3(res[0], g, 1, 1)))


def _neumann(nl):
    C = nl.shape[1]
    eye = (lax.broadcasted_iota(jnp.int32, (1, C, C), 1) == lax.broadcasted_iota(jnp.int32, (1, C, C), 2)).astype(F32)
    T = eye + nl
    pw = nl
    for _ in range(C.bit_length() - 2):
        pw = _bdot3(pw, pw, 2, 1)
        T = T + _bdot3(T, pw, 2, 1)
    return T


_unit_lower_inv = jax.custom_vjp(_neumann)


def _unit_lower_inv_fwd(nl):
    T = _neumann(nl)
    return T, T


def _unit_lower_inv_bwd(T, g):
    return (_bdot3(_bdot3(T, g, 1, 1), T, 2, 2),)


_unit_lower_inv.defvjp(_unit_lower_inv_fwd, _unit_lower_inv_bwd)


@jax.custom_vjp
def _known_inv(nl, T):
    return T


_known_inv.defvjp(lambda nl, T: (T, T), lambda T, g: (_unit_lower_inv_bwd(T, g)[0], jnp.zeros_like(T)))


def _gdn_chunk(q, k, v, gcol, grow, bcol, S, T_saved=None):
    C = CHUNK
    ii = lax.broadcasted_iota(jnp.int32, (1, C, C), 1)
    jj = lax.broadcasted_iota(jnp.int32, (1, C, C), 2)
    incl, strict = ii >= jj, ii > jj
    gc_col = jnp.sum(jnp.where(incl, 1.0, 0.0) * grow, axis=2, keepdims=True)
    gc_row = jnp.sum(jnp.where(jj >= ii, 1.0, 0.0) * gcol, axis=1, keepdims=True)
    decay = jnp.where(incl, jnp.exp(jnp.where(incl, gc_col - gc_row, 0.0)), 0.0)
    qs = q * (DH ** -0.5)
    kb = k * bcol
    nl = -jnp.where(strict, _bdot(kb, k, 2, 2) * decay, 0.0)
    T = _unit_lower_inv(nl) if T_saved is None else _known_inv(nl, T_saved)
    egc = jnp.exp(gc_col)
    u = _bmm3(T, v * bcol)
    w = _bmm3(T, kb * egc)
    att = jnp.where(incl, _bdot(qs, k, 2, 2) * decay, 0.0)
    v_new = u - _bdot(w, S, 2, 1)
    o = _bdot(qs * egc, S, 2, 1) + _bdot(att, v_new, 2, 1)
    g_last = jnp.sum(grow, axis=2, keepdims=True)
    k_dec = k * jnp.exp(g_last - gc_col)
    S_out = S * jnp.exp(g_last) + _bdot(k_dec, v_new, 1, 1)
    return o, S_out, T


def _heads(ref, w):
    return jnp.stack([ref[:, h * w:(h + 1) * w] for h in range(NH)])


def _gdn_specs(NC, rev):
    ix = (lambda i: NC - 1 - i) if rev else (lambda i: i)
    wide = pl.BlockSpec((CHUNK, D), lambda i: (ix(i), 0))
    col = pl.BlockSpec((CHUNK, NH), lambda i: (ix(i), 0))
    row = pl.BlockSpec((1, NH, CHUNK), lambda i: (ix(i), 0, 0))
    st = pl.BlockSpec((1, NH, DH, DH), lambda i: (ix(i), 0, 0, 0))
    tinv = pl.BlockSpec((1, NH, CHUNK, CHUNK), lambda i: (ix(i), 0, 0, 0))
    return wide, col, row, st, tinv


def _gdn_fwd(name, q, k, v, gcol, grow, bcol):
    S = q.shape[0]
    NC = S // CHUNK

    def body(q_ref, k_ref, v_ref, gc_ref, gr_ref, b_ref, o_ref, ss_ref, t_ref, st):
        @pl.when(pl.program_id(0) == 0)
        def _():
            st[...] = jnp.zeros(st.shape, F32)
        s_in = st[...]
        ss_ref[0] = s_in
        grow = jnp.stack([gr_ref[0, h:h + 1, :] for h in range(NH)])
        o, s_out, tinv = _gdn_chunk(_heads(q_ref, DH), _heads(k_ref, DH), _heads(v_ref, DH), _heads(gc_ref, 1), grow,
                                    _heads(b_ref, 1), s_in)
        for h in range(NH):
            o_ref[:, h * DH:(h + 1) * DH] = o[h]
        t_ref[0] = tinv
        st[...] = s_out

    wide, col, row, stsp, tsp = _gdn_specs(NC, False)
    return pl.pallas_call(body, name=name, grid=(NC,), in_specs=[wide, wide, wide, col, row, col],
                          out_specs=[wide, stsp, tsp],
                          out_shape=[jax.ShapeDtypeStruct((S, D), F32), jax.ShapeDtypeStruct((NC, NH, DH, DH), F32),
                                     jax.ShapeDtypeStruct((NC, NH, CHUNK, CHUNK), F32)],
                          scratch_shapes=[pltpu.VMEM((NH, DH, DH), F32)],
                          compiler_params=_cparams(("arbitrary",)))(q, k, v, gcol, grow, bcol)


def _gdn_bwd(name, q, k, v, gcol, grow, bcol, ssave, tsave, do):
    S = q.shape[0]
    NC = S // CHUNK

    def body(q_ref, k_ref, v_ref, gc_ref, gr_ref, b_ref, ss_ref, t_ref, do_ref, dq_ref, dk_ref, dv_ref, dgc_ref, dgr_ref,
             db_ref, dst):
        @pl.when(pl.program_id(0) == 0)
        def _():
            dst[...] = jnp.zeros(dst.shape, F32)
        grow = jnp.stack([gr_ref[0, h:h + 1, :] for h in range(NH)])
        prim = (_heads(q_ref, DH), _heads(k_ref, DH), _heads(v_ref, DH), _heads(gc_ref, 1), grow, _heads(b_ref, 1), ss_ref[0])
        tinv = t_ref[0]
        _, vjp = jax.vjp(lambda *p: _gdn_chunk(*p, T_saved=tinv)[:2], *prim)
        dq, dk, dv, dgc, dgr, db, ds = vjp((_heads(do_ref, DH), dst[...]))
        for h in range(NH):
            hs = slice(h * DH, (h + 1) * DH)
            dq_ref[:, hs] = dq[h]
            dk_ref[:, hs] = dk[h]
            dv_ref[:, hs] = dv[h]
            dgc_ref[:, h:h + 1] = dgc[h]
            dgr_ref[0, h:h + 1, :] = dgr[h]
            db_ref[:, h:h + 1] = db[h]
        dst[...] = ds

    wide, col, row, stsp, tsp = _gdn_specs(NC, True)
    return pl.pallas_call(body, name=name, grid=(NC,), in_specs=[wide, wide, wide, col, row, col, stsp, tsp, wide],
                          out_specs=[wide, wide, wide, col, row, col],
                          out_shape=[jax.ShapeDtypeStruct((S, D), F32)] * 3 + [jax.ShapeDtypeStruct((S, NH), F32),
                                                                                 jax.ShapeDtypeStruct((NC, NH, CHUNK), F32),
                                                                                 jax.ShapeDtypeStruct((S, NH), F32)],
                          scratch_shapes=[pltpu.VMEM((NH, DH, DH), F32)],
                          compiler_params=_cparams(("arbitrary",)))(q, k, v, gcol, grow, bcol, ssave, tsave, do)


TQ = 512
SM_SCALE = QKH ** -0.5
NEG = -1e30


def _diag_mask(transposed):
    r = lax.broadcasted_iota(jnp.int32, (TQ, TQ), 0) // CHUNK
    c = lax.broadcasted_iota(jnp.int32, (TQ, TQ), 1) // CHUNK
    return (r <= c) if transposed else (c <= r)


def _dot_nt(a, b):
    return lax.dot_general(a, b, (((1,), (1,)), ((), ())), preferred_element_type=F32)


def _flash_fwd(name, qp, kp, kv):
    S = qp.shape[0]
    nq = S // (2 * TQ)

    def body(q_ref, k_ref, v_ref, o_ref, lse_ref):
        qi = pl.program_id(1)
        qs = (q_ref[:TQ, :], q_ref[TQ:, :])

        def step(q, j, carry, masked):
            m, l, acc = carry
            rows = pl.ds(pl.multiple_of(j * TQ, TQ), TQ)
            s = _dot_nt(q, k_ref[rows, :]) * SM_SCALE
            if masked:
                s = jnp.where(_diag_mask(False), s, NEG)
            m_new = jnp.maximum(m, jnp.max(s, axis=-1, keepdims=True))
            p = jnp.exp(s - m_new)
            alpha = jnp.exp(m - m_new)
            l = alpha * l + jnp.sum(p, axis=-1, keepdims=True)
            acc = alpha * acc + jnp.dot(p.astype(BF16), v_ref[rows, :].astype(BF16), preferred_element_type=F32)
            return m_new, l, acc

        init = (jnp.full((TQ, 1), NEG, F32), jnp.zeros((TQ, 1), F32), jnp.zeros((TQ, DH), F32))
        ca, cb = lax.fori_loop(0, 2 * qi, lambda j, c: (step(qs[0], j, c[0], False), step(qs[1], j, c[1], False)),
                               (init, init))
        ca = step(qs[0], 2 * qi, ca, True)
        cb = step(qs[1], 2 * qi + 1, step(qs[1], 2 * qi, cb, False), True)
        for u, (m, l, acc) in enumerate((ca, cb)):
            o_ref[u * TQ:(u + 1) * TQ, :] = acc / l
            lse_ref[0, u * TQ:(u + 1) * TQ, :] = m + jnp.log(l)

    return pl.pallas_call(
        body, name=name, grid=(NH, nq),
        in_specs=[pl.BlockSpec((2 * TQ, HP), lambda h, i: (i, h)), pl.BlockSpec((S, HP), lambda h, i: (0, h)),
                  pl.BlockSpec((S, DH), lambda h, i: (0, NH + h))],
        out_specs=[pl.BlockSpec((2 * TQ, DH), lambda h, i: (i, h)), pl.BlockSpec((1, 2 * TQ, 1), lambda h, i: (h, i, 0))],
        out_shape=[jax.ShapeDtypeStruct((S, NH * DH), F32), jax.ShapeDtypeStruct((NH, S, 1), F32)],
        compiler_params=_cparams(("parallel", "arbitrary")))(qp, kp, kv)


def _flash_bwd_dq(name, qp, kp, kv, o, do, lse):
    S = qp.shape[0]
    nq = S // (2 * TQ)

    def body(q_ref, k_ref, v_ref, o_ref, do_ref, lse_ref, dq_ref, dl_ref):
        qi = pl.program_id(1)
        subs = []
        for u in range(2):
            sl = slice(u * TQ, (u + 1) * TQ)
            do = do_ref[sl, :]
            delta = jnp.sum(o_ref[sl, :] * do, axis=-1, keepdims=True)
            dl_ref[0, sl, :] = delta
            subs.append((q_ref[sl, :], do.astype(BF16), lse_ref[0, sl, :], delta))

        def step(sub, j, dq, masked):
            q, dob, lse, delta = sub
            rows = pl.ds(pl.multiple_of(j * TQ, TQ), TQ)
            k = k_ref[rows, :]
            s = _dot_nt(q, k) * SM_SCALE
            if masked:
                s = jnp.where(_diag_mask(False), s, NEG)
            p = jnp.exp(s - lse)
            dp = _dot_nt(dob, v_ref[rows, :].astype(BF16))
            ds = p * (dp - delta) * SM_SCALE
            return dq + jnp.dot(ds.astype(BF16), k, preferred_element_type=F32)

        zero = jnp.zeros((TQ, HP), F32)
        dqa, dqb = lax.fori_loop(0, 2 * qi, lambda j, c: (step(subs[0], j, c[0], False), step(subs[1], j, c[1], False)),
                                 (zero, zero))
        dq_ref[:TQ, :] = step(subs[0], 2 * qi, dqa, True)
        dq_ref[TQ:, :] = step(subs[1], 2 * qi + 1, step(subs[1], 2 * qi, dqb, False), True)

    return pl.pallas_call(
        body, name=name, grid=(NH, nq),
        in_specs=[pl.BlockSpec((2 * TQ, HP), lambda h, i: (i, h)), pl.BlockSpec((S, HP), lambda h, i: (0, h)),
                  pl.BlockSpec((S, DH), lambda h, i: (0, NH + h)), pl.BlockSpec((2 * TQ, DH), lambda h, i: (i, h)),
                  pl.BlockSpec((2 * TQ, DH), lambda h, i: (i, h)), pl.BlockSpec((1, 2 * TQ, 1), lambda h, i: (h, i, 0))],
        out_specs=[pl.BlockSpec((2 * TQ, HP), lambda h, i: (i, h)), pl.BlockSpec((1, 2 * TQ, 1), lambda h, i: (h, i, 0))],
        out_shape=[jax.ShapeDtypeStruct((S, NH * HP), F32), jax.ShapeDtypeStruct((NH, S, 1), F32)],
        compiler_params=_cparams(("parallel", "arbitrary")))(qp, kp, kv, o, do, lse)


def _flash_bwd_dkv(name, qp, kp, kv, do, lse_row, delta_row):
    S = qp.shape[0]
    nq = S // TQ

    def body(q_ref, k_ref, v_ref, do_ref, lse_ref, dl_ref, dk_ref, dv_ref):
        kj = pl.program_id(1)
        subs = [(k_ref[u * TQ:(u + 1) * TQ, :], v_ref[u * TQ:(u + 1) * TQ, :].astype(BF16)) for u in range(2)]

        def step(sub, i, carry, masked):
            k, vb = sub
            dk, dv = carry
            rows = pl.ds(pl.multiple_of(i * TQ, TQ), TQ)
            q = q_ref[rows, :]
            dob = do_ref[rows, :].astype(BF16)
            st = _dot_nt(k, q) * SM_SCALE
            pt = jnp.exp(st - lse_ref[0, :, rows])
            if masked:
                pt = jnp.where(_diag_mask(True), pt, 0.0)
            dv = dv + jnp.dot(pt.astype(BF16), dob, preferred_element_type=F32)
            dpt = _dot_nt(vb, dob)
            dst = pt * (dpt - dl_ref[0, :, rows]) * SM_SCALE
            dk = dk + jnp.dot(dst.astype(BF16), q, preferred_element_type=F32)
            return dk, dv

        zero = (jnp.zeros((TQ, HP), F32), jnp.zeros((TQ, DH), F32))
        ca = step(subs[0], 2 * kj + 1, step(subs[0], 2 * kj, zero, True), False)
        cb = step(subs[1], 2 * kj + 1, zero, True)
        ca, cb = lax.fori_loop(2 * kj + 2, nq, lambda i, c: (step(subs[0], i, c[0], False), step(subs[1], i, c[1], False)),
                               (ca, cb))
        for u, (dk, dv) in enumerate((ca, cb)):
            dk_ref[u * TQ:(u + 1) * TQ, :] = dk
            dv_ref[u * TQ:(u + 1) * TQ, :] = dv

    return pl.pallas_call(
        body, name=name, grid=(NH, nq // 2),
        in_specs=[pl.BlockSpec((S, HP), lambda h, j: (0, h)), pl.BlockSpec((2 * TQ, HP), lambda h, j: (j, h)),
                  pl.BlockSpec((2 * TQ, DH), lambda h, j: (j, NH + h)), pl.BlockSpec((S, DH), lambda h, j: (0, h)),
                  pl.BlockSpec((1, 1, S), lambda h, j: (h, 0, 0)), pl.BlockSpec((1, 1, S), lambda h, j: (h, 0, 0))],
        out_specs=[pl.BlockSpec((2 * TQ, HP), lambda h, j: (j, h)), pl.BlockSpec((2 * TQ, DH), lambda h, j: (j, h))],
        out_shape=[jax.ShapeDtypeStruct((S, NH * HP), F32), jax.ShapeDtypeStruct((S, NH * DH), F32)],
        compiler_params=_cparams(("parallel", "arbitrary")))(qp, kp, kv, do, lse_row, delta_row)


def _tm(S, width):
    t = 512 if width <= 1024 else (256 if width <= 3072 else 128)
    return min(t, S)


def _mod_fwd(tag, x, g, shift, scale):
    S = x.shape[0]
    return _rw_fwd(tag + "_mod", f_mod, [Row(x)], [g, shift, scale], [((D,), BF16)], _tm(S, D))[0]


def _mod_bwd(tag, x, g, shift, scale, dh, dx_direct):
    S = x.shape[0]
    r = _rw_bwd(tag + "_mod_b", f_mod, [Row(x)], [g, shift, scale], [Row(dh)], [True], [True] * 3, [((D,), F32)],
                _tm(S, D), add=Row(dx_direct))
    return r[0], r[1:]


def _res_fwd(tag, x, y, gate, coef):
    S = x.shape[0]

    def fn(pieces, bvals):
        return [pieces[0] + coef * bvals[0] * pieces[1]], []
    return _rowwise(tag + "_res", fn, [Row(x), Row(y)], [gate], [((D,), F32)], [], _tm(S, D))[0]


def _res_bwd(tag, y, gate, dxn, coef):
    S = y.shape[0]
    r = _rw_bwd(tag + "_res_b", make_f_res(coef), [Row(y)], [gate], [Row(dxn)], [True], [True], [((D,), BF16)], _tm(S, D))
    return r[0], r[1]


def _ffn_fwd(tag, x, mod3, g, w_in4, w_out4, li):
    shift, scale, gate = mod3
    S = x.shape[0]
    h = _mod_fwd(tag, x, g, shift, scale)
    gu = _matmul(tag + "_in", h, w_in4, lay="b_cols", li=li, out_dtype=BF16)
    a = _rw_fwd(tag + "_act", f_act, [Row(gu, splits=[FF, FF])], [], [((FF,), BF16)], _tm(S, FF))[0]
    y = _matmul(tag + "_out", a, w_out4, lay="b_rows", li=li)
    xn = _res_fwd(tag, x, y, gate, 0.5)
    return xn, (x, h, gu, a, y)


def _ffn_bwd(tag, dxn, res, mod3, g, w_in4, w_out4, li, g_in4, g_out4):
    shift, scale, gate = mod3
    x, h, gu, a, y = res
    S = x.shape[0]
    nmat = w_in4.shape[1]
    dy, dgate = _res_bwd(tag, y, gate, dxn, 0.5)
    da = _matmul(tag + "_out_bi", dy, w_out4, "nt", lay="b_rows", li=li, out_dtype=BF16)
    g_out4 = _matmul(tag + "_out_bw", a, dy, "tn", lay="o_rows", li=li, into=g_out4, nmat=nmat, out_dtype=BF16)
    dgu = _rw_bwd(tag + "_act_b", f_act, [Row(gu, splits=[FF, FF])], [], [Row(da)], [True, True], [],
                  [((FF, FF), BF16)], _tm(S, FF))[0]
    dh = _matmul(tag + "_in_bi", dgu, w_in4, "nt", lay="b_cols", li=li)
    g_in4 = _matmul(tag + "_in_bw", h, dgu, "tn", lay="o_cols", li=li, into=g_in4, nmat=nmat, out_dtype=BF16)
    dx, (dg, dshift, dscale) = _mod_bwd(tag, x, g, shift, scale, dh, dxn)
    return dx, g_in4, g_out4, dict(g=dg, mod=(dshift, dscale, dgate))


def _pad_lanes(a, lo, width=LANE):
    return jnp.pad(a, ((0, 0), (lo, width - lo - a.shape[1])))


def _gdn_layer_fwd(tag, x, mod3, g, p):
    shift, scale, gate = mod3
    S = x.shape[0]
    NC = S // CHUNK
    h = _mod_fwd(tag, x, g, shift, scale)
    proj = _matmul(tag + "_in", h, p["w_in"])
    qc = _conv_fwd(tag + "_conv", proj, p["conv_w8"], 3 * D, _tm(S, 3 * D))
    q, k, v = _rw_fwd(tag + "_pre", f_gdnpre, [Row(qc, splits=[DH] * (3 * NH))], [],
                      [((DH,) * NH, F32)] * 3, _tm(S, 3 * D))
    betaf, gf = _rw_fwd(tag + "_gates", f_gates, [Row(proj, LANE, cb=GATE_CB)], [p["a_log128"], p["dt_bias128"]],
                        [((LANE,), F32)] * 2, _tm(S, LANE))
    bcol, gcol = betaf[:, :NH], gf[:, NH:2 * NH]
    grow = gcol.reshape(NC, CHUNK, NH).transpose(0, 2, 1)
    o, ssave, tsave = _gdn_fwd(tag + "_core", q, k, v, gcol, grow, bcol)
    on = _rw_fwd(tag + "_post", f_gdnpost, [Row(o, splits=[DH] * NH), Row(proj, D, cb=3, splits=[DH] * NH)],
                 [p["norm_g"]], [((DH,) * NH, BF16)], _tm(S, 2 * D))[0]
    y = _matmul(tag + "_out", on, p["w_out"])
    xn = _res_fwd(tag, x, y, gate, 1.0)
    return xn, (x, h, proj, qc, q, k, v, gcol, grow, bcol, ssave, tsave, o, on, y)


def _gdn_layer_bwd(tag, dxn, res, mod3, g, p):
    shift, scale, gate = mod3
    x, h, proj, qc, q, k, v, gcol, grow, bcol, ssave, tsave, o, on, y = res
    S = x.shape[0]
    dy, dgate = _res_bwd(tag, y, gate, dxn, 1.0)
    don = _matmul(tag + "_out_bi", dy, p["w_out"], "nt")
    dw_out = _matmul(tag + "_out_bw", on, dy, "tn")
    do, dz, dnorm = _rw_bwd(tag + "_post_b", f_gdnpost, [Row(o, splits=[DH] * NH), Row(proj, D, cb=3, splits=[DH] * NH)],
                            [p["norm_g"]], [Row(don, splits=[DH] * NH)], [True] * (2 * NH), [True],
                            [((DH,) * NH, F32), ((DH,) * NH, BF16)], _tm(S, 2 * D))
    dq, dk, dv, dgc, dgr, db = _gdn_bwd(tag + "_core_b", q, k, v, gcol, grow, bcol, ssave, tsave, do)
    dgcol = dgc + dgr.transpose(0, 2, 1).reshape(S, NH)
    dgates, da_log, ddt = _rw_bwd(tag + "_gates_b", f_gates, [Row(proj, LANE, cb=GATE_CB)], [p["a_log128"], p["dt_bias128"]],
                                  [Row(_pad_lanes(db, 0)), Row(_pad_lanes(dgcol, NH))], [True], [True, True],
                                  [((LANE,), BF16)], _tm(S, LANE))
    dqc = _rw_bwd(tag + "_pre_b", f_gdnpre, [Row(qc, splits=[DH] * (3 * NH))], [],
                  [Row(dq, splits=[DH] * NH), Row(dk, splits=[DH] * NH), Row(dv, splits=[DH] * NH)],
                  [True] * (3 * NH), [], [((DH,) * (3 * NH), F32)], _tm(S, 3 * D))[0]
    dqkv, dconv = _conv_bwd(tag + "_conv_b", proj, dqc, p["conv_w8"], 3 * D, _tm(S, 3 * D), BF16)
    dproj = jnp.concatenate([dqkv, dz, dgates], axis=1)
    dh = _matmul(tag + "_in_bi", dproj, p["w_in"], "nt")
    dw_in = _matmul(tag + "_in_bw", h, dproj, "tn")
    dx, (dg, dshift, dscale) = _mod_bwd(tag, x, g, shift, scale, dh, dxn)
    return dx, dict(w_in=dw_in, conv_w8=dconv, a_log128=da_log, dt_bias128=ddt, norm_g=dnorm,
                    w_out=dw_out, g=dg, mod=(dshift, dscale, dgate))


def _qk_rows(src, shared_rope, ckv=None):
    if shared_rope:
        return [Row(src, D, cb=0, splits=[DH] * NH), Row(ckv, LANE, cb=2)]
    return [Row(src, splits=[DH] * (2 * NH))]


def _kv_fwd(x, kvmod, p, tabs):
    shift, scale = kvmod
    S = x.shape[0]
    h = _mod_fwd("kv", x, p["kv_norm_g"], shift, scale)
    ckv = _matmul("kv_dkv", h, p["w_dkv"])
    lat = _rw_fwd("kv_lat", f_rms, [Row(ckv, KVL)], [p["kv_lat_g"]], [((KVL,), BF16)], _tm(S, KVL))[0]
    kvf = _matmul("kv_ukv", lat, p["w_ukv"])
    kp = _rw_fwd("kv_k", make_f_qk(True), _qk_rows(kvf, True, ckv) + [Row(tabs[0]), Row(tabs[1])],
                 [p["k_gn"], p["k_gr"], p["pm"]], [((DH,) * (2 * NH), BF16)], _tm(S, 2 * D))[0]
    return kp, kvf, (x, h, ckv, lat)


def _kv_bwd(dkp, dv, dx_direct, res, kvmod, kvf, p, tabs):
    shift, scale = kvmod
    x, h, ckv, lat = res
    S = x.shape[0]
    dkn, dkr, dgn, dgr = _rw_bwd("kv_k_b", make_f_qk(True), _qk_rows(kvf, True, ckv) + [Row(tabs[0]), Row(tabs[1])],
                                 [p["k_gn"], p["k_gr"], p["pm"]], [Row(dkp, splits=[DH] * (2 * NH))],
                                 [True] * (NH + 1) + [False, False], [True, True, False],
                                 [((DH,) * NH, BF16), ((LANE,), BF16)], _tm(S, 2 * D))
    dkvf = jnp.concatenate([dkn, dv.astype(BF16)], axis=1)
    dlat = _matmul("kv_ukv_bi", dkvf, p["w_ukv"], "nt")
    dw_ukv = _matmul("kv_ukv_bw", lat, dkvf, "tn")
    dcl, dlg = _rw_bwd("kv_lat_b", f_rms, [Row(ckv, KVL)], [p["kv_lat_g"]], [Row(dlat)], [True], [True],
                       [((KVL,), BF16)], _tm(S, KVL))
    dckv = jnp.concatenate([dcl, dkr], axis=1)
    dh = _matmul("kv_dkv_bi", dckv, p["w_dkv"], "nt")
    dw_dkv = _matmul("kv_dkv_bw", h, dckv, "tn")
    dx, (dg, dshift, dscale) = _mod_bwd("kv", x, p["kv_norm_g"], shift, scale, dh, dx_direct)
    return dx, dict(w_dkv=dw_dkv, w_ukv=dw_ukv, kv_lat_g=dlg, k_gn=dgn, k_gr=dgr, kv_norm_g=dg, mod=(dshift, dscale))


def _mla_layer_fwd(tag, x, mod3, g, p, kp, kvf, tabs):
    shift, scale, gate = mod3
    S = x.shape[0]
    h = _mod_fwd(tag, x, g, shift, scale)
    ql = _matmul(tag + "_dq", h, p["w_dq"])
    qln = _rw_fwd(tag + "_qln", f_rms, [Row(ql)], [p["ql_g"]], [((QL,), BF16)], _tm(S, QL))[0]
    qu = _matmul(tag + "_uq", qln, p["w_uq"])
    qp = _rw_fwd(tag + "_q", make_f_qk(False), _qk_rows(qu, False) + [Row(tabs[0]), Row(tabs[1])],
                 [p["q_gn"], p["q_gr"], p["pm"]], [((DH,) * (2 * NH), BF16)], _tm(S, 2 * D))[0]
    o, lse = _flash_fwd(tag + "_att", qp, kp, kvf)
    y = _matmul(tag + "_out", o, p["w_out"])
    xn = _res_fwd(tag, x, y, gate, 1.0)
    return xn, (x, h, ql, qln, qu, qp, o, lse, y)


def _mla_layer_bwd(tag, dxn, res, mod3, g, p, kp, kvf, tabs):
    shift, scale, gate = mod3
    x, h, ql, qln, qu, qp, o, lse, y = res
    S = x.shape[0]
    dy, dgate = _res_bwd(tag, y, gate, dxn, 1.0)
    do = _matmul(tag + "_out_bi", dy, p["w_out"], "nt")
    dw_out = _matmul(tag + "_out_bw", o, dy, "tn")
    dqp, delta = _flash_bwd_dq(tag + "_att_bq", qp, kp, kvf, o, do, lse)
    dkp, dv = _flash_bwd_dkv(tag + "_att_bkv", qp, kp, kvf, do, lse.reshape(NH, 1, S), delta.reshape(NH, 1, S))
    dqu, dgn, dgr = _rw_bwd(tag + "_q_b", make_f_qk(False), _qk_rows(qu, False) + [Row(tabs[0]), Row(tabs[1])],
                            [p["q_gn"], p["q_gr"], p["pm"]], [Row(dqp, splits=[DH] * (2 * NH))],
                            [True] * (2 * NH) + [False, False], [True, True, False],
                            [((DH,) * (2 * NH), BF16)], _tm(S, 2 * D))
    dqln = _matmul(tag + "_uq_bi", dqu, p["w_uq"], "nt")
    dw_uq = _matmul(tag + "_uq_bw", qln, dqu, "tn")
    dql, dqlg = _rw_bwd(tag + "_qln_b", f_rms, [Row(ql)], [p["ql_g"]], [Row(dqln)], [True], [True], [((QL,), BF16)],
                        _tm(S, QL))
    dh = _matmul(tag + "_dq_bi", dql, p["w_dq"], "nt")
    dw_dq = _matmul(tag + "_dq_bw", h, dql, "tn")
    dx, (dg, dshift, dscale) = _mod_bwd(tag, x, g, shift, scale, dh, dxn)
    return dx, dkp, dv, dict(w_dq=dw_dq, w_uq=dw_uq, w_out=dw_out, ql_g=dqlg, q_gn=dgn, q_gr=dgr, g=dg,
                             mod=(dshift, dscale, dgate))


def _loss_head(y, tgt):
    S = y.shape[0]

    def fn(pieces, bvals):
        e = pieces[0] - pieces[1]
        part = jnp.sum(e * e) * (0.5 / D)
        return [e * (1.0 / D)], [jnp.full((1, LANE), part, F32)]
    dy, part = _rowwise("loss", fn, [Row(y), Row(tgt)], [], [((D,), F32)], [(1, LANE)], _tm(S, D))
    return part[0, 0], dy


def _rope_tables(positions):
    S = positions.shape[0]
    half = ROPE // 2
    lane = lax.broadcasted_iota(jnp.int32, (1, LANE), 1)
    inv_freq = ROPE_BASE ** (-(lane % half).astype(F32) / half)
    live = (lane < ROPE).astype(F32)
    sign = jnp.where(lane < half, -1.0, 1.0) * live

    def fn(pieces, bvals):
        ang = pieces[0] * bvals[0]
        return [jnp.cos(ang) * bvals[1], jnp.sin(ang) * bvals[2]], []
    pos = jnp.broadcast_to(positions.astype(F32)[:, None], (S, LANE))
    cosp, sins = _rowwise("rope_tab", fn, [Row(pos)], [inv_freq, live, sign], [((LANE,), F32)] * 2, [], _tm(S, LANE))
    r = lax.broadcasted_iota(jnp.int32, (LANE, LANE), 0)
    c = lax.broadcasted_iota(jnp.int32, (LANE, LANE), 1)
    pm = (((c < half) & (r == c + half)) | ((c >= half) & (c < ROPE) & (r == c - half))).astype(F32)
    return (cosp, sins), pm


def _adamw(name, w, g, m, v):
    shape = w.shape
    C = shape[-1]
    R = w.size // C
    tr = R
    for t in (1024, 512, 256, 128, 64, 32, 16, 8):
        if R % t == 0 and t * C * 4 <= (1 << 21):
            tr = t
            break
    c1 = 1.0 - ADAM_B1 ** ADAM_STEP
    c2 = 1.0 - ADAM_B2 ** ADAM_STEP

    def body(w_ref, g_ref, m_ref, v_ref, d_ref, mo_ref, vo_ref):
        gg = g_ref[...]
        mn = ADAM_B1 * m_ref[...] + (1.0 - ADAM_B1) * gg
        vn = ADAM_B2 * v_ref[...] + (1.0 - ADAM_B2) * (gg * gg)
        d_ref[...] = -ADAM_LR * ((mn / c1) / (jnp.sqrt(vn / c2) + ADAM_EPS) + ADAM_WD * w_ref[...])
        mo_ref[...] = mn
        vo_ref[...] = vn

    spec = pl.BlockSpec((tr, C), lambda i: (i, 0))
    outs = pl.pallas_call(body, name=name, grid=(R // tr,), in_specs=[spec] * 4, out_specs=[spec] * 3,
                          out_shape=[jax.ShapeDtypeStruct((R, C), F32)] * 3,
                          compiler_params=_cparams(("parallel",)))(*[t.reshape(R, C) for t in (w, g, m, v)])
    return [o.reshape(shape) for o in outs]


HBM_SPEC = pl.BlockSpec(memory_space=pltpu.HBM)
OTHER_CHIPS = (4, 2, 6)
SIBLING = 1


def _me():
    return lax.axis_index("x"), lax.axis_index("y"), lax.axis_index("c")


def _peer(me, k):
    mx, my, mc = me
    return ((1 - mx) if k & 4 else mx, (1 - my) if k & 2 else my, (1 - mc) if k & 1 else mc)


def _rcopy(src, dst, ssem, rsem, to):
    return pltpu.make_async_remote_copy(src_ref=src, dst_ref=dst, send_sem=ssem, recv_sem=rsem, device_id=to,
                                        device_id_type=MESH)


def _all_gather8(name, x):
    def body(x_ref, o_ref, ssem, rsem, lsem):
        me = _me()
        mine = 4 * me[0] + 2 * me[1] + me[2]
        loc = pltpu.make_async_copy(x_ref, o_ref.at[mine], lsem)
        loc.start()
        sends = []
        for k in range(1, 8):
            cp = _rcopy(x_ref, o_ref.at[mine], ssem.at[k - 1], rsem.at[k - 1], _peer(me, k))
            cp.start()
            sends.append(cp)
        for k in range(1, 8):
            px, py, pc = _peer(me, k)
            _rcopy(x_ref, o_ref.at[4 * px + 2 * py + pc], ssem.at[k - 1], rsem.at[k - 1], (px, py, pc)).wait_recv()
        for cp in sends:
            cp.wait_send()
        loc.wait()

    return pl.pallas_call(body, name=name, out_shape=jax.ShapeDtypeStruct((8,) + x.shape, x.dtype),
                          in_specs=[HBM_SPEC], out_specs=HBM_SPEC,
                          scratch_shapes=[pltpu.SemaphoreType.DMA((7,)), pltpu.SemaphoreType.DMA((7,)),
                                          pltpu.SemaphoreType.DMA(())])(x)


PACK_L = 1024
PACK_RT = 256


def _place_shard(name, wp, chip):
    rh, ln = wp.shape[1:]

    def body(s_ref, w_ref, o_ref):
        o_ref[...] = w_ref[...]

    gs = pltpu.PrefetchScalarGridSpec(
        num_scalar_prefetch=1, grid=(2, rh // PACK_RT),
        in_specs=[pl.BlockSpec((None, PACK_RT, ln), lambda h, i, s_ref: (h, i, 0))],
        out_specs=pl.BlockSpec((None, None, PACK_RT, ln), lambda h, i, s_ref: (s_ref[0], h, i, 0)))
    return pl.pallas_call(body, name=name, grid_spec=gs, out_shape=jax.ShapeDtypeStruct((4,) + wp.shape, wp.dtype),
                          compiler_params=_cparams(("parallel", "parallel")))(chip.reshape(1).astype(jnp.int32), wp)


def _gather_weights(name, w4):
    r2 = w4.shape[2] // 2

    def body(w_ref, o_ref, ssem, rsem):
        me = _me()
        mc = me[2]
        px, py, pd, sib = _peer(me, 4), _peer(me, 2), _peer(me, 6), _peer(me, SIBLING)
        chip = lambda p: 2 * p[0] + p[1]
        mine, from_x, from_y, from_d = (o_ref.at[chip(p), mc] for p in (me, px, py, pd))
        q0, q1 = pl.ds(0, r2), pl.ds(r2, r2)
        sends = [_rcopy(mine, mine, ssem.at[0], rsem.at[0], px), _rcopy(mine, mine, ssem.at[1], rsem.at[1], py)]
        for cp in sends:
            cp.start()
        _rcopy(from_x, from_x, ssem.at[0], rsem.at[0], px).wait_recv()
        sends += [_rcopy(from_x.at[q1], from_x.at[q1], ssem.at[2], rsem.at[2], py),
                  _rcopy(from_x, from_x, ssem.at[4], rsem.at[4], sib)]
        sends[-2].start()
        sends[-1].start()
        _rcopy(from_y, from_y, ssem.at[1], rsem.at[1], py).wait_recv()
        sends += [_rcopy(from_y.at[q0], from_y.at[q0], ssem.at[3], rsem.at[3], px),
                  _rcopy(from_y, from_y, ssem.at[5], rsem.at[5], sib)]
        sends[-2].start()
        sends[-1].start()
        _rcopy(from_d.at[q0], from_d.at[q0], ssem.at[3], rsem.at[3], px).wait_recv()
        _rcopy(from_d.at[q1], from_d.at[q1], ssem.at[2], rsem.at[2], py).wait_recv()
        sends.append(_rcopy(from_d, from_d, ssem.at[6], rsem.at[6], sib))
        sends[-1].start()
        for j, p in enumerate((px, py, pd)):
            land = o_ref.at[chip(p), 1 - mc]
            _rcopy(land, land, ssem.at[4 + j], rsem.at[4 + j], sib).wait_recv()
        for cp in sends:
            cp.wait_send()

    return pl.pallas_call(body, name=name, out_shape=jax.ShapeDtypeStruct(w4.shape, w4.dtype),
                          in_specs=[HBM_SPEC], out_specs=HBM_SPEC, input_output_aliases={0: 0},
                          scratch_shapes=[pltpu.SemaphoreType.DMA((7,)), pltpu.SemaphoreType.DMA((7,))])(w4)


def _exchange_half(name, g):
    def body(g_ref, p_ref, ssem, rsem):
        me = _me()
        cps = []
        for s in range(4):
            cp = _rcopy(g_ref.at[s, 1 - me[2]], p_ref.at[s], ssem.at[s], rsem.at[s], _peer(me, SIBLING))
            cp.start()
            cps.append(cp)
        for cp in cps:
            cp.wait()

    return pl.pallas_call(body, name=name, out_shape=jax.ShapeDtypeStruct((4,) + g.shape[2:], g.dtype),
                          in_specs=[HBM_SPEC], out_specs=HBM_SPEC,
                          scratch_shapes=[pltpu.SemaphoreType.DMA((4,)), pltpu.SemaphoreType.DMA((4,))])(g)


def _scatter_chips(name, q):
    r2 = q.shape[1] // 2

    def body(q_ref, t_ref, relay, ssem, rsem):
        me = _me()
        px, py, pd = _peer(me, 4), _peer(me, 2), _peer(me, 6)
        chip = lambda p: 2 * p[0] + p[1]
        q0, q1 = pl.ds(0, r2), pl.ds(r2, r2)
        sends = [_rcopy(q_ref.at[chip(px)], t_ref.at[0], ssem.at[0], rsem.at[0], px),
                 _rcopy(q_ref.at[chip(py)], t_ref.at[1], ssem.at[1], rsem.at[1], py),
                 _rcopy(q_ref.at[chip(pd), q0], relay.at[0], ssem.at[2], rsem.at[2], py),
                 _rcopy(q_ref.at[chip(pd), q1], relay.at[1], ssem.at[3], rsem.at[3], px)]
        for cp in sends:
            cp.start()
        _rcopy(relay.at[0], relay.at[0], ssem.at[2], rsem.at[2], py).wait_recv()
        sends.append(_rcopy(relay.at[0], t_ref.at[2, q0], ssem.at[4], rsem.at[4], px))
        sends[-1].start()
        _rcopy(relay.at[1], relay.at[1], ssem.at[3], rsem.at[3], px).wait_recv()
        sends.append(_rcopy(relay.at[1], t_ref.at[2, q1], ssem.at[5], rsem.at[5], py))
        sends[-1].start()
        _rcopy(t_ref.at[0], t_ref.at[0], ssem.at[0], rsem.at[0], px).wait_recv()
        _rcopy(t_ref.at[1], t_ref.at[1], ssem.at[1], rsem.at[1], py).wait_recv()
        _rcopy(t_ref.at[2, q0], t_ref.at[2, q0], ssem.at[4], rsem.at[4], px).wait_recv()
        _rcopy(t_ref.at[2, q1], t_ref.at[2, q1], ssem.at[5], rsem.at[5], py).wait_recv()
        for cp in sends:
            cp.wait_send()

    return pl.pallas_call(body, name=name,
                          out_shape=[jax.ShapeDtypeStruct((3,) + q.shape[1:], q.dtype),
                                     jax.ShapeDtypeStruct((2, r2) + q.shape[2:], q.dtype)],
                          in_specs=[HBM_SPEC], out_specs=[HBM_SPEC, HBM_SPEC],
                          scratch_shapes=[pltpu.SemaphoreType.DMA((6,)), pltpu.SemaphoreType.DMA((6,))])(q)[0]


def _exchange_full(name, r2):
    def body(r_ref, o_ref, ssem, rsem):
        me = _me()
        mc = me[2]
        cp = _rcopy(o_ref.at[mc], o_ref.at[mc], ssem, rsem, _peer(me, SIBLING))
        cp.start()
        _rcopy(o_ref.at[1 - mc], o_ref.at[1 - mc], ssem, rsem, _peer(me, SIBLING)).wait_recv()
        cp.wait_send()

    return pl.pallas_call(body, name=name, out_shape=jax.ShapeDtypeStruct(r2.shape, r2.dtype),
                          in_specs=[HBM_SPEC], out_specs=HBM_SPEC, input_output_aliases={0: 0},
                          scratch_shapes=[pltpu.SemaphoreType.DMA(()), pltpu.SemaphoreType.DMA(())])(r2)


def _add_half(name, g, p, c):
    rh, ln = g.shape[2:]

    def body(c_ref, g_ref, p_ref, o_ref):
        o_ref[0] = (g_ref[0, 0].astype(F32) + p_ref[0].astype(F32)).astype(o_ref.dtype)

    gs = pltpu.PrefetchScalarGridSpec(
        num_scalar_prefetch=1, grid=(4, rh // PACK_RT),
        in_specs=[pl.BlockSpec((1, 1, PACK_RT, ln), lambda s, i, c_ref: (s, c_ref[0], i, 0)),
                  pl.BlockSpec((1, PACK_RT, ln), lambda s, i, c_ref: (s, i, 0))],
        out_specs=pl.BlockSpec((1, PACK_RT, ln), lambda s, i, c_ref: (s, i, 0)))
    return pl.pallas_call(body, name=name, grid_spec=gs, out_shape=jax.ShapeDtypeStruct((4, rh, ln), BF16),
                          compiler_params=_cparams(("parallel", "parallel")))(c.reshape(1).astype(jnp.int32), g, p)


def _add_chips(name, q, t, chip, c):
    rh, ln = q.shape[1:]

    def body(s_ref, c_ref, q_ref, t_ref, o_ref):
        o_ref[...] = ((q_ref[0].astype(F32) + t_ref[0].astype(F32)) + t_ref[1].astype(F32)) + t_ref[2].astype(F32)

    gs = pltpu.PrefetchScalarGridSpec(
        num_scalar_prefetch=2, grid=(rh // PACK_RT,),
        in_specs=[pl.BlockSpec((1, PACK_RT, ln), lambda i, s_ref, c_ref: (s_ref[0], i, 0)),
                  pl.BlockSpec((3, PACK_RT, ln), lambda i, s_ref, c_ref: (0, i, 0))],
        out_specs=pl.BlockSpec((None, PACK_RT, ln), lambda i, s_ref, c_ref: (c_ref[0], i, 0)))
    return pl.pallas_call(body, name=name, grid_spec=gs, out_shape=jax.ShapeDtypeStruct((2, rh, ln), F32),
                          compiler_params=_cparams(("parallel",)))(chip.reshape(1).astype(jnp.int32),
                                                                    c.reshape(1).astype(jnp.int32), q, t)


def _sum8(name, a):
    def body(a_ref, o_ref):
        acc = a_ref[0]
        for d in range(1, 8):
            acc = acc + a_ref[d]
        o_ref[...] = acc
    return pl.pallas_call(body, name=name, out_shape=jax.ShapeDtypeStruct(a.shape[1:], F32))(a)


def _silu_rows(name, a):
    def body(a_ref, o_ref):
        o_ref[...] = _silu(a_ref[...])
    return pl.pallas_call(body, name=name, out_shape=jax.ShapeDtypeStruct(a.shape, F32))(a)


REST = (("gdn_w_out", 1), ("mla_w_dkv", 0), ("mla_w_ukv", 1), ("mla_w_dq", 1), ("mla_w_uq", 2), ("mla_w_out", 1))


def _packed_rows(n):
    per_half = -(-n // (2 * PACK_L))
    return -(-per_half // PACK_RT) * PACK_RT


def _pack_flat(flat):
    n = flat.shape[-1]
    rh = _packed_rows(n)
    pad = [(0, 0)] * (flat.ndim - 1) + [(0, 2 * rh * PACK_L - n)]
    return jnp.pad(flat, pad).reshape(flat.shape[:-1] + (2, rh, PACK_L))


def _shards_first(full, axis):
    sh = full.shape
    t = full.reshape(sh[:axis] + (4, sh[axis] // 4) + sh[axis + 1:])
    return jnp.moveaxis(t, axis, 0)


def _shards_merge(stacked, axis):
    t = jnp.moveaxis(stacked, 0, axis)
    sh = t.shape
    return t.reshape(sh[:axis] + (4 * sh[axis + 1],) + sh[axis + 2:])


def _pack_small(parts):
    flat = jnp.concatenate([p.reshape(-1).astype(F32) for p in parts])
    n = flat.shape[0]
    rows = -(-n // (SUB * LANE)) * SUB
    return jnp.pad(flat, (0, rows * LANE - n)).reshape(rows, LANE)


def _unpack_small(buf, shapes):
    lead = buf.shape[:-2]
    flat = buf.reshape(lead + (-1,))
    out, off = [], 0
    for sh in shapes:
        n = 1
        for d in sh:
            n *= d
        out.append(flat[..., off:off + n].reshape(lead + tuple(sh)))
        off += n
    return out


WEIGHTS = ('ada_w', 'ada_b', 'norm_g', 'ffn_w_in', 'ffn_w_out', 'gdn_w_in', 'gdn_conv_w', 'gdn_a_log', 'gdn_dt_bias',
           'gdn_norm_g', 'gdn_w_out', 'kv_ada_w', 'kv_ada_b', 'kv_norm_g', 'mla_w_dkv', 'mla_kv_norm_g', 'mla_w_ukv',
           'mla_k_norm_g', 'mla_w_dq', 'mla_q_lora_norm_g', 'mla_w_uq', 'mla_q_norm_g', 'mla_w_out')
ARGS = ('x', 'c', 'positions') + WEIGHTS + ('loss_target',) + tuple('m_' + n for n in WEIGHTS) + tuple('v_' + n for n in WEIGHTS)


def _split_norm(v):
    return v[None, :DH], _pad_lanes(v[None, DH:], 0)


def _join_norm(gn, gr):
    return jnp.concatenate([gn[0], gr[0, :ROPE]])


def _step(x, tgt, pos, mods, kvmod, W, P):
    tabs, pm = _rope_tables(pos)
    m3 = lambda l, i: tuple(mods[l][3 * i + j][None] for j in range(3))
    ng = lambda l, i: P["norm_g"][l, i][None]
    gdn_p, mla_p = [], []
    for l in range(2):
        gdn_p.append(dict(w_in=jnp.pad(W["gdn_w_in"][l], ((0, 0), (0, GDN_IN - W["gdn_w_in"].shape[2]))),
                          conv_w8=jnp.pad(P["gdn_conv_w"][l], ((0, 4), (0, 0))),
                          a_log128=_pad_lanes(P["gdn_a_log"][l][None], NH), dt_bias128=_pad_lanes(P["gdn_dt_bias"][l][None], NH),
                          norm_g=P["gdn_norm_g"][l][None], w_out=W["gdn_w_out"][l]))
        q_gn, q_gr = _split_norm(P["mla_q_norm_g"][l])
        mla_p.append(dict(w_dq=W["mla_w_dq"][l], ql_g=P["mla_q_lora_norm_g"][l][None],
                          w_uq=jnp.pad(W["mla_w_uq"][l].reshape(QL, NH, QKH), ((0, 0), (0, 0), (0, HP - QKH))).reshape(QL, NH * HP),
                          q_gn=q_gn, q_gr=q_gr, pm=pm, w_out=W["mla_w_out"][l]))
    k_gn, k_gr = _split_norm(P["mla_k_norm_g"])
    kv_p = dict(kv_norm_g=P["kv_norm_g"][None], w_dkv=jnp.pad(W["mla_w_dkv"], ((0, 0), (0, QL - KVL - ROPE))),
                kv_lat_g=P["mla_kv_norm_g"][None],
                w_ukv=W["mla_w_ukv"].reshape(KVL, NH, 2, DH).transpose(0, 2, 1, 3).reshape(KVL, 2 * NH * DH),
                k_gn=k_gn, k_gr=k_gr, pm=pm)
    kvm = (kvmod[0][None], kvmod[1][None])

    res = {}
    for l in range(4):
        x, res[l, 0] = _ffn_fwd(f"l{l}a", x, m3(l, 0), ng(l, 0), W["ffn_w_in"], W["ffn_w_out"], 2 * l)
        if l < 2:
            x, res[l, 1] = _gdn_layer_fwd(f"l{l}g", x, m3(l, 1), ng(l, 1), gdn_p[l])
        else:
            x, res[l, 1] = _mla_layer_fwd(f"l{l}m", x, m3(l, 1), ng(l, 1), mla_p[l - 2], kp, kvf, tabs)
        x, res[l, 2] = _ffn_fwd(f"l{l}b", x, m3(l, 2), ng(l, 2), W["ffn_w_in"], W["ffn_w_out"], 2 * l + 1)
        if l == 1:
            kp, kvf, kres = _kv_fwd(x, kvm, kv_p, tabs)
    loss, dx = _loss_head(x, tgt)

    gw = {n: [None] * W[n].shape[0] for n in ("gdn_w_in", "gdn_w_out", "mla_w_dq", "mla_w_uq", "mla_w_out")}
    g_in4 = g_out4 = None
    gp = {n: [None] * 2 for n in ("gdn_conv_w", "gdn_a_log", "gdn_dt_bias", "gdn_norm_g", "mla_q_lora_norm_g", "mla_q_norm_g")}
    gnorm = [[None] * 3 for _ in range(4)]
    dmod = [[None] * NMOD for _ in range(4)]
    dkp = dv = None
    for l in (3, 2, 1, 0):
        if l == 1:
            dx, gk = _kv_bwd(dkp, dv, dx, kres, kvm, kvf, kv_p, tabs)
        for i in (2, 1, 0):
            if i != 1:
                dx, g_in4, g_out4, gd = _ffn_bwd(f"l{l}{'ab'[i // 2]}", dx, res[l, i], m3(l, i), ng(l, i), W["ffn_w_in"],
                                                 W["ffn_w_out"], 2 * l + i // 2, g_in4, g_out4)
            elif l < 2:
                dx, gd = _gdn_layer_bwd(f"l{l}g", dx, res[l, 1], m3(l, 1), ng(l, 1), gdn_p[l])
                gw["gdn_w_in"][l] = gd["w_in"][:, :W["gdn_w_in"].shape[2]]
                gw["gdn_w_out"][l] = gd["w_out"]
                gp["gdn_conv_w"][l] = gd["conv_w8"][:4]
                gp["gdn_a_log"][l] = gd["a_log128"][0, NH:2 * NH]
                gp["gdn_dt_bias"][l] = gd["dt_bias128"][0, NH:2 * NH]
                gp["gdn_norm_g"][l] = gd["norm_g"][0]
            else:
                dx, dkp_l, dv_l, gd = _mla_layer_bwd(f"l{l}m", dx, res[l, 1], m3(l, 1), ng(l, 1), mla_p[l - 2], kp, kvf, tabs)
                dkp = dkp_l if dkp is None else dkp + dkp_l
                dv = dv_l if dv is None else dv + dv_l
                gw["mla_w_dq"][l - 2], gw["mla_w_out"][l - 2] = gd["w_dq"], gd["w_out"]
                gw["mla_w_uq"][l - 2] = gd["w_uq"].reshape(QL, NH, HP)[:, :, :QKH].reshape(QL, NH * QKH)
                gp["mla_q_lora_norm_g"][l - 2] = gd["ql_g"][0]
                gp["mla_q_norm_g"][l - 2] = _join_norm(gd["q_gn"], gd["q_gr"])
            gnorm[l][i] = gd["g"][0]
            for j in range(3):
                dmod[l][3 * i + j] = gd["mod"][j][0]
    gwf = {n: jnp.stack(v) for n, v in gw.items()}
    gwf["ffn_w_in"], gwf["ffn_w_out"] = g_in4, g_out4
    gwf["mla_w_dkv"] = gk["w_dkv"][:, :KVL + ROPE]
    gwf["mla_w_ukv"] = gk["w_ukv"].reshape(KVL, 2, NH, DH).transpose(0, 2, 1, 3).reshape(KVL, 2 * NH * DH)
    gpf = {n: jnp.stack(v) for n, v in gp.items()}
    gpf["norm_g"] = jnp.stack([jnp.stack(r) for r in gnorm])
    gpf["kv_norm_g"] = gk["kv_norm_g"][0]
    gpf["mla_kv_norm_g"] = gk["kv_lat_g"][0]
    gpf["mla_k_norm_g"] = _join_norm(gk["k_gn"], gk["k_gr"])
    dmods = jnp.stack([jnp.stack(r) for r in dmod])
    dkvmod = jnp.stack([gk["mod"][0][0], gk["mod"][1][0]])
    return loss, dx, gwf, gpf, dmods, dkvmod


SMALL = ("norm_g", "gdn_conv_w", "gdn_a_log", "gdn_dt_bias", "gdn_norm_g", "kv_norm_g", "mla_kv_norm_g", "mla_k_norm_g",
         "mla_q_lora_norm_g", "mla_q_norm_g")


def kernel(x, c, positions, ada_w, ada_b, norm_g, ffn_w_in, ffn_w_out, gdn_w_in, gdn_conv_w, gdn_a_log, gdn_dt_bias,
           gdn_norm_g, gdn_w_out, kv_ada_w, kv_ada_b, kv_norm_g, mla_w_dkv, mla_kv_norm_g, mla_w_ukv, mla_k_norm_g,
           mla_w_dq, mla_q_lora_norm_g, mla_w_uq, mla_q_norm_g, mla_w_out, loss_target, m_ada_w, m_ada_b, m_norm_g,
           m_ffn_w_in, m_ffn_w_out, m_gdn_w_in, m_gdn_conv_w, m_gdn_a_log, m_gdn_dt_bias, m_gdn_norm_g, m_gdn_w_out,
           m_kv_ada_w, m_kv_ada_b, m_kv_norm_g, m_mla_w_dkv, m_mla_kv_norm_g, m_mla_w_ukv, m_mla_k_norm_g, m_mla_w_dq,
           m_mla_q_lora_norm_g, m_mla_w_uq, m_mla_q_norm_g, m_mla_w_out, v_ada_w, v_ada_b, v_norm_g, v_ffn_w_in,
           v_ffn_w_out, v_gdn_w_in, v_gdn_conv_w, v_gdn_a_log, v_gdn_dt_bias, v_gdn_norm_g, v_gdn_w_out, v_kv_ada_w,
           v_kv_ada_b, v_kv_norm_g, v_mla_w_dkv, v_mla_kv_norm_g, v_mla_w_ukv, v_mla_k_norm_g, v_mla_w_dq,
           v_mla_q_lora_norm_g, v_mla_w_uq, v_mla_q_norm_g, v_mla_w_out):
    a = dict(locals())
    mx, my, mc = _me()
    dev = 4 * mx + 2 * my + mc
    chip = 2 * mx + my
    x, tgt, pos = a["x"][0], a["loss_target"][0], a["positions"][0]
    take = lambda arr, i, axis=0: lax.dynamic_index_in_dim(arr, i, axis, keepdims=False)

    pre = _all_gather8("ag_pre", _pack_small([a["c"], a["gdn_conv_w"], a["norm_g"]]))
    c_all, conv_sh, norm_sh = _unpack_small(pre, [(D,), a["gdn_conv_w"].shape, a["norm_g"].shape])
    P = {n: a[n] for n in SMALL}
    P["gdn_conv_w"] = jnp.concatenate([conv_sh[2 * s] for s in range(4)], axis=2)
    P["norm_g"] = jnp.concatenate([norm_sh[2 * s] for s in range(4)], axis=2)
    c_act = _silu_rows("c_act", c_all)
    nada = a["ada_w"].shape[2]
    nkv = a["kv_ada_w"].shape[1]
    modp = [_matmul(f"mod{l}", c_act, a["ada_w"], precise=True, lay="b_stack", li=l) for l in range(4)]
    kvp = _matmul("modkv", c_act, a["kv_ada_w"], precise=True)
    mp = _all_gather8("ag_mod", _pack_small(modp + [kvp]))
    modp_all, kvp_all = _unpack_small(mp, [(4, 8, nada), (8, nkv)])
    mods = jnp.concatenate([take(modp_all[2 * s], dev, 1) for s in range(4)], axis=1) + a["ada_b"]
    mods = mods.reshape(4, NMOD, D)
    kvmod = (jnp.concatenate([take(kvp_all[2 * s], dev, 0) for s in range(4)]) + a["kv_ada_b"]).reshape(2, D)

    def gather(tag, w2):
        return _gather_weights("ag_" + tag, _place_shard("own_" + tag, w2, chip))

    def reduce(tag, g4):
        q = _add_half("rsp_" + tag, g4, _exchange_half("rs1_" + tag, g4), mc)
        r2 = _add_chips("rsc_" + tag, q, _scatter_chips("rs2_" + tag, q), chip, mc)
        return _exchange_full("rs3_" + tag, r2)

    halves = lambda t: t.reshape((2, -1) + t.shape[-1:])
    W = {n: gather(t, halves(a[n].astype(BF16))).reshape((4, 8) + a[n].shape[2:])
         for n, t in (("ffn_w_in", "wi"), ("ffn_w_out", "wo"))}
    wg = gather("wg", a["gdn_w_in"].astype(BF16))
    W["gdn_w_in"] = jnp.concatenate([wg[s] for s in range(4)], axis=2)
    wall = gather("wr", _pack_flat(jnp.concatenate([a[n].reshape(-1).astype(BF16) for n, _ in REST]))).reshape(4, -1)
    off = 0
    for n, ax in REST:
        sz = a[n].size
        W[n] = _shards_merge(wall[:, off:off + sz].reshape((4,) + a[n].shape), ax)
        off += sz

    loss, dx, gw, gp, dmods, dkvmod = _step(x, tgt, pos, mods, kvmod, W, P)
    loss = lax.psum(loss, ("x", "y", "c"))

    grads = {n: reduce(t, gw[n].reshape((4, 2, -1) + a[n].shape[-1:])).reshape(a[n].shape)
             for n, t in (("ffn_w_in", "wi"), ("ffn_w_out", "wo"))}
    ng = a["gdn_w_in"].shape[2]
    grads["gdn_w_in"] = reduce("wg", jnp.stack([gw["gdn_w_in"][:, :, s * ng:(s + 1) * ng] for s in range(4)]))
    gsh = reduce("wr", _pack_flat(jnp.concatenate([_shards_first(gw[n], ax).reshape(4, -1) for n, ax in REST], axis=1)))
    gsh = gsh.reshape(-1)
    off = 0
    for n, _ in REST:
        grads[n] = gsh[off:off + a[n].size].reshape(a[n].shape)
        off += a[n].size

    small = _all_gather8("ag_small", _pack_small([dmods, dkvmod] + [gp[n] for n in SMALL]))
    shapes = [(4, NMOD * D), (2 * D,)] + [gp[n].shape for n in SMALL]
    dmod_all, dkv_all = _unpack_small(small, shapes)[:2]
    tot = _unpack_small(_sum8("sum_small", small), shapes)
    grads["ada_b"], grads["kv_ada_b"] = tot[0], tot[1]
    for n, t in zip(SMALL, tot[2:]):
        grads[n] = t
    grads["norm_g"] = lax.dynamic_slice_in_dim(grads["norm_g"], chip * a["norm_g"].shape[2], a["norm_g"].shape[2], 2)
    grads["gdn_conv_w"] = lax.dynamic_slice_in_dim(grads["gdn_conv_w"], chip * a["gdn_conv_w"].shape[2],
                                                   a["gdn_conv_w"].shape[2], 2)
    ca = jnp.pad(c_act, ((0, LANE - 8), (0, 0)))
    dm = jnp.pad(lax.dynamic_slice_in_dim(dmod_all.reshape(8, 4, NMOD * D), chip * nada, nada, 2), ((0, LANE - 8), (0, 0), (0, 0)))
    gada = None
    for l in range(4):
        gada = _matmul(f"gada{l}", ca, dm[:, l], "tn", precise=True, lay="o_stack", li=l, into=gada, nmat=4)
    grads["ada_w"] = gada
    dk = jnp.pad(lax.dynamic_slice_in_dim(dkv_all, chip * nkv, nkv, 1), ((0, LANE - 8), (0, 0)))
    grads["kv_ada_w"] = _matmul("gadakv", ca, dk, "tn", precise=True)

    upd = [_adamw("adamw_" + n, a[n], grads[n], a["m_" + n], a["v_" + n]) for n in WEIGHTS]
    return (loss, dx[None], *[grads[n] for n in WEIGHTS], *[u[0] for u in upd], *[u[1] for u in upd], *[u[2] for u in upd])
```

```python
import functools

import jax
import jax.numpy as jnp
from jax import lax
from jax.experimental import pallas as pl
from jax.experimental.pallas import tpu as pltpu

F32 = jnp.float32
BF16 = jnp.bfloat16
HI = lax.Precision.HIGHEST
MESH = pl.DeviceIdType.MESH

D = 1024
NH = 8
DH = 128
FF = 2816
NMOD = 9
CHUNK = 64
ROPE = 64
QKH = 192
HP = 256
KVL = 256
QL = 384
GDN_IN = 4224
GATE_CB = 32
EPS = 1e-6
ROPE_BASE = 10000.0
LANE = 128
SUB = 8
VMEM_LIMIT = 56 * 1024 * 1024

ADAM_LR, ADAM_B1, ADAM_B2, ADAM_EPS, ADAM_WD, ADAM_STEP = 0.001, 0.9, 0.999, 1e-08, 0.01, 10


def _tile(n, prefs=(512, 384, 256, 128)):
    for p in prefs:
        if n % p == 0:
            return p
    return n


def _cparams(sem):
    return pltpu.CompilerParams(dimension_semantics=sem, vmem_limit_bytes=VMEM_LIMIT)


class Row:
    def __init__(self, arr, width=None, cb=0, splits=None, halo=None):
        self.arr = arr
        self.width = arr.shape[1] if width is None else width
        self.cb = cb
        self.splits = splits
        self.halo = halo


def _rowwise(name, fn, rows, bcs, outs, accs, tm):
    S = rows[0].arr.shape[0]
    n = S // tm
    nr, nb, no, na = len(rows), len(bcs), len(outs), len(accs)

    def body(*refs):
        rrefs, brefs = refs[:nr], refs[nr:nr + nb]
        orefs, arefs = refs[nr + nb:nr + nb + no], refs[nr + nb + no:]
        pieces = []
        for r, ref in zip(rows, rrefs):
            if r.splits is None:
                pieces.append(ref[...])
            else:
                off = 0
                for w in r.splits:
                    pieces.append(ref[:, off:off + w])
                    off += w
        out_pieces, acc_vals = fn(pieces, [b[...] for b in brefs])
        k = 0
        for (widths, dt), oref in zip(outs, orefs):
            off = 0
            for w in widths:
                oref[:, off:off + w] = out_pieces[k].astype(dt)
                k += 1
                off += w
        if na:
            @pl.when(pl.program_id(0) == 0)
            def _():
                for a in arefs:
                    a[...] = jnp.zeros(a.shape, F32)
            for a, v in zip(arefs, acc_vals):
                a[...] += v

    in_specs = []
    for r in rows:
        if r.halo is None:
            in_specs.append(pl.BlockSpec((tm, r.width), lambda i, cb=r.cb: (i, cb)))
        elif r.halo == "prev":
            in_specs.append(pl.BlockSpec((SUB, r.width), lambda i, cb=r.cb: (jnp.maximum(i * (tm // SUB) - 1, 0), cb)))
        else:
            in_specs.append(pl.BlockSpec((SUB, r.width), lambda i, cb=r.cb: (jnp.minimum((i + 1) * (tm // SUB), S // SUB - 1), cb)))
    in_specs += [pl.BlockSpec(b.shape, lambda i, nd=b.ndim: (0,) * nd) for b in bcs]
    out_specs = [pl.BlockSpec((tm, sum(w)), lambda i: (i, 0)) for w, _ in outs]
    out_specs += [pl.BlockSpec(s, lambda i: (0, 0)) for s in accs]
    out_shape = [jax.ShapeDtypeStruct((S, sum(w)), dt) for w, dt in outs]
    out_shape += [jax.ShapeDtypeStruct(s, F32) for s in accs]
    res = pl.pallas_call(body, name=name, grid=(n,), in_specs=in_specs, out_specs=out_specs, out_shape=out_shape,
                         compiler_params=_cparams(("arbitrary",)))(*[r.arr for r in rows], *bcs)
    return res


def _rw_fwd(name, f, rows, bcs, outs, tm):
    def fn(pieces, bvals):
        return list(f(*[p.astype(F32) for p in pieces], *[b.astype(F32) for b in bvals])), []
    return _rowwise(name, fn, rows, bcs, outs, [], tm)


def _npieces(rows):
    return sum(1 if r.splits is None else len(r.splits) for r in rows)


def _rw_bwd(name, f, rows, bcs, cts, drow, dbc, outs, tm, add=None):
    np_, nct = _npieces(rows), _npieces(cts)

    def fn(pieces, bvals):
        allv = [p.astype(F32) for p in pieces[:np_]] + [b.astype(F32) for b in bvals]
        ct = [p.astype(F32) for p in pieces[np_:np_ + nct]]
        didx = [i for i, m in enumerate(list(drow) + list(dbc)) if m]

        def g(*dv):
            full = list(allv)
            for i, v in zip(didx, dv):
                full[i] = v
            return tuple(f(*full))

        _, vjp = jax.vjp(g, *[allv[i] for i in didx])
        grads = vjp(tuple(ct))
        nrd = sum(bool(m) for m in drow)
        rg, bg = list(grads[:nrd]), list(grads[nrd:])
        if add is not None:
            rg[0] = rg[0] + pieces[np_ + nct].astype(F32)
        return rg, bg

    accs = [b.shape for b, m in zip(bcs, dbc) if m]
    return _rowwise(name, fn, list(rows) + list(cts) + ([add] if add is not None else []), bcs, outs, accs, tm)


def _sigmoid(x):
    return 1.0 / (1.0 + jnp.exp(-x))


def _silu(x):
    return x * _sigmoid(x)


def _softplus(x):
    return jnp.maximum(x, 0.0) + jnp.log(1.0 + jnp.exp(-jnp.abs(x)))


def f_mod(x, g, shift, scale):
    y = x * lax.rsqrt(jnp.mean(x * x, axis=-1, keepdims=True) + EPS)
    return (y * g * (1.0 + scale) + shift,)


def f_rms(x, g):
    return (x * lax.rsqrt(jnp.mean(x * x, axis=-1, keepdims=True) + EPS) * g,)


def f_act(gate, up):
    return (_silu(gate) * up,)


def make_f_res(coef):
    def f_res(y, gate):
        return (coef * gate * y,)
    return f_res


def f_gdnpre(*p):
    out = []
    for i, t in enumerate(p):
        t = _silu(t)
        if i < 2 * NH:
            t = t * lax.rsqrt(jnp.sum(t * t, axis=-1, keepdims=True) + EPS)
        out.append(t)
    return tuple(out)


def f_gates(gates, a_log, dt_bias):
    return _sigmoid(gates), -jnp.exp(a_log) * _softplus(gates + dt_bias)


def f_gdnpost(*a):
    o, z, g = a[:NH], a[NH:2 * NH], a[2 * NH]
    out = []
    for oh, zh in zip(o, z):
        y = oh * lax.rsqrt(jnp.mean(oh * oh, axis=-1, keepdims=True) + EPS) * g
        out.append(y * _silu(zh))
    return tuple(out)


def make_f_qk(shared_rope):
    def f(*a):
        if shared_rope:
            ns, rs = a[:NH], [a[NH]] * NH
            cosp, sins, gn, gr, pm = a[NH + 1:NH + 6]
        else:
            ns, rs = a[0:2 * NH:2], a[1:2 * NH:2]
            cosp, sins, gn, gr, pm = a[2 * NH:2 * NH + 5]
        out = []
        for n, r in zip(ns, rs):
            ss = jnp.sum(n * n, axis=-1, keepdims=True) + jnp.sum(r * r, axis=-1, keepdims=True)
            rstd = lax.rsqrt(ss * (1.0 / QKH) + EPS)
            yn = n * rstd * gn
            yr = r * rstd * gr
            sw = jnp.dot(yr, pm, precision=HI, preferred_element_type=F32)
            out += [yn, yr * cosp + sw * sins]
        return tuple(out)
    return f


def _matmul(name, a, b, mode="nn", out_dtype=F32, precise=False, lay=None, li=0, into=None, nmat=1):
    if lay == "b_cols":
        per = b.shape[3]
        rb, cb = b.shape[2], 4 * per
    elif lay == "b_rows":
        per = b.shape[2]
        rb, cb = 4 * per, b.shape[3]
    elif lay == "b_stack":
        rb, cb = b.shape[1:]
    else:
        rb, cb = b.shape
    if mode == "nn":
        (M, K), N = a.shape, cb
    elif mode == "nt":
        (M, K), N = a.shape, rb
    else:
        (K, M), N = a.shape, cb
    tm = _tile(M, (1024, 512, 256, 128))
    tn = _tile(N, (1408, 1024, 512, 384, 256, 128))
    tk = _tile(K, (1408, 1024, 512, 384, 256, 128))
    if lay == "b_cols":
        tn, tk = (per, tk) if mode == "nn" else (tn, per)
    elif lay == "b_rows":
        tm, tn, tk = (tm, 512, K) if mode == "nn" else (min(tm, 512), N, tk)
    elif lay == "o_cols":
        per = N // 4
        tn = per
    elif lay == "o_rows":
        per = M // 4
        tm, tn = M, 512
    nk = K // tk
    dims = {"nn": (((1,), (0,)), ((), ())), "nt": (((1,), (1,)), ((), ())), "tn": (((0,), (0,)), ((), ()))}[mode]

    def body(a_ref, b_ref, *rest):
        o_ref, acc_ref = rest[-2:]
        k = pl.program_id(2)

        @pl.when(k == 0)
        def _():
            acc_ref[...] = jnp.zeros(acc_ref.shape, F32)

        bv = b_ref[...]
        if lay == "b_rows":
            bv = bv.reshape(4 * per, bv.shape[2])
        if precise:
            acc_ref[...] += lax.dot_general(a_ref[...].astype(F32), bv.astype(F32), dims, precision=HI,
                                            preferred_element_type=F32)
        else:
            acc_ref[...] += lax.dot_general(a_ref[...].astype(BF16), bv.astype(BF16), dims, preferred_element_type=F32)

        @pl.when(k == nk - 1)
        def _():
            if lay == "o_rows":
                for s in range(4):
                    o_ref[s] = acc_ref[s * per:(s + 1) * per, :].astype(o_ref.dtype)
            else:
                o_ref[...] = acc_ref[...].astype(o_ref.dtype)

    a_spec = pl.BlockSpec((tk, tm), lambda i, j, k: (k, i)) if mode == "tn" else pl.BlockSpec((tm, tk), lambda i, j, k: (i, k))
    if lay == "b_cols":
        b_spec = (pl.BlockSpec((None, None, tk, per), lambda i, j, k: (j, li, k, 0)) if mode == "nn" else
                  pl.BlockSpec((None, None, tn, per), lambda i, j, k: (k, li, j, 0)))
    elif lay == "b_rows":
        b_spec = (pl.BlockSpec((4, None, per, tn), lambda i, j, k: (0, li, 0, j)) if mode == "nn" else
                  pl.BlockSpec((4, None, per, tk), lambda i, j, k: (0, li, 0, k)))
    elif lay == "b_stack":
        b_spec = pl.BlockSpec((None, tk, tn), lambda i, j, k: (li, k, j))
    elif mode == "nt":
        b_spec = pl.BlockSpec((tn, tk), lambda i, j, k: (j, k))
    else:
        b_spec = pl.BlockSpec((tk, tn), lambda i, j, k: (k, j))
    if lay == "o_stack":
        o_spec = pl.BlockSpec((None, tm, tn), lambda i, j, k: (li, i, j))
        o_shape = jax.ShapeDtypeStruct((nmat, M, N), out_dtype)
    elif lay == "o_cols":
        o_spec = pl.BlockSpec((None, None, tm, per), lambda i, j, k: (j, li, i, 0))
        o_shape = jax.ShapeDtypeStruct((4, nmat, M, per), out_dtype)
    elif lay == "o_rows":
        o_spec = pl.BlockSpec((4, None, per, tn), lambda i, j, k: (0, li, 0, j))
        o_shape = jax.ShapeDtypeStruct((4, nmat, per, N), out_dtype)
    else:
        o_spec = pl.BlockSpec((tm, tn), lambda i, j, k: (i, j))
        o_shape = jax.ShapeDtypeStruct((M, N), out_dtype)
    in_specs, args, alias = [a_spec, b_spec], [a, b], {}
    if into is not None:
        in_specs.append(pl.BlockSpec(memory_space=pl.ANY))
        args.append(into)
        alias = {2: 0}
    return pl.pallas_call(body, name=name, grid=(M // tm, N // tn, nk), in_specs=in_specs, out_specs=o_spec,
                          out_shape=o_shape, scratch_shapes=[pltpu.VMEM((tm, tn), F32)], input_output_aliases=alias,
                          compiler_params=_cparams(("parallel", "parallel", "arbitrary")))(*args)


def _shift_down(t, p, d):
    if d == 0:
        return t
    tr = pltpu.roll(t, d, 0)
    pr = pltpu.roll(p, d, 0)
    r8 = lax.broadcasted_iota(jnp.int32, p.shape, 0)
    first = jnp.where(r8 < d, pr, tr[:SUB])
    return jnp.concatenate([first, tr[SUB:]], axis=0)


def _shift_up(t, nx, d):
    if d == 0:
        return t
    tm = t.shape[0]
    tr = pltpu.roll(t, tm - d, 0)
    nr = pltpu.roll(nx, SUB - d, 0)
    r8 = lax.broadcasted_iota(jnp.int32, nx.shape, 0)
    last = jnp.where(r8 >= SUB - d, nr, tr[tm - SUB:])
    return jnp.concatenate([tr[:tm - SUB], last], axis=0)


def _conv_fwd(name, proj, w8, C, tm):
    def fn(pieces, bvals):
        t, p = pieces[0].astype(F32), pieces[1].astype(F32)
        w = bvals[0]
        p = jnp.where(pl.program_id(0) == 0, 0.0, p)
        out = w[3:4] * t
        for d in (1, 2, 3):
            out = out + w[3 - d:4 - d] * _shift_down(t, p, d)
        return [out], []
    return _rowwise(name, fn, [Row(proj, C), Row(proj, C, halo="prev")], [w8], [((C,), F32)], [], tm)[0]


def _conv_bwd(name, proj, dout, w8, C, tm, out_dtype):
    n = proj.shape[0] // tm

    def fn(pieces, bvals):
        t, p, g, gn = [v.astype(F32) for v in pieces]
        w = bvals[0]
        i = pl.program_id(0)
        p = jnp.where(i == 0, 0.0, p)
        gn = jnp.where(i == n - 1, 0.0, gn)
        dx = w[3:4] * g
        dws = [jnp.sum(g * t, axis=0, keepdims=True)]
        for d in (1, 2, 3):
            dx = dx + w[3 - d:4 - d] * _shift_up(g, gn, d)
            dws.append(jnp.sum(g * _shift_down(t, p, d), axis=0, keepdims=True))
        dw = jnp.concatenate([dws[3], dws[2], dws[1], dws[0], jnp.zeros((4, g.shape[1]), F32)], axis=0)
        return [dx], [dw]
    return _rowwise(name, fn, [Row(proj, C), Row(proj, C, halo="prev"), Row(dout), Row(dout, halo="next")], [w8],
                    [((C,), out_dtype)], [(SUB, C)], tm)


def _bdot(a, b, ca, cb):
    return lax.dot_general(a.astype(BF16), b.astype(BF16), (((ca,), (cb,)), ((0,), (0,))), preferred_element_type=F32)


def _bdot3(a, b, ca, cb):
    dims = (((ca,), (cb,)), ((0,), (0,)))
    ah, bh = a.astype(BF16), b.astype(BF16)
    al, bl = (a - ah.astype(F32)).astype(BF16), (b - bh.astype(F32)).astype(BF16)
    d = lambda x, y: lax.dot_general(x, y, dims, preferred_element_type=F32)
    return d(ah, bh) + (d(ah, bl) + d(al, bh))


@jax.custom_vjp
def _bmm3(a, b):
    return _bdot3(a, b, 2, 1)


_bmm3.defvjp(lambda a, b: (_bdot3(a, b, 2, 1), (a, b)),
             lambda res, g: (_bdot3(g, res[1], 2, 2), _bdot3(res[0], g, 1, 1)))


def _neumann(nl):
    C = nl.shape[1]
    eye = (lax.broadcasted_iota(jnp.int32, (1, C, C), 1) == lax.broadcasted_iota(jnp.int32, (1, C, C), 2)).astype(F32)
    T = eye + nl
    pw = nl
    for _ in range(C.bit_length() - 2):
        pw = _bdot3(pw, pw, 2, 1)
        T = T + _bdot3(T, pw, 2, 1)
    return T


_unit_lower_inv = jax.custom_vjp(_neumann)


def _unit_lower_inv_fwd(nl):
    T = _neumann(nl)
    return T, T


def _unit_lower_inv_bwd(T, g):
    return (_bdot3(_bdot3(T, g, 1, 1), T, 2, 2),)


_unit_lower_inv.defvjp(_unit_lower_inv_fwd, _unit_lower_inv_bwd)


@jax.custom_vjp
def _known_inv(nl, T):
    return T


_known_inv.defvjp(lambda nl, T: (T, T), lambda T, g: (_unit_lower_inv_bwd(T, g)[0], jnp.zeros_like(T)))


def _gdn_chunk(q, k, v, gcol, grow, bcol, S, T_saved=None):
    C = CHUNK
    ii = lax.broadcasted_iota(jnp.int32, (1, C, C), 1)
    jj = lax.broadcasted_iota(jnp.int32, (1, C, C), 2)
    incl, strict = ii >= jj, ii > jj
    gc_col = jnp.sum(jnp.where(incl, 1.0, 0.0) * grow, axis=2, keepdims=True)
    gc_row = jnp.sum(jnp.where(jj >= ii, 1.0, 0.0) * gcol, axis=1, keepdims=True)
    decay = jnp.where(incl, jnp.exp(jnp.where(incl, gc_col - gc_row, 0.0)), 0.0)
    qs = q * (DH ** -0.5)
    kb = k * bcol
    nl = -jnp.where(strict, _bdot(kb, k, 2, 2) * decay, 0.0)
    T = _unit_lower_inv(nl) if T_saved is None else _known_inv(nl, T_saved)
    egc = jnp.exp(gc_col)
    u = _bmm3(T, v * bcol)
    w = _bmm3(T, kb * egc)
    att = jnp.where(incl, _bdot(qs, k, 2, 2) * decay, 0.0)
    v_new = u - _bdot(w, S, 2, 1)
    o = _bdot(qs * egc, S, 2, 1) + _bdot(att, v_new, 2, 1)
    g_last = jnp.sum(grow, axis=2, keepdims=True)
    k_dec = k * jnp.exp(g_last - gc_col)
    S_out = S * jnp.exp(g_last) + _bdot(k_dec, v_new, 1, 1)
    return o, S_out, T


GDN_STEP = 2


def _chunk_args(refs, c):
    q_ref, k_ref, v_ref, gc_ref, gr_ref, b_ref = refs
    rows = slice(c * CHUNK, (c + 1) * CHUNK)
    heads = lambda ref, w: jnp.stack([ref[rows, h * w:(h + 1) * w] for h in range(NH)])
    grow = jnp.stack([gr_ref[c, h:h + 1, :] for h in range(NH)])
    return heads(q_ref, DH), heads(k_ref, DH), heads(v_ref, DH), heads(gc_ref, 1), grow, heads(b_ref, 1)


def _gdn_specs(NP, rev):
    ix = (lambda i: NP - 1 - i) if rev else (lambda i: i)
    wide = pl.BlockSpec((GDN_STEP * CHUNK, D), lambda i: (ix(i), 0))
    col = pl.BlockSpec((GDN_STEP * CHUNK, NH), lambda i: (ix(i), 0))
    row = pl.BlockSpec((GDN_STEP, NH, CHUNK), lambda i: (ix(i), 0, 0))
    st = pl.BlockSpec((1, NH, DH, DH), lambda i: (ix(i), 0, 0, 0))
    tinv = pl.BlockSpec((GDN_STEP, NH, CHUNK, CHUNK), lambda i: (ix(i), 0, 0, 0))
    return wide, col, row, st, tinv


def _gdn_fwd(name, q, k, v, gcol, grow, bcol):
    S = q.shape[0]
    NC = S // CHUNK
    NP = NC // GDN_STEP

    def body(q_ref, k_ref, v_ref, gc_ref, gr_ref, b_ref, o_ref, ss_ref, t_ref, st):
        @pl.when(pl.program_id(0) == 0)
        def _():
            st[...] = jnp.zeros(st.shape, F32)
        state = st[...]
        ss_ref[0] = state
        for c in range(GDN_STEP):
            o, state, tinv = _gdn_chunk(*_chunk_args((q_ref, k_ref, v_ref, gc_ref, gr_ref, b_ref), c), state)
            for h in range(NH):
                o_ref[c * CHUNK:(c + 1) * CHUNK, h * DH:(h + 1) * DH] = o[h]
            t_ref[c] = tinv
        st[...] = state

    wide, col, row, stsp, tsp = _gdn_specs(NP, False)
    return pl.pallas_call(body, name=name, grid=(NP,), in_specs=[wide, wide, wide, col, row, col],
                          out_specs=[wide, stsp, tsp],
                          out_shape=[jax.ShapeDtypeStruct((S, D), F32), jax.ShapeDtypeStruct((NP, NH, DH, DH), F32),
                                     jax.ShapeDtypeStruct((NC, NH, CHUNK, CHUNK), F32)],
                          scratch_shapes=[pltpu.VMEM((NH, DH, DH), F32)],
                          compiler_params=_cparams(("arbitrary",)))(q, k, v, gcol, grow, bcol)


def _gdn_bwd(name, q, k, v, gcol, grow, bcol, ssave, tsave, do):
    S = q.shape[0]
    NC = S // CHUNK
    NP = NC // GDN_STEP

    def body(q_ref, k_ref, v_ref, gc_ref, gr_ref, b_ref, ss_ref, t_ref, do_ref, dq_ref, dk_ref, dv_ref, dgc_ref, dgr_ref,
             db_ref, dst):
        @pl.when(pl.program_id(0) == 0)
        def _():
            dst[...] = jnp.zeros(dst.shape, F32)
        refs = (q_ref, k_ref, v_ref, gc_ref, gr_ref, b_ref)
        tinv = [t_ref[c] for c in range(GDN_STEP)]

        def chain(state, *flat):
            outs = []
            for c in range(GDN_STEP):
                o, state, _ = _gdn_chunk(*flat[6 * c:6 * c + 6], state, T_saved=tinv[c])
                outs.append(o)
            return tuple(outs) + (state,)

        prim = [a for c in range(GDN_STEP) for a in _chunk_args(refs, c)]
        _, vjp = jax.vjp(chain, ss_ref[0], *prim)
        dos = tuple(jnp.stack([do_ref[c * CHUNK:(c + 1) * CHUNK, h * DH:(h + 1) * DH] for h in range(NH)])
                    for c in range(GDN_STEP))
        grads = vjp(dos + (dst[...],))
        dst[...] = grads[0]
        for c in range(GDN_STEP):
            dq, dk, dv, dgc, dgr, db = grads[1 + 6 * c:7 + 6 * c]
            rows = slice(c * CHUNK, (c + 1) * CHUNK)
            for h in range(NH):
                hs = slice(h * DH, (h + 1) * DH)
                dq_ref[rows, hs] = dq[h]
                dk_ref[rows, hs] = dk[h]
                dv_ref[rows, hs] = dv[h]
                dgc_ref[rows, h:h + 1] = dgc[h]
                dgr_ref[c, h:h + 1, :] = dgr[h]
                db_ref[rows, h:h + 1] = db[h]

    wide, col, row, stsp, tsp = _gdn_specs(NP, True)
    return pl.pallas_call(body, name=name, grid=(NP,), in_specs=[wide, wide, wide, col, row, col, stsp, tsp, wide],
                          out_specs=[wide, wide, wide, col, row, col],
                          out_shape=[jax.ShapeDtypeStruct((S, D), F32)] * 3 + [jax.ShapeDtypeStruct((S, NH), F32),
                                                                                 jax.ShapeDtypeStruct((NC, NH, CHUNK), F32),
                                                                                 jax.ShapeDtypeStruct((S, NH), F32)],
                          scratch_shapes=[pltpu.VMEM((NH, DH, DH), F32)],
                          compiler_params=_cparams(("arbitrary",)))(q, k, v, gcol, grow, bcol, ssave, tsave, do)


TQ = 512
SM_SCALE = QKH ** -0.5
NEG = -1e30


def _diag_mask(transposed):
    r = lax.broadcasted_iota(jnp.int32, (TQ, TQ), 0) // CHUNK
    c = lax.broadcasted_iota(jnp.int32, (TQ, TQ), 1) // CHUNK
    return (r <= c) if transposed else (c <= r)


def _dot_nt(a, b):
    return lax.dot_general(a, b, (((1,), (1,)), ((), ())), preferred_element_type=F32)


def _flash_fwd(name, qp, kp, kv):
    S = qp.shape[0]
    nq = S // (2 * TQ)

    def body(q_ref, k_ref, v_ref, o_ref, lse_ref):
        qi = pl.program_id(1)
        qs = (q_ref[:TQ, :], q_ref[TQ:, :])

        def step(q, j, carry, masked):
            m, l, acc = carry
            rows = pl.ds(pl.multiple_of(j * TQ, TQ), TQ)
            s = _dot_nt(q, k_ref[rows, :]) * SM_SCALE
            if masked:
                s = jnp.where(_diag_mask(False), s, NEG)
            m_new = jnp.maximum(m, jnp.max(s, axis=-1, keepdims=True))
            p = jnp.exp(s - m_new)
            alpha = jnp.exp(m - m_new)
            l = alpha * l + jnp.sum(p, axis=-1, keepdims=True)
            acc = alpha * acc + jnp.dot(p.astype(BF16), v_ref[rows, :].astype(BF16), preferred_element_type=F32)
            return m_new, l, acc

        init = (jnp.full((TQ, 1), NEG, F32), jnp.zeros((TQ, 1), F32), jnp.zeros((TQ, DH), F32))
        ca, cb = lax.fori_loop(0, 2 * qi, lambda j, c: (step(qs[0], j, c[0], False), step(qs[1], j, c[1], False)),
                               (init, init))
        ca = step(qs[0], 2 * qi, ca, True)
        cb = step(qs[1], 2 * qi + 1, step(qs[1], 2 * qi, cb, False), True)
        for u, (m, l, acc) in enumerate((ca, cb)):
            o_ref[u * TQ:(u + 1) * TQ, :] = acc / l
            lse_ref[0, u * TQ:(u + 1) * TQ, :] = m + jnp.log(l)

    return pl.pallas_call(
        body, name=name, grid=(NH, nq),
        in_specs=[pl.BlockSpec((2 * TQ, HP), lambda h, i: (i, h)), pl.BlockSpec((S, HP), lambda h, i: (0, h)),
                  pl.BlockSpec((S, DH), lambda h, i: (0, NH + h))],
        out_specs=[pl.BlockSpec((2 * TQ, DH), lambda h, i: (i, h)), pl.BlockSpec((1, 2 * TQ, 1), lambda h, i: (h, i, 0))],
        out_shape=[jax.ShapeDtypeStruct((S, NH * DH), F32), jax.ShapeDtypeStruct((NH, S, 1), F32)],
        compiler_params=_cparams(("parallel", "arbitrary")))(qp, kp, kv)


def _flash_bwd_dq(name, qp, kp, kv, o, do, lse):
    S = qp.shape[0]
    nq = S // (2 * TQ)

    def body(q_ref, k_ref, v_ref, o_ref, do_ref, lse_ref, dq_ref, dl_ref):
        qi = pl.program_id(1)
        subs = []
        for u in range(2):
            sl = slice(u * TQ, (u + 1) * TQ)
            do = do_ref[sl, :]
            delta = jnp.sum(o_ref[sl, :] * do, axis=-1, keepdims=True)
            dl_ref[0, sl, :] = delta
            subs.append((q_ref[sl, :], do.astype(BF16), lse_ref[0, sl, :], delta))

        def step(sub, j, dq, masked):
            q, dob, lse, delta = sub
            rows = pl.ds(pl.multiple_of(j * TQ, TQ), TQ)
            k = k_ref[rows, :]
            s = _dot_nt(q, k) * SM_SCALE
            if masked:
                s = jnp.where(_diag_mask(False), s, NEG)
            p = jnp.exp(s - lse)
            dp = _dot_nt(dob, v_ref[rows, :].astype(BF16))
            ds = p * (dp - delta) * SM_SCALE
            return dq + jnp.dot(ds.astype(BF16), k, preferred_element_type=F32)

        zero = jnp.zeros((TQ, HP), F32)
        dqa, dqb = lax.fori_loop(0, 2 * qi, lambda j, c: (step(subs[0], j, c[0], False), step(subs[1], j, c[1], False)),
                                 (zero, zero))
        dq_ref[:TQ, :] = step(subs[0], 2 * qi, dqa, True)
        dq_ref[TQ:, :] = step(subs[1], 2 * qi + 1, step(subs[1], 2 * qi, dqb, False), True)

    return pl.pallas_call(
        body, name=name, grid=(NH, nq),
        in_specs=[pl.BlockSpec((2 * TQ, HP), lambda h, i: (i, h)), pl.BlockSpec((S, HP), lambda h, i: (0, h)),
                  pl.BlockSpec((S, DH), lambda h, i: (0, NH + h)), pl.BlockSpec((2 * TQ, DH), lambda h, i: (i, h)),
                  pl.BlockSpec((2 * TQ, DH), lambda h, i: (i, h)), pl.BlockSpec((1, 2 * TQ, 1), lambda h, i: (h, i, 0))],
        out_specs=[pl.BlockSpec((2 * TQ, HP), lambda h, i: (i, h)), pl.BlockSpec((1, 2 * TQ, 1), lambda h, i: (h, i, 0))],
        out_shape=[jax.ShapeDtypeStruct((S, NH * HP), F32), jax.ShapeDtypeStruct((NH, S, 1), F32)],
        compiler_params=_cparams(("parallel", "arbitrary")))(qp, kp, kv, o, do, lse)


def _flash_bwd_dkv(name, qp, kp, kv, do, lse_row, delta_row):
    S = qp.shape[0]
    nq = S // TQ

    def body(q_ref, k_ref, v_ref, do_ref, lse_ref, dl_ref, dk_ref, dv_ref):
        kj = pl.program_id(1)
        subs = [(k_ref[u * TQ:(u + 1) * TQ, :], v_ref[u * TQ:(u + 1) * TQ, :].astype(BF16)) for u in range(2)]

        def step(sub, i, carry, masked):
            k, vb = sub
            dk, dv = carry
            rows = pl.ds(pl.multiple_of(i * TQ, TQ), TQ)
            q = q_ref[rows, :]
            dob = do_ref[rows, :].astype(BF16)
            st = _dot_nt(k, q) * SM_SCALE
            pt = jnp.exp(st - lse_ref[0, :, rows])
            if masked:
                pt = jnp.where(_diag_mask(True), pt, 0.0)
            dv = dv + jnp.dot(pt.astype(BF16), dob, preferred_element_type=F32)
            dpt = _dot_nt(vb, dob)
            dst = pt * (dpt - dl_ref[0, :, rows]) * SM_SCALE
            dk = dk + jnp.dot(dst.astype(BF16), q, preferred_element_type=F32)
            return dk, dv

        zero = (jnp.zeros((TQ, HP), F32), jnp.zeros((TQ, DH), F32))
        ca = step(subs[0], 2 * kj + 1, step(subs[0], 2 * kj, zero, True), False)
        cb = step(subs[1], 2 * kj + 1, zero, True)
        ca, cb = lax.fori_loop(2 * kj + 2, nq, lambda i, c: (step(subs[0], i, c[0], False), step(subs[1], i, c[1], False)),
                               (ca, cb))
        for u, (dk, dv) in enumerate((ca, cb)):
            dk_ref[u * TQ:(u + 1) * TQ, :] = dk
            dv_ref[u * TQ:(u + 1) * TQ, :] = dv

    return pl.pallas_call(
        body, name=name, grid=(NH, nq // 2),
        in_specs=[pl.BlockSpec((S, HP), lambda h, j: (0, h)), pl.BlockSpec((2 * TQ, HP), lambda h, j: (j, h)),
                  pl.BlockSpec((2 * TQ, DH), lambda h, j: (j, NH + h)), pl.BlockSpec((S, DH), lambda h, j: (0, h)),
                  pl.BlockSpec((1, 1, S), lambda h, j: (h, 0, 0)), pl.BlockSpec((1, 1, S), lambda h, j: (h, 0, 0))],
        out_specs=[pl.BlockSpec((2 * TQ, HP), lambda h, j: (j, h)), pl.BlockSpec((2 * TQ, DH), lambda h, j: (j, h))],
        out_shape=[jax.ShapeDtypeStruct((S, NH * HP), F32), jax.ShapeDtypeStruct((S, NH * DH), F32)],
        compiler_params=_cparams(("parallel", "arbitrary")))(qp, kp, kv, do, lse_row, delta_row)


def _tm(S, width):
    t = 512 if width <= 1024 else (256 if width <= 3072 else 128)
    return min(t, S)


def _mod_fwd(tag, x, g, shift, scale):
    S = x.shape[0]
    return _rw_fwd(tag + "_mod", f_mod, [Row(x)], [g, shift, scale], [((D,), BF16)], _tm(S, D))[0]


def _mod_bwd(tag, x, g, shift, scale, dh, dx_direct):
    S = x.shape[0]
    r = _rw_bwd(tag + "_mod_b", f_mod, [Row(x)], [g, shift, scale], [Row(dh)], [True], [True] * 3, [((D,), F32)],
                _tm(S, D), add=Row(dx_direct))
    return r[0], r[1:]


def _res_fwd(tag, x, y, gate, coef):
    S = x.shape[0]

    def fn(pieces, bvals):
        return [pieces[0] + coef * bvals[0] * pieces[1]], []
    return _rowwise(tag + "_res", fn, [Row(x), Row(y)], [gate], [((D,), F32)], [], _tm(S, D))[0]


def _res_bwd(tag, y, gate, dxn, coef):
    S = y.shape[0]
    r = _rw_bwd(tag + "_res_b", make_f_res(coef), [Row(y)], [gate], [Row(dxn)], [True], [True], [((D,), BF16)], _tm(S, D))
    return r[0], r[1]


def _ffn_fwd(tag, x, mod3, g, w_in4, w_out4, li):
    shift, scale, gate = mod3
    S = x.shape[0]
    h = _mod_fwd(tag, x, g, shift, scale)
    gu = _matmul(tag + "_in", h, w_in4, lay="b_cols", li=li, out_dtype=BF16)
    a = _rw_fwd(tag + "_act", f_act, [Row(gu, splits=[FF, FF])], [], [((FF,), BF16)], _tm(S, FF))[0]
    y = _matmul(tag + "_out", a, w_out4, lay="b_rows", li=li)
    xn = _res_fwd(tag, x, y, gate, 0.5)
    return xn, (x, h, gu, a, y)


def _ffn_bwd(tag, dxn, res, mod3, g, w_in4, w_out4, li, g_in4, g_out4):
    shift, scale, gate = mod3
    x, h, gu, a, y = res
    S = x.shape[0]
    nmat = w_in4.shape[1]
    dy, dgate = _res_bwd(tag, y, gate, dxn, 0.5)
    da = _matmul(tag + "_out_bi", dy, w_out4, "nt", lay="b_rows", li=li, out_dtype=BF16)
    g_out4 = _matmul(tag + "_out_bw", a, dy, "tn", lay="o_rows", li=li, into=g_out4, nmat=nmat, out_dtype=BF16)
    dgu = _rw_bwd(tag + "_act_b", f_act, [Row(gu, splits=[FF, FF])], [], [Row(da)], [True, True], [],
                  [((FF, FF), BF16)], _tm(S, FF))[0]
    dh = _matmul(tag + "_in_bi", dgu, w_in4, "nt", lay="b_cols", li=li)
    g_in4 = _matmul(tag + "_in_bw", h, dgu, "tn", lay="o_cols", li=li, into=g_in4, nmat=nmat, out_dtype=BF16)
    dx, (dg, dshift, dscale) = _mod_bwd(tag, x, g, shift, scale, dh, dxn)
    return dx, g_in4, g_out4, dict(g=dg, mod=(dshift, dscale, dgate))


def _pad_lanes(a, lo, width=LANE):
    return jnp.pad(a, ((0, 0), (lo, width - lo - a.shape[1])))


def _gdn_layer_fwd(tag, x, mod3, g, p):
    shift, scale, gate = mod3
    S = x.shape[0]
    NC = S // CHUNK
    h = _mod_fwd(tag, x, g, shift, scale)
    proj = _matmul(tag + "_in", h, p["w_in"])
    qc = _conv_fwd(tag + "_conv", proj, p["conv_w8"], 3 * D, _tm(S, 3 * D))
    q, k, v = _rw_fwd(tag + "_pre", f_gdnpre, [Row(qc, splits=[DH] * (3 * NH))], [],
                      [((DH,) * NH, F32)] * 3, _tm(S, 3 * D))
    betaf, gf = _rw_fwd(tag + "_gates", f_gates, [Row(proj, LANE, cb=GATE_CB)], [p["a_log128"], p["dt_bias128"]],
                        [((LANE,), F32)] * 2, _tm(S, LANE))
    bcol, gcol = betaf[:, :NH], gf[:, NH:2 * NH]
    grow = gcol.reshape(NC, CHUNK, NH).transpose(0, 2, 1)
    o, ssave, tsave = _gdn_fwd(tag + "_core", q, k, v, gcol, grow, bcol)
    on = _rw_fwd(tag + "_post", f_gdnpost, [Row(o, splits=[DH] * NH), Row(proj, D, cb=3, splits=[DH] * NH)],
                 [p["norm_g"]], [((DH,) * NH, BF16)], _tm(S, 2 * D))[0]
    y = _matmul(tag + "_out", on, p["w_out"])
    xn = _res_fwd(tag, x, y, gate, 1.0)
    return xn, (x, h, proj, qc, q, k, v, gcol, grow, bcol, ssave, tsave, o, on, y)


def _gdn_layer_bwd(tag, dxn, res, mod3, g, p):
    shift, scale, gate = mod3
    x, h, proj, qc, q, k, v, gcol, grow, bcol, ssave, tsave, o, on, y = res
    S = x.shape[0]
    dy, dgate = _res_bwd(tag, y, gate, dxn, 1.0)
    don = _matmul(tag + "_out_bi", dy, p["w_out"], "nt")
    dw_out = _matmul(tag + "_out_bw", on, dy, "tn")
    do, dz, dnorm = _rw_bwd(tag + "_post_b", f_gdnpost, [Row(o, splits=[DH] * NH), Row(proj, D, cb=3, splits=[DH] * NH)],
                            [p["norm_g"]], [Row(don, splits=[DH] * NH)], [True] * (2 * NH), [True],
                            [((DH,) * NH, F32), ((DH,) * NH, BF16)], _tm(S, 2 * D))
    dq, dk, dv, dgc, dgr, db = _gdn_bwd(tag + "_core_b", q, k, v, gcol, grow, bcol, ssave, tsave, do)
    dgcol = dgc + dgr.transpose(0, 2, 1).reshape(S, NH)
    dgates, da_log, ddt = _rw_bwd(tag + "_gates_b", f_gates, [Row(proj, LANE, cb=GATE_CB)], [p["a_log128"], p["dt_bias128"]],
                                  [Row(_pad_lanes(db, 0)), Row(_pad_lanes(dgcol, NH))], [True], [True, True],
                                  [((LANE,), BF16)], _tm(S, LANE))
    dqc = _rw_bwd(tag + "_pre_b", f_gdnpre, [Row(qc, splits=[DH] * (3 * NH))], [],
                  [Row(dq, splits=[DH] * NH), Row(dk, splits=[DH] * NH), Row(dv, splits=[DH] * NH)],
                  [True] * (3 * NH), [], [((DH,) * (3 * NH), F32)], _tm(S, 3 * D))[0]
    dqkv, dconv = _conv_bwd(tag + "_conv_b", proj, dqc, p["conv_w8"], 3 * D, _tm(S, 3 * D), BF16)
    dproj = jnp.concatenate([dqkv, dz, dgates], axis=1)
    dh = _matmul(tag + "_in_bi", dproj, p["w_in"], "nt")
    dw_in = _matmul(tag + "_in_bw", h, dproj, "tn")
    dx, (dg, dshift, dscale) = _mod_bwd(tag, x, g, shift, scale, dh, dxn)
    return dx, dict(w_in=dw_in, conv_w8=dconv, a_log128=da_log, dt_bias128=ddt, norm_g=dnorm,
                    w_out=dw_out, g=dg, mod=(dshift, dscale, dgate))


def _qk_rows(src, shared_rope, ckv=None):
    if shared_rope:
        return [Row(src, D, cb=0, splits=[DH] * NH), Row(ckv, LANE, cb=2)]
    return [Row(src, splits=[DH] * (2 * NH))]


def _kv_fwd(x, kvmod, p, tabs):
    shift, scale = kvmod
    S = x.shape[0]
    h = _mod_fwd("kv", x, p["kv_norm_g"], shift, scale)
    ckv = _matmul("kv_dkv", h, p["w_dkv"])
    lat = _rw_fwd("kv_lat", f_rms, [Row(ckv, KVL)], [p["kv_lat_g"]], [((KVL,), BF16)], _tm(S, KVL))[0]
    kvf = _matmul("kv_ukv", lat, p["w_ukv"])
    kp = _rw_fwd("kv_k", make_f_qk(True), _qk_rows(kvf, True, ckv) + [Row(tabs[0]), Row(tabs[1])],
                 [p["k_gn"], p["k_gr"], p["pm"]], [((DH,) * (2 * NH), BF16)], _tm(S, 2 * D))[0]
    return kp, kvf, (x, h, ckv, lat)


def _kv_bwd(dkp, dv, dx_direct, res, kvmod, kvf, p, tabs):
    shift, scale = kvmod
    x, h, ckv, lat = res
    S = x.shape[0]
    dkn, dkr, dgn, dgr = _rw_bwd("kv_k_b", make_f_qk(True), _qk_rows(kvf, True, ckv) + [Row(tabs[0]), Row(tabs[1])],
                                 [p["k_gn"], p["k_gr"], p["pm"]], [Row(dkp, splits=[DH] * (2 * NH))],
                                 [True] * (NH + 1) + [False, False], [True, True, False],
                                 [((DH,) * NH, BF16), ((LANE,), BF16)], _tm(S, 2 * D))
    dkvf = jnp.concatenate([dkn, dv.astype(BF16)], axis=1)
    dlat = _matmul("kv_ukv_bi", dkvf, p["w_ukv"], "nt")
    dw_ukv = _matmul("kv_ukv_bw", lat, dkvf, "tn")
    dcl, dlg = _rw_bwd("kv_lat_b", f_rms, [Row(ckv, KVL)], [p["kv_lat_g"]], [Row(dlat)], [True], [True],
                       [((KVL,), BF16)], _tm(S, KVL))
    dckv = jnp.concatenate([dcl, dkr], axis=1)
    dh = _matmul("kv_dkv_bi", dckv, p["w_dkv"], "nt")
    dw_dkv = _matmul("kv_dkv_bw", h, dckv, "tn")
    dx, (dg, dshift, dscale) = _mod_bwd("kv", x, p["kv_norm_g"], shift, scale, dh, dx_direct)
    return dx, dict(w_dkv=dw_dkv, w_ukv=dw_ukv, kv_lat_g=dlg, k_gn=dgn, k_gr=dgr, kv_norm_g=dg, mod=(dshift, dscale))


def _mla_layer_fwd(tag, x, mod3, g, p, kp, kvf, tabs):
    shift, scale, gate = mod3
    S = x.shape[0]
    h = _mod_fwd(tag, x, g, shift, scale)
    ql = _matmul(tag + "_dq", h, p["w_dq"])
    qln = _rw_fwd(tag + "_qln", f_rms, [Row(ql)], [p["ql_g"]], [((QL,), BF16)], _tm(S, QL))[0]
    qu = _matmul(tag + "_uq", qln, p["w_uq"])
    qp = _rw_fwd(tag + "_q", make_f_qk(False), _qk_rows(qu, False) + [Row(tabs[0]), Row(tabs[1])],
                 [p["q_gn"], p["q_gr"], p["pm"]], [((DH,) * (2 * NH), BF16)], _tm(S, 2 * D))[0]
    o, lse = _flash_fwd(tag + "_att", qp, kp, kvf)
    y = _matmul(tag + "_out", o, p["w_out"])
    xn = _res_fwd(tag, x, y, gate, 1.0)
    return xn, (x, h, ql, qln, qu, qp, o, lse, y)


def _mla_layer_bwd(tag, dxn, res, mod3, g, p, kp, kvf, tabs):
    shift, scale, gate = mod3
    x, h, ql, qln, qu, qp, o, lse, y = res
    S = x.shape[0]
    dy, dgate = _res_bwd(tag, y, gate, dxn, 1.0)
    do = _matmul(tag + "_out_bi", dy, p["w_out"], "nt")
    dw_out = _matmul(tag + "_out_bw", o, dy, "tn")
    dqp, delta = _flash_bwd_dq(tag + "_att_bq", qp, kp, kvf, o, do, lse)
    dkp, dv = _flash_bwd_dkv(tag + "_att_bkv", qp, kp, kvf, do, lse.reshape(NH, 1, S), delta.reshape(NH, 1, S))
    dqu, dgn, dgr = _rw_bwd(tag + "_q_b", make_f_qk(False), _qk_rows(qu, False) + [Row(tabs[0]), Row(tabs[1])],
                            [p["q_gn"], p["q_gr"], p["pm"]], [Row(dqp, splits=[DH] * (2 * NH))],
                            [True] * (2 * NH) + [False, False], [True, True, False],
                            [((DH,) * (2 * NH), BF16)], _tm(S, 2 * D))
    dqln = _matmul(tag + "_uq_bi", dqu, p["w_uq"], "nt")
    dw_uq = _matmul(tag + "_uq_bw", qln, dqu, "tn")
    dql, dqlg = _rw_bwd(tag + "_qln_b", f_rms, [Row(ql)], [p["ql_g"]], [Row(dqln)], [True], [True], [((QL,), BF16)],
                        _tm(S, QL))
    dh = _matmul(tag + "_dq_bi", dql, p["w_dq"], "nt")
    dw_dq = _matmul(tag + "_dq_bw", h, dql, "tn")
    dx, (dg, dshift, dscale) = _mod_bwd(tag, x, g, shift, scale, dh, dxn)
    return dx, dkp, dv, dict(w_dq=dw_dq, w_uq=dw_uq, w_out=dw_out, ql_g=dqlg, q_gn=dgn, q_gr=dgr, g=dg,
                             mod=(dshift, dscale, dgate))


def _loss_head(y, tgt):
    S = y.shape[0]

    def fn(pieces, bvals):
        e = pieces[0] - pieces[1]
        part = jnp.sum(e * e) * (0.5 / D)
        return [e * (1.0 / D)], [jnp.full((1, LANE), part, F32)]
    dy, part = _rowwise("loss", fn, [Row(y), Row(tgt)], [], [((D,), F32)], [(1, LANE)], _tm(S, D))
    return part[0, 0], dy


def _rope_tables(positions):
    S = positions.shape[0]
    half = ROPE // 2
    lane = lax.broadcasted_iota(jnp.int32, (1, LANE), 1)
    inv_freq = ROPE_BASE ** (-(lane % half).astype(F32) / half)
    live = (lane < ROPE).astype(F32)
    sign = jnp.where(lane < half, -1.0, 1.0) * live

    def fn(pieces, bvals):
        ang = pieces[0] * bvals[0]
        return [jnp.cos(ang) * bvals[1], jnp.sin(ang) * bvals[2]], []
    pos = jnp.broadcast_to(positions.astype(F32)[:, None], (S, LANE))
    cosp, sins = _rowwise("rope_tab", fn, [Row(pos)], [inv_freq, live, sign], [((LANE,), F32)] * 2, [], _tm(S, LANE))
    r = lax.broadcasted_iota(jnp.int32, (LANE, LANE), 0)
    c = lax.broadcasted_iota(jnp.int32, (LANE, LANE), 1)
    pm = (((c < half) & (r == c + half)) | ((c >= half) & (c < ROPE) & (r == c - half))).astype(F32)
    return (cosp, sins), pm


def _adamw(name, w, g, m, v):
    shape = w.shape
    C = shape[-1]
    R = w.size // C
    tr = R
    for t in (1024, 512, 256, 128, 64, 32, 16, 8):
        if R % t == 0 and t * C * 4 <= (1 << 21):
            tr = t
            break
    c1 = 1.0 - ADAM_B1 ** ADAM_STEP
    c2 = 1.0 - ADAM_B2 ** ADAM_STEP

    def body(w_ref, g_ref, m_ref, v_ref, d_ref, mo_ref, vo_ref):
        gg = g_ref[...]
        mn = ADAM_B1 * m_ref[...] + (1.0 - ADAM_B1) * gg
        vn = ADAM_B2 * v_ref[...] + (1.0 - ADAM_B2) * (gg * gg)
        d_ref[...] = -ADAM_LR * ((mn / c1) / (jnp.sqrt(vn / c2) + ADAM_EPS) + ADAM_WD * w_ref[...])
        mo_ref[...] = mn
        vo_ref[...] = vn

    spec = pl.BlockSpec((tr, C), lambda i: (i, 0))
    outs = pl.pallas_call(body, name=name, grid=(R // tr,), in_specs=[spec] * 4, out_specs=[spec] * 3,
                          out_shape=[jax.ShapeDtypeStruct((R, C), F32)] * 3,
                          compiler_params=_cparams(("parallel",)))(*[t.reshape(R, C) for t in (w, g, m, v)])
    return [o.reshape(shape) for o in outs]


HBM_SPEC = pl.BlockSpec(memory_space=pltpu.HBM)
OTHER_CHIPS = (4, 2, 6)
SIBLING = 1


def _me():
    return lax.axis_index("x"), lax.axis_index("y"), lax.axis_index("c")


def _peer(me, k):
    mx, my, mc = me
    return ((1 - mx) if k & 4 else mx, (1 - my) if k & 2 else my, (1 - mc) if k & 1 else mc)


def _rcopy(src, dst, ssem, rsem, to):
    return pltpu.make_async_remote_copy(src_ref=src, dst_ref=dst, send_sem=ssem, recv_sem=rsem, device_id=to,
                                        device_id_type=MESH)


def _all_gather8(name, x):
    def body(x_ref, o_ref, ssem, rsem, lsem):
        me = _me()
        mine = 4 * me[0] + 2 * me[1] + me[2]
        loc = pltpu.make_async_copy(x_ref, o_ref.at[mine], lsem)
        loc.start()
        sends = []
        for k in range(1, 8):
            cp = _rcopy(x_ref, o_ref.at[mine], ssem.at[k - 1], rsem.at[k - 1], _peer(me, k))
            cp.start()
            sends.append(cp)
        for k in range(1, 8):
            px, py, pc = _peer(me, k)
            _rcopy(x_ref, o_ref.at[4 * px + 2 * py + pc], ssem.at[k - 1], rsem.at[k - 1], (px, py, pc)).wait_recv()
        for cp in sends:
            cp.wait_send()
        loc.wait()

    return pl.pallas_call(body, name=name, out_shape=jax.ShapeDtypeStruct((8,) + x.shape, x.dtype),
                          in_specs=[HBM_SPEC], out_specs=HBM_SPEC,
                          scratch_shapes=[pltpu.SemaphoreType.DMA((7,)), pltpu.SemaphoreType.DMA((7,)),
                                          pltpu.SemaphoreType.DMA(())])(x)


PACK_L = 1024
PACK_RT = 256


def _place_shard(name, wp, chip):
    rh, ln = wp.shape[1:]

    def body(s_ref, w_ref, o_ref):
        o_ref[...] = w_ref[...]

    gs = pltpu.PrefetchScalarGridSpec(
        num_scalar_prefetch=1, grid=(2, rh // PACK_RT),
        in_specs=[pl.BlockSpec((None, PACK_RT, ln), lambda h, i, s_ref: (h, i, 0))],
        out_specs=pl.BlockSpec((None, None, PACK_RT, ln), lambda h, i, s_ref: (s_ref[0], h, i, 0)))
    return pl.pallas_call(body, name=name, grid_spec=gs, out_shape=jax.ShapeDtypeStruct((4,) + wp.shape, wp.dtype),
                          compiler_params=_cparams(("parallel", "parallel")))(chip.reshape(1).astype(jnp.int32), wp)


def _gather_weights(name, w4):
    r2 = w4.shape[2] // 2

    def body(w_ref, o_ref, ssem, rsem):
        me = _me()
        mc = me[2]
        px, py, pd, sib = _peer(me, 4), _peer(me, 2), _peer(me, 6), _peer(me, SIBLING)
        chip = lambda p: 2 * p[0] + p[1]
        mine, from_x, from_y, from_d = (o_ref.at[chip(p), mc] for p in (me, px, py, pd))
        q0, q1 = pl.ds(0, r2), pl.ds(r2, r2)
        sends = [_rcopy(mine, mine, ssem.at[0], rsem.at[0], px), _rcopy(mine, mine, ssem.at[1], rsem.at[1], py)]
        for cp in sends:
            cp.start()
        _rcopy(from_x, from_x, ssem.at[0], rsem.at[0], px).wait_recv()
        sends += [_rcopy(from_x.at[q1], from_x.at[q1], ssem.at[2], rsem.at[2], py),
                  _rcopy(from_x, from_x, ssem.at[4], rsem.at[4], sib)]
        sends[-2].start()
        sends[-1].start()
        _rcopy(from_y, from_y, ssem.at[1], rsem.at[1], py).wait_recv()
        sends += [_rcopy(from_y.at[q0], from_y.at[q0], ssem.at[3], rsem.at[3], px),
                  _rcopy(from_y, from_y, ssem.at[5], rsem.at[5], sib)]
        sends[-2].start()
        sends[-1].start()
        _rcopy(from_d.at[q0], from_d.at[q0], ssem.at[3], rsem.at[3], px).wait_recv()
        _rcopy(from_d.at[q1], from_d.at[q1], ssem.at[2], rsem.at[2], py).wait_recv()
        sends.append(_rcopy(from_d, from_d, ssem.at[6], rsem.at[6], sib))
        sends[-1].start()
        for j, p in enumerate((px, py, pd)):
            land = o_ref.at[chip(p), 1 - mc]
            _rcopy(land, land, ssem.at[4 + j], rsem.at[4 + j], sib).wait_recv()
        for cp in sends:
            cp.wait_send()

    return pl.pallas_call(body, name=name, out_shape=jax.ShapeDtypeStruct(w4.shape, w4.dtype),
                          in_specs=[HBM_SPEC], out_specs=HBM_SPEC, input_output_aliases={0: 0},
                          scratch_shapes=[pltpu.SemaphoreType.DMA((7,)), pltpu.SemaphoreType.DMA((7,))])(w4)


def _exchange_half(name, g):
    def body(g_ref, p_ref, ssem, rsem):
        me = _me()
        cps = []
        for s in range(4):
            cp = _rcopy(g_ref.at[s, 1 - me[2]], p_ref.at[s], ssem.at[s], rsem.at[s], _peer(me, SIBLING))
            cp.start()
            cps.append(cp)
        for cp in cps:
            cp.wait()

    return pl.pallas_call(body, name=name, out_shape=jax.ShapeDtypeStruct((4,) + g.shape[2:], g.dtype),
                          in_specs=[HBM_SPEC], out_specs=HBM_SPEC,
                          scratch_shapes=[pltpu.SemaphoreType.DMA((4,)), pltpu.SemaphoreType.DMA((4,))])(g)


def _scatter_chips(name, q):
    r2 = q.shape[1] // 2

    def body(q_ref, t_ref, relay, ssem, rsem):
        me = _me()
        px, py, pd = _peer(me, 4), _peer(me, 2), _peer(me, 6)
        chip = lambda p: 2 * p[0] + p[1]
        q0, q1 = pl.ds(0, r2), pl.ds(r2, r2)
        sends = [_rcopy(q_ref.at[chip(px)], t_ref.at[0], ssem.at[0], rsem.at[0], px),
                 _rcopy(q_ref.at[chip(py)], t_ref.at[1], ssem.at[1], rsem.at[1], py),
                 _rcopy(q_ref.at[chip(pd), q0], relay.at[0], ssem.at[2], rsem.at[2], py),
                 _rcopy(q_ref.at[chip(pd), q1], relay.at[1], ssem.at[3], rsem.at[3], px)]
        for cp in sends:
            cp.start()
        _rcopy(relay.at[0], relay.at[0], ssem.at[2], rsem.at[2], py).wait_recv()
        sends.append(_rcopy(relay.at[0], t_ref.at[2, q0], ssem.at[4], rsem.at[4], px))
        sends[-1].start()
        _rcopy(relay.at[1], relay.at[1], ssem.at[3], rsem.at[3], px).wait_recv()
        sends.append(_rcopy(relay.at[1], t_ref.at[2, q1], ssem.at[5], rsem.at[5], py))
        sends[-1].start()
        _rcopy(t_ref.at[0], t_ref.at[0], ssem.at[0], rsem.at[0], px).wait_recv()
        _rcopy(t_ref.at[1], t_ref.at[1], ssem.at[1], rsem.at[1], py).wait_recv()
        _rcopy(t_ref.at[2, q0], t_ref.at[2, q0], ssem.at[4], rsem.at[4], px).wait_recv()
        _rcopy(t_ref.at[2, q1], t_ref.at[2, q1], ssem.at[5], rsem.at[5], py).wait_recv()
        for cp in sends:
            cp.wait_send()

    return pl.pallas_call(body, name=name,
                          out_shape=[jax.ShapeDtypeStruct((3,) + q.shape[1:], q.dtype),
                                     jax.ShapeDtypeStruct((2, r2) + q.shape[2:], q.dtype)],
                          in_specs=[HBM_SPEC], out_specs=[HBM_SPEC, HBM_SPEC],
                          scratch_shapes=[pltpu.SemaphoreType.DMA((6,)), pltpu.SemaphoreType.DMA((6,))])(q)[0]


def _exchange_full(name, r2):
    def body(r_ref, o_ref, ssem, rsem):
        me = _me()
        mc = me[2]
        cp = _rcopy(o_ref.at[mc], o_ref.at[mc], ssem, rsem, _peer(me, SIBLING))
        cp.start()
        _rcopy(o_ref.at[1 - mc], o_ref.at[1 - mc], ssem, rsem, _peer(me, SIBLING)).wait_recv()
        cp.wait_send()

    return pl.pallas_call(body, name=name, out_shape=jax.ShapeDtypeStruct(r2.shape, r2.dtype),
                          in_specs=[HBM_SPEC], out_specs=HBM_SPEC, input_output_aliases={0: 0},
                          scratch_shapes=[pltpu.SemaphoreType.DMA(()), pltpu.SemaphoreType.DMA(())])(r2)


def _add_half(name, g, p, c):
    rh, ln = g.shape[2:]

    def body(c_ref, g_ref, p_ref, o_ref):
        o_ref[0] = (g_ref[0, 0].astype(F32) + p_ref[0].astype(F32)).astype(o_ref.dtype)

    gs = pltpu.PrefetchScalarGridSpec(
        num_scalar_prefetch=1, grid=(4, rh // PACK_RT),
        in_specs=[pl.BlockSpec((1, 1, PACK_RT, ln), lambda s, i, c_ref: (s, c_ref[0], i, 0)),
                  pl.BlockSpec((1, PACK_RT, ln), lambda s, i, c_ref: (s, i, 0))],
        out_specs=pl.BlockSpec((1, PACK_RT, ln), lambda s, i, c_ref: (s, i, 0)))
    return pl.pallas_call(body, name=name, grid_spec=gs, out_shape=jax.ShapeDtypeStruct((4, rh, ln), BF16),
                          compiler_params=_cparams(("parallel", "parallel")))(c.reshape(1).astype(jnp.int32), g, p)


def _add_chips(name, q, t, chip, c):
    rh, ln = q.shape[1:]

    def body(s_ref, c_ref, q_ref, t_ref, o_ref):
        o_ref[...] = ((q_ref[0].astype(F32) + t_ref[0].astype(F32)) + t_ref[1].astype(F32)) + t_ref[2].astype(F32)

    gs = pltpu.PrefetchScalarGridSpec(
        num_scalar_prefetch=2, grid=(rh // PACK_RT,),
        in_specs=[pl.BlockSpec((1, PACK_RT, ln), lambda i, s_ref, c_ref: (s_ref[0], i, 0)),
                  pl.BlockSpec((3, PACK_RT, ln), lambda i, s_ref, c_ref: (0, i, 0))],
        out_specs=pl.BlockSpec((None, PACK_RT, ln), lambda i, s_ref, c_ref: (c_ref[0], i, 0)))
    return pl.pallas_call(body, name=name, grid_spec=gs, out_shape=jax.ShapeDtypeStruct((2, rh, ln), F32),
                          compiler_params=_cparams(("parallel",)))(chip.reshape(1).astype(jnp.int32),
                                                                    c.reshape(1).astype(jnp.int32), q, t)


def _sum8(name, a):
    def body(a_ref, o_ref):
        acc = a_ref[0]
        for d in range(1, 8):
            acc = acc + a_ref[d]
        o_ref[...] = acc
    return pl.pallas_call(body, name=name, out_shape=jax.ShapeDtypeStruct(a.shape[1:], F32))(a)


def _silu_rows(name, a):
    def body(a_ref, o_ref):
        o_ref[...] = _silu(a_ref[...])
    return pl.pallas_call(body, name=name, out_shape=jax.ShapeDtypeStruct(a.shape, F32))(a)


REST = (("gdn_w_out", 1), ("mla_w_dkv", 0), ("mla_w_ukv", 1), ("mla_w_dq", 1), ("mla_w_uq", 2), ("mla_w_out", 1))


def _packed_rows(n):
    per_half = -(-n // (2 * PACK_L))
    return -(-per_half // PACK_RT) * PACK_RT


def _pack_flat(flat):
    n = flat.shape[-1]
    rh = _packed_rows(n)
    pad = [(0, 0)] * (flat.ndim - 1) + [(0, 2 * rh * PACK_L - n)]
    return jnp.pad(flat, pad).reshape(flat.shape[:-1] + (2, rh, PACK_L))


def _shards_first(full, axis):
    sh = full.shape
    t = full.reshape(sh[:axis] + (4, sh[axis] // 4) + sh[axis + 1:])
    return jnp.moveaxis(t, axis, 0)


def _shards_merge(stacked, axis):
    t = jnp.moveaxis(stacked, 0, axis)
    sh = t.shape
    return t.reshape(sh[:axis] + (4 * sh[axis + 1],) + sh[axis + 2:])


def _pack_small(parts):
    flat = jnp.concatenate([p.reshape(-1).astype(F32) for p in parts])
    n = flat.shape[0]
    rows = -(-n // (SUB * LANE)) * SUB
    return jnp.pad(flat, (0, rows * LANE - n)).reshape(rows, LANE)


def _unpack_small(buf, shapes):
    lead = buf.shape[:-2]
    flat = buf.reshape(lead + (-1,))
    out, off = [], 0
    for sh in shapes:
        n = 1
        for d in sh:
            n *= d
        out.append(flat[..., off:off + n].reshape(lead + tuple(sh)))
        off += n
    return out


WEIGHTS = ('ada_w', 'ada_b', 'norm_g', 'ffn_w_in', 'ffn_w_out', 'gdn_w_in', 'gdn_conv_w', 'gdn_a_log', 'gdn_dt_bias',
           'gdn_norm_g', 'gdn_w_out', 'kv_ada_w', 'kv_ada_b', 'kv_norm_g', 'mla_w_dkv', 'mla_kv_norm_g', 'mla_w_ukv',
           'mla_k_norm_g', 'mla_w_dq', 'mla_q_lora_norm_g', 'mla_w_uq', 'mla_q_norm_g', 'mla_w_out')
ARGS = ('x', 'c', 'positions') + WEIGHTS + ('loss_target',) + tuple('m_' + n for n in WEIGHTS) + tuple('v_' + n for n in WEIGHTS)


def _split_norm(v):
    return v[None, :DH], _pad_lanes(v[None, DH:], 0)


def _join_norm(gn, gr):
    return jnp.concatenate([gn[0], gr[0, :ROPE]])


def _step(x, tgt, pos, mods, kvmod, W, P):
    tabs, pm = _rope_tables(pos)
    m3 = lambda l, i: tuple(mods[l][3 * i + j][None] for j in range(3))
    ng = lambda l, i: P["norm_g"][l, i][None]
    gdn_p, mla_p = [], []
    for l in range(2):
        gdn_p.append(dict(w_in=jnp.pad(W["gdn_w_in"][l], ((0, 0), (0, GDN_IN - W["gdn_w_in"].shape[2]))),
                          conv_w8=jnp.pad(P["gdn_conv_w"][l], ((0, 4), (0, 0))),
                          a_log128=_pad_lanes(P["gdn_a_log"][l][None], NH), dt_bias128=_pad_lanes(P["gdn_dt_bias"][l][None], NH),
                          norm_g=P["gdn_norm_g"][l][None], w_out=W["gdn_w_out"][l]))
        q_gn, q_gr = _split_norm(P["mla_q_norm_g"][l])
        mla_p.append(dict(w_dq=W["mla_w_dq"][l], ql_g=P["mla_q_lora_norm_g"][l][None],
                          w_uq=jnp.pad(W["mla_w_uq"][l].reshape(QL, NH, QKH), ((0, 0), (0, 0), (0, HP - QKH))).reshape(QL, NH * HP),
                          q_gn=q_gn, q_gr=q_gr, pm=pm, w_out=W["mla_w_out"][l]))
    k_gn, k_gr = _split_norm(P["mla_k_norm_g"])
    kv_p = dict(kv_norm_g=P["kv_norm_g"][None], w_dkv=jnp.pad(W["mla_w_dkv"], ((0, 0), (0, QL - KVL - ROPE))),
                kv_lat_g=P["mla_kv_norm_g"][None],
                w_ukv=W["mla_w_ukv"].reshape(KVL, NH, 2, DH).transpose(0, 2, 1, 3).reshape(KVL, 2 * NH * DH),
                k_gn=k_gn, k_gr=k_gr, pm=pm)
    kvm = (kvmod[0][None], kvmod[1][None])

    res = {}
    for l in range(4):
        x, res[l, 0] = _ffn_fwd(f"l{l}a", x, m3(l, 0), ng(l, 0), W["ffn_w_in"], W["ffn_w_out"], 2 * l)
        if l < 2:
            x, res[l, 1] = _gdn_layer_fwd(f"l{l}g", x, m3(l, 1), ng(l, 1), gdn_p[l])
        else:
            x, res[l, 1] = _mla_layer_fwd(f"l{l}m", x, m3(l, 1), ng(l, 1), mla_p[l - 2], kp, kvf, tabs)
        x, res[l, 2] = _ffn_fwd(f"l{l}b", x, m3(l, 2), ng(l, 2), W["ffn_w_in"], W["ffn_w_out"], 2 * l + 1)
        if l == 1:
            kp, kvf, kres = _kv_fwd(x, kvm, kv_p, tabs)
    loss, dx = _loss_head(x, tgt)

    gw = {n: [None] * W[n].shape[0] for n in ("gdn_w_in", "gdn_w_out", "mla_w_dq", "mla_w_uq", "mla_w_out")}
    g_in4 = g_out4 = None
    gp = {n: [None] * 2 for n in ("gdn_conv_w", "gdn_a_log", "gdn_dt_bias", "gdn_norm_g", "mla_q_lora_norm_g", "mla_q_norm_g")}
    gnorm = [[None] * 3 for _ in range(4)]
    dmod = [[None] * NMOD for _ in range(4)]
    dkp = dv = None
    for l in (3, 2, 1, 0):
        if l == 1:
            dx, gk = _kv_bwd(dkp, dv, dx, kres, kvm, kvf, kv_p, tabs)
        for i in (2, 1, 0):
            if i != 1:
                dx, g_in4, g_out4, gd = _ffn_bwd(f"l{l}{'ab'[i // 2]}", dx, res[l, i], m3(l, i), ng(l, i), W["ffn_w_in"],
                                                 W["ffn_w_out"], 2 * l + i // 2, g_in4, g_out4)
            elif l < 2:
                dx, gd = _gdn_layer_bwd(f"l{l}g", dx, res[l, 1], m3(l, 1), ng(l, 1), gdn_p[l])
                gw["gdn_w_in"][l] = gd["w_in"][:, :W["gdn_w_in"].shape[2]]
                gw["gdn_w_out"][l] = gd["w_out"]
                gp["gdn_conv_w"][l] = gd["conv_w8"][:4]
                gp["gdn_a_log"][l] = gd["a_log128"][0, NH:2 * NH]
                gp["gdn_dt_bias"][l] = gd["dt_bias128"][0, NH:2 * NH]
                gp["gdn_norm_g"][l] = gd["norm_g"][0]
            else:
                dx, dkp_l, dv_l, gd = _mla_layer_bwd(f"l{l}m", dx, res[l, 1], m3(l, 1), ng(l, 1), mla_p[l - 2], kp, kvf, tabs)
                dkp = dkp_l if dkp is None else dkp + dkp_l
                dv = dv_l if dv is None else dv + dv_l
                gw["mla_w_dq"][l - 2], gw["mla_w_out"][l - 2] = gd["w_dq"], gd["w_out"]
                gw["mla_w_uq"][l - 2] = gd["w_uq"].reshape(QL, NH, HP)[:, :, :QKH].reshape(QL, NH * QKH)
                gp["mla_q_lora_norm_g"][l - 2] = gd["ql_g"][0]
                gp["mla_q_norm_g"][l - 2] = _join_norm(gd["q_gn"], gd["q_gr"])
            gnorm[l][i] = gd["g"][0]
            for j in range(3):
                dmod[l][3 * i + j] = gd["mod"][j][0]
    gwf = {n: jnp.stack(v) for n, v in gw.items()}
    gwf["ffn_w_in"], gwf["ffn_w_out"] = g_in4, g_out4
    gwf["mla_w_dkv"] = gk["w_dkv"][:, :KVL + ROPE]
    gwf["mla_w_ukv"] = gk["w_ukv"].reshape(KVL, 2, NH, DH).transpose(0, 2, 1, 3).reshape(KVL, 2 * NH * DH)
    gpf = {n: jnp.stack(v) for n, v in gp.items()}
    gpf["norm_g"] = jnp.stack([jnp.stack(r) for r in gnorm])
    gpf["kv_norm_g"] = gk["kv_norm_g"][0]
    gpf["mla_kv_norm_g"] = gk["kv_lat_g"][0]
    gpf["mla_k_norm_g"] = _join_norm(gk["k_gn"], gk["k_gr"])
    dmods = jnp.stack([jnp.stack(r) for r in dmod])
    dkvmod = jnp.stack([gk["mod"][0][0], gk["mod"][1][0]])
    return loss, dx, gwf, gpf, dmods, dkvmod


SMALL = ("norm_g", "gdn_conv_w", "gdn_a_log", "gdn_dt_bias", "gdn_norm_g", "kv_norm_g", "mla_kv_norm_g", "mla_k_norm_g",
         "mla_q_lora_norm_g", "mla_q_norm_g")


def kernel(x, c, positions, ada_w, ada_b, norm_g, ffn_w_in, ffn_w_out, gdn_w_in, gdn_conv_w, gdn_a_log, gdn_dt_bias,
           gdn_norm_g, gdn_w_out, kv_ada_w, kv_ada_b, kv_norm_g, mla_w_dkv, mla_kv_norm_g, mla_w_ukv, mla_k_norm_g,
           mla_w_dq, mla_q_lora_norm_g, mla_w_uq, mla_q_norm_g, mla_w_out, loss_target, m_ada_w, m_ada_b, m_norm_g,
           m_ffn_w_in, m_ffn_w_out, m_gdn_w_in, m_gdn_conv_w, m_gdn_a_log, m_gdn_dt_bias, m_gdn_norm_g, m_gdn_w_out,
           m_kv_ada_w, m_kv_ada_b, m_kv_norm_g, m_mla_w_dkv, m_mla_kv_norm_g, m_mla_w_ukv, m_mla_k_norm_g, m_mla_w_dq,
           m_mla_q_lora_norm_g, m_mla_w_uq, m_mla_q_norm_g, m_mla_w_out, v_ada_w, v_ada_b, v_norm_g, v_ffn_w_in,
           v_ffn_w_out, v_gdn_w_in, v_gdn_conv_w, v_gdn_a_log, v_gdn_dt_bias, v_gdn_norm_g, v_gdn_w_out, v_kv_ada_w,
           v_kv_ada_b, v_kv_norm_g, v_mla_w_dkv, v_mla_kv_norm_g, v_mla_w_ukv, v_mla_k_norm_g, v_mla_w_dq,
           v_mla_q_lora_norm_g, v_mla_w_uq, v_mla_q_norm_g, v_mla_w_out):
    a = dict(locals())
    mx, my, mc = _me()
    dev = 4 * mx + 2 * my + mc
    chip = 2 * mx + my
    x, tgt, pos = a["x"][0], a["loss_target"][0], a["positions"][0]
    take = lambda arr, i, axis=0: lax.dynamic_index_in_dim(arr, i, axis, keepdims=False)

    pre = _all_gather8("ag_pre", _pack_small([a["c"], a["gdn_conv_w"], a["norm_g"]]))
    c_all, conv_sh, norm_sh = _unpack_small(pre, [(D,), a["gdn_conv_w"].shape, a["norm_g"].shape])
    P = {n: a[n] for n in SMALL}
    P["gdn_conv_w"] = jnp.concatenate([conv_sh[2 * s] for s in range(4)], axis=2)
    P["norm_g"] = jnp.concatenate([norm_sh[2 * s] for s in range(4)], axis=2)
    c_act = _silu_rows("c_act", c_all)
    nada = a["ada_w"].shape[2]
    nkv = a["kv_ada_w"].shape[1]
    modp = [_matmul(f"mod{l}", c_act, a["ada_w"], precise=True, lay="b_stack", li=l) for l in range(4)]
    kvp = _matmul("modkv", c_act, a["kv_ada_w"], precise=True)
    mp = _all_gather8("ag_mod", _pack_small(modp + [kvp]))
    modp_all, kvp_all = _unpack_small(mp, [(4, 8, nada), (8, nkv)])
    mods = jnp.concatenate([take(modp_all[2 * s], dev, 1) for s in range(4)], axis=1) + a["ada_b"]
    mods = mods.reshape(4, NMOD, D)
    kvmod = (jnp.concatenate([take(kvp_all[2 * s], dev, 0) for s in range(4)]) + a["kv_ada_b"]).reshape(2, D)

    def gather(tag, w2):
        return _gather_weights("ag_" + tag, _place_shard("own_" + tag, w2, chip))

    def reduce(tag, g4):
        q = _add_half("rsp_" + tag, g4, _exchange_half("rs1_" + tag, g4), mc)
        r2 = _add_chips("rsc_" + tag, q, _scatter_chips("rs2_" + tag, q), chip, mc)
        return _exchange_full("rs3_" + tag, r2)

    halves = lambda t: t.reshape((2, -1) + t.shape[-1:])
    W = {n: gather(t, halves(a[n].astype(BF16))).reshape((4, 8) + a[n].shape[2:])
         for n, t in (("ffn_w_in", "wi"), ("ffn_w_out", "wo"))}
    wg = gather("wg", a["gdn_w_in"].astype(BF16))
    W["gdn_w_in"] = jnp.concatenate([wg[s] for s in range(4)], axis=2)
    wall = gather("wr", _pack_flat(jnp.concatenate([a[n].reshape(-1).astype(BF16) for n, _ in REST]))).reshape(4, -1)
    off = 0
    for n, ax in REST:
        sz = a[n].size
        W[n] = _shards_merge(wall[:, off:off + sz].reshape((4,) + a[n].shape), ax)
        off += sz

    loss, dx, gw, gp, dmods, dkvmod = _step(x, tgt, pos, mods, kvmod, W, P)
    loss = lax.psum(loss, ("x", "y", "c"))

    grads = {n: reduce(t, gw[n].reshape((4, 2, -1) + a[n].shape[-1:])).reshape(a[n].shape)
             for n, t in (("ffn_w_in", "wi"), ("ffn_w_out", "wo"))}
    ng = a["gdn_w_in"].shape[2]
    grads["gdn_w_in"] = reduce("wg", jnp.stack([gw["gdn_w_in"][:, :, s * ng:(s + 1) * ng] for s in range(4)]))
    gsh = reduce("wr", _pack_flat(jnp.concatenate([_shards_first(gw[n], ax).reshape(4, -1) for n, ax in REST], axis=1)))
    gsh = gsh.reshape(-1)
    off = 0
    for n, _ in REST:
        grads[n] = gsh[off:off + a[n].size].reshape(a[n].shape)
        off += a[n].size

    small = _all_gather8("ag_small", _pack_small([dmods, dkvmod] + [gp[n] for n in SMALL]))
    shapes = [(4, NMOD * D), (2 * D,)] + [gp[n].shape for n in SMALL]
    dmod_all, dkv_all = _unpack_small(small, shapes)[:2]
    tot = _unpack_small(_sum8("sum_small", small), shapes)
    grads["ada_b"], grads["kv_ada_b"] = tot[0], tot[1]
    for n, t in zip(SMALL, tot[2:]):
        grads[n] = t
    grads["norm_g"] = lax.dynamic_slice_in_dim(grads["norm_g"], chip * a["norm_g"].shape[2], a["norm_g"].shape[2], 2)
    grads["gdn_conv_w"] = lax.dynamic_slice_in_dim(grads["gdn_conv_w"], chip * a["gdn_conv_w"].shape[2],
                                                   a["gdn_conv_w"].shape[2], 2)
    ca = jnp.pad(c_act, ((0, LANE - 8), (0, 0)))
    dm = jnp.pad(lax.dynamic_slice_in_dim(dmod_all.reshape(8, 4, NMOD * D), chip * nada, nada, 2), ((0, LANE - 8), (0, 0), (0, 0)))
    gada = None
    for l in range(4):
        gada = _matmul(f"gada{l}", ca, dm[:, l], "tn", precise=True, lay="o_stack", li=l, into=gada, nmat=4)
    grads["ada_w"] = gada
    dk = jnp.pad(lax.dynamic_slice_in_dim(dkv_all, chip * nkv, nkv, 1), ((0, LANE - 8), (0, 0)))
    grads["kv_ada_w"] = _matmul("gadakv", ca, dk, "tn", precise=True)

    upd = [_adamw("adamw_" + n, a[n], grads[n], a["m_" + n], a["v_" + n]) for n in WEIGHTS]
    return (loss, dx[None], *[grads[n] for n in WEIGHTS], *[u[0] for u in upd], *[u[1] for u in upd], *[u[2] for u in upd])
```

```python
import functools

import jax
import jax.numpy as jnp
from jax import lax
from jax.experimental import pallas as pl
from jax.experimental.pallas import tpu as pltpu

F32 = jnp.float32
BF16 = jnp.bfloat16
HI = lax.Precision.HIGHEST
MESH = pl.DeviceIdType.MESH

D = 1024
NH = 8
DH = 128
FF = 2816
NMOD = 9
CHUNK = 64
ROPE = 64
QKH = 192
HP = 256
KVL = 256
QL = 384
GDN_IN = 4224
GATE_CB = 32
EPS = 1e-6
ROPE_BASE = 10000.0
LANE = 128
SUB = 8
VMEM_LIMIT = 56 * 1024 * 1024

ADAM_LR, ADAM_B1, ADAM_B2, ADAM_EPS, ADAM_WD, ADAM_STEP = 0.001, 0.9, 0.999, 1e-08, 0.01, 10


def _tile(n, prefs=(512, 384, 256, 128)):
    for p in prefs:
        if n % p == 0:
            return p
    return n


def _cparams(sem):
    return pltpu.CompilerParams(dimension_semantics=sem, vmem_limit_bytes=VMEM_LIMIT)


class Row:
    def __init__(self, arr, width=None, cb=0, splits=None, halo=None):
        self.arr = arr
        self.width = arr.shape[1] if width is None else width
        self.cb = cb
        self.splits = splits
        self.halo = halo


def _rowwise(name, fn, rows, bcs, outs, accs, tm):
    S = rows[0].arr.shape[0]
    n = S // tm
    nr, nb, no, na = len(rows), len(bcs), len(outs), len(accs)

    def body(*refs):
        rrefs, brefs = refs[:nr], refs[nr:nr + nb]
        orefs, arefs = refs[nr + nb:nr + nb + no], refs[nr + nb + no:]
        pieces = []
        for r, ref in zip(rows, rrefs):
            if r.splits is None:
                pieces.append(ref[...])
            else:
                off = 0
                for w in r.splits:
                    pieces.append(ref[:, off:off + w])
                    off += w
        out_pieces, acc_vals = fn(pieces, [b[...] for b in brefs])
        k = 0
        for (widths, dt), oref in zip(outs, orefs):
            off = 0
            for w in widths:
                oref[:, off:off + w] = out_pieces[k].astype(dt)
                k += 1
                off += w
        if na:
            @pl.when(pl.program_id(0) == 0)
            def _():
                for a in arefs:
                    a[...] = jnp.zeros(a.shape, F32)
            for a, v in zip(arefs, acc_vals):
                a[...] += v

    in_specs = []
    for r in rows:
        if r.halo is None:
            in_specs.append(pl.BlockSpec((tm, r.width), lambda i, cb=r.cb: (i, cb)))
        elif r.halo == "prev":
            in_specs.append(pl.BlockSpec((SUB, r.width), lambda i, cb=r.cb: (jnp.maximum(i * (tm // SUB) - 1, 0), cb)))
        else:
            in_specs.append(pl.BlockSpec((SUB, r.width), lambda i, cb=r.cb: (jnp.minimum((i + 1) * (tm // SUB), S // SUB - 1), cb)))
    in_specs += [pl.BlockSpec(b.shape, lambda i, nd=b.ndim: (0,) * nd) for b in bcs]
    out_specs = [pl.BlockSpec((tm, sum(w)), lambda i: (i, 0)) for w, _ in outs]
    out_specs += [pl.BlockSpec(s, lambda i: (0, 0)) for s in accs]
    out_shape = [jax.ShapeDtypeStruct((S, sum(w)), dt) for w, dt in outs]
    out_shape += [jax.ShapeDtypeStruct(s, F32) for s in accs]
    res = pl.pallas_call(body, name=name, grid=(n,), in_specs=in_specs, out_specs=out_specs, out_shape=out_shape,
                         compiler_params=_cparams(("arbitrary",)))(*[r.arr for r in rows], *bcs)
    return res


def _rw_fwd(name, f, rows, bcs, outs, tm):
    def fn(pieces, bvals):
        return list(f(*[p.astype(F32) for p in pieces], *[b.astype(F32) for b in bvals])), []
    return _rowwise(name, fn, rows, bcs, outs, [], tm)


def _npieces(rows):
    return sum(1 if r.splits is None else len(r.splits) for r in rows)


def _rw_bwd(name, f, rows, bcs, cts, drow, dbc, outs, tm, add=None):
    np_, nct = _npieces(rows), _npieces(cts)

    def fn(pieces, bvals):
        allv = [p.astype(F32) for p in pieces[:np_]] + [b.astype(F32) for b in bvals]
        ct = [p.astype(F32) for p in pieces[np_:np_ + nct]]
        didx = [i for i, m in enumerate(list(drow) + list(dbc)) if m]

        def g(*dv):
            full = list(allv)
            for i, v in zip(didx, dv):
                full[i] = v
            return tuple(f(*full))

        _, vjp = jax.vjp(g, *[allv[i] for i in didx])
        grads = vjp(tuple(ct))
        nrd = sum(bool(m) for m in drow)
        rg, bg = list(grads[:nrd]), list(grads[nrd:])
        if add is not None:
            rg[0] = rg[0] + pieces[np_ + nct].astype(F32)
        return rg, bg

    accs = [b.shape for b, m in zip(bcs, dbc) if m]
    return _rowwise(name, fn, list(rows) + list(cts) + ([add] if add is not None else []), bcs, outs, accs, tm)


def _sigmoid(x):
    return 1.0 / (1.0 + jnp.exp(-x))


def _silu(x):
    return x * _sigmoid(x)


def _softplus(x):
    return jnp.maximum(x, 0.0) + jnp.log(1.0 + jnp.exp(-jnp.abs(x)))


def f_mod(x, g, shift, scale):
    y = x * lax.rsqrt(jnp.mean(x * x, axis=-1, keepdims=True) + EPS)
    return (y * g * (1.0 + scale) + shift,)


def f_rms(x, g):
    return (x * lax.rsqrt(jnp.mean(x * x, axis=-1, keepdims=True) + EPS) * g,)


def f_act(gate, up):
    return (_silu(gate) * up,)


def make_f_res(coef):
    def f_res(y, gate):
        return (coef * gate * y,)
    return f_res


def f_gdnpre(*p):
    out = []
    for i, t in enumerate(p):
        t = _silu(t)
        if i < 2 * NH:
            t = t * lax.rsqrt(jnp.sum(t * t, axis=-1, keepdims=True) + EPS)
        out.append(t)
    return tuple(out)


def f_gates(gates, a_log, dt_bias):
    return _sigmoid(gates), -jnp.exp(a_log) * _softplus(gates + dt_bias)


def f_gdnpost(*a):
    o, z, g = a[:NH], a[NH:2 * NH], a[2 * NH]
    out = []
    for oh, zh in zip(o, z):
        y = oh * lax.rsqrt(jnp.mean(oh * oh, axis=-1, keepdims=True) + EPS) * g
        out.append(y * _silu(zh))
    return tuple(out)


def make_f_qk(shared_rope):
    def f(*a):
        if shared_rope:
            ns, rs = a[:NH], [a[NH]] * NH
            cosp, sins, gn, gr, pm = a[NH + 1:NH + 6]
        else:
            ns, rs = a[0:2 * NH:2], a[1:2 * NH:2]
            cosp, sins, gn, gr, pm = a[2 * NH:2 * NH + 5]
        out = []
        for n, r in zip(ns, rs):
            ss = jnp.sum(n * n, axis=-1, keepdims=True) + jnp.sum(r * r, axis=-1, keepdims=True)
            rstd = lax.rsqrt(ss * (1.0 / QKH) + EPS)
            yn = n * rstd * gn
            yr = r * rstd * gr
            sw = jnp.dot(yr, pm, precision=HI, preferred_element_type=F32)
            out += [yn, yr * cosp + sw * sins]
        return tuple(out)
    return f


def _matmul(name, a, b, mode="nn", out_dtype=F32, precise=False, lay=None, li=0, into=None, nmat=1):
    if lay == "b_cols":
        per = b.shape[3]
        rb, cb = b.shape[2], 4 * per
    elif lay == "b_rows":
        per = b.shape[2]
        rb, cb = 4 * per, b.shape[3]
    elif lay == "b_stack":
        rb, cb = b.shape[1:]
    else:
        rb, cb = b.shape
    if mode == "nn":
        (M, K), N = a.shape, cb
    elif mode == "nt":
        (M, K), N = a.shape, rb
    else:
        (K, M), N = a.shape, cb
    tm = _tile(M, (1024, 512, 256, 128))
    tn = _tile(N, (1408, 1024, 512, 384, 256, 128))
    tk = _tile(K, (1408, 1024, 512, 384, 256, 128))
    if lay == "b_cols":
        tn, tk = (per, tk) if mode == "nn" else (tn, per)
    elif lay == "b_rows":
        tm, tn, tk = (tm, 512, K) if mode == "nn" else (min(tm, 512), N, tk)
    elif lay == "o_cols":
        per = N // 4
        tn = per
    elif lay == "o_rows":
        per = M // 4
        tm, tn = M, 512
    nk = K // tk
    dims = {"nn": (((1,), (0,)), ((), ())), "nt": (((1,), (1,)), ((), ())), "tn": (((0,), (0,)), ((), ()))}[mode]

    def body(a_ref, b_ref, *rest):
        o_ref, acc_ref = rest[-2:]
        k = pl.program_id(2)

        @pl.when(k == 0)
        def _():
            acc_ref[...] = jnp.zeros(acc_ref.shape, F32)

        bv = b_ref[...]
        if lay == "b_rows":
            bv = bv.reshape(4 * per, bv.shape[2])
        if precise:
            acc_ref[...] += lax.dot_general(a_ref[...].astype(F32), bv.astype(F32), dims, precision=HI,
                                            preferred_element_type=F32)
        else:
            acc_ref[...] += lax.dot_general(a_ref[...].astype(BF16), bv.astype(BF16), dims, preferred_element_type=F32)

        @pl.when(k == nk - 1)
        def _():
            if lay == "o_rows":
                for s in range(4):
                    o_ref[s] = acc_ref[s * per:(s + 1) * per, :].astype(o_ref.dtype)
            else:
                o_ref[...] = acc_ref[...].astype(o_ref.dtype)

    a_spec = pl.BlockSpec((tk, tm), lambda i, j, k: (k, i)) if mode == "tn" else pl.BlockSpec((tm, tk), lambda i, j, k: (i, k))
    if lay == "b_cols":
        b_spec = (pl.BlockSpec((None, None, tk, per), lambda i, j, k: (j, li, k, 0)) if mode == "nn" else
                  pl.BlockSpec((None, None, tn, per), lambda i, j, k: (k, li, j, 0)))
    elif lay == "b_rows":
        b_spec = (pl.BlockSpec((4, None, per, tn), lambda i, j, k: (0, li, 0, j)) if mode == "nn" else
                  pl.BlockSpec((4, None, per, tk), lambda i, j, k: (0, li, 0, k)))
    elif lay == "b_stack":
        b_spec = pl.BlockSpec((None, tk, tn), lambda i, j, k: (li, k, j))
    elif mode == "nt":
        b_spec = pl.BlockSpec((tn, tk), lambda i, j, k: (j, k))
    else:
        b_spec = pl.BlockSpec((tk, tn), lambda i, j, k: (k, j))
    if lay == "o_stack":
        o_spec = pl.BlockSpec((None, tm, tn), lambda i, j, k: (li, i, j))
        o_shape = jax.ShapeDtypeStruct((nmat, M, N), out_dtype)
    elif lay == "o_cols":
        o_spec = pl.BlockSpec((None, None, tm, per), lambda i, j, k: (j, li, i, 0))
        o_shape = jax.ShapeDtypeStruct((4, nmat, M, per), out_dtype)
    elif lay == "o_rows":
        o_spec = pl.BlockSpec((4, None, per, tn), lambda i, j, k: (0, li, 0, j))
        o_shape = jax.ShapeDtypeStruct((4, nmat, per, N), out_dtype)
    else:
        o_spec = pl.BlockSpec((tm, tn), lambda i, j, k: (i, j))
        o_shape = jax.ShapeDtypeStruct((M, N), out_dtype)
    in_specs, args, alias = [a_spec, b_spec], [a, b], {}
    if into is not None:
        in_specs.append(pl.BlockSpec(memory_space=pl.ANY))
        args.append(into)
        alias = {2: 0}
    return pl.pallas_call(body, name=name, grid=(M // tm, N // tn, nk), in_specs=in_specs, out_specs=o_spec,
                          out_shape=o_shape, scratch_shapes=[pltpu.VMEM((tm, tn), F32)], input_output_aliases=alias,
                          compiler_params=_cparams(("parallel", "parallel", "arbitrary")))(*args)


def _shift_down(t, p, d):
    if d == 0:
        return t
    tr = pltpu.roll(t, d, 0)
    pr = pltpu.roll(p, d, 0)
    r8 = lax.broadcasted_iota(jnp.int32, p.shape, 0)
    first = jnp.where(r8 < d, pr, tr[:SUB])
    return jnp.concatenate([first, tr[SUB:]], axis=0)


def _shift_up(t, nx, d):
    if d == 0:
        return t
    tm = t.shape[0]
    tr = pltpu.roll(t, tm - d, 0)
    nr = pltpu.roll(nx, SUB - d, 0)
    r8 = lax.broadcasted_iota(jnp.int32, nx.shape, 0)
    last = jnp.where(r8 >= SUB - d, nr, tr[tm - SUB:])
    return jnp.concatenate([tr[:tm - SUB], last], axis=0)


def _conv_fwd(name, proj, w8, C, tm):
    def fn(pieces, bvals):
        t, p = pieces[0].astype(F32), pieces[1].astype(F32)
        w = bvals[0]
        p = jnp.where(pl.program_id(0) == 0, 0.0, p)
        out = w[3:4] * t
        for d in (1, 2, 3):
            out = out + w[3 - d:4 - d] * _shift_down(t, p, d)
        return [out], []
    return _rowwise(name, fn, [Row(proj, C), Row(proj, C, halo="prev")], [w8], [((C,), F32)], [], tm)[0]


def _conv_bwd(name, proj, dout, w8, C, tm, out_dtype):
    n = proj.shape[0] // tm

    def fn(pieces, bvals):
        t, p, g, gn = [v.astype(F32) for v in pieces]
        w = bvals[0]
        i = pl.program_id(0)
        p = jnp.where(i == 0, 0.0, p)
        gn = jnp.where(i == n - 1, 0.0, gn)
        dx = w[3:4] * g
        dws = [jnp.sum(g * t, axis=0, keepdims=True)]
        for d in (1, 2, 3):
            dx = dx + w[3 - d:4 - d] * _shift_up(g, gn, d)
            dws.append(jnp.sum(g * _shift_down(t, p, d), axis=0, keepdims=True))
        dw = jnp.concatenate([dws[3], dws[2], dws[1], dws[0], jnp.zeros((4, g.shape[1]), F32)], axis=0)
        return [dx], [dw]
    return _rowwise(name, fn, [Row(proj, C), Row(proj, C, halo="prev"), Row(dout), Row(dout, halo="next")], [w8],
                    [((C,), out_dtype)], [(SUB, C)], tm)


def _bdot(a, b, ca, cb):
    return lax.dot_general(a.astype(BF16), b.astype(BF16), (((ca,), (cb,)), ((0,), (0,))), preferred_element_type=F32)


def _bdot3(a, b, ca, cb):
    dims = (((ca,), (cb,)), ((0,), (0,)))
    ah, bh = a.astype(BF16), b.astype(BF16)
    al, bl = (a - ah.astype(F32)).astype(BF16), (b - bh.astype(F32)).astype(BF16)
    d = lambda x, y: lax.dot_general(x, y, dims, preferred_element_type=F32)
    return d(ah, bh) + (d(ah, bl) + d(al, bh))


@jax.custom_vjp
def _bmm3(a, b):
    return _bdot3(a, b, 2, 1)


_bmm3.defvjp(lambda a, b: (_bdot3(a, b, 2, 1), (a, b)),
             lambda res, g: (_bdot3(g, res[1], 2, 2), _bdot3(res[0], g, 1, 1)))


def _neumann(nl):
    C = nl.shape[1]
    eye = (lax.broadcasted_iota(jnp.int32, (1, C, C), 1) == lax.broadcasted_iota(jnp.int32, (1, C, C), 2)).astype(F32)
    T = eye + nl
    pw = nl
    for _ in range(C.bit_length() - 2):
        pw = _bdot3(pw, pw, 2, 1)
        T = T + _bdot3(T, pw, 2, 1)
    return T


_unit_lower_inv = jax.custom_vjp(_neumann)


def _unit_lower_inv_fwd(nl):
    T = _neumann(nl)
    return T, T


def _unit_lower_inv_bwd(T, g):
    return (_bdot3(_bdot3(T, g, 1, 1), T, 2, 2),)


_unit_lower_inv.defvjp(_unit_lower_inv_fwd, _unit_lower_inv_bwd)


@jax.custom_vjp
def _known_inv(nl, T):
    return T


_known_inv.defvjp(lambda nl, T: (T, T), lambda T, g: (_unit_lower_inv_bwd(T, g)[0], jnp.zeros_like(T)))


def _gdn_chunk(q, k, v, gcol, grow, bcol, S, T_saved=None):
    C = CHUNK
    ii = lax.broadcasted_iota(jnp.int32, (1, C, C), 1)
    jj = lax.broadcasted_iota(jnp.int32, (1, C, C), 2)
    incl, strict = ii >= jj, ii > jj
    gc_col = jnp.sum(jnp.where(incl, 1.0, 0.0) * grow, axis=2, keepdims=True)
    gc_row = jnp.sum(jnp.where(jj >= ii, 1.0, 0.0) * gcol, axis=1, keepdims=True)
    decay = jnp.where(incl, jnp.exp(jnp.where(incl, gc_col - gc_row, 0.0)), 0.0)
    qs = q * (DH ** -0.5)
    kb = k * bcol
    nl = -jnp.where(strict, _bdot(kb, k, 2, 2) * decay, 0.0)
    T = _unit_lower_inv(nl) if T_saved is None else _known_inv(nl, T_saved)
    egc = jnp.exp(gc_col)
    u = _bmm3(T, v * bcol)
    w = _bmm3(T, kb * egc)
    att = jnp.where(incl, _bdot(qs, k, 2, 2) * decay, 0.0)
    v_new = u - _bdot(w, S, 2, 1)
    o = _bdot(qs * egc, S, 2, 1) + _bdot(att, v_new, 2, 1)
    g_last = jnp.sum(grow, axis=2, keepdims=True)
    k_dec = k * jnp.exp(g_last - gc_col)
    S_out = S * jnp.exp(g_last) + _bdot(k_dec, v_new, 1, 1)
    return o, S_out, T


GDN_STEP = 2


def _chunk_args(refs, c):
    q_ref, k_ref, v_ref, gc_ref, gr_ref, b_ref = refs
    rows = slice(c * CHUNK, (c + 1) * CHUNK)
    heads = lambda ref, w: jnp.stack([ref[rows, h * w:(h + 1) * w] for h in range(NH)])
    grow = jnp.stack([gr_ref[c, h:h + 1, :] for h in range(NH)])
    return heads(q_ref, DH), heads(k_ref, DH), heads(v_ref, DH), heads(gc_ref, 1), grow, heads(b_ref, 1)


def _gdn_specs(NP, rev):
    ix = (lambda i: NP - 1 - i) if rev else (lambda i: i)
    wide = pl.BlockSpec((GDN_STEP * CHUNK, D), lambda i: (ix(i), 0))
    col = pl.BlockSpec((GDN_STEP * CHUNK, NH), lambda i: (ix(i), 0))
    row = pl.BlockSpec((GDN_STEP, NH, CHUNK), lambda i: (ix(i), 0, 0))
    st = pl.BlockSpec((1, NH, DH, DH), lambda i: (ix(i), 0, 0, 0))
    tinv = pl.BlockSpec((GDN_STEP, NH, CHUNK, CHUNK), lambda i: (ix(i), 0, 0, 0))
    return wide, col, row, st, tinv


def _gdn_fwd(name, q, k, v, gcol, grow, bcol):
    S = q.shape[0]
    NC = S // CHUNK
    NP = NC // GDN_STEP

    def body(q_ref, k_ref, v_ref, gc_ref, gr_ref, b_ref, o_ref, ss_ref, t_ref, st):
        @pl.when(pl.program_id(0) == 0)
        def _():
            st[...] = jnp.zeros(st.shape, F32)
        state = st[...]
        ss_ref[0] = state
        for c in range(GDN_STEP):
            o, state, tinv = _gdn_chunk(*_chunk_args((q_ref, k_ref, v_ref, gc_ref, gr_ref, b_ref), c), state)
            for h in range(NH):
                o_ref[c * CHUNK:(c + 1) * CHUNK, h * DH:(h + 1) * DH] = o[h]
            t_ref[c] = tinv
        st[...] = state

    wide, col, row, stsp, tsp = _gdn_specs(NP, False)
    return pl.pallas_call(body, name=name, grid=(NP,), in_specs=[wide, wide, wide, col, row, col],
                          out_specs=[wide, stsp, tsp],
                          out_shape=[jax.ShapeDtypeStruct((S, D), F32), jax.ShapeDtypeStruct((NP, NH, DH, DH), F32),
                                     jax.ShapeDtypeStruct((NC, NH, CHUNK, CHUNK), F32)],
                          scratch_shapes=[pltpu.VMEM((NH, DH, DH), F32)],
                          compiler_params=_cparams(("arbitrary",)))(q, k, v, gcol, grow, bcol)


def _gdn_bwd(name, q, k, v, gcol, grow, bcol, ssave, tsave, do):
    S = q.shape[0]
    NC = S // CHUNK
    NP = NC // GDN_STEP

    def body(q_ref, k_ref, v_ref, gc_ref, gr_ref, b_ref, ss_ref, t_ref, do_ref, dq_ref, dk_ref, dv_ref, dgc_ref, dgr_ref,
             db_ref, dst):
        @pl.when(pl.program_id(0) == 0)
        def _():
            dst[...] = jnp.zeros(dst.shape, F32)
        refs = (q_ref, k_ref, v_ref, gc_ref, gr_ref, b_ref)
        tinv = [t_ref[c] for c in range(GDN_STEP)]

        def chain(state, *flat):
            outs = []
            for c in range(GDN_STEP):
                o, state, _ = _gdn_chunk(*flat[6 * c:6 * c + 6], state, T_saved=tinv[c])
                outs.append(o)
            return tuple(outs) + (state,)

        prim = [a for c in range(GDN_STEP) for a in _chunk_args(refs, c)]
        _, vjp = jax.vjp(chain, ss_ref[0], *prim)
        dos = tuple(jnp.stack([do_ref[c * CHUNK:(c + 1) * CHUNK, h * DH:(h + 1) * DH] for h in range(NH)])
                    for c in range(GDN_STEP))
        grads = vjp(dos + (dst[...],))
        dst[...] = grads[0]
        for c in range(GDN_STEP):
            dq, dk, dv, dgc, dgr, db = grads[1 + 6 * c:7 + 6 * c]
            rows = slice(c * CHUNK, (c + 1) * CHUNK)
            for h in range(NH):
                hs = slice(h * DH, (h + 1) * DH)
                dq_ref[rows, hs] = dq[h]
                dk_ref[rows, hs] = dk[h]
                dv_ref[rows, hs] = dv[h]
                dgc_ref[rows, h:h + 1] = dgc[h]
                dgr_ref[c, h:h + 1, :] = dgr[h]
                db_ref[rows, h:h + 1] = db[h]

    wide, col, row, stsp, tsp = _gdn_specs(NP, True)
    return pl.pallas_call(body, name=name, grid=(NP,), in_specs=[wide, wide, wide, col, row, col, stsp, tsp, wide],
                          out_specs=[wide, wide, wide, col, row, col],
                          out_shape=[jax.ShapeDtypeStruct((S, D), F32)] * 3 + [jax.ShapeDtypeStruct((S, NH), F32),
                                                                                 jax.ShapeDtypeStruct((NC, NH, CHUNK), F32),
                                                                                 jax.ShapeDtypeStruct((S, NH), F32)],
                          scratch_shapes=[pltpu.VMEM((NH, DH, DH), F32)],
                          compiler_params=_cparams(("arbitrary",)))(q, k, v, gcol, grow, bcol, ssave, tsave, do)


TQ = 512
SM_SCALE = QKH ** -0.5
NEG = -1e30


def _diag_mask(transposed):
    r = lax.broadcasted_iota(jnp.int32, (TQ, TQ), 0) // CHUNK
    c = lax.broadcasted_iota(jnp.int32, (TQ, TQ), 1) // CHUNK
    return (r <= c) if transposed else (c <= r)


def _dot_nt(a, b):
    return lax.dot_general(a, b, (((1,), (1,)), ((), ())), preferred_element_type=F32)


def _flash_fwd(name, qp, kp, kv):
    S = qp.shape[0]
    nq = S // (2 * TQ)

    def body(q_ref, k_ref, v_ref, o_ref, lse_ref):
        qi = pl.program_id(1)
        qs = (q_ref[:TQ, :], q_ref[TQ:, :])

        def step(q, j, carry, masked):
            m, l, acc = carry
            rows = pl.ds(pl.multiple_of(j * TQ, TQ), TQ)
            s = _dot_nt(q, k_ref[rows, :]) * SM_SCALE
            if masked:
                s = jnp.where(_diag_mask(False), s, NEG)
            m_new = jnp.maximum(m, jnp.max(s, axis=-1, keepdims=True))
            p = jnp.exp(s - m_new)
            alpha = jnp.exp(m - m_new)
            l = alpha * l + jnp.sum(p, axis=-1, keepdims=True)
            acc = alpha * acc + jnp.dot(p.astype(BF16), v_ref[rows, :].astype(BF16), preferred_element_type=F32)
            return m_new, l, acc

        init = (jnp.full((TQ, 1), NEG, F32), jnp.zeros((TQ, 1), F32), jnp.zeros((TQ, DH), F32))
        ca, cb = lax.fori_loop(0, 2 * qi, lambda j, c: (step(qs[0], j, c[0], False), step(qs[1], j, c[1], False)),
                               (init, init))
        ca = step(qs[0], 2 * qi, ca, True)
        cb = step(qs[1], 2 * qi + 1, step(qs[1], 2 * qi, cb, False), True)
        for u, (m, l, acc) in enumerate((ca, cb)):
            o_ref[u * TQ:(u + 1) * TQ, :] = acc / l
            lse_ref[0, u * TQ:(u + 1) * TQ, :] = m + jnp.log(l)

    return pl.pallas_call(
        body, name=name, grid=(NH, nq),
        in_specs=[pl.BlockSpec((2 * TQ, HP), lambda h, i: (i, h)), pl.BlockSpec((S, HP), lambda h, i: (0, h)),
                  pl.BlockSpec((S, DH), lambda h, i: (0, NH + h))],
        out_specs=[pl.BlockSpec((2 * TQ, DH), lambda h, i: (i, h)), pl.BlockSpec((1, 2 * TQ, 1), lambda h, i: (h, i, 0))],
        out_shape=[jax.ShapeDtypeStruct((S, NH * DH), F32), jax.ShapeDtypeStruct((NH, S, 1), F32)],
        compiler_params=_cparams(("parallel", "arbitrary")))(qp, kp, kv)


def _flash_bwd_dq(name, qp, kp, kv, o, do, lse):
    S = qp.shape[0]
    nq = S // (2 * TQ)

    def body(q_ref, k_ref, v_ref, o_ref, do_ref, lse_ref, dq_ref, dl_ref):
        qi = pl.program_id(1)
        subs = []
        for u in range(2):
            sl = slice(u * TQ, (u + 1) * TQ)
            do = do_ref[sl, :]
            delta = jnp.sum(o_ref[sl, :] * do, axis=-1, keepdims=True)
            dl_ref[0, sl, :] = delta
            subs.append((q_ref[sl, :], do.astype(BF16), lse_ref[0, sl, :], delta))

        def step(sub, j, dq, masked):
            q, dob, lse, delta = sub
            rows = pl.ds(pl.multiple_of(j * TQ, TQ), TQ)
            k = k_ref[rows, :]
            s = _dot_nt(q, k) * SM_SCALE
            if masked:
                s = jnp.where(_diag_mask(False), s, NEG)
            p = jnp.exp(s - lse)
            dp = _dot_nt(dob, v_ref[rows, :].astype(BF16))
            ds = p * (dp - delta) * SM_SCALE
            return dq + jnp.dot(ds.astype(BF16), k, preferred_element_type=F32)

        zero = jnp.zeros((TQ, HP), F32)
        dqa, dqb = lax.fori_loop(0, 2 * qi, lambda j, c: (step(subs[0], j, c[0], False), step(subs[1], j, c[1], False)),
                                 (zero, zero))
        dq_ref[:TQ, :] = step(subs[0], 2 * qi, dqa, True)
        dq_ref[TQ:, :] = step(subs[1], 2 * qi + 1, step(subs[1], 2 * qi, dqb, False), True)

    return pl.pallas_call(
        body, name=name, grid=(NH, nq),
        in_specs=[pl.BlockSpec((2 * TQ, HP), lambda h, i: (i, h)), pl.BlockSpec((S, HP), lambda h, i: (0, h)),
                  pl.BlockSpec((S, DH), lambda h, i: (0, NH + h)), pl.BlockSpec((2 * TQ, DH), lambda h, i: (i, h)),
                  pl.BlockSpec((2 * TQ, DH), lambda h, i: (i, h)), pl.BlockSpec((1, 2 * TQ, 1), lambda h, i: (h, i, 0))],
        out_specs=[pl.BlockSpec((2 * TQ, HP), lambda h, i: (i, h)), pl.BlockSpec((1, 2 * TQ, 1), lambda h, i: (h, i, 0))],
        out_shape=[jax.ShapeDtypeStruct((S, NH * HP), F32), jax.ShapeDtypeStruct((NH, S, 1), F32)],
        compiler_params=_cparams(("parallel", "arbitrary")))(qp, kp, kv, o, do, lse)


def _flash_bwd_dkv(name, qp, kp, kv, do, lse_row, delta_row):
    S = qp.shape[0]
    nq = S // TQ

    def body(q_ref, k_ref, v_ref, do_ref, lse_ref, dl_ref, dk_ref, dv_ref):
        kj = pl.program_id(1)
        subs = [(k_ref[u * TQ:(u + 1) * TQ, :], v_ref[u * TQ:(u + 1) * TQ, :].astype(BF16)) for u in range(2)]

        def step(sub, i, carry, masked):
            k, vb = sub
            dk, dv = carry
            rows = pl.ds(pl.multiple_of(i * TQ, TQ), TQ)
            q = q_ref[rows, :]
            dob = do_ref[rows, :].astype(BF16)
            st = _dot_nt(k, q) * SM_SCALE
            pt = jnp.exp(st - lse_ref[0, :, rows])
            if masked:
                pt = jnp.where(_diag_mask(True), pt, 0.0)
            dv = dv + jnp.dot(pt.astype(BF16), dob, preferred_element_type=F32)
            dpt = _dot_nt(vb, dob)
            dst = pt * (dpt - dl_ref[0, :, rows]) * SM_SCALE
            dk = dk + jnp.dot(dst.astype(BF16), q, preferred_element_type=F32)
            return dk, dv

        zero = (jnp.zeros((TQ, HP), F32), jnp.zeros((TQ, DH), F32))
        ca = step(subs[0], 2 * kj + 1, step(subs[0], 2 * kj, zero, True), False)
        cb = step(subs[1], 2 * kj + 1, zero, True)
        ca, cb = lax.fori_loop(2 * kj + 2, nq, lambda i, c: (step(subs[0], i, c[0], False), step(subs[1], i, c[1], False)),
                               (ca, cb))
        for u, (dk, dv) in enumerate((ca, cb)):
            dk_ref[u * TQ:(u + 1) * TQ, :] = dk
            dv_ref[u * TQ:(u + 1) * TQ, :] = dv

    return pl.pallas_call(
        body, name=name, grid=(NH, nq // 2),
        in_specs=[pl.BlockSpec((S, HP), lambda h, j: (0, h)), pl.BlockSpec((2 * TQ, HP), lambda h, j: (j, h)),
                  pl.BlockSpec((2 * TQ, DH), lambda h, j: (j, NH + h)), pl.BlockSpec((S, DH), lambda h, j: (0, h)),
                  pl.BlockSpec((1, 1, S), lambda h, j: (h, 0, 0)), pl.BlockSpec((1, 1, S), lambda h, j: (h, 0, 0))],
        out_specs=[pl.BlockSpec((2 * TQ, HP), lambda h, j: (j, h)), pl.BlockSpec((2 * TQ, DH), lambda h, j: (j, h))],
        out_shape=[jax.ShapeDtypeStruct((S, NH * HP), F32), jax.ShapeDtypeStruct((S, NH * DH), F32)],
        compiler_params=_cparams(("parallel", "arbitrary")))(qp, kp, kv, do, lse_row, delta_row)


def _tm(S, width):
    t = 512 if width <= 1024 else (256 if width <= 3072 else 128)
    return min(t, S)


TM_LIGHT = 1024


def _mod_fwd(tag, x, g, shift, scale):
    S = x.shape[0]
    return _rw_fwd(tag + "_mod", f_mod, [Row(x)], [g, shift, scale], [((D,), BF16)], min(TM_LIGHT, S))[0]


def _mod_bwd(tag, x, g, shift, scale, dh, dx_direct):
    S = x.shape[0]
    r = _rw_bwd(tag + "_mod_b", f_mod, [Row(x)], [g, shift, scale], [Row(dh)], [True], [True] * 3, [((D,), F32)],
                _tm(S, D), add=Row(dx_direct))
    return r[0], r[1:]


def _res_fwd(tag, x, y, gate, coef):
    S = x.shape[0]

    def fn(pieces, bvals):
        return [pieces[0] + coef * bvals[0] * pieces[1]], []
    return _rowwise(tag + "_res", fn, [Row(x), Row(y)], [gate], [((D,), F32)], [], min(TM_LIGHT, S))[0]


def _res_bwd(tag, y, gate, dxn, coef):
    S = y.shape[0]
    r = _rw_bwd(tag + "_res_b", make_f_res(coef), [Row(y)], [gate], [Row(dxn)], [True], [True], [((D,), BF16)],
                min(TM_LIGHT, S))
    return r[0], r[1]


def _ffn_fwd(tag, x, mod3, g, w_in4, w_out4, li):
    shift, scale, gate = mod3
    S = x.shape[0]
    h = _mod_fwd(tag, x, g, shift, scale)
    gu = _matmul(tag + "_in", h, w_in4, lay="b_cols", li=li, out_dtype=BF16)
    a = _rw_fwd(tag + "_act", f_act, [Row(gu, splits=[FF, FF])], [], [((FF,), BF16)], _tm(S, FF))[0]
    y = _matmul(tag + "_out", a, w_out4, lay="b_rows", li=li)
    xn = _res_fwd(tag, x, y, gate, 0.5)
    return xn, (x, h, gu, a, y)


def _ffn_bwd(tag, dxn, res, mod3, g, w_in4, w_out4, li, g_in4, g_out4):
    shift, scale, gate = mod3
    x, h, gu, a, y = res
    S = x.shape[0]
    nmat = w_in4.shape[1]
    dy, dgate = _res_bwd(tag, y, gate, dxn, 0.5)
    da = _matmul(tag + "_out_bi", dy, w_out4, "nt", lay="b_rows", li=li, out_dtype=BF16)
    g_out4 = _matmul(tag + "_out_bw", a, dy, "tn", lay="o_rows", li=li, into=g_out4, nmat=nmat, out_dtype=BF16)
    dgu = _rw_bwd(tag + "_act_b", f_act, [Row(gu, splits=[FF, FF])], [], [Row(da)], [True, True], [],
                  [((FF, FF), BF16)], _tm(S, FF))[0]
    dh = _matmul(tag + "_in_bi", dgu, w_in4, "nt", lay="b_cols", li=li)
    g_in4 = _matmul(tag + "_in_bw", h, dgu, "tn", lay="o_cols", li=li, into=g_in4, nmat=nmat, out_dtype=BF16)
    dx, (dg, dshift, dscale) = _mod_bwd(tag, x, g, shift, scale, dh, dxn)
    return dx, g_in4, g_out4, dict(g=dg, mod=(dshift, dscale, dgate))


def _pad_lanes(a, lo, width=LANE):
    return jnp.pad(a, ((0, 0), (lo, width - lo - a.shape[1])))


def _gdn_layer_fwd(tag, x, mod3, g, p):
    shift, scale, gate = mod3
    S = x.shape[0]
    NC = S // CHUNK
    h = _mod_fwd(tag, x, g, shift, scale)
    proj = _matmul(tag + "_in", h, p["w_in"])
    qc = _conv_fwd(tag + "_conv", proj, p["conv_w8"], 3 * D, _tm(S, 3 * D))
    q, k, v = _rw_fwd(tag + "_pre", f_gdnpre, [Row(qc, splits=[DH] * (3 * NH))], [],
                      [((DH,) * NH, F32)] * 3, _tm(S, 3 * D))
    betaf, gf = _rw_fwd(tag + "_gates", f_gates, [Row(proj, LANE, cb=GATE_CB)], [p["a_log128"], p["dt_bias128"]],
                        [((LANE,), F32)] * 2, _tm(S, LANE))
    bcol, gcol = betaf[:, :NH], gf[:, NH:2 * NH]
    grow = gcol.reshape(NC, CHUNK, NH).transpose(0, 2, 1)
    o, ssave, tsave = _gdn_fwd(tag + "_core", q, k, v, gcol, grow, bcol)
    on = _rw_fwd(tag + "_post", f_gdnpost, [Row(o, splits=[DH] * NH), Row(proj, D, cb=3, splits=[DH] * NH)],
                 [p["norm_g"]], [((DH,) * NH, BF16)], _tm(S, 2 * D))[0]
    y = _matmul(tag + "_out", on, p["w_out"])
    xn = _res_fwd(tag, x, y, gate, 1.0)
    return xn, (x, h, proj, qc, q, k, v, gcol, grow, bcol, ssave, tsave, o, on, y)


def _gdn_layer_bwd(tag, dxn, res, mod3, g, p):
    shift, scale, gate = mod3
    x, h, proj, qc, q, k, v, gcol, grow, bcol, ssave, tsave, o, on, y = res
    S = x.shape[0]
    dy, dgate = _res_bwd(tag, y, gate, dxn, 1.0)
    don = _matmul(tag + "_out_bi", dy, p["w_out"], "nt")
    dw_out = _matmul(tag + "_out_bw", on, dy, "tn")
    do, dz, dnorm = _rw_bwd(tag + "_post_b", f_gdnpost, [Row(o, splits=[DH] * NH), Row(proj, D, cb=3, splits=[DH] * NH)],
                            [p["norm_g"]], [Row(don, splits=[DH] * NH)], [True] * (2 * NH), [True],
                            [((DH,) * NH, F32), ((DH,) * NH, BF16)], _tm(S, 2 * D))
    dq, dk, dv, dgc, dgr, db = _gdn_bwd(tag + "_core_b", q, k, v, gcol, grow, bcol, ssave, tsave, do)
    dgcol = dgc + dgr.transpose(0, 2, 1).reshape(S, NH)
    dgates, da_log, ddt = _rw_bwd(tag + "_gates_b", f_gates, [Row(proj, LANE, cb=GATE_CB)], [p["a_log128"], p["dt_bias128"]],
                                  [Row(_pad_lanes(db, 0)), Row(_pad_lanes(dgcol, NH))], [True], [True, True],
                                  [((LANE,), BF16)], _tm(S, LANE))
    dqc = _rw_bwd(tag + "_pre_b", f_gdnpre, [Row(qc, splits=[DH] * (3 * NH))], [],
                  [Row(dq, splits=[DH] * NH), Row(dk, splits=[DH] * NH), Row(dv, splits=[DH] * NH)],
                  [True] * (3 * NH), [], [((DH,) * (3 * NH), F32)], _tm(S, 3 * D))[0]
    dqkv, dconv = _conv_bwd(tag + "_conv_b", proj, dqc, p["conv_w8"], 3 * D, _tm(S, 3 * D), BF16)
    dproj = jnp.concatenate([dqkv, dz, dgates], axis=1)
    dh = _matmul(tag + "_in_bi", dproj, p["w_in"], "nt")
    dw_in = _matmul(tag + "_in_bw", h, dproj, "tn")
    dx, (dg, dshift, dscale) = _mod_bwd(tag, x, g, shift, scale, dh, dxn)
    return dx, dict(w_in=dw_in, conv_w8=dconv, a_log128=da_log, dt_bias128=ddt, norm_g=dnorm,
                    w_out=dw_out, g=dg, mod=(dshift, dscale, dgate))


def _qk_rows(src, shared_rope, ckv=None):
    if shared_rope:
        return [Row(src, D, cb=0, splits=[DH] * NH), Row(ckv, LANE, cb=2)]
    return [Row(src, splits=[DH] * (2 * NH))]


def _kv_fwd(x, kvmod, p, tabs):
    shift, scale = kvmod
    S = x.shape[0]
    h = _mod_fwd("kv", x, p["kv_norm_g"], shift, scale)
    ckv = _matmul("kv_dkv", h, p["w_dkv"])
    lat = _rw_fwd("kv_lat", f_rms, [Row(ckv, KVL)], [p["kv_lat_g"]], [((KVL,), BF16)], _tm(S, KVL))[0]
    kvf = _matmul("kv_ukv", lat, p["w_ukv"])
    kp = _rw_fwd("kv_k", make_f_qk(True), _qk_rows(kvf, True, ckv) + [Row(tabs[0]), Row(tabs[1])],
                 [p["k_gn"], p["k_gr"], p["pm"]], [((DH,) * (2 * NH), BF16)], _tm(S, 2 * D))[0]
    return kp, kvf, (x, h, ckv, lat)


def _kv_bwd(dkp, dv, dx_direct, res, kvmod, kvf, p, tabs):
    shift, scale = kvmod
    x, h, ckv, lat = res
    S = x.shape[0]
    dkn, dkr, dgn, dgr = _rw_bwd("kv_k_b", make_f_qk(True), _qk_rows(kvf, True, ckv) + [Row(tabs[0]), Row(tabs[1])],
                                 [p["k_gn"], p["k_gr"], p["pm"]], [Row(dkp, splits=[DH] * (2 * NH))],
                                 [True] * (NH + 1) + [False, False], [True, True, False],
                                 [((DH,) * NH, BF16), ((LANE,), BF16)], _tm(S, 2 * D))
    dkvf = jnp.concatenate([dkn, dv.astype(BF16)], axis=1)
    dlat = _matmul("kv_ukv_bi", dkvf, p["w_ukv"], "nt")
    dw_ukv = _matmul("kv_ukv_bw", lat, dkvf, "tn")
    dcl, dlg = _rw_bwd("kv_lat_b", f_rms, [Row(ckv, KVL)], [p["kv_lat_g"]], [Row(dlat)], [True], [True],
                       [((KVL,), BF16)], _tm(S, KVL))
    dckv = jnp.concatenate([dcl, dkr], axis=1)
    dh = _matmul("kv_dkv_bi", dckv, p["w_dkv"], "nt")
    dw_dkv = _matmul("kv_dkv_bw", h, dckv, "tn")
    dx, (dg, dshift, dscale) = _mod_bwd("kv", x, p["kv_norm_g"], shift, scale, dh, dx_direct)
    return dx, dict(w_dkv=dw_dkv, w_ukv=dw_ukv, kv_lat_g=dlg, k_gn=dgn, k_gr=dgr, kv_norm_g=dg, mod=(dshift, dscale))


def _mla_layer_fwd(tag, x, mod3, g, p, kp, kvf, tabs):
    shift, scale, gate = mod3
    S = x.shape[0]
    h = _mod_fwd(tag, x, g, shift, scale)
    ql = _matmul(tag + "_dq", h, p["w_dq"])
    qln = _rw_fwd(tag + "_qln", f_rms, [Row(ql)], [p["ql_g"]], [((QL,), BF16)], _tm(S, QL))[0]
    qu = _matmul(tag + "_uq", qln, p["w_uq"])
    qp = _rw_fwd(tag + "_q", make_f_qk(False), _qk_rows(qu, False) + [Row(tabs[0]), Row(tabs[1])],
                 [p["q_gn"], p["q_gr"], p["pm"]], [((DH,) * (2 * NH), BF16)], _tm(S, 2 * D))[0]
    o, lse = _flash_fwd(tag + "_att", qp, kp, kvf)
    y = _matmul(tag + "_out", o, p["w_out"])
    xn = _res_fwd(tag, x, y, gate, 1.0)
    return xn, (x, h, ql, qln, qu, qp, o, lse, y)


def _mla_layer_bwd(tag, dxn, res, mod3, g, p, kp, kvf, tabs):
    shift, scale, gate = mod3
    x, h, ql, qln, qu, qp, o, lse, y = res
    S = x.shape[0]
    dy, dgate = _res_bwd(tag, y, gate, dxn, 1.0)
    do = _matmul(tag + "_out_bi", dy, p["w_out"], "nt")
    dw_out = _matmul(tag + "_out_bw", o, dy, "tn")
    dqp, delta = _flash_bwd_dq(tag + "_att_bq", qp, kp, kvf, o, do, lse)
    dkp, dv = _flash_bwd_dkv(tag + "_att_bkv", qp, kp, kvf, do, lse.reshape(NH, 1, S), delta.reshape(NH, 1, S))
    dqu, dgn, dgr = _rw_bwd(tag + "_q_b", make_f_qk(False), _qk_rows(qu, False) + [Row(tabs[0]), Row(tabs[1])],
                            [p["q_gn"], p["q_gr"], p["pm"]], [Row(dqp, splits=[DH] * (2 * NH))],
                            [True] * (2 * NH) + [False, False], [True, True, False],
                            [((DH,) * (2 * NH), BF16)], _tm(S, 2 * D))
    dqln = _matmul(tag + "_uq_bi", dqu, p["w_uq"], "nt")
    dw_uq = _matmul(tag + "_uq_bw", qln, dqu, "tn")
    dql, dqlg = _rw_bwd(tag + "_qln_b", f_rms, [Row(ql)], [p["ql_g"]], [Row(dqln)], [True], [True], [((QL,), BF16)],
                        _tm(S, QL))
    dh = _matmul(tag + "_dq_bi", dql, p["w_dq"], "nt")
    dw_dq = _matmul(tag + "_dq_bw", h, dql, "tn")
    dx, (dg, dshift, dscale) = _mod_bwd(tag, x, g, shift, scale, dh, dxn)
    return dx, dkp, dv, dict(w_dq=dw_dq, w_uq=dw_uq, w_out=dw_out, ql_g=dqlg, q_gn=dgn, q_gr=dgr, g=dg,
                             mod=(dshift, dscale, dgate))


def _loss_head(y, tgt):
    S = y.shape[0]

    def fn(pieces, bvals):
        e = pieces[0] - pieces[1]
        part = jnp.sum(e * e) * (0.5 / D)
        return [e * (1.0 / D)], [jnp.full((1, LANE), part, F32)]
    dy, part = _rowwise("loss", fn, [Row(y), Row(tgt)], [], [((D,), F32)], [(1, LANE)], _tm(S, D))
    return part[0, 0], dy


def _rope_tables(positions):
    S = positions.shape[0]
    half = ROPE // 2
    lane = lax.broadcasted_iota(jnp.int32, (1, LANE), 1)
    inv_freq = ROPE_BASE ** (-(lane % half).astype(F32) / half)
    live = (lane < ROPE).astype(F32)
    sign = jnp.where(lane < half, -1.0, 1.0) * live

    def fn(pieces, bvals):
        ang = pieces[0] * bvals[0]
        return [jnp.cos(ang) * bvals[1], jnp.sin(ang) * bvals[2]], []
    pos = jnp.broadcast_to(positions.astype(F32)[:, None], (S, LANE))
    cosp, sins = _rowwise("rope_tab", fn, [Row(pos)], [inv_freq, live, sign], [((LANE,), F32)] * 2, [], _tm(S, LANE))
    r = lax.broadcasted_iota(jnp.int32, (LANE, LANE), 0)
    c = lax.broadcasted_iota(jnp.int32, (LANE, LANE), 1)
    pm = (((c < half) & (r == c + half)) | ((c >= half) & (c < ROPE) & (r == c - half))).astype(F32)
    return (cosp, sins), pm


def _adamw(name, w, g, m, v):
    shape = w.shape
    C = shape[-1]
    R = w.size // C
    tr = R
    for t in (1024, 512, 256, 128, 64, 32, 16, 8):
        if R % t == 0 and t * C * 4 <= (1 << 21):
            tr = t
            break
    c1 = 1.0 - ADAM_B1 ** ADAM_STEP
    c2 = 1.0 - ADAM_B2 ** ADAM_STEP

    def body(w_ref, g_ref, m_ref, v_ref, d_ref, mo_ref, vo_ref):
        gg = g_ref[...]
        mn = ADAM_B1 * m_ref[...] + (1.0 - ADAM_B1) * gg
        vn = ADAM_B2 * v_ref[...] + (1.0 - ADAM_B2) * (gg * gg)
        d_ref[...] = -ADAM_LR * ((mn / c1) / (jnp.sqrt(vn / c2) + ADAM_EPS) + ADAM_WD * w_ref[...])
        mo_ref[...] = mn
        vo_ref[...] = vn

    spec = pl.BlockSpec((tr, C), lambda i: (i, 0))
    outs = pl.pallas_call(body, name=name, grid=(R // tr,), in_specs=[spec] * 4, out_specs=[spec] * 3,
                          out_shape=[jax.ShapeDtypeStruct((R, C), F32)] * 3,
                          compiler_params=_cparams(("parallel",)))(*[t.reshape(R, C) for t in (w, g, m, v)])
    return [o.reshape(shape) for o in outs]


HBM_SPEC = pl.BlockSpec(memory_space=pltpu.HBM)
OTHER_CHIPS = (4, 2, 6)
SIBLING = 1


def _me():
    return lax.axis_index("x"), lax.axis_index("y"), lax.axis_index("c")


def _peer(me, k):
    mx, my, mc = me
    return ((1 - mx) if k & 4 else mx, (1 - my) if k & 2 else my, (1 - mc) if k & 1 else mc)


def _rcopy(src, dst, ssem, rsem, to):
    return pltpu.make_async_remote_copy(src_ref=src, dst_ref=dst, send_sem=ssem, recv_sem=rsem, device_id=to,
                                        device_id_type=MESH)


def _all_gather8(name, x):
    def body(x_ref, o_ref, ssem, rsem, lsem):
        me = _me()
        mine = 4 * me[0] + 2 * me[1] + me[2]
        loc = pltpu.make_async_copy(x_ref, o_ref.at[mine], lsem)
        loc.start()
        sends = []
        for k in range(1, 8):
            cp = _rcopy(x_ref, o_ref.at[mine], ssem.at[k - 1], rsem.at[k - 1], _peer(me, k))
            cp.start()
            sends.append(cp)
        for k in range(1, 8):
            px, py, pc = _peer(me, k)
            _rcopy(x_ref, o_ref.at[4 * px + 2 * py + pc], ssem.at[k - 1], rsem.at[k - 1], (px, py, pc)).wait_recv()
        for cp in sends:
            cp.wait_send()
        loc.wait()

    return pl.pallas_call(body, name=name, out_shape=jax.ShapeDtypeStruct((8,) + x.shape, x.dtype),
                          in_specs=[HBM_SPEC], out_specs=HBM_SPEC,
                          scratch_shapes=[pltpu.SemaphoreType.DMA((7,)), pltpu.SemaphoreType.DMA((7,)),
                                          pltpu.SemaphoreType.DMA(())])(x)


PACK_L = 1024
PACK_RT = 256


def _place_shard(name, wp, chip):
    rh, ln = wp.shape[1:]

    def body(s_ref, w_ref, o_ref):
        o_ref[...] = w_ref[...]

    gs = pltpu.PrefetchScalarGridSpec(
        num_scalar_prefetch=1, grid=(2, rh // PACK_RT),
        in_specs=[pl.BlockSpec((None, PACK_RT, ln), lambda h, i, s_ref: (h, i, 0))],
        out_specs=pl.BlockSpec((None, None, PACK_RT, ln), lambda h, i, s_ref: (s_ref[0], h, i, 0)))
    return pl.pallas_call(body, name=name, grid_spec=gs, out_shape=jax.ShapeDtypeStruct((4,) + wp.shape, wp.dtype),
                          compiler_params=_cparams(("parallel", "parallel")))(chip.reshape(1).astype(jnp.int32), wp)


def _gather_weights(name, w4):
    r2 = w4.shape[2] // 2

    def body(w_ref, o_ref, ssem, rsem):
        me = _me()
        mc = me[2]
        px, py, pd, sib = _peer(me, 4), _peer(me, 2), _peer(me, 6), _peer(me, SIBLING)
        chip = lambda p: 2 * p[0] + p[1]
        mine, from_x, from_y, from_d = (o_ref.at[chip(p), mc] for p in (me, px, py, pd))
        q0, q1 = pl.ds(0, r2), pl.ds(r2, r2)
        sends = [_rcopy(mine, mine, ssem.at[0], rsem.at[0], px), _rcopy(mine, mine, ssem.at[1], rsem.at[1], py)]
        for cp in sends:
            cp.start()
        _rcopy(from_x, from_x, ssem.at[0], rsem.at[0], px).wait_recv()
        sends += [_rcopy(from_x.at[q1], from_x.at[q1], ssem.at[2], rsem.at[2], py),
                  _rcopy(from_x, from_x, ssem.at[4], rsem.at[4], sib)]
        sends[-2].start()
        sends[-1].start()
        _rcopy(from_y, from_y, ssem.at[1], rsem.at[1], py).wait_recv()
        sends += [_rcopy(from_y.at[q0], from_y.at[q0], ssem.at[3], rsem.at[3], px),
                  _rcopy(from_y, from_y, ssem.at[5], rsem.at[5], sib)]
        sends[-2].start()
        sends[-1].start()
        _rcopy(from_d.at[q0], from_d.at[q0], ssem.at[3], rsem.at[3], px).wait_recv()
        _rcopy(from_d.at[q1], from_d.at[q1], ssem.at[2], rsem.at[2], py).wait_recv()
        sends.append(_rcopy(from_d, from_d, ssem.at[6], rsem.at[6], sib))
        sends[-1].start()
        for j, p in enumerate((px, py, pd)):
            land = o_ref.at[chip(p), 1 - mc]
            _rcopy(land, land, ssem.at[4 + j], rsem.at[4 + j], sib).wait_recv()
        for cp in sends:
            cp.wait_send()

    return pl.pallas_call(body, name=name, out_shape=jax.ShapeDtypeStruct(w4.shape, w4.dtype),
                          in_specs=[HBM_SPEC], out_specs=HBM_SPEC, input_output_aliases={0: 0},
                          scratch_shapes=[pltpu.SemaphoreType.DMA((7,)), pltpu.SemaphoreType.DMA((7,))])(w4)


def _exchange_half(name, g):
    def body(g_ref, p_ref, ssem, rsem):
        me = _me()
        cps = []
        for s in range(4):
            cp = _rcopy(g_ref.at[s, 1 - me[2]], p_ref.at[s], ssem.at[s], rsem.at[s], _peer(me, SIBLING))
            cp.start()
            cps.append(cp)
        for cp in cps:
            cp.wait()

    return pl.pallas_call(body, name=name, out_shape=jax.ShapeDtypeStruct((4,) + g.shape[2:], g.dtype),
                          in_specs=[HBM_SPEC], out_specs=HBM_SPEC,
                          scratch_shapes=[pltpu.SemaphoreType.DMA((4,)), pltpu.SemaphoreType.DMA((4,))])(g)


def _scatter_chips(name, q):
    r2 = q.shape[1] // 2

    def body(q_ref, t_ref, relay, ssem, rsem):
        me = _me()
        px, py, pd = _peer(me, 4), _peer(me, 2), _peer(me, 6)
        chip = lambda p: 2 * p[0] + p[1]
        q0, q1 = pl.ds(0, r2), pl.ds(r2, r2)
        sends = [_rcopy(q_ref.at[chip(px)], t_ref.at[0], ssem.at[0], rsem.at[0], px),
                 _rcopy(q_ref.at[chip(py)], t_ref.at[1], ssem.at[1], rsem.at[1], py),
                 _rcopy(q_ref.at[chip(pd), q0], relay.at[0], ssem.at[2], rsem.at[2], py),
                 _rcopy(q_ref.at[chip(pd), q1], relay.at[1], ssem.at[3], rsem.at[3], px)]
        for cp in sends:
            cp.start()
        _rcopy(relay.at[0], relay.at[0], ssem.at[2], rsem.at[2], py).wait_recv()
        sends.append(_rcopy(relay.at[0], t_ref.at[2, q0], ssem.at[4], rsem.at[4], px))
        sends[-1].start()
        _rcopy(relay.at[1], relay.at[1], ssem.at[3], rsem.at[3], px).wait_recv()
        sends.append(_rcopy(relay.at[1], t_ref.at[2, q1], ssem.at[5], rsem.at[5], py))
        sends[-1].start()
        _rcopy(t_ref.at[0], t_ref.at[0], ssem.at[0], rsem.at[0], px).wait_recv()
        _rcopy(t_ref.at[1], t_ref.at[1], ssem.at[1], rsem.at[1], py).wait_recv()
        _rcopy(t_ref.at[2, q0], t_ref.at[2, q0], ssem.at[4], rsem.at[4], px).wait_recv()
        _rcopy(t_ref.at[2, q1], t_ref.at[2, q1], ssem.at[5], rsem.at[5], py).wait_recv()
        for cp in sends:
            cp.wait_send()

    return pl.pallas_call(body, name=name,
                          out_shape=[jax.ShapeDtypeStruct((3,) + q.shape[1:], q.dtype),
                                     jax.ShapeDtypeStruct((2, r2) + q.shape[2:], q.dtype)],
                          in_specs=[HBM_SPEC], out_specs=[HBM_SPEC, HBM_SPEC],
                          scratch_shapes=[pltpu.SemaphoreType.DMA((6,)), pltpu.SemaphoreType.DMA((6,))])(q)[0]


def _exchange_full(name, r2):
    def body(r_ref, o_ref, ssem, rsem):
        me = _me()
        mc = me[2]
        cp = _rcopy(o_ref.at[mc], o_ref.at[mc], ssem, rsem, _peer(me, SIBLING))
        cp.start()
        _rcopy(o_ref.at[1 - mc], o_ref.at[1 - mc], ssem, rsem, _peer(me, SIBLING)).wait_recv()
        cp.wait_send()

    return pl.pallas_call(body, name=name, out_shape=jax.ShapeDtypeStruct(r2.shape, r2.dtype),
                          in_specs=[HBM_SPEC], out_specs=HBM_SPEC, input_output_aliases={0: 0},
                          scratch_shapes=[pltpu.SemaphoreType.DMA(()), pltpu.SemaphoreType.DMA(())])(r2)


def _add_half(name, g, p, c):
    rh, ln = g.shape[2:]

    def body(c_ref, g_ref, p_ref, o_ref):
        o_ref[0] = (g_ref[0, 0].astype(F32) + p_ref[0].astype(F32)).astype(o_ref.dtype)

    gs = pltpu.PrefetchScalarGridSpec(
        num_scalar_prefetch=1, grid=(4, rh // PACK_RT),
        in_specs=[pl.BlockSpec((1, 1, PACK_RT, ln), lambda s, i, c_ref: (s, c_ref[0], i, 0)),
                  pl.BlockSpec((1, PACK_RT, ln), lambda s, i, c_ref: (s, i, 0))],
        out_specs=pl.BlockSpec((1, PACK_RT, ln), lambda s, i, c_ref: (s, i, 0)))
    return pl.pallas_call(body, name=name, grid_spec=gs, out_shape=jax.ShapeDtypeStruct((4, rh, ln), BF16),
                          compiler_params=_cparams(("parallel", "parallel")))(c.reshape(1).astype(jnp.int32), g, p)


def _add_chips(name, q, t, chip, c):
    rh, ln = q.shape[1:]

    def body(s_ref, c_ref, q_ref, t_ref, o_ref):
        o_ref[...] = ((q_ref[0].astype(F32) + t_ref[0].astype(F32)) + t_ref[1].astype(F32)) + t_ref[2].astype(F32)

    gs = pltpu.PrefetchScalarGridSpec(
        num_scalar_prefetch=2, grid=(rh // PACK_RT,),
        in_specs=[pl.BlockSpec((1, PACK_RT, ln), lambda i, s_ref, c_ref: (s_ref[0], i, 0)),
                  pl.BlockSpec((3, PACK_RT, ln), lambda i, s_ref, c_ref: (0, i, 0))],
        out_specs=pl.BlockSpec((None, PACK_RT, ln), lambda i, s_ref, c_ref: (c_ref[0], i, 0)))
    return pl.pallas_call(body, name=name, grid_spec=gs, out_shape=jax.ShapeDtypeStruct((2, rh, ln), F32),
                          compiler_params=_cparams(("parallel",)))(chip.reshape(1).astype(jnp.int32),
                                                                    c.reshape(1).astype(jnp.int32), q, t)


def _sum8(name, a):
    def body(a_ref, o_ref):
        acc = a_ref[0]
        for d in range(1, 8):
            acc = acc + a_ref[d]
        o_ref[...] = acc
    return pl.pallas_call(body, name=name, out_shape=jax.ShapeDtypeStruct(a.shape[1:], F32))(a)


def _silu_rows(name, a):
    def body(a_ref, o_ref):
        o_ref[...] = _silu(a_ref[...])
    return pl.pallas_call(body, name=name, out_shape=jax.ShapeDtypeStruct(a.shape, F32))(a)


REST = (("gdn_w_out", 1), ("mla_w_dkv", 0), ("mla_w_ukv", 1), ("mla_w_dq", 1), ("mla_w_uq", 2), ("mla_w_out", 1))


def _packed_rows(n):
    per_half = -(-n // (2 * PACK_L))
    return -(-per_half // PACK_RT) * PACK_RT


def _pack_flat(flat):
    n = flat.shape[-1]
    rh = _packed_rows(n)
    pad = [(0, 0)] * (flat.ndim - 1) + [(0, 2 * rh * PACK_L - n)]
    return jnp.pad(flat, pad).reshape(flat.shape[:-1] + (2, rh, PACK_L))


def _shards_first(full, axis):
    sh = full.shape
    t = full.reshape(sh[:axis] + (4, sh[axis] // 4) + sh[axis + 1:])
    return jnp.moveaxis(t, axis, 0)


def _shards_merge(stacked, axis):
    t = jnp.moveaxis(stacked, 0, axis)
    sh = t.shape
    return t.reshape(sh[:axis] + (4 * sh[axis + 1],) + sh[axis + 2:])


def _pack_small(parts):
    flat = jnp.concatenate([p.reshape(-1).astype(F32) for p in parts])
    n = flat.shape[0]
    rows = -(-n // (SUB * LANE)) * SUB
    return jnp.pad(flat, (0, rows * LANE - n)).reshape(rows, LANE)


def _unpack_small(buf, shapes):
    lead = buf.shape[:-2]
    flat = buf.reshape(lead + (-1,))
    out, off = [], 0
    for sh in shapes:
        n = 1
        for d in sh:
            n *= d
        out.append(flat[..., off:off + n].reshape(lead + tuple(sh)))
        off += n
    return out


WEIGHTS = ('ada_w', 'ada_b', 'norm_g', 'ffn_w_in', 'ffn_w_out', 'gdn_w_in', 'gdn_conv_w', 'gdn_a_log', 'gdn_dt_bias',
           'gdn_norm_g', 'gdn_w_out', 'kv_ada_w', 'kv_ada_b', 'kv_norm_g', 'mla_w_dkv', 'mla_kv_norm_g', 'mla_w_ukv',
           'mla_k_norm_g', 'mla_w_dq', 'mla_q_lora_norm_g', 'mla_w_uq', 'mla_q_norm_g', 'mla_w_out')
ARGS = ('x', 'c', 'positions') + WEIGHTS + ('loss_target',) + tuple('m_' + n for n in WEIGHTS) + tuple('v_' + n for n in WEIGHTS)


def _split_norm(v):
    return v[None, :DH], _pad_lanes(v[None, DH:], 0)


def _join_norm(gn, gr):
    return jnp.concatenate([gn[0], gr[0, :ROPE]])


def _step(x, tgt, pos, mods, kvmod, W, P):
    tabs, pm = _rope_tables(pos)
    m3 = lambda l, i: tuple(mods[l][3 * i + j][None] for j in range(3))
    ng = lambda l, i: P["norm_g"][l, i][None]
    gdn_p, mla_p = [], []
    for l in range(2):
        gdn_p.append(dict(w_in=jnp.pad(W["gdn_w_in"][l], ((0, 0), (0, GDN_IN - W["gdn_w_in"].shape[2]))),
                          conv_w8=jnp.pad(P["gdn_conv_w"][l], ((0, 4), (0, 0))),
                          a_log128=_pad_lanes(P["gdn_a_log"][l][None], NH), dt_bias128=_pad_lanes(P["gdn_dt_bias"][l][None], NH),
                          norm_g=P["gdn_norm_g"][l][None], w_out=W["gdn_w_out"][l]))
        q_gn, q_gr = _split_norm(P["mla_q_norm_g"][l])
        mla_p.append(dict(w_dq=W["mla_w_dq"][l], ql_g=P["mla_q_lora_norm_g"][l][None],
                          w_uq=jnp.pad(W["mla_w_uq"][l].reshape(QL, NH, QKH), ((0, 0), (0, 0), (0, HP - QKH))).reshape(QL, NH * HP),
                          q_gn=q_gn, q_gr=q_gr, pm=pm, w_out=W["mla_w_out"][l]))
    k_gn, k_gr = _split_norm(P["mla_k_norm_g"])
    kv_p = dict(kv_norm_g=P["kv_norm_g"][None], w_dkv=jnp.pad(W["mla_w_dkv"], ((0, 0), (0, QL - KVL - ROPE))),
                kv_lat_g=P["mla_kv_norm_g"][None],
                w_ukv=W["mla_w_ukv"].reshape(KVL, NH, 2, DH).transpose(0, 2, 1, 3).reshape(KVL, 2 * NH * DH),
                k_gn=k_gn, k_gr=k_gr, pm=pm)
    kvm = (kvmod[0][None], kvmod[1][None])

    res = {}
    for l in range(4):
        x, res[l, 0] = _ffn_fwd(f"l{l}a", x, m3(l, 0), ng(l, 0), W["ffn_w_in"], W["ffn_w_out"], 2 * l)
        if l < 2:
            x, res[l, 1] = _gdn_layer_fwd(f"l{l}g", x, m3(l, 1), ng(l, 1), gdn_p[l])
        else:
            x, res[l, 1] = _mla_layer_fwd(f"l{l}m", x, m3(l, 1), ng(l, 1), mla_p[l - 2], kp, kvf, tabs)
        x, res[l, 2] = _ffn_fwd(f"l{l}b", x, m3(l, 2), ng(l, 2), W["ffn_w_in"], W["ffn_w_out"], 2 * l + 1)
        if l == 1:
            kp, kvf, kres = _kv_fwd(x, kvm, kv_p, tabs)
    loss, dx = _loss_head(x, tgt)

    gw = {n: [None] * W[n].shape[0] for n in ("gdn_w_in", "gdn_w_out", "mla_w_dq", "mla_w_uq", "mla_w_out")}
    g_in4 = g_out4 = None
    gp = {n: [None] * 2 for n in ("gdn_conv_w", "gdn_a_log", "gdn_dt_bias", "gdn_norm_g", "mla_q_lora_norm_g", "mla_q_norm_g")}
    gnorm = [[None] * 3 for _ in range(4)]
    dmod = [[None] * NMOD for _ in range(4)]
    dkp = dv = None
    for l in (3, 2, 1, 0):
        if l == 1:
            dx, gk = _kv_bwd(dkp, dv, dx, kres, kvm, kvf, kv_p, tabs)
        for i in (2, 1, 0):
            if i != 1:
                dx, g_in4, g_out4, gd = _ffn_bwd(f"l{l}{'ab'[i // 2]}", dx, res[l, i], m3(l, i), ng(l, i), W["ffn_w_in"],
                                                 W["ffn_w_out"], 2 * l + i // 2, g_in4, g_out4)
            elif l < 2:
                dx, gd = _gdn_layer_bwd(f"l{l}g", dx, res[l, 1], m3(l, 1), ng(l, 1), gdn_p[l])
                gw["gdn_w_in"][l] = gd["w_in"][:, :W["gdn_w_in"].shape[2]]
                gw["gdn_w_out"][l] = gd["w_out"]
                gp["gdn_conv_w"][l] = gd["conv_w8"][:4]
                gp["gdn_a_log"][l] = gd["a_log128"][0, NH:2 * NH]
                gp["gdn_dt_bias"][l] = gd["dt_bias128"][0, NH:2 * NH]
                gp["gdn_norm_g"][l] = gd["norm_g"][0]
            else:
                dx, dkp_l, dv_l, gd = _mla_layer_bwd(f"l{l}m", dx, res[l, 1], m3(l, 1), ng(l, 1), mla_p[l - 2], kp, kvf, tabs)
                dkp = dkp_l if dkp is None else dkp + dkp_l
                dv = dv_l if dv is None else dv + dv_l
                gw["mla_w_dq"][l - 2], gw["mla_w_out"][l - 2] = gd["w_dq"], gd["w_out"]
                gw["mla_w_uq"][l - 2] = gd["w_uq"].reshape(QL, NH, HP)[:, :, :QKH].reshape(QL, NH * QKH)
                gp["mla_q_lora_norm_g"][l - 2] = gd["ql_g"][0]
                gp["mla_q_norm_g"][l - 2] = _join_norm(gd["q_gn"], gd["q_gr"])
            gnorm[l][i] = gd["g"][0]
            for j in range(3):
                dmod[l][3 * i + j] = gd["mod"][j][0]
    gwf = {n: jnp.stack(v) for n, v in gw.items()}
    gwf["ffn_w_in"], gwf["ffn_w_out"] = g_in4, g_out4
    gwf["mla_w_dkv"] = gk["w_dkv"][:, :KVL + ROPE]
    gwf["mla_w_ukv"] = gk["w_ukv"].reshape(KVL, 2, NH, DH).transpose(0, 2, 1, 3).reshape(KVL, 2 * NH * DH)
    gpf = {n: jnp.stack(v) for n, v in gp.items()}
    gpf["norm_g"] = jnp.stack([jnp.stack(r) for r in gnorm])
    gpf["kv_norm_g"] = gk["kv_norm_g"][0]
    gpf["mla_kv_norm_g"] = gk["kv_lat_g"][0]
    gpf["mla_k_norm_g"] = _join_norm(gk["k_gn"], gk["k_gr"])
    dmods = jnp.stack([jnp.stack(r) for r in dmod])
    dkvmod = jnp.stack([gk["mod"][0][0], gk["mod"][1][0]])
    return loss, dx, gwf, gpf, dmods, dkvmod


SMALL = ("norm_g", "gdn_conv_w", "gdn_a_log", "gdn_dt_bias", "gdn_norm_g", "kv_norm_g", "mla_kv_norm_g", "mla_k_norm_g",
         "mla_q_lora_norm_g", "mla_q_norm_g")


def kernel(x, c, positions, ada_w, ada_b, norm_g, ffn_w_in, ffn_w_out, gdn_w_in, gdn_conv_w, gdn_a_log, gdn_dt_bias,
           gdn_norm_g, gdn_w_out, kv_ada_w, kv_ada_b, kv_norm_g, mla_w_dkv, mla_kv_norm_g, mla_w_ukv, mla_k_norm_g,
           mla_w_dq, mla_q_lora_norm_g, mla_w_uq, mla_q_norm_g, mla_w_out, loss_target, m_ada_w, m_ada_b, m_norm_g,
           m_ffn_w_in, m_ffn_w_out, m_gdn_w_in, m_gdn_conv_w, m_gdn_a_log, m_gdn_dt_bias, m_gdn_norm_g, m_gdn_w_out,
           m_kv_ada_w, m_kv_ada_b, m_kv_norm_g, m_mla_w_dkv, m_mla_kv_norm_g, m_mla_w_ukv, m_mla_k_norm_g, m_mla_w_dq,
           m_mla_q_lora_norm_g, m_mla_w_uq, m_mla_q_norm_g, m_mla_w_out, v_ada_w, v_ada_b, v_norm_g, v_ffn_w_in,
           v_ffn_w_out, v_gdn_w_in, v_gdn_conv_w, v_gdn_a_log, v_gdn_dt_bias, v_gdn_norm_g, v_gdn_w_out, v_kv_ada_w,
           v_kv_ada_b, v_kv_norm_g, v_mla_w_dkv, v_mla_kv_norm_g, v_mla_w_ukv, v_mla_k_norm_g, v_mla_w_dq,
           v_mla_q_lora_norm_g, v_mla_w_uq, v_mla_q_norm_g, v_mla_w_out):
    a = dict(locals())
    mx, my, mc = _me()
    dev = 4 * mx + 2 * my + mc
    chip = 2 * mx + my
    x, tgt, pos = a["x"][0], a["loss_target"][0], a["positions"][0]
    take = lambda arr, i, axis=0: lax.dynamic_index_in_dim(arr, i, axis, keepdims=False)

    pre = _all_gather8("ag_pre", _pack_small([a["c"], a["gdn_conv_w"], a["norm_g"]]))
    c_all, conv_sh, norm_sh = _unpack_small(pre, [(D,), a["gdn_conv_w"].shape, a["norm_g"].shape])
    P = {n: a[n] for n in SMALL}
    P["gdn_conv_w"] = jnp.concatenate([conv_sh[2 * s] for s in range(4)], axis=2)
    P["norm_g"] = jnp.concatenate([norm_sh[2 * s] for s in range(4)], axis=2)
    c_act = _silu_rows("c_act", c_all)
    nada = a["ada_w"].shape[2]
    nkv = a["kv_ada_w"].shape[1]
    modp = [_matmul(f"mod{l}", c_act, a["ada_w"], precise=True, lay="b_stack", li=l) for l in range(4)]
    kvp = _matmul("modkv", c_act, a["kv_ada_w"], precise=True)
    mp = _all_gather8("ag_mod", _pack_small(modp + [kvp]))
    modp_all, kvp_all = _unpack_small(mp, [(4, 8, nada), (8, nkv)])
    mods = jnp.concatenate([take(modp_all[2 * s], dev, 1) for s in range(4)], axis=1) + a["ada_b"]
    mods = mods.reshape(4, NMOD, D)
    kvmod = (jnp.concatenate([take(kvp_all[2 * s], dev, 0) for s in range(4)]) + a["kv_ada_b"]).reshape(2, D)

    def gather(tag, w2):
        return _gather_weights("ag_" + tag, _place_shard("own_" + tag, w2, chip))

    def reduce(tag, g4):
        q = _add_half("rsp_" + tag, g4, _exchange_half("rs1_" + tag, g4), mc)
        r2 = _add_chips("rsc_" + tag, q, _scatter_chips("rs2_" + tag, q), chip, mc)
        return _exchange_full("rs3_" + tag, r2)

    halves = lambda t: t.reshape((2, -1) + t.shape[-1:])
    W = {n: gather(t, halves(a[n].astype(BF16))).reshape((4, 8) + a[n].shape[2:])
         for n, t in (("ffn_w_in", "wi"), ("ffn_w_out", "wo"))}
    wg = gather("wg", a["gdn_w_in"].astype(BF16))
    W["gdn_w_in"] = jnp.concatenate([wg[s] for s in range(4)], axis=2)
    wall = gather("wr", _pack_flat(jnp.concatenate([a[n].reshape(-1).astype(BF16) for n, _ in REST]))).reshape(4, -1)
    off = 0
    for n, ax in REST:
        sz = a[n].size
        W[n] = _shards_merge(wall[:, off:off + sz].reshape((4,) + a[n].shape), ax)
        off += sz

    loss, dx, gw, gp, dmods, dkvmod = _step(x, tgt, pos, mods, kvmod, W, P)
    loss = lax.psum(loss, ("x", "y", "c"))

    grads = {n: reduce(t, gw[n].reshape((4, 2, -1) + a[n].shape[-1:])).reshape(a[n].shape)
             for n, t in (("ffn_w_in", "wi"), ("ffn_w_out", "wo"))}
    ng = a["gdn_w_in"].shape[2]
    grads["gdn_w_in"] = reduce("wg", jnp.stack([gw["gdn_w_in"][:, :, s * ng:(s + 1) * ng] for s in range(4)]))
    gsh = reduce("wr", _pack_flat(jnp.concatenate([_shards_first(gw[n], ax).reshape(4, -1) for n, ax in REST], axis=1)))
    gsh = gsh.reshape(-1)
    off = 0
    for n, _ in REST:
        grads[n] = gsh[off:off + a[n].size].reshape(a[n].shape)
        off += a[n].size

    small = _all_gather8("ag_small", _pack_small([dmods, dkvmod] + [gp[n] for n in SMALL]))
    shapes = [(4, NMOD * D), (2 * D,)] + [gp[n].shape for n in SMALL]
    dmod_all, dkv_all = _unpack_small(small, shapes)[:2]
    tot = _unpack_small(_sum8("sum_small", small), shapes)
    grads["ada_b"], grads["kv_ada_b"] = tot[0], tot[1]
    for n, t in zip(SMALL, tot[2:]):
        grads[n] = t
    grads["norm_g"] = lax.dynamic_slice_in_dim(grads["norm_g"], chip * a["norm_g"].shape[2], a["norm_g"].shape[2], 2)
    grads["gdn_conv_w"] = lax.dynamic_slice_in_dim(grads["gdn_conv_w"], chip * a["gdn_conv_w"].shape[2],
                                                   a["gdn_conv_w"].shape[2], 2)
    ca = jnp.pad(c_act, ((0, LANE - 8), (0, 0)))
    dm = jnp.pad(lax.dynamic_slice_in_dim(dmod_all.reshape(8, 4, NMOD * D), chip * nada, nada, 2), ((0, LANE - 8), (0, 0), (0, 0)))
    gada = None
    for l in range(4):
        gada = _matmul(f"gada{l}", ca, dm[:, l], "tn", precise=True, lay="o_stack", li=l, into=gada, nmat=4)
    grads["ada_w"] = gada
    dk = jnp.pad(lax.dynamic_slice_in_dim(dkv_all, chip * nkv, nkv, 1), ((0, LANE - 8), (0, 0)))
    grads["kv_ada_w"] = _matmul("gadakv", ca, dk, "tn", precise=True)

    upd = [_adamw("adamw_" + n, a[n], grads[n], a["m_" + n], a["v_" + n]) for n in WEIGHTS]
    return (loss, dx[None], *[grads[n] for n in WEIGHTS], *[u[0] for u in upd], *[u[1] for u in upd], *[u[2] for u in upd])
```

```python
import functools

import jax
import jax.numpy as jnp
from jax import lax
from jax.experimental import pallas as pl
from jax.experimental.pallas import tpu as pltpu

F32 = jnp.float32
BF16 = jnp.bfloat16
HI = lax.Precision.HIGHEST
MESH = pl.DeviceIdType.MESH

D = 1024
NH = 8
DH = 128
FF = 2816
NMOD = 9
CHUNK = 64
ROPE = 64
QKH = 192
HP = 256
KVL = 256
QL = 384
GDN_IN = 4224
GATE_CB = 32
EPS = 1e-6
ROPE_BASE = 10000.0
LANE = 128
SUB = 8
VMEM_LIMIT = 56 * 1024 * 1024

ADAM_LR, ADAM_B1, ADAM_B2, ADAM_EPS, ADAM_WD, ADAM_STEP = 0.001, 0.9, 0.999, 1e-08, 0.01, 10


def _tile(n, prefs=(512, 384, 256, 128)):
    for p in prefs:
        if n % p == 0:
            return p
    return n


def _cparams(sem):
    return pltpu.CompilerParams(dimension_semantics=sem, vmem_limit_bytes=VMEM_LIMIT)


class Row:
    def __init__(self, arr, width=None, cb=0, splits=None, halo=None):
        self.arr = arr
        self.width = arr.shape[1] if width is None else width
        self.cb = cb
        self.splits = splits
        self.halo = halo


def _rowwise(name, fn, rows, bcs, outs, accs, tm):
    S = rows[0].arr.shape[0]
    n = S // tm
    nr, nb, no, na = len(rows), len(bcs), len(outs), len(accs)

    def body(*refs):
        rrefs, brefs = refs[:nr], refs[nr:nr + nb]
        orefs, arefs = refs[nr + nb:nr + nb + no], refs[nr + nb + no:]
        pieces = []
        for r, ref in zip(rows, rrefs):
            if r.splits is None:
                pieces.append(ref[...])
            else:
                off = 0
                for w in r.splits:
                    pieces.append(ref[:, off:off + w])
                    off += w
        out_pieces, acc_vals = fn(pieces, [b[...] for b in brefs])
        k = 0
        for (widths, dt), oref in zip(outs, orefs):
            off = 0
            for w in widths:
                oref[:, off:off + w] = out_pieces[k].astype(dt)
                k += 1
                off += w
        if na:
            @pl.when(pl.program_id(0) == 0)
            def _():
                for a in arefs:
                    a[...] = jnp.zeros(a.shape, F32)
            for a, v in zip(arefs, acc_vals):
                a[...] += v

    in_specs = []
    for r in rows:
        if r.halo is None:
            in_specs.append(pl.BlockSpec((tm, r.width), lambda i, cb=r.cb: (i, cb)))
        elif r.halo == "prev":
            in_specs.append(pl.BlockSpec((SUB, r.width), lambda i, cb=r.cb: (jnp.maximum(i * (tm // SUB) - 1, 0), cb)))
        else:
            in_specs.append(pl.BlockSpec((SUB, r.width), lambda i, cb=r.cb: (jnp.minimum((i + 1) * (tm // SUB), S // SUB - 1), cb)))
    in_specs += [pl.BlockSpec(b.shape, lambda i, nd=b.ndim: (0,) * nd) for b in bcs]
    out_specs = [pl.BlockSpec((tm, sum(w)), lambda i: (i, 0)) for w, _ in outs]
    out_specs += [pl.BlockSpec(s, lambda i: (0, 0)) for s in accs]
    out_shape = [jax.ShapeDtypeStruct((S, sum(w)), dt) for w, dt in outs]
    out_shape += [jax.ShapeDtypeStruct(s, F32) for s in accs]
    res = pl.pallas_call(body, name=name, grid=(n,), in_specs=in_specs, out_specs=out_specs, out_shape=out_shape,
                         compiler_params=_cparams(("arbitrary",)))(*[r.arr for r in rows], *bcs)
    return res


def _rw_fwd(name, f, rows, bcs, outs, tm):
    def fn(pieces, bvals):
        return list(f(*[p.astype(F32) for p in pieces], *[b.astype(F32) for b in bvals])), []
    return _rowwise(name, fn, rows, bcs, outs, [], tm)


def _npieces(rows):
    return sum(1 if r.splits is None else len(r.splits) for r in rows)


def _rw_bwd(name, f, rows, bcs, cts, drow, dbc, outs, tm, add=None):
    np_, nct = _npieces(rows), _npieces(cts)

    def fn(pieces, bvals):
        allv = [p.astype(F32) for p in pieces[:np_]] + [b.astype(F32) for b in bvals]
        ct = [p.astype(F32) for p in pieces[np_:np_ + nct]]
        didx = [i for i, m in enumerate(list(drow) + list(dbc)) if m]

        def g(*dv):
            full = list(allv)
            for i, v in zip(didx, dv):
                full[i] = v
            return tuple(f(*full))

        _, vjp = jax.vjp(g, *[allv[i] for i in didx])
        grads = vjp(tuple(ct))
        nrd = sum(bool(m) for m in drow)
        rg, bg = list(grads[:nrd]), list(grads[nrd:])
        if add is not None:
            rg[0] = rg[0] + pieces[np_ + nct].astype(F32)
        return rg, bg

    accs = [b.shape for b, m in zip(bcs, dbc) if m]
    return _rowwise(name, fn, list(rows) + list(cts) + ([add] if add is not None else []), bcs, outs, accs, tm)


def _sigmoid(x):
    return 1.0 / (1.0 + jnp.exp(-x))


def _silu(x):
    return x * _sigmoid(x)


def _softplus(x):
    return jnp.maximum(x, 0.0) + jnp.log(1.0 + jnp.exp(-jnp.abs(x)))


def f_mod(x, g, shift, scale):
    y = x * lax.rsqrt(jnp.mean(x * x, axis=-1, keepdims=True) + EPS)
    return (y * g * (1.0 + scale) + shift,)


def f_rms(x, g):
    return (x * lax.rsqrt(jnp.mean(x * x, axis=-1, keepdims=True) + EPS) * g,)


def f_act(gate, up):
    return (_silu(gate) * up,)


def make_f_res(coef):
    def f_res(y, gate):
        return (coef * gate * y,)
    return f_res


def f_gdnpre(*p):
    out = []
    for i, t in enumerate(p):
        t = _silu(t)
        if i < 2 * NH:
            t = t * lax.rsqrt(jnp.sum(t * t, axis=-1, keepdims=True) + EPS)
        out.append(t)
    return tuple(out)


def f_gates(gates, a_log, dt_bias):
    return _sigmoid(gates), -jnp.exp(a_log) * _softplus(gates + dt_bias)


def f_gdnpost(*a):
    o, z, g = a[:NH], a[NH:2 * NH], a[2 * NH]
    out = []
    for oh, zh in zip(o, z):
        y = oh * lax.rsqrt(jnp.mean(oh * oh, axis=-1, keepdims=True) + EPS) * g
        out.append(y * _silu(zh))
    return tuple(out)


def make_f_qk(shared_rope):
    def f(*a):
        if shared_rope:
            ns, rs = a[:NH], [a[NH]] * NH
            cosp, sins, gn, gr, pm = a[NH + 1:NH + 6]
        else:
            ns, rs = a[0:2 * NH:2], a[1:2 * NH:2]
            cosp, sins, gn, gr, pm = a[2 * NH:2 * NH + 5]
        out = []
        for n, r in zip(ns, rs):
            ss = jnp.sum(n * n, axis=-1, keepdims=True) + jnp.sum(r * r, axis=-1, keepdims=True)
            rstd = lax.rsqrt(ss * (1.0 / QKH) + EPS)
            yn = n * rstd * gn
            yr = r * rstd * gr
            sw = jnp.dot(yr, pm, precision=HI, preferred_element_type=F32)
            out += [yn, yr * cosp + sw * sins]
        return tuple(out)
    return f


def _matmul(name, a, b, mode="nn", out_dtype=F32, precise=False, lay=None, li=0, into=None, nmat=1):
    if lay == "b_cols":
        per = b.shape[3]
        rb, cb = b.shape[2], 4 * per
    elif lay == "b_rows":
        per = b.shape[2]
        rb, cb = 4 * per, b.shape[3]
    elif lay == "b_stack":
        rb, cb = b.shape[1:]
    else:
        rb, cb = b.shape
    if mode == "nn":
        (M, K), N = a.shape, cb
    elif mode == "nt":
        (M, K), N = a.shape, rb
    else:
        (K, M), N = a.shape, cb
    tm = _tile(M, (1024, 512, 256, 128))
    tn = _tile(N, (1408, 1024, 512, 384, 256, 128))
    tk = _tile(K, (1408, 1024, 512, 384, 256, 128))
    if lay == "b_cols":
        tn, tk = (per, tk) if mode == "nn" else (tn, per)
    elif lay == "b_rows":
        tm, tn, tk = (tm, 512, K) if mode == "nn" else (min(tm, 512), N, tk)
    elif lay == "o_cols":
        per = N // 4
        tn = per
    elif lay == "o_rows":
        per = M // 4
        tm, tn = M, 512
    nk = K // tk
    dims = {"nn": (((1,), (0,)), ((), ())), "nt": (((1,), (1,)), ((), ())), "tn": (((0,), (0,)), ((), ()))}[mode]

    def body(a_ref, b_ref, *rest):
        o_ref, acc_ref = rest[-2:]
        k = pl.program_id(2)

        @pl.when(k == 0)
        def _():
            acc_ref[...] = jnp.zeros(acc_ref.shape, F32)

        bv = b_ref[...]
        if lay == "b_rows":
            bv = bv.reshape(4 * per, bv.shape[2])
        if precise:
            acc_ref[...] += lax.dot_general(a_ref[...].astype(F32), bv.astype(F32), dims, precision=HI,
                                            preferred_element_type=F32)
        else:
            acc_ref[...] += lax.dot_general(a_ref[...].astype(BF16), bv.astype(BF16), dims, preferred_element_type=F32)

        @pl.when(k == nk - 1)
        def _():
            if lay == "o_rows":
                for s in range(4):
                    o_ref[s] = acc_ref[s * per:(s + 1) * per, :].astype(o_ref.dtype)
            else:
                o_ref[...] = acc_ref[...].astype(o_ref.dtype)

    a_spec = pl.BlockSpec((tk, tm), lambda i, j, k: (k, i)) if mode == "tn" else pl.BlockSpec((tm, tk), lambda i, j, k: (i, k))
    if lay == "b_cols":
        b_spec = (pl.BlockSpec((None, None, tk, per), lambda i, j, k: (j, li, k, 0)) if mode == "nn" else
                  pl.BlockSpec((None, None, tn, per), lambda i, j, k: (k, li, j, 0)))
    elif lay == "b_rows":
        b_spec = (pl.BlockSpec((4, None, per, tn), lambda i, j, k: (0, li, 0, j)) if mode == "nn" else
                  pl.BlockSpec((4, None, per, tk), lambda i, j, k: (0, li, 0, k)))
    elif lay == "b_stack":
        b_spec = pl.BlockSpec((None, tk, tn), lambda i, j, k: (li, k, j))
    elif mode == "nt":
        b_spec = pl.BlockSpec((tn, tk), lambda i, j, k: (j, k))
    else:
        b_spec = pl.BlockSpec((tk, tn), lambda i, j, k: (k, j))
    if lay == "o_stack":
        o_spec = pl.BlockSpec((None, tm, tn), lambda i, j, k: (li, i, j))
        o_shape = jax.ShapeDtypeStruct((nmat, M, N), out_dtype)
    elif lay == "o_cols":
        o_spec = pl.BlockSpec((None, None, tm, per), lambda i, j, k: (j, li, i, 0))
        o_shape = jax.ShapeDtypeStruct((4, nmat, M, per), out_dtype)
    elif lay == "o_rows":
        o_spec = pl.BlockSpec((4, None, per, tn), lambda i, j, k: (0, li, 0, j))
        o_shape = jax.ShapeDtypeStruct((4, nmat, per, N), out_dtype)
    else:
        o_spec = pl.BlockSpec((tm, tn), lambda i, j, k: (i, j))
        o_shape = jax.ShapeDtypeStruct((M, N), out_dtype)
    in_specs, args, alias = [a_spec, b_spec], [a, b], {}
    if into is not None:
        in_specs.append(pl.BlockSpec(memory_space=pl.ANY))
        args.append(into)
        alias = {2: 0}
    return pl.pallas_call(body, name=name, grid=(M // tm, N // tn, nk), in_specs=in_specs, out_specs=o_spec,
                          out_shape=o_shape, scratch_shapes=[pltpu.VMEM((tm, tn), F32)], input_output_aliases=alias,
                          compiler_params=_cparams(("parallel", "parallel", "arbitrary")))(*args)


def _shift_down(t, p, d):
    if d == 0:
        return t
    tr = pltpu.roll(t, d, 0)
    pr = pltpu.roll(p, d, 0)
    r8 = lax.broadcasted_iota(jnp.int32, p.shape, 0)
    first = jnp.where(r8 < d, pr, tr[:SUB])
    return jnp.concatenate([first, tr[SUB:]], axis=0)


def _shift_up(t, nx, d):
    if d == 0:
        return t
    tm = t.shape[0]
    tr = pltpu.roll(t, tm - d, 0)
    nr = pltpu.roll(nx, SUB - d, 0)
    r8 = lax.broadcasted_iota(jnp.int32, nx.shape, 0)
    last = jnp.where(r8 >= SUB - d, nr, tr[tm - SUB:])
    return jnp.concatenate([tr[:tm - SUB], last], axis=0)


def _conv_fwd(name, proj, w8, C, tm):
    def fn(pieces, bvals):
        t, p = pieces[0].astype(F32), pieces[1].astype(F32)
        w = bvals[0]
        p = jnp.where(pl.program_id(0) == 0, 0.0, p)
        out = w[3:4] * t
        for d in (1, 2, 3):
            out = out + w[3 - d:4 - d] * _shift_down(t, p, d)
        return [out], []
    return _rowwise(name, fn, [Row(proj, C), Row(proj, C, halo="prev")], [w8], [((C,), F32)], [], tm)[0]


def _conv_bwd(name, proj, dout, w8, C, tm, out_dtype):
    n = proj.shape[0] // tm

    def fn(pieces, bvals):
        t, p, g, gn = [v.astype(F32) for v in pieces]
        w = bvals[0]
        i = pl.program_id(0)
        p = jnp.where(i == 0, 0.0, p)
        gn = jnp.where(i == n - 1, 0.0, gn)
        dx = w[3:4] * g
        dws = [jnp.sum(g * t, axis=0, keepdims=True)]
        for d in (1, 2, 3):
            dx = dx + w[3 - d:4 - d] * _shift_up(g, gn, d)
            dws.append(jnp.sum(g * _shift_down(t, p, d), axis=0, keepdims=True))
        dw = jnp.concatenate([dws[3], dws[2], dws[1], dws[0], jnp.zeros((4, g.shape[1]), F32)], axis=0)
        return [dx], [dw]
    return _rowwise(name, fn, [Row(proj, C), Row(proj, C, halo="prev"), Row(dout), Row(dout, halo="next")], [w8],
                    [((C,), out_dtype)], [(SUB, C)], tm)


def _bdot(a, b, ca, cb):
    return lax.dot_general(a.astype(BF16), b.astype(BF16), (((ca,), (cb,)), ((0,), (0,))), preferred_element_type=F32)


def _bdot3(a, b, ca, cb):
    dims = (((ca,), (cb,)), ((0,), (0,)))
    ah, bh = a.astype(BF16), b.astype(BF16)
    al, bl = (a - ah.astype(F32)).astype(BF16), (b - bh.astype(F32)).astype(BF16)
    d = lambda x, y: lax.dot_general(x, y, dims, preferred_element_type=F32)
    return d(ah, bh) + (d(ah, bl) + d(al, bh))


@jax.custom_vjp
def _bmm3(a, b):
    return _bdot3(a, b, 2, 1)


_bmm3.defvjp(lambda a, b: (_bdot3(a, b, 2, 1), (a, b)),
             lambda res, g: (_bdot3(g, res[1], 2, 2), _bdot3(res[0], g, 1, 1)))


def _neumann(nl):
    C = nl.shape[1]
    eye = (lax.broadcasted_iota(jnp.int32, (1, C, C), 1) == lax.broadcasted_iota(jnp.int32, (1, C, C), 2)).astype(F32)
    T = eye + nl
    pw = nl
    for _ in range(C.bit_length() - 2):
        pw = _bdot3(pw, pw, 2, 1)
        T = T + _bdot3(T, pw, 2, 1)
    return T


_unit_lower_inv = jax.custom_vjp(_neumann)


def _unit_lower_inv_fwd(nl):
    T = _neumann(nl)
    return T, T


def _unit_lower_inv_bwd(T, g):
    return (_bdot3(_bdot3(T, g, 1, 1), T, 2, 2),)


_unit_lower_inv.defvjp(_unit_lower_inv_fwd, _unit_lower_inv_bwd)


@jax.custom_vjp
def _known_inv(nl, T):
    return T


_known_inv.defvjp(lambda nl, T: (T, T), lambda T, g: (_unit_lower_inv_bwd(T, g)[0], jnp.zeros_like(T)))


def _gdn_chunk(q, k, v, gcol, grow, bcol, S, T_saved=None):
    C = CHUNK
    ii = lax.broadcasted_iota(jnp.int32, (1, C, C), 1)
    jj = lax.broadcasted_iota(jnp.int32, (1, C, C), 2)
    incl, strict = ii >= jj, ii > jj
    gc_col = jnp.sum(jnp.where(incl, 1.0, 0.0) * grow, axis=2, keepdims=True)
    gc_row = jnp.sum(jnp.where(jj >= ii, 1.0, 0.0) * gcol, axis=1, keepdims=True)
    decay = jnp.where(incl, jnp.exp(jnp.where(incl, gc_col - gc_row, 0.0)), 0.0)
    qs = q * (DH ** -0.5)
    kb = k * bcol
    nl = -jnp.where(strict, _bdot(kb, k, 2, 2) * decay, 0.0)
    T = _unit_lower_inv(nl) if T_saved is None else _known_inv(nl, T_saved)
    egc = jnp.exp(gc_col)
    u = _bmm3(T, v * bcol)
    w = _bmm3(T, kb * egc)
    att = jnp.where(incl, _bdot(qs, k, 2, 2) * decay, 0.0)
    v_new = u - _bdot(w, S, 2, 1)
    o = _bdot(qs * egc, S, 2, 1) + _bdot(att, v_new, 2, 1)
    g_last = jnp.sum(grow, axis=2, keepdims=True)
    k_dec = k * jnp.exp(g_last - gc_col)
    S_out = S * jnp.exp(g_last) + _bdot(k_dec, v_new, 1, 1)
    return o, S_out, T


GDN_STEP = 4


def _chunk_args(refs, c):
    q_ref, k_ref, v_ref, gc_ref, gr_ref, b_ref = refs
    rows = slice(c * CHUNK, (c + 1) * CHUNK)
    heads = lambda ref, w: jnp.stack([ref[rows, h * w:(h + 1) * w] for h in range(NH)])
    grow = jnp.stack([gr_ref[c, h:h + 1, :] for h in range(NH)])
    return heads(q_ref, DH), heads(k_ref, DH), heads(v_ref, DH), heads(gc_ref, 1), grow, heads(b_ref, 1)


def _gdn_specs(NP, rev):
    ix = (lambda i: NP - 1 - i) if rev else (lambda i: i)
    wide = pl.BlockSpec((GDN_STEP * CHUNK, D), lambda i: (ix(i), 0))
    col = pl.BlockSpec((GDN_STEP * CHUNK, NH), lambda i: (ix(i), 0))
    row = pl.BlockSpec((GDN_STEP, NH, CHUNK), lambda i: (ix(i), 0, 0))
    st = pl.BlockSpec((1, NH, DH, DH), lambda i: (ix(i), 0, 0, 0))
    tinv = pl.BlockSpec((GDN_STEP, NH, CHUNK, CHUNK), lambda i: (ix(i), 0, 0, 0))
    return wide, col, row, st, tinv


def _gdn_fwd(name, q, k, v, gcol, grow, bcol):
    S = q.shape[0]
    NC = S // CHUNK
    NP = NC // GDN_STEP

    def body(q_ref, k_ref, v_ref, gc_ref, gr_ref, b_ref, o_ref, ss_ref, t_ref, st):
        @pl.when(pl.program_id(0) == 0)
        def _():
            st[...] = jnp.zeros(st.shape, F32)
        state = st[...]
        ss_ref[0] = state
        for c in range(GDN_STEP):
            o, state, tinv = _gdn_chunk(*_chunk_args((q_ref, k_ref, v_ref, gc_ref, gr_ref, b_ref), c), state)
            for h in range(NH):
                o_ref[c * CHUNK:(c + 1) * CHUNK, h * DH:(h + 1) * DH] = o[h]
            t_ref[c] = tinv
        st[...] = state

    wide, col, row, stsp, tsp = _gdn_specs(NP, False)
    return pl.pallas_call(body, name=name, grid=(NP,), in_specs=[wide, wide, wide, col, row, col],
                          out_specs=[wide, stsp, tsp],
                          out_shape=[jax.ShapeDtypeStruct((S, D), F32), jax.ShapeDtypeStruct((NP, NH, DH, DH), F32),
                                     jax.ShapeDtypeStruct((NC, NH, CHUNK, CHUNK), F32)],
                          scratch_shapes=[pltpu.VMEM((NH, DH, DH), F32)],
                          compiler_params=_cparams(("arbitrary",)))(q, k, v, gcol, grow, bcol)


def _gdn_bwd(name, q, k, v, gcol, grow, bcol, ssave, tsave, do):
    S = q.shape[0]
    NC = S // CHUNK
    NP = NC // GDN_STEP

    def body(q_ref, k_ref, v_ref, gc_ref, gr_ref, b_ref, ss_ref, t_ref, do_ref, dq_ref, dk_ref, dv_ref, dgc_ref, dgr_ref,
             db_ref, dst):
        @pl.when(pl.program_id(0) == 0)
        def _():
            dst[...] = jnp.zeros(dst.shape, F32)
        refs = (q_ref, k_ref, v_ref, gc_ref, gr_ref, b_ref)
        tinv = [t_ref[c] for c in range(GDN_STEP)]

        def chain(state, *flat):
            outs = []
            for c in range(GDN_STEP):
                o, state, _ = _gdn_chunk(*flat[6 * c:6 * c + 6], state, T_saved=tinv[c])
                outs.append(o)
            return tuple(outs) + (state,)

        prim = [a for c in range(GDN_STEP) for a in _chunk_args(refs, c)]
        _, vjp = jax.vjp(chain, ss_ref[0], *prim)
        dos = tuple(jnp.stack([do_ref[c * CHUNK:(c + 1) * CHUNK, h * DH:(h + 1) * DH] for h in range(NH)])
                    for c in range(GDN_STEP))
        grads = vjp(dos + (dst[...],))
        dst[...] = grads[0]
        for c in range(GDN_STEP):
            dq, dk, dv, dgc, dgr, db = grads[1 + 6 * c:7 + 6 * c]
            rows = slice(c * CHUNK, (c + 1) * CHUNK)
            for h in range(NH):
                hs = slice(h * DH, (h + 1) * DH)
                dq_ref[rows, hs] = dq[h]
                dk_ref[rows, hs] = dk[h]
                dv_ref[rows, hs] = dv[h]
                dgc_ref[rows, h:h + 1] = dgc[h]
                dgr_ref[c, h:h + 1, :] = dgr[h]
                db_ref[rows, h:h + 1] = db[h]

    wide, col, row, stsp, tsp = _gdn_specs(NP, True)
    return pl.pallas_call(body, name=name, grid=(NP,), in_specs=[wide, wide, wide, col, row, col, stsp, tsp, wide],
                          out_specs=[wide, wide, wide, col, row, col],
                          out_shape=[jax.ShapeDtypeStruct((S, D), F32)] * 3 + [jax.ShapeDtypeStruct((S, NH), F32),
                                                                                 jax.ShapeDtypeStruct((NC, NH, CHUNK), F32),
                                                                                 jax.ShapeDtypeStruct((S, NH), F32)],
                          scratch_shapes=[pltpu.VMEM((NH, DH, DH), F32)],
                          compiler_params=_cparams(("arbitrary",)))(q, k, v, gcol, grow, bcol, ssave, tsave, do)


TQ = 512
SM_SCALE = QKH ** -0.5
NEG = -1e30


def _diag_mask(transposed):
    r = lax.broadcasted_iota(jnp.int32, (TQ, TQ), 0) // CHUNK
    c = lax.broadcasted_iota(jnp.int32, (TQ, TQ), 1) // CHUNK
    return (r <= c) if transposed else (c <= r)


def _dot_nt(a, b):
    return lax.dot_general(a, b, (((1,), (1,)), ((), ())), preferred_element_type=F32)


def _flash_fwd(name, qp, kp, kv):
    S = qp.shape[0]
    nq = S // (2 * TQ)

    def body(q_ref, k_ref, v_ref, o_ref, lse_ref):
        qi = pl.program_id(1)
        qs = (q_ref[:TQ, :], q_ref[TQ:, :])

        def step(q, j, carry, masked):
            m, l, acc = carry
            rows = pl.ds(pl.multiple_of(j * TQ, TQ), TQ)
            s = _dot_nt(q, k_ref[rows, :]) * SM_SCALE
            if masked:
                s = jnp.where(_diag_mask(False), s, NEG)
            m_new = jnp.maximum(m, jnp.max(s, axis=-1, keepdims=True))
            p = jnp.exp(s - m_new)
            alpha = jnp.exp(m - m_new)
            l = alpha * l + jnp.sum(p, axis=-1, keepdims=True)
            acc = alpha * acc + jnp.dot(p.astype(BF16), v_ref[rows, :].astype(BF16), preferred_element_type=F32)
            return m_new, l, acc

        init = (jnp.full((TQ, 1), NEG, F32), jnp.zeros((TQ, 1), F32), jnp.zeros((TQ, DH), F32))
        ca, cb = lax.fori_loop(0, 2 * qi, lambda j, c: (step(qs[0], j, c[0], False), step(qs[1], j, c[1], False)),
                               (init, init))
        ca = step(qs[0], 2 * qi, ca, True)
        cb = step(qs[1], 2 * qi + 1, step(qs[1], 2 * qi, cb, False), True)
        for u, (m, l, acc) in enumerate((ca, cb)):
            o_ref[u * TQ:(u + 1) * TQ, :] = acc / l
            lse_ref[0, u * TQ:(u + 1) * TQ, :] = m + jnp.log(l)

    return pl.pallas_call(
        body, name=name, grid=(NH, nq),
        in_specs=[pl.BlockSpec((2 * TQ, HP), lambda h, i: (i, h)), pl.BlockSpec((S, HP), lambda h, i: (0, h)),
                  pl.BlockSpec((S, DH), lambda h, i: (0, NH + h))],
        out_specs=[pl.BlockSpec((2 * TQ, DH), lambda h, i: (i, h)), pl.BlockSpec((1, 2 * TQ, 1), lambda h, i: (h, i, 0))],
        out_shape=[jax.ShapeDtypeStruct((S, NH * DH), F32), jax.ShapeDtypeStruct((NH, S, 1), F32)],
        compiler_params=_cparams(("parallel", "arbitrary")))(qp, kp, kv)


def _flash_bwd_dq(name, qp, kp, kv, o, do, lse):
    S = qp.shape[0]
    nq = S // (2 * TQ)

    def body(q_ref, k_ref, v_ref, o_ref, do_ref, lse_ref, dq_ref, dl_ref):
        qi = pl.program_id(1)
        subs = []
        for u in range(2):
            sl = slice(u * TQ, (u + 1) * TQ)
            do = do_ref[sl, :]
            delta = jnp.sum(o_ref[sl, :] * do, axis=-1, keepdims=True)
            dl_ref[0, sl, :] = delta
            subs.append((q_ref[sl, :], do.astype(BF16), lse_ref[0, sl, :], delta))

        def step(sub, j, dq, masked):
            q, dob, lse, delta = sub
            rows = pl.ds(pl.multiple_of(j * TQ, TQ), TQ)
            k = k_ref[rows, :]
            s = _dot_nt(q, k) * SM_SCALE
            if masked:
                s = jnp.where(_diag_mask(False), s, NEG)
            p = jnp.exp(s - lse)
            dp = _dot_nt(dob, v_ref[rows, :].astype(BF16))
            ds = p * (dp - delta) * SM_SCALE
            return dq + jnp.dot(ds.astype(BF16), k, preferred_element_type=F32)

        zero = jnp.zeros((TQ, HP), F32)
        dqa, dqb = lax.fori_loop(0, 2 * qi, lambda j, c: (step(subs[0], j, c[0], False), step(subs[1], j, c[1], False)),
                                 (zero, zero))
        dq_ref[:TQ, :] = step(subs[0], 2 * qi, dqa, True)
        dq_ref[TQ:, :] = step(subs[1], 2 * qi + 1, step(subs[1], 2 * qi, dqb, False), True)

    return pl.pallas_call(
        body, name=name, grid=(NH, nq),
        in_specs=[pl.BlockSpec((2 * TQ, HP), lambda h, i: (i, h)), pl.BlockSpec((S, HP), lambda h, i: (0, h)),
                  pl.BlockSpec((S, DH), lambda h, i: (0, NH + h)), pl.BlockSpec((2 * TQ, DH), lambda h, i: (i, h)),
                  pl.BlockSpec((2 * TQ, DH), lambda h, i: (i, h)), pl.BlockSpec((1, 2 * TQ, 1), lambda h, i: (h, i, 0))],
        out_specs=[pl.BlockSpec((2 * TQ, HP), lambda h, i: (i, h)), pl.BlockSpec((1, 2 * TQ, 1), lambda h, i: (h, i, 0))],
        out_shape=[jax.ShapeDtypeStruct((S, NH * HP), F32), jax.ShapeDtypeStruct((NH, S, 1), F32)],
        compiler_params=_cparams(("parallel", "arbitrary")))(qp, kp, kv, o, do, lse)


def _flash_bwd_dkv(name, qp, kp, kv, do, lse_row, delta_row):
    S = qp.shape[0]
    nq = S // TQ

    def body(q_ref, k_ref, v_ref, do_ref, lse_ref, dl_ref, dk_ref, dv_ref):
        kj = pl.program_id(1)
        subs = [(k_ref[u * TQ:(u + 1) * TQ, :], v_ref[u * TQ:(u + 1) * TQ, :].astype(BF16)) for u in range(2)]

        def step(sub, i, carry, masked):
            k, vb = sub
            dk, dv = carry
            rows = pl.ds(pl.multiple_of(i * TQ, TQ), TQ)
            q = q_ref[rows, :]
            dob = do_ref[rows, :].astype(BF16)
            st = _dot_nt(k, q) * SM_SCALE
            pt = jnp.exp(st - lse_ref[0, :, rows])
            if masked:
                pt = jnp.where(_diag_mask(True), pt, 0.0)
            dv = dv + jnp.dot(pt.astype(BF16), dob, preferred_element_type=F32)
            dpt = _dot_nt(vb, dob)
            dst = pt * (dpt - dl_ref[0, :, rows]) * SM_SCALE
            dk = dk + jnp.dot(dst.astype(BF16), q, preferred_element_type=F32)
            return dk, dv

        zero = (jnp.zeros((TQ, HP), F32), jnp.zeros((TQ, DH), F32))
        ca = step(subs[0], 2 * kj + 1, step(subs[0], 2 * kj, zero, True), False)
        cb = step(subs[1], 2 * kj + 1, zero, True)
        ca, cb = lax.fori_loop(2 * kj + 2, nq, lambda i, c: (step(subs[0], i, c[0], False), step(subs[1], i, c[1], False)),
                               (ca, cb))
        for u, (dk, dv) in enumerate((ca, cb)):
            dk_ref[u * TQ:(u + 1) * TQ, :] = dk
            dv_ref[u * TQ:(u + 1) * TQ, :] = dv

    return pl.pallas_call(
        body, name=name, grid=(NH, nq // 2),
        in_specs=[pl.BlockSpec((S, HP), lambda h, j: (0, h)), pl.BlockSpec((2 * TQ, HP), lambda h, j: (j, h)),
                  pl.BlockSpec((2 * TQ, DH), lambda h, j: (j, NH + h)), pl.BlockSpec((S, DH), lambda h, j: (0, h)),
                  pl.BlockSpec((1, 1, S), lambda h, j: (h, 0, 0)), pl.BlockSpec((1, 1, S), lambda h, j: (h, 0, 0))],
        out_specs=[pl.BlockSpec((2 * TQ, HP), lambda h, j: (j, h)), pl.BlockSpec((2 * TQ, DH), lambda h, j: (j, h))],
        out_shape=[jax.ShapeDtypeStruct((S, NH * HP), F32), jax.ShapeDtypeStruct((S, NH * DH), F32)],
        compiler_params=_cparams(("parallel", "arbitrary")))(qp, kp, kv, do, lse_row, delta_row)


def _tm(S, width):
    t = 512 if width <= 1024 else (256 if width <= 3072 else 128)
    return min(t, S)


TM_LIGHT = 1024


def _mod_fwd(tag, x, g, shift, scale):
    S = x.shape[0]
    return _rw_fwd(tag + "_mod", f_mod, [Row(x)], [g, shift, scale], [((D,), BF16)], min(TM_LIGHT, S))[0]


def _mod_bwd(tag, x, g, shift, scale, dh, dx_direct):
    S = x.shape[0]
    r = _rw_bwd(tag + "_mod_b", f_mod, [Row(x)], [g, shift, scale], [Row(dh)], [True], [True] * 3, [((D,), F32)],
                _tm(S, D), add=Row(dx_direct))
    return r[0], r[1:]


def _res_fwd(tag, x, y, gate, coef):
    S = x.shape[0]

    def fn(pieces, bvals):
        return [pieces[0] + coef * bvals[0] * pieces[1]], []
    return _rowwise(tag + "_res", fn, [Row(x), Row(y)], [gate], [((D,), F32)], [], min(TM_LIGHT, S))[0]


def _res_bwd(tag, y, gate, dxn, coef):
    S = y.shape[0]
    r = _rw_bwd(tag + "_res_b", make_f_res(coef), [Row(y)], [gate], [Row(dxn)], [True], [True], [((D,), BF16)],
                min(TM_LIGHT, S))
    return r[0], r[1]


def _ffn_fwd(tag, x, mod3, g, w_in4, w_out4, li):
    shift, scale, gate = mod3
    S = x.shape[0]
    h = _mod_fwd(tag, x, g, shift, scale)
    gu = _matmul(tag + "_in", h, w_in4, lay="b_cols", li=li, out_dtype=BF16)
    a = _rw_fwd(tag + "_act", f_act, [Row(gu, splits=[FF, FF])], [], [((FF,), BF16)], _tm(S, FF))[0]
    y = _matmul(tag + "_out", a, w_out4, lay="b_rows", li=li)
    xn = _res_fwd(tag, x, y, gate, 0.5)
    return xn, (x, h, gu, a, y)


def _ffn_bwd(tag, dxn, res, mod3, g, w_in4, w_out4, li, g_in4, g_out4):
    shift, scale, gate = mod3
    x, h, gu, a, y = res
    S = x.shape[0]
    nmat = w_in4.shape[1]
    dy, dgate = _res_bwd(tag, y, gate, dxn, 0.5)
    da = _matmul(tag + "_out_bi", dy, w_out4, "nt", lay="b_rows", li=li, out_dtype=BF16)
    g_out4 = _matmul(tag + "_out_bw", a, dy, "tn", lay="o_rows", li=li, into=g_out4, nmat=nmat, out_dtype=BF16)
    dgu = _rw_bwd(tag + "_act_b", f_act, [Row(gu, splits=[FF, FF])], [], [Row(da)], [True, True], [],
                  [((FF, FF), BF16)], _tm(S, FF))[0]
    dh = _matmul(tag + "_in_bi", dgu, w_in4, "nt", lay="b_cols", li=li)
    g_in4 = _matmul(tag + "_in_bw", h, dgu, "tn", lay="o_cols", li=li, into=g_in4, nmat=nmat, out_dtype=BF16)
    dx, (dg, dshift, dscale) = _mod_bwd(tag, x, g, shift, scale, dh, dxn)
    return dx, g_in4, g_out4, dict(g=dg, mod=(dshift, dscale, dgate))


def _pad_lanes(a, lo, width=LANE):
    return jnp.pad(a, ((0, 0), (lo, width - lo - a.shape[1])))


def _gdn_layer_fwd(tag, x, mod3, g, p):
    shift, scale, gate = mod3
    S = x.shape[0]
    NC = S // CHUNK
    h = _mod_fwd(tag, x, g, shift, scale)
    proj = _matmul(tag + "_in", h, p["w_in"])
    qc = _conv_fwd(tag + "_conv", proj, p["conv_w8"], 3 * D, _tm(S, 3 * D))
    q, k, v = _rw_fwd(tag + "_pre", f_gdnpre, [Row(qc, splits=[DH] * (3 * NH))], [],
                      [((DH,) * NH, F32)] * 3, _tm(S, 3 * D))
    betaf, gf = _rw_fwd(tag + "_gates", f_gates, [Row(proj, LANE, cb=GATE_CB)], [p["a_log128"], p["dt_bias128"]],
                        [((LANE,), F32)] * 2, _tm(S, LANE))
    bcol, gcol = betaf[:, :NH], gf[:, NH:2 * NH]
    grow = gcol.reshape(NC, CHUNK, NH).transpose(0, 2, 1)
    o, ssave, tsave = _gdn_fwd(tag + "_core", q, k, v, gcol, grow, bcol)
    on = _rw_fwd(tag + "_post", f_gdnpost, [Row(o, splits=[DH] * NH), Row(proj, D, cb=3, splits=[DH] * NH)],
                 [p["norm_g"]], [((DH,) * NH, BF16)], _tm(S, 2 * D))[0]
    y = _matmul(tag + "_out", on, p["w_out"])
    xn = _res_fwd(tag, x, y, gate, 1.0)
    return xn, (x, h, proj, qc, q, k, v, gcol, grow, bcol, ssave, tsave, o, on, y)


def _gdn_layer_bwd(tag, dxn, res, mod3, g, p):
    shift, scale, gate = mod3
    x, h, proj, qc, q, k, v, gcol, grow, bcol, ssave, tsave, o, on, y = res
    S = x.shape[0]
    dy, dgate = _res_bwd(tag, y, gate, dxn, 1.0)
    don = _matmul(tag + "_out_bi", dy, p["w_out"], "nt")
    dw_out = _matmul(tag + "_out_bw", on, dy, "tn")
    do, dz, dnorm = _rw_bwd(tag + "_post_b", f_gdnpost, [Row(o, splits=[DH] * NH), Row(proj, D, cb=3, splits=[DH] * NH)],
                            [p["norm_g"]], [Row(don, splits=[DH] * NH)], [True] * (2 * NH), [True],
                            [((DH,) * NH, F32), ((DH,) * NH, BF16)], _tm(S, 2 * D))
    dq, dk, dv, dgc, dgr, db = _gdn_bwd(tag + "_core_b", q, k, v, gcol, grow, bcol, ssave, tsave, do)
    dgcol = dgc + dgr.transpose(0, 2, 1).reshape(S, NH)
    dgates, da_log, ddt = _rw_bwd(tag + "_gates_b", f_gates, [Row(proj, LANE, cb=GATE_CB)], [p["a_log128"], p["dt_bias128"]],
                                  [Row(_pad_lanes(db, 0)), Row(_pad_lanes(dgcol, NH))], [True], [True, True],
                                  [((LANE,), BF16)], _tm(S, LANE))
    dqc = _rw_bwd(tag + "_pre_b", f_gdnpre, [Row(qc, splits=[DH] * (3 * NH))], [],
                  [Row(dq, splits=[DH] * NH), Row(dk, splits=[DH] * NH), Row(dv, splits=[DH] * NH)],
                  [True] * (3 * NH), [], [((DH,) * (3 * NH), F32)], _tm(S, 3 * D))[0]
    dqkv, dconv = _conv_bwd(tag + "_conv_b", proj, dqc, p["conv_w8"], 3 * D, _tm(S, 3 * D), BF16)
    dproj = jnp.concatenate([dqkv, dz, dgates], axis=1)
    dh = _matmul(tag + "_in_bi", dproj, p["w_in"], "nt")
    dw_in = _matmul(tag + "_in_bw", h, dproj, "tn")
    dx, (dg, dshift, dscale) = _mod_bwd(tag, x, g, shift, scale, dh, dxn)
    return dx, dict(w_in=dw_in, conv_w8=dconv, a_log128=da_log, dt_bias128=ddt, norm_g=dnorm,
                    w_out=dw_out, g=dg, mod=(dshift, dscale, dgate))


def _qk_rows(src, shared_rope, ckv=None):
    if shared_rope:
        return [Row(src, D, cb=0, splits=[DH] * NH), Row(ckv, LANE, cb=2)]
    return [Row(src, splits=[DH] * (2 * NH))]


def _kv_fwd(x, kvmod, p, tabs):
    shift, scale = kvmod
    S = x.shape[0]
    h = _mod_fwd("kv", x, p["kv_norm_g"], shift, scale)
    ckv = _matmul("kv_dkv", h, p["w_dkv"])
    lat = _rw_fwd("kv_lat", f_rms, [Row(ckv, KVL)], [p["kv_lat_g"]], [((KVL,), BF16)], _tm(S, KVL))[0]
    kvf = _matmul("kv_ukv", lat, p["w_ukv"])
    kp = _rw_fwd("kv_k", make_f_qk(True), _qk_rows(kvf, True, ckv) + [Row(tabs[0]), Row(tabs[1])],
                 [p["k_gn"], p["k_gr"], p["pm"]], [((DH,) * (2 * NH), BF16)], _tm(S, 2 * D))[0]
    return kp, kvf, (x, h, ckv, lat)


def _kv_bwd(dkp, dv, dx_direct, res, kvmod, kvf, p, tabs):
    shift, scale = kvmod
    x, h, ckv, lat = res
    S = x.shape[0]
    dkn, dkr, dgn, dgr = _rw_bwd("kv_k_b", make_f_qk(True), _qk_rows(kvf, True, ckv) + [Row(tabs[0]), Row(tabs[1])],
                                 [p["k_gn"], p["k_gr"], p["pm"]], [Row(dkp, splits=[DH] * (2 * NH))],
                                 [True] * (NH + 1) + [False, False], [True, True, False],
                                 [((DH,) * NH, BF16), ((LANE,), BF16)], _tm(S, 2 * D))
    dkvf = jnp.concatenate([dkn, dv.astype(BF16)], axis=1)
    dlat = _matmul("kv_ukv_bi", dkvf, p["w_ukv"], "nt")
    dw_ukv = _matmul("kv_ukv_bw", lat, dkvf, "tn")
    dcl, dlg = _rw_bwd("kv_lat_b", f_rms, [Row(ckv, KVL)], [p["kv_lat_g"]], [Row(dlat)], [True], [True],
                       [((KVL,), BF16)], _tm(S, KVL))
    dckv = jnp.concatenate([dcl, dkr], axis=1)
    dh = _matmul("kv_dkv_bi", dckv, p["w_dkv"], "nt")
    dw_dkv = _matmul("kv_dkv_bw", h, dckv, "tn")
    dx, (dg, dshift, dscale) = _mod_bwd("kv", x, p["kv_norm_g"], shift, scale, dh, dx_direct)
    return dx, dict(w_dkv=dw_dkv, w_ukv=dw_ukv, kv_lat_g=dlg, k_gn=dgn, k_gr=dgr, kv_norm_g=dg, mod=(dshift, dscale))


def _mla_layer_fwd(tag, x, mod3, g, p, kp, kvf, tabs):
    shift, scale, gate = mod3
    S = x.shape[0]
    h = _mod_fwd(tag, x, g, shift, scale)
    ql = _matmul(tag + "_dq", h, p["w_dq"])
    qln = _rw_fwd(tag + "_qln", f_rms, [Row(ql)], [p["ql_g"]], [((QL,), BF16)], _tm(S, QL))[0]
    qu = _matmul(tag + "_uq", qln, p["w_uq"])
    qp = _rw_fwd(tag + "_q", make_f_qk(False), _qk_rows(qu, False) + [Row(tabs[0]), Row(tabs[1])],
                 [p["q_gn"], p["q_gr"], p["pm"]], [((DH,) * (2 * NH), BF16)], _tm(S, 2 * D))[0]
    o, lse = _flash_fwd(tag + "_att", qp, kp, kvf)
    y = _matmul(tag + "_out", o, p["w_out"])
    xn = _res_fwd(tag, x, y, gate, 1.0)
    return xn, (x, h, ql, qln, qu, qp, o, lse, y)


def _mla_layer_bwd(tag, dxn, res, mod3, g, p, kp, kvf, tabs):
    shift, scale, gate = mod3
    x, h, ql, qln, qu, qp, o, lse, y = res
    S = x.shape[0]
    dy, dgate = _res_bwd(tag, y, gate, dxn, 1.0)
    do = _matmul(tag + "_out_bi", dy, p["w_out"], "nt")
    dw_out = _matmul(tag + "_out_bw", o, dy, "tn")
    dqp, delta = _flash_bwd_dq(tag + "_att_bq", qp, kp, kvf, o, do, lse)
    dkp, dv = _flash_bwd_dkv(tag + "_att_bkv", qp, kp, kvf, do, lse.reshape(NH, 1, S), delta.reshape(NH, 1, S))
    dqu, dgn, dgr = _rw_bwd(tag + "_q_b", make_f_qk(False), _qk_rows(qu, False) + [Row(tabs[0]), Row(tabs[1])],
                            [p["q_gn"], p["q_gr"], p["pm"]], [Row(dqp, splits=[DH] * (2 * NH))],
                            [True] * (2 * NH) + [False, False], [True, True, False],
                            [((DH,) * (2 * NH), BF16)], _tm(S, 2 * D))
    dqln = _matmul(tag + "_uq_bi", dqu, p["w_uq"], "nt")
    dw_uq = _matmul(tag + "_uq_bw", qln, dqu, "tn")
    dql, dqlg = _rw_bwd(tag + "_qln_b", f_rms, [Row(ql)], [p["ql_g"]], [Row(dqln)], [True], [True], [((QL,), BF16)],
                        _tm(S, QL))
    dh = _matmul(tag + "_dq_bi", dql, p["w_dq"], "nt")
    dw_dq = _matmul(tag + "_dq_bw", h, dql, "tn")
    dx, (dg, dshift, dscale) = _mod_bwd(tag, x, g, shift, scale, dh, dxn)
    return dx, dkp, dv, dict(w_dq=dw_dq, w_uq=dw_uq, w_out=dw_out, ql_g=dqlg, q_gn=dgn, q_gr=dgr, g=dg,
                             mod=(dshift, dscale, dgate))


def _loss_head(y, tgt):
    S = y.shape[0]

    def fn(pieces, bvals):
        e = pieces[0] - pieces[1]
        part = jnp.sum(e * e) * (0.5 / D)
        return [e * (1.0 / D)], [jnp.full((1, LANE), part, F32)]
    dy, part = _rowwise("loss", fn, [Row(y), Row(tgt)], [], [((D,), F32)], [(1, LANE)], _tm(S, D))
    return part[0, 0], dy


def _rope_tables(positions):
    S = positions.shape[0]
    half = ROPE // 2
    lane = lax.broadcasted_iota(jnp.int32, (1, LANE), 1)
    inv_freq = ROPE_BASE ** (-(lane % half).astype(F32) / half)
    live = (lane < ROPE).astype(F32)
    sign = jnp.where(lane < half, -1.0, 1.0) * live

    def fn(pieces, bvals):
        ang = pieces[0] * bvals[0]
        return [jnp.cos(ang) * bvals[1], jnp.sin(ang) * bvals[2]], []
    pos = jnp.broadcast_to(positions.astype(F32)[:, None], (S, LANE))
    cosp, sins = _rowwise("rope_tab", fn, [Row(pos)], [inv_freq, live, sign], [((LANE,), F32)] * 2, [], _tm(S, LANE))
    r = lax.broadcasted_iota(jnp.int32, (LANE, LANE), 0)
    c = lax.broadcasted_iota(jnp.int32, (LANE, LANE), 1)
    pm = (((c < half) & (r == c + half)) | ((c >= half) & (c < ROPE) & (r == c - half))).astype(F32)
    return (cosp, sins), pm


def _adamw(name, w, g, m, v):
    shape = w.shape
    C = shape[-1]
    R = w.size // C
    tr = R
    for t in (1024, 512, 256, 128, 64, 32, 16, 8):
        if R % t == 0 and t * C * 4 <= (1 << 21):
            tr = t
            break
    c1 = 1.0 - ADAM_B1 ** ADAM_STEP
    c2 = 1.0 - ADAM_B2 ** ADAM_STEP

    def body(w_ref, g_ref, m_ref, v_ref, d_ref, mo_ref, vo_ref):
        gg = g_ref[...]
        mn = ADAM_B1 * m_ref[...] + (1.0 - ADAM_B1) * gg
        vn = ADAM_B2 * v_ref[...] + (1.0 - ADAM_B2) * (gg * gg)
        d_ref[...] = -ADAM_LR * ((mn / c1) / (jnp.sqrt(vn / c2) + ADAM_EPS) + ADAM_WD * w_ref[...])
        mo_ref[...] = mn
        vo_ref[...] = vn

    spec = pl.BlockSpec((tr, C), lambda i: (i, 0))
    outs = pl.pallas_call(body, name=name, grid=(R // tr,), in_specs=[spec] * 4, out_specs=[spec] * 3,
                          out_shape=[jax.ShapeDtypeStruct((R, C), F32)] * 3,
                          compiler_params=_cparams(("parallel",)))(*[t.reshape(R, C) for t in (w, g, m, v)])
    return [o.reshape(shape) for o in outs]


HBM_SPEC = pl.BlockSpec(memory_space=pltpu.HBM)
OTHER_CHIPS = (4, 2, 6)
SIBLING = 1


def _me():
    return lax.axis_index("x"), lax.axis_index("y"), lax.axis_index("c")


def _peer(me, k):
    mx, my, mc = me
    return ((1 - mx) if k & 4 else mx, (1 - my) if k & 2 else my, (1 - mc) if k & 1 else mc)


def _rcopy(src, dst, ssem, rsem, to):
    return pltpu.make_async_remote_copy(src_ref=src, dst_ref=dst, send_sem=ssem, recv_sem=rsem, device_id=to,
                                        device_id_type=MESH)


def _all_gather8(name, x):
    def body(x_ref, o_ref, ssem, rsem, lsem):
        me = _me()
        mine = 4 * me[0] + 2 * me[1] + me[2]
        loc = pltpu.make_async_copy(x_ref, o_ref.at[mine], lsem)
        loc.start()
        sends = []
        for k in range(1, 8):
            cp = _rcopy(x_ref, o_ref.at[mine], ssem.at[k - 1], rsem.at[k - 1], _peer(me, k))
            cp.start()
            sends.append(cp)
        for k in range(1, 8):
            px, py, pc = _peer(me, k)
            _rcopy(x_ref, o_ref.at[4 * px + 2 * py + pc], ssem.at[k - 1], rsem.at[k - 1], (px, py, pc)).wait_recv()
        for cp in sends:
            cp.wait_send()
        loc.wait()

    return pl.pallas_call(body, name=name, out_shape=jax.ShapeDtypeStruct((8,) + x.shape, x.dtype),
                          in_specs=[HBM_SPEC], out_specs=HBM_SPEC,
                          scratch_shapes=[pltpu.SemaphoreType.DMA((7,)), pltpu.SemaphoreType.DMA((7,)),
                                          pltpu.SemaphoreType.DMA(())])(x)


PACK_L = 1024
PACK_RT = 256


def _place_shard(name, wp, chip):
    rh, ln = wp.shape[1:]

    def body(s_ref, w_ref, o_ref):
        o_ref[...] = w_ref[...]

    gs = pltpu.PrefetchScalarGridSpec(
        num_scalar_prefetch=1, grid=(2, rh // PACK_RT),
        in_specs=[pl.BlockSpec((None, PACK_RT, ln), lambda h, i, s_ref: (h, i, 0))],
        out_specs=pl.BlockSpec((None, None, PACK_RT, ln), lambda h, i, s_ref: (s_ref[0], h, i, 0)))
    return pl.pallas_call(body, name=name, grid_spec=gs, out_shape=jax.ShapeDtypeStruct((4,) + wp.shape, wp.dtype),
                          compiler_params=_cparams(("parallel", "parallel")))(chip.reshape(1).astype(jnp.int32), wp)


def _gather_weights(name, w4):
    r2 = w4.shape[2] // 2

    def body(w_ref, o_ref, ssem, rsem):
        me = _me()
        mc = me[2]
        px, py, pd, sib = _peer(me, 4), _peer(me, 2), _peer(me, 6), _peer(me, SIBLING)
        chip = lambda p: 2 * p[0] + p[1]
        mine, from_x, from_y, from_d = (o_ref.at[chip(p), mc] for p in (me, px, py, pd))
        q0, q1 = pl.ds(0, r2), pl.ds(r2, r2)
        sends = [_rcopy(mine, mine, ssem.at[0], rsem.at[0], px), _rcopy(mine, mine, ssem.at[1], rsem.at[1], py)]
        for cp in sends:
            cp.start()
        _rcopy(from_x, from_x, ssem.at[0], rsem.at[0], px).wait_recv()
        sends += [_rcopy(from_x.at[q1], from_x.at[q1], ssem.at[2], rsem.at[2], py),
                  _rcopy(from_x, from_x, ssem.at[4], rsem.at[4], sib)]
        sends[-2].start()
        sends[-1].start()
        _rcopy(from_y, from_y, ssem.at[1], rsem.at[1], py).wait_recv()
        sends += [_rcopy(from_y.at[q0], from_y.at[q0], ssem.at[3], rsem.at[3], px),
                  _rcopy(from_y, from_y, ssem.at[5], rsem.at[5], sib)]
        sends[-2].start()
        sends[-1].start()
        _rcopy(from_d.at[q0], from_d.at[q0], ssem.at[3], rsem.at[3], px).wait_recv()
        _rcopy(from_d.at[q1], from_d.at[q1], ssem.at[2], rsem.at[2], py).wait_recv()
        sends.append(_rcopy(from_d, from_d, ssem.at[6], rsem.at[6], sib))
        sends[-1].start()
        for j, p in enumerate((px, py, pd)):
            land = o_ref.at[chip(p), 1 - mc]
            _rcopy(land, land, ssem.at[4 + j], rsem.at[4 + j], sib).wait_recv()
        for cp in sends:
            cp.wait_send()

    return pl.pallas_call(body, name=name, out_shape=jax.ShapeDtypeStruct(w4.shape, w4.dtype),
                          in_specs=[HBM_SPEC], out_specs=HBM_SPEC, input_output_aliases={0: 0},
                          scratch_shapes=[pltpu.SemaphoreType.DMA((7,)), pltpu.SemaphoreType.DMA((7,))])(w4)


def _exchange_half(name, g):
    def body(g_ref, p_ref, ssem, rsem):
        me = _me()
        cps = []
        for s in range(4):
            cp = _rcopy(g_ref.at[s, 1 - me[2]], p_ref.at[s], ssem.at[s], rsem.at[s], _peer(me, SIBLING))
            cp.start()
            cps.append(cp)
        for cp in cps:
            cp.wait()

    return pl.pallas_call(body, name=name, out_shape=jax.ShapeDtypeStruct((4,) + g.shape[2:], g.dtype),
                          in_specs=[HBM_SPEC], out_specs=HBM_SPEC,
                          scratch_shapes=[pltpu.SemaphoreType.DMA((4,)), pltpu.SemaphoreType.DMA((4,))])(g)


def _scatter_chips(name, q):
    r2 = q.shape[1] // 2

    def body(q_ref, t_ref, relay, ssem, rsem):
        me = _me()
        px, py, pd = _peer(me, 4), _peer(me, 2), _peer(me, 6)
        chip = lambda p: 2 * p[0] + p[1]
        q0, q1 = pl.ds(0, r2), pl.ds(r2, r2)
        sends = [_rcopy(q_ref.at[chip(px)], t_ref.at[0], ssem.at[0], rsem.at[0], px),
                 _rcopy(q_ref.at[chip(py)], t_ref.at[1], ssem.at[1], rsem.at[1], py),
                 _rcopy(q_ref.at[chip(pd), q0], relay.at[0], ssem.at[2], rsem.at[2], py),
                 _rcopy(q_ref.at[chip(pd), q1], relay.at[1], ssem.at[3], rsem.at[3], px)]
        for cp in sends:
            cp.start()
        _rcopy(relay.at[0], relay.at[0], ssem.at[2], rsem.at[2], py).wait_recv()
        sends.append(_rcopy(relay.at[0], t_ref.at[2, q0], ssem.at[4], rsem.at[4], px))
        sends[-1].start()
        _rcopy(relay.at[1], relay.at[1], ssem.at[3], rsem.at[3], px).wait_recv()
        sends.append(_rcopy(relay.at[1], t_ref.at[2, q1], ssem.at[5], rsem.at[5], py))
        sends[-1].start()
        _rcopy(t_ref.at[0], t_ref.at[0], ssem.at[0], rsem.at[0], px).wait_recv()
        _rcopy(t_ref.at[1], t_ref.at[1], ssem.at[1], rsem.at[1], py).wait_recv()
        _rcopy(t_ref.at[2, q0], t_ref.at[2, q0], ssem.at[4], rsem.at[4], px).wait_recv()
        _rcopy(t_ref.at[2, q1], t_ref.at[2, q1], ssem.at[5], rsem.at[5], py).wait_recv()
        for cp in sends:
            cp.wait_send()

    return pl.pallas_call(body, name=name,
                          out_shape=[jax.ShapeDtypeStruct((3,) + q.shape[1:], q.dtype),
                                     jax.ShapeDtypeStruct((2, r2) + q.shape[2:], q.dtype)],
                          in_specs=[HBM_SPEC], out_specs=[HBM_SPEC, HBM_SPEC],
                          scratch_shapes=[pltpu.SemaphoreType.DMA((6,)), pltpu.SemaphoreType.DMA((6,))])(q)[0]


def _exchange_full(name, r2):
    def body(r_ref, o_ref, ssem, rsem):
        me = _me()
        mc = me[2]
        cp = _rcopy(o_ref.at[mc], o_ref.at[mc], ssem, rsem, _peer(me, SIBLING))
        cp.start()
        _rcopy(o_ref.at[1 - mc], o_ref.at[1 - mc], ssem, rsem, _peer(me, SIBLING)).wait_recv()
        cp.wait_send()

    return pl.pallas_call(body, name=name, out_shape=jax.ShapeDtypeStruct(r2.shape, r2.dtype),
                          in_specs=[HBM_SPEC], out_specs=HBM_SPEC, input_output_aliases={0: 0},
                          scratch_shapes=[pltpu.SemaphoreType.DMA(()), pltpu.SemaphoreType.DMA(())])(r2)


def _add_half(name, g, p, c):
    rh, ln = g.shape[2:]

    def body(c_ref, g_ref, p_ref, o_ref):
        o_ref[0] = (g_ref[0, 0].astype(F32) + p_ref[0].astype(F32)).astype(o_ref.dtype)

    gs = pltpu.PrefetchScalarGridSpec(
        num_scalar_prefetch=1, grid=(4, rh // PACK_RT),
        in_specs=[pl.BlockSpec((1, 1, PACK_RT, ln), lambda s, i, c_ref: (s, c_ref[0], i, 0)),
                  pl.BlockSpec((1, PACK_RT, ln), lambda s, i, c_ref: (s, i, 0))],
        out_specs=pl.BlockSpec((1, PACK_RT, ln), lambda s, i, c_ref: (s, i, 0)))
    return pl.pallas_call(body, name=name, grid_spec=gs, out_shape=jax.ShapeDtypeStruct((4, rh, ln), BF16),
                          compiler_params=_cparams(("parallel", "parallel")))(c.reshape(1).astype(jnp.int32), g, p)


def _add_chips(name, q, t, chip, c):
    rh, ln = q.shape[1:]

    def body(s_ref, c_ref, q_ref, t_ref, o_ref):
        o_ref[...] = ((q_ref[0].astype(F32) + t_ref[0].astype(F32)) + t_ref[1].astype(F32)) + t_ref[2].astype(F32)

    gs = pltpu.PrefetchScalarGridSpec(
        num_scalar_prefetch=2, grid=(rh // PACK_RT,),
        in_specs=[pl.BlockSpec((1, PACK_RT, ln), lambda i, s_ref, c_ref: (s_ref[0], i, 0)),
                  pl.BlockSpec((3, PACK_RT, ln), lambda i, s_ref, c_ref: (0, i, 0))],
        out_specs=pl.BlockSpec((None, PACK_RT, ln), lambda i, s_ref, c_ref: (c_ref[0], i, 0)))
    return pl.pallas_call(body, name=name, grid_spec=gs, out_shape=jax.ShapeDtypeStruct((2, rh, ln), F32),
                          compiler_params=_cparams(("parallel",)))(chip.reshape(1).astype(jnp.int32),
                                                                    c.reshape(1).astype(jnp.int32), q, t)


def _sum8(name, a):
    def body(a_ref, o_ref):
        acc = a_ref[0]
        for d in range(1, 8):
            acc = acc + a_ref[d]
        o_ref[...] = acc
    return pl.pallas_call(body, name=name, out_shape=jax.ShapeDtypeStruct(a.shape[1:], F32))(a)


def _silu_rows(name, a):
    def body(a_ref, o_ref):
        o_ref[...] = _silu(a_ref[...])
    return pl.pallas_call(body, name=name, out_shape=jax.ShapeDtypeStruct(a.shape, F32))(a)


REST = (("gdn_w_out", 1), ("mla_w_dkv", 0), ("mla_w_ukv", 1), ("mla_w_dq", 1), ("mla_w_uq", 2), ("mla_w_out", 1))


def _packed_rows(n):
    per_half = -(-n // (2 * PACK_L))
    return -(-per_half // PACK_RT) * PACK_RT


def _pack_flat(flat):
    n = flat.shape[-1]
    rh = _packed_rows(n)
    pad = [(0, 0)] * (flat.ndim - 1) + [(0, 2 * rh * PACK_L - n)]
    return jnp.pad(flat, pad).reshape(flat.shape[:-1] + (2, rh, PACK_L))


def _shards_first(full, axis):
    sh = full.shape
    t = full.reshape(sh[:axis] + (4, sh[axis] // 4) + sh[axis + 1:])
    return jnp.moveaxis(t, axis, 0)


def _shards_merge(stacked, axis):
    t = jnp.moveaxis(stacked, 0, axis)
    sh = t.shape
    return t.reshape(sh[:axis] + (4 * sh[axis + 1],) + sh[axis + 2:])


def _pack_small(parts):
    flat = jnp.concatenate([p.reshape(-1).astype(F32) for p in parts])
    n = flat.shape[0]
    rows = -(-n // (SUB * LANE)) * SUB
    return jnp.pad(flat, (0, rows * LANE - n)).reshape(rows, LANE)


def _unpack_small(buf, shapes):
    lead = buf.shape[:-2]
    flat = buf.reshape(lead + (-1,))
    out, off = [], 0
    for sh in shapes:
        n = 1
        for d in sh:
            n *= d
        out.append(flat[..., off:off + n].reshape(lead + tuple(sh)))
        off += n
    return out


WEIGHTS = ('ada_w', 'ada_b', 'norm_g', 'ffn_w_in', 'ffn_w_out', 'gdn_w_in', 'gdn_conv_w', 'gdn_a_log', 'gdn_dt_bias',
           'gdn_norm_g', 'gdn_w_out', 'kv_ada_w', 'kv_ada_b', 'kv_norm_g', 'mla_w_dkv', 'mla_kv_norm_g', 'mla_w_ukv',
           'mla_k_norm_g', 'mla_w_dq', 'mla_q_lora_norm_g', 'mla_w_uq', 'mla_q_norm_g', 'mla_w_out')
ARGS = ('x', 'c', 'positions') + WEIGHTS + ('loss_target',) + tuple('m_' + n for n in WEIGHTS) + tuple('v_' + n for n in WEIGHTS)


def _split_norm(v):
    return v[None, :DH], _pad_lanes(v[None, DH:], 0)


def _join_norm(gn, gr):
    return jnp.concatenate([gn[0], gr[0, :ROPE]])


def _step(x, tgt, pos, mods, kvmod, W, P):
    tabs, pm = _rope_tables(pos)
    m3 = lambda l, i: tuple(mods[l][3 * i + j][None] for j in range(3))
    ng = lambda l, i: P["norm_g"][l, i][None]
    gdn_p, mla_p = [], []
    for l in range(2):
        gdn_p.append(dict(w_in=jnp.pad(W["gdn_w_in"][l], ((0, 0), (0, GDN_IN - W["gdn_w_in"].shape[2]))),
                          conv_w8=jnp.pad(P["gdn_conv_w"][l], ((0, 4), (0, 0))),
                          a_log128=_pad_lanes(P["gdn_a_log"][l][None], NH), dt_bias128=_pad_lanes(P["gdn_dt_bias"][l][None], NH),
                          norm_g=P["gdn_norm_g"][l][None], w_out=W["gdn_w_out"][l]))
        q_gn, q_gr = _split_norm(P["mla_q_norm_g"][l])
        mla_p.append(dict(w_dq=W["mla_w_dq"][l], ql_g=P["mla_q_lora_norm_g"][l][None],
                          w_uq=jnp.pad(W["mla_w_uq"][l].reshape(QL, NH, QKH), ((0, 0), (0, 0), (0, HP - QKH))).reshape(QL, NH * HP),
                          q_gn=q_gn, q_gr=q_gr, pm=pm, w_out=W["mla_w_out"][l]))
    k_gn, k_gr = _split_norm(P["mla_k_norm_g"])
    kv_p = dict(kv_norm_g=P["kv_norm_g"][None], w_dkv=jnp.pad(W["mla_w_dkv"], ((0, 0), (0, QL - KVL - ROPE))),
                kv_lat_g=P["mla_kv_norm_g"][None],
                w_ukv=W["mla_w_ukv"].reshape(KVL, NH, 2, DH).transpose(0, 2, 1, 3).reshape(KVL, 2 * NH * DH),
                k_gn=k_gn, k_gr=k_gr, pm=pm)
    kvm = (kvmod[0][None], kvmod[1][None])

    res = {}
    for l in range(4):
        x, res[l, 0] = _ffn_fwd(f"l{l}a", x, m3(l, 0), ng(l, 0), W["ffn_w_in"], W["ffn_w_out"], 2 * l)
        if l < 2:
            x, res[l, 1] = _gdn_layer_fwd(f"l{l}g", x, m3(l, 1), ng(l, 1), gdn_p[l])
        else:
            x, res[l, 1] = _mla_layer_fwd(f"l{l}m", x, m3(l, 1), ng(l, 1), mla_p[l - 2], kp, kvf, tabs)
        x, res[l, 2] = _ffn_fwd(f"l{l}b", x, m3(l, 2), ng(l, 2), W["ffn_w_in"], W["ffn_w_out"], 2 * l + 1)
        if l == 1:
            kp, kvf, kres = _kv_fwd(x, kvm, kv_p, tabs)
    loss, dx = _loss_head(x, tgt)

    gw = {n: [None] * W[n].shape[0] for n in ("gdn_w_in", "gdn_w_out", "mla_w_dq", "mla_w_uq", "mla_w_out")}
    g_in4 = g_out4 = None
    gp = {n: [None] * 2 for n in ("gdn_conv_w", "gdn_a_log", "gdn_dt_bias", "gdn_norm_g", "mla_q_lora_norm_g", "mla_q_norm_g")}
    gnorm = [[None] * 3 for _ in range(4)]
    dmod = [[None] * NMOD for _ in range(4)]
    dkp = dv = None
    for l in (3, 2, 1, 0):
        if l == 1:
            dx, gk = _kv_bwd(dkp, dv, dx, kres, kvm, kvf, kv_p, tabs)
        for i in (2, 1, 0):
            if i != 1:
                dx, g_in4, g_out4, gd = _ffn_bwd(f"l{l}{'ab'[i // 2]}", dx, res[l, i], m3(l, i), ng(l, i), W["ffn_w_in"],
                                                 W["ffn_w_out"], 2 * l + i // 2, g_in4, g_out4)
            elif l < 2:
                dx, gd = _gdn_layer_bwd(f"l{l}g", dx, res[l, 1], m3(l, 1), ng(l, 1), gdn_p[l])
                gw["gdn_w_in"][l] = gd["w_in"][:, :W["gdn_w_in"].shape[2]]
                gw["gdn_w_out"][l] = gd["w_out"]
                gp["gdn_conv_w"][l] = gd["conv_w8"][:4]
                gp["gdn_a_log"][l] = gd["a_log128"][0, NH:2 * NH]
                gp["gdn_dt_bias"][l] = gd["dt_bias128"][0, NH:2 * NH]
                gp["gdn_norm_g"][l] = gd["norm_g"][0]
            else:
                dx, dkp_l, dv_l, gd = _mla_layer_bwd(f"l{l}m", dx, res[l, 1], m3(l, 1), ng(l, 1), mla_p[l - 2], kp, kvf, tabs)
                dkp = dkp_l if dkp is None else dkp + dkp_l
                dv = dv_l if dv is None else dv + dv_l
                gw["mla_w_dq"][l - 2], gw["mla_w_out"][l - 2] = gd["w_dq"], gd["w_out"]
                gw["mla_w_uq"][l - 2] = gd["w_uq"].reshape(QL, NH, HP)[:, :, :QKH].reshape(QL, NH * QKH)
                gp["mla_q_lora_norm_g"][l - 2] = gd["ql_g"][0]
                gp["mla_q_norm_g"][l - 2] = _join_norm(gd["q_gn"], gd["q_gr"])
            gnorm[l][i] = gd["g"][0]
            for j in range(3):
                dmod[l][3 * i + j] = gd["mod"][j][0]
    gwf = {n: jnp.stack(v) for n, v in gw.items()}
    gwf["ffn_w_in"], gwf["ffn_w_out"] = g_in4, g_out4
    gwf["mla_w_dkv"] = gk["w_dkv"][:, :KVL + ROPE]
    gwf["mla_w_ukv"] = gk["w_ukv"].reshape(KVL, 2, NH, DH).transpose(0, 2, 1, 3).reshape(KVL, 2 * NH * DH)
    gpf = {n: jnp.stack(v) for n, v in gp.items()}
    gpf["norm_g"] = jnp.stack([jnp.stack(r) for r in gnorm])
    gpf["kv_norm_g"] = gk["kv_norm_g"][0]
    gpf["mla_kv_norm_g"] = gk["kv_lat_g"][0]
    gpf["mla_k_norm_g"] = _join_norm(gk["k_gn"], gk["k_gr"])
    dmods = jnp.stack([jnp.stack(r) for r in dmod])
    dkvmod = jnp.stack([gk["mod"][0][0], gk["mod"][1][0]])
    return loss, dx, gwf, gpf, dmods, dkvmod


SMALL = ("norm_g", "gdn_conv_w", "gdn_a_log", "gdn_dt_bias", "gdn_norm_g", "kv_norm_g", "mla_kv_norm_g", "mla_k_norm_g",
         "mla_q_lora_norm_g", "mla_q_norm_g")


def kernel(x, c, positions, ada_w, ada_b, norm_g, ffn_w_in, ffn_w_out, gdn_w_in, gdn_conv_w, gdn_a_log, gdn_dt_bias,
           gdn_norm_g, gdn_w_out, kv_ada_w, kv_ada_b, kv_norm_g, mla_w_dkv, mla_kv_norm_g, mla_w_ukv, mla_k_norm_g,
           mla_w_dq, mla_q_lora_norm_g, mla_w_uq, mla_q_norm_g, mla_w_out, loss_target, m_ada_w, m_ada_b, m_norm_g,
           m_ffn_w_in, m_ffn_w_out, m_gdn_w_in, m_gdn_conv_w, m_gdn_a_log, m_gdn_dt_bias, m_gdn_norm_g, m_gdn_w_out,
           m_kv_ada_w, m_kv_ada_b, m_kv_norm_g, m_mla_w_dkv, m_mla_kv_norm_g, m_mla_w_ukv, m_mla_k_norm_g, m_mla_w_dq,
           m_mla_q_lora_norm_g, m_mla_w_uq, m_mla_q_norm_g, m_mla_w_out, v_ada_w, v_ada_b, v_norm_g, v_ffn_w_in,
           v_ffn_w_out, v_gdn_w_in, v_gdn_conv_w, v_gdn_a_log, v_gdn_dt_bias, v_gdn_norm_g, v_gdn_w_out, v_kv_ada_w,
           v_kv_ada_b, v_kv_norm_g, v_mla_w_dkv, v_mla_kv_norm_g, v_mla_w_ukv, v_mla_k_norm_g, v_mla_w_dq,
           v_mla_q_lora_norm_g, v_mla_w_uq, v_mla_q_norm_g, v_mla_w_out):
    a = dict(locals())
    mx, my, mc = _me()
    dev = 4 * mx + 2 * my + mc
    chip = 2 * mx + my
    x, tgt, pos = a["x"][0], a["loss_target"][0], a["positions"][0]
    take = lambda arr, i, axis=0: lax.dynamic_index_in_dim(arr, i, axis, keepdims=False)

    pre = _all_gather8("ag_pre", _pack_small([a["c"], a["gdn_conv_w"], a["norm_g"]]))
    c_all, conv_sh, norm_sh = _unpack_small(pre, [(D,), a["gdn_conv_w"].shape, a["norm_g"].shape])
    P = {n: a[n] for n in SMALL}
    P["gdn_conv_w"] = jnp.concatenate([conv_sh[2 * s] for s in range(4)], axis=2)
    P["norm_g"] = jnp.concatenate([norm_sh[2 * s] for s in range(4)], axis=2)
    c_act = _silu_rows("c_act", c_all)
    nada = a["ada_w"].shape[2]
    nkv = a["kv_ada_w"].shape[1]
    modp = [_matmul(f"mod{l}", c_act, a["ada_w"], precise=True, lay="b_stack", li=l) for l in range(4)]
    kvp = _matmul("modkv", c_act, a["kv_ada_w"], precise=True)
    mp = _all_gather8("ag_mod", _pack_small(modp + [kvp]))
    modp_all, kvp_all = _unpack_small(mp, [(4, 8, nada), (8, nkv)])
    mods = jnp.concatenate([take(modp_all[2 * s], dev, 1) for s in range(4)], axis=1) + a["ada_b"]
    mods = mods.reshape(4, NMOD, D)
    kvmod = (jnp.concatenate([take(kvp_all[2 * s], dev, 0) for s in range(4)]) + a["kv_ada_b"]).reshape(2, D)

    def gather(tag, w2):
        return _gather_weights("ag_" + tag, _place_shard("own_" + tag, w2, chip))

    def reduce(tag, g4):
        q = _add_half("rsp_" + tag, g4, _exchange_half("rs1_" + tag, g4), mc)
        r2 = _add_chips("rsc_" + tag, q, _scatter_chips("rs2_" + tag, q), chip, mc)
        return _exchange_full("rs3_" + tag, r2)

    halves = lambda t: t.reshape((2, -1) + t.shape[-1:])
    W = {n: gather(t, halves(a[n].astype(BF16))).reshape((4, 8) + a[n].shape[2:])
         for n, t in (("ffn_w_in", "wi"), ("ffn_w_out", "wo"))}
    wg = gather("wg", a["gdn_w_in"].astype(BF16))
    W["gdn_w_in"] = jnp.concatenate([wg[s] for s in range(4)], axis=2)
    wall = gather("wr", _pack_flat(jnp.concatenate([a[n].reshape(-1).astype(BF16) for n, _ in REST]))).reshape(4, -1)
    off = 0
    for n, ax in REST:
        sz = a[n].size
        W[n] = _shards_merge(wall[:, off:off + sz].reshape((4,) + a[n].shape), ax)
        off += sz

    loss, dx, gw, gp, dmods, dkvmod = _step(x, tgt, pos, mods, kvmod, W, P)
    loss = lax.psum(loss, ("x", "y", "c"))

    grads = {n: reduce(t, gw[n].reshape((4, 2, -1) + a[n].shape[-1:])).reshape(a[n].shape)
             for n, t in (("ffn_w_in", "wi"), ("ffn_w_out", "wo"))}
    ng = a["gdn_w_in"].shape[2]
    grads["gdn_w_in"] = reduce("wg", jnp.stack([gw["gdn_w_in"][:, :, s * ng:(s + 1) * ng] for s in range(4)]))
    gsh = reduce("wr", _pack_flat(jnp.concatenate([_shards_first(gw[n], ax).reshape(4, -1) for n, ax in REST], axis=1)))
    gsh = gsh.reshape(-1)
    off = 0
    for n, _ in REST:
        grads[n] = gsh[off:off + a[n].size].reshape(a[n].shape)
        off += a[n].size

    small = _all_gather8("ag_small", _pack_small([dmods, dkvmod] + [gp[n] for n in SMALL]))
    shapes = [(4, NMOD * D), (2 * D,)] + [gp[n].shape for n in SMALL]
    dmod_all, dkv_all = _unpack_small(small, shapes)[:2]
    tot = _unpack_small(_sum8("sum_small", small), shapes)
    grads["ada_b"], grads["kv_ada_b"] = tot[0], tot[1]
    for n, t in zip(SMALL, tot[2:]):
        grads[n] = t
    grads["norm_g"] = lax.dynamic_slice_in_dim(grads["norm_g"], chip * a["norm_g"].shape[2], a["norm_g"].shape[2], 2)
    grads["gdn_conv_w"] = lax.dynamic_slice_in_dim(grads["gdn_conv_w"], chip * a["gdn_conv_w"].shape[2],
                                                   a["gdn_conv_w"].shape[2], 2)
    ca = jnp.pad(c_act, ((0, LANE - 8), (0, 0)))
    dm = jnp.pad(lax.dynamic_slice_in_dim(dmod_all.reshape(8, 4, NMOD * D), chip * nada, nada, 2), ((0, LANE - 8), (0, 0), (0, 0)))
    gada = None
    for l in range(4):
        gada = _matmul(f"gada{l}", ca, dm[:, l], "tn", precise=True, lay="o_stack", li=l, into=gada, nmat=4)
    grads["ada_w"] = gada
    dk = jnp.pad(lax.dynamic_slice_in_dim(dkv_all, chip * nkv, nkv, 1), ((0, LANE - 8), (0, 0)))
    grads["kv_ada_w"] = _matmul("gadakv", ca, dk, "tn", precise=True)

    upd = [_adamw("adamw_" + n, a[n], grads[n], a["m_" + n], a["v_" + n]) for n in WEIGHTS]
    return (loss, dx[None], *[grads[n] for n in WEIGHTS], *[u[0] for u in upd], *[u[1] for u in upd], *[u[2] for u in upd])
```

```python
import functools

import jax
import jax.numpy as jnp
from jax import lax
from jax.experimental import pallas as pl
from jax.experimental.pallas import tpu as pltpu

F32 = jnp.float32
BF16 = jnp.bfloat16
HI = lax.Precision.HIGHEST
MESH = pl.DeviceIdType.MESH

D = 1024
NH = 8
DH = 128
FF = 2816
NMOD = 9
CHUNK = 64
ROPE = 64
QKH = 192
HP = 256
KVL = 256
QL = 384
GDN_IN = 4224
GATE_CB = 32
EPS = 1e-6
ROPE_BASE = 10000.0
LANE = 128
SUB = 8
VMEM_LIMIT = 56 * 1024 * 1024

ADAM_LR, ADAM_B1, ADAM_B2, ADAM_EPS, ADAM_WD, ADAM_STEP = 0.001, 0.9, 0.999, 1e-08, 0.01, 10


def _tile(n, prefs=(512, 384, 256, 128)):
    for p in prefs:
        if n % p == 0:
            return p
    return n


def _cparams(sem):
    return pltpu.CompilerParams(dimension_semantics=sem, vmem_limit_bytes=VMEM_LIMIT)


class Row:
    def __init__(self, arr, width=None, cb=0, splits=None, halo=None):
        self.arr = arr
        self.width = arr.shape[1] if width is None else width
        self.cb = cb
        self.splits = splits
        self.halo = halo


def _rowwise(name, fn, rows, bcs, outs, accs, tm):
    S = rows[0].arr.shape[0]
    n = S // tm
    nr, nb, no, na = len(rows), len(bcs), len(outs), len(accs)

    def body(*refs):
        rrefs, brefs = refs[:nr], refs[nr:nr + nb]
        orefs, arefs = refs[nr + nb:nr + nb + no], refs[nr + nb + no:]
        pieces = []
        for r, ref in zip(rows, rrefs):
            if r.splits is None:
                pieces.append(ref[...])
            else:
                off = 0
                for w in r.splits:
                    pieces.append(ref[:, off:off + w])
                    off += w
        out_pieces, acc_vals = fn(pieces, [b[...] for b in brefs])
        k = 0
        for (widths, dt), oref in zip(outs, orefs):
            off = 0
            for w in widths:
                oref[:, off:off + w] = out_pieces[k].astype(dt)
                k += 1
                off += w
        if na:
            @pl.when(pl.program_id(0) == 0)
            def _():
                for a in arefs:
                    a[...] = jnp.zeros(a.shape, F32)
            for a, v in zip(arefs, acc_vals):
                a[...] += v

    in_specs = []
    for r in rows:
        if r.halo is None:
            in_specs.append(pl.BlockSpec((tm, r.width), lambda i, cb=r.cb: (i, cb)))
        elif r.halo == "prev":
            in_specs.append(pl.BlockSpec((SUB, r.width), lambda i, cb=r.cb: (jnp.maximum(i * (tm // SUB) - 1, 0), cb)))
        else:
            in_specs.append(pl.BlockSpec((SUB, r.width), lambda i, cb=r.cb: (jnp.minimum((i + 1) * (tm // SUB), S // SUB - 1), cb)))
    in_specs += [pl.BlockSpec(b.shape, lambda i, nd=b.ndim: (0,) * nd) for b in bcs]
    out_specs = [pl.BlockSpec((tm, sum(w)), lambda i: (i, 0)) for w, _ in outs]
    out_specs += [pl.BlockSpec(s, lambda i: (0, 0)) for s in accs]
    out_shape = [jax.ShapeDtypeStruct((S, sum(w)), dt) for w, dt in outs]
    out_shape += [jax.ShapeDtypeStruct(s, F32) for s in accs]
    res = pl.pallas_call(body, name=name, grid=(n,), in_specs=in_specs, out_specs=out_specs, out_shape=out_shape,
                         compiler_params=_cparams(("arbitrary",)))(*[r.arr for r in rows], *bcs)
    return res


def _rw_fwd(name, f, rows, bcs, outs, tm):
    def fn(pieces, bvals):
        return list(f(*[p.astype(F32) for p in pieces], *[b.astype(F32) for b in bvals])), []
    return _rowwise(name, fn, rows, bcs, outs, [], tm)


def _npieces(rows):
    return sum(1 if r.splits is None else len(r.splits) for r in rows)


def _rw_bwd(name, f, rows, bcs, cts, drow, dbc, outs, tm, add=None):
    np_, nct = _npieces(rows), _npieces(cts)

    def fn(pieces, bvals):
        allv = [p.astype(F32) for p in pieces[:np_]] + [b.astype(F32) for b in bvals]
        ct = [p.astype(F32) for p in pieces[np_:np_ + nct]]
        didx = [i for i, m in enumerate(list(drow) + list(dbc)) if m]

        def g(*dv):
            full = list(allv)
            for i, v in zip(didx, dv):
                full[i] = v
            return tuple(f(*full))

        _, vjp = jax.vjp(g, *[allv[i] for i in didx])
        grads = vjp(tuple(ct))
        nrd = sum(bool(m) for m in drow)
        rg, bg = list(grads[:nrd]), list(grads[nrd:])
        if add is not None:
            rg[0] = rg[0] + pieces[np_ + nct].astype(F32)
        return rg, bg

    accs = [b.shape for b, m in zip(bcs, dbc) if m]
    return _rowwise(name, fn, list(rows) + list(cts) + ([add] if add is not None else []), bcs, outs, accs, tm)


def _sigmoid(x):
    return 1.0 / (1.0 + jnp.exp(-x))


def _silu(x):
    return x * _sigmoid(x)


def _softplus(x):
    return jnp.maximum(x, 0.0) + jnp.log(1.0 + jnp.exp(-jnp.abs(x)))


def f_mod(x, g, shift, scale):
    y = x * lax.rsqrt(jnp.mean(x * x, axis=-1, keepdims=True) + EPS)
    return (y * g * (1.0 + scale) + shift,)


def f_rms(x, g):
    return (x * lax.rsqrt(jnp.mean(x * x, axis=-1, keepdims=True) + EPS) * g,)


def f_act(gate, up):
    return (_silu(gate) * up,)


def make_f_res(coef):
    def f_res(y, gate):
        return (coef * gate * y,)
    return f_res


def f_gdnpre(*p):
    out = []
    for i, t in enumerate(p):
        t = _silu(t)
        if i < 2 * NH:
            t = t * lax.rsqrt(jnp.sum(t * t, axis=-1, keepdims=True) + EPS)
        out.append(t)
    return tuple(out)


def f_gates(gates, a_log, dt_bias):
    return _sigmoid(gates), -jnp.exp(a_log) * _softplus(gates + dt_bias)


def f_gdnpost(*a):
    o, z, g = a[:NH], a[NH:2 * NH], a[2 * NH]
    out = []
    for oh, zh in zip(o, z):
        y = oh * lax.rsqrt(jnp.mean(oh * oh, axis=-1, keepdims=True) + EPS) * g
        out.append(y * _silu(zh))
    return tuple(out)


def make_f_qk(shared_rope):
    def f(*a):
        if shared_rope:
            ns, rs = a[:NH], [a[NH]] * NH
            cosp, sins, gn, gr, pm = a[NH + 1:NH + 6]
        else:
            ns, rs = a[0:2 * NH:2], a[1:2 * NH:2]
            cosp, sins, gn, gr, pm = a[2 * NH:2 * NH + 5]
        out = []
        for n, r in zip(ns, rs):
            ss = jnp.sum(n * n, axis=-1, keepdims=True) + jnp.sum(r * r, axis=-1, keepdims=True)
            rstd = lax.rsqrt(ss * (1.0 / QKH) + EPS)
            yn = n * rstd * gn
            yr = r * rstd * gr
            sw = jnp.dot(yr, pm, precision=HI, preferred_element_type=F32)
            out += [yn, yr * cosp + sw * sins]
        return tuple(out)
    return f


def _matmul(name, a, b, mode="nn", out_dtype=F32, precise=False, lay=None, li=0, into=None, nmat=1, res=None):
    if lay == "b_cols":
        per = b.shape[3]
        rb, cb = b.shape[2], 4 * per
    elif lay == "b_rows":
        per = b.shape[2]
        rb, cb = 4 * per, b.shape[3]
    elif lay == "b_stack":
        rb, cb = b.shape[1:]
    else:
        rb, cb = b.shape
    if mode == "nn":
        (M, K), N = a.shape, cb
    elif mode == "nt":
        (M, K), N = a.shape, rb
    else:
        (K, M), N = a.shape, cb
    tm = _tile(M, (1024, 512, 256, 128))
    tn = _tile(N, (1408, 1024, 512, 384, 256, 128))
    tk = _tile(K, (1408, 1024, 512, 384, 256, 128))
    if lay == "b_cols":
        tn, tk = (per, tk) if mode == "nn" else (tn, per)
    elif lay == "b_rows":
        tm, tn, tk = (tm, 512, K) if mode == "nn" else (min(tm, 512), N, tk)
    elif lay == "o_cols":
        per = N // 4
        tn = per
    elif lay == "o_rows":
        per = M // 4
        tm, tn = M, 512
    nk = K // tk
    dims = {"nn": (((1,), (0,)), ((), ())), "nt": (((1,), (1,)), ((), ())), "tn": (((0,), (0,)), ((), ()))}[mode]

    def body(a_ref, b_ref, *rest):
        acc_ref = rest[-1]
        o_ref = rest[2] if res is not None else rest[-2]
        k = pl.program_id(2)

        @pl.when(k == 0)
        def _():
            acc_ref[...] = jnp.zeros(acc_ref.shape, F32)

        bv = b_ref[...]
        if lay == "b_rows":
            bv = bv.reshape(4 * per, bv.shape[2])
        if precise:
            acc_ref[...] += lax.dot_general(a_ref[...].astype(F32), bv.astype(F32), dims, precision=HI,
                                            preferred_element_type=F32)
        else:
            acc_ref[...] += lax.dot_general(a_ref[...].astype(BF16), bv.astype(BF16), dims, preferred_element_type=F32)

        @pl.when(k == nk - 1)
        def _():
            if lay == "o_rows":
                for s in range(4):
                    o_ref[s] = acc_ref[s * per:(s + 1) * per, :].astype(o_ref.dtype)
            else:
                o_ref[...] = acc_ref[...].astype(o_ref.dtype)
            if res is not None:
                rest[3][...] = rest[0][...] + res[2] * rest[1][...] * acc_ref[...]

    a_spec =pl.BlockSpec((tk, tm), lambda i, j, k: (k, i)) if mode == "tn" else pl.BlockSpec((tm, tk), lambda i, j, k: (i, k))
    if lay == "b_cols":
        b_spec = (pl.BlockSpec((None, None, tk, per), lambda i, j, k: (j, li, k, 0)) if mode == "nn" else
                  pl.BlockSpec((None, None, tn, per), lambda i, j, k: (k, li, j, 0)))
    elif lay == "b_rows":
        b_spec = (pl.BlockSpec((4, None, per, tn), lambda i, j, k: (0, li, 0, j)) if mode == "nn" else
                  pl.BlockSpec((4, None, per, tk), lambda i, j, k: (0, li, 0, k)))
    elif lay == "b_stack":
        b_spec = pl.BlockSpec((None, tk, tn), lambda i, j, k: (li, k, j))
    elif mode == "nt":
        b_spec = pl.BlockSpec((tn, tk), lambda i, j, k: (j, k))
    else:
        b_spec = pl.BlockSpec((tk, tn), lambda i, j, k: (k, j))
    if lay == "o_stack":
        o_spec = pl.BlockSpec((None, tm, tn), lambda i, j, k: (li, i, j))
        o_shape = jax.ShapeDtypeStruct((nmat, M, N), out_dtype)
    elif lay == "o_cols":
        o_spec = pl.BlockSpec((None, None, tm, per), lambda i, j, k: (j, li, i, 0))
        o_shape = jax.ShapeDtypeStruct((4, nmat, M, per), out_dtype)
    elif lay == "o_rows":
        o_spec = pl.BlockSpec((4, None, per, tn), lambda i, j, k: (0, li, 0, j))
        o_shape = jax.ShapeDtypeStruct((4, nmat, per, N), out_dtype)
    else:
        o_spec = pl.BlockSpec((tm, tn), lambda i, j, k: (i, j))
        o_shape = jax.ShapeDtypeStruct((M, N), out_dtype)
    in_specs, args, alias = [a_spec, b_spec], [a, b], {}
    if into is not None:
        in_specs.append(pl.BlockSpec(memory_space=pl.ANY))
        args.append(into)
        alias = {2: 0}
    if res is not None:
        in_specs += [pl.BlockSpec((tm, tn), lambda i, j, k: (i, j)), pl.BlockSpec((1, tn), lambda i, j, k: (0, j))]
        args += [res[0], res[1]]
        o_spec = [o_spec, pl.BlockSpec((tm, tn), lambda i, j, k: (i, j))]
        o_shape = [o_shape, jax.ShapeDtypeStruct((M, N), F32)]
    return pl.pallas_call(body, name=name, grid=(M // tm, N // tn, nk), in_specs=in_specs, out_specs=o_spec,
                          out_shape=o_shape, scratch_shapes=[pltpu.VMEM((tm, tn), F32)], input_output_aliases=alias,
                          compiler_params=_cparams(("parallel", "parallel", "arbitrary")))(*args)


def _shift_down(t, p, d):
    if d == 0:
        return t
    tr = pltpu.roll(t, d, 0)
    pr = pltpu.roll(p, d, 0)
    r8 = lax.broadcasted_iota(jnp.int32, p.shape, 0)
    first = jnp.where(r8 < d, pr, tr[:SUB])
    return jnp.concatenate([first, tr[SUB:]], axis=0)


def _shift_up(t, nx, d):
    if d == 0:
        return t
    tm = t.shape[0]
    tr = pltpu.roll(t, tm - d, 0)
    nr = pltpu.roll(nx, SUB - d, 0)
    r8 = lax.broadcasted_iota(jnp.int32, nx.shape, 0)
    last = jnp.where(r8 >= SUB - d, nr, tr[tm - SUB:])
    return jnp.concatenate([tr[:tm - SUB], last], axis=0)


def _conv_fwd(name, proj, w8, C, tm):
    def fn(pieces, bvals):
        t, p = pieces[0].astype(F32), pieces[1].astype(F32)
        w = bvals[0]
        p = jnp.where(pl.program_id(0) == 0, 0.0, p)
        out = w[3:4] * t
        for d in (1, 2, 3):
            out = out + w[3 - d:4 - d] * _shift_down(t, p, d)
        return [out], []
    return _rowwise(name, fn, [Row(proj, C), Row(proj, C, halo="prev")], [w8], [((C,), F32)], [], tm)[0]


def _conv_bwd(name, proj, dout, w8, C, tm, out_dtype):
    n = proj.shape[0] // tm

    def fn(pieces, bvals):
        t, p, g, gn = [v.astype(F32) for v in pieces]
        w = bvals[0]
        i = pl.program_id(0)
        p = jnp.where(i == 0, 0.0, p)
        gn = jnp.where(i == n - 1, 0.0, gn)
        dx = w[3:4] * g
        dws = [jnp.sum(g * t, axis=0, keepdims=True)]
        for d in (1, 2, 3):
            dx = dx + w[3 - d:4 - d] * _shift_up(g, gn, d)
            dws.append(jnp.sum(g * _shift_down(t, p, d), axis=0, keepdims=True))
        dw = jnp.concatenate([dws[3], dws[2], dws[1], dws[0], jnp.zeros((4, g.shape[1]), F32)], axis=0)
        return [dx], [dw]
    return _rowwise(name, fn, [Row(proj, C), Row(proj, C, halo="prev"), Row(dout), Row(dout, halo="next")], [w8],
                    [((C,), out_dtype)], [(SUB, C)], tm)


def _bdot(a, b, ca, cb):
    return lax.dot_general(a.astype(BF16), b.astype(BF16), (((ca,), (cb,)), ((0,), (0,))), preferred_element_type=F32)


def _bdot3(a, b, ca, cb):
    dims = (((ca,), (cb,)), ((0,), (0,)))
    ah, bh = a.astype(BF16), b.astype(BF16)
    al, bl = (a - ah.astype(F32)).astype(BF16), (b - bh.astype(F32)).astype(BF16)
    d = lambda x, y: lax.dot_general(x, y, dims, preferred_element_type=F32)
    return d(ah, bh) + (d(ah, bl) + d(al, bh))


@jax.custom_vjp
def _bmm3(a, b):
    return _bdot3(a, b, 2, 1)


_bmm3.defvjp(lambda a, b: (_bdot3(a, b, 2, 1), (a, b)),
             lambda res, g: (_bdot3(g, res[1], 2, 2), _bdot3(res[0], g, 1, 1)))


def _neumann(nl):
    C = nl.shape[1]
    eye = (lax.broadcasted_iota(jnp.int32, (1, C, C), 1) == lax.broadcasted_iota(jnp.int32, (1, C, C), 2)).astype(F32)
    T = eye + nl
    pw = nl
    for _ in range(C.bit_length() - 2):
        pw = _bdot3(pw, pw, 2, 1)
        T = T + _bdot3(T, pw, 2, 1)
    return T


_unit_lower_inv = jax.custom_vjp(_neumann)


def _unit_lower_inv_fwd(nl):
    T = _neumann(nl)
    return T, T


def _unit_lower_inv_bwd(T, g):
    return (_bdot3(_bdot3(T, g, 1, 1), T, 2, 2),)


_unit_lower_inv.defvjp(_unit_lower_inv_fwd, _unit_lower_inv_bwd)


@jax.custom_vjp
def _known_inv(nl, T):
    return T


_known_inv.defvjp(lambda nl, T: (T, T), lambda T, g: (_unit_lower_inv_bwd(T, g)[0], jnp.zeros_like(T)))


def _gdn_chunk(q, k, v, gcol, grow, bcol, S, T_saved=None):
    C = CHUNK
    ii = lax.broadcasted_iota(jnp.int32, (1, C, C), 1)
    jj = lax.broadcasted_iota(jnp.int32, (1, C, C), 2)
    incl, strict = ii >= jj, ii > jj
    gc_col = jnp.sum(jnp.where(incl, 1.0, 0.0) * grow, axis=2, keepdims=True)
    gc_row = jnp.sum(jnp.where(jj >= ii, 1.0, 0.0) * gcol, axis=1, keepdims=True)
    decay = jnp.where(incl, jnp.exp(jnp.where(incl, gc_col - gc_row, 0.0)), 0.0)
    qs = q * (DH ** -0.5)
    kb = k * bcol
    nl = -jnp.where(strict, _bdot(kb, k, 2, 2) * decay, 0.0)
    T = _unit_lower_inv(nl) if T_saved is None else _known_inv(nl, T_saved)
    egc = jnp.exp(gc_col)
    u = _bmm3(T, v * bcol)
    w = _bmm3(T, kb * egc)
    att = jnp.where(incl, _bdot(qs, k, 2, 2) * decay, 0.0)
    v_new = u - _bdot(w, S, 2, 1)
    o = _bdot(qs * egc, S, 2, 1) + _bdot(att, v_new, 2, 1)
    g_last = jnp.sum(grow, axis=2, keepdims=True)
    k_dec = k * jnp.exp(g_last - gc_col)
    S_out = S * jnp.exp(g_last) + _bdot(k_dec, v_new, 1, 1)
    return o, S_out, T


GDN_STEP = 4


def _chunk_args(refs, c):
    q_ref, k_ref, v_ref, gc_ref, gr_ref, b_ref = refs
    rows = slice(c * CHUNK, (c + 1) * CHUNK)
    heads = lambda ref, w: jnp.stack([ref[rows, h * w:(h + 1) * w] for h in range(NH)])
    grow = jnp.stack([gr_ref[c, h:h + 1, :] for h in range(NH)])
    return heads(q_ref, DH), heads(k_ref, DH), heads(v_ref, DH), heads(gc_ref, 1), grow, heads(b_ref, 1)


def _gdn_specs(NP, rev):
    ix = (lambda i: NP - 1 - i) if rev else (lambda i: i)
    wide = pl.BlockSpec((GDN_STEP * CHUNK, D), lambda i: (ix(i), 0))
    col = pl.BlockSpec((GDN_STEP * CHUNK, NH), lambda i: (ix(i), 0))
    row = pl.BlockSpec((GDN_STEP, NH, CHUNK), lambda i: (ix(i), 0, 0))
    st = pl.BlockSpec((1, NH, DH, DH), lambda i: (ix(i), 0, 0, 0))
    tinv = pl.BlockSpec((GDN_STEP, NH, CHUNK, CHUNK), lambda i: (ix(i), 0, 0, 0))
    return wide, col, row, st, tinv


def _gdn_fwd(name, q, k, v, gcol, grow, bcol):
    S = q.shape[0]
    NC = S // CHUNK
    NP = NC // GDN_STEP

    def body(q_ref, k_ref, v_ref, gc_ref, gr_ref, b_ref, o_ref, ss_ref, t_ref, st):
        @pl.when(pl.program_id(0) == 0)
        def _():
            st[...] = jnp.zeros(st.shape, F32)
        state = st[...]
        ss_ref[0] = state
        for c in range(GDN_STEP):
            o, state, tinv = _gdn_chunk(*_chunk_args((q_ref, k_ref, v_ref, gc_ref, gr_ref, b_ref), c), state)
            for h in range(NH):
                o_ref[c * CHUNK:(c + 1) * CHUNK, h * DH:(h + 1) * DH] = o[h]
            t_ref[c] = tinv
        st[...] = state

    wide, col, row, stsp, tsp = _gdn_specs(NP, False)
    return pl.pallas_call(body, name=name, grid=(NP,), in_specs=[wide, wide, wide, col, row, col],
                          out_specs=[wide, stsp, tsp],
                          out_shape=[jax.ShapeDtypeStruct((S, D), F32), jax.ShapeDtypeStruct((NP, NH, DH, DH), F32),
                                     jax.ShapeDtypeStruct((NC, NH, CHUNK, CHUNK), F32)],
                          scratch_shapes=[pltpu.VMEM((NH, DH, DH), F32)],
                          compiler_params=_cparams(("arbitrary",)))(q, k, v, gcol, grow, bcol)


def _gdn_bwd(name, q, k, v, gcol, grow, bcol, ssave, tsave, do):
    S = q.shape[0]
    NC = S // CHUNK
    NP = NC // GDN_STEP

    def body(q_ref, k_ref, v_ref, gc_ref, gr_ref, b_ref, ss_ref, t_ref, do_ref, dq_ref, dk_ref, dv_ref, dgc_ref, dgr_ref,
             db_ref, dst):
        @pl.when(pl.program_id(0) == 0)
        def _():
            dst[...] = jnp.zeros(dst.shape, F32)
        refs = (q_ref, k_ref, v_ref, gc_ref, gr_ref, b_ref)
        tinv = [t_ref[c] for c in range(GDN_STEP)]

        def chain(state, *flat):
            outs = []
            for c in range(GDN_STEP):
                o, state, _ = _gdn_chunk(*flat[6 * c:6 * c + 6], state, T_saved=tinv[c])
                outs.append(o)
            return tuple(outs) + (state,)

        prim = [a for c in range(GDN_STEP) for a in _chunk_args(refs, c)]
        _, vjp = jax.vjp(chain, ss_ref[0], *prim)
        dos = tuple(jnp.stack([do_ref[c * CHUNK:(c + 1) * CHUNK, h * DH:(h + 1) * DH] for h in range(NH)])
                    for c in range(GDN_STEP))
        grads = vjp(dos + (dst[...],))
        dst[...] = grads[0]
        for c in range(GDN_STEP):
            dq, dk, dv, dgc, dgr, db = grads[1 + 6 * c:7 + 6 * c]
            rows = slice(c * CHUNK, (c + 1) * CHUNK)
            for h in range(NH):
                hs = slice(h * DH, (h + 1) * DH)
                dq_ref[rows, hs] = dq[h]
                dk_ref[rows, hs] = dk[h]
                dv_ref[rows, hs] = dv[h]
                dgc_ref[rows, h:h + 1] = dgc[h]
                dgr_ref[c, h:h + 1, :] = dgr[h]
                db_ref[rows, h:h + 1] = db[h]

    wide, col, row, stsp, tsp = _gdn_specs(NP, True)
    return pl.pallas_call(body, name=name, grid=(NP,), in_specs=[wide, wide, wide, col, row, col, stsp, tsp, wide],
                          out_specs=[wide, wide, wide, col, row, col],
                          out_shape=[jax.ShapeDtypeStruct((S, D), F32)] * 3 + [jax.ShapeDtypeStruct((S, NH), F32),
                                                                                 jax.ShapeDtypeStruct((NC, NH, CHUNK), F32),
                                                                                 jax.ShapeDtypeStruct((S, NH), F32)],
                          scratch_shapes=[pltpu.VMEM((NH, DH, DH), F32)],
                          compiler_params=_cparams(("arbitrary",)))(q, k, v, gcol, grow, bcol, ssave, tsave, do)


TQ = 512
SM_SCALE = QKH ** -0.5
NEG = -1e30


def _diag_mask(transposed):
    r = lax.broadcasted_iota(jnp.int32, (TQ, TQ), 0) // CHUNK
    c = lax.broadcasted_iota(jnp.int32, (TQ, TQ), 1) // CHUNK
    return (r <= c) if transposed else (c <= r)


def _dot_nt(a, b):
    return lax.dot_general(a, b, (((1,), (1,)), ((), ())), preferred_element_type=F32)


def _flash_fwd(name, qp, kp, kv):
    S = qp.shape[0]
    nq = S // (2 * TQ)

    def body(q_ref, k_ref, v_ref, o_ref, lse_ref):
        qi = pl.program_id(1)
        qs = (q_ref[:TQ, :], q_ref[TQ:, :])

        def step(q, j, carry, masked):
            m, l, acc = carry
            rows = pl.ds(pl.multiple_of(j * TQ, TQ), TQ)
            s = _dot_nt(q, k_ref[rows, :]) * SM_SCALE
            if masked:
                s = jnp.where(_diag_mask(False), s, NEG)
            m_new = jnp.maximum(m, jnp.max(s, axis=-1, keepdims=True))
            p = jnp.exp(s - m_new)
            alpha = jnp.exp(m - m_new)
            l = alpha * l + jnp.sum(p, axis=-1, keepdims=True)
            acc = alpha * acc + jnp.dot(p.astype(BF16), v_ref[rows, :].astype(BF16), preferred_element_type=F32)
            return m_new, l, acc

        init = (jnp.full((TQ, 1), NEG, F32), jnp.zeros((TQ, 1), F32), jnp.zeros((TQ, DH), F32))
        ca, cb = lax.fori_loop(0, 2 * qi, lambda j, c: (step(qs[0], j, c[0], False), step(qs[1], j, c[1], False)),
                               (init, init))
        ca = step(qs[0], 2 * qi, ca, True)
        cb = step(qs[1], 2 * qi + 1, step(qs[1], 2 * qi, cb, False), True)
        for u, (m, l, acc) in enumerate((ca, cb)):
            o_ref[u * TQ:(u + 1) * TQ, :] = acc / l
            lse_ref[0, u * TQ:(u + 1) * TQ, :] = m + jnp.log(l)

    return pl.pallas_call(
        body, name=name, grid=(NH, nq),
        in_specs=[pl.BlockSpec((2 * TQ, HP), lambda h, i: (i, h)), pl.BlockSpec((S, HP), lambda h, i: (0, h)),
                  pl.BlockSpec((S, DH), lambda h, i: (0, NH + h))],
        out_specs=[pl.BlockSpec((2 * TQ, DH), lambda h, i: (i, h)), pl.BlockSpec((1, 2 * TQ, 1), lambda h, i: (h, i, 0))],
        out_shape=[jax.ShapeDtypeStruct((S, NH * DH), F32), jax.ShapeDtypeStruct((NH, S, 1), F32)],
        compiler_params=_cparams(("parallel", "arbitrary")))(qp, kp, kv)


def _flash_bwd_dq(name, qp, kp, kv, o, do, lse):
    S = qp.shape[0]
    nq = S // (2 * TQ)

    def body(q_ref, k_ref, v_ref, o_ref, do_ref, lse_ref, dq_ref, dl_ref):
        qi = pl.program_id(1)
        subs = []
        for u in range(2):
            sl = slice(u * TQ, (u + 1) * TQ)
            do = do_ref[sl, :]
            delta = jnp.sum(o_ref[sl, :] * do, axis=-1, keepdims=True)
            dl_ref[0, sl, :] = delta
            subs.append((q_ref[sl, :], do.astype(BF16), lse_ref[0, sl, :], delta))

        def step(sub, j, dq, masked):
            q, dob, lse, delta = sub
            rows = pl.ds(pl.multiple_of(j * TQ, TQ), TQ)
            k = k_ref[rows, :]
            s = _dot_nt(q, k) * SM_SCALE
            if masked:
                s = jnp.where(_diag_mask(False), s, NEG)
            p = jnp.exp(s - lse)
            dp = _dot_nt(dob, v_ref[rows, :].astype(BF16))
            ds = p * (dp - delta) * SM_SCALE
            return dq + jnp.dot(ds.astype(BF16), k, preferred_element_type=F32)

        zero = jnp.zeros((TQ, HP), F32)
        dqa, dqb = lax.fori_loop(0, 2 * qi, lambda j, c: (step(subs[0], j, c[0], False), step(subs[1], j, c[1], False)),
                                 (zero, zero))
        dq_ref[:TQ, :] = step(subs[0], 2 * qi, dqa, True)
        dq_ref[TQ:, :] = step(subs[1], 2 * qi + 1, step(subs[1], 2 * qi, dqb, False), True)

    return pl.pallas_call(
        body, name=name, grid=(NH, nq),
        in_specs=[pl.BlockSpec((2 * TQ, HP), lambda h, i: (i, h)), pl.BlockSpec((S, HP), lambda h, i: (0, h)),
                  pl.BlockSpec((S, DH), lambda h, i: (0, NH + h)), pl.BlockSpec((2 * TQ, DH), lambda h, i: (i, h)),
                  pl.BlockSpec((2 * TQ, DH), lambda h, i: (i, h)), pl.BlockSpec((1, 2 * TQ, 1), lambda h, i: (h, i, 0))],
        out_specs=[pl.BlockSpec((2 * TQ, HP), lambda h, i: (i, h)), pl.BlockSpec((1, 2 * TQ, 1), lambda h, i: (h, i, 0))],
        out_shape=[jax.ShapeDtypeStruct((S, NH * HP), F32), jax.ShapeDtypeStruct((NH, S, 1), F32)],
        compiler_params=_cparams(("parallel", "arbitrary")))(qp, kp, kv, o, do, lse)


def _flash_bwd_dkv(name, qp, kp, kv, do, lse_row, delta_row):
    S = qp.shape[0]
    nq = S // TQ

    def body(q_ref, k_ref, v_ref, do_ref, lse_ref, dl_ref, dk_ref, dv_ref):
        kj = pl.program_id(1)
        subs = [(k_ref[u * TQ:(u + 1) * TQ, :], v_ref[u * TQ:(u + 1) * TQ, :].astype(BF16)) for u in range(2)]

        def step(sub, i, carry, masked):
            k, vb = sub
            dk, dv = carry
            rows = pl.ds(pl.multiple_of(i * TQ, TQ), TQ)
            q = q_ref[rows, :]
            dob = do_ref[rows, :].astype(BF16)
            st = _dot_nt(k, q) * SM_SCALE
            pt = jnp.exp(st - lse_ref[0, :, rows])
            if masked:
                pt = jnp.where(_diag_mask(True), pt, 0.0)
            dv = dv + jnp.dot(pt.astype(BF16), dob, preferred_element_type=F32)
            dpt = _dot_nt(vb, dob)
            dst = pt * (dpt - dl_ref[0, :, rows]) * SM_SCALE
            dk = dk + jnp.dot(dst.astype(BF16), q, preferred_element_type=F32)
            return dk, dv

        zero = (jnp.zeros((TQ, HP), F32), jnp.zeros((TQ, DH), F32))
        ca = step(subs[0], 2 * kj + 1, step(subs[0], 2 * kj, zero, True), False)
        cb = step(subs[1], 2 * kj + 1, zero, True)
        ca, cb = lax.fori_loop(2 * kj + 2, nq, lambda i, c: (step(subs[0], i, c[0], False), step(subs[1], i, c[1], False)),
                               (ca, cb))
        for u, (dk, dv) in enumerate((ca, cb)):
            dk_ref[u * TQ:(u + 1) * TQ, :] = dk
            dv_ref[u * TQ:(u + 1) * TQ, :] = dv

    return pl.pallas_call(
        body, name=name, grid=(NH, nq // 2),
        in_specs=[pl.BlockSpec((S, HP), lambda h, j: (0, h)), pl.BlockSpec((2 * TQ, HP), lambda h, j: (j, h)),
                  pl.BlockSpec((2 * TQ, DH), lambda h, j: (j, NH + h)), pl.BlockSpec((S, DH), lambda h, j: (0, h)),
                  pl.BlockSpec((1, 1, S), lambda h, j: (h, 0, 0)), pl.BlockSpec((1, 1, S), lambda h, j: (h, 0, 0))],
        out_specs=[pl.BlockSpec((2 * TQ, HP), lambda h, j: (j, h)), pl.BlockSpec((2 * TQ, DH), lambda h, j: (j, h))],
        out_shape=[jax.ShapeDtypeStruct((S, NH * HP), F32), jax.ShapeDtypeStruct((S, NH * DH), F32)],
        compiler_params=_cparams(("parallel", "arbitrary")))(qp, kp, kv, do, lse_row, delta_row)


def _tm(S, width):
    t = 512 if width <= 1024 else (256 if width <= 3072 else 128)
    return min(t, S)


TM_LIGHT = 1024


def _mod_fwd(tag, x, g, shift, scale):
    S = x.shape[0]
    return _rw_fwd(tag + "_mod", f_mod, [Row(x)], [g, shift, scale], [((D,), BF16)], min(TM_LIGHT, S))[0]


def _mod_bwd(tag, x, g, shift, scale, dh, dx_direct):
    S = x.shape[0]
    r = _rw_bwd(tag + "_mod_b", f_mod, [Row(x)], [g, shift, scale], [Row(dh)], [True], [True] * 3, [((D,), F32)],
                _tm(S, D), add=Row(dx_direct))
    return r[0], r[1:]


def _res_fwd(tag, x, y, gate, coef):
    S = x.shape[0]

    def fn(pieces, bvals):
        return [pieces[0] + coef * bvals[0] * pieces[1]], []
    return _rowwise(tag + "_res", fn, [Row(x), Row(y)], [gate], [((D,), F32)], [], min(TM_LIGHT, S))[0]


def _res_bwd(tag, y, gate, dxn, coef):
    S = y.shape[0]
    r = _rw_bwd(tag + "_res_b", make_f_res(coef), [Row(y)], [gate], [Row(dxn)], [True], [True], [((D,), BF16)],
                min(TM_LIGHT, S))
    return r[0], r[1]


def _ffn_fwd(tag, x, mod3, g, w_in4, w_out4, li):
    shift, scale, gate = mod3
    S = x.shape[0]
    h = _mod_fwd(tag, x, g, shift, scale)
    gu = _matmul(tag + "_in", h, w_in4, lay="b_cols", li=li, out_dtype=BF16)
    a = _rw_fwd(tag + "_act", f_act, [Row(gu, splits=[FF, FF])], [], [((FF,), BF16)], _tm(S, FF))[0]
    y, xn = _matmul(tag + "_out", a, w_out4, lay="b_rows", li=li, res=(x, gate, 0.5))
    return xn, (x, h, gu, a, y)


def _ffn_bwd(tag, dxn, res, mod3, g, w_in4, w_out4, li, g_in4, g_out4):
    shift, scale, gate = mod3
    x, h, gu, a, y = res
    S = x.shape[0]
    nmat = w_in4.shape[1]
    dy, dgate = _res_bwd(tag, y, gate, dxn, 0.5)
    da = _matmul(tag + "_out_bi", dy, w_out4, "nt", lay="b_rows", li=li, out_dtype=BF16)
    g_out4 = _matmul(tag + "_out_bw", a, dy, "tn", lay="o_rows", li=li, into=g_out4, nmat=nmat, out_dtype=BF16)
    dgu = _rw_bwd(tag + "_act_b", f_act, [Row(gu, splits=[FF, FF])], [], [Row(da)], [True, True], [],
                  [((FF, FF), BF16)], _tm(S, FF))[0]
    dh = _matmul(tag + "_in_bi", dgu, w_in4, "nt", lay="b_cols", li=li)
    g_in4 = _matmul(tag + "_in_bw", h, dgu, "tn", lay="o_cols", li=li, into=g_in4, nmat=nmat, out_dtype=BF16)
    dx, (dg, dshift, dscale) = _mod_bwd(tag, x, g, shift, scale, dh, dxn)
    return dx, g_in4, g_out4, dict(g=dg, mod=(dshift, dscale, dgate))


def _pad_lanes(a, lo, width=LANE):
    return jnp.pad(a, ((0, 0), (lo, width - lo - a.shape[1])))


def _gdn_layer_fwd(tag, x, mod3, g, p):
    shift, scale, gate = mod3
    S = x.shape[0]
    NC = S // CHUNK
    h = _mod_fwd(tag, x, g, shift, scale)
    proj = _matmul(tag + "_in", h, p["w_in"])
    qc = _conv_fwd(tag + "_conv", proj, p["conv_w8"], 3 * D, _tm(S, 3 * D))
    q, k, v = _rw_fwd(tag + "_pre", f_gdnpre, [Row(qc, splits=[DH] * (3 * NH))], [],
                      [((DH,) * NH, F32)] * 3, _tm(S, 3 * D))
    betaf, gf = _rw_fwd(tag + "_gates", f_gates, [Row(proj, LANE, cb=GATE_CB)], [p["a_log128"], p["dt_bias128"]],
                        [((LANE,), F32)] * 2, _tm(S, LANE))
    bcol, gcol = betaf[:, :NH], gf[:, NH:2 * NH]
    grow = gcol.reshape(NC, CHUNK, NH).transpose(0, 2, 1)
    o, ssave, tsave = _gdn_fwd(tag + "_core", q, k, v, gcol, grow, bcol)
    on = _rw_fwd(tag + "_post", f_gdnpost, [Row(o, splits=[DH] * NH), Row(proj, D, cb=3, splits=[DH] * NH)],
                 [p["norm_g"]], [((DH,) * NH, BF16)], _tm(S, 2 * D))[0]
    y, xn = _matmul(tag + "_out", on, p["w_out"], res=(x, gate, 1.0))
    return xn, (x, h, proj, qc, q, k, v, gcol, grow, bcol, ssave, tsave, o, on, y)


def _gdn_layer_bwd(tag, dxn, res, mod3, g, p):
    shift, scale, gate = mod3
    x, h, proj, qc, q, k, v, gcol, grow, bcol, ssave, tsave, o, on, y = res
    S = x.shape[0]
    dy, dgate = _res_bwd(tag, y, gate, dxn, 1.0)
    don = _matmul(tag + "_out_bi", dy, p["w_out"], "nt")
    dw_out = _matmul(tag + "_out_bw", on, dy, "tn")
    do, dz, dnorm = _rw_bwd(tag + "_post_b", f_gdnpost, [Row(o, splits=[DH] * NH), Row(proj, D, cb=3, splits=[DH] * NH)],
                            [p["norm_g"]], [Row(don, splits=[DH] * NH)], [True] * (2 * NH), [True],
                            [((DH,) * NH, F32), ((DH,) * NH, BF16)], _tm(S, 2 * D))
    dq, dk, dv, dgc, dgr, db = _gdn_bwd(tag + "_core_b", q, k, v, gcol, grow, bcol, ssave, tsave, do)
    dgcol = dgc + dgr.transpose(0, 2, 1).reshape(S, NH)
    dgates, da_log, ddt = _rw_bwd(tag + "_gates_b", f_gates, [Row(proj, LANE, cb=GATE_CB)], [p["a_log128"], p["dt_bias128"]],
                                  [Row(_pad_lanes(db, 0)), Row(_pad_lanes(dgcol, NH))], [True], [True, True],
                                  [((LANE,), BF16)], _tm(S, LANE))
    dqc = _rw_bwd(tag + "_pre_b", f_gdnpre, [Row(qc, splits=[DH] * (3 * NH))], [],
                  [Row(dq, splits=[DH] * NH), Row(dk, splits=[DH] * NH), Row(dv, splits=[DH] * NH)],
                  [True] * (3 * NH), [], [((DH,) * (3 * NH), F32)], _tm(S, 3 * D))[0]
    dqkv, dconv = _conv_bwd(tag + "_conv_b", proj, dqc, p["conv_w8"], 3 * D, _tm(S, 3 * D), BF16)
    dproj = jnp.concatenate([dqkv, dz, dgates], axis=1)
    dh = _matmul(tag + "_in_bi", dproj, p["w_in"], "nt")
    dw_in = _matmul(tag + "_in_bw", h, dproj, "tn")
    dx, (dg, dshift, dscale) = _mod_bwd(tag, x, g, shift, scale, dh, dxn)
    return dx, dict(w_in=dw_in, conv_w8=dconv, a_log128=da_log, dt_bias128=ddt, norm_g=dnorm,
                    w_out=dw_out, g=dg, mod=(dshift, dscale, dgate))


def _qk_rows(src, shared_rope, ckv=None):
    if shared_rope:
        return [Row(src, D, cb=0, splits=[DH] * NH), Row(ckv, LANE, cb=2)]
    return [Row(src, splits=[DH] * (2 * NH))]


def _kv_fwd(x, kvmod, p, tabs):
    shift, scale = kvmod
    S = x.shape[0]
    h = _mod_fwd("kv", x, p["kv_norm_g"], shift, scale)
    ckv = _matmul("kv_dkv", h, p["w_dkv"])
    lat = _rw_fwd("kv_lat", f_rms, [Row(ckv, KVL)], [p["kv_lat_g"]], [((KVL,), BF16)], _tm(S, KVL))[0]
    kvf = _matmul("kv_ukv", lat, p["w_ukv"])
    kp = _rw_fwd("kv_k", make_f_qk(True), _qk_rows(kvf, True, ckv) + [Row(tabs[0]), Row(tabs[1])],
                 [p["k_gn"], p["k_gr"], p["pm"]], [((DH,) * (2 * NH), BF16)], _tm(S, 2 * D))[0]
    return kp, kvf, (x, h, ckv, lat)


def _kv_bwd(dkp, dv, dx_direct, res, kvmod, kvf, p, tabs):
    shift, scale = kvmod
    x, h, ckv, lat = res
    S = x.shape[0]
    dkn, dkr, dgn, dgr = _rw_bwd("kv_k_b", make_f_qk(True), _qk_rows(kvf, True, ckv) + [Row(tabs[0]), Row(tabs[1])],
                                 [p["k_gn"], p["k_gr"], p["pm"]], [Row(dkp, splits=[DH] * (2 * NH))],
                                 [True] * (NH + 1) + [False, False], [True, True, False],
                                 [((DH,) * NH, BF16), ((LANE,), BF16)], _tm(S, 2 * D))
    dkvf = jnp.concatenate([dkn, dv.astype(BF16)], axis=1)
    dlat = _matmul("kv_ukv_bi", dkvf, p["w_ukv"], "nt")
    dw_ukv = _matmul("kv_ukv_bw", lat, dkvf, "tn")
    dcl, dlg = _rw_bwd("kv_lat_b", f_rms, [Row(ckv, KVL)], [p["kv_lat_g"]], [Row(dlat)], [True], [True],
                       [((KVL,), BF16)], _tm(S, KVL))
    dckv = jnp.concatenate([dcl, dkr], axis=1)
    dh = _matmul("kv_dkv_bi", dckv, p["w_dkv"], "nt")
    dw_dkv = _matmul("kv_dkv_bw", h, dckv, "tn")
    dx, (dg, dshift, dscale) = _mod_bwd("kv", x, p["kv_norm_g"], shift, scale, dh, dx_direct)
    return dx, dict(w_dkv=dw_dkv, w_ukv=dw_ukv, kv_lat_g=dlg, k_gn=dgn, k_gr=dgr, kv_norm_g=dg, mod=(dshift, dscale))


def _mla_layer_fwd(tag, x, mod3, g, p, kp, kvf, tabs):
    shift, scale, gate = mod3
    S = x.shape[0]
    h = _mod_fwd(tag, x, g, shift, scale)
    ql = _matmul(tag + "_dq", h, p["w_dq"])
    qln = _rw_fwd(tag + "_qln", f_rms, [Row(ql)], [p["ql_g"]], [((QL,), BF16)], _tm(S, QL))[0]
    qu = _matmul(tag + "_uq", qln, p["w_uq"])
    qp = _rw_fwd(tag + "_q", make_f_qk(False), _qk_rows(qu, False) + [Row(tabs[0]), Row(tabs[1])],
                 [p["q_gn"], p["q_gr"], p["pm"]], [((DH,) * (2 * NH), BF16)], _tm(S, 2 * D))[0]
    o, lse = _flash_fwd(tag + "_att", qp, kp, kvf)
    y, xn = _matmul(tag + "_out", o, p["w_out"], res=(x, gate, 1.0))
    return xn, (x, h, ql, qln, qu, qp, o, lse, y)


def _mla_layer_bwd(tag, dxn, res, mod3, g, p, kp, kvf, tabs):
    shift, scale, gate = mod3
    x, h, ql, qln, qu, qp, o, lse, y = res
    S = x.shape[0]
    dy, dgate = _res_bwd(tag, y, gate, dxn, 1.0)
    do = _matmul(tag + "_out_bi", dy, p["w_out"], "nt")
    dw_out = _matmul(tag + "_out_bw", o, dy, "tn")
    dqp, delta = _flash_bwd_dq(tag + "_att_bq", qp, kp, kvf, o, do, lse)
    dkp, dv = _flash_bwd_dkv(tag + "_att_bkv", qp, kp, kvf, do, lse.reshape(NH, 1, S), delta.reshape(NH, 1, S))
    dqu, dgn, dgr = _rw_bwd(tag + "_q_b", make_f_qk(False), _qk_rows(qu, False) + [Row(tabs[0]), Row(tabs[1])],
                            [p["q_gn"], p["q_gr"], p["pm"]], [Row(dqp, splits=[DH] * (2 * NH))],
                            [True] * (2 * NH) + [False, False], [True, True, False],
                            [((DH,) * (2 * NH), BF16)], _tm(S, 2 * D))
    dqln = _matmul(tag + "_uq_bi", dqu, p["w_uq"], "nt")
    dw_uq = _matmul(tag + "_uq_bw", qln, dqu, "tn")
    dql, dqlg = _rw_bwd(tag + "_qln_b", f_rms, [Row(ql)], [p["ql_g"]], [Row(dqln)], [True], [True], [((QL,), BF16)],
                        _tm(S, QL))
    dh = _matmul(tag + "_dq_bi", dql, p["w_dq"], "nt")
    dw_dq = _matmul(tag + "_dq_bw", h, dql, "tn")
    dx, (dg, dshift, dscale) = _mod_bwd(tag, x, g, shift, scale, dh, dxn)
    return dx, dkp, dv, dict(w_dq=dw_dq, w_uq=dw_uq, w_out=dw_out, ql_g=dqlg, q_gn=dgn, q_gr=dgr, g=dg,
                             mod=(dshift, dscale, dgate))


def _loss_head(y, tgt):
    S = y.shape[0]

    def fn(pieces, bvals):
        e = pieces[0] - pieces[1]
        part = jnp.sum(e * e) * (0.5 / D)
        return [e * (1.0 / D)], [jnp.full((1, LANE), part, F32)]
    dy, part = _rowwise("loss", fn, [Row(y), Row(tgt)], [], [((D,), F32)], [(1, LANE)], _tm(S, D))
    return part[0, 0], dy


def _rope_tables(positions):
    S = positions.shape[0]
    half = ROPE // 2
    lane = lax.broadcasted_iota(jnp.int32, (1, LANE), 1)
    inv_freq = ROPE_BASE ** (-(lane % half).astype(F32) / half)
    live = (lane < ROPE).astype(F32)
    sign = jnp.where(lane < half, -1.0, 1.0) * live

    def fn(pieces, bvals):
        ang = pieces[0] * bvals[0]
        return [jnp.cos(ang) * bvals[1], jnp.sin(ang) * bvals[2]], []
    pos = jnp.broadcast_to(positions.astype(F32)[:, None], (S, LANE))
    cosp, sins = _rowwise("rope_tab", fn, [Row(pos)], [inv_freq, live, sign], [((LANE,), F32)] * 2, [], _tm(S, LANE))
    r = lax.broadcasted_iota(jnp.int32, (LANE, LANE), 0)
    c = lax.broadcasted_iota(jnp.int32, (LANE, LANE), 1)
    pm = (((c < half) & (r == c + half)) | ((c >= half) & (c < ROPE) & (r == c - half))).astype(F32)
    return (cosp, sins), pm


def _adamw(name, w, g, m, v):
    shape = w.shape
    C = shape[-1]
    R = w.size // C
    tr = R
    for t in (1024, 512, 256, 128, 64, 32, 16, 8):
        if R % t == 0 and t * C * 4 <= (1 << 21):
            tr = t
            break
    c1 = 1.0 - ADAM_B1 ** ADAM_STEP
    c2 = 1.0 - ADAM_B2 ** ADAM_STEP

    def body(w_ref, g_ref, m_ref, v_ref, d_ref, mo_ref, vo_ref):
        gg = g_ref[...]
        mn = ADAM_B1 * m_ref[...] + (1.0 - ADAM_B1) * gg
        vn = ADAM_B2 * v_ref[...] + (1.0 - ADAM_B2) * (gg * gg)
        d_ref[...] = -ADAM_LR * ((mn / c1) / (jnp.sqrt(vn / c2) + ADAM_EPS) + ADAM_WD * w_ref[...])
        mo_ref[...] = mn
        vo_ref[...] = vn

    spec = pl.BlockSpec((tr, C), lambda i: (i, 0))
    outs = pl.pallas_call(body, name=name, grid=(R // tr,), in_specs=[spec] * 4, out_specs=[spec] * 3,
                          out_shape=[jax.ShapeDtypeStruct((R, C), F32)] * 3,
                          compiler_params=_cparams(("parallel",)))(*[t.reshape(R, C) for t in (w, g, m, v)])
    return [o.reshape(shape) for o in outs]


HBM_SPEC = pl.BlockSpec(memory_space=pltpu.HBM)
OTHER_CHIPS = (4, 2, 6)
SIBLING = 1


def _me():
    return lax.axis_index("x"), lax.axis_index("y"), lax.axis_index("c")


def _peer(me, k):
    mx, my, mc = me
    return ((1 - mx) if k & 4 else mx, (1 - my) if k & 2 else my, (1 - mc) if k & 1 else mc)


def _rcopy(src, dst, ssem, rsem, to):
    return pltpu.make_async_remote_copy(src_ref=src, dst_ref=dst, send_sem=ssem, recv_sem=rsem, device_id=to,
                                        device_id_type=MESH)


def _all_gather8(name, x):
    def body(x_ref, o_ref, ssem, rsem, lsem):
        me = _me()
        mine = 4 * me[0] + 2 * me[1] + me[2]
        loc = pltpu.make_async_copy(x_ref, o_ref.at[mine], lsem)
        loc.start()
        sends = []
        for k in range(1, 8):
            cp = _rcopy(x_ref, o_ref.at[mine], ssem.at[k - 1], rsem.at[k - 1], _peer(me, k))
            cp.start()
            sends.append(cp)
        for k in range(1, 8):
            px, py, pc = _peer(me, k)
            _rcopy(x_ref, o_ref.at[4 * px + 2 * py + pc], ssem.at[k - 1], rsem.at[k - 1], (px, py, pc)).wait_recv()
        for cp in sends:
            cp.wait_send()
        loc.wait()

    return pl.pallas_call(body, name=name, out_shape=jax.ShapeDtypeStruct((8,) + x.shape, x.dtype),
                          in_specs=[HBM_SPEC], out_specs=HBM_SPEC,
                          scratch_shapes=[pltpu.SemaphoreType.DMA((7,)), pltpu.SemaphoreType.DMA((7,)),
                                          pltpu.SemaphoreType.DMA(())])(x)


PACK_L = 1024
PACK_RT = 256


def _place_shard(name, wp, chip):
    rh, ln = wp.shape[1:]

    def body(s_ref, w_ref, o_ref):
        o_ref[...] = w_ref[...]

    gs = pltpu.PrefetchScalarGridSpec(
        num_scalar_prefetch=1, grid=(2, rh // PACK_RT),
        in_specs=[pl.BlockSpec((None, PACK_RT, ln), lambda h, i, s_ref: (h, i, 0))],
        out_specs=pl.BlockSpec((None, None, PACK_RT, ln), lambda h, i, s_ref: (s_ref[0], h, i, 0)))
    return pl.pallas_call(body, name=name, grid_spec=gs, out_shape=jax.ShapeDtypeStruct((4,) + wp.shape, wp.dtype),
                          compiler_params=_cparams(("parallel", "parallel")))(chip.reshape(1).astype(jnp.int32), wp)


def _gather_weights(name, w4):
    r2 = w4.shape[2] // 2

    def body(w_ref, o_ref, ssem, rsem):
        me = _me()
        mc = me[2]
        px, py, pd, sib = _peer(me, 4), _peer(me, 2), _peer(me, 6), _peer(me, SIBLING)
        chip = lambda p: 2 * p[0] + p[1]
        mine, from_x, from_y, from_d = (o_ref.at[chip(p), mc] for p in (me, px, py, pd))
        q0, q1 = pl.ds(0, r2), pl.ds(r2, r2)
        sends = [_rcopy(mine, mine, ssem.at[0], rsem.at[0], px), _rcopy(mine, mine, ssem.at[1], rsem.at[1], py)]
        for cp in sends:
            cp.start()
        _rcopy(from_x, from_x, ssem.at[0], rsem.at[0], px).wait_recv()
        sends += [_rcopy(from_x.at[q1], from_x.at[q1], ssem.at[2], rsem.at[2], py),
                  _rcopy(from_x, from_x, ssem.at[4], rsem.at[4], sib)]
        sends[-2].start()
        sends[-1].start()
        _rcopy(from_y, from_y, ssem.at[1], rsem.at[1], py).wait_recv()
        sends += [_rcopy(from_y.at[q0], from_y.at[q0], ssem.at[3], rsem.at[3], px),
                  _rcopy(from_y, from_y, ssem.at[5], rsem.at[5], sib)]
        sends[-2].start()
        sends[-1].start()
        _rcopy(from_d.at[q0], from_d.at[q0], ssem.at[3], rsem.at[3], px).wait_recv()
        _rcopy(from_d.at[q1], from_d.at[q1], ssem.at[2], rsem.at[2], py).wait_recv()
        sends.append(_rcopy(from_d, from_d, ssem.at[6], rsem.at[6], sib))
        sends[-1].start()
        for j, p in enumerate((px, py, pd)):
            land = o_ref.at[chip(p), 1 - mc]
            _rcopy(land, land, ssem.at[4 + j], rsem.at[4 + j], sib).wait_recv()
        for cp in sends:
            cp.wait_send()

    return pl.pallas_call(body, name=name, out_shape=jax.ShapeDtypeStruct(w4.shape, w4.dtype),
                          in_specs=[HBM_SPEC], out_specs=HBM_SPEC, input_output_aliases={0: 0},
                          scratch_shapes=[pltpu.SemaphoreType.DMA((7,)), pltpu.SemaphoreType.DMA((7,))])(w4)


def _exchange_half(name, g):
    def body(g_ref, p_ref, ssem, rsem):
        me = _me()
        cps = []
        for s in range(4):
            cp = _rcopy(g_ref.at[s, 1 - me[2]], p_ref.at[s], ssem.at[s], rsem.at[s], _peer(me, SIBLING))
            cp.start()
            cps.append(cp)
        for cp in cps:
            cp.wait()

    return pl.pallas_call(body, name=name, out_shape=jax.ShapeDtypeStruct((4,) + g.shape[2:], g.dtype),
                          in_specs=[HBM_SPEC], out_specs=HBM_SPEC,
                          scratch_shapes=[pltpu.SemaphoreType.DMA((4,)), pltpu.SemaphoreType.DMA((4,))])(g)


def _scatter_chips(name, q):
    r2 = q.shape[1] // 2

    def body(q_ref, t_ref, relay, ssem, rsem):
        me = _me()
        px, py, pd = _peer(me, 4), _peer(me, 2), _peer(me, 6)
        chip = lambda p: 2 * p[0] + p[1]
        q0, q1 = pl.ds(0, r2), pl.ds(r2, r2)
        sends = [_rcopy(q_ref.at[chip(px)], t_ref.at[0], ssem.at[0], rsem.at[0], px),
                 _rcopy(q_ref.at[chip(py)], t_ref.at[1], ssem.at[1], rsem.at[1], py),
                 _rcopy(q_ref.at[chip(pd), q0], relay.at[0], ssem.at[2], rsem.at[2], py),
                 _rcopy(q_ref.at[chip(pd), q1], relay.at[1], ssem.at[3], rsem.at[3], px)]
        for cp in sends:
            cp.start()
        _rcopy(relay.at[0], relay.at[0], ssem.at[2], rsem.at[2], py).wait_recv()
        sends.append(_rcopy(relay.at[0], t_ref.at[2, q0], ssem.at[4], rsem.at[4], px))
        sends[-1].start()
        _rcopy(relay.at[1], relay.at[1], ssem.at[3], rsem.at[3], px).wait_recv()
        sends.append(_rcopy(relay.at[1], t_ref.at[2, q1], ssem.at[5], rsem.at[5], py))
        sends[-1].start()
        _rcopy(t_ref.at[0], t_ref.at[0], ssem.at[0], rsem.at[0], px).wait_recv()
        _rcopy(t_ref.at[1], t_ref.at[1], ssem.at[1], rsem.at[1], py).wait_recv()
        _rcopy(t_ref.at[2, q0], t_ref.at[2, q0], ssem.at[4], rsem.at[4], px).wait_recv()
        _rcopy(t_ref.at[2, q1], t_ref.at[2, q1], ssem.at[5], rsem.at[5], py).wait_recv()
        for cp in sends:
            cp.wait_send()

    return pl.pallas_call(body, name=name,
                          out_shape=[jax.ShapeDtypeStruct((3,) + q.shape[1:], q.dtype),
                                     jax.ShapeDtypeStruct((2, r2) + q.shape[2:], q.dtype)],
                          in_specs=[HBM_SPEC], out_specs=[HBM_SPEC, HBM_SPEC],
                          scratch_shapes=[pltpu.SemaphoreType.DMA((6,)), pltpu.SemaphoreType.DMA((6,))])(q)[0]


def _exchange_full(name, r2):
    def body(r_ref, o_ref, ssem, rsem):
        me = _me()
        mc = me[2]
        cp = _rcopy(o_ref.at[mc], o_ref.at[mc], ssem, rsem, _peer(me, SIBLING))
        cp.start()
        _rcopy(o_ref.at[1 - mc], o_ref.at[1 - mc], ssem, rsem, _peer(me, SIBLING)).wait_recv()
        cp.wait_send()

    return pl.pallas_call(body, name=name, out_shape=jax.ShapeDtypeStruct(r2.shape, r2.dtype),
                          in_specs=[HBM_SPEC], out_specs=HBM_SPEC, input_output_aliases={0: 0},
                          scratch_shapes=[pltpu.SemaphoreType.DMA(()), pltpu.SemaphoreType.DMA(())])(r2)


def _add_half(name, g, p, c):
    rh, ln = g.shape[2:]

    def body(c_ref, g_ref, p_ref, o_ref):
        o_ref[0] = (g_ref[0, 0].astype(F32) + p_ref[0].astype(F32)).astype(o_ref.dtype)

    gs = pltpu.PrefetchScalarGridSpec(
        num_scalar_prefetch=1, grid=(4, rh // PACK_RT),
        in_specs=[pl.BlockSpec((1, 1, PACK_RT, ln), lambda s, i, c_ref: (s, c_ref[0], i, 0)),
                  pl.BlockSpec((1, PACK_RT, ln), lambda s, i, c_ref: (s, i, 0))],
        out_specs=pl.BlockSpec((1, PACK_RT, ln), lambda s, i, c_ref: (s, i, 0)))
    return pl.pallas_call(body, name=name, grid_spec=gs, out_shape=jax.ShapeDtypeStruct((4, rh, ln), BF16),
                          compiler_params=_cparams(("parallel", "parallel")))(c.reshape(1).astype(jnp.int32), g, p)


def _add_chips(name, q, t, chip, c):
    rh, ln = q.shape[1:]

    def body(s_ref, c_ref, q_ref, t_ref, o_ref):
        o_ref[...] = ((q_ref[0].astype(F32) + t_ref[0].astype(F32)) + t_ref[1].astype(F32)) + t_ref[2].astype(F32)

    gs = pltpu.PrefetchScalarGridSpec(
        num_scalar_prefetch=2, grid=(rh // PACK_RT,),
        in_specs=[pl.BlockSpec((1, PACK_RT, ln), lambda i, s_ref, c_ref: (s_ref[0], i, 0)),
                  pl.BlockSpec((3, PACK_RT, ln), lambda i, s_ref, c_ref: (0, i, 0))],
        out_specs=pl.BlockSpec((None, PACK_RT, ln), lambda i, s_ref, c_ref: (c_ref[0], i, 0)))
    return pl.pallas_call(body, name=name, grid_spec=gs, out_shape=jax.ShapeDtypeStruct((2, rh, ln), F32),
                          compiler_params=_cparams(("parallel",)))(chip.reshape(1).astype(jnp.int32),
                                                                    c.reshape(1).astype(jnp.int32), q, t)


def _sum8(name, a):
    def body(a_ref, o_ref):
        acc = a_ref[0]
        for d in range(1, 8):
            acc = acc + a_ref[d]
        o_ref[...] = acc
    return pl.pallas_call(body, name=name, out_shape=jax.ShapeDtypeStruct(a.shape[1:], F32))(a)


def _silu_rows(name, a):
    def body(a_ref, o_ref):
        o_ref[...] = _silu(a_ref[...])
    return pl.pallas_call(body, name=name, out_shape=jax.ShapeDtypeStruct(a.shape, F32))(a)


REST = (("gdn_w_out", 1), ("mla_w_dkv", 0), ("mla_w_ukv", 1), ("mla_w_dq", 1), ("mla_w_uq", 2), ("mla_w_out", 1))


def _packed_rows(n):
    per_half = -(-n // (2 * PACK_L))
    return -(-per_half // PACK_RT) * PACK_RT


def _pack_flat(flat):
    n = flat.shape[-1]
    rh = _packed_rows(n)
    pad = [(0, 0)] * (flat.ndim - 1) + [(0, 2 * rh * PACK_L - n)]
    return jnp.pad(flat, pad).reshape(flat.shape[:-1] + (2, rh, PACK_L))


def _shards_first(full, axis):
    sh = full.shape
    t = full.reshape(sh[:axis] + (4, sh[axis] // 4) + sh[axis + 1:])
    return jnp.moveaxis(t, axis, 0)


def _shards_merge(stacked, axis):
    t = jnp.moveaxis(stacked, 0, axis)
    sh = t.shape
    return t.reshape(sh[:axis] + (4 * sh[axis + 1],) + sh[axis + 2:])


def _pack_small(parts):
    flat = jnp.concatenate([p.reshape(-1).astype(F32) for p in parts])
    n = flat.shape[0]
    rows = -(-n // (SUB * LANE)) * SUB
    return jnp.pad(flat, (0, rows * LANE - n)).reshape(rows, LANE)


def _unpack_small(buf, shapes):
    lead = buf.shape[:-2]
    flat = buf.reshape(lead + (-1,))
    out, off = [], 0
    for sh in shapes:
        n = 1
        for d in sh:
            n *= d
        out.append(flat[..., off:off + n].reshape(lead + tuple(sh)))
        off += n
    return out


WEIGHTS = ('ada_w', 'ada_b', 'norm_g', 'ffn_w_in', 'ffn_w_out', 'gdn_w_in', 'gdn_conv_w', 'gdn_a_log', 'gdn_dt_bias',
           'gdn_norm_g', 'gdn_w_out', 'kv_ada_w', 'kv_ada_b', 'kv_norm_g', 'mla_w_dkv', 'mla_kv_norm_g', 'mla_w_ukv',
           'mla_k_norm_g', 'mla_w_dq', 'mla_q_lora_norm_g', 'mla_w_uq', 'mla_q_norm_g', 'mla_w_out')
ARGS = ('x', 'c', 'positions') + WEIGHTS + ('loss_target',) + tuple('m_' + n for n in WEIGHTS) + tuple('v_' + n for n in WEIGHTS)


def _split_norm(v):
    return v[None, :DH], _pad_lanes(v[None, DH:], 0)


def _join_norm(gn, gr):
    return jnp.concatenate([gn[0], gr[0, :ROPE]])


def _step(x, tgt, pos, mods, kvmod, W, P):
    tabs, pm = _rope_tables(pos)
    m3 = lambda l, i: tuple(mods[l][3 * i + j][None] for j in range(3))
    ng = lambda l, i: P["norm_g"][l, i][None]
    gdn_p, mla_p = [], []
    for l in range(2):
        gdn_p.append(dict(w_in=jnp.pad(W["gdn_w_in"][l], ((0, 0), (0, GDN_IN - W["gdn_w_in"].shape[2]))),
                          conv_w8=jnp.pad(P["gdn_conv_w"][l], ((0, 4), (0, 0))),
                          a_log128=_pad_lanes(P["gdn_a_log"][l][None], NH), dt_bias128=_pad_lanes(P["gdn_dt_bias"][l][None], NH),
                          norm_g=P["gdn_norm_g"][l][None], w_out=W["gdn_w_out"][l]))
        q_gn, q_gr = _split_norm(P["mla_q_norm_g"][l])
        mla_p.append(dict(w_dq=W["mla_w_dq"][l], ql_g=P["mla_q_lora_norm_g"][l][None],
                          w_uq=jnp.pad(W["mla_w_uq"][l].reshape(QL, NH, QKH), ((0, 0), (0, 0), (0, HP - QKH))).reshape(QL, NH * HP),
                          q_gn=q_gn, q_gr=q_gr, pm=pm, w_out=W["mla_w_out"][l]))
    k_gn, k_gr = _split_norm(P["mla_k_norm_g"])
    kv_p = dict(kv_norm_g=P["kv_norm_g"][None], w_dkv=jnp.pad(W["mla_w_dkv"], ((0, 0), (0, QL - KVL - ROPE))),
                kv_lat_g=P["mla_kv_norm_g"][None],
                w_ukv=W["mla_w_ukv"].reshape(KVL, NH, 2, DH).transpose(0, 2, 1, 3).reshape(KVL, 2 * NH * DH),
                k_gn=k_gn, k_gr=k_gr, pm=pm)
    kvm = (kvmod[0][None], kvmod[1][None])

    res = {}
    for l in range(4):
        x, res[l, 0] = _ffn_fwd(f"l{l}a", x, m3(l, 0), ng(l, 0), W["ffn_w_in"], W["ffn_w_out"], 2 * l)
        if l < 2:
            x, res[l, 1] = _gdn_layer_fwd(f"l{l}g", x, m3(l, 1), ng(l, 1), gdn_p[l])
        else:
            x, res[l, 1] = _mla_layer_fwd(f"l{l}m", x, m3(l, 1), ng(l, 1), mla_p[l - 2], kp, kvf, tabs)
        x, res[l, 2] = _ffn_fwd(f"l{l}b", x, m3(l, 2), ng(l, 2), W["ffn_w_in"], W["ffn_w_out"], 2 * l + 1)
        if l == 1:
            kp, kvf, kres = _kv_fwd(x, kvm, kv_p, tabs)
    loss, dx = _loss_head(x, tgt)

    gw = {n: [None] * W[n].shape[0] for n in ("gdn_w_in", "gdn_w_out", "mla_w_dq", "mla_w_uq", "mla_w_out")}
    g_in4 = g_out4 = None
    gp = {n: [None] * 2 for n in ("gdn_conv_w", "gdn_a_log", "gdn_dt_bias", "gdn_norm_g", "mla_q_lora_norm_g", "mla_q_norm_g")}
    gnorm = [[None] * 3 for _ in range(4)]
    dmod = [[None] * NMOD for _ in range(4)]
    dkp = dv = None
    for l in (3, 2, 1, 0):
        if l == 1:
            dx, gk = _kv_bwd(dkp, dv, dx, kres, kvm, kvf, kv_p, tabs)
        for i in (2, 1, 0):
            if i != 1:
                dx, g_in4, g_out4, gd = _ffn_bwd(f"l{l}{'ab'[i // 2]}", dx, res[l, i], m3(l, i), ng(l, i), W["ffn_w_in"],
                                                 W["ffn_w_out"], 2 * l + i // 2, g_in4, g_out4)
            elif l < 2:
                dx, gd = _gdn_layer_bwd(f"l{l}g", dx, res[l, 1], m3(l, 1), ng(l, 1), gdn_p[l])
                gw["gdn_w_in"][l] = gd["w_in"][:, :W["gdn_w_in"].shape[2]]
                gw["gdn_w_out"][l] = gd["w_out"]
                gp["gdn_conv_w"][l] = gd["conv_w8"][:4]
                gp["gdn_a_log"][l] = gd["a_log128"][0, NH:2 * NH]
                gp["gdn_dt_bias"][l] = gd["dt_bias128"][0, NH:2 * NH]
                gp["gdn_norm_g"][l] = gd["norm_g"][0]
            else:
                dx, dkp_l, dv_l, gd = _mla_layer_bwd(f"l{l}m", dx, res[l, 1], m3(l, 1), ng(l, 1), mla_p[l - 2], kp, kvf, tabs)
                dkp = dkp_l if dkp is None else dkp + dkp_l
                dv = dv_l if dv is None else dv + dv_l
                gw["mla_w_dq"][l - 2], gw["mla_w_out"][l - 2] = gd["w_dq"], gd["w_out"]
                gw["mla_w_uq"][l - 2] = gd["w_uq"].reshape(QL, NH, HP)[:, :, :QKH].reshape(QL, NH * QKH)
                gp["mla_q_lora_norm_g"][l - 2] = gd["ql_g"][0]
                gp["mla_q_norm_g"][l - 2] = _join_norm(gd["q_gn"], gd["q_gr"])
            gnorm[l][i] = gd["g"][0]
            for j in range(3):
                dmod[l][3 * i + j] = gd["mod"][j][0]
    gwf = {n: jnp.stack(v) for n, v in gw.items()}
    gwf["ffn_w_in"], gwf["ffn_w_out"] = g_in4, g_out4
    gwf["mla_w_dkv"] = gk["w_dkv"][:, :KVL + ROPE]
    gwf["mla_w_ukv"] = gk["w_ukv"].reshape(KVL, 2, NH, DH).transpose(0, 2, 1, 3).reshape(KVL, 2 * NH * DH)
    gpf = {n: jnp.stack(v) for n, v in gp.items()}
    gpf["norm_g"] = jnp.stack([jnp.stack(r) for r in gnorm])
    gpf["kv_norm_g"] = gk["kv_norm_g"][0]
    gpf["mla_kv_norm_g"] = gk["kv_lat_g"][0]
    gpf["mla_k_norm_g"] = _join_norm(gk["k_gn"], gk["k_gr"])
    dmods = jnp.stack([jnp.stack(r) for r in dmod])
    dkvmod = jnp.stack([gk["mod"][0][0], gk["mod"][1][0]])
    return loss, dx, gwf, gpf, dmods, dkvmod


SMALL = ("norm_g", "gdn_conv_w", "gdn_a_log", "gdn_dt_bias", "gdn_norm_g", "kv_norm_g", "mla_kv_norm_g", "mla_k_norm_g",
         "mla_q_lora_norm_g", "mla_q_norm_g")


def kernel(x, c, positions, ada_w, ada_b, norm_g, ffn_w_in, ffn_w_out, gdn_w_in, gdn_conv_w, gdn_a_log, gdn_dt_bias,
           gdn_norm_g, gdn_w_out, kv_ada_w, kv_ada_b, kv_norm_g, mla_w_dkv, mla_kv_norm_g, mla_w_ukv, mla_k_norm_g,
           mla_w_dq, mla_q_lora_norm_g, mla_w_uq, mla_q_norm_g, mla_w_out, loss_target, m_ada_w, m_ada_b, m_norm_g,
           m_ffn_w_in, m_ffn_w_out, m_gdn_w_in, m_gdn_conv_w, m_gdn_a_log, m_gdn_dt_bias, m_gdn_norm_g, m_gdn_w_out,
           m_kv_ada_w, m_kv_ada_b, m_kv_norm_g, m_mla_w_dkv, m_mla_kv_norm_g, m_mla_w_ukv, m_mla_k_norm_g, m_mla_w_dq,
           m_mla_q_lora_norm_g, m_mla_w_uq, m_mla_q_norm_g, m_mla_w_out, v_ada_w, v_ada_b, v_norm_g, v_ffn_w_in,
           v_ffn_w_out, v_gdn_w_in, v_gdn_conv_w, v_gdn_a_log, v_gdn_dt_bias, v_gdn_norm_g, v_gdn_w_out, v_kv_ada_w,
           v_kv_ada_b, v_kv_norm_g, v_mla_w_dkv, v_mla_kv_norm_g, v_mla_w_ukv, v_mla_k_norm_g, v_mla_w_dq,
           v_mla_q_lora_norm_g, v_mla_w_uq, v_mla_q_norm_g, v_mla_w_out):
    a = dict(locals())
    mx, my, mc = _me()
    dev = 4 * mx + 2 * my + mc
    chip = 2 * mx + my
    x, tgt, pos = a["x"][0], a["loss_target"][0], a["positions"][0]
    take = lambda arr, i, axis=0: lax.dynamic_index_in_dim(arr, i, axis, keepdims=False)

    pre = _all_gather8("ag_pre", _pack_small([a["c"], a["gdn_conv_w"], a["norm_g"]]))
    c_all, conv_sh, norm_sh = _unpack_small(pre, [(D,), a["gdn_conv_w"].shape, a["norm_g"].shape])
    P = {n: a[n] for n in SMALL}
    P["gdn_conv_w"] = jnp.concatenate([conv_sh[2 * s] for s in range(4)], axis=2)
    P["norm_g"] = jnp.concatenate([norm_sh[2 * s] for s in range(4)], axis=2)
    c_act = _silu_rows("c_act", c_all)
    nada = a["ada_w"].shape[2]
    nkv = a["kv_ada_w"].shape[1]
    modp = [_matmul(f"mod{l}", c_act, a["ada_w"], precise=True, lay="b_stack", li=l) for l in range(4)]
    kvp = _matmul("modkv", c_act, a["kv_ada_w"], precise=True)
    mp = _all_gather8("ag_mod", _pack_small(modp + [kvp]))
    modp_all, kvp_all = _unpack_small(mp, [(4, 8, nada), (8, nkv)])
    mods = jnp.concatenate([take(modp_all[2 * s], dev, 1) for s in range(4)], axis=1) + a["ada_b"]
    mods = mods.reshape(4, NMOD, D)
    kvmod = (jnp.concatenate([take(kvp_all[2 * s], dev, 0) for s in range(4)]) + a["kv_ada_b"]).reshape(2, D)

    def gather(tag, w2):
        return _gather_weights("ag_" + tag, _place_shard("own_" + tag, w2, chip))

    def reduce(tag, g4):
        q = _add_half("rsp_" + tag, g4, _exchange_half("rs1_" + tag, g4), mc)
        r2 = _add_chips("rsc_" + tag, q, _scatter_chips("rs2_" + tag, q), chip, mc)
        return _exchange_full("rs3_" + tag, r2)

    halves = lambda t: t.reshape((2, -1) + t.shape[-1:])
    W = {n: gather(t, halves(a[n].astype(BF16))).reshape((4, 8) + a[n].shape[2:])
         for n, t in (("ffn_w_in", "wi"), ("ffn_w_out", "wo"))}
    wg = gather("wg", a["gdn_w_in"].astype(BF16))
    W["gdn_w_in"] = jnp.concatenate([wg[s] for s in range(4)], axis=2)
    wall = gather("wr", _pack_flat(jnp.concatenate([a[n].reshape(-1).astype(BF16) for n, _ in REST]))).reshape(4, -1)
    off = 0
    for n, ax in REST:
        sz = a[n].size
        W[n] = _shards_merge(wall[:, off:off + sz].reshape((4,) + a[n].shape), ax)
        off += sz

    loss, dx, gw, gp, dmods, dkvmod = _step(x, tgt, pos, mods, kvmod, W, P)
    loss = lax.psum(loss, ("x", "y", "c"))

    grads = {n: reduce(t, gw[n].reshape((4, 2, -1) + a[n].shape[-1:])).reshape(a[n].shape)
             for n, t in (("ffn_w_in", "wi"), ("ffn_w_out", "wo"))}
    ng = a["gdn_w_in"].shape[2]
    grads["gdn_w_in"] = reduce("wg", jnp.stack([gw["gdn_w_in"][:, :, s * ng:(s + 1) * ng] for s in range(4)]))
    gsh = reduce("wr", _pack_flat(jnp.concatenate([_shards_first(gw[n], ax).reshape(4, -1) for n, ax in REST], axis=1)))
    gsh = gsh.reshape(-1)
    off = 0
    for n, _ in REST:
        grads[n] = gsh[off:off + a[n].size].reshape(a[n].shape)
        off += a[n].size

    small = _all_gather8("ag_small", _pack_small([dmods, dkvmod] + [gp[n] for n in SMALL]))
    shapes = [(4, NMOD * D), (2 * D,)] + [gp[n].shape for n in SMALL]
    dmod_all, dkv_all = _unpack_small(small, shapes)[:2]
    tot = _unpack_small(_sum8("sum_small", small), shapes)
    grads["ada_b"], grads["kv_ada_b"] = tot[0], tot[1]
    for n, t in zip(SMALL, tot[2:]):
        grads[n] = t
    grads["norm_g"] = lax.dynamic_slice_in_dim(grads["norm_g"], chip * a["norm_g"].shape[2], a["norm_g"].shape[2], 2)
    grads["gdn_conv_w"] = lax.dynamic_slice_in_dim(grads["gdn_conv_w"], chip * a["gdn_conv_w"].shape[2],
                                                   a["gdn_conv_w"].shape[2], 2)
    ca = jnp.pad(c_act, ((0, LANE - 8), (0, 0)))
    dm = jnp.pad(lax.dynamic_slice_in_dim(dmod_all.reshape(8, 4, NMOD * D), chip * nada, nada, 2), ((0, LANE - 8), (0, 0), (0, 0)))
    gada = None
    for l in range(4):
        gada = _matmul(f"gada{l}", ca, dm[:, l], "tn", precise=True, lay="o_stack", li=l, into=gada, nmat=4)
    grads["ada_w"] = gada
    dk = jnp.pad(lax.dynamic_slice_in_dim(dkv_all, chip * nkv, nkv, 1), ((0, LANE - 8), (0, 0)))
    grads["kv_ada_w"] = _matmul("gadakv", ca, dk, "tn", precise=True)

    upd = [_adamw("adamw_" + n, a[n], grads[n], a["m_" + n], a["v_" + n]) for n in WEIGHTS]
    return (loss, dx[None], *[grads[n] for n in WEIGHTS], *[u[0] for u in upd], *[u[1] for u in upd], *[u[2] for u in upd])
```
